```python
import math
import jax
import jax.numpy as jnp
from jax import lax
import numpy as np

D_MODEL = 1024
BATCH = 2
SEQ = 8192
DEPTH = 1

HEAD_DIM = 64
N_ATTN_HEADS = 8
N_KV_HEADS = 2
GQA_SIZE = N_ATTN_HEADS // N_KV_HEADS
D_ATTN = N_ATTN_HEADS * HEAD_DIM
D_MIX = D_MODEL
D_RNN = D_MIX - D_ATTN
N_RNN_BLOCKS = 8
RNN_BLOCK = D_RNN // N_RNN_BLOCKS
CONV_WIDTH = 4
LRU_C = 8.0
CMP_LEN = 32
CMP_STRIDE = 16
SLC_LEN = 64
SLC_TOP = 16
WINDOW = 512
Q_BLOCK = 128
N_BRANCH = 3
REL_BUCKETS = 32
REL_MAX_DIST = 1024
N_EXPERTS = 256
TOP_K = 8
N_EXPERT_GROUPS = 8
TOP_GROUPS = 4
D_EXPERT = 256
ROUTED_SCALE = 2.5
MOE_BLOCK = 128
EPS = 1e-6
KV_W = N_KV_HEADS * HEAD_DIM
N_GATE = N_BRANCH * N_ATTN_HEADS
SPLITS = (D_ATTN, D_ATTN + 6 * KV_W, D_ATTN + 6 * KV_W + N_GATE, D_ATTN + 6 * KV_W + N_GATE + D_RNN)
N_IN_COLS = SPLITS[-1] + D_RNN

kernel_name = 'hybrid_nsa_rglru_moe_block'


def rmsnorm(x, w):
    xf = x.astype(jnp.float32)
    y = xf * lax.rsqrt(jnp.mean(xf * xf, axis=-1, keepdims=True) + EPS)
    return (y * w.astype(jnp.float32)).astype(x.dtype)


def t5_bucket(dist):
    n = jnp.maximum(dist, 0)
    max_exact = REL_BUCKETS // 2
    nf = jnp.maximum(n, 1).astype(jnp.float32)
    large = max_exact + (jnp.log(nf / max_exact) / math.log(REL_MAX_DIST / max_exact)
                         * (REL_BUCKETS - max_exact)).astype(jnp.int32)
    large = jnp.minimum(large, REL_BUCKETS - 1)
    return jnp.where(n < max_exact, n, large)


def masked_softmax(logits, mask):
    logits = jnp.where(mask, logits, -jnp.inf)
    m = jnp.max(logits, axis=-1, keepdims=True)
    m = jnp.where(jnp.isfinite(m), m, 0.0)
    p = jnp.where(mask, jnp.exp(logits - m), 0.0)
    return p / jnp.maximum(jnp.sum(p, axis=-1, keepdims=True), 1e-30)


def nsa_mixer(q, k_c, v_c, k_s, v_s, k_w, v_w, gates, cmp_pos, cmp_k_w1, cmp_k_w2, cmp_v_w1, cmp_v_w2, rel_table):
    f32 = jnp.float32
    B, S = q.shape[0], q.shape[1]
    G, Q, Dh = N_KV_HEADS, GQA_SIZE, HEAD_DIM
    q = q.astype(f32) * (Dh ** -0.5)
    k_c, v_c, k_s, v_s, k_w, v_w = (t.astype(f32) for t in (k_c, v_c, k_s, v_s, k_w, v_w))

    n_cmp = (S - CMP_LEN) // CMP_STRIDE + 1
    cmp_idx = CMP_STRIDE * jnp.arange(n_cmp)[:, None] + jnp.arange(CMP_LEN)[None, :]
    cmp_end = cmp_idx[:, -1]
    pos = cmp_pos.astype(f32)[None, None, :, None, :]

    def compress(t, w1, w2):
        blk = t[:, cmp_idx] + pos
        blk = jnp.swapaxes(blk, 2, 3).reshape(B, n_cmp, G, CMP_LEN * Dh)
        return jax.nn.gelu(blk @ w1.astype(f32)) @ w2.astype(f32)

    kc = compress(k_c, cmp_k_w1, cmp_k_w2)
    vc = compress(v_c, cmp_v_w1, cmp_v_w2)

    n_slc = S // SLC_LEN
    n_top = min(SLC_TOP, n_slc)
    c_start = CMP_STRIDE * jnp.arange(n_cmp)[:, None]
    s_start = SLC_LEN * jnp.arange(n_slc)[None, :]
    overlap = ((c_start < s_start + SLC_LEN) & (c_start + CMP_LEN > s_start)).astype(f32)
    ks_blk = jnp.moveaxis(k_s.reshape(B, n_slc, SLC_LEN, G, Dh), 3, 1)
    vs_blk = jnp.moveaxis(v_s.reshape(B, n_slc, SLC_LEN, G, Dh), 3, 1)
    kw_pad = jnp.pad(k_w, ((0, 0), (WINDOW, 0), (0, 0), (0, 0)))
    vw_pad = jnp.pad(v_w, ((0, 0), (WINDOW, 0), (0, 0), (0, 0)))
    bi = jnp.arange(B)[:, None, None, None]
    gi = jnp.arange(G)[None, :, None, None]
    tbl_g = rel_table.astype(f32).reshape(REL_BUCKETS, G, Q)
    slc_off = jnp.arange(SLC_LEN)
    blk_ids = jnp.arange(n_slc)

    def grid_bias(dist):
        return jnp.moveaxis(tbl_g[t5_bucket(dist)], (2, 3), (0, 1))

    def block(i):
        q0 = i * Q_BLOCK
        tpos = q0 + jnp.arange(Q_BLOCK)
        qb = lax.dynamic_slice_in_dim(q, q0, Q_BLOCK, axis=1).reshape(B, Q_BLOCK, G, Q, Dh)
        gb = lax.dynamic_slice_in_dim(gates, q0, Q_BLOCK, axis=1).astype(f32).reshape(B, Q_BLOCK, G, Q, N_BRANCH)

        dist_c = tpos[:, None] - cmp_end[None, :]
        p_c = masked_softmax(jnp.einsum('btgqd,bngd->bgqtn', qb, kc) + grid_bias(dist_c), dist_c >= 0)
        o_c = jnp.einsum('bgqtn,bngd->btgqd', p_c, vc)

        score = jnp.einsum('bgqtn,nj->bgtj', p_c, overlap)
        cur = (tpos // SLC_LEN)[:, None]
        forced = (blk_ids == 0) | (blk_ids == cur) | (blk_ids == cur - 1)
        score = jnp.where(forced, jnp.inf, jnp.where(blk_ids <= cur, score, -jnp.inf))
        sel = lax.top_k(score, n_top)[1]
        ksel = ks_blk[bi, gi, sel].reshape(B, G, Q_BLOCK, n_top * SLC_LEN, Dh)
        vsel = vs_blk[bi, gi, sel].reshape(B, G, Q_BLOCK, n_top * SLC_LEN, Dh)
        kpos = (sel[..., None] * SLC_LEN + slc_off).reshape(B, G, Q_BLOCK, n_top * SLC_LEN)
        dist_s = tpos[:, None] - kpos
        bias_s = jnp.moveaxis(tbl_g[t5_bucket(dist_s), gi], -1, 2)
        p_s = masked_softmax(jnp.einsum('btgqd,bgtkd->bgqtk', qb, ksel) + bias_s, (dist_s >= 0)[:, :, None])
        o_s = jnp.einsum('bgqtk,bgtkd->btgqd', p_s, vsel)

        kwb = lax.dynamic_slice_in_dim(kw_pad, q0, WINDOW + Q_BLOCK, axis=1)
        vwb = lax.dynamic_slice_in_dim(vw_pad, q0, WINDOW + Q_BLOCK, axis=1)
        kpos_w = q0 - WINDOW + jnp.arange(WINDOW + Q_BLOCK)
        dist_w = tpos[:, None] - kpos_w[None, :]
        mask_w = (dist_w >= 0) & (dist_w < WINDOW) & (kpos_w >= 0)[None, :]
        p_w = masked_softmax(jnp.einsum('btgqd,bkgd->bgqtk', qb, kwb) + grid_bias(dist_w), mask_w)
        o_w = jnp.einsum('bgqtk,bkgd->btgqd', p_w, vwb)

        o = gb[..., 0:1] * o_c + gb[..., 1:2] * o_s + gb[..., 2:3] * o_w
        return o.reshape(B, Q_BLOCK, G * Q * Dh)

    out = lax.map(block, jnp.arange(S // Q_BLOCK))
    return jnp.swapaxes(out, 0, 1).reshape(B, S, D_ATTN)


def _linear_recurrence(left, right):
    a_l, b_l = left
    a_r, b_r = right
    return a_l * a_r, a_r * b_l + b_r


def rglru_mixer(xr, gr, conv_w, conv_b, wa, ba, wx, bx, lam):
    f32 = jnp.float32
    B, S, C = xr.shape
    xc = lax.conv_general_dilated(xr.astype(f32), conv_w.astype(f32), window_strides=(1,),
                                  padding=[(CONV_WIDTH - 1, 0)], dimension_numbers=('NWC', 'WIO', 'NWC'),
                                  feature_group_count=C) + conv_b.astype(f32)
    xb = xc.reshape(B, S, N_RNN_BLOCKS, RNN_BLOCK)
    r = jax.nn.sigmoid(jnp.einsum('bshi,hij->bshj', xb, wa.astype(f32)).reshape(B, S, C) + ba.astype(f32))
    i = jax.nn.sigmoid(jnp.einsum('bshi,hij->bshj', xb, wx.astype(f32)).reshape(B, S, C) + bx.astype(f32))
    log_a = -LRU_C * r * jax.nn.softplus(-lam.astype(f32))
    a = jnp.exp(log_a)
    u = jnp.sqrt(-jnp.expm1(2.0 * log_a)) * (i * xc)
    _, hs = lax.associative_scan(_linear_recurrence, (a, u), axis=1)
    return hs * jax.nn.gelu(gr.astype(f32))


def hybrid_mixer(h, w_in, cmp_pos, cmp_k_w1, cmp_k_w2, cmp_v_w1, cmp_v_w2, rel_table, conv_w, conv_b,
                 lru_wa, lru_ba, lru_wx, lru_bx, lru_lambda, gnorm_attn_w, gnorm_rnn_w, w_out):
    B, S, _ = h.shape
    proj = h @ w_in
    q, kv, g, xr, gr = jnp.split(proj, SPLITS, axis=-1)
    q = q.reshape(B, S, N_ATTN_HEADS, HEAD_DIM)
    kv = kv.reshape(B, S, 6, N_KV_HEADS, HEAD_DIM)
    k_c, v_c, k_s, v_s, k_w, v_w = (kv[:, :, j] for j in range(6))
    gates = jax.nn.sigmoid(g.astype(jnp.float32)).reshape(B, S, N_ATTN_HEADS, N_BRANCH)
    o_attn = nsa_mixer(q, k_c, v_c, k_s, v_s, k_w, v_w, gates, cmp_pos, cmp_k_w1, cmp_k_w2,
                       cmp_v_w1, cmp_v_w2, rel_table)
    o_rnn = rglru_mixer(xr, gr, conv_w, conv_b, lru_wa, lru_ba, lru_wx, lru_bx, lru_lambda)
    y = jnp.concatenate([rmsnorm(o_attn, gnorm_attn_w), rmsnorm(o_rnn, gnorm_rnn_w)], axis=-1)
    return y.astype(h.dtype) @ w_out


def moe_ffn(h, w_router, b_router, w1, w3, w2, ws1, ws3, ws2):
    f32 = jnp.float32
    B, S, D = h.shape
    N = B * S
    xt = h.reshape(N, D)
    s = jax.nn.sigmoid((xt @ w_router).astype(f32))
    s_sel = s + b_router.astype(f32)
    per_group = N_EXPERTS // N_EXPERT_GROUPS
    grp_score = lax.top_k(s_sel.reshape(N, N_EXPERT_GROUPS, per_group), 2)[0].sum(-1)
    top_grp = lax.top_k(grp_score, TOP_GROUPS)[1]
    grp_ok = jnp.any(top_grp[:, :, None] == jnp.arange(N_EXPERT_GROUPS), axis=1)
    s_sel = jnp.where(jnp.repeat(grp_ok, per_group, axis=1), s_sel, -jnp.inf)
    eid = lax.top_k(s_sel, TOP_K)[1]
    s_top = jnp.take_along_axis(s, eid, axis=1)
    wgt = ROUTED_SCALE * s_top / jnp.sum(s_top, axis=-1, keepdims=True)

    NK = N * TOP_K
    e_flat = eid.reshape(NK)
    order = jnp.argsort(e_flat)
    e_sorted = e_flat[order]
    counts = jnp.bincount(e_flat, length=N_EXPERTS)
    padded = (counts + MOE_BLOCK - 1) // MOE_BLOCK * MOE_BLOCK
    pend = jnp.cumsum(padded)
    dest = (pend - padded)[e_sorted] + jnp.arange(NK) - (jnp.cumsum(counts) - counts)[e_sorted]
    n_blk = -(-(NK + N_EXPERTS * (MOE_BLOCK - 1)) // MOE_BLOCK)
    P = n_blk * MOE_BLOCK
    slot_tok = jnp.zeros((P,), jnp.int32).at[dest].set((order // TOP_K).astype(jnp.int32))
    slot_w = jnp.zeros((P,), f32).at[dest].set(wgt.reshape(NK)[order])
    blk_e = jnp.minimum(jnp.searchsorted(pend, jnp.arange(n_blk) * MOE_BLOCK, side='right'), N_EXPERTS - 1)

    def expert_block(args):
        tok, e, w = args
        xb = xt[tok]
        hid = jax.nn.silu(xb @ w1[e]) * (xb @ w3[e])
        return (hid @ w2[e]) * w[:, None]

    out = lax.map(expert_block, (slot_tok.reshape(n_blk, MOE_BLOCK), blk_e, slot_w.reshape(n_blk, MOE_BLOCK)))
    routed = jax.ops.segment_sum(out.reshape(P, D), slot_tok, num_segments=N)
    shared = (jax.nn.silu(xt @ ws1) * (xt @ ws3)) @ ws2
    return (routed + shared).astype(h.dtype).reshape(B, S, D)


def setup_inputs(seed: int = 0) -> dict:
    key = jax.random.key(seed)
    ks = jax.random.split(key, 32)
    f32 = jnp.float32
    L = DEPTH

    def nrm(k, shape, scale):
        return jax.random.normal(k, shape, f32) * scale

    def gain(k, shape):
        return 1.0 + 0.05 * jax.random.normal(k, shape, f32)

    a_c = jax.random.uniform(ks[20], (L, D_RNN), f32, minval=0.9, maxval=0.999)
    a = a_c ** (1.0 / LRU_C)
    return {
        'x': nrm(ks[0], (BATCH, SEQ, D_MODEL), 1.0),
        'c': nrm(ks[1], (BATCH, D_MODEL), 1.0),
        'w_ada': nrm(ks[2], (L, D_MODEL, 6 * D_MODEL), 0.5 * D_MODEL ** -0.5),
        'b_ada': nrm(ks[3], (L, 6 * D_MODEL), 0.02),
        'norm1_w': gain(ks[4], (L, D_MODEL)),
        'w_in': nrm(ks[5], (L, D_MODEL, N_IN_COLS), D_MODEL ** -0.5),
        'cmp_pos': nrm(ks[6], (L, CMP_LEN, HEAD_DIM), 0.5),
        'cmp_k_w1': nrm(ks[7], (L, CMP_LEN * HEAD_DIM, HEAD_DIM), (CMP_LEN * HEAD_DIM) ** -0.5),
        'cmp_k_w2': nrm(ks[8], (L, HEAD_DIM, HEAD_DIM), HEAD_DIM ** -0.5),
        'cmp_v_w1': nrm(ks[9], (L, CMP_LEN * HEAD_DIM, HEAD_DIM), (CMP_LEN * HEAD_DIM) ** -0.5),
        'cmp_v_w2': nrm(ks[10], (L, HEAD_DIM, HEAD_DIM), HEAD_DIM ** -0.5),
        'rel_table': nrm(ks[11], (REL_BUCKETS, N_ATTN_HEADS), 0.5),
        'conv_w': nrm(ks[12], (L, CONV_WIDTH, 1, D_RNN), CONV_WIDTH ** -0.5),
        'conv_b': nrm(ks[13], (L, D_RNN), 0.02),
        'lru_wa': nrm(ks[14], (L, N_RNN_BLOCKS, RNN_BLOCK, RNN_BLOCK), RNN_BLOCK ** -0.5),
        'lru_ba': nrm(ks[15], (L, D_RNN), 0.02),
        'lru_wx': nrm(ks[16], (L, N_RNN_BLOCKS, RNN_BLOCK, RNN_BLOCK), RNN_BLOCK ** -0.5),
        'lru_bx': nrm(ks[17], (L, D_RNN), 0.02),
        'lru_lambda': jnp.log(a) - jnp.log1p(-a),
        'gnorm_attn_w': gain(ks[18], (L, D_ATTN)),
        'gnorm_rnn_w': gain(ks[19], (L, D_RNN)),
        'w_out': nrm(ks[21], (L, D_MIX, D_MODEL), D_MIX ** -0.5),
        'norm2_w': gain(ks[22], (L, D_MODEL)),
        'w_router': nrm(ks[23], (L, D_MODEL, N_EXPERTS), D_MODEL ** -0.5),
        'b_router': nrm(ks[24], (L, N_EXPERTS), 0.01),
        'w1': nrm(ks[25], (L, N_EXPERTS, D_MODEL, D_EXPERT), D_MODEL ** -0.5),
        'w3': nrm(ks[26], (L, N_EXPERTS, D_MODEL, D_EXPERT), D_MODEL ** -0.5),
        'w2': nrm(ks[27], (L, N_EXPERTS, D_EXPERT, D_MODEL), D_EXPERT ** -0.5),
        'ws1': nrm(ks[28], (L, D_MODEL, D_EXPERT), D_MODEL ** -0.5),
        'ws3': nrm(ks[29], (L, D_MODEL, D_EXPERT), D_MODEL ** -0.5),
        'ws2': nrm(ks[30], (L, D_EXPERT, D_MODEL), D_EXPERT ** -0.5),
        'final_norm_w': gain(ks[31], (D_MODEL,)),
    }


def reference(x, c, w_ada, b_ada, norm1_w, w_in, cmp_pos, cmp_k_w1, cmp_k_w2, cmp_v_w1, cmp_v_w2, rel_table,
              conv_w, conv_b, lru_wa, lru_ba, lru_wx, lru_bx, lru_lambda, gnorm_attn_w, gnorm_rnn_w, w_out,
              norm2_w, w_router, b_router, w1, w3, w2, ws1, ws3, ws2, final_norm_w):
    for l in range(DEPTH):
        mod = (jax.nn.silu(c) @ w_ada[l] + b_ada[l])[:, None, :]
        sh1, sc1, g1, sh2, sc2, g2 = jnp.split(mod, 6, axis=-1)
        h = rmsnorm(x, norm1_w[l]) * (1.0 + sc1) + sh1
        x = x + g1 * hybrid_mixer(h, w_in[l], cmp_pos[l], cmp_k_w1[l], cmp_k_w2[l], cmp_v_w1[l], cmp_v_w2[l],
                                  rel_table, conv_w[l], conv_b[l], lru_wa[l], lru_ba[l], lru_wx[l], lru_bx[l],
                                  lru_lambda[l], gnorm_attn_w[l], gnorm_rnn_w[l], w_out[l])
        h = rmsnorm(x, norm2_w[l]) * (1.0 + sc2) + sh2
        x = x + g2 * moe_ffn(h, w_router[l], b_router[l], w1[l], w3[l], w2[l], ws1[l], ws3[l], ws2[l])
    return rmsnorm(x, final_norm_w)
```

```python
import functools
import math

import jax
import jax.numpy as jnp
import numpy as np
from jax import lax
from jax.experimental import pallas as pl
from jax.experimental.pallas import tpu as pltpu

F32 = jnp.float32
BF16 = jnp.bfloat16
I32 = jnp.int32

HEAD_DIM = 64
N_HEADS = 8
N_KV = 2
GQA = N_HEADS // N_KV
N_BRANCH = 3
CONV_WIDTH = 4
LRU_C = 8.0
CMP_LEN = 32
CMP_STRIDE = 16
SLC_LEN = 64
SLC_TOP = 16
WINDOW = 512
QB = 128
REL_BUCKETS = 32
REL_MAX_DIST = 1024
N_EXPERT_GROUPS = 8
TOP_GROUPS = 4
TOP_K = 8
ROUTED_SCALE = 2.5
MOE_BLOCK = 128
EPS = 1e-6
NEG = -1e30
CMP_PAD = 56
CMP_WIN = 64
N_BIAS_TILES = 11
FAR_TILE_DIST = 8
FAR_CHUNK = 4
VMEM_LIMIT = 52 * 1024 * 1024


def _dot(a, b, **kw):
    return jnp.dot(a, b, preferred_element_type=F32, **kw)


def _dot_nt(a, b, **kw):
    return lax.dot_general(a, b, (((1,), (1,)), ((), ())), preferred_element_type=F32, **kw)


def _gelu(x):
    return 0.5 * x * (1.0 + jnp.tanh(math.sqrt(2.0 / math.pi) * (x + 0.044715 * (x * x * x))))


def _sigmoid(x):
    return 1.0 / (1.0 + jnp.exp(-x))


def _rms(x):
    return x * lax.rsqrt(jnp.mean(x * x, axis=-1, keepdims=True) + EPS)


def _params(*sem):
    return pltpu.CompilerParams(dimension_semantics=sem, vmem_limit_bytes=VMEM_LIMIT)


def _ada_kernel(c_ref, w_ref, b_ref, o_ref):
    c = c_ref[...]
    a = c * _sigmoid(c)
    o_ref[...] = _dot(a, w_ref[...], precision=lax.Precision.HIGHEST) + b_ref[...]


def _ada(c8, w, b):
    d, n = w.shape
    tn = 1536
    return pl.pallas_call(
        _ada_kernel,
        grid=(n // tn,),
        in_specs=[pl.BlockSpec((8, d), lambda j: (0, 0)),
                  pl.BlockSpec((d, tn), lambda j: (0, j)),
                  pl.BlockSpec((1, tn), lambda j: (0, j))],
        out_specs=pl.BlockSpec((8, tn), lambda j: (0, j)),
        out_shape=jax.ShapeDtypeStruct((8, n), F32),
        compiler_params=_params("parallel"),
        name="ada",
    )(c8, w, b)


def _inproj_kernel(x_ref, mod_ref, nw_ref, w_ref, q_ref, kv_ref, kvc_ref, g_ref, xr_ref, gr_ref):
    h = _rms(x_ref[0]) * nw_ref[...]
    h = h * (1.0 + mod_ref[0, 1:2, :]) + mod_ref[0, 0:1, :]
    p = _dot(h.astype(BF16), w_ref[...])
    dq = N_HEADS * HEAD_DIM
    for hh in range(N_HEADS):
        q_ref[0, hh] = (p[:, hh * HEAD_DIM:(hh + 1) * HEAD_DIM] * (HEAD_DIM ** -0.5)).astype(BF16)
    for j in range(6):
        for g in range(N_KV):
            col = dq + (j * N_KV + g) * HEAD_DIM
            piece = p[:, col:col + HEAD_DIM]
            if j < 2:
                kvc_ref[0, j * N_KV + g] = piece
            else:
                kv_ref[0, (j - 2) * N_KV + g] = piece.astype(BF16)
    c0 = dq + 6 * N_KV * HEAD_DIM
    g_ref[0] = _sigmoid(p[:, c0:c0 + 256])
    xr_ref[0] = p[:, c0 + 256:c0 + 768]
    gr_ref[0] = p[:, c0 + 768:c0 + 1280]


def _inproj(x, mod, nw, w_pad, tm=256):
    B, S, D = x.shape
    ncol = w_pad.shape[1]
    return pl.pallas_call(
        _inproj_kernel,
        grid=(B, S // tm),
        in_specs=[pl.BlockSpec((1, tm, D), lambda b, t: (b, t, 0)),
                  pl.BlockSpec((1, 6, D), lambda b, t: (b, 0, 0)),
                  pl.BlockSpec((1, D), lambda b, t: (0, 0)),
                  pl.BlockSpec((D, ncol), lambda b, t: (0, 0))],
        out_specs=[pl.BlockSpec((1, N_HEADS, tm, HEAD_DIM), lambda b, t: (b, 0, t, 0)),
                   pl.BlockSpec((1, 8, tm, HEAD_DIM), lambda b, t: (b, 0, t, 0)),
                   pl.BlockSpec((1, 4, tm, HEAD_DIM), lambda b, t: (b, 0, t, 0)),
                   pl.BlockSpec((1, tm, 256), lambda b, t: (b, t, 0)),
                   pl.BlockSpec((1, tm, 512), lambda b, t: (b, t, 0)),
                   pl.BlockSpec((1, tm, 512), lambda b, t: (b, t, 0))],
        out_shape=[jax.ShapeDtypeStruct((B, N_HEADS, S, HEAD_DIM), BF16),
                   jax.ShapeDtypeStruct((B, 8, S, HEAD_DIM), BF16),
                   jax.ShapeDtypeStruct((B, 4, S, HEAD_DIM), F32),
                   jax.ShapeDtypeStruct((B, S, 256), F32),
                   jax.ShapeDtypeStruct((B, S, 512), F32),
                   jax.ShapeDtypeStruct((B, S, 512), F32)],
        compiler_params=_params("parallel", "parallel"),
        name="inproj",
    )(x, mod, nw, w_pad)


def _compress_kernel(x_ref, pos_ref, w1_ref, w2_ref, o_ref):
    x = x_ref[0, 0]
    nc = x.shape[0]
    half = CMP_STRIDE * HEAD_DIM
    kv = pl.program_id(1) // N_KV
    a = _dot((x + pos_ref[0:1, :]).astype(BF16), w1_ref[kv, 0:half, :].astype(BF16))
    b = _dot((x + pos_ref[1:2, :]).astype(BF16), w1_ref[kv, half:2 * half, :].astype(BF16))
    hid = _gelu(a + pltpu.roll(b, nc - 1, 0))
    out = _dot(hid.astype(BF16), w2_ref[kv].astype(BF16))
    row = lax.broadcasted_iota(I32, out.shape, 0)
    out = jnp.where(row < nc - 1, out, 0.0)
    o_ref[0, 0, 0:CMP_PAD, :] = jnp.zeros((CMP_PAD, HEAD_DIM), F32)
    o_ref[0, 0, CMP_PAD:CMP_PAD + nc, :] = out
    o_ref[0, 0, CMP_PAD + nc:, :] = jnp.zeros((CMP_WIN - CMP_PAD, HEAD_DIM), F32)


def _compress(kvc, pos2, w1, w2):
    B, _, NC, W = kvc.shape
    return pl.pallas_call(
        _compress_kernel,
        grid=(B, 4),
        in_specs=[pl.BlockSpec((1, 1, NC, W), lambda b, i: (b, i, 0, 0)),
                  pl.BlockSpec((2, W), lambda b, i: (0, 0)),
                  pl.BlockSpec((2, 2 * W, HEAD_DIM), lambda b, i: (0, 0, 0)),
                  pl.BlockSpec((2, HEAD_DIM, HEAD_DIM), lambda b, i: (0, 0, 0))],
        out_specs=pl.BlockSpec((1, 1, NC + CMP_WIN, HEAD_DIM), lambda b, i: (b, i, 0, 0)),
        out_shape=jax.ShapeDtypeStruct((B, 4, NC + CMP_WIN, HEAD_DIM), F32),
        compiler_params=_params("parallel", "parallel"),
        name="compress",
    )(kvc, pos2, w1, w2)


def _attn_kernel(c31_ref, q_ref, ks_ref, vs_ref, kw_ref, vw_ref, kc_ref, vc_ref, g_ref, t_ref, w_ref, ov_ref,
                 o_ref, m_scr, l_scr, acc_scr):
    g = pl.program_id(1)
    qb = pl.program_id(2)
    R = GQA * QB
    q = q_ref[0].reshape(R, HEAD_DIM)
    nc = kc_ref.shape[2] - CMP_WIN
    nb = ov_ref.shape[1]
    n_top = min(SLC_TOP, ks_ref.shape[2] // SLC_LEN)

    c31 = jnp.concatenate([jnp.full((QB, 1), c31_ref[GQA * g + hq], F32) for hq in range(GQA)], axis=0)

    w0 = pl.multiple_of(qb * 8, 8)
    kcf = kc_ref[0, 0, CMP_PAD:CMP_PAD + nc, :].astype(BF16)
    vcf = vc_ref[0, 0, CMP_PAD:CMP_PAD + nc, :].astype(BF16)
    kcw = kc_ref[0, 0, pl.ds(w0, CMP_WIN), :].astype(BF16)
    vcw = vc_ref[0, 0, pl.ds(w0, CMP_WIN), :].astype(BF16)
    s_far = _dot_nt(q, kcf) + c31
    n_io = lax.broadcasted_iota(I32, (R, nc), 1)
    s_far = jnp.where(n_io < qb * 8 - CMP_PAD, s_far, -jnp.inf)
    s_win = _dot_nt(q, kcw) + w_ref[0]
    i_w = lax.broadcasted_iota(I32, (R, CMP_WIN), 0) & (QB - 1)
    j_w = lax.broadcasted_iota(I32, (R, CMP_WIN), 1)
    dist_w = i_w - CMP_STRIDE * (j_w - CMP_PAD) - (CMP_LEN - 1)
    s_win = jnp.where((dist_w >= 0) & (j_w >= CMP_PAD - qb * 8), s_win, -jnp.inf)
    m = jnp.maximum(jnp.max(s_far, axis=-1, keepdims=True), jnp.max(s_win, axis=-1, keepdims=True))
    m = jnp.where(m == -jnp.inf, 0.0, m)
    e_far = jnp.exp(s_far - m)
    e_win = jnp.exp(s_win - m)
    l = jnp.sum(e_far, axis=-1, keepdims=True) + jnp.sum(e_win, axis=-1, keepdims=True)
    inv = 1.0 / jnp.maximum(l, 1e-30)
    p_far = e_far * inv
    p_win = e_win * inv
    o_c = _dot(p_far.astype(BF16), vcf) + _dot(p_win.astype(BF16), vcw)
    pg_far = jnp.sum(p_far.reshape(GQA, QB, nc), axis=0)
    pg_win = jnp.sum(p_win.reshape(GQA, QB, CMP_WIN), axis=0)
    score = (_dot(pg_far.astype(BF16), ov_ref[CMP_PAD:CMP_PAD + nc, :].astype(BF16))
             + _dot(pg_win.astype(BF16), ov_ref[pl.ds(w0, CMP_WIN), :].astype(BF16)))

    sc_t = score.T
    blk = lax.broadcasted_iota(I32, (nb, QB), 0)
    t_io = lax.broadcasted_iota(I32, (nb, QB), 1)
    cur = qb * (QB // SLC_LEN) + t_io // SLC_LEN
    forced = (blk == 0) | (blk == cur) | (blk == cur - 1)
    val = jnp.where(forced, jnp.inf, jnp.where(blk <= cur, sc_t, -jnp.inf))
    sel_t = jnp.zeros((nb, QB), F32)
    for _ in range(n_top):
        mx = jnp.max(val, axis=0, keepdims=True)
        hit = (val == mx) & (val > -jnp.inf)
        first = jnp.min(jnp.where(hit, blk, nb), axis=0, keepdims=True)
        pick = blk == first
        sel_t = jnp.where(pick, 1.0, sel_t)
        val = jnp.where(pick, -jnp.inf, val)
    sel = sel_t.T.astype(BF16)

    def reset():
        m_scr[...] = jnp.full(m_scr.shape, NEG, F32)
        l_scr[...] = jnp.zeros(l_scr.shape, F32)
        acc_scr[...] = jnp.zeros(acc_scr.shape, F32)

    def update(s, mask, v):
        kc = s.shape[1]
        s = jnp.where(mask[None], s.reshape(GQA, QB, kc), NEG).reshape(R, kc)
        m_old = m_scr[...]
        m_new = jnp.maximum(m_old, jnp.max(s, axis=-1, keepdims=True))
        alpha = jnp.exp(m_old - m_new)
        p = jnp.exp(s - m_new)
        l_scr[...] = alpha * l_scr[...] + jnp.sum(p, axis=-1, keepdims=True)
        acc_scr[...] = alpha * acc_scr[...] + _dot(p.astype(BF16), v)
        m_scr[...] = m_new

    def sel_mask(first_blk, kc):
        bj = lax.broadcasted_iota(I32, (nb, kc), 0)
        kj = lax.broadcasted_iota(I32, (nb, kc), 1)
        expand = jnp.where(bj == first_blk + kj // SLC_LEN, 1.0, 0.0).astype(BF16)
        return _dot(sel, expand) > 0.5

    i_t = lax.broadcasted_iota(I32, (QB, QB), 0)
    j_t = lax.broadcasted_iota(I32, (QB, QB), 1)

    reset()
    n_far = jnp.maximum(qb - (FAR_TILE_DIST - 1), 0) // FAR_CHUNK
    far_w = FAR_CHUNK * QB

    def far_body(c, carry):
        k0 = pl.multiple_of(c * far_w, far_w)
        s = _dot_nt(q, ks_ref[0, 0, pl.ds(k0, far_w), :]) + c31
        update(s, sel_mask(c * (far_w // SLC_LEN), far_w), vs_ref[0, 0, pl.ds(k0, far_w), :])
        return carry

    lax.fori_loop(0, n_far, far_body, 0)

    def near_body(kb, carry):
        k0 = pl.multiple_of(kb * QB, QB)
        delta = qb - kb
        s = _dot_nt(q, ks_ref[0, 0, pl.ds(k0, QB), :]) + t_ref[0, delta]
        mask = sel_mask(kb * (QB // SLC_LEN), QB) & (delta * QB + i_t - j_t >= 0)
        update(s, mask, vs_ref[0, 0, pl.ds(k0, QB), :])
        return carry

    lax.fori_loop(n_far * FAR_CHUNK, qb + 1, near_body, 0)
    o_s = acc_scr[...] / l_scr[...]

    reset()

    def win_body(kb, carry):
        k0 = pl.multiple_of(kb * QB, QB)
        delta = qb - kb
        s = _dot_nt(q, kw_ref[0, 0, pl.ds(k0, QB), :]) + t_ref[0, delta]
        dist = delta * QB + i_t - j_t
        update(s, (dist >= 0) & (dist < WINDOW), vw_ref[0, 0, pl.ds(k0, QB), :])
        return carry

    lax.fori_loop(jnp.maximum(qb - WINDOW // QB, 0), qb + 1, win_body, 0)
    o_w = acc_scr[...] / l_scr[...]

    gates = g_ref[0]
    for hq in range(GQA):
        rows = slice(hq * QB, (hq + 1) * QB)
        c = hq * N_BRANCH
        o_ref[0, :, hq * HEAD_DIM:(hq + 1) * HEAD_DIM] = (
            gates[:, c:c + 1] * o_c[rows] + gates[:, c + 1:c + 2] * o_s[rows] + gates[:, c + 2:c + 3] * o_w[rows])


def _attn(c31, q, kv, kvc, gates, t_tab, w_tab, ov):
    B, H, S, Dh = q.shape
    ncp = kvc.shape[2]
    nb = ov.shape[1]

    def kv_spec(j):
        return pl.BlockSpec((1, 1, S, Dh), lambda b, g, t, c: (b, j * N_KV + g, 0, 0))

    def kvc_spec(j):
        return pl.BlockSpec((1, 1, ncp, Dh), lambda b, g, t, c: (b, j * N_KV + g, 0, 0))

    grid_spec = pltpu.PrefetchScalarGridSpec(
        num_scalar_prefetch=1,
        grid=(B, N_KV, S // QB),
        in_specs=[pl.BlockSpec((1, GQA, QB, Dh), lambda b, g, t, c: (b, g, t, 0)),
                  kv_spec(0), kv_spec(1), kv_spec(2), kv_spec(3), kvc_spec(0), kvc_spec(1),
                  pl.BlockSpec((1, QB, 128), lambda b, g, t, c: (b, t, g)),
                  pl.BlockSpec((1, N_BIAS_TILES, GQA * QB, QB), lambda b, g, t, c: (g, 0, 0, 0)),
                  pl.BlockSpec((1, GQA * QB, CMP_WIN), lambda b, g, t, c: (g, 0, 0)),
                  pl.BlockSpec((ncp, nb), lambda b, g, t, c: (0, 0))],
        out_specs=pl.BlockSpec((1, QB, GQA * Dh), lambda b, g, t, c: (b, t, g)),
        scratch_shapes=[pltpu.VMEM((GQA * QB, 1), F32), pltpu.VMEM((GQA * QB, 1), F32),
                        pltpu.VMEM((GQA * QB, Dh), F32)],
    )
    return pl.pallas_call(
        _attn_kernel,
        grid_spec=grid_spec,
        out_shape=jax.ShapeDtypeStruct((B, S, H * Dh), F32),
        compiler_params=_params("parallel", "parallel", "arbitrary"),
        name="attn",
    )(c31, q, kv, kv, kv, kv, kvc, kvc, gates, t_tab, w_tab, ov)


def _rglru_kernel(xr_ref, gr_ref, cw_ref, cb_ref, wa_ref, ba_ref, wx_ref, bx_ref, lam_ref, gw_ref, o_ref,
                  xbuf, hprev, a_scr, u_scr, h_scr):
    ts = xr_ref.shape[1]
    C = xr_ref.shape[2]

    @pl.when(pl.program_id(1) == 0)
    def _():
        xbuf[0:8, :] = jnp.zeros((8, C), F32)
        hprev[...] = jnp.zeros(hprev.shape, F32)

    xbuf[8:8 + ts, :] = xr_ref[0]
    xc = cb_ref[...] + jnp.zeros((ts, C), F32)
    for j in range(CONV_WIDTH):
        xc = xc + cw_ref[j:j + 1, :] * xbuf[pl.ds(8 - (CONV_WIDTH - 1) + j, ts), :]
    xbuf[0:8, :] = xbuf[ts:ts + 8, :]

    xcb = xc.astype(BF16)
    r = _sigmoid(_dot(xcb, wa_ref[...]) + ba_ref[...])
    i = _sigmoid(_dot(xcb, wx_ref[...]) + bx_ref[...])
    z = -lam_ref[...]
    softplus = jnp.maximum(z, 0.0) + jnp.log(1.0 + jnp.exp(-jnp.abs(z)))
    log_a = -LRU_C * r * softplus
    a_scr[...] = jnp.exp(log_a)
    u_scr[...] = jnp.sqrt(1.0 - jnp.exp(2.0 * log_a)) * (i * xc)

    row = lax.broadcasted_iota(I32, (8, C), 0)

    def body(k, h):
        r0 = pl.multiple_of(k * 8, 8)
        a = a_scr[pl.ds(r0, 8), :]
        b = u_scr[pl.ds(r0, 8), :]
        for s in (1, 2, 4):
            keep = row >= s
            b = jnp.where(keep, a * pltpu.roll(b, s, 0) + b, b)
            a = jnp.where(keep, a * pltpu.roll(a, s, 0), a)
        hh = a * h + b
        h_scr[pl.ds(r0, 8), :] = hh
        return jnp.broadcast_to(hh[7:8, :], (8, C))

    hprev[...] = lax.fori_loop(0, ts // 8, body, hprev[...])
    out = h_scr[...] * _gelu(gr_ref[0])
    o_ref[0] = (_rms(out) * gw_ref[...]).astype(BF16)


def _rglru(xr, gr, cw, cb, wa_bd, ba, wx_bd, bx, lam, gw, ts=512):
    B, S, C = xr.shape
    ts = min(ts, S)
    vec = pl.BlockSpec((1, C), lambda b, t: (0, 0))
    mat = pl.BlockSpec((C, C), lambda b, t: (0, 0))
    seq = pl.BlockSpec((1, ts, C), lambda b, t: (b, t, 0))
    return pl.pallas_call(
        _rglru_kernel,
        grid=(B, S // ts),
        in_specs=[seq, seq, pl.BlockSpec((CONV_WIDTH, C), lambda b, t: (0, 0)), vec, mat, vec, mat, vec, vec, vec],
        out_specs=seq,
        out_shape=jax.ShapeDtypeStruct((B, S, C), BF16),
        scratch_shapes=[pltpu.VMEM((ts + 8, C), F32), pltpu.VMEM((8, C), F32), pltpu.VMEM((ts, C), F32),
                        pltpu.VMEM((ts, C), F32), pltpu.VMEM((ts, C), F32)],
        compiler_params=_params("parallel", "arbitrary"),
        name="rglru",
    )(xr, gr, cw, cb, wa_bd, ba, wx_bd, bx, lam, gw)


def _outproj_kernel(oa_ref, yr_ref, x_ref, mod_ref, gaw_ref, wo_ref, n2_ref, wr_ref, x1_ref, h2_ref, st_ref):
    da = oa_ref.shape[2]
    ya = (_rms(oa_ref[0]) * gaw_ref[...]).astype(BF16)
    mix = _dot(ya, wo_ref[0:da, :]) + _dot(yr_ref[0], wo_ref[da:, :])
    x1 = x_ref[0] + mod_ref[0, 2:3, :] * mix
    x1_ref[0] = x1
    h2 = (_rms(x1) * n2_ref[...]) * (1.0 + mod_ref[0, 4:5, :]) + mod_ref[0, 3:4, :]
    h2_ref[0] = h2
    st_ref[...] = _sigmoid(_dot_nt(wr_ref[...], h2, precision=lax.Precision.HIGHEST))


def _outproj(oa, yr, x, mod, gaw, wo, n2, wr_t, tm=256):
    B, S, D = x.shape
    E = wr_t.shape[0]
    nt = S // tm
    row = lambda w: pl.BlockSpec((1, tm, w), lambda b, t: (b, t, 0))
    return pl.pallas_call(
        _outproj_kernel,
        grid=(B, nt),
        in_specs=[row(oa.shape[2]), row(yr.shape[2]), row(D),
                  pl.BlockSpec((1, 6, D), lambda b, t: (b, 0, 0)),
                  pl.BlockSpec((1, oa.shape[2]), lambda b, t: (0, 0)),
                  pl.BlockSpec(wo.shape, lambda b, t: (0, 0)),
                  pl.BlockSpec((1, D), lambda b, t: (0, 0)),
                  pl.BlockSpec((E, D), lambda b, t: (0, 0))],
        out_specs=[row(D), row(D), pl.BlockSpec((E, tm), lambda b, t: (0, b * nt + t))],
        out_shape=[jax.ShapeDtypeStruct((B, S, D), F32), jax.ShapeDtypeStruct((B, S, D), F32),
                   jax.ShapeDtypeStruct((E, B * S), F32)],
        compiler_params=_params("parallel", "parallel"),
        name="outproj",
    )(oa, yr, x, mod, gaw, wo, n2, wr_t)


def _route_kernel(s_ref, b_ref, eid_ref, wgt_ref, rank_ref, cnt_ref, carry):
    E, tn = s_ref.shape
    per = E // N_EXPERT_GROUPS

    @pl.when(pl.program_id(0) == 0)
    def _():
        carry[...] = jnp.zeros(carry.shape, F32)

    s = s_ref[...]
    s_sel = s + b_ref[...]
    eidx = lax.broadcasted_iota(I32, (E, tn), 0)

    grp = []
    for gi in range(N_EXPERT_GROUPS):
        xg = s_sel[gi * per:(gi + 1) * per, :]
        ig = lax.broadcasted_iota(I32, (per, tn), 0)
        m1 = jnp.max(xg, axis=0, keepdims=True)
        f1 = jnp.min(jnp.where(xg == m1, ig, per), axis=0, keepdims=True)
        m2 = jnp.max(jnp.where(ig == f1, -jnp.inf, xg), axis=0, keepdims=True)
        grp.append(m1 + m2)
    val = []
    for gi in range(N_EXPERT_GROUPS):
        rank = jnp.zeros((1, tn), I32)
        for gj in range(N_EXPERT_GROUPS):
            if gj == gi:
                continue
            ahead = (grp[gj] > grp[gi]) | ((grp[gj] == grp[gi]) & (gj < gi))
            rank = rank + ahead.astype(I32)
        val.append(jnp.where(rank < TOP_GROUPS, s_sel[gi * per:(gi + 1) * per, :], -jnp.inf))
    val = jnp.concatenate(val, axis=0)

    eids, tops = [], []
    hot = jnp.zeros((E, tn), F32)
    for _ in range(TOP_K):
        mx = jnp.max(val, axis=0, keepdims=True)
        first = jnp.min(jnp.where(val == mx, eidx, E), axis=0, keepdims=True)
        pick = eidx == first
        eids.append(first)
        tops.append(jnp.sum(jnp.where(pick, s, 0.0), axis=0, keepdims=True))
        hot = jnp.where(pick, 1.0, hot)
        val = jnp.where(pick, -jnp.inf, val)
    denom = tops[0]
    for t in tops[1:]:
        denom = denom + t

    ti = lax.broadcasted_iota(I32, (tn, tn), 0)
    tj = lax.broadcasted_iota(I32, (tn, tn), 1)
    upper = jnp.where(ti < tj, 1.0, 0.0).astype(BF16)
    before = _dot(hot.astype(BF16), upper) + carry[...]
    for k in range(TOP_K):
        eid_ref[k:k + 1, :] = eids[k]
        wgt_ref[k:k + 1, :] = ROUTED_SCALE * tops[k] / denom
        rank_ref[k:k + 1, :] = jnp.sum(jnp.where(eidx == eids[k], before, 0.0), axis=0, keepdims=True).astype(I32)
    carry[...] = carry[...] + jnp.sum(hot, axis=1, keepdims=True)
    cnt_ref[...] = carry[...]


def _route(s_t, b_col, tn=256):
    E, N = s_t.shape
    tn = min(tn, N)
    out = pl.BlockSpec((TOP_K, tn), lambda t: (0, t))
    return pl.pallas_call(
        _route_kernel,
        grid=(N // tn,),
        in_specs=[pl.BlockSpec((E, tn), lambda t: (0, t)), pl.BlockSpec((E, 1), lambda t: (0, 0))],
        out_specs=[out, out, out, pl.BlockSpec((E, 1), lambda t: (0, 0))],
        out_shape=[jax.ShapeDtypeStruct((TOP_K, N), I32), jax.ShapeDtypeStruct((TOP_K, N), F32),
                   jax.ShapeDtypeStruct((TOP_K, N), I32), jax.ShapeDtypeStruct((E, 1), F32)],
        scratch_shapes=[pltpu.VMEM((E, 1), F32)],
        compiler_params=_params("arbitrary"),
        name="route",
    )(s_t, b_col)


def _experts_kernel(meta_ref, blk_e_ref, cnt_ref, tok_ref, ntok_ref, dst_ref, w1_ref, w3_ref, w2_ref, h_hbm, o_hbm,
                    xbuf, ybuf, gsem, ssem):
    i = pl.program_id(0)
    n_used = meta_ref[0]
    slot = i % 2

    def gather(tref, sl, n):
        def go(r, c):
            pltpu.make_async_copy(h_hbm.at[pl.ds(tref[0, 0, r], 1)], xbuf.at[sl, pl.ds(r, 1)], gsem.at[sl]).start()
            return c
        lax.fori_loop(0, n, go, 0)

    @pl.when(i == 0)
    def _():
        xbuf[...] = jnp.zeros(xbuf.shape, F32)
        plane = o_hbm.shape[0] // TOP_K
        for k in range(TOP_K):
            spare = pltpu.make_async_copy(xbuf.at[1, pl.ds(0, 8)], o_hbm.at[pl.ds(k * plane + plane - 8, 8)],
                                          ssem.at[0])
            spare.start()
            spare.wait()
        gather(tok_ref, 0, cnt_ref[0])

    @pl.when(i < n_used)
    def _():
        n = pl.multiple_of(cnt_ref[i], 8)
        pltpu.make_async_copy(h_hbm.at[pl.ds(0, n)], xbuf.at[slot, pl.ds(0, n)], gsem.at[slot]).wait()

        @pl.when(i + 1 < n_used)
        def _():
            gather(ntok_ref, 1 - slot, cnt_ref[i + 1])

        x = xbuf[slot].astype(BF16)
        h1 = _dot(x, w1_ref[0].astype(BF16))
        h3 = _dot(x, w3_ref[0].astype(BF16))
        hid = (h1 * _sigmoid(h1)) * h3
        y = _dot(hid.astype(BF16), w2_ref[0].astype(BF16))

        @pl.when(i > 0)
        def _():
            n_prev = pl.multiple_of(cnt_ref[i - 1], 8)
            pltpu.make_async_copy(ybuf.at[pl.ds(0, n_prev)], o_hbm.at[pl.ds(0, n_prev)], ssem.at[0]).wait()

        ybuf[...] = y

        def put(r, c):
            pltpu.make_async_copy(ybuf.at[pl.ds(r, 1)], o_hbm.at[pl.ds(dst_ref[0, 0, r], 1)], ssem.at[0]).start()
            return c
        lax.fori_loop(0, n, put, 0)

        @pl.when(i == n_used - 1)
        def _():
            pltpu.make_async_copy(ybuf.at[pl.ds(0, n)], o_hbm.at[pl.ds(0, n)], ssem.at[0]).wait()


def _experts(meta, blk_e, blk_cnt, slot_tok, slot_dst, w1, w3, w2, h2, n_out_rows):
    n_blk = slot_tok.shape[0]
    E, D, De = w1.shape
    smem = lambda f: pl.BlockSpec((1, 1, MOE_BLOCK), f, memory_space=pltpu.SMEM)
    grid_spec = pltpu.PrefetchScalarGridSpec(
        num_scalar_prefetch=3,
        grid=(n_blk,),
        in_specs=[smem(lambda i, m, e, c: (i, 0, 0)),
                  smem(lambda i, m, e, c: (jnp.minimum(i + 1, n_blk - 1), 0, 0)),
                  smem(lambda i, m, e, c: (i, 0, 0)),
                  pl.BlockSpec((1, D, De), lambda i, m, e, c: (e[i], 0, 0)),
                  pl.BlockSpec((1, D, De), lambda i, m, e, c: (e[i], 0, 0)),
                  pl.BlockSpec((1, De, D), lambda i, m, e, c: (e[i], 0, 0)),
                  pl.BlockSpec(memory_space=pl.ANY)],
        out_specs=pl.BlockSpec(memory_space=pl.ANY),
        scratch_shapes=[pltpu.VMEM((2, MOE_BLOCK, D), F32), pltpu.VMEM((MOE_BLOCK, D), F32),
                        pltpu.SemaphoreType.DMA((2,)), pltpu.SemaphoreType.DMA((1,))],
    )
    return pl.pallas_call(
        _experts_kernel,
        grid_spec=grid_spec,
        out_shape=jax.ShapeDtypeStruct((n_out_rows, D), F32),
        compiler_params=_params("arbitrary"),
        name="experts",
    )(meta, blk_e, blk_cnt, slot_tok, slot_tok, slot_dst, w1, w3, w2, h2)


def _combine_kernel(y_ref, w_ref, h2_ref, x1_ref, mod_ref, ws1_ref, ws3_ref, ws2_ref, fw_ref, o_ref):
    routed = w_ref[:, 0:1] * y_ref[0]
    for k in range(1, TOP_K):
        routed = routed + w_ref[:, k:k + 1] * y_ref[k]
    h = h2_ref[...].astype(BF16)
    a = _dot(h, ws1_ref[...])
    hid = (a * _sigmoid(a)) * _dot(h, ws3_ref[...])
    shared = _dot(hid.astype(BF16), ws2_ref[...])
    x2 = x1_ref[...] + mod_ref[0, 5:6, :] * (routed + shared)
    o_ref[...] = _rms(x2) * fw_ref[...]


def _combine(y8, wgt, h2, x1, mod, ws1, ws3, ws2, fw, seq, tm=256):
    N, D = h2.shape
    nt = seq // tm
    row = pl.BlockSpec((tm, D), lambda t: (t, 0))
    full = lambda a: pl.BlockSpec(a.shape, lambda t: (0,) * a.ndim)
    return pl.pallas_call(
        _combine_kernel,
        grid=(N // tm,),
        in_specs=[pl.BlockSpec((TOP_K, tm, D), lambda t: (0, t, 0)),
                  pl.BlockSpec((tm, TOP_K), lambda t: (t, 0)),
                  row, row,
                  pl.BlockSpec((1, 6, D), lambda t: (t // nt, 0, 0)),
                  full(ws1), full(ws3), full(ws2), full(fw)],
        out_specs=row,
        out_shape=jax.ShapeDtypeStruct((N, D), F32),
        compiler_params=_params("parallel"),
        name="combine",
    )(y8, wgt, h2, x1, mod, ws1, ws3, ws2, fw)


def _t5_bucket(dist):
    n = jnp.maximum(dist, 0)
    max_exact = REL_BUCKETS // 2
    nf = jnp.maximum(n, 1).astype(F32)
    large = max_exact + (jnp.log(nf / max_exact) / math.log(REL_MAX_DIST / max_exact)
                         * (REL_BUCKETS - max_exact)).astype(I32)
    large = jnp.minimum(large, REL_BUCKETS - 1)
    return jnp.where(n < max_exact, n, large)


def _bias_tables(rel_table):
    tbl = rel_table.astype(F32)
    i = jnp.arange(QB)
    d_t = (QB * jnp.arange(N_BIAS_TILES)[:, None, None] + i[None, :, None] - i[None, None, :])
    t_tab = tbl[_t5_bucket(d_t)]
    t_tab = jnp.transpose(t_tab, (3, 0, 1, 2)).reshape(N_KV, GQA, N_BIAS_TILES, QB, QB)
    t_tab = jnp.transpose(t_tab, (0, 2, 1, 3, 4)).reshape(N_KV, N_BIAS_TILES, GQA * QB, QB)
    d_w = i[:, None] - CMP_STRIDE * (jnp.arange(CMP_WIN)[None, :] - CMP_PAD) - (CMP_LEN - 1)
    w_tab = jnp.transpose(tbl[_t5_bucket(d_w)], (2, 0, 1)).reshape(N_KV, GQA * QB, CMP_WIN)
    far_min = min(FAR_TILE_DIST * QB - QB + 1, CMP_STRIDE * (CMP_PAD + 1) - (CMP_LEN - 1))
    max_exact = REL_BUCKETS // 2
    assert math.log(far_min / max_exact) / math.log(REL_MAX_DIST / max_exact) * max_exact > max_exact - 0.75
    return t_tab, w_tab, tbl[REL_BUCKETS - 1]


def _overlap_table(nc, nb):
    c_start = CMP_STRIDE * np.arange(nc)[:, None]
    s_start = SLC_LEN * np.arange(nb)[None, :]
    ov = ((c_start < s_start + SLC_LEN) & (c_start + CMP_LEN > s_start)).astype(np.float32)
    ov[nc - 1:] = 0.0
    out = np.zeros((nc + CMP_WIN, nb), np.float32)
    out[CMP_PAD:CMP_PAD + nc] = ov
    return jnp.asarray(out)


def _block_diag(w):
    nblk, bs, _ = w.shape
    eye = jnp.eye(nblk, dtype=w.dtype)
    return (eye[:, None, :, None] * w[:, :, None, :]).reshape(nblk * bs, nblk * bs)


def _layer(x, c8, w_ada, b_ada, norm1_w, w_in, cmp_pos, cmp_k_w1, cmp_k_w2, cmp_v_w1, cmp_v_w2, rel_table,
           conv_w, conv_b, lru_wa, lru_ba, lru_wx, lru_bx, lru_lambda, gnorm_attn_w, gnorm_rnn_w, w_out,
           norm2_w, w_router, b_router, w1, w3, w2, ws1, ws3, ws2, final_norm_w):
    B, S, D = x.shape
    N = B * S
    E = w_router.shape[1]
    d_attn = N_HEADS * HEAD_DIM
    d_rnn = D - d_attn
    assert S % 512 == 0 and S // SLC_LEN <= 128

    mod = _ada(c8, w_ada, b_ada[None, :])[:B].reshape(B, 6, D)

    n_kv_cols = 6 * N_KV * HEAD_DIM
    n_gate = N_BRANCH * N_HEADS
    wq, wkv, wg, wxr, wgr = jnp.split(w_in, np.cumsum([d_attn, n_kv_cols, n_gate, d_rnn]).tolist(), axis=1)
    wg = jnp.pad(wg.reshape(D, N_KV, GQA * N_BRANCH), ((0, 0), (0, 0), (0, 128 - GQA * N_BRANCH))).reshape(D, 256)
    w_pad = jnp.concatenate([wq, wkv, wg, wxr, wgr], axis=1).astype(BF16)
    q, kv, kvc, gates, xr, gr = _inproj(x, mod, norm1_w[None, :], w_pad)

    nc = S // CMP_STRIDE
    kvc16 = kvc.reshape(B, 4, nc, CMP_STRIDE * HEAD_DIM)
    pos2 = cmp_pos.reshape(2, CMP_STRIDE * HEAD_DIM)
    kvc_pad = _compress(kvc16, pos2, jnp.stack([cmp_k_w1, cmp_v_w1]), jnp.stack([cmp_k_w2, cmp_v_w2]))
    t_tab, w_tab, c31 = _bias_tables(rel_table)
    o_attn = _attn(c31, q, kv, kvc_pad, gates, t_tab, w_tab, _overlap_table(nc, 128))

    y_rnn = _rglru(xr, gr, conv_w.reshape(CONV_WIDTH, d_rnn), conv_b[None, :], _block_diag(lru_wa).astype(BF16),
                   lru_ba[None, :], _block_diag(lru_wx).astype(BF16), lru_bx[None, :], lru_lambda[None, :],
                   gnorm_rnn_w[None, :])

    x1, h2, s_t = _outproj(o_attn, y_rnn, x, mod, gnorm_attn_w[None, :], w_out.astype(BF16), norm2_w[None, :],
                           w_router.T)

    eid_t, wgt_t, rank_t, counts = _route(s_t, b_router[:, None])
    counts = counts[:, 0].astype(I32)
    padded = (counts + MOE_BLOCK - 1) // MOE_BLOCK * MOE_BLOCK
    pend = jnp.cumsum(padded)
    dest = (pend - padded)[eid_t] + rank_t
    nk = N * TOP_K
    n_blk = -(-(nk + E * (MOE_BLOCK - 1)) // MOE_BLOCK)
    P = n_blk * MOE_BLOCK
    tok_ids = jnp.broadcast_to(jnp.arange(N, dtype=I32)[None, :], (TOP_K, N))
    row_ids = tok_ids + (N + 8) * jnp.arange(TOP_K, dtype=I32)[:, None]
    slot_tok = jnp.zeros((P,), I32).at[dest.reshape(-1)].set(tok_ids.reshape(-1))
    spare = N + jnp.arange(P, dtype=I32) % 8
    slot_dst = spare.at[dest.reshape(-1)].set(row_ids.reshape(-1))
    blk_start = jnp.arange(n_blk, dtype=I32) * MOE_BLOCK
    blk_e = jnp.minimum(jnp.searchsorted(pend, blk_start, side='right'), E - 1).astype(I32)
    blk_cnt = jnp.clip(counts[blk_e] - (blk_start - (pend - padded)[blk_e]), 0, MOE_BLOCK)
    blk_cnt = ((blk_cnt + 7) // 8 * 8).astype(I32)
    meta = (pend[-1:] // MOE_BLOCK).astype(I32)

    h2f = h2.reshape(N, D)
    y8 = _experts(meta, blk_e, blk_cnt, slot_tok.reshape(n_blk, 1, MOE_BLOCK),
                  slot_dst.reshape(n_blk, 1, MOE_BLOCK), w1, w3, w2, h2f, TOP_K * (N + 8))
    out = _combine(y8.reshape(TOP_K, N + 8, D), wgt_t.T, h2f, x1.reshape(N, D), mod, ws1.astype(BF16),
                   ws3.astype(BF16), ws2.astype(BF16), final_norm_w[None, :], S)
    return out.reshape(B, S, D)


def kernel(x, c, w_ada, b_ada, norm1_w, w_in, cmp_pos, cmp_k_w1, cmp_k_w2, cmp_v_w1, cmp_v_w2, rel_table, conv_w, conv_b, lru_wa, lru_ba, lru_wx, lru_bx, lru_lambda, gnorm_attn_w, gnorm_rnn_w, w_out, norm2_w, w_router, b_router, w1, w3, w2, ws1, ws3, ws2, final_norm_w):
    assert w_ada.shape[0] == 1
    c8 = jnp.pad(c, ((0, 8 - c.shape[0]), (0, 0)))
    return _layer(x, c8, w_ada[0], b_ada[0], norm1_w[0], w_in[0], cmp_pos[0], cmp_k_w1[0], cmp_k_w2[0],
                  cmp_v_w1[0], cmp_v_w2[0], rel_table, conv_w[0], conv_b[0], lru_wa[0], lru_ba[0], lru_wx[0],
                  lru_bx[0], lru_lambda[0], gnorm_attn_w[0], gnorm_rnn_w[0], w_out[0], norm2_w[0], w_router[0],
                  b_router[0], w1[0], w3[0], w2[0], ws1[0], ws3[0], ws2[0], final_norm_w)
```

```python
import functools
import math

import jax
import jax.numpy as jnp
import numpy as np
from jax import lax
from jax.experimental import pallas as pl
from jax.experimental.pallas import tpu as pltpu

F32 = jnp.float32
BF16 = jnp.bfloat16
I32 = jnp.int32

HEAD_DIM = 64
N_HEADS = 8
N_KV = 2
GQA = N_HEADS // N_KV
N_BRANCH = 3
CONV_WIDTH = 4
LRU_C = 8.0
CMP_LEN = 32
CMP_STRIDE = 16
SLC_LEN = 64
SLC_TOP = 16
WINDOW = 512
QB = 128
REL_BUCKETS = 32
REL_MAX_DIST = 1024
N_EXPERT_GROUPS = 8
TOP_GROUPS = 4
TOP_K = 8
ROUTED_SCALE = 2.5
MOE_BLOCK = 128
EPS = 1e-6
NEG = -1e30
CMP_PAD = 56
CMP_WIN = 64
N_BIAS_TILES = 11
FAR_TILE_DIST = 8
FAR_CHUNK = 4
VMEM_LIMIT = 52 * 1024 * 1024


def _dot(a, b, **kw):
    return jnp.dot(a, b, preferred_element_type=F32, **kw)


def _dot_nt(a, b, **kw):
    return lax.dot_general(a, b, (((1,), (1,)), ((), ())), preferred_element_type=F32, **kw)


def _gelu(x):
    return 0.5 * x * (1.0 + jnp.tanh(math.sqrt(2.0 / math.pi) * (x + 0.044715 * (x * x * x))))


def _sigmoid(x):
    return 1.0 / (1.0 + jnp.exp(-x))


def _rms(x):
    return x * lax.rsqrt(jnp.mean(x * x, axis=-1, keepdims=True) + EPS)


def _params(*sem):
    return pltpu.CompilerParams(dimension_semantics=sem, vmem_limit_bytes=VMEM_LIMIT)


def _ada_kernel(c_ref, w_ref, b_ref, o_ref):
    c = c_ref[...]
    a = c * _sigmoid(c)
    o_ref[...] = _dot(a, w_ref[...], precision=lax.Precision.HIGHEST) + b_ref[...]


def _ada(c8, w, b):
    d, n = w.shape
    tn = 1536
    return pl.pallas_call(
        _ada_kernel,
        grid=(n // tn,),
        in_specs=[pl.BlockSpec((8, d), lambda j: (0, 0)),
                  pl.BlockSpec((d, tn), lambda j: (0, j)),
                  pl.BlockSpec((1, tn), lambda j: (0, j))],
        out_specs=pl.BlockSpec((8, tn), lambda j: (0, j)),
        out_shape=jax.ShapeDtypeStruct((8, n), F32),
        compiler_params=_params("parallel"),
        name="ada",
    )(c8, w, b)


def _inproj_kernel(x_ref, mod_ref, nw_ref, w_ref, q_ref, kv_ref, kvc_ref, g_ref, xr_ref, gr_ref):
    h = _rms(x_ref[0]) * nw_ref[...]
    h = h * (1.0 + mod_ref[0, 1:2, :]) + mod_ref[0, 0:1, :]
    p = _dot(h.astype(BF16), w_ref[...])
    dq = N_HEADS * HEAD_DIM
    for hh in range(N_HEADS):
        q_ref[0, hh] = (p[:, hh * HEAD_DIM:(hh + 1) * HEAD_DIM] * (HEAD_DIM ** -0.5)).astype(BF16)
    for j in range(6):
        for g in range(N_KV):
            col = dq + (j * N_KV + g) * HEAD_DIM
            piece = p[:, col:col + HEAD_DIM]
            if j < 2:
                kvc_ref[0, j * N_KV + g] = piece
            else:
                kv_ref[0, (j - 2) * N_KV + g] = piece.astype(BF16)
    c0 = dq + 6 * N_KV * HEAD_DIM
    g_ref[0] = _sigmoid(p[:, c0:c0 + 256])
    xr_ref[0] = p[:, c0 + 256:c0 + 768]
    gr_ref[0] = p[:, c0 + 768:c0 + 1280]


def _inproj(x, mod, nw, w_pad, tm=256):
    B, S, D = x.shape
    ncol = w_pad.shape[1]
    return pl.pallas_call(
        _inproj_kernel,
        grid=(B, S // tm),
        in_specs=[pl.BlockSpec((1, tm, D), lambda b, t: (b, t, 0)),
                  pl.BlockSpec((1, 6, D), lambda b, t: (b, 0, 0)),
                  pl.BlockSpec((1, D), lambda b, t: (0, 0)),
                  pl.BlockSpec((D, ncol), lambda b, t: (0, 0))],
        out_specs=[pl.BlockSpec((1, N_HEADS, tm, HEAD_DIM), lambda b, t: (b, 0, t, 0)),
                   pl.BlockSpec((1, 8, tm, HEAD_DIM), lambda b, t: (b, 0, t, 0)),
                   pl.BlockSpec((1, 4, tm, HEAD_DIM), lambda b, t: (b, 0, t, 0)),
                   pl.BlockSpec((1, tm, 256), lambda b, t: (b, t, 0)),
                   pl.BlockSpec((1, tm, 512), lambda b, t: (b, t, 0)),
                   pl.BlockSpec((1, tm, 512), lambda b, t: (b, t, 0))],
        out_shape=[jax.ShapeDtypeStruct((B, N_HEADS, S, HEAD_DIM), BF16),
                   jax.ShapeDtypeStruct((B, 8, S, HEAD_DIM), BF16),
                   jax.ShapeDtypeStruct((B, 4, S, HEAD_DIM), F32),
                   jax.ShapeDtypeStruct((B, S, 256), F32),
                   jax.ShapeDtypeStruct((B, S, 512), F32),
                   jax.ShapeDtypeStruct((B, S, 512), F32)],
        compiler_params=_params("parallel", "parallel"),
        name="inproj",
    )(x, mod, nw, w_pad)


def _compress_kernel(x_ref, pos_ref, w1_ref, w2_ref, o_ref):
    x = x_ref[0, 0]
    nc = x.shape[0]
    half = CMP_STRIDE * HEAD_DIM
    kv = pl.program_id(1) // N_KV
    a = _dot((x + pos_ref[0:1, :]).astype(BF16), w1_ref[kv, 0:half, :].astype(BF16))
    b = _dot((x + pos_ref[1:2, :]).astype(BF16), w1_ref[kv, half:2 * half, :].astype(BF16))
    hid = _gelu(a + pltpu.roll(b, nc - 1, 0))
    out = _dot(hid.astype(BF16), w2_ref[kv].astype(BF16))
    row = lax.broadcasted_iota(I32, out.shape, 0)
    out = jnp.where(row < nc - 1, out, 0.0)
    o_ref[0, 0, 0:CMP_PAD, :] = jnp.zeros((CMP_PAD, HEAD_DIM), F32)
    o_ref[0, 0, CMP_PAD:CMP_PAD + nc, :] = out
    o_ref[0, 0, CMP_PAD + nc:, :] = jnp.zeros((CMP_WIN - CMP_PAD, HEAD_DIM), F32)


def _compress(kvc, pos2, w1, w2):
    B, _, NC, W = kvc.shape
    return pl.pallas_call(
        _compress_kernel,
        grid=(B, 4),
        in_specs=[pl.BlockSpec((1, 1, NC, W), lambda b, i: (b, i, 0, 0)),
                  pl.BlockSpec((2, W), lambda b, i: (0, 0)),
                  pl.BlockSpec((2, 2 * W, HEAD_DIM), lambda b, i: (0, 0, 0)),
                  pl.BlockSpec((2, HEAD_DIM, HEAD_DIM), lambda b, i: (0, 0, 0))],
        out_specs=pl.BlockSpec((1, 1, NC + CMP_WIN, HEAD_DIM), lambda b, i: (b, i, 0, 0)),
        out_shape=jax.ShapeDtypeStruct((B, 4, NC + CMP_WIN, HEAD_DIM), F32),
        compiler_params=_params("parallel", "parallel"),
        name="compress",
    )(kvc, pos2, w1, w2)


def _attn_kernel(c31_ref, q_ref, ks_ref, vs_ref, kw_ref, vw_ref, kc_ref, vc_ref, g_ref, t_ref, w_ref, ov_ref,
                 o_ref, m_scr, l_scr, acc_scr):
    g = pl.program_id(1)
    qb = pl.program_id(2)
    R = GQA * QB
    q = q_ref[0].reshape(R, HEAD_DIM)
    nc = kc_ref.shape[2] - CMP_WIN
    nb = ov_ref.shape[1]
    n_top = min(SLC_TOP, ks_ref.shape[2] // SLC_LEN)

    c31 = jnp.concatenate([jnp.full((QB, 1), c31_ref[GQA * g + hq], F32) for hq in range(GQA)], axis=0)

    w0 = pl.multiple_of(qb * 8, 8)
    kcf = kc_ref[0, 0, CMP_PAD:CMP_PAD + nc, :].astype(BF16)
    vcf = vc_ref[0, 0, CMP_PAD:CMP_PAD + nc, :].astype(BF16)
    kcw = kc_ref[0, 0, pl.ds(w0, CMP_WIN), :].astype(BF16)
    vcw = vc_ref[0, 0, pl.ds(w0, CMP_WIN), :].astype(BF16)
    s_far = _dot_nt(q, kcf) + c31
    n_io = lax.broadcasted_iota(I32, (R, nc), 1)
    s_far = jnp.where(n_io < qb * 8 - CMP_PAD, s_far, -jnp.inf)
    s_win = _dot_nt(q, kcw) + w_ref[0]
    i_w = lax.broadcasted_iota(I32, (R, CMP_WIN), 0) & (QB - 1)
    j_w = lax.broadcasted_iota(I32, (R, CMP_WIN), 1)
    dist_w = i_w - CMP_STRIDE * (j_w - CMP_PAD) - (CMP_LEN - 1)
    s_win = jnp.where((dist_w >= 0) & (j_w >= CMP_PAD - qb * 8), s_win, -jnp.inf)
    m = jnp.maximum(jnp.max(s_far, axis=-1, keepdims=True), jnp.max(s_win, axis=-1, keepdims=True))
    m = jnp.where(m == -jnp.inf, 0.0, m)
    e_far = jnp.exp(s_far - m)
    e_win = jnp.exp(s_win - m)
    l = jnp.sum(e_far, axis=-1, keepdims=True) + jnp.sum(e_win, axis=-1, keepdims=True)
    inv = 1.0 / jnp.maximum(l, 1e-30)
    p_far = e_far * inv
    p_win = e_win * inv
    o_c = _dot(p_far.astype(BF16), vcf) + _dot(p_win.astype(BF16), vcw)
    pg_far = jnp.sum(p_far.reshape(GQA, QB, nc), axis=0)
    pg_win = jnp.sum(p_win.reshape(GQA, QB, CMP_WIN), axis=0)
    score = (_dot(pg_far.astype(BF16), ov_ref[CMP_PAD:CMP_PAD + nc, :].astype(BF16))
             + _dot(pg_win.astype(BF16), ov_ref[pl.ds(w0, CMP_WIN), :].astype(BF16)))

    sc_t = score.T
    blk = lax.broadcasted_iota(I32, (nb, QB), 0)
    t_io = lax.broadcasted_iota(I32, (nb, QB), 1)
    cur = qb * (QB // SLC_LEN) + t_io // SLC_LEN
    forced = (blk == 0) | (blk == cur) | (blk == cur - 1)
    val = jnp.where(forced, jnp.inf, jnp.where(blk <= cur, sc_t, -jnp.inf))
    sel_t = jnp.zeros((nb, QB), F32)
    for _ in range(n_top):
        mx = jnp.max(val, axis=0, keepdims=True)
        hit = (val == mx) & (val > -jnp.inf)
        first = jnp.min(jnp.where(hit, blk, nb), axis=0, keepdims=True)
        pick = blk == first
        sel_t = jnp.where(pick, 1.0, sel_t)
        val = jnp.where(pick, -jnp.inf, val)
    sel = sel_t.T.astype(BF16)

    def reset():
        m_scr[...] = jnp.full(m_scr.shape, NEG, F32)
        l_scr[...] = jnp.zeros(l_scr.shape, F32)
        acc_scr[...] = jnp.zeros(acc_scr.shape, F32)

    def update(s, mask, v):
        kc = s.shape[1]
        s = jnp.where(mask[None], s.reshape(GQA, QB, kc), NEG).reshape(R, kc)
        m_old = m_scr[...]
        m_new = jnp.maximum(m_old, jnp.max(s, axis=-1, keepdims=True))
        alpha = jnp.exp(m_old - m_new)
        p = jnp.exp(s - m_new)
        l_scr[...] = alpha * l_scr[...] + jnp.sum(p, axis=-1, keepdims=True)
        acc_scr[...] = alpha * acc_scr[...] + _dot(p.astype(BF16), v)
        m_scr[...] = m_new

    def sel_mask(first_blk, kc):
        bj = lax.broadcasted_iota(I32, (nb, kc), 0)
        kj = lax.broadcasted_iota(I32, (nb, kc), 1)
        expand = jnp.where(bj == first_blk + kj // SLC_LEN, 1.0, 0.0).astype(BF16)
        return _dot(sel, expand) > 0.5

    i_t = lax.broadcasted_iota(I32, (QB, QB), 0)
    j_t = lax.broadcasted_iota(I32, (QB, QB), 1)

    reset()
    n_far = jnp.maximum(qb - (FAR_TILE_DIST - 1), 0) // FAR_CHUNK
    far_w = FAR_CHUNK * QB

    def far_body(c, carry):
        k0 = pl.multiple_of(c * far_w, far_w)
        s = _dot_nt(q, ks_ref[0, 0, pl.ds(k0, far_w), :]) + c31
        update(s, sel_mask(c * (far_w // SLC_LEN), far_w), vs_ref[0, 0, pl.ds(k0, far_w), :])
        return carry

    lax.fori_loop(0, n_far, far_body, 0)

    def near_body(kb, carry):
        k0 = pl.multiple_of(kb * QB, QB)
        delta = qb - kb
        s = _dot_nt(q, ks_ref[0, 0, pl.ds(k0, QB), :]) + t_ref[0, delta]
        mask = sel_mask(kb * (QB // SLC_LEN), QB) & (delta * QB + i_t - j_t >= 0)
        update(s, mask, vs_ref[0, 0, pl.ds(k0, QB), :])
        return carry

    lax.fori_loop(n_far * FAR_CHUNK, qb + 1, near_body, 0)
    o_s = acc_scr[...] / l_scr[...]

    reset()

    def win_body(kb, carry):
        k0 = pl.multiple_of(kb * QB, QB)
        delta = qb - kb
        s = _dot_nt(q, kw_ref[0, 0, pl.ds(k0, QB), :]) + t_ref[0, delta]
        dist = delta * QB + i_t - j_t
        update(s, (dist >= 0) & (dist < WINDOW), vw_ref[0, 0, pl.ds(k0, QB), :])
        return carry

    lax.fori_loop(jnp.maximum(qb - WINDOW // QB, 0), qb + 1, win_body, 0)
    o_w = acc_scr[...] / l_scr[...]

    gates = g_ref[0]
    for hq in range(GQA):
        rows = slice(hq * QB, (hq + 1) * QB)
        c = hq * N_BRANCH
        o_ref[0, :, hq * HEAD_DIM:(hq + 1) * HEAD_DIM] = (
            gates[:, c:c + 1] * o_c[rows] + gates[:, c + 1:c + 2] * o_s[rows] + gates[:, c + 2:c + 3] * o_w[rows])


def _attn(c31, q, kv, kvc, gates, t_tab, w_tab, ov):
    B, H, S, Dh = q.shape
    ncp = kvc.shape[2]
    nb = ov.shape[1]

    def kv_spec(j):
        return pl.BlockSpec((1, 1, S, Dh), lambda b, g, t, c: (b, j * N_KV + g, 0, 0))

    def kvc_spec(j):
        return pl.BlockSpec((1, 1, ncp, Dh), lambda b, g, t, c: (b, j * N_KV + g, 0, 0))

    grid_spec = pltpu.PrefetchScalarGridSpec(
        num_scalar_prefetch=1,
        grid=(B, N_KV, S // QB),
        in_specs=[pl.BlockSpec((1, GQA, QB, Dh), lambda b, g, t, c: (b, g, t, 0)),
                  kv_spec(0), kv_spec(1), kv_spec(2), kv_spec(3), kvc_spec(0), kvc_spec(1),
                  pl.BlockSpec((1, QB, 128), lambda b, g, t, c: (b, t, g)),
                  pl.BlockSpec((1, N_BIAS_TILES, GQA * QB, QB), lambda b, g, t, c: (g, 0, 0, 0)),
                  pl.BlockSpec((1, GQA * QB, CMP_WIN), lambda b, g, t, c: (g, 0, 0)),
                  pl.BlockSpec((ncp, nb), lambda b, g, t, c: (0, 0))],
        out_specs=pl.BlockSpec((1, QB, GQA * Dh), lambda b, g, t, c: (b, t, g)),
        scratch_shapes=[pltpu.VMEM((GQA * QB, 1), F32), pltpu.VMEM((GQA * QB, 1), F32),
                        pltpu.VMEM((GQA * QB, Dh), F32)],
    )
    return pl.pallas_call(
        _attn_kernel,
        grid_spec=grid_spec,
        out_shape=jax.ShapeDtypeStruct((B, S, H * Dh), F32),
        compiler_params=_params("parallel", "parallel", "arbitrary"),
        name="attn",
    )(c31, q, kv, kv, kv, kv, kvc, kvc, gates, t_tab, w_tab, ov)


def _rglru_kernel(xr_ref, gr_ref, cw_ref, cb_ref, wa_ref, ba_ref, wx_ref, bx_ref, lam_ref, gw_ref, o_ref,
                  xbuf, hprev, a_scr, u_scr, h_scr):
    ts = xr_ref.shape[1]
    C = xr_ref.shape[2]

    @pl.when(pl.program_id(1) == 0)
    def _():
        xbuf[0:8, :] = jnp.zeros((8, C), F32)
        hprev[...] = jnp.zeros(hprev.shape, F32)

    xbuf[8:8 + ts, :] = xr_ref[0]
    xc = cb_ref[...] + jnp.zeros((ts, C), F32)
    for j in range(CONV_WIDTH):
        xc = xc + cw_ref[j:j + 1, :] * xbuf[pl.ds(8 - (CONV_WIDTH - 1) + j, ts), :]
    xbuf[0:8, :] = xbuf[ts:ts + 8, :]

    xcb = xc.astype(BF16)
    r = _sigmoid(_dot(xcb, wa_ref[...]) + ba_ref[...])
    i = _sigmoid(_dot(xcb, wx_ref[...]) + bx_ref[...])
    z = -lam_ref[...]
    softplus = jnp.maximum(z, 0.0) + jnp.log(1.0 + jnp.exp(-jnp.abs(z)))
    log_a = -LRU_C * r * softplus
    a_scr[...] = jnp.exp(log_a)
    u_scr[...] = jnp.sqrt(1.0 - jnp.exp(2.0 * log_a)) * (i * xc)

    row = lax.broadcasted_iota(I32, (8, C), 0)

    def body(k, h):
        r0 = pl.multiple_of(k * 8, 8)
        a = a_scr[pl.ds(r0, 8), :]
        b = u_scr[pl.ds(r0, 8), :]
        for s in (1, 2, 4):
            keep = row >= s
            b = jnp.where(keep, a * pltpu.roll(b, s, 0) + b, b)
            a = jnp.where(keep, a * pltpu.roll(a, s, 0), a)
        hh = a * h + b
        h_scr[pl.ds(r0, 8), :] = hh
        return jnp.broadcast_to(hh[7:8, :], (8, C))

    hprev[...] = lax.fori_loop(0, ts // 8, body, hprev[...])
    out = h_scr[...] * _gelu(gr_ref[0])
    o_ref[0] = (_rms(out) * gw_ref[...]).astype(BF16)


def _rglru(xr, gr, cw, cb, wa_bd, ba, wx_bd, bx, lam, gw, ts=512):
    B, S, C = xr.shape
    ts = min(ts, S)
    vec = pl.BlockSpec((1, C), lambda b, t: (0, 0))
    mat = pl.BlockSpec((C, C), lambda b, t: (0, 0))
    seq = pl.BlockSpec((1, ts, C), lambda b, t: (b, t, 0))
    return pl.pallas_call(
        _rglru_kernel,
        grid=(B, S // ts),
        in_specs=[seq, seq, pl.BlockSpec((CONV_WIDTH, C), lambda b, t: (0, 0)), vec, mat, vec, mat, vec, vec, vec],
        out_specs=seq,
        out_shape=jax.ShapeDtypeStruct((B, S, C), BF16),
        scratch_shapes=[pltpu.VMEM((ts + 8, C), F32), pltpu.VMEM((8, C), F32), pltpu.VMEM((ts, C), F32),
                        pltpu.VMEM((ts, C), F32), pltpu.VMEM((ts, C), F32)],
        compiler_params=_params("parallel", "arbitrary"),
        name="rglru",
    )(xr, gr, cw, cb, wa_bd, ba, wx_bd, bx, lam, gw)


def _outproj_kernel(oa_ref, yr_ref, x_ref, mod_ref, gaw_ref, wo_ref, n2_ref, wr_ref, x1_ref, h2_ref, st_ref):
    da = oa_ref.shape[2]
    ya = (_rms(oa_ref[0]) * gaw_ref[...]).astype(BF16)
    mix = _dot(ya, wo_ref[0:da, :]) + _dot(yr_ref[0], wo_ref[da:, :])
    x1 = x_ref[0] + mod_ref[0, 2:3, :] * mix
    x1_ref[0] = x1
    h2 = (_rms(x1) * n2_ref[...]) * (1.0 + mod_ref[0, 4:5, :]) + mod_ref[0, 3:4, :]
    h2_ref[0] = h2
    st_ref[...] = _sigmoid(_dot_nt(wr_ref[...], h2, precision=lax.Precision.HIGHEST))


def _outproj(oa, yr, x, mod, gaw, wo, n2, wr_t, tm=256):
    B, S, D = x.shape
    E = wr_t.shape[0]
    nt = S // tm
    row = lambda w: pl.BlockSpec((1, tm, w), lambda b, t: (b, t, 0))
    return pl.pallas_call(
        _outproj_kernel,
        grid=(B, nt),
        in_specs=[row(oa.shape[2]), row(yr.shape[2]), row(D),
                  pl.BlockSpec((1, 6, D), lambda b, t: (b, 0, 0)),
                  pl.BlockSpec((1, oa.shape[2]), lambda b, t: (0, 0)),
                  pl.BlockSpec(wo.shape, lambda b, t: (0, 0)),
                  pl.BlockSpec((1, D), lambda b, t: (0, 0)),
                  pl.BlockSpec((E, D), lambda b, t: (0, 0))],
        out_specs=[row(D), row(D), pl.BlockSpec((E, tm), lambda b, t: (0, b * nt + t))],
        out_shape=[jax.ShapeDtypeStruct((B, S, D), F32), jax.ShapeDtypeStruct((B, S, D), F32),
                   jax.ShapeDtypeStruct((E, B * S), F32)],
        compiler_params=_params("parallel", "parallel"),
        name="outproj",
    )(oa, yr, x, mod, gaw, wo, n2, wr_t)


def _route_kernel(s_ref, b_ref, eid_ref, wgt_ref, rank_ref, cnt_ref, carry):
    E, tn = s_ref.shape
    per = E // N_EXPERT_GROUPS

    @pl.when(pl.program_id(0) == 0)
    def _():
        carry[...] = jnp.zeros(carry.shape, F32)

    s = s_ref[...]
    s_sel = s + b_ref[...]
    eidx = lax.broadcasted_iota(I32, (E, tn), 0)

    grp = []
    for gi in range(N_EXPERT_GROUPS):
        xg = s_sel[gi * per:(gi + 1) * per, :]
        ig = lax.broadcasted_iota(I32, (per, tn), 0)
        m1 = jnp.max(xg, axis=0, keepdims=True)
        f1 = jnp.min(jnp.where(xg == m1, ig, per), axis=0, keepdims=True)
        m2 = jnp.max(jnp.where(ig == f1, -jnp.inf, xg), axis=0, keepdims=True)
        grp.append(m1 + m2)
    val = []
    for gi in range(N_EXPERT_GROUPS):
        rank = jnp.zeros((1, tn), I32)
        for gj in range(N_EXPERT_GROUPS):
            if gj == gi:
                continue
            ahead = (grp[gj] > grp[gi]) | ((grp[gj] == grp[gi]) & (gj < gi))
            rank = rank + ahead.astype(I32)
        val.append(jnp.where(rank < TOP_GROUPS, s_sel[gi * per:(gi + 1) * per, :], -jnp.inf))
    val = jnp.concatenate(val, axis=0)

    eids, tops = [], []
    hot = jnp.zeros((E, tn), F32)
    for _ in range(TOP_K):
        mx = jnp.max(val, axis=0, keepdims=True)
        first = jnp.min(jnp.where(val == mx, eidx, E), axis=0, keepdims=True)
        pick = eidx == first
        eids.append(first)
        tops.append(jnp.sum(jnp.where(pick, s, 0.0), axis=0, keepdims=True))
        hot = jnp.where(pick, 1.0, hot)
        val = jnp.where(pick, -jnp.inf, val)
    denom = tops[0]
    for t in tops[1:]:
        denom = denom + t

    ti = lax.broadcasted_iota(I32, (tn, tn), 0)
    tj = lax.broadcasted_iota(I32, (tn, tn), 1)
    upper = jnp.where(ti < tj, 1.0, 0.0).astype(BF16)
    before = _dot(hot.astype(BF16), upper) + carry[...]
    for k in range(TOP_K):
        eid_ref[k:k + 1, :] = eids[k]
        wgt_ref[k:k + 1, :] = ROUTED_SCALE * tops[k] / denom
        rank_ref[k:k + 1, :] = jnp.sum(jnp.where(eidx == eids[k], before, 0.0), axis=0, keepdims=True).astype(I32)
    carry[...] = carry[...] + jnp.sum(hot, axis=1, keepdims=True)
    cnt_ref[...] = carry[...]


def _route(s_t, b_col, tn=256):
    E, N = s_t.shape
    tn = min(tn, N)
    out = pl.BlockSpec((TOP_K, tn), lambda t: (0, t))
    return pl.pallas_call(
        _route_kernel,
        grid=(N // tn,),
        in_specs=[pl.BlockSpec((E, tn), lambda t: (0, t)), pl.BlockSpec((E, 1), lambda t: (0, 0))],
        out_specs=[out, out, out, pl.BlockSpec((E, 1), lambda t: (0, 0))],
        out_shape=[jax.ShapeDtypeStruct((TOP_K, N), I32), jax.ShapeDtypeStruct((TOP_K, N), F32),
                   jax.ShapeDtypeStruct((TOP_K, N), I32), jax.ShapeDtypeStruct((E, 1), F32)],
        scratch_shapes=[pltpu.VMEM((E, 1), F32)],
        compiler_params=_params("arbitrary"),
        name="route",
    )(s_t, b_col)


def _dest_kernel(eid_ref, rank_ref, off_ref, o_ref):
    E = off_ref.shape[0]
    tn = eid_ref.shape[1]
    eidx = lax.broadcasted_iota(I32, (E, tn), 0)
    for k in range(TOP_K):
        base = jnp.sum(jnp.where(eidx == eid_ref[k:k + 1, :], off_ref[...], 0.0), axis=0, keepdims=True)
        o_ref[k:k + 1, :] = base.astype(I32) + rank_ref[k:k + 1, :]


def _dest(eid_t, rank_t, off_col, tn=512):
    K, N = eid_t.shape
    tn = min(tn, N)
    blk = pl.BlockSpec((K, tn), lambda t: (0, t))
    return pl.pallas_call(
        _dest_kernel,
        grid=(N // tn,),
        in_specs=[blk, blk, pl.BlockSpec(off_col.shape, lambda t: (0, 0))],
        out_specs=blk,
        out_shape=jax.ShapeDtypeStruct((K, N), I32),
        compiler_params=_params("parallel"),
        name="dest",
    )(eid_t, rank_t, off_col)


def _dispatch_kernel(dest_ref, h_ref, xs_hbm, sem):
    tn = h_ref.shape[0]

    def go(t, c):
        for k in range(TOP_K):
            pltpu.make_async_copy(h_ref.at[pl.ds(t, 1)], xs_hbm.at[pl.ds(dest_ref[k, t], 1)], sem.at[0]).start()
        return c
    lax.fori_loop(0, tn, go, 0)
    for k in range(TOP_K):
        pltpu.make_async_copy(h_ref, xs_hbm.at[pl.ds(0, tn)], sem.at[0]).wait()


def _dispatch(dest_t, h2, n_slots, tn=256):
    N, D = h2.shape
    tn = min(tn, N)
    return pl.pallas_call(
        _dispatch_kernel,
        grid=(N // tn,),
        in_specs=[pl.BlockSpec((TOP_K, tn), lambda t: (0, t), memory_space=pltpu.SMEM),
                  pl.BlockSpec((tn, D), lambda t: (t, 0))],
        out_specs=pl.BlockSpec(memory_space=pl.ANY),
        out_shape=jax.ShapeDtypeStruct((n_slots, D), F32),
        scratch_shapes=[pltpu.SemaphoreType.DMA((1,))],
        compiler_params=_params("arbitrary"),
        name="dispatch",
    )(dest_t, h2)


def _experts_kernel(rb_ref, e_ref, lo_ref, hi_ref, first_ref, x_ref, w1_ref, w3_ref, w2_ref, y_ref):
    i = pl.program_id(0)
    lo = lo_ref[i]
    hi = hi_ref[i]

    @pl.when(hi > lo)
    def _():
        x = x_ref[...].astype(BF16)
        h1 = _dot(x, w1_ref[0].astype(BF16))
        h3 = _dot(x, w3_ref[0].astype(BF16))
        hid = (h1 * _sigmoid(h1)) * h3
        y = _dot(hid.astype(BF16), w2_ref[0].astype(BF16))
        row = lax.broadcasted_iota(I32, y.shape, 0)
        mine = (row >= lo) & (row < hi)

        @pl.when(first_ref[i] == 1)
        def _():
            y_ref[...] = jnp.where(mine, y, 0.0)

        @pl.when(first_ref[i] == 0)
        def _():
            y_ref[...] = jnp.where(mine, y, y_ref[...])


def _experts(items, xs, w1, w3, w2):
    NK, D = xs.shape
    De = w1.shape[2]
    rows = pl.BlockSpec((MOE_BLOCK, D), lambda i, rb, e, lo, hi, first: (rb[i], 0))
    wspec = lambda shape: pl.BlockSpec(shape, lambda i, rb, e, lo, hi, first: (e[i], 0, 0))
    grid_spec = pltpu.PrefetchScalarGridSpec(
        num_scalar_prefetch=5,
        grid=(items[0].shape[0],),
        in_specs=[rows, wspec((1, D, De)), wspec((1, D, De)), wspec((1, De, D))],
        out_specs=rows,
    )
    return pl.pallas_call(
        _experts_kernel,
        grid_spec=grid_spec,
        out_shape=jax.ShapeDtypeStruct((NK, D), F32),
        compiler_params=_params("arbitrary"),
        name="experts",
    )(*items, xs, w1, w3, w2)


def _expert_items(counts, n_rows):
    E = counts.shape[0]
    n_rb = n_rows // MOE_BLOCK
    n_items = n_rb + E
    ce = jnp.cumsum(counts)
    cs = ce - counts
    first_blk = cs // MOE_BLOCK
    nb = jnp.where(counts > 0, (ce - 1) // MOE_BLOCK - first_blk + 1, 0)
    io_end = jnp.cumsum(nb)
    w = jnp.arange(n_items, dtype=I32)
    valid = w < io_end[-1]
    e_w = jnp.minimum(jnp.sum(io_end[None, :] <= w[:, None], axis=1), E - 1).astype(I32)
    rb = jnp.where(valid, first_blk[e_w] + w - (io_end - nb)[e_w], n_rb - 1).astype(I32)
    lo = jnp.where(valid, jnp.clip(cs[e_w] - MOE_BLOCK * rb, 0, MOE_BLOCK), 0).astype(I32)
    hi = jnp.where(valid, jnp.clip(ce[e_w] - MOE_BLOCK * rb, 0, MOE_BLOCK), 0).astype(I32)
    first = jnp.concatenate([jnp.ones((1,), I32), (rb[1:] != rb[:-1]).astype(I32)])
    return rb, e_w, lo, hi, first


def _combine_kernel(dest_ref, ys_hbm, w_ref, h2_ref, x1_ref, mod_ref, ws1_ref, ws3_ref, ws2_ref, fw_ref, o_ref,
                    ybuf, sem):
    tm = h2_ref.shape[0]

    def go(t, c):
        for k in range(TOP_K):
            pltpu.make_async_copy(ys_hbm.at[pl.ds(dest_ref[k, t], 1)], ybuf.at[k, pl.ds(t, 1)], sem.at[0]).start()
        return c
    lax.fori_loop(0, tm, go, 0)

    h = h2_ref[...].astype(BF16)
    a = _dot(h, ws1_ref[...])
    hid = (a * _sigmoid(a)) * _dot(h, ws3_ref[...])
    shared = _dot(hid.astype(BF16), ws2_ref[...])

    for k in range(TOP_K):
        pltpu.make_async_copy(ys_hbm.at[pl.ds(0, tm)], ybuf.at[k], sem.at[0]).wait()
    routed = w_ref[:, 0:1] * ybuf[0]
    for k in range(1, TOP_K):
        routed = routed + w_ref[:, k:k + 1] * ybuf[k]
    x2 = x1_ref[...] + mod_ref[0, 5:6, :] * (routed + shared)
    o_ref[...] = _rms(x2) * fw_ref[...]


def _combine(dest_t, ys, wgt, h2, x1, mod, ws1, ws3, ws2, fw, seq, tm=256):
    N, D = h2.shape
    tm = min(tm, seq)
    nt = seq // tm
    row = pl.BlockSpec((tm, D), lambda t: (t, 0))
    full = lambda a: pl.BlockSpec(a.shape, lambda t: (0,) * a.ndim)
    return pl.pallas_call(
        _combine_kernel,
        grid=(N // tm,),
        in_specs=[pl.BlockSpec((TOP_K, tm), lambda t: (0, t), memory_space=pltpu.SMEM),
                  pl.BlockSpec(memory_space=pl.ANY),
                  pl.BlockSpec((tm, TOP_K), lambda t: (t, 0)),
                  row, row,
                  pl.BlockSpec((1, 6, D), lambda t: (t // nt, 0, 0)),
                  full(ws1), full(ws3), full(ws2), full(fw)],
        out_specs=row,
        out_shape=jax.ShapeDtypeStruct((N, D), F32),
        scratch_shapes=[pltpu.VMEM((TOP_K, tm, D), F32), pltpu.SemaphoreType.DMA((1,))],
        compiler_params=_params("arbitrary"),
        name="combine",
    )(dest_t, ys, wgt, h2, x1, mod, ws1, ws3, ws2, fw)


def _t5_bucket(dist):
    n = jnp.maximum(dist, 0)
    max_exact = REL_BUCKETS // 2
    nf = jnp.maximum(n, 1).astype(F32)
    large = max_exact + (jnp.log(nf / max_exact) / math.log(REL_MAX_DIST / max_exact)
                         * (REL_BUCKETS - max_exact)).astype(I32)
    large = jnp.minimum(large, REL_BUCKETS - 1)
    return jnp.where(n < max_exact, n, large)


def _bias_kernel(tbl_ref, bt_ref, bw_ref, t_ref, w_ref):
    g = pl.program_id(0)

    def lookup(bkt, head):
        out = jnp.full(bkt.shape, tbl_ref[0, head], F32)
        for b in range(1, REL_BUCKETS):
            out = jnp.where(bkt == b, tbl_ref[b, head], out)
        return out

    for hq in range(GQA):
        t_ref[0, 0, hq * QB:(hq + 1) * QB, :] = lookup(bt_ref[0], GQA * g + hq)

    @pl.when(pl.program_id(1) == 0)
    def _():
        for hq in range(GQA):
            w_ref[0, hq * QB:(hq + 1) * QB, :] = lookup(bw_ref[...], GQA * g + hq)


def _bias_tables(rel_table):
    tbl = rel_table.astype(F32)
    i = jnp.arange(QB)
    d_t = (QB * jnp.arange(N_BIAS_TILES)[:, None, None] + i[None, :, None] - i[None, None, :])
    d_w = i[:, None] - CMP_STRIDE * (jnp.arange(CMP_WIN)[None, :] - CMP_PAD) - (CMP_LEN - 1)
    t_tab, w_tab = pl.pallas_call(
        _bias_kernel,
        grid=(N_KV, N_BIAS_TILES),
        in_specs=[pl.BlockSpec(memory_space=pltpu.SMEM),
                  pl.BlockSpec((1, QB, QB), lambda g, d: (d, 0, 0)),
                  pl.BlockSpec((QB, CMP_WIN), lambda g, d: (0, 0))],
        out_specs=[pl.BlockSpec((1, 1, GQA * QB, QB), lambda g, d: (g, d, 0, 0)),
                   pl.BlockSpec((1, GQA * QB, CMP_WIN), lambda g, d: (g, 0, 0))],
        out_shape=[jax.ShapeDtypeStruct((N_KV, N_BIAS_TILES, GQA * QB, QB), F32),
                   jax.ShapeDtypeStruct((N_KV, GQA * QB, CMP_WIN), F32)],
        compiler_params=_params("parallel", "arbitrary"),
        name="bias",
    )(tbl, _t5_bucket(d_t), _t5_bucket(d_w))
    far_min = min(FAR_TILE_DIST * QB - QB + 1, CMP_STRIDE * (CMP_PAD + 1) - (CMP_LEN - 1))
    max_exact = REL_BUCKETS // 2
    assert math.log(far_min / max_exact) / math.log(REL_MAX_DIST / max_exact) * max_exact > max_exact - 0.75
    return t_tab, w_tab, tbl[REL_BUCKETS - 1]


def _overlap_table(nc, nb):
    c_start = CMP_STRIDE * np.arange(nc)[:, None]
    s_start = SLC_LEN * np.arange(nb)[None, :]
    ov = ((c_start < s_start + SLC_LEN) & (c_start + CMP_LEN > s_start)).astype(np.float32)
    ov[nc - 1:] = 0.0
    out = np.zeros((nc + CMP_WIN, nb), np.float32)
    out[CMP_PAD:CMP_PAD + nc] = ov
    return jnp.asarray(out)


def _block_diag(w):
    nblk, bs, _ = w.shape
    eye = jnp.eye(nblk, dtype=w.dtype)
    return (eye[:, None, :, None] * w[:, :, None, :]).reshape(nblk * bs, nblk * bs)


def _layer(x, c8, w_ada, b_ada, norm1_w, w_in, cmp_pos, cmp_k_w1, cmp_k_w2, cmp_v_w1, cmp_v_w2, rel_table,
           conv_w, conv_b, lru_wa, lru_ba, lru_wx, lru_bx, lru_lambda, gnorm_attn_w, gnorm_rnn_w, w_out,
           norm2_w, w_router, b_router, w1, w3, w2, ws1, ws3, ws2, final_norm_w):
    B, S, D = x.shape
    N = B * S
    E = w_router.shape[1]
    d_attn = N_HEADS * HEAD_DIM
    d_rnn = D - d_attn
    assert S % 512 == 0 and S // SLC_LEN <= 128

    mod = _ada(c8, w_ada, b_ada[None, :])[:B].reshape(B, 6, D)

    n_kv_cols = 6 * N_KV * HEAD_DIM
    n_gate = N_BRANCH * N_HEADS
    wq, wkv, wg, wxr, wgr = jnp.split(w_in, np.cumsum([d_attn, n_kv_cols, n_gate, d_rnn]).tolist(), axis=1)
    wg = jnp.pad(wg.reshape(D, N_KV, GQA * N_BRANCH), ((0, 0), (0, 0), (0, 128 - GQA * N_BRANCH))).reshape(D, 256)
    w_pad = jnp.concatenate([wq, wkv, wg, wxr, wgr], axis=1).astype(BF16)
    q, kv, kvc, gates, xr, gr = _inproj(x, mod, norm1_w[None, :], w_pad)

    nc = S // CMP_STRIDE
    kvc16 = kvc.reshape(B, 4, nc, CMP_STRIDE * HEAD_DIM)
    pos2 = cmp_pos.reshape(2, CMP_STRIDE * HEAD_DIM)
    kvc_pad = _compress(kvc16, pos2, jnp.stack([cmp_k_w1, cmp_v_w1]), jnp.stack([cmp_k_w2, cmp_v_w2]))
    t_tab, w_tab, c31 = _bias_tables(rel_table)
    o_attn = _attn(c31, q, kv, kvc_pad, gates, t_tab, w_tab, _overlap_table(nc, 128))

    y_rnn = _rglru(xr, gr, conv_w.reshape(CONV_WIDTH, d_rnn), conv_b[None, :], _block_diag(lru_wa).astype(BF16),
                   lru_ba[None, :], _block_diag(lru_wx).astype(BF16), lru_bx[None, :], lru_lambda[None, :],
                   gnorm_rnn_w[None, :])

    x1, h2, s_t = _outproj(o_attn, y_rnn, x, mod, gnorm_attn_w[None, :], w_out.astype(BF16), norm2_w[None, :],
                           w_router.T)

    eid_t, wgt_t, rank_t, counts = _route(s_t, b_router[:, None])
    counts = counts[:, 0].astype(I32)
    off = jnp.cumsum(counts) - counts
    dest_t = _dest(eid_t, rank_t, off.astype(F32)[:, None])
    h2f = h2.reshape(N, D)
    xs = _dispatch(dest_t, h2f, N * TOP_K)
    ys = _experts(_expert_items(counts, N * TOP_K), xs, w1, w3, w2)
    out = _combine(dest_t, ys, wgt_t.T, h2f, x1.reshape(N, D), mod, ws1.astype(BF16), ws3.astype(BF16),
                   ws2.astype(BF16), final_norm_w[None, :], S)
    return out.reshape(B, S, D)


def kernel(x, c, w_ada, b_ada, norm1_w, w_in, cmp_pos, cmp_k_w1, cmp_k_w2, cmp_v_w1, cmp_v_w2, rel_table, conv_w, conv_b, lru_wa, lru_ba, lru_wx, lru_bx, lru_lambda, gnorm_attn_w, gnorm_rnn_w, w_out, norm2_w, w_router, b_router, w1, w3, w2, ws1, ws3, ws2, final_norm_w):
    assert w_ada.shape[0] == 1
    c8 = jnp.pad(c, ((0, 8 - c.shape[0]), (0, 0)))
    return _layer(x, c8, w_ada[0], b_ada[0], norm1_w[0], w_in[0], cmp_pos[0], cmp_k_w1[0], cmp_k_w2[0],
                  cmp_v_w1[0], cmp_v_w2[0], rel_table, conv_w[0], conv_b[0], lru_wa[0], lru_ba[0], lru_wx[0],
                  lru_bx[0], lru_lambda[0], gnorm_attn_w[0], gnorm_rnn_w[0], w_out[0], norm2_w[0], w_router[0],
                  b_router[0], w1[0], w3[0], w2[0], ws1[0], ws3[0], ws2[0], final_norm_w)
```

```python
import functools
import math

import jax
import jax.numpy as jnp
import numpy as np
from jax import lax
from jax.experimental import pallas as pl
from jax.experimental.pallas import tpu as pltpu

F32 = jnp.float32
BF16 = jnp.bfloat16
I32 = jnp.int32

HEAD_DIM = 64
N_HEADS = 8
N_KV = 2
GQA = N_HEADS // N_KV
N_BRANCH = 3
CONV_WIDTH = 4
LRU_C = 8.0
CMP_LEN = 32
CMP_STRIDE = 16
SLC_LEN = 64
SLC_TOP = 16
WINDOW = 512
QB = 128
REL_BUCKETS = 32
REL_MAX_DIST = 1024
N_EXPERT_GROUPS = 8
TOP_GROUPS = 4
TOP_K = 8
ROUTED_SCALE = 2.5
MOE_BLOCK = 128
EPS = 1e-6
NEG = -1e30
CMP_PAD = 120
CMP_WIN = 128
N_BIAS_TILES = 11
TILE_MASKED = N_BIAS_TILES
TILE_WINDOW_EDGE = N_BIAS_TILES + 1
FAR_TILE_DIST = 8
FAR_CHUNK = 4
VMEM_LIMIT = 52 * 1024 * 1024


def _dot(a, b, **kw):
    return jnp.dot(a, b, preferred_element_type=F32, **kw)


def _dot_nt(a, b, **kw):
    return lax.dot_general(a, b, (((1,), (1,)), ((), ())), preferred_element_type=F32, **kw)


def _gelu(x):
    return 0.5 * x * (1.0 + jnp.tanh(math.sqrt(2.0 / math.pi) * (x + 0.044715 * (x * x * x))))


def _sigmoid(x):
    return 1.0 / (1.0 + jnp.exp(-x))


def _rms(x):
    return x * lax.rsqrt(jnp.mean(x * x, axis=-1, keepdims=True) + EPS)


def _params(*sem):
    return pltpu.CompilerParams(dimension_semantics=sem, vmem_limit_bytes=VMEM_LIMIT)


def _ada_kernel(c_ref, w_ref, b_ref, o_ref):
    c = c_ref[...]
    a = c * _sigmoid(c)
    o_ref[...] = _dot(a, w_ref[...], precision=lax.Precision.HIGHEST) + b_ref[...]


def _ada(c8, w, b):
    d, n = w.shape
    tn = 1536
    return pl.pallas_call(
        _ada_kernel,
        grid=(n // tn,),
        in_specs=[pl.BlockSpec((8, d), lambda j: (0, 0)),
                  pl.BlockSpec((d, tn), lambda j: (0, j)),
                  pl.BlockSpec((1, tn), lambda j: (0, j))],
        out_specs=pl.BlockSpec((8, tn), lambda j: (0, j)),
        out_shape=jax.ShapeDtypeStruct((8, n), F32),
        compiler_params=_params("parallel"),
        name="ada",
    )(c8, w, b)


def _inproj_kernel(x_ref, mod_ref, nw_ref, w_ref, qt_ref, q_ref, kx_ref, vx_ref, kvc_ref, g_ref, xr_ref, gr_ref):
    h = _rms(x_ref[0]) * nw_ref[...]
    h = h * (1.0 + mod_ref[0, 1:2, :]) + mod_ref[0, 0:1, :]
    p = _dot(h.astype(BF16), w_ref[...])
    tm = p.shape[0]
    dq = N_HEADS * HEAD_DIM
    lane = lax.broadcasted_iota(I32, (tm, HEAD_DIM), 1)
    k_tail = jnp.where(lane < 2, 1.0, 0.0)
    v_tail = jnp.where(lane < 1, 1.0, 0.0)
    for hh in range(N_HEADS):
        qh = p[:, hh * HEAD_DIM:(hh + 1) * HEAD_DIM] * (HEAD_DIM ** -0.5)
        q_tail = jnp.broadcast_to(qt_ref[hh:hh + 1, :], (tm, HEAD_DIM))
        q_t = jnp.concatenate([qh, q_tail], axis=1).T.astype(BF16)
        for blk in range(tm // QB):
            col = (blk * GQA + hh % GQA) * QB
            q_ref[0, hh // GQA, :, col:col + QB] = q_t[:, blk * QB:(blk + 1) * QB]
    for j in range(6):
        for g in range(N_KV):
            col = dq + (j * N_KV + g) * HEAD_DIM
            piece = p[:, col:col + HEAD_DIM]
            if j < 2:
                kvc_ref[0, j * N_KV + g] = piece
            elif j % 2 == 0:
                kx_ref[0, (j // 2 - 1) * N_KV + g] = jnp.concatenate([piece, k_tail], axis=1).astype(BF16)
            else:
                v_t = jnp.concatenate([piece, v_tail], axis=1).T.astype(BF16)
                for blk in range(tm // QB):
                    vx_ref[0, (j // 2 - 1) * N_KV + g, blk] = v_t[:, blk * QB:(blk + 1) * QB]
    c0 = dq + 6 * N_KV * HEAD_DIM
    for g in range(N_KV):
        g_ref[0, g] = _sigmoid(p[:, c0 + g * 128:c0 + (g + 1) * 128]).T[0:16, :]
    xr_ref[0] = p[:, c0 + 256:c0 + 768]
    gr_ref[0] = p[:, c0 + 768:c0 + 1280]


def _inproj(x, mod, nw, w_pad, q_tail, tm=256):
    B, S, D = x.shape
    ncol = w_pad.shape[1]
    heads = lambda n, w: pl.BlockSpec((1, n, tm, w), lambda b, t: (b, 0, t, 0))
    return pl.pallas_call(
        _inproj_kernel,
        grid=(B, S // tm),
        in_specs=[pl.BlockSpec((1, tm, D), lambda b, t: (b, t, 0)),
                  pl.BlockSpec((1, 6, D), lambda b, t: (b, 0, 0)),
                  pl.BlockSpec((1, D), lambda b, t: (0, 0)),
                  pl.BlockSpec((D, ncol), lambda b, t: (0, 0)),
                  pl.BlockSpec(q_tail.shape, lambda b, t: (0, 0))],
        out_specs=[pl.BlockSpec((1, N_KV, 2 * HEAD_DIM, tm * GQA), lambda b, t: (b, 0, 0, t)),
                   heads(4, 2 * HEAD_DIM),
                   pl.BlockSpec((1, 4, tm // QB, 2 * HEAD_DIM, QB), lambda b, t: (b, 0, t, 0, 0)),
                   heads(4, HEAD_DIM),
                   pl.BlockSpec((1, N_KV, 16, tm), lambda b, t: (b, 0, 0, t)),
                   pl.BlockSpec((1, tm, 512), lambda b, t: (b, t, 0)),
                   pl.BlockSpec((1, tm, 512), lambda b, t: (b, t, 0))],
        out_shape=[jax.ShapeDtypeStruct((B, N_KV, 2 * HEAD_DIM, S * GQA), BF16),
                   jax.ShapeDtypeStruct((B, 4, S, 2 * HEAD_DIM), BF16),
                   jax.ShapeDtypeStruct((B, 4, S // QB, 2 * HEAD_DIM, QB), BF16),
                   jax.ShapeDtypeStruct((B, 4, S, HEAD_DIM), F32),
                   jax.ShapeDtypeStruct((B, N_KV, 16, S), F32),
                   jax.ShapeDtypeStruct((B, S, 512), F32),
                   jax.ShapeDtypeStruct((B, S, 512), F32)],
        compiler_params=_params("parallel", "parallel"),
        name="inproj",
    )(x, mod, nw, w_pad, q_tail)


def _compress_kernel(x_ref, pos_ref, w1_ref, w2_ref, o_ref, ot_ref):
    x = x_ref[0, 0]
    nc = x.shape[0]
    half = CMP_STRIDE * HEAD_DIM
    kv = pl.program_id(1) // N_KV
    a = _dot((x + pos_ref[0:1, :]).astype(BF16), w1_ref[kv, 0:half, :].astype(BF16))
    b = _dot((x + pos_ref[1:2, :]).astype(BF16), w1_ref[kv, half:2 * half, :].astype(BF16))
    hid = _gelu(a + pltpu.roll(b, nc - 1, 0))
    out = _dot(hid.astype(BF16), w2_ref[kv].astype(BF16))
    row = lax.broadcasted_iota(I32, out.shape, 0)
    out = jnp.where(row < nc - 1, out, 0.0)
    lane = lax.broadcasted_iota(I32, (nc, HEAD_DIM), 1)
    out = jnp.concatenate([out, jnp.where(lane < 2, 1.0, 0.0)], axis=1)
    o_ref[0, 0, 0:CMP_PAD, :] = jnp.zeros((CMP_PAD, 2 * HEAD_DIM), F32)
    o_ref[0, 0, CMP_PAD:CMP_PAD + nc, :] = out
    o_ref[0, 0, CMP_PAD + nc:, :] = jnp.zeros((CMP_WIN - CMP_PAD, 2 * HEAD_DIM), F32)
    ot_ref[0, 0] = out.T


def _compress(kvc, pos2, w1, w2):
    B, _, NC, W = kvc.shape
    return pl.pallas_call(
        _compress_kernel,
        grid=(B, 4),
        in_specs=[pl.BlockSpec((1, 1, NC, W), lambda b, i: (b, i, 0, 0)),
                  pl.BlockSpec((2, W), lambda b, i: (0, 0)),
                  pl.BlockSpec((2, 2 * W, HEAD_DIM), lambda b, i: (0, 0, 0)),
                  pl.BlockSpec((2, HEAD_DIM, HEAD_DIM), lambda b, i: (0, 0, 0))],
        out_specs=[pl.BlockSpec((1, 1, NC + CMP_WIN, 2 * HEAD_DIM), lambda b, i: (b, i, 0, 0)),
                   pl.BlockSpec((1, 1, 2 * HEAD_DIM, NC), lambda b, i: (b, i, 0, 0))],
        out_shape=[jax.ShapeDtypeStruct((B, 4, NC + CMP_WIN, 2 * HEAD_DIM), F32),
                   jax.ShapeDtypeStruct((B, 4, 2 * HEAD_DIM, NC), F32)],
        compiler_params=_params("parallel", "parallel"),
        name="compress",
    )(kvc, pos2, w1, w2)


def _attn_kernel(q_ref, ks_ref, vs_ref, kw_ref, vw_ref, kc_ref, vc_ref, vct_ref, g_ref, t_ref, w_ref, ovt_ref, ov_ref,
                 o_ref, m_scr, acc_scr, sel_scr):
    qb = pl.program_id(2)
    R = GQA * QB
    q = q_ref[0, 0]
    nc = vct_ref.shape[3]
    nb = ov_ref.shape[1]
    n_top = min(SLC_TOP, ks_ref.shape[2] // SLC_LEN)

    def heads_sum(x):
        out = x[:, 0:QB]
        for hq in range(1, GQA):
            out = out + x[:, hq * QB:(hq + 1) * QB]
        return out

    w0 = pl.multiple_of(qb * 8, 8)
    s_far = _dot(kc_ref[0, 0, CMP_PAD:CMP_PAD + nc, :].astype(BF16), q)
    n_io = lax.broadcasted_iota(I32, (nc, R), 0)
    s_far = jnp.where(n_io < qb * 8 - CMP_PAD, s_far, -jnp.inf)
    s_win = _dot(kc_ref[0, 0, pl.ds(w0, CMP_WIN), :].astype(BF16), q) + w_ref[0]
    j_w = lax.broadcasted_iota(I32, (CMP_WIN, R), 0)
    i_w = lax.broadcasted_iota(I32, (CMP_WIN, R), 1) & (QB - 1)
    dist_w = i_w - CMP_STRIDE * (j_w - CMP_PAD) - (CMP_LEN - 1)
    s_win = jnp.where((dist_w >= 0) & (j_w >= CMP_PAD - qb * 8), s_win, -jnp.inf)
    m = jnp.maximum(jnp.max(s_far, axis=0, keepdims=True), jnp.max(s_win, axis=0, keepdims=True))
    m = jnp.where(m == -jnp.inf, 0.0, m)
    e_far = jnp.exp(s_far - m)
    e_win = jnp.exp(s_win - m)
    l = jnp.sum(e_far, axis=0, keepdims=True) + jnp.sum(e_win, axis=0, keepdims=True)
    inv = 1.0 / jnp.maximum(l, 1e-30)
    p_far = e_far * inv
    p_win = e_win * inv
    vcw_t = vc_ref[0, 0, pl.ds(w0, CMP_WIN), :].T
    o_c = _dot(vct_ref[0, 0].astype(BF16), p_far.astype(BF16)) + _dot(vcw_t.astype(BF16), p_win.astype(BF16))
    ov_win_t = ov_ref[pl.ds(w0, CMP_WIN), :].T
    sc_t = (_dot(ovt_ref[...], heads_sum(p_far).astype(BF16))
            + _dot(ov_win_t.astype(BF16), heads_sum(p_win).astype(BF16)))

    blk = lax.broadcasted_iota(I32, (nb, QB), 0)
    t_io = lax.broadcasted_iota(I32, (nb, QB), 1)
    cur = qb * (QB // SLC_LEN) + t_io // SLC_LEN
    forced = (blk == 0) | (blk == cur) | (blk == cur - 1)
    val = jnp.where(forced, jnp.inf, jnp.where(blk <= cur, sc_t, -jnp.inf))
    sel_t = jnp.zeros((nb, QB), F32)
    for _ in range(n_top):
        mx = jnp.max(val, axis=0, keepdims=True)
        hit = (val == mx) & (val > -jnp.inf)
        first = jnp.min(jnp.where(hit, blk, nb), axis=0, keepdims=True)
        pick = blk == first
        sel_t = jnp.where(pick, 1.0, sel_t)
        val = jnp.where(pick, -jnp.inf, val)
    sel_scr[...] = sel_t

    m_scr[...] = jnp.full(m_scr.shape, NEG, F32)
    acc_scr[...] = jnp.zeros(acc_scr.shape, F32)
    blocks_per_chunk = FAR_CHUNK * QB // SLC_LEN

    def update(c, bias):
        s = _dot(ks_ref[0, 0, pl.ds(pl.multiple_of(c * FAR_CHUNK * QB, FAR_CHUNK * QB), FAR_CHUNK * QB), :], q)
        if bias is not None:
            s = s + bias
        rows = sel_scr[pl.ds(pl.multiple_of(c * blocks_per_chunk, blocks_per_chunk), blocks_per_chunk), :]
        keep = jnp.concatenate([jnp.broadcast_to(rows[r:r + 1, :], (SLC_LEN, QB)) for r in range(blocks_per_chunk)],
                               axis=0)
        s = jnp.where(jnp.concatenate([keep] * GQA, axis=1) > 0.5, s, NEG)
        m_old = m_scr[...]
        m_new = jnp.maximum(m_old, jnp.max(s, axis=0, keepdims=True))
        p = jnp.exp(s - m_new)
        v_t = jnp.concatenate([vs_ref[0, 0, c * FAR_CHUNK + j] for j in range(FAR_CHUNK)], axis=1)
        acc_scr[...] = jnp.exp(m_old - m_new) * acc_scr[...] + _dot(v_t, p.astype(BF16))
        m_scr[...] = m_new

    def far_body(c, carry):
        update(c, None)
        return carry

    def near_body(c, carry):
        tiles = []
        for j in range(FAR_CHUNK):
            delta = qb - (c * FAR_CHUNK + j)
            tiles.append(t_ref[0, jnp.where(delta < 0, TILE_MASKED, delta)])
        update(c, jnp.concatenate(tiles, axis=0))
        return carry

    n_far = jnp.maximum(qb - (FAR_TILE_DIST - 1), 0) // FAR_CHUNK
    lax.fori_loop(0, n_far, far_body, 0)
    lax.fori_loop(n_far, qb // FAR_CHUNK + 1, near_body, 0)
    acc = acc_scr[...]
    o_s = acc[0:HEAD_DIM, :] / acc[HEAD_DIM:HEAD_DIM + 1, :]

    n_wt = WINDOW // QB + 1
    k_t, v_t, b_t = [], [], []
    for j in range(n_wt):
        kb = qb - (n_wt - 1) + j
        kb0 = jnp.maximum(kb, 0)
        k_t.append(kw_ref[0, 0, pl.ds(pl.multiple_of(kb0 * QB, QB), QB), :])
        v_t.append(vw_ref[0, 0, kb0])
        b_t.append(t_ref[0, jnp.where(kb < 0, TILE_MASKED, TILE_WINDOW_EDGE if j == 0 else n_wt - 1 - j)])
    s = _dot(jnp.concatenate(k_t, axis=0), q) + jnp.concatenate(b_t, axis=0)
    p = jnp.exp(s - jnp.max(s, axis=0, keepdims=True))
    acc = _dot(jnp.concatenate(v_t, axis=1), p.astype(BF16))
    o_w = acc[0:HEAD_DIM, :] / acc[HEAD_DIM:HEAD_DIM + 1, :]

    gates = g_ref[0, 0]
    outs = []
    for hq in range(GQA):
        cols = slice(hq * QB, (hq + 1) * QB)
        c = hq * N_BRANCH
        outs.append(gates[c:c + 1, :] * o_c[0:HEAD_DIM, cols] + gates[c + 1:c + 2, :] * o_s[:, cols]
                    + gates[c + 2:c + 3, :] * o_w[:, cols])
    o_ref[0] = jnp.concatenate(outs, axis=0).T


def _attn(q_t, kx, vx_t, kvc, kvc_t, gates_t, t_tab, w_tab, ov_t, ov):
    B, _, W, _ = q_t.shape
    S = kx.shape[2]
    ncp = kvc.shape[2]
    nb = ov.shape[1]
    keys = lambda j: pl.BlockSpec((1, 1, S, W), lambda b, g, t: (b, j * N_KV + g, 0, 0))
    vals = lambda j: pl.BlockSpec((1, 1, S // QB, W, QB), lambda b, g, t: (b, j * N_KV + g, 0, 0, 0))
    cmp = lambda j: pl.BlockSpec((1, 1, ncp, W), lambda b, g, t: (b, j * N_KV + g, 0, 0))
    return pl.pallas_call(
        _attn_kernel,
        grid=(B, N_KV, S // QB),
        in_specs=[pl.BlockSpec((1, 1, W, GQA * QB), lambda b, g, t: (b, g, 0, t)),
                  keys(0), vals(0), keys(1), vals(1), cmp(0), cmp(1),
                  pl.BlockSpec((1, 1, W, ncp - CMP_WIN), lambda b, g, t: (b, N_KV + g, 0, 0)),
                  pl.BlockSpec((1, 1, 16, QB), lambda b, g, t: (b, g, 0, t)),
                  pl.BlockSpec((1,) + t_tab.shape[1:], lambda b, g, t: (g, 0, 0, 0)),
                  pl.BlockSpec((1, CMP_WIN, GQA * QB), lambda b, g, t: (g, 0, 0)),
                  pl.BlockSpec(ov_t.shape, lambda b, g, t: (0, 0)),
                  pl.BlockSpec(ov.shape, lambda b, g, t: (0, 0))],
        out_specs=pl.BlockSpec((1, QB, GQA * HEAD_DIM), lambda b, g, t: (b, t, g)),
        out_shape=jax.ShapeDtypeStruct((B, S, N_HEADS * HEAD_DIM), F32),
        scratch_shapes=[pltpu.VMEM((1, GQA * QB), F32), pltpu.VMEM((W, GQA * QB), F32), pltpu.VMEM((nb, QB), F32)],
        compiler_params=_params("parallel", "parallel", "arbitrary"),
        name="attn",
    )(q_t, kx, vx_t, kx, vx_t, kvc, kvc, kvc_t, gates_t, t_tab, w_tab, ov_t, ov)


def _rglru_kernel(xr_ref, gr_ref, cw_ref, cb_ref, wa_ref, ba_ref, wx_ref, bx_ref, lam_ref, gw_ref, o_ref,
                  xbuf, hprev, a_scr, u_scr, h_scr):
    ts = xr_ref.shape[1]
    C = xr_ref.shape[2]

    @pl.when(pl.program_id(1) == 0)
    def _():
        xbuf[0:8, :] = jnp.zeros((8, C), F32)
        hprev[...] = jnp.zeros(hprev.shape, F32)

    xbuf[8:8 + ts, :] = xr_ref[0]
    xc = cb_ref[...] + jnp.zeros((ts, C), F32)
    for j in range(CONV_WIDTH):
        xc = xc + cw_ref[j:j + 1, :] * xbuf[pl.ds(8 - (CONV_WIDTH - 1) + j, ts), :]
    xbuf[0:8, :] = xbuf[ts:ts + 8, :]

    xcb = xc.astype(BF16)
    r = _sigmoid(_dot(xcb, wa_ref[...]) + ba_ref[...])
    i = _sigmoid(_dot(xcb, wx_ref[...]) + bx_ref[...])
    z = -lam_ref[...]
    softplus = jnp.maximum(z, 0.0) + jnp.log(1.0 + jnp.exp(-jnp.abs(z)))
    log_a = -LRU_C * r * softplus
    a_scr[...] = jnp.exp(log_a)
    u_scr[...] = jnp.sqrt(1.0 - jnp.exp(2.0 * log_a)) * (i * xc)

    row = lax.broadcasted_iota(I32, (8, C), 0)

    def body(k, h):
        r0 = pl.multiple_of(k * 8, 8)
        a = a_scr[pl.ds(r0, 8), :]
        b = u_scr[pl.ds(r0, 8), :]
        for s in (1, 2, 4):
            keep = row >= s
            b = jnp.where(keep, a * pltpu.roll(b, s, 0) + b, b)
            a = jnp.where(keep, a * pltpu.roll(a, s, 0), a)
        hh = a * h + b
        h_scr[pl.ds(r0, 8), :] = hh
        return jnp.broadcast_to(hh[7:8, :], (8, C))

    hprev[...] = lax.fori_loop(0, ts // 8, body, hprev[...])
    out = h_scr[...] * _gelu(gr_ref[0])
    o_ref[0] = (_rms(out) * gw_ref[...]).astype(BF16)


def _rglru(xr, gr, cw, cb, wa_bd, ba, wx_bd, bx, lam, gw, ts=512):
    B, S, C = xr.shape
    ts = min(ts, S)
    vec = pl.BlockSpec((1, C), lambda b, t: (0, 0))
    mat = pl.BlockSpec((C, C), lambda b, t: (0, 0))
    seq = pl.BlockSpec((1, ts, C), lambda b, t: (b, t, 0))
    return pl.pallas_call(
        _rglru_kernel,
        grid=(B, S // ts),
        in_specs=[seq, seq, pl.BlockSpec((CONV_WIDTH, C), lambda b, t: (0, 0)), vec, mat, vec, mat, vec, vec, vec],
        out_specs=seq,
        out_shape=jax.ShapeDtypeStruct((B, S, C), BF16),
        scratch_shapes=[pltpu.VMEM((ts + 8, C), F32), pltpu.VMEM((8, C), F32), pltpu.VMEM((ts, C), F32),
                        pltpu.VMEM((ts, C), F32), pltpu.VMEM((ts, C), F32)],
        compiler_params=_params("parallel", "arbitrary"),
        name="rglru",
    )(xr, gr, cw, cb, wa_bd, ba, wx_bd, bx, lam, gw)


def _outproj_kernel(oa_ref, yr_ref, x_ref, mod_ref, gaw_ref, wo_ref, n2_ref, wr_ref, x1_ref, h2_ref, st_ref):
    da = oa_ref.shape[2]
    ya = (_rms(oa_ref[0]) * gaw_ref[...]).astype(BF16)
    mix = _dot(ya, wo_ref[0:da, :]) + _dot(yr_ref[0], wo_ref[da:, :])
    x1 = x_ref[0] + mod_ref[0, 2:3, :] * mix
    x1_ref[0] = x1
    h2 = (_rms(x1) * n2_ref[...]) * (1.0 + mod_ref[0, 4:5, :]) + mod_ref[0, 3:4, :]
    h2_ref[0] = h2
    st_ref[...] = _sigmoid(_dot_nt(wr_ref[...], h2, precision=lax.Precision.HIGHEST))


def _outproj(oa, yr, x, mod, gaw, wo, n2, wr_t, tm=256):
    B, S, D = x.shape
    E = wr_t.shape[0]
    nt = S // tm
    row = lambda w: pl.BlockSpec((1, tm, w), lambda b, t: (b, t, 0))
    return pl.pallas_call(
        _outproj_kernel,
        grid=(B, nt),
        in_specs=[row(oa.shape[2]), row(yr.shape[2]), row(D),
                  pl.BlockSpec((1, 6, D), lambda b, t: (b, 0, 0)),
                  pl.BlockSpec((1, oa.shape[2]), lambda b, t: (0, 0)),
                  pl.BlockSpec(wo.shape, lambda b, t: (0, 0)),
                  pl.BlockSpec((1, D), lambda b, t: (0, 0)),
                  pl.BlockSpec((E, D), lambda b, t: (0, 0))],
        out_specs=[row(D), row(D), pl.BlockSpec((E, tm), lambda b, t: (0, b * nt + t))],
        out_shape=[jax.ShapeDtypeStruct((B, S, D), F32), jax.ShapeDtypeStruct((B, S, D), F32),
                   jax.ShapeDtypeStruct((E, B * S), F32)],
        compiler_params=_params("parallel", "parallel"),
        name="outproj",
    )(oa, yr, x, mod, gaw, wo, n2, wr_t)


def _route_kernel(s_ref, b_ref, eid_ref, wgt_ref, rank_ref, cnt_ref, carry):
    E, tn = s_ref.shape
    per = E // N_EXPERT_GROUPS

    @pl.when(pl.program_id(0) == 0)
    def _():
        carry[...] = jnp.zeros(carry.shape, F32)

    s = s_ref[...]
    s_sel = s + b_ref[...]
    eidx = lax.broadcasted_iota(I32, (E, tn), 0)

    grp = []
    for gi in range(N_EXPERT_GROUPS):
        xg = s_sel[gi * per:(gi + 1) * per, :]
        ig = lax.broadcasted_iota(I32, (per, tn), 0)
        m1 = jnp.max(xg, axis=0, keepdims=True)
        f1 = jnp.min(jnp.where(xg == m1, ig, per), axis=0, keepdims=True)
        m2 = jnp.max(jnp.where(ig == f1, -jnp.inf, xg), axis=0, keepdims=True)
        grp.append(m1 + m2)
    val = []
    for gi in range(N_EXPERT_GROUPS):
        rank = jnp.zeros((1, tn), I32)
        for gj in range(N_EXPERT_GROUPS):
            if gj == gi:
                continue
            ahead = (grp[gj] > grp[gi]) | ((grp[gj] == grp[gi]) & (gj < gi))
            rank = rank + ahead.astype(I32)
        val.append(jnp.where(rank < TOP_GROUPS, s_sel[gi * per:(gi + 1) * per, :], -jnp.inf))
    val = jnp.concatenate(val, axis=0)

    eids, tops = [], []
    hot = jnp.zeros((E, tn), F32)
    for _ in range(TOP_K):
        mx = jnp.max(val, axis=0, keepdims=True)
        first = jnp.min(jnp.where(val == mx, eidx, E), axis=0, keepdims=True)
        pick = eidx == first
        eids.append(first)
        tops.append(jnp.sum(jnp.where(pick, s, 0.0), axis=0, keepdims=True))
        hot = jnp.where(pick, 1.0, hot)
        val = jnp.where(pick, -jnp.inf, val)
    denom = tops[0]
    for t in tops[1:]:
        denom = denom + t

    ti = lax.broadcasted_iota(I32, (tn, tn), 0)
    tj = lax.broadcasted_iota(I32, (tn, tn), 1)
    upper = jnp.where(ti < tj, 1.0, 0.0).astype(BF16)
    before = _dot(hot.astype(BF16), upper) + carry[...]
    for k in range(TOP_K):
        eid_ref[k:k + 1, :] = eids[k]
        wgt_ref[k:k + 1, :] = ROUTED_SCALE * tops[k] / denom
        rank_ref[k:k + 1, :] = jnp.sum(jnp.where(eidx == eids[k], before, 0.0), axis=0, keepdims=True).astype(I32)
    carry[...] = carry[...] + jnp.sum(hot, axis=1, keepdims=True)
    cnt_ref[...] = carry[...]


def _route(s_t, b_col, tn=256):
    E, N = s_t.shape
    tn = min(tn, N)
    out = pl.BlockSpec((TOP_K, tn), lambda t: (0, t))
    return pl.pallas_call(
        _route_kernel,
        grid=(N // tn,),
        in_specs=[pl.BlockSpec((E, tn), lambda t: (0, t)), pl.BlockSpec((E, 1), lambda t: (0, 0))],
        out_specs=[out, out, out, pl.BlockSpec((E, 1), lambda t: (0, 0))],
        out_shape=[jax.ShapeDtypeStruct((TOP_K, N), I32), jax.ShapeDtypeStruct((TOP_K, N), F32),
                   jax.ShapeDtypeStruct((TOP_K, N), I32), jax.ShapeDtypeStruct((E, 1), F32)],
        scratch_shapes=[pltpu.VMEM((E, 1), F32)],
        compiler_params=_params("arbitrary"),
        name="route",
    )(s_t, b_col)


def _dest_kernel(eid_ref, rank_ref, off_ref, o_ref):
    E = off_ref.shape[0]
    tn = eid_ref.shape[1]
    eidx = lax.broadcasted_iota(I32, (E, tn), 0)
    for k in range(TOP_K):
        base = jnp.sum(jnp.where(eidx == eid_ref[k:k + 1, :], off_ref[...], 0.0), axis=0, keepdims=True)
        o_ref[k:k + 1, :] = base.astype(I32) + rank_ref[k:k + 1, :]


def _dest(eid_t, rank_t, off_col, tn=512):
    K, N = eid_t.shape
    tn = min(tn, N)
    blk = pl.BlockSpec((K, tn), lambda t: (0, t))
    return pl.pallas_call(
        _dest_kernel,
        grid=(N // tn,),
        in_specs=[blk, blk, pl.BlockSpec(off_col.shape, lambda t: (0, 0))],
        out_specs=blk,
        out_shape=jax.ShapeDtypeStruct((K, N), I32),
        compiler_params=_params("parallel"),
        name="dest",
    )(eid_t, rank_t, off_col)


def _dispatch_kernel(dest_ref, h_ref, xs_hbm, sem):
    tn = h_ref.shape[0]

    def go(t, c):
        for k in range(TOP_K):
            pltpu.make_async_copy(h_ref.at[pl.ds(t, 1)], xs_hbm.at[pl.ds(dest_ref[k, t], 1)], sem.at[0]).start()
        return c
    lax.fori_loop(0, tn, go, 0)
    for k in range(TOP_K):
        pltpu.make_async_copy(h_ref, xs_hbm.at[pl.ds(0, tn)], sem.at[0]).wait()


def _dispatch(dest_t, h2, n_slots, tn=256):
    N, D = h2.shape
    tn = min(tn, N)
    return pl.pallas_call(
        _dispatch_kernel,
        grid=(N // tn,),
        in_specs=[pl.BlockSpec((TOP_K, tn), lambda t: (0, t), memory_space=pltpu.SMEM),
                  pl.BlockSpec((tn, D), lambda t: (t, 0))],
        out_specs=pl.BlockSpec(memory_space=pl.ANY),
        out_shape=jax.ShapeDtypeStruct((n_slots, D), F32),
        scratch_shapes=[pltpu.SemaphoreType.DMA((1,))],
        compiler_params=_params("arbitrary"),
        name="dispatch",
    )(dest_t, h2)


def _experts_kernel(rb_ref, e_ref, lo_ref, hi_ref, first_ref, x_ref, w1_ref, w3_ref, w2_ref, y_ref):
    i = pl.program_id(0)
    lo = lo_ref[i]
    hi = hi_ref[i]

    @pl.when(hi > lo)
    def _():
        x = x_ref[...].astype(BF16)
        h1 = _dot(x, w1_ref[0].astype(BF16))
        h3 = _dot(x, w3_ref[0].astype(BF16))
        hid = (h1 * _sigmoid(h1)) * h3
        y = _dot(hid.astype(BF16), w2_ref[0].astype(BF16))
        row = lax.broadcasted_iota(I32, y.shape, 0)
        mine = (row >= lo) & (row < hi)

        @pl.when(first_ref[i] == 1)
        def _():
            y_ref[...] = jnp.where(mine, y, 0.0)

        @pl.when(first_ref[i] == 0)
        def _():
            y_ref[...] = jnp.where(mine, y, y_ref[...])


def _experts(items, xs, w1, w3, w2):
    NK, D = xs.shape
    De = w1.shape[2]
    rows = pl.BlockSpec((MOE_BLOCK, D), lambda i, rb, e, lo, hi, first: (rb[i], 0))
    wspec = lambda shape: pl.BlockSpec(shape, lambda i, rb, e, lo, hi, first: (e[i], 0, 0))
    grid_spec = pltpu.PrefetchScalarGridSpec(
        num_scalar_prefetch=5,
        grid=(items[0].shape[0],),
        in_specs=[rows, wspec((1, D, De)), wspec((1, D, De)), wspec((1, De, D))],
        out_specs=rows,
    )
    return pl.pallas_call(
        _experts_kernel,
        grid_spec=grid_spec,
        out_shape=jax.ShapeDtypeStruct((NK, D), F32),
        compiler_params=_params("arbitrary"),
        name="experts",
    )(*items, xs, w1, w3, w2)


def _expert_items(counts, n_rows):
    E = counts.shape[0]
    n_rb = n_rows // MOE_BLOCK
    n_items = n_rb + E
    ce = jnp.cumsum(counts)
    cs = ce - counts
    first_blk = cs // MOE_BLOCK
    nb = jnp.where(counts > 0, (ce - 1) // MOE_BLOCK - first_blk + 1, 0)
    io_end = jnp.cumsum(nb)
    w = jnp.arange(n_items, dtype=I32)
    valid = w < io_end[-1]
    e_w = jnp.minimum(jnp.sum(io_end[None, :] <= w[:, None], axis=1), E - 1).astype(I32)
    rb = jnp.where(valid, first_blk[e_w] + w - (io_end - nb)[e_w], n_rb - 1).astype(I32)
    lo = jnp.where(valid, jnp.clip(cs[e_w] - MOE_BLOCK * rb, 0, MOE_BLOCK), 0).astype(I32)
    hi = jnp.where(valid, jnp.clip(ce[e_w] - MOE_BLOCK * rb, 0, MOE_BLOCK), 0).astype(I32)
    first = jnp.concatenate([jnp.ones((1,), I32), (rb[1:] != rb[:-1]).astype(I32)])
    return rb, e_w, lo, hi, first


def _combine_kernel(dest_ref, ys_hbm, w_ref, h2_ref, x1_ref, mod_ref, ws1_ref, ws3_ref, ws2_ref, fw_ref, o_ref,
                    ybuf, sem):
    tm = h2_ref.shape[0]

    def go(t, c):
        for k in range(TOP_K):
            pltpu.make_async_copy(ys_hbm.at[pl.ds(dest_ref[k, t], 1)], ybuf.at[k, pl.ds(t, 1)], sem.at[0]).start()
        return c
    lax.fori_loop(0, tm, go, 0)

    h = h2_ref[...].astype(BF16)
    a = _dot(h, ws1_ref[...])
    hid = (a * _sigmoid(a)) * _dot(h, ws3_ref[...])
    shared = _dot(hid.astype(BF16), ws2_ref[...])

    for k in range(TOP_K):
        pltpu.make_async_copy(ys_hbm.at[pl.ds(0, tm)], ybuf.at[k], sem.at[0]).wait()
    routed = w_ref[:, 0:1] * ybuf[0]
    for k in range(1, TOP_K):
        routed = routed + w_ref[:, k:k + 1] * ybuf[k]
    x2 = x1_ref[...] + mod_ref[0, 5:6, :] * (routed + shared)
    o_ref[...] = _rms(x2) * fw_ref[...]


def _combine(dest_t, ys, wgt, h2, x1, mod, ws1, ws3, ws2, fw, seq, tm=256):
    N, D = h2.shape
    tm = min(tm, seq)
    nt = seq // tm
    row = pl.BlockSpec((tm, D), lambda t: (t, 0))
    full = lambda a: pl.BlockSpec(a.shape, lambda t: (0,) * a.ndim)
    return pl.pallas_call(
        _combine_kernel,
        grid=(N // tm,),
        in_specs=[pl.BlockSpec((TOP_K, tm), lambda t: (0, t), memory_space=pltpu.SMEM),
                  pl.BlockSpec(memory_space=pl.ANY),
                  pl.BlockSpec((tm, TOP_K), lambda t: (t, 0)),
                  row, row,
                  pl.BlockSpec((1, 6, D), lambda t: (t // nt, 0, 0)),
                  full(ws1), full(ws3), full(ws2), full(fw)],
        out_specs=row,
        out_shape=jax.ShapeDtypeStruct((N, D), F32),
        scratch_shapes=[pltpu.VMEM((TOP_K, tm, D), F32), pltpu.SemaphoreType.DMA((1,))],
        compiler_params=_params("arbitrary"),
        name="combine",
    )(dest_t, ys, wgt, h2, x1, mod, ws1, ws3, ws2, fw)


def _t5_bucket(dist):
    n = jnp.maximum(dist, 0)
    max_exact = REL_BUCKETS // 2
    nf = jnp.maximum(n, 1).astype(F32)
    large = max_exact + (jnp.log(nf / max_exact) / math.log(REL_MAX_DIST / max_exact)
                         * (REL_BUCKETS - max_exact)).astype(I32)
    large = jnp.minimum(large, REL_BUCKETS - 1)
    return jnp.where(n < max_exact, n, large)


def _bias_kernel(tbl_ref, far_ref, bt_ref, bw_ref, t_ref, w_ref):
    g = pl.program_id(0)
    d = pl.program_id(1)

    def lookup(bkt, head):
        out = jnp.full(bkt.shape, tbl_ref[0, head], F32)
        for b in range(1, REL_BUCKETS):
            out = jnp.where(bkt == b, tbl_ref[b, head], out)
        return out - far_ref[head]

    j = lax.broadcasted_iota(I32, (QB, QB), 0)
    i = lax.broadcasted_iota(I32, (QB, QB), 1)
    keep = ((d != 0) | (i >= j)) & (d != TILE_MASKED) & ((d != TILE_WINDOW_EDGE) | (j > i))
    for hq in range(GQA):
        t_ref[0, 0, :, hq * QB:(hq + 1) * QB] = jnp.where(keep, lookup(bt_ref[0], GQA * g + hq), NEG)

    @pl.when(d == 0)
    def _():
        for hq in range(GQA):
            w_ref[0, :, hq * QB:(hq + 1) * QB] = lookup(bw_ref[...], GQA * g + hq)


def _bias_tables(rel_table):
    tbl = rel_table.astype(F32)
    far_hi = tbl[REL_BUCKETS - 1].astype(BF16)
    far_lo = (tbl[REL_BUCKETS - 1] - far_hi.astype(F32)).astype(BF16)
    far = far_hi.astype(F32) + far_lo.astype(F32)
    q_tail = jnp.zeros((N_HEADS, HEAD_DIM), F32).at[:, 0].set(far_hi.astype(F32)).at[:, 1].set(far_lo.astype(F32))
    i = jnp.arange(QB)
    tile_dist = np.array(list(range(N_BIAS_TILES)) + [0, WINDOW // QB])
    d_t = (QB * jnp.asarray(tile_dist)[:, None, None] + i[None, None, :] - i[None, :, None])
    d_w = i[None, :] - CMP_STRIDE * (jnp.arange(CMP_WIN)[:, None] - CMP_PAD) - (CMP_LEN - 1)
    n_tiles = len(tile_dist)
    t_tab, w_tab = pl.pallas_call(
        _bias_kernel,
        grid=(N_KV, n_tiles),
        in_specs=[pl.BlockSpec(memory_space=pltpu.SMEM),
                  pl.BlockSpec(memory_space=pltpu.SMEM),
                  pl.BlockSpec((1, QB, QB), lambda g, d: (d, 0, 0)),
                  pl.BlockSpec((CMP_WIN, QB), lambda g, d: (0, 0))],
        out_specs=[pl.BlockSpec((1, 1, QB, GQA * QB), lambda g, d: (g, d, 0, 0)),
                   pl.BlockSpec((1, CMP_WIN, GQA * QB), lambda g, d: (g, 0, 0))],
        out_shape=[jax.ShapeDtypeStruct((N_KV, n_tiles, QB, GQA * QB), F32),
                   jax.ShapeDtypeStruct((N_KV, CMP_WIN, GQA * QB), F32)],
        compiler_params=_params("parallel", "arbitrary"),
        name="bias",
    )(tbl, far, _t5_bucket(d_t), _t5_bucket(d_w))
    far_min = min(FAR_TILE_DIST * QB - QB + 1, CMP_STRIDE * (CMP_PAD + 1) - (CMP_LEN - 1))
    max_exact = REL_BUCKETS // 2
    assert math.log(far_min / max_exact) / math.log(REL_MAX_DIST / max_exact) * max_exact > max_exact - 0.75
    return t_tab, w_tab, q_tail


def _overlap_tables(nc, nb):
    c_start = CMP_STRIDE * np.arange(nc)[:, None]
    s_start = SLC_LEN * np.arange(nb)[None, :]
    ov = ((c_start < s_start + SLC_LEN) & (c_start + CMP_LEN > s_start)).astype(np.float32)
    ov[nc - 1:] = 0.0
    out = np.zeros((nc + CMP_WIN, nb), np.float32)
    out[CMP_PAD:CMP_PAD + nc] = ov
    return jnp.asarray(ov.T, dtype=BF16), jnp.asarray(out)


def _block_diag(w):
    nblk, bs, _ = w.shape
    eye = jnp.eye(nblk, dtype=w.dtype)
    return (eye[:, None, :, None] * w[:, :, None, :]).reshape(nblk * bs, nblk * bs)


def _layer(x, c8, w_ada, b_ada, norm1_w, w_in, cmp_pos, cmp_k_w1, cmp_k_w2, cmp_v_w1, cmp_v_w2, rel_table,
           conv_w, conv_b, lru_wa, lru_ba, lru_wx, lru_bx, lru_lambda, gnorm_attn_w, gnorm_rnn_w, w_out,
           norm2_w, w_router, b_router, w1, w3, w2, ws1, ws3, ws2, final_norm_w):
    B, S, D = x.shape
    N = B * S
    E = w_router.shape[1]
    d_attn = N_HEADS * HEAD_DIM
    d_rnn = D - d_attn
    assert S % 512 == 0 and S // SLC_LEN <= 128

    mod = _ada(c8, w_ada, b_ada[None, :])[:B].reshape(B, 6, D)

    n_kv_cols = 6 * N_KV * HEAD_DIM
    n_gate = N_BRANCH * N_HEADS
    wq, wkv, wg, wxr, wgr = jnp.split(w_in, np.cumsum([d_attn, n_kv_cols, n_gate, d_rnn]).tolist(), axis=1)
    wg = jnp.pad(wg.reshape(D, N_KV, GQA * N_BRANCH), ((0, 0), (0, 0), (0, 128 - GQA * N_BRANCH))).reshape(D, 256)
    w_pad = jnp.concatenate([wq, wkv, wg, wxr, wgr], axis=1).astype(BF16)
    t_tab, w_tab, q_tail = _bias_tables(rel_table)
    q_t, kx, vx_t, kvc, gates_t, xr, gr = _inproj(x, mod, norm1_w[None, :], w_pad, q_tail)

    nc = S // CMP_STRIDE
    kvc16 = kvc.reshape(B, 4, nc, CMP_STRIDE * HEAD_DIM)
    pos2 = cmp_pos.reshape(2, CMP_STRIDE * HEAD_DIM)
    kvc_pad, kvc_t = _compress(kvc16, pos2, jnp.stack([cmp_k_w1, cmp_v_w1]), jnp.stack([cmp_k_w2, cmp_v_w2]))
    o_attn = _attn(q_t, kx, vx_t, kvc_pad, kvc_t, gates_t, t_tab, w_tab, *_overlap_tables(nc, 128))

    y_rnn = _rglru(xr, gr, conv_w.reshape(CONV_WIDTH, d_rnn), conv_b[None, :], _block_diag(lru_wa).astype(BF16),
                   lru_ba[None, :], _block_diag(lru_wx).astype(BF16), lru_bx[None, :], lru_lambda[None, :],
                   gnorm_rnn_w[None, :])

    x1, h2, s_t = _outproj(o_attn, y_rnn, x, mod, gnorm_attn_w[None, :], w_out.astype(BF16), norm2_w[None, :],
                           w_router.T)

    eid_t, wgt_t, rank_t, counts = _route(s_t, b_router[:, None])
    counts = counts[:, 0].astype(I32)
    off = jnp.cumsum(counts) - counts
    dest_t = _dest(eid_t, rank_t, off.astype(F32)[:, None])
    h2f = h2.reshape(N, D)
    xs = _dispatch(dest_t, h2f, N * TOP_K)
    ys = _experts(_expert_items(counts, N * TOP_K), xs, w1, w3, w2)
    out = _combine(dest_t, ys, wgt_t.T, h2f, x1.reshape(N, D), mod, ws1.astype(BF16), ws3.astype(BF16),
                   ws2.astype(BF16), final_norm_w[None, :], S)
    return out.reshape(B, S, D)


def kernel(x, c, w_ada, b_ada, norm1_w, w_in, cmp_pos, cmp_k_w1, cmp_k_w2, cmp_v_w1, cmp_v_w2, rel_table, conv_w, conv_b, lru_wa, lru_ba, lru_wx, lru_bx, lru_lambda, gnorm_attn_w, gnorm_rnn_w, w_out, norm2_w, w_router, b_router, w1, w3, w2, ws1, ws3, ws2, final_norm_w):
    assert w_ada.shape[0] == 1
    c8 = jnp.pad(c, ((0, 8 - c.shape[0]), (0, 0)))
    return _layer(x, c8, w_ada[0], b_ada[0], norm1_w[0], w_in[0], cmp_pos[0], cmp_k_w1[0], cmp_k_w2[0],
                  cmp_v_w1[0], cmp_v_w2[0], rel_table, conv_w[0], conv_b[0], lru_wa[0], lru_ba[0], lru_wx[0],
                  lru_bx[0], lru_lambda[0], gnorm_attn_w[0], gnorm_rnn_w[0], w_out[0], norm2_w[0], w_router[0],
                  b_router[0], w1[0], w3[0], w2[0], ws1[0], ws3[0], ws2[0], final_norm_w)
```

```python
import functools
import math

import jax
import jax.numpy as jnp
import numpy as np
from jax import lax
from jax.experimental import pallas as pl
from jax.experimental.pallas import tpu as pltpu

F32 = jnp.float32
BF16 = jnp.bfloat16
I32 = jnp.int32

HEAD_DIM = 64
N_HEADS = 8
N_KV = 2
GQA = N_HEADS // N_KV
N_BRANCH = 3
CONV_WIDTH = 4
LRU_C = 8.0
CMP_LEN = 32
CMP_STRIDE = 16
SLC_LEN = 64
SLC_TOP = 16
WINDOW = 512
QB = 128
REL_BUCKETS = 32
REL_MAX_DIST = 1024
N_EXPERT_GROUPS = 8
TOP_GROUPS = 4
TOP_K = 8
ROUTED_SCALE = 2.5
MOE_BLOCK = 256
ROW_TILE = 8
EPS = 1e-6
NEG = -1e30
CMP_PAD = 120
CMP_WIN = 128
N_BIAS_TILES = 11
TILE_MASKED = N_BIAS_TILES
TILE_WINDOW_EDGE = N_BIAS_TILES + 1
FAR_TILE_DIST = 8
FAR_CHUNK = 4
VMEM_LIMIT = 52 * 1024 * 1024


def _dot(a, b, **kw):
    return jnp.dot(a, b, preferred_element_type=F32, **kw)


def _dot_nt(a, b, **kw):
    return lax.dot_general(a, b, (((1,), (1,)), ((), ())), preferred_element_type=F32, **kw)


def _gelu(x):
    return 0.5 * x * (1.0 + jnp.tanh(math.sqrt(2.0 / math.pi) * (x + 0.044715 * (x * x * x))))


def _sigmoid(x):
    return 1.0 / (1.0 + jnp.exp(-x))


def _rms(x):
    return x * lax.rsqrt(jnp.mean(x * x, axis=-1, keepdims=True) + EPS)


def _params(*sem):
    return pltpu.CompilerParams(dimension_semantics=sem, vmem_limit_bytes=VMEM_LIMIT)


def _ada_kernel(c_ref, w_ref, b_ref, o_ref):
    c = c_ref[...]
    a = c * _sigmoid(c)
    o_ref[...] = _dot(a, w_ref[...], precision=lax.Precision.HIGHEST) + b_ref[...]


def _ada(c8, w, b):
    d, n = w.shape
    tn = 1536
    return pl.pallas_call(
        _ada_kernel,
        grid=(n // tn,),
        in_specs=[pl.BlockSpec((8, d), lambda j: (0, 0)),
                  pl.BlockSpec((d, tn), lambda j: (0, j)),
                  pl.BlockSpec((1, tn), lambda j: (0, j))],
        out_specs=pl.BlockSpec((8, tn), lambda j: (0, j)),
        out_shape=jax.ShapeDtypeStruct((8, n), F32),
        compiler_params=_params("parallel"),
        name="ada",
    )(c8, w, b)


def _inproj_kernel(x_ref, mod_ref, nw_ref, w_ref, qt_ref, q_ref, kx_ref, vx_ref, kvc_ref, g_ref, xr_ref, gr_ref):
    h = _rms(x_ref[0]) * nw_ref[...]
    h = h * (1.0 + mod_ref[0, 1:2, :]) + mod_ref[0, 0:1, :]
    p = _dot(h.astype(BF16), w_ref[...])
    tm = p.shape[0]
    dq = N_HEADS * HEAD_DIM
    lane = lax.broadcasted_iota(I32, (tm, HEAD_DIM), 1)
    k_tail = jnp.where(lane < 2, 1.0, 0.0)
    v_tail = jnp.where(lane < 1, 1.0, 0.0)
    for hh in range(N_HEADS):
        qh = p[:, hh * HEAD_DIM:(hh + 1) * HEAD_DIM] * (HEAD_DIM ** -0.5)
        q_tail = jnp.broadcast_to(qt_ref[hh:hh + 1, :], (tm, HEAD_DIM))
        q_t = jnp.concatenate([qh, q_tail], axis=1).T.astype(BF16)
        for blk in range(tm // QB):
            col = (blk * GQA + hh % GQA) * QB
            q_ref[0, hh // GQA, :, col:col + QB] = q_t[:, blk * QB:(blk + 1) * QB]
    for j in range(6):
        for g in range(N_KV):
            col = dq + (j * N_KV + g) * HEAD_DIM
            piece = p[:, col:col + HEAD_DIM]
            if j < 2:
                kvc_ref[0, j * N_KV + g] = piece
            elif j % 2 == 0:
                kx_ref[0, (j // 2 - 1) * N_KV + g] = jnp.concatenate([piece, k_tail], axis=1).astype(BF16)
            else:
                v_t = jnp.concatenate([piece, v_tail], axis=1).T.astype(BF16)
                for blk in range(tm // QB):
                    vx_ref[0, (j // 2 - 1) * N_KV + g, blk] = v_t[:, blk * QB:(blk + 1) * QB]
    c0 = dq + 6 * N_KV * HEAD_DIM
    for g in range(N_KV):
        g_ref[0, g] = _sigmoid(p[:, c0 + g * 128:c0 + (g + 1) * 128]).T[0:16, :]
    xr_ref[0] = p[:, c0 + 256:c0 + 768]
    gr_ref[0] = p[:, c0 + 768:c0 + 1280]


def _inproj(x, mod, nw, w_pad, q_tail, tm=256):
    B, S, D = x.shape
    ncol = w_pad.shape[1]
    heads = lambda n, w: pl.BlockSpec((1, n, tm, w), lambda b, t: (b, 0, t, 0))
    return pl.pallas_call(
        _inproj_kernel,
        grid=(B, S // tm),
        in_specs=[pl.BlockSpec((1, tm, D), lambda b, t: (b, t, 0)),
                  pl.BlockSpec((1, 6, D), lambda b, t: (b, 0, 0)),
                  pl.BlockSpec((1, D), lambda b, t: (0, 0)),
                  pl.BlockSpec((D, ncol), lambda b, t: (0, 0)),
                  pl.BlockSpec(q_tail.shape, lambda b, t: (0, 0))],
        out_specs=[pl.BlockSpec((1, N_KV, 2 * HEAD_DIM, tm * GQA), lambda b, t: (b, 0, 0, t)),
                   heads(4, 2 * HEAD_DIM),
                   pl.BlockSpec((1, 4, tm // QB, 2 * HEAD_DIM, QB), lambda b, t: (b, 0, t, 0, 0)),
                   heads(4, HEAD_DIM),
                   pl.BlockSpec((1, N_KV, 16, tm), lambda b, t: (b, 0, 0, t)),
                   pl.BlockSpec((1, tm, 512), lambda b, t: (b, t, 0)),
                   pl.BlockSpec((1, tm, 512), lambda b, t: (b, t, 0))],
        out_shape=[jax.ShapeDtypeStruct((B, N_KV, 2 * HEAD_DIM, S * GQA), BF16),
                   jax.ShapeDtypeStruct((B, 4, S, 2 * HEAD_DIM), BF16),
                   jax.ShapeDtypeStruct((B, 4, S // QB, 2 * HEAD_DIM, QB), BF16),
                   jax.ShapeDtypeStruct((B, 4, S, HEAD_DIM), F32),
                   jax.ShapeDtypeStruct((B, N_KV, 16, S), F32),
                   jax.ShapeDtypeStruct((B, S, 512), F32),
                   jax.ShapeDtypeStruct((B, S, 512), F32)],
        compiler_params=_params("parallel", "parallel"),
        name="inproj",
    )(x, mod, nw, w_pad, q_tail)


def _compress_kernel(x_ref, pos_ref, w1_ref, w2_ref, o_ref, ot_ref):
    x = x_ref[0, 0]
    nc = x.shape[0]
    half = CMP_STRIDE * HEAD_DIM
    kv = pl.program_id(1) // N_KV
    a = _dot((x + pos_ref[0:1, :]).astype(BF16), w1_ref[kv, 0:half, :].astype(BF16))
    b = _dot((x + pos_ref[1:2, :]).astype(BF16), w1_ref[kv, half:2 * half, :].astype(BF16))
    hid = _gelu(a + pltpu.roll(b, nc - 1, 0))
    out = _dot(hid.astype(BF16), w2_ref[kv].astype(BF16))
    row = lax.broadcasted_iota(I32, out.shape, 0)
    out = jnp.where(row < nc - 1, out, 0.0)
    lane = lax.broadcasted_iota(I32, (nc, HEAD_DIM), 1)
    out = jnp.concatenate([out, jnp.where(lane < 2, 1.0, 0.0)], axis=1)
    o_ref[0, 0, 0:CMP_PAD, :] = jnp.zeros((CMP_PAD, 2 * HEAD_DIM), F32)
    o_ref[0, 0, CMP_PAD:CMP_PAD + nc, :] = out
    o_ref[0, 0, CMP_PAD + nc:, :] = jnp.zeros((CMP_WIN - CMP_PAD, 2 * HEAD_DIM), F32)
    ot_ref[0, 0] = out.T


def _compress(kvc, pos2, w1, w2):
    B, _, NC, W = kvc.shape
    return pl.pallas_call(
        _compress_kernel,
        grid=(B, 4),
        in_specs=[pl.BlockSpec((1, 1, NC, W), lambda b, i: (b, i, 0, 0)),
                  pl.BlockSpec((2, W), lambda b, i: (0, 0)),
                  pl.BlockSpec((2, 2 * W, HEAD_DIM), lambda b, i: (0, 0, 0)),
                  pl.BlockSpec((2, HEAD_DIM, HEAD_DIM), lambda b, i: (0, 0, 0))],
        out_specs=[pl.BlockSpec((1, 1, NC + CMP_WIN, 2 * HEAD_DIM), lambda b, i: (b, i, 0, 0)),
                   pl.BlockSpec((1, 1, 2 * HEAD_DIM, NC), lambda b, i: (b, i, 0, 0))],
        out_shape=[jax.ShapeDtypeStruct((B, 4, NC + CMP_WIN, 2 * HEAD_DIM), F32),
                   jax.ShapeDtypeStruct((B, 4, 2 * HEAD_DIM, NC), F32)],
        compiler_params=_params("parallel", "parallel"),
        name="compress",
    )(kvc, pos2, w1, w2)


def _attn_kernel(q_ref, ks_ref, vs_ref, kw_ref, vw_ref, kc_ref, vc_ref, vct_ref, g_ref, t_ref, w_ref, ovt_ref, ov_ref,
                 o_ref, m_scr, acc_scr, sel_scr):
    qb = pl.program_id(2)
    R = GQA * QB
    q = q_ref[0, 0]
    nc = vct_ref.shape[3]
    nb = ov_ref.shape[1]
    n_top = min(SLC_TOP, ks_ref.shape[2] // SLC_LEN)

    def heads_sum(x):
        out = x[:, 0:QB]
        for hq in range(1, GQA):
            out = out + x[:, hq * QB:(hq + 1) * QB]
        return out

    w0 = pl.multiple_of(qb * 8, 8)
    s_far = _dot(kc_ref[0, 0, CMP_PAD:CMP_PAD + nc, :].astype(BF16), q)
    n_io = lax.broadcasted_iota(I32, (nc, R), 0)
    s_far = jnp.where(n_io < qb * 8 - CMP_PAD, s_far, -jnp.inf)
    s_win = _dot(kc_ref[0, 0, pl.ds(w0, CMP_WIN), :].astype(BF16), q) + w_ref[0]
    j_w = lax.broadcasted_iota(I32, (CMP_WIN, R), 0)
    i_w = lax.broadcasted_iota(I32, (CMP_WIN, R), 1) & (QB - 1)
    dist_w = i_w - CMP_STRIDE * (j_w - CMP_PAD) - (CMP_LEN - 1)
    s_win = jnp.where((dist_w >= 0) & (j_w >= CMP_PAD - qb * 8), s_win, -jnp.inf)
    m = jnp.maximum(jnp.max(s_far, axis=0, keepdims=True), jnp.max(s_win, axis=0, keepdims=True))
    m = jnp.where(m == -jnp.inf, 0.0, m)
    e_far = jnp.exp(s_far - m)
    e_win = jnp.exp(s_win - m)
    l = jnp.sum(e_far, axis=0, keepdims=True) + jnp.sum(e_win, axis=0, keepdims=True)
    inv = 1.0 / jnp.maximum(l, 1e-30)
    p_far = e_far * inv
    p_win = e_win * inv
    vcw_t = vc_ref[0, 0, pl.ds(w0, CMP_WIN), :].T
    o_c = _dot(vct_ref[0, 0].astype(BF16), p_far.astype(BF16)) + _dot(vcw_t.astype(BF16), p_win.astype(BF16))
    ov_win_t = ov_ref[pl.ds(w0, CMP_WIN), :].T
    sc_t = (_dot(ovt_ref[...], heads_sum(p_far).astype(BF16))
            + _dot(ov_win_t.astype(BF16), heads_sum(p_win).astype(BF16)))

    blk = lax.broadcasted_iota(I32, (nb, QB), 0)
    t_io = lax.broadcasted_iota(I32, (nb, QB), 1)
    cur = qb * (QB // SLC_LEN) + t_io // SLC_LEN
    forced = (blk == 0) | (blk == cur) | (blk == cur - 1)
    val = jnp.where(forced, jnp.inf, jnp.where(blk <= cur, sc_t, -jnp.inf))
    sel_t = jnp.zeros((nb, QB), F32)
    for _ in range(n_top):
        mx = jnp.max(val, axis=0, keepdims=True)
        hit = (val == mx) & (val > -jnp.inf)
        first = jnp.min(jnp.where(hit, blk, nb), axis=0, keepdims=True)
        pick = blk == first
        sel_t = jnp.where(pick, 1.0, sel_t)
        val = jnp.where(pick, -jnp.inf, val)
    sel_scr[...] = sel_t

    m_scr[...] = jnp.full(m_scr.shape, NEG, F32)
    acc_scr[...] = jnp.zeros(acc_scr.shape, F32)
    blocks_per_chunk = FAR_CHUNK * QB // SLC_LEN

    def update(c, bias):
        s = _dot(ks_ref[0, 0, pl.ds(pl.multiple_of(c * FAR_CHUNK * QB, FAR_CHUNK * QB), FAR_CHUNK * QB), :], q)
        if bias is not None:
            s = s + bias
        rows = sel_scr[pl.ds(pl.multiple_of(c * blocks_per_chunk, blocks_per_chunk), blocks_per_chunk), :]
        keep = jnp.concatenate([jnp.broadcast_to(rows[r:r + 1, :], (SLC_LEN, QB)) for r in range(blocks_per_chunk)],
                               axis=0)
        s = jnp.where(jnp.concatenate([keep] * GQA, axis=1) > 0.5, s, NEG)
        m_old = m_scr[...]
        m_new = jnp.maximum(m_old, jnp.max(s, axis=0, keepdims=True))
        p = jnp.exp(s - m_new)
        v_t = jnp.concatenate([vs_ref[0, 0, c * FAR_CHUNK + j] for j in range(FAR_CHUNK)], axis=1)
        acc_scr[...] = jnp.exp(m_old - m_new) * acc_scr[...] + _dot(v_t, p.astype(BF16))
        m_scr[...] = m_new

    def far_body(c, carry):
        update(c, None)
        return carry

    def near_body(c, carry):
        tiles = []
        for j in range(FAR_CHUNK):
            delta = qb - (c * FAR_CHUNK + j)
            tiles.append(t_ref[0, jnp.where(delta < 0, TILE_MASKED, delta)])
        update(c, jnp.concatenate(tiles, axis=0))
        return carry

    n_far = jnp.maximum(qb - (FAR_TILE_DIST - 1), 0) // FAR_CHUNK
    lax.fori_loop(0, n_far, far_body, 0)
    lax.fori_loop(n_far, qb // FAR_CHUNK + 1, near_body, 0)
    acc = acc_scr[...]
    o_s = acc[0:HEAD_DIM, :] / acc[HEAD_DIM:HEAD_DIM + 1, :]

    n_wt = WINDOW // QB + 1
    k_t, v_t, b_t = [], [], []
    for j in range(n_wt):
        kb = qb - (n_wt - 1) + j
        kb0 = jnp.maximum(kb, 0)
        k_t.append(kw_ref[0, 0, pl.ds(pl.multiple_of(kb0 * QB, QB), QB), :])
        v_t.append(vw_ref[0, 0, kb0])
        b_t.append(t_ref[0, jnp.where(kb < 0, TILE_MASKED, TILE_WINDOW_EDGE if j == 0 else n_wt - 1 - j)])
    s = _dot(jnp.concatenate(k_t, axis=0), q) + jnp.concatenate(b_t, axis=0)
    p = jnp.exp(s - jnp.max(s, axis=0, keepdims=True))
    acc = _dot(jnp.concatenate(v_t, axis=1), p.astype(BF16))
    o_w = acc[0:HEAD_DIM, :] / acc[HEAD_DIM:HEAD_DIM + 1, :]

    gates = g_ref[0, 0]
    outs = []
    for hq in range(GQA):
        cols = slice(hq * QB, (hq + 1) * QB)
        c = hq * N_BRANCH
        outs.append(gates[c:c + 1, :] * o_c[0:HEAD_DIM, cols] + gates[c + 1:c + 2, :] * o_s[:, cols]
                    + gates[c + 2:c + 3, :] * o_w[:, cols])
    o_ref[0] = jnp.concatenate(outs, axis=0).T


def _attn(q_t, kx, vx_t, kvc, kvc_t, gates_t, t_tab, w_tab, ov_t, ov):
    B, _, W, _ = q_t.shape
    S = kx.shape[2]
    ncp = kvc.shape[2]
    nb = ov.shape[1]
    keys = lambda j: pl.BlockSpec((1, 1, S, W), lambda b, g, t: (b, j * N_KV + g, 0, 0))
    vals = lambda j: pl.BlockSpec((1, 1, S // QB, W, QB), lambda b, g, t: (b, j * N_KV + g, 0, 0, 0))
    cmp = lambda j: pl.BlockSpec((1, 1, ncp, W), lambda b, g, t: (b, j * N_KV + g, 0, 0))
    return pl.pallas_call(
        _attn_kernel,
        grid=(B, N_KV, S // QB),
        in_specs=[pl.BlockSpec((1, 1, W, GQA * QB), lambda b, g, t: (b, g, 0, t)),
                  keys(0), vals(0), keys(1), vals(1), cmp(0), cmp(1),
                  pl.BlockSpec((1, 1, W, ncp - CMP_WIN), lambda b, g, t: (b, N_KV + g, 0, 0)),
                  pl.BlockSpec((1, 1, 16, QB), lambda b, g, t: (b, g, 0, t)),
                  pl.BlockSpec((1,) + t_tab.shape[1:], lambda b, g, t: (g, 0, 0, 0)),
                  pl.BlockSpec((1, CMP_WIN, GQA * QB), lambda b, g, t: (g, 0, 0)),
                  pl.BlockSpec(ov_t.shape, lambda b, g, t: (0, 0)),
                  pl.BlockSpec(ov.shape, lambda b, g, t: (0, 0))],
        out_specs=pl.BlockSpec((1, QB, GQA * HEAD_DIM), lambda b, g, t: (b, t, g)),
        out_shape=jax.ShapeDtypeStruct((B, S, N_HEADS * HEAD_DIM), F32),
        scratch_shapes=[pltpu.VMEM((1, GQA * QB), F32), pltpu.VMEM((W, GQA * QB), F32), pltpu.VMEM((nb, QB), F32)],
        compiler_params=_params("parallel", "parallel", "arbitrary"),
        name="attn",
    )(q_t, kx, vx_t, kx, vx_t, kvc, kvc, kvc_t, gates_t, t_tab, w_tab, ov_t, ov)


def _rglru_kernel(xr_ref, gr_ref, cw_ref, cb_ref, wa_ref, ba_ref, wx_ref, bx_ref, lam_ref, gw_ref, o_ref,
                  xbuf, hprev, a_scr, u_scr, h_scr):
    ts = xr_ref.shape[1]
    C = xr_ref.shape[2]

    @pl.when(pl.program_id(1) == 0)
    def _():
        xbuf[0:8, :] = jnp.zeros((8, C), F32)
        hprev[...] = jnp.zeros(hprev.shape, F32)

    xbuf[8:8 + ts, :] = xr_ref[0]
    xc = cb_ref[...] + jnp.zeros((ts, C), F32)
    for j in range(CONV_WIDTH):
        xc = xc + cw_ref[j:j + 1, :] * xbuf[pl.ds(8 - (CONV_WIDTH - 1) + j, ts), :]
    xbuf[0:8, :] = xbuf[ts:ts + 8, :]

    xcb = xc.astype(BF16)
    r = _sigmoid(_dot(xcb, wa_ref[...]) + ba_ref[...])
    i = _sigmoid(_dot(xcb, wx_ref[...]) + bx_ref[...])
    z = -lam_ref[...]
    softplus = jnp.maximum(z, 0.0) + jnp.log(1.0 + jnp.exp(-jnp.abs(z)))
    log_a = -LRU_C * r * softplus
    a_scr[...] = jnp.exp(log_a)
    u_scr[...] = jnp.sqrt(1.0 - jnp.exp(2.0 * log_a)) * (i * xc)

    row = lax.broadcasted_iota(I32, (8, C), 0)

    def body(k, h):
        r0 = pl.multiple_of(k * 8, 8)
        a = a_scr[pl.ds(r0, 8), :]
        b = u_scr[pl.ds(r0, 8), :]
        for s in (1, 2, 4):
            keep = row >= s
            b = jnp.where(keep, a * pltpu.roll(b, s, 0) + b, b)
            a = jnp.where(keep, a * pltpu.roll(a, s, 0), a)
        hh = a * h + b
        h_scr[pl.ds(r0, 8), :] = hh
        return jnp.broadcast_to(hh[7:8, :], (8, C))

    hprev[...] = lax.fori_loop(0, ts // 8, body, hprev[...])
    out = h_scr[...] * _gelu(gr_ref[0])
    o_ref[0] = (_rms(out) * gw_ref[...]).astype(BF16)


def _rglru(xr, gr, cw, cb, wa_bd, ba, wx_bd, bx, lam, gw, ts=512):
    B, S, C = xr.shape
    ts = min(ts, S)
    vec = pl.BlockSpec((1, C), lambda b, t: (0, 0))
    mat = pl.BlockSpec((C, C), lambda b, t: (0, 0))
    seq = pl.BlockSpec((1, ts, C), lambda b, t: (b, t, 0))
    return pl.pallas_call(
        _rglru_kernel,
        grid=(B, S // ts),
        in_specs=[seq, seq, pl.BlockSpec((CONV_WIDTH, C), lambda b, t: (0, 0)), vec, mat, vec, mat, vec, vec, vec],
        out_specs=seq,
        out_shape=jax.ShapeDtypeStruct((B, S, C), BF16),
        scratch_shapes=[pltpu.VMEM((ts + 8, C), F32), pltpu.VMEM((8, C), F32), pltpu.VMEM((ts, C), F32),
                        pltpu.VMEM((ts, C), F32), pltpu.VMEM((ts, C), F32)],
        compiler_params=_params("parallel", "arbitrary"),
        name="rglru",
    )(xr, gr, cw, cb, wa_bd, ba, wx_bd, bx, lam, gw)


def _to_row_tiles(ref, x):
    rows = x.shape[0]
    for s in range(ROW_TILE):
        ref[pl.ds(s, rows, stride=ROW_TILE), :] = x[:, s * 128:(s + 1) * 128]


def _from_row_tiles(ref, rows, start=0):
    return jnp.concatenate([ref[pl.ds(start + s, rows, stride=ROW_TILE), :] for s in range(ROW_TILE)], axis=1)


def _outproj_kernel(oa_ref, yr_ref, x_ref, mod_ref, gaw_ref, wo_ref, n2_ref, wr_ref, x1_ref, h2_ref, h2t_ref, st_ref):
    da = oa_ref.shape[2]
    ya = (_rms(oa_ref[0]) * gaw_ref[...]).astype(BF16)
    mix = _dot(ya, wo_ref[0:da, :]) + _dot(yr_ref[0], wo_ref[da:, :])
    x1 = x_ref[0] + mod_ref[0, 2:3, :] * mix
    x1_ref[0] = x1
    h2 = (_rms(x1) * n2_ref[...]) * (1.0 + mod_ref[0, 4:5, :]) + mod_ref[0, 3:4, :]
    h2_ref[0] = h2
    _to_row_tiles(h2t_ref, h2)
    st_ref[...] = _sigmoid(_dot_nt(wr_ref[...], h2, precision=lax.Precision.HIGHEST))


def _outproj(oa, yr, x, mod, gaw, wo, n2, wr_t, tm=256):
    B, S, D = x.shape
    E = wr_t.shape[0]
    nt = S // tm
    row = lambda w: pl.BlockSpec((1, tm, w), lambda b, t: (b, t, 0))
    return pl.pallas_call(
        _outproj_kernel,
        grid=(B, nt),
        in_specs=[row(oa.shape[2]), row(yr.shape[2]), row(D),
                  pl.BlockSpec((1, 6, D), lambda b, t: (b, 0, 0)),
                  pl.BlockSpec((1, oa.shape[2]), lambda b, t: (0, 0)),
                  pl.BlockSpec(wo.shape, lambda b, t: (0, 0)),
                  pl.BlockSpec((1, D), lambda b, t: (0, 0)),
                  pl.BlockSpec((E, D), lambda b, t: (0, 0))],
        out_specs=[row(D), row(D), pl.BlockSpec((tm * ROW_TILE, D // ROW_TILE), lambda b, t: (b * nt + t, 0)),
                   pl.BlockSpec((E, tm), lambda b, t: (0, b * nt + t))],
        out_shape=[jax.ShapeDtypeStruct((B, S, D), F32), jax.ShapeDtypeStruct((B, S, D), F32),
                   jax.ShapeDtypeStruct((B * S * ROW_TILE, D // ROW_TILE), F32),
                   jax.ShapeDtypeStruct((E, B * S), F32)],
        compiler_params=_params("parallel", "parallel"),
        name="outproj",
    )(oa, yr, x, mod, gaw, wo, n2, wr_t)


def _route_kernel(s_ref, b_ref, eid_ref, wgt_ref, rank_ref, cnt_ref, carry):
    E, tn = s_ref.shape
    per = E // N_EXPERT_GROUPS

    @pl.when(pl.program_id(0) == 0)
    def _():
        carry[...] = jnp.zeros(carry.shape, F32)

    s = s_ref[...]
    s_sel = s + b_ref[...]
    eidx = lax.broadcasted_iota(I32, (E, tn), 0)

    grp = []
    for gi in range(N_EXPERT_GROUPS):
        xg = s_sel[gi * per:(gi + 1) * per, :]
        ig = lax.broadcasted_iota(I32, (per, tn), 0)
        m1 = jnp.max(xg, axis=0, keepdims=True)
        f1 = jnp.min(jnp.where(xg == m1, ig, per), axis=0, keepdims=True)
        m2 = jnp.max(jnp.where(ig == f1, -jnp.inf, xg), axis=0, keepdims=True)
        grp.append(m1 + m2)
    val = []
    for gi in range(N_EXPERT_GROUPS):
        rank = jnp.zeros((1, tn), I32)
        for gj in range(N_EXPERT_GROUPS):
            if gj == gi:
                continue
            ahead = (grp[gj] > grp[gi]) | ((grp[gj] == grp[gi]) & (gj < gi))
            rank = rank + ahead.astype(I32)
        val.append(jnp.where(rank < TOP_GROUPS, s_sel[gi * per:(gi + 1) * per, :], -jnp.inf))
    val = jnp.concatenate(val, axis=0)

    eids, tops = [], []
    hot = jnp.zeros((E, tn), F32)
    for _ in range(TOP_K):
        mx = jnp.max(val, axis=0, keepdims=True)
        first = jnp.min(jnp.where(val == mx, eidx, E), axis=0, keepdims=True)
        pick = eidx == first
        eids.append(first)
        tops.append(jnp.sum(jnp.where(pick, s, 0.0), axis=0, keepdims=True))
        hot = jnp.where(pick, 1.0, hot)
        val = jnp.where(pick, -jnp.inf, val)
    denom = tops[0]
    for t in tops[1:]:
        denom = denom + t

    ti = lax.broadcasted_iota(I32, (tn, tn), 0)
    tj = lax.broadcasted_iota(I32, (tn, tn), 1)
    upper = jnp.where(ti < tj, 1.0, 0.0).astype(BF16)
    before = _dot(hot.astype(BF16), upper) + carry[...]
    for k in range(TOP_K):
        eid_ref[k:k + 1, :] = eids[k]
        wgt_ref[k:k + 1, :] = ROUTED_SCALE * tops[k] / denom
        rank_ref[k:k + 1, :] = jnp.sum(jnp.where(eidx == eids[k], before, 0.0), axis=0, keepdims=True).astype(I32)
    carry[...] = carry[...] + jnp.sum(hot, axis=1, keepdims=True)
    cnt_ref[...] = carry[...]


def _route(s_t, b_col, tn=256):
    E, N = s_t.shape
    tn = min(tn, N)
    out = pl.BlockSpec((TOP_K, tn), lambda t: (0, t))
    return pl.pallas_call(
        _route_kernel,
        grid=(N // tn,),
        in_specs=[pl.BlockSpec((E, tn), lambda t: (0, t)), pl.BlockSpec((E, 1), lambda t: (0, 0))],
        out_specs=[out, out, out, pl.BlockSpec((E, 1), lambda t: (0, 0))],
        out_shape=[jax.ShapeDtypeStruct((TOP_K, N), I32), jax.ShapeDtypeStruct((TOP_K, N), F32),
                   jax.ShapeDtypeStruct((TOP_K, N), I32), jax.ShapeDtypeStruct((E, 1), F32)],
        scratch_shapes=[pltpu.VMEM((E, 1), F32)],
        compiler_params=_params("arbitrary"),
        name="route",
    )(s_t, b_col)


def _dest_kernel(eid_ref, rank_ref, off_ref, o_ref):
    E = off_ref.shape[0]
    tn = eid_ref.shape[1]
    eidx = lax.broadcasted_iota(I32, (E, tn), 0)
    for k in range(TOP_K):
        base = jnp.sum(jnp.where(eidx == eid_ref[k:k + 1, :], off_ref[...], 0.0), axis=0, keepdims=True)
        o_ref[k:k + 1, :] = base.astype(I32) + rank_ref[k:k + 1, :]


def _dest(eid_t, rank_t, off_col, tn=512):
    K, N = eid_t.shape
    tn = min(tn, N)
    blk = pl.BlockSpec((K, tn), lambda t: (0, t))
    return pl.pallas_call(
        _dest_kernel,
        grid=(N // tn,),
        in_specs=[blk, blk, pl.BlockSpec(off_col.shape, lambda t: (0, 0))],
        out_specs=blk,
        out_shape=jax.ShapeDtypeStruct((K, N), I32),
        compiler_params=_params("parallel"),
        name="dest",
    )(eid_t, rank_t, off_col)


def _row_tile(ref, r):
    return ref.at[pl.ds(pl.multiple_of(r * ROW_TILE, ROW_TILE), ROW_TILE)]


def _dispatch_kernel(dest_ref, h_ref, xs_hbm, sem):
    tn = h_ref.shape[0] // ROW_TILE

    def go(t, c):
        for k in range(TOP_K):
            pltpu.make_async_copy(_row_tile(h_ref, t), _row_tile(xs_hbm, dest_ref[k, t]), sem.at[0]).start()
        return c
    lax.fori_loop(0, tn, go, 0, unroll=4)
    for k in range(TOP_K):
        pltpu.make_async_copy(h_ref, xs_hbm.at[pl.ds(0, tn * ROW_TILE)], sem.at[0]).wait()


def _dispatch(dest_t, h2t, n_slots, tn=256):
    rows, W = h2t.shape
    tn = min(tn, rows // ROW_TILE)
    return pl.pallas_call(
        _dispatch_kernel,
        grid=(rows // ROW_TILE // tn,),
        in_specs=[pl.BlockSpec((TOP_K, tn), lambda t: (0, t), memory_space=pltpu.SMEM),
                  pl.BlockSpec((tn * ROW_TILE, W), lambda t: (t, 0))],
        out_specs=pl.BlockSpec(memory_space=pl.ANY),
        out_shape=jax.ShapeDtypeStruct((n_slots * ROW_TILE, W), F32),
        scratch_shapes=[pltpu.SemaphoreType.DMA((1,))],
        compiler_params=_params("arbitrary"),
        name="dispatch",
    )(dest_t, h2t)


def _experts_kernel(rb_ref, e_ref, lo_ref, hi_ref, first_ref, efirst_ref, x_ref, w1_ref, w3_ref, w2_ref, y_ref,
                    w1b, w3b, w2b):
    i = pl.program_id(0)
    lo = lo_ref[i]
    hi = hi_ref[i]

    @pl.when((hi > lo) & (efirst_ref[i] == 1))
    def _():
        w1b[...] = w1_ref[0].astype(BF16)
        w3b[...] = w3_ref[0].astype(BF16)
        w2b[...] = w2_ref[0].astype(BF16)

    @pl.when(hi > lo)
    def _():
        x = _from_row_tiles(x_ref, MOE_BLOCK).astype(BF16)
        h1 = _dot(x, w1b[...])
        h3 = _dot(x, w3b[...])
        hid = (h1 * _sigmoid(h1)) * h3
        y = _dot(hid.astype(BF16), w2b[...])
        row = lax.broadcasted_iota(I32, y.shape, 0)
        mine = (row >= lo) & (row < hi)

        @pl.when(first_ref[i] == 1)
        def _():
            _to_row_tiles(y_ref, jnp.where(mine, y, 0.0))

        @pl.when(first_ref[i] == 0)
        def _():
            _to_row_tiles(y_ref, jnp.where(mine, y, _from_row_tiles(y_ref, MOE_BLOCK)))


def _experts(items, xs, w1, w3, w2):
    rows_total, W = xs.shape
    _, D, De = w1.shape
    n_pre = len(items)
    rows = pl.BlockSpec((MOE_BLOCK * ROW_TILE, W), lambda i, rb, *_: (rb[i], 0))
    wspec = lambda shape: pl.BlockSpec(shape, lambda i, rb, e, *_: (e[i], 0, 0))
    grid_spec = pltpu.PrefetchScalarGridSpec(
        num_scalar_prefetch=n_pre,
        grid=(items[0].shape[0],),
        in_specs=[rows, wspec((1, D, De)), wspec((1, D, De)), wspec((1, De, D))],
        out_specs=rows,
        scratch_shapes=[pltpu.VMEM((D, De), BF16), pltpu.VMEM((D, De), BF16), pltpu.VMEM((De, D), BF16)],
    )
    return pl.pallas_call(
        _experts_kernel,
        grid_spec=grid_spec,
        out_shape=jax.ShapeDtypeStruct((rows_total, W), F32),
        compiler_params=_params("arbitrary"),
        name="experts",
    )(*items, xs, w1, w3, w2)


def _expert_items(counts, n_rows):
    E = counts.shape[0]
    n_rb = n_rows // MOE_BLOCK
    n_items = n_rb + E
    ce = jnp.cumsum(counts)
    cs = ce - counts
    first_blk = cs // MOE_BLOCK
    nb = jnp.where(counts > 0, (ce - 1) // MOE_BLOCK - first_blk + 1, 0)
    io_end = jnp.cumsum(nb)
    w = jnp.arange(n_items, dtype=I32)
    valid = w < io_end[-1]
    e_w = jnp.minimum(jnp.sum(io_end[None, :] <= w[:, None], axis=1), E - 1).astype(I32)
    rb = jnp.where(valid, first_blk[e_w] + w - (io_end - nb)[e_w], n_rb - 1).astype(I32)
    lo = jnp.where(valid, jnp.clip(cs[e_w] - MOE_BLOCK * rb, 0, MOE_BLOCK), 0).astype(I32)
    hi = jnp.where(valid, jnp.clip(ce[e_w] - MOE_BLOCK * rb, 0, MOE_BLOCK), 0).astype(I32)
    first = jnp.concatenate([jnp.ones((1,), I32), (rb[1:] != rb[:-1]).astype(I32)])
    e_first = jnp.concatenate([jnp.ones((1,), I32), (e_w[1:] != e_w[:-1]).astype(I32)])
    return rb, e_w, lo, hi, first, e_first


def _combine_kernel(dest_ref, ys_hbm, w_ref, h2_ref, x1_ref, mod_ref, ws1_ref, ws3_ref, ws2_ref, fw_ref, o_ref,
                    ybuf, sem):
    tm = h2_ref.shape[0]

    def go(t, c):
        for k in range(TOP_K):
            pltpu.make_async_copy(_row_tile(ys_hbm, dest_ref[k, t]), _row_tile(ybuf.at[k], t), sem.at[0]).start()
        return c
    lax.fori_loop(0, tm, go, 0, unroll=4)

    h = h2_ref[...].astype(BF16)
    a = _dot(h, ws1_ref[...])
    hid = (a * _sigmoid(a)) * _dot(h, ws3_ref[...])
    shared = _dot(hid.astype(BF16), ws2_ref[...])

    for k in range(TOP_K):
        pltpu.make_async_copy(ys_hbm.at[pl.ds(0, tm * ROW_TILE)], ybuf.at[k], sem.at[0]).wait()
    routed = w_ref[:, 0:1] * _from_row_tiles(ybuf.at[0], tm)
    for k in range(1, TOP_K):
        routed = routed + w_ref[:, k:k + 1] * _from_row_tiles(ybuf.at[k], tm)
    x2 = x1_ref[...] + mod_ref[0, 5:6, :] * (routed + shared)
    o_ref[...] = _rms(x2) * fw_ref[...]


def _combine(dest_t, ys, wgt, h2, x1, mod, ws1, ws3, ws2, fw, seq, tm=256):
    N, D = h2.shape
    tm = min(tm, seq)
    nt = seq // tm
    row = pl.BlockSpec((tm, D), lambda t: (t, 0))
    full = lambda a: pl.BlockSpec(a.shape, lambda t: (0,) * a.ndim)
    return pl.pallas_call(
        _combine_kernel,
        grid=(N // tm,),
        in_specs=[pl.BlockSpec((TOP_K, tm), lambda t: (0, t), memory_space=pltpu.SMEM),
                  pl.BlockSpec(memory_space=pl.ANY),
                  pl.BlockSpec((tm, TOP_K), lambda t: (t, 0)),
                  row, row,
                  pl.BlockSpec((1, 6, D), lambda t: (t // nt, 0, 0)),
                  full(ws1), full(ws3), full(ws2), full(fw)],
        out_specs=row,
        out_shape=jax.ShapeDtypeStruct((N, D), F32),
        scratch_shapes=[pltpu.VMEM((TOP_K, tm * ROW_TILE, D // ROW_TILE), F32), pltpu.SemaphoreType.DMA((1,))],
        compiler_params=_params("arbitrary"),
        name="combine",
    )(dest_t, ys, wgt, h2, x1, mod, ws1, ws3, ws2, fw)


def _t5_bucket(dist):
    n = jnp.maximum(dist, 0)
    max_exact = REL_BUCKETS // 2
    nf = jnp.maximum(n, 1).astype(F32)
    large = max_exact + (jnp.log(nf / max_exact) / math.log(REL_MAX_DIST / max_exact)
                         * (REL_BUCKETS - max_exact)).astype(I32)
    large = jnp.minimum(large, REL_BUCKETS - 1)
    return jnp.where(n < max_exact, n, large)


def _bias_kernel(tbl_ref, far_ref, bt_ref, bw_ref, t_ref, w_ref):
    g = pl.program_id(0)
    d = pl.program_id(1)

    def lookup(bkt, head):
        out = jnp.full(bkt.shape, tbl_ref[0, head], F32)
        for b in range(1, REL_BUCKETS):
            out = jnp.where(bkt == b, tbl_ref[b, head], out)
        return out - far_ref[head]

    j = lax.broadcasted_iota(I32, (QB, QB), 0)
    i = lax.broadcasted_iota(I32, (QB, QB), 1)
    keep = ((d != 0) | (i >= j)) & (d != TILE_MASKED) & ((d != TILE_WINDOW_EDGE) | (j > i))
    for hq in range(GQA):
        t_ref[0, 0, :, hq * QB:(hq + 1) * QB] = jnp.where(keep, lookup(bt_ref[0], GQA * g + hq), NEG)

    @pl.when(d == 0)
    def _():
        for hq in range(GQA):
            w_ref[0, :, hq * QB:(hq + 1) * QB] = lookup(bw_ref[...], GQA * g + hq)


def _bias_tables(rel_table):
    tbl = rel_table.astype(F32)
    far_hi = tbl[REL_BUCKETS - 1].astype(BF16)
    far_lo = (tbl[REL_BUCKETS - 1] - far_hi.astype(F32)).astype(BF16)
    far = far_hi.astype(F32) + far_lo.astype(F32)
    q_tail = jnp.zeros((N_HEADS, HEAD_DIM), F32).at[:, 0].set(far_hi.astype(F32)).at[:, 1].set(far_lo.astype(F32))
    i = jnp.arange(QB)
    tile_dist = np.array(list(range(N_BIAS_TILES)) + [0, WINDOW // QB])
    d_t = (QB * jnp.asarray(tile_dist)[:, None, None] + i[None, None, :] - i[None, :, None])
    d_w = i[None, :] - CMP_STRIDE * (jnp.arange(CMP_WIN)[:, None] - CMP_PAD) - (CMP_LEN - 1)
    n_tiles = len(tile_dist)
    t_tab, w_tab = pl.pallas_call(
        _bias_kernel,
        grid=(N_KV, n_tiles),
        in_specs=[pl.BlockSpec(memory_space=pltpu.SMEM),
                  pl.BlockSpec(memory_space=pltpu.SMEM),
                  pl.BlockSpec((1, QB, QB), lambda g, d: (d, 0, 0)),
                  pl.BlockSpec((CMP_WIN, QB), lambda g, d: (0, 0))],
        out_specs=[pl.BlockSpec((1, 1, QB, GQA * QB), lambda g, d: (g, d, 0, 0)),
                   pl.BlockSpec((1, CMP_WIN, GQA * QB), lambda g, d: (g, 0, 0))],
        out_shape=[jax.ShapeDtypeStruct((N_KV, n_tiles, QB, GQA * QB), F32),
                   jax.ShapeDtypeStruct((N_KV, CMP_WIN, GQA * QB), F32)],
        compiler_params=_params("parallel", "arbitrary"),
        name="bias",
    )(tbl, far, _t5_bucket(d_t), _t5_bucket(d_w))
    far_min = min(FAR_TILE_DIST * QB - QB + 1, CMP_STRIDE * (CMP_PAD + 1) - (CMP_LEN - 1))
    max_exact = REL_BUCKETS // 2
    assert math.log(far_min / max_exact) / math.log(REL_MAX_DIST / max_exact) * max_exact > max_exact - 0.75
    return t_tab, w_tab, q_tail


def _overlap_tables(nc, nb):
    c_start = CMP_STRIDE * np.arange(nc)[:, None]
    s_start = SLC_LEN * np.arange(nb)[None, :]
    ov = ((c_start < s_start + SLC_LEN) & (c_start + CMP_LEN > s_start)).astype(np.float32)
    ov[nc - 1:] = 0.0
    out = np.zeros((nc + CMP_WIN, nb), np.float32)
    out[CMP_PAD:CMP_PAD + nc] = ov
    return jnp.asarray(ov.T, dtype=BF16), jnp.asarray(out)


def _block_diag(w):
    nblk, bs, _ = w.shape
    eye = jnp.eye(nblk, dtype=w.dtype)
    return (eye[:, None, :, None] * w[:, :, None, :]).reshape(nblk * bs, nblk * bs)


def _layer(x, c8, w_ada, b_ada, norm1_w, w_in, cmp_pos, cmp_k_w1, cmp_k_w2, cmp_v_w1, cmp_v_w2, rel_table,
           conv_w, conv_b, lru_wa, lru_ba, lru_wx, lru_bx, lru_lambda, gnorm_attn_w, gnorm_rnn_w, w_out,
           norm2_w, w_router, b_router, w1, w3, w2, ws1, ws3, ws2, final_norm_w):
    B, S, D = x.shape
    N = B * S
    E = w_router.shape[1]
    d_attn = N_HEADS * HEAD_DIM
    d_rnn = D - d_attn
    assert S % 512 == 0 and S // SLC_LEN <= 128

    mod = _ada(c8, w_ada, b_ada[None, :])[:B].reshape(B, 6, D)

    n_kv_cols = 6 * N_KV * HEAD_DIM
    n_gate = N_BRANCH * N_HEADS
    wq, wkv, wg, wxr, wgr = jnp.split(w_in, np.cumsum([d_attn, n_kv_cols, n_gate, d_rnn]).tolist(), axis=1)
    wg = jnp.pad(wg.reshape(D, N_KV, GQA * N_BRANCH), ((0, 0), (0, 0), (0, 128 - GQA * N_BRANCH))).reshape(D, 256)
    w_pad = jnp.concatenate([wq, wkv, wg, wxr, wgr], axis=1).astype(BF16)
    t_tab, w_tab, q_tail = _bias_tables(rel_table)
    q_t, kx, vx_t, kvc, gates_t, xr, gr = _inproj(x, mod, norm1_w[None, :], w_pad, q_tail)

    nc = S // CMP_STRIDE
    kvc16 = kvc.reshape(B, 4, nc, CMP_STRIDE * HEAD_DIM)
    pos2 = cmp_pos.reshape(2, CMP_STRIDE * HEAD_DIM)
    kvc_pad, kvc_t = _compress(kvc16, pos2, jnp.stack([cmp_k_w1, cmp_v_w1]), jnp.stack([cmp_k_w2, cmp_v_w2]))
    o_attn = _attn(q_t, kx, vx_t, kvc_pad, kvc_t, gates_t, t_tab, w_tab, *_overlap_tables(nc, 128))

    y_rnn = _rglru(xr, gr, conv_w.reshape(CONV_WIDTH, d_rnn), conv_b[None, :], _block_diag(lru_wa).astype(BF16),
                   lru_ba[None, :], _block_diag(lru_wx).astype(BF16), lru_bx[None, :], lru_lambda[None, :],
                   gnorm_rnn_w[None, :])

    x1, h2, h2t, s_t = _outproj(o_attn, y_rnn, x, mod, gnorm_attn_w[None, :], w_out.astype(BF16), norm2_w[None, :],
                                w_router.T)

    eid_t, wgt_t, rank_t, counts = _route(s_t, b_router[:, None])
    counts = counts[:, 0].astype(I32)
    off = jnp.cumsum(counts) - counts
    dest_t = _dest(eid_t, rank_t, off.astype(F32)[:, None])
    h2f = h2.reshape(N, D)
    xs = _dispatch(dest_t, h2t, N * TOP_K)
    ys = _experts(_expert_items(counts, N * TOP_K), xs, w1, w3, w2)
    out = _combine(dest_t, ys, wgt_t.T, h2f, x1.reshape(N, D), mod, ws1.astype(BF16), ws3.astype(BF16),
                   ws2.astype(BF16), final_norm_w[None, :], S)
    return out.reshape(B, S, D)


def kernel(x, c, w_ada, b_ada, norm1_w, w_in, cmp_pos, cmp_k_w1, cmp_k_w2, cmp_v_w1, cmp_v_w2, rel_table, conv_w, conv_b, lru_wa, lru_ba, lru_wx, lru_bx, lru_lambda, gnorm_attn_w, gnorm_rnn_w, w_out, norm2_w, w_router, b_router, w1, w3, w2, ws1, ws3, ws2, final_norm_w):
    assert w_ada.shape[0] == 1
    c8 = jnp.pad(c, ((0, 8 - c.shape[0]), (0, 0)))
    return _layer(x, c8, w_ada[0], b_ada[0], norm1_w[0], w_in[0], cmp_pos[0], cmp_k_w1[0], cmp_k_w2[0],
                  cmp_v_w1[0], cmp_v_w2[0], rel_table, conv_w[0], conv_b[0], lru_wa[0], lru_ba[0], lru_wx[0],
                  lru_bx[0], lru_lambda[0], gnorm_attn_w[0], gnorm_rnn_w[0], w_out[0], norm2_w[0], w_router[0],
                  b_router[0], w1[0], w3[0], w2[0], ws1[0], ws3[0], ws2[0], final_norm_w)
```

```python
import functools
import math

import jax
import jax.numpy as jnp
import numpy as np
from jax import lax
from jax.experimental import pallas as pl
from jax.experimental.pallas import tpu as pltpu

F32 = jnp.float32
BF16 = jnp.bfloat16
I32 = jnp.int32

HEAD_DIM = 64
N_HEADS = 8
N_KV = 2
GQA = N_HEADS // N_KV
N_BRANCH = 3
CONV_WIDTH = 4
LRU_C = 8.0
CMP_LEN = 32
CMP_STRIDE = 16
SLC_LEN = 64
SLC_TOP = 16
WINDOW = 512
QB = 128
REL_BUCKETS = 32
REL_MAX_DIST = 1024
N_EXPERT_GROUPS = 8
TOP_GROUPS = 4
TOP_K = 8
ROUTED_SCALE = 2.5
MOE_BLOCK = 256
ROW_TILE = 8
EPS = 1e-6
NEG = -1e30
LOG2E = math.log2(math.e)
CMP_PAD = 120
CMP_WIN = 128
N_BIAS_TILES = 11
TILE_MASKED = N_BIAS_TILES
TILE_WINDOW_EDGE = N_BIAS_TILES + 1
FAR_TILE_DIST = 8
FAR_CHUNK = 4
VMEM_LIMIT = 52 * 1024 * 1024


def _dot(a, b, **kw):
    return jnp.dot(a, b, preferred_element_type=F32, **kw)


def _dot_nt(a, b, **kw):
    return lax.dot_general(a, b, (((1,), (1,)), ((), ())), preferred_element_type=F32, **kw)


def _gelu(x):
    return 0.5 * x * (1.0 + jnp.tanh(math.sqrt(2.0 / math.pi) * (x + 0.044715 * (x * x * x))))


def _sigmoid(x):
    return 1.0 / (1.0 + jnp.exp(-x))


def _rms(x):
    return x * lax.rsqrt(jnp.mean(x * x, axis=-1, keepdims=True) + EPS)


def _params(*sem):
    return pltpu.CompilerParams(dimension_semantics=sem, vmem_limit_bytes=VMEM_LIMIT)


def _ada_kernel(c_ref, w_ref, b_ref, o_ref):
    c = c_ref[...]
    a = c * _sigmoid(c)
    o_ref[...] = _dot(a, w_ref[...], precision=lax.Precision.HIGHEST) + b_ref[...]


def _ada(c8, w, b):
    d, n = w.shape
    tn = 1536
    return pl.pallas_call(
        _ada_kernel,
        grid=(n // tn,),
        in_specs=[pl.BlockSpec((8, d), lambda j: (0, 0)),
                  pl.BlockSpec((d, tn), lambda j: (0, j)),
                  pl.BlockSpec((1, tn), lambda j: (0, j))],
        out_specs=pl.BlockSpec((8, tn), lambda j: (0, j)),
        out_shape=jax.ShapeDtypeStruct((8, n), F32),
        compiler_params=_params("parallel"),
        name="ada",
    )(c8, w, b)


def _inproj_kernel(x_ref, mod_ref, nw_ref, w_ref, qt_ref, q_ref, kx_ref, vx_ref, kvc_ref, g_ref, xr_ref, gr_ref):
    h = _rms(x_ref[0]) * nw_ref[...]
    h = h * (1.0 + mod_ref[0, 1:2, :]) + mod_ref[0, 0:1, :]
    p = _dot(h.astype(BF16), w_ref[...])
    tm = p.shape[0]
    dq = N_HEADS * HEAD_DIM
    lane = lax.broadcasted_iota(I32, (tm, HEAD_DIM), 1)
    k_tail = jnp.where(lane < 2, 1.0, 0.0)
    v_tail = jnp.where(lane < 1, 1.0, 0.0)
    for hh in range(N_HEADS):
        qh = p[:, hh * HEAD_DIM:(hh + 1) * HEAD_DIM] * (HEAD_DIM ** -0.5 * LOG2E)
        q_tail = jnp.broadcast_to(qt_ref[hh:hh + 1, :], (tm, HEAD_DIM))
        q_t = jnp.concatenate([qh, q_tail], axis=1).T.astype(BF16)
        for blk in range(tm // QB):
            col = (blk * GQA + hh % GQA) * QB
            q_ref[0, hh // GQA, :, col:col + QB] = q_t[:, blk * QB:(blk + 1) * QB]
    for j in range(6):
        for g in range(N_KV):
            col = dq + (j * N_KV + g) * HEAD_DIM
            piece = p[:, col:col + HEAD_DIM]
            if j < 2:
                kvc_ref[0, j * N_KV + g] = piece
            elif j % 2 == 0:
                kx_ref[0, (j // 2 - 1) * N_KV + g] = jnp.concatenate([piece, k_tail], axis=1).astype(BF16)
            else:
                v_t = jnp.concatenate([piece, v_tail], axis=1).T.astype(BF16)
                for blk in range(tm // QB):
                    vx_ref[0, (j // 2 - 1) * N_KV + g, blk] = v_t[:, blk * QB:(blk + 1) * QB]
    c0 = dq + 6 * N_KV * HEAD_DIM
    for g in range(N_KV):
        g_ref[0, g] = _sigmoid(p[:, c0 + g * 128:c0 + (g + 1) * 128]).T[0:16, :]
    xr_ref[0] = p[:, c0 + 256:c0 + 768]
    gr_ref[0] = p[:, c0 + 768:c0 + 1280]


def _inproj(x, mod, nw, w_pad, q_tail, tm=256):
    B, S, D = x.shape
    ncol = w_pad.shape[1]
    heads = lambda n, w: pl.BlockSpec((1, n, tm, w), lambda b, t: (b, 0, t, 0))
    return pl.pallas_call(
        _inproj_kernel,
        grid=(B, S // tm),
        in_specs=[pl.BlockSpec((1, tm, D), lambda b, t: (b, t, 0)),
                  pl.BlockSpec((1, 6, D), lambda b, t: (b, 0, 0)),
                  pl.BlockSpec((1, D), lambda b, t: (0, 0)),
                  pl.BlockSpec((D, ncol), lambda b, t: (0, 0)),
                  pl.BlockSpec(q_tail.shape, lambda b, t: (0, 0))],
        out_specs=[pl.BlockSpec((1, N_KV, 2 * HEAD_DIM, tm * GQA), lambda b, t: (b, 0, 0, t)),
                   heads(4, 2 * HEAD_DIM),
                   pl.BlockSpec((1, 4, tm // QB, 2 * HEAD_DIM, QB), lambda b, t: (b, 0, t, 0, 0)),
                   heads(4, HEAD_DIM),
                   pl.BlockSpec((1, N_KV, 16, tm), lambda b, t: (b, 0, 0, t)),
                   pl.BlockSpec((1, tm, 512), lambda b, t: (b, t, 0)),
                   pl.BlockSpec((1, tm, 512), lambda b, t: (b, t, 0))],
        out_shape=[jax.ShapeDtypeStruct((B, N_KV, 2 * HEAD_DIM, S * GQA), BF16),
                   jax.ShapeDtypeStruct((B, 4, S, 2 * HEAD_DIM), BF16),
                   jax.ShapeDtypeStruct((B, 4, S // QB, 2 * HEAD_DIM, QB), BF16),
                   jax.ShapeDtypeStruct((B, 4, S, HEAD_DIM), F32),
                   jax.ShapeDtypeStruct((B, N_KV, 16, S), F32),
                   jax.ShapeDtypeStruct((B, S, 512), F32),
                   jax.ShapeDtypeStruct((B, S, 512), F32)],
        compiler_params=_params("parallel", "parallel"),
        name="inproj",
    )(x, mod, nw, w_pad, q_tail)


def _compress_kernel(x_ref, pos_ref, w1_ref, w2_ref, o_ref, ot_ref):
    x = x_ref[0, 0]
    nc = x.shape[0]
    half = CMP_STRIDE * HEAD_DIM
    kv = pl.program_id(1) // N_KV
    a = _dot((x + pos_ref[0:1, :]).astype(BF16), w1_ref[kv, 0:half, :].astype(BF16))
    b = _dot((x + pos_ref[1:2, :]).astype(BF16), w1_ref[kv, half:2 * half, :].astype(BF16))
    hid = _gelu(a + pltpu.roll(b, nc - 1, 0))
    out = _dot(hid.astype(BF16), w2_ref[kv].astype(BF16))
    row = lax.broadcasted_iota(I32, out.shape, 0)
    out = jnp.where(row < nc - 1, out, 0.0)
    lane = lax.broadcasted_iota(I32, (nc, HEAD_DIM), 1)
    out = jnp.concatenate([out, jnp.where(lane < 2, 1.0, 0.0)], axis=1)
    o_ref[0, 0, 0:CMP_PAD, :] = jnp.zeros((CMP_PAD, 2 * HEAD_DIM), F32)
    o_ref[0, 0, CMP_PAD:CMP_PAD + nc, :] = out
    o_ref[0, 0, CMP_PAD + nc:, :] = jnp.zeros((CMP_WIN - CMP_PAD, 2 * HEAD_DIM), F32)
    ot_ref[0, 0] = out.T


def _compress(kvc, pos2, w1, w2):
    B, _, NC, W = kvc.shape
    return pl.pallas_call(
        _compress_kernel,
        grid=(B, 4),
        in_specs=[pl.BlockSpec((1, 1, NC, W), lambda b, i: (b, i, 0, 0)),
                  pl.BlockSpec((2, W), lambda b, i: (0, 0)),
                  pl.BlockSpec((2, 2 * W, HEAD_DIM), lambda b, i: (0, 0, 0)),
                  pl.BlockSpec((2, HEAD_DIM, HEAD_DIM), lambda b, i: (0, 0, 0))],
        out_specs=[pl.BlockSpec((1, 1, NC + CMP_WIN, 2 * HEAD_DIM), lambda b, i: (b, i, 0, 0)),
                   pl.BlockSpec((1, 1, 2 * HEAD_DIM, NC), lambda b, i: (b, i, 0, 0))],
        out_shape=[jax.ShapeDtypeStruct((B, 4, NC + CMP_WIN, 2 * HEAD_DIM), F32),
                   jax.ShapeDtypeStruct((B, 4, 2 * HEAD_DIM, NC), F32)],
        compiler_params=_params("parallel", "parallel"),
        name="compress",
    )(kvc, pos2, w1, w2)


def _attn_kernel(q_ref, ks_ref, vs_ref, kw_ref, vw_ref, kc_ref, vc_ref, vct_ref, g_ref, t_ref, w_ref, ovt_ref, ov_ref,
                 o_ref, m_scr, acc_scr, sel_scr, sa_scr, sb_scr):
    qb = pl.program_id(2)
    R = GQA * QB
    q = q_ref[0, 0]
    nc = vct_ref.shape[3]
    nb = ov_ref.shape[1]
    n_top = min(SLC_TOP, ks_ref.shape[2] // SLC_LEN)

    def heads_sum(x):
        out = x[:, 0:QB]
        for hq in range(1, GQA):
            out = out + x[:, hq * QB:(hq + 1) * QB]
        return out

    w0 = pl.multiple_of(qb * 8, 8)
    s_far = _dot(kc_ref[0, 0, CMP_PAD:CMP_PAD + nc, :].astype(BF16), q)
    n_io = lax.broadcasted_iota(I32, (nc, R), 0)
    s_far = jnp.where(n_io < qb * 8 - CMP_PAD, s_far, -jnp.inf)
    s_win = _dot(kc_ref[0, 0, pl.ds(w0, CMP_WIN), :].astype(BF16), q) + w_ref[0]
    j_w = lax.broadcasted_iota(I32, (CMP_WIN, R), 0)
    i_w = lax.broadcasted_iota(I32, (CMP_WIN, R), 1) & (QB - 1)
    dist_w = i_w - CMP_STRIDE * (j_w - CMP_PAD) - (CMP_LEN - 1)
    s_win = jnp.where((dist_w >= 0) & (j_w >= CMP_PAD - qb * 8), s_win, -jnp.inf)
    m = jnp.maximum(jnp.max(s_far, axis=0, keepdims=True), jnp.max(s_win, axis=0, keepdims=True))
    m = jnp.where(m == -jnp.inf, 0.0, m)
    e_far = jnp.exp2(s_far - m)
    e_win = jnp.exp2(s_win - m)
    l = jnp.sum(e_far, axis=0, keepdims=True) + jnp.sum(e_win, axis=0, keepdims=True)
    inv = 1.0 / jnp.maximum(l, 1e-30)
    p_far = e_far * inv
    p_win = e_win * inv
    vcw_t = vc_ref[0, 0, pl.ds(w0, CMP_WIN), :].T
    o_c = _dot(vct_ref[0, 0].astype(BF16), p_far.astype(BF16)) + _dot(vcw_t.astype(BF16), p_win.astype(BF16))
    ov_win_t = ov_ref[pl.ds(w0, CMP_WIN), :].T
    sc_t = (_dot(ovt_ref[...], heads_sum(p_far).astype(BF16))
            + _dot(ov_win_t.astype(BF16), heads_sum(p_win).astype(BF16)))

    blk = lax.broadcasted_iota(I32, (nb, QB), 0)
    t_io = lax.broadcasted_iota(I32, (nb, QB), 1)
    cur = qb * (QB // SLC_LEN) + t_io // SLC_LEN
    forced = (blk == 0) | (blk == cur) | (blk == cur - 1)
    val = jnp.where(forced, jnp.inf, jnp.where(blk <= cur, sc_t, -jnp.inf))
    sel_t = jnp.zeros((nb, QB), F32)
    for _ in range(n_top):
        mx = jnp.max(val, axis=0, keepdims=True)
        hit = (val == mx) & (val > -jnp.inf)
        first = jnp.min(jnp.where(hit, blk, nb), axis=0, keepdims=True)
        pick = blk == first
        sel_t = jnp.where(pick, 1.0, sel_t)
        val = jnp.where(pick, -jnp.inf, val)
    sel_scr[...] = jnp.where(sel_t > 0.5, 0.0, NEG)

    m_scr[...] = jnp.full(m_scr.shape, NEG, F32)
    acc_scr[...] = jnp.zeros(acc_scr.shape, F32)
    blocks_per_chunk = FAR_CHUNK * QB // SLC_LEN

    chunk = FAR_CHUNK * QB

    def qk(c):
        return _dot(ks_ref[0, 0, pl.ds(pl.multiple_of(c * chunk, chunk), chunk), :], q)

    def soft(c, s):
        rows = sel_scr[pl.ds(pl.multiple_of(c * blocks_per_chunk, blocks_per_chunk), blocks_per_chunk), :]
        drop = jnp.concatenate([jnp.broadcast_to(rows[r:r + 1, :], (SLC_LEN, QB)) for r in range(blocks_per_chunk)],
                               axis=0)
        s = s + jnp.concatenate([drop] * GQA, axis=1)
        m_old = m_scr[...]
        m_new = jnp.maximum(m_old, jnp.max(s, axis=0, keepdims=True))
        p = jnp.exp2(s - m_new)
        v_t = jnp.concatenate([vs_ref[0, 0, c * FAR_CHUNK + j] for j in range(FAR_CHUNK)], axis=1)
        acc_scr[...] = jnp.exp2(m_old - m_new) * acc_scr[...] + _dot(v_t, p.astype(BF16))
        m_scr[...] = m_new

    n_far = jnp.maximum(qb - (FAR_TILE_DIST - 1), 0) // FAR_CHUNK
    last_far = jnp.maximum(n_far - 1, 0)
    sa_scr[...] = qk(0)

    def pair_body(j, carry):
        sb_scr[...] = qk(2 * j + 1)
        soft(2 * j, sa_scr[...])
        sa_scr[...] = qk(jnp.minimum(2 * j + 2, last_far))
        soft(2 * j + 1, sb_scr[...])
        return carry

    lax.fori_loop(0, n_far // 2, pair_body, 0)

    @pl.when(n_far % 2 == 1)
    def _():
        soft(last_far, sa_scr[...])

    def near_body(c, carry):
        tiles = []
        for j in range(FAR_CHUNK):
            delta = qb - (c * FAR_CHUNK + j)
            tiles.append(t_ref[0, jnp.where(delta < 0, TILE_MASKED, delta)])
        soft(c, qk(c) + jnp.concatenate(tiles, axis=0))
        return carry

    lax.fori_loop(n_far, qb // FAR_CHUNK + 1, near_body, 0)
    acc = acc_scr[...]
    o_s = acc[0:HEAD_DIM, :] / acc[HEAD_DIM:HEAD_DIM + 1, :]

    n_wt = WINDOW // QB + 1
    k_t, v_t, b_t = [], [], []
    for j in range(n_wt):
        kb = qb - (n_wt - 1) + j
        kb0 = jnp.maximum(kb, 0)
        k_t.append(kw_ref[0, 0, pl.ds(pl.multiple_of(kb0 * QB, QB), QB), :])
        v_t.append(vw_ref[0, 0, kb0])
        b_t.append(t_ref[0, jnp.where(kb < 0, TILE_MASKED, TILE_WINDOW_EDGE if j == 0 else n_wt - 1 - j)])
    s = _dot(jnp.concatenate(k_t, axis=0), q) + jnp.concatenate(b_t, axis=0)
    p = jnp.exp2(s - jnp.max(s, axis=0, keepdims=True))
    acc = _dot(jnp.concatenate(v_t, axis=1), p.astype(BF16))
    o_w = acc[0:HEAD_DIM, :] / acc[HEAD_DIM:HEAD_DIM + 1, :]

    gates = g_ref[0, 0]
    outs = []
    for hq in range(GQA):
        cols = slice(hq * QB, (hq + 1) * QB)
        c = hq * N_BRANCH
        outs.append(gates[c:c + 1, :] * o_c[0:HEAD_DIM, cols] + gates[c + 1:c + 2, :] * o_s[:, cols]
                    + gates[c + 2:c + 3, :] * o_w[:, cols])
    o_ref[0] = jnp.concatenate(outs, axis=0).T


def _attn(q_t, kx, vx_t, kvc, kvc_t, gates_t, t_tab, w_tab, ov_t, ov):
    B, _, W, _ = q_t.shape
    S = kx.shape[2]
    ncp = kvc.shape[2]
    nb = ov.shape[1]
    keys = lambda j: pl.BlockSpec((1, 1, S, W), lambda b, g, t: (b, j * N_KV + g, 0, 0))
    vals = lambda j: pl.BlockSpec((1, 1, S // QB, W, QB), lambda b, g, t: (b, j * N_KV + g, 0, 0, 0))
    cmp = lambda j: pl.BlockSpec((1, 1, ncp, W), lambda b, g, t: (b, j * N_KV + g, 0, 0))
    return pl.pallas_call(
        _attn_kernel,
        grid=(B, N_KV, S // QB),
        in_specs=[pl.BlockSpec((1, 1, W, GQA * QB), lambda b, g, t: (b, g, 0, t)),
                  keys(0), vals(0), keys(1), vals(1), cmp(0), cmp(1),
                  pl.BlockSpec((1, 1, W, ncp - CMP_WIN), lambda b, g, t: (b, N_KV + g, 0, 0)),
                  pl.BlockSpec((1, 1, 16, QB), lambda b, g, t: (b, g, 0, t)),
                  pl.BlockSpec((1,) + t_tab.shape[1:], lambda b, g, t: (g, 0, 0, 0)),
                  pl.BlockSpec((1, CMP_WIN, GQA * QB), lambda b, g, t: (g, 0, 0)),
                  pl.BlockSpec(ov_t.shape, lambda b, g, t: (0, 0)),
                  pl.BlockSpec(ov.shape, lambda b, g, t: (0, 0))],
        out_specs=pl.BlockSpec((1, QB, GQA * HEAD_DIM), lambda b, g, t: (b, t, g)),
        out_shape=jax.ShapeDtypeStruct((B, S, N_HEADS * HEAD_DIM), F32),
        scratch_shapes=[pltpu.VMEM((1, GQA * QB), F32), pltpu.VMEM((W, GQA * QB), F32), pltpu.VMEM((nb, QB), F32),
                        pltpu.VMEM((FAR_CHUNK * QB, GQA * QB), F32), pltpu.VMEM((FAR_CHUNK * QB, GQA * QB), F32)],
        compiler_params=_params("parallel", "parallel", "arbitrary"),
        name="attn",
    )(q_t, kx, vx_t, kx, vx_t, kvc, kvc, kvc_t, gates_t, t_tab, w_tab, ov_t, ov)


def _rglru_kernel(xr_ref, gr_ref, cw_ref, cb_ref, wa_ref, ba_ref, wx_ref, bx_ref, lam_ref, gw_ref, o_ref,
                  xbuf, hprev, a_scr, u_scr, h_scr):
    ts = xr_ref.shape[1]
    C = xr_ref.shape[2]

    @pl.when(pl.program_id(1) == 0)
    def _():
        xbuf[0:8, :] = jnp.zeros((8, C), F32)
        hprev[...] = jnp.zeros(hprev.shape, F32)

    xbuf[8:8 + ts, :] = xr_ref[0]
    xc = cb_ref[...] + jnp.zeros((ts, C), F32)
    for j in range(CONV_WIDTH):
        xc = xc + cw_ref[j:j + 1, :] * xbuf[pl.ds(8 - (CONV_WIDTH - 1) + j, ts), :]
    xbuf[0:8, :] = xbuf[ts:ts + 8, :]

    xcb = xc.astype(BF16)
    r = _sigmoid(_dot(xcb, wa_ref[...]) + ba_ref[...])
    i = _sigmoid(_dot(xcb, wx_ref[...]) + bx_ref[...])
    z = -lam_ref[...]
    softplus = jnp.maximum(z, 0.0) + jnp.log(1.0 + jnp.exp(-jnp.abs(z)))
    log_a = -LRU_C * r * softplus
    a_scr[...] = jnp.exp(log_a)
    u_scr[...] = jnp.sqrt(1.0 - jnp.exp(2.0 * log_a)) * (i * xc)

    row = lax.broadcasted_iota(I32, (8, C), 0)

    def body(k, h):
        r0 = pl.multiple_of(k * 8, 8)
        a = a_scr[pl.ds(r0, 8), :]
        b = u_scr[pl.ds(r0, 8), :]
        for s in (1, 2, 4):
            keep = row >= s
            b = jnp.where(keep, a * pltpu.roll(b, s, 0) + b, b)
            a = jnp.where(keep, a * pltpu.roll(a, s, 0), a)
        hh = a * h + b
        h_scr[pl.ds(r0, 8), :] = hh
        return jnp.broadcast_to(hh[7:8, :], (8, C))

    hprev[...] = lax.fori_loop(0, ts // 8, body, hprev[...])
    out = h_scr[...] * _gelu(gr_ref[0])
    o_ref[0] = (_rms(out) * gw_ref[...]).astype(BF16)


def _rglru(xr, gr, cw, cb, wa_bd, ba, wx_bd, bx, lam, gw, ts=512):
    B, S, C = xr.shape
    ts = min(ts, S)
    vec = pl.BlockSpec((1, C), lambda b, t: (0, 0))
    mat = pl.BlockSpec((C, C), lambda b, t: (0, 0))
    seq = pl.BlockSpec((1, ts, C), lambda b, t: (b, t, 0))
    return pl.pallas_call(
        _rglru_kernel,
        grid=(B, S // ts),
        in_specs=[seq, seq, pl.BlockSpec((CONV_WIDTH, C), lambda b, t: (0, 0)), vec, mat, vec, mat, vec, vec, vec],
        out_specs=seq,
        out_shape=jax.ShapeDtypeStruct((B, S, C), BF16),
        scratch_shapes=[pltpu.VMEM((ts + 8, C), F32), pltpu.VMEM((8, C), F32), pltpu.VMEM((ts, C), F32),
                        pltpu.VMEM((ts, C), F32), pltpu.VMEM((ts, C), F32)],
        compiler_params=_params("parallel", "arbitrary"),
        name="rglru",
    )(xr, gr, cw, cb, wa_bd, ba, wx_bd, bx, lam, gw)


def _to_row_tiles(ref, x):
    rows = x.shape[0]
    for s in range(ROW_TILE):
        ref[pl.ds(s, rows, stride=ROW_TILE), :] = x[:, s * 128:(s + 1) * 128]


def _from_row_tiles(ref, rows, start=0):
    return jnp.concatenate([ref[pl.ds(start + s, rows, stride=ROW_TILE), :] for s in range(ROW_TILE)], axis=1)


def _outproj_kernel(oa_ref, yr_ref, x_ref, mod_ref, gaw_ref, wo_ref, n2_ref, wr_ref, x1_ref, h2_ref, h2t_ref, st_ref):
    da = oa_ref.shape[2]
    ya = (_rms(oa_ref[0]) * gaw_ref[...]).astype(BF16)
    mix = _dot(ya, wo_ref[0:da, :]) + _dot(yr_ref[0], wo_ref[da:, :])
    x1 = x_ref[0] + mod_ref[0, 2:3, :] * mix
    x1_ref[0] = x1
    h2 = (_rms(x1) * n2_ref[...]) * (1.0 + mod_ref[0, 4:5, :]) + mod_ref[0, 3:4, :]
    h2_ref[0] = h2
    _to_row_tiles(h2t_ref, h2)
    st_ref[...] = _sigmoid(_dot_nt(wr_ref[...], h2, precision=lax.Precision.HIGHEST))


def _outproj(oa, yr, x, mod, gaw, wo, n2, wr_t, tm=256):
    B, S, D = x.shape
    E = wr_t.shape[0]
    nt = S // tm
    row = lambda w: pl.BlockSpec((1, tm, w), lambda b, t: (b, t, 0))
    return pl.pallas_call(
        _outproj_kernel,
        grid=(B, nt),
        in_specs=[row(oa.shape[2]), row(yr.shape[2]), row(D),
                  pl.BlockSpec((1, 6, D), lambda b, t: (b, 0, 0)),
                  pl.BlockSpec((1, oa.shape[2]), lambda b, t: (0, 0)),
                  pl.BlockSpec(wo.shape, lambda b, t: (0, 0)),
                  pl.BlockSpec((1, D), lambda b, t: (0, 0)),
                  pl.BlockSpec((E, D), lambda b, t: (0, 0))],
        out_specs=[row(D), row(D), pl.BlockSpec((tm * ROW_TILE, D // ROW_TILE), lambda b, t: (b * nt + t, 0)),
                   pl.BlockSpec((E, tm), lambda b, t: (0, b * nt + t))],
        out_shape=[jax.ShapeDtypeStruct((B, S, D), F32), jax.ShapeDtypeStruct((B, S, D), F32),
                   jax.ShapeDtypeStruct((B * S * ROW_TILE, D // ROW_TILE), F32),
                   jax.ShapeDtypeStruct((E, B * S), F32)],
        compiler_params=_params("parallel", "parallel"),
        name="outproj",
    )(oa, yr, x, mod, gaw, wo, n2, wr_t)


def _route_kernel(s_ref, b_ref, eid_ref, wgt_ref, rank_ref, cnt_ref, carry):
    E, tn = s_ref.shape
    per = E // N_EXPERT_GROUPS

    @pl.when(pl.program_id(0) == 0)
    def _():
        carry[...] = jnp.zeros(carry.shape, F32)

    s = s_ref[...]
    s_sel = s + b_ref[...]
    eidx = lax.broadcasted_iota(I32, (E, tn), 0)

    grp = []
    for gi in range(N_EXPERT_GROUPS):
        xg = s_sel[gi * per:(gi + 1) * per, :]
        ig = lax.broadcasted_iota(I32, (per, tn), 0)
        m1 = jnp.max(xg, axis=0, keepdims=True)
        f1 = jnp.min(jnp.where(xg == m1, ig, per), axis=0, keepdims=True)
        m2 = jnp.max(jnp.where(ig == f1, -jnp.inf, xg), axis=0, keepdims=True)
        grp.append(m1 + m2)
    val = []
    for gi in range(N_EXPERT_GROUPS):
        rank = jnp.zeros((1, tn), I32)
        for gj in range(N_EXPERT_GROUPS):
            if gj == gi:
                continue
            ahead = (grp[gj] > grp[gi]) | ((grp[gj] == grp[gi]) & (gj < gi))
            rank = rank + ahead.astype(I32)
        val.append(jnp.where(rank < TOP_GROUPS, s_sel[gi * per:(gi + 1) * per, :], -jnp.inf))
    val = jnp.concatenate(val, axis=0)

    eids, tops = [], []
    hot = jnp.zeros((E, tn), F32)
    for _ in range(TOP_K):
        mx = jnp.max(val, axis=0, keepdims=True)
        first = jnp.min(jnp.where(val == mx, eidx, E), axis=0, keepdims=True)
        pick = eidx == first
        eids.append(first)
        tops.append(jnp.sum(jnp.where(pick, s, 0.0), axis=0, keepdims=True))
        hot = jnp.where(pick, 1.0, hot)
        val = jnp.where(pick, -jnp.inf, val)
    denom = tops[0]
    for t in tops[1:]:
        denom = denom + t

    ti = lax.broadcasted_iota(I32, (tn, tn), 0)
    tj = lax.broadcasted_iota(I32, (tn, tn), 1)
    upper = jnp.where(ti < tj, 1.0, 0.0).astype(BF16)
    before = _dot(hot.astype(BF16), upper) + carry[...]
    for k in range(TOP_K):
        eid_ref[k:k + 1, :] = eids[k]
        wgt_ref[k:k + 1, :] = ROUTED_SCALE * tops[k] / denom
        rank_ref[k:k + 1, :] = jnp.sum(jnp.where(eidx == eids[k], before, 0.0), axis=0, keepdims=True).astype(I32)
    carry[...] = carry[...] + jnp.sum(hot, axis=1, keepdims=True)
    cnt_ref[...] = carry[...]


def _route(s_t, b_col, tn=256):
    E, N = s_t.shape
    tn = min(tn, N)
    out = pl.BlockSpec((TOP_K, tn), lambda t: (0, t))
    return pl.pallas_call(
        _route_kernel,
        grid=(N // tn,),
        in_specs=[pl.BlockSpec((E, tn), lambda t: (0, t)), pl.BlockSpec((E, 1), lambda t: (0, 0))],
        out_specs=[out, out, out, pl.BlockSpec((E, 1), lambda t: (0, 0))],
        out_shape=[jax.ShapeDtypeStruct((TOP_K, N), I32), jax.ShapeDtypeStruct((TOP_K, N), F32),
                   jax.ShapeDtypeStruct((TOP_K, N), I32), jax.ShapeDtypeStruct((E, 1), F32)],
        scratch_shapes=[pltpu.VMEM((E, 1), F32)],
        compiler_params=_params("arbitrary"),
        name="route",
    )(s_t, b_col)


def _dest_kernel(eid_ref, rank_ref, off_ref, o_ref):
    E = off_ref.shape[0]
    tn = eid_ref.shape[1]
    eidx = lax.broadcasted_iota(I32, (E, tn), 0)
    for k in range(TOP_K):
        base = jnp.sum(jnp.where(eidx == eid_ref[k:k + 1, :], off_ref[...], 0.0), axis=0, keepdims=True)
        o_ref[k:k + 1, :] = base.astype(I32) + rank_ref[k:k + 1, :]


def _dest(eid_t, rank_t, off_col, tn=512):
    K, N = eid_t.shape
    tn = min(tn, N)
    blk = pl.BlockSpec((K, tn), lambda t: (0, t))
    return pl.pallas_call(
        _dest_kernel,
        grid=(N // tn,),
        in_specs=[blk, blk, pl.BlockSpec(off_col.shape, lambda t: (0, 0))],
        out_specs=blk,
        out_shape=jax.ShapeDtypeStruct((K, N), I32),
        compiler_params=_params("parallel"),
        name="dest",
    )(eid_t, rank_t, off_col)


def _row_tile(ref, r):
    return ref.at[pl.ds(pl.multiple_of(r * ROW_TILE, ROW_TILE), ROW_TILE)]


def _dispatch_kernel(dest_ref, h_ref, xs_hbm, sem):
    tn = h_ref.shape[0] // ROW_TILE

    def go(t, c):
        for k in range(TOP_K):
            pltpu.make_async_copy(_row_tile(h_ref, t), _row_tile(xs_hbm, dest_ref[k, t]), sem.at[0]).start()
        return c
    lax.fori_loop(0, tn, go, 0, unroll=4)
    for k in range(TOP_K):
        pltpu.make_async_copy(h_ref, xs_hbm.at[pl.ds(0, tn * ROW_TILE)], sem.at[0]).wait()


def _dispatch(dest_t, h2t, n_slots, tn=256):
    rows, W = h2t.shape
    tn = min(tn, rows // ROW_TILE)
    return pl.pallas_call(
        _dispatch_kernel,
        grid=(rows // ROW_TILE // tn,),
        in_specs=[pl.BlockSpec((TOP_K, tn), lambda t: (0, t), memory_space=pltpu.SMEM),
                  pl.BlockSpec((tn * ROW_TILE, W), lambda t: (t, 0))],
        out_specs=pl.BlockSpec(memory_space=pl.ANY),
        out_shape=jax.ShapeDtypeStruct((n_slots * ROW_TILE, W), F32),
        scratch_shapes=[pltpu.SemaphoreType.DMA((1,))],
        compiler_params=_params("arbitrary"),
        name="dispatch",
    )(dest_t, h2t)


def _experts_kernel(start_ref, count_ref, w1_ref, w3_ref, w2_ref, xs_hbm, ys_hbm, xbuf, ybuf, w1b, w3b, w2b, xsem, ysem):
    e = pl.program_id(0)
    start = start_ref[e]
    count = count_ref[e]
    n_chunks = (count + MOE_BLOCK - 1) // MOE_BLOCK

    @pl.when(e == 0)
    def _():
        xbuf[...] = jnp.zeros(xbuf.shape, F32)

    def chunk_rows(j):
        first = pl.multiple_of((start + j * MOE_BLOCK) * ROW_TILE, ROW_TILE)
        size = pl.multiple_of(jnp.minimum(MOE_BLOCK, count - j * MOE_BLOCK) * ROW_TILE, ROW_TILE)
        return first, size

    def x_copy(j, slot):
        first, size = chunk_rows(j)
        return pltpu.make_async_copy(xs_hbm.at[pl.ds(first, size)], xbuf.at[slot, pl.ds(0, size)], xsem.at[slot])

    def y_copy(j, slot):
        first, size = chunk_rows(j)
        return pltpu.make_async_copy(ybuf.at[slot, pl.ds(0, size)], ys_hbm.at[pl.ds(first, size)], ysem.at[slot])

    @pl.when(count > 0)
    def _():
        x_copy(0, 0).start()
        w1b[...] = w1_ref[0].astype(BF16)
        w3b[...] = w3_ref[0].astype(BF16)
        w2b[...] = w2_ref[0].astype(BF16)

        def body(j, carry):
            slot = j % 2
            x_copy(j, slot).wait()

            @pl.when(j + 1 < n_chunks)
            def _():
                x_copy(j + 1, 1 - slot).start()

            x = _from_row_tiles(xbuf.at[slot], MOE_BLOCK).astype(BF16)
            h1 = _dot(x, w1b[...])
            h3 = _dot(x, w3b[...])
            hid = (h1 * _sigmoid(h1)) * h3
            y = _dot(hid.astype(BF16), w2b[...])

            @pl.when(j >= 2)
            def _():
                y_copy(j - 2, slot).wait()

            _to_row_tiles(ybuf.at[slot], y)
            y_copy(j, slot).start()
            return carry

        lax.fori_loop(0, n_chunks, body, 0)

        @pl.when(n_chunks >= 2)
        def _():
            y_copy(n_chunks - 2, n_chunks % 2).wait()

        y_copy(n_chunks - 1, (n_chunks - 1) % 2).wait()


def _experts(starts, counts, xs, w1, w3, w2):
    rows_total, W = xs.shape
    E, D, De = w1.shape
    wspec = lambda shape: pl.BlockSpec(shape, lambda e, s, c: (e, 0, 0))
    grid_spec = pltpu.PrefetchScalarGridSpec(
        num_scalar_prefetch=2,
        grid=(E,),
        in_specs=[wspec((1, D, De)), wspec((1, D, De)), wspec((1, De, D)), pl.BlockSpec(memory_space=pl.ANY)],
        out_specs=pl.BlockSpec(memory_space=pl.ANY),
        scratch_shapes=[pltpu.VMEM((2, MOE_BLOCK * ROW_TILE, W), F32), pltpu.VMEM((2, MOE_BLOCK * ROW_TILE, W), F32),
                        pltpu.VMEM((D, De), BF16), pltpu.VMEM((D, De), BF16), pltpu.VMEM((De, D), BF16),
                        pltpu.SemaphoreType.DMA((2,)), pltpu.SemaphoreType.DMA((2,))],
    )
    return pl.pallas_call(
        _experts_kernel,
        grid_spec=grid_spec,
        out_shape=jax.ShapeDtypeStruct((rows_total, W), F32),
        compiler_params=_params("arbitrary"),
        name="experts",
    )(starts, counts, w1, w3, w2, xs)


def _combine_kernel(dest_ref, ys_hbm, w_ref, h2_ref, x1_ref, mod_ref, ws1_ref, ws3_ref, ws2_ref, fw_ref, o_ref,
                    ybuf, sem):
    tm = h2_ref.shape[0]

    def go(t, c):
        for k in range(TOP_K):
            pltpu.make_async_copy(_row_tile(ys_hbm, dest_ref[k, t]), _row_tile(ybuf.at[k], t), sem.at[0]).start()
        return c
    lax.fori_loop(0, tm, go, 0, unroll=4)

    h = h2_ref[...].astype(BF16)
    a = _dot(h, ws1_ref[...])
    hid = (a * _sigmoid(a)) * _dot(h, ws3_ref[...])
    shared = _dot(hid.astype(BF16), ws2_ref[...])

    for k in range(TOP_K):
        pltpu.make_async_copy(ys_hbm.at[pl.ds(0, tm * ROW_TILE)], ybuf.at[k], sem.at[0]).wait()
    routed = w_ref[:, 0:1] * _from_row_tiles(ybuf.at[0], tm)
    for k in range(1, TOP_K):
        routed = routed + w_ref[:, k:k + 1] * _from_row_tiles(ybuf.at[k], tm)
    x2 = x1_ref[...] + mod_ref[0, 5:6, :] * (routed + shared)
    o_ref[...] = _rms(x2) * fw_ref[...]


def _combine(dest_t, ys, wgt, h2, x1, mod, ws1, ws3, ws2, fw, seq, tm=256):
    N, D = h2.shape
    tm = min(tm, seq)
    nt = seq // tm
    row = pl.BlockSpec((tm, D), lambda t: (t, 0))
    full = lambda a: pl.BlockSpec(a.shape, lambda t: (0,) * a.ndim)
    return pl.pallas_call(
        _combine_kernel,
        grid=(N // tm,),
        in_specs=[pl.BlockSpec((TOP_K, tm), lambda t: (0, t), memory_space=pltpu.SMEM),
                  pl.BlockSpec(memory_space=pl.ANY),
                  pl.BlockSpec((tm, TOP_K), lambda t: (t, 0)),
                  row, row,
                  pl.BlockSpec((1, 6, D), lambda t: (t // nt, 0, 0)),
                  full(ws1), full(ws3), full(ws2), full(fw)],
        out_specs=row,
        out_shape=jax.ShapeDtypeStruct((N, D), F32),
        scratch_shapes=[pltpu.VMEM((TOP_K, tm * ROW_TILE, D // ROW_TILE), F32), pltpu.SemaphoreType.DMA((1,))],
        compiler_params=_params("arbitrary"),
        name="combine",
    )(dest_t, ys, wgt, h2, x1, mod, ws1, ws3, ws2, fw)


def _t5_bucket(dist):
    n = jnp.maximum(dist, 0)
    max_exact = REL_BUCKETS // 2
    nf = jnp.maximum(n, 1).astype(F32)
    large = max_exact + (jnp.log(nf / max_exact) / math.log(REL_MAX_DIST / max_exact)
                         * (REL_BUCKETS - max_exact)).astype(I32)
    large = jnp.minimum(large, REL_BUCKETS - 1)
    return jnp.where(n < max_exact, n, large)


def _bias_kernel(tbl_ref, far_ref, bt_ref, bw_ref, t_ref, w_ref):
    g = pl.program_id(0)
    d = pl.program_id(1)

    def lookup(bkt, head):
        out = jnp.full(bkt.shape, tbl_ref[0, head], F32)
        for b in range(1, REL_BUCKETS):
            out = jnp.where(bkt == b, tbl_ref[b, head], out)
        return out - far_ref[head]

    j = lax.broadcasted_iota(I32, (QB, QB), 0)
    i = lax.broadcasted_iota(I32, (QB, QB), 1)
    keep = ((d != 0) | (i >= j)) & (d != TILE_MASKED) & ((d != TILE_WINDOW_EDGE) | (j > i))
    for hq in range(GQA):
        t_ref[0, 0, :, hq * QB:(hq + 1) * QB] = jnp.where(keep, lookup(bt_ref[0], GQA * g + hq), NEG)

    @pl.when(d == 0)
    def _():
        for hq in range(GQA):
            w_ref[0, :, hq * QB:(hq + 1) * QB] = lookup(bw_ref[...], GQA * g + hq)


def _bias_tables(rel_table):
    tbl = rel_table.astype(F32) * LOG2E
    far_hi = tbl[REL_BUCKETS - 1].astype(BF16)
    far_lo = (tbl[REL_BUCKETS - 1] - far_hi.astype(F32)).astype(BF16)
    far = far_hi.astype(F32) + far_lo.astype(F32)
    q_tail = jnp.zeros((N_HEADS, HEAD_DIM), F32).at[:, 0].set(far_hi.astype(F32)).at[:, 1].set(far_lo.astype(F32))
    i = jnp.arange(QB)
    tile_dist = np.array(list(range(N_BIAS_TILES)) + [0, WINDOW // QB])
    d_t = (QB * jnp.asarray(tile_dist)[:, None, None] + i[None, None, :] - i[None, :, None])
    d_w = i[None, :] - CMP_STRIDE * (jnp.arange(CMP_WIN)[:, None] - CMP_PAD) - (CMP_LEN - 1)
    n_tiles = len(tile_dist)
    t_tab, w_tab = pl.pallas_call(
        _bias_kernel,
        grid=(N_KV, n_tiles),
        in_specs=[pl.BlockSpec(memory_space=pltpu.SMEM),
                  pl.BlockSpec(memory_space=pltpu.SMEM),
                  pl.BlockSpec((1, QB, QB), lambda g, d: (d, 0, 0)),
                  pl.BlockSpec((CMP_WIN, QB), lambda g, d: (0, 0))],
        out_specs=[pl.BlockSpec((1, 1, QB, GQA * QB), lambda g, d: (g, d, 0, 0)),
                   pl.BlockSpec((1, CMP_WIN, GQA * QB), lambda g, d: (g, 0, 0))],
        out_shape=[jax.ShapeDtypeStruct((N_KV, n_tiles, QB, GQA * QB), F32),
                   jax.ShapeDtypeStruct((N_KV, CMP_WIN, GQA * QB), F32)],
        compiler_params=_params("parallel", "arbitrary"),
        name="bias",
    )(tbl, far, _t5_bucket(d_t), _t5_bucket(d_w))
    far_min = min(FAR_TILE_DIST * QB - QB + 1, CMP_STRIDE * (CMP_PAD + 1) - (CMP_LEN - 1))
    max_exact = REL_BUCKETS // 2
    assert math.log(far_min / max_exact) / math.log(REL_MAX_DIST / max_exact) * max_exact > max_exact - 0.75
    return t_tab, w_tab, q_tail


def _overlap_tables(nc, nb):
    c_start = CMP_STRIDE * np.arange(nc)[:, None]
    s_start = SLC_LEN * np.arange(nb)[None, :]
    ov = ((c_start < s_start + SLC_LEN) & (c_start + CMP_LEN > s_start)).astype(np.float32)
    ov[nc - 1:] = 0.0
    out = np.zeros((nc + CMP_WIN, nb), np.float32)
    out[CMP_PAD:CMP_PAD + nc] = ov
    return jnp.asarray(ov.T, dtype=BF16), jnp.asarray(out)


def _block_diag(w):
    nblk, bs, _ = w.shape
    eye = jnp.eye(nblk, dtype=w.dtype)
    return (eye[:, None, :, None] * w[:, :, None, :]).reshape(nblk * bs, nblk * bs)


def _layer(x, c8, w_ada, b_ada, norm1_w, w_in, cmp_pos, cmp_k_w1, cmp_k_w2, cmp_v_w1, cmp_v_w2, rel_table,
           conv_w, conv_b, lru_wa, lru_ba, lru_wx, lru_bx, lru_lambda, gnorm_attn_w, gnorm_rnn_w, w_out,
           norm2_w, w_router, b_router, w1, w3, w2, ws1, ws3, ws2, final_norm_w):
    B, S, D = x.shape
    N = B * S
    E = w_router.shape[1]
    d_attn = N_HEADS * HEAD_DIM
    d_rnn = D - d_attn
    assert S % 512 == 0 and S // SLC_LEN <= 128

    mod = _ada(c8, w_ada, b_ada[None, :])[:B].reshape(B, 6, D)

    n_kv_cols = 6 * N_KV * HEAD_DIM
    n_gate = N_BRANCH * N_HEADS
    wq, wkv, wg, wxr, wgr = jnp.split(w_in, np.cumsum([d_attn, n_kv_cols, n_gate, d_rnn]).tolist(), axis=1)
    wg = jnp.pad(wg.reshape(D, N_KV, GQA * N_BRANCH), ((0, 0), (0, 0), (0, 128 - GQA * N_BRANCH))).reshape(D, 256)
    w_pad = jnp.concatenate([wq, wkv, wg, wxr, wgr], axis=1).astype(BF16)
    t_tab, w_tab, q_tail = _bias_tables(rel_table)
    q_t, kx, vx_t, kvc, gates_t, xr, gr = _inproj(x, mod, norm1_w[None, :], w_pad, q_tail)

    nc = S // CMP_STRIDE
    kvc16 = kvc.reshape(B, 4, nc, CMP_STRIDE * HEAD_DIM)
    pos2 = cmp_pos.reshape(2, CMP_STRIDE * HEAD_DIM)
    kvc_pad, kvc_t = _compress(kvc16, pos2, jnp.stack([cmp_k_w1, cmp_v_w1]), jnp.stack([cmp_k_w2, cmp_v_w2]))
    o_attn = _attn(q_t, kx, vx_t, kvc_pad, kvc_t, gates_t, t_tab, w_tab, *_overlap_tables(nc, 128))

    y_rnn = _rglru(xr, gr, conv_w.reshape(CONV_WIDTH, d_rnn), conv_b[None, :], _block_diag(lru_wa).astype(BF16),
                   lru_ba[None, :], _block_diag(lru_wx).astype(BF16), lru_bx[None, :], lru_lambda[None, :],
                   gnorm_rnn_w[None, :])

    x1, h2, h2t, s_t = _outproj(o_attn, y_rnn, x, mod, gnorm_attn_w[None, :], w_out.astype(BF16), norm2_w[None, :],
                                w_router.T)

    eid_t, wgt_t, rank_t, counts = _route(s_t, b_router[:, None])
    counts = counts[:, 0].astype(I32)
    off = jnp.cumsum(counts) - counts
    dest_t = _dest(eid_t, rank_t, off.astype(F32)[:, None])
    h2f = h2.reshape(N, D)
    xs = _dispatch(dest_t, h2t, N * TOP_K)
    ys = _experts(off.astype(I32), counts, xs, w1, w3, w2)
    out = _combine(dest_t, ys, wgt_t.T, h2f, x1.reshape(N, D), mod, ws1.astype(BF16), ws3.astype(BF16),
                   ws2.astype(BF16), final_norm_w[None, :], S)
    return out.reshape(B, S, D)


def kernel(x, c, w_ada, b_ada, norm1_w, w_in, cmp_pos, cmp_k_w1, cmp_k_w2, cmp_v_w1, cmp_v_w2, rel_table, conv_w, conv_b, lru_wa, lru_ba, lru_wx, lru_bx, lru_lambda, gnorm_attn_w, gnorm_rnn_w, w_out, norm2_w, w_router, b_router, w1, w3, w2, ws1, ws3, ws2, final_norm_w):
    assert w_ada.shape[0] == 1
    c8 = jnp.pad(c, ((0, 8 - c.shape[0]), (0, 0)))
    return _layer(x, c8, w_ada[0], b_ada[0], norm1_w[0], w_in[0], cmp_pos[0], cmp_k_w1[0], cmp_k_w2[0],
                  cmp_v_w1[0], cmp_v_w2[0], rel_table, conv_w[0], conv_b[0], lru_wa[0], lru_ba[0], lru_wx[0],
                  lru_bx[0], lru_lambda[0], gnorm_attn_w[0], gnorm_rnn_w[0], w_out[0], norm2_w[0], w_router[0],
                  b_router[0], w1[0], w3[0], w2[0], ws1[0], ws3[0], ws2[0], final_norm_w)
```

```python
import functools
import math

import jax
import jax.numpy as jnp
import numpy as np
from jax import lax
from jax.experimental import pallas as pl
from jax.experimental.pallas import tpu as pltpu

F32 = jnp.float32
BF16 = jnp.bfloat16
I32 = jnp.int32

HEAD_DIM = 64
N_HEADS = 8
N_KV = 2
GQA = N_HEADS // N_KV
N_BRANCH = 3
CONV_WIDTH = 4
LRU_C = 8.0
CMP_LEN = 32
CMP_STRIDE = 16
SLC_LEN = 64
SLC_TOP = 16
WINDOW = 512
QB = 128
REL_BUCKETS = 32
REL_MAX_DIST = 1024
N_EXPERT_GROUPS = 8
TOP_GROUPS = 4
TOP_K = 8
ROUTED_SCALE = 2.5
MOE_BLOCK = 256
ROW_TILE = 8
EPS = 1e-6
NEG = -1e30
LOG2E = math.log2(math.e)
CMP_PAD = 120
CMP_WIN = 128
N_BIAS_TILES = 11
TILE_MASKED = N_BIAS_TILES
TILE_WINDOW_EDGE = N_BIAS_TILES + 1
FAR_TILE_DIST = 8
FAR_CHUNK = 4
VMEM_LIMIT = 52 * 1024 * 1024


def _dot(a, b, **kw):
    return jnp.dot(a, b, preferred_element_type=F32, **kw)


def _dot_nt(a, b, **kw):
    return lax.dot_general(a, b, (((1,), (1,)), ((), ())), preferred_element_type=F32, **kw)


def _gelu(x):
    return 0.5 * x * (1.0 + jnp.tanh(math.sqrt(2.0 / math.pi) * (x + 0.044715 * (x * x * x))))


def _sigmoid(x):
    return 1.0 / (1.0 + jnp.exp(-x))


def _rms(x):
    return x * lax.rsqrt(jnp.mean(x * x, axis=-1, keepdims=True) + EPS)


def _params(*sem):
    return pltpu.CompilerParams(dimension_semantics=sem, vmem_limit_bytes=VMEM_LIMIT)


def _ada_kernel(c_ref, w_ref, b_ref, o_ref):
    c = c_ref[...]
    a = c * _sigmoid(c)
    o_ref[...] = _dot(a, w_ref[...], precision=lax.Precision.HIGHEST) + b_ref[...]


def _ada(c8, w, b):
    d, n = w.shape
    tn = 1536
    return pl.pallas_call(
        _ada_kernel,
        grid=(n // tn,),
        in_specs=[pl.BlockSpec((8, d), lambda j: (0, 0)),
                  pl.BlockSpec((d, tn), lambda j: (0, j)),
                  pl.BlockSpec((1, tn), lambda j: (0, j))],
        out_specs=pl.BlockSpec((8, tn), lambda j: (0, j)),
        out_shape=jax.ShapeDtypeStruct((8, n), F32),
        compiler_params=_params("parallel"),
        name="ada",
    )(c8, w, b)


def _inproj_kernel(x_ref, mod_ref, nw_ref, w_ref, qt_ref, q_ref, kx_ref, vx_ref, kvc_ref, g_ref, xr_ref, gr_ref):
    h = _rms(x_ref[0]) * nw_ref[...]
    h = h * (1.0 + mod_ref[0, 1:2, :]) + mod_ref[0, 0:1, :]
    p = _dot(h.astype(BF16), w_ref[...])
    tm = p.shape[0]
    dq = N_HEADS * HEAD_DIM
    lane = lax.broadcasted_iota(I32, (tm, HEAD_DIM), 1)
    k_tail = jnp.where(lane < 2, 1.0, 0.0)
    v_tail = jnp.where(lane < 1, 1.0, 0.0)
    for hh in range(N_HEADS):
        qh = p[:, hh * HEAD_DIM:(hh + 1) * HEAD_DIM] * (HEAD_DIM ** -0.5 * LOG2E)
        q_tail = jnp.broadcast_to(qt_ref[hh:hh + 1, :], (tm, HEAD_DIM))
        q_t = jnp.concatenate([qh, q_tail], axis=1).T.astype(BF16)
        for blk in range(tm // QB):
            col = (blk * GQA + hh % GQA) * QB
            q_ref[0, hh // GQA, :, col:col + QB] = q_t[:, blk * QB:(blk + 1) * QB]
    for j in range(6):
        for g in range(N_KV):
            col = dq + (j * N_KV + g) * HEAD_DIM
            piece = p[:, col:col + HEAD_DIM]
            if j < 2:
                kvc_ref[0, j * N_KV + g] = piece
            elif j % 2 == 0:
                kx_ref[0, (j // 2 - 1) * N_KV + g] = jnp.concatenate([piece, k_tail], axis=1).astype(BF16)
            else:
                v_t = jnp.concatenate([piece, v_tail], axis=1).T.astype(BF16)
                for blk in range(tm // QB):
                    vx_ref[0, (j // 2 - 1) * N_KV + g, blk] = v_t[:, blk * QB:(blk + 1) * QB]
    c0 = dq + 6 * N_KV * HEAD_DIM
    for g in range(N_KV):
        g_ref[0, g] = _sigmoid(p[:, c0 + g * 128:c0 + (g + 1) * 128]).T[0:16, :]
    xr_ref[0] = p[:, c0 + 256:c0 + 768]
    gr_ref[0] = p[:, c0 + 768:c0 + 1280]


def _inproj(x, mod, nw, w_pad, q_tail, tm=256):
    B, S, D = x.shape
    ncol = w_pad.shape[1]
    heads = lambda n, w: pl.BlockSpec((1, n, tm, w), lambda b, t: (b, 0, t, 0))
    return pl.pallas_call(
        _inproj_kernel,
        grid=(B, S // tm),
        in_specs=[pl.BlockSpec((1, tm, D), lambda b, t: (b, t, 0)),
                  pl.BlockSpec((1, 6, D), lambda b, t: (b, 0, 0)),
                  pl.BlockSpec((1, D), lambda b, t: (0, 0)),
                  pl.BlockSpec((D, ncol), lambda b, t: (0, 0)),
                  pl.BlockSpec(q_tail.shape, lambda b, t: (0, 0))],
        out_specs=[pl.BlockSpec((1, N_KV, 2 * HEAD_DIM, tm * GQA), lambda b, t: (b, 0, 0, t)),
                   heads(4, 2 * HEAD_DIM),
                   pl.BlockSpec((1, 4, tm // QB, 2 * HEAD_DIM, QB), lambda b, t: (b, 0, t, 0, 0)),
                   heads(4, HEAD_DIM),
                   pl.BlockSpec((1, N_KV, 16, tm), lambda b, t: (b, 0, 0, t)),
                   pl.BlockSpec((1, tm, 512), lambda b, t: (b, t, 0)),
                   pl.BlockSpec((1, tm, 512), lambda b, t: (b, t, 0))],
        out_shape=[jax.ShapeDtypeStruct((B, N_KV, 2 * HEAD_DIM, S * GQA), BF16),
                   jax.ShapeDtypeStruct((B, 4, S, 2 * HEAD_DIM), BF16),
                   jax.ShapeDtypeStruct((B, 4, S // QB, 2 * HEAD_DIM, QB), BF16),
                   jax.ShapeDtypeStruct((B, 4, S, HEAD_DIM), F32),
                   jax.ShapeDtypeStruct((B, N_KV, 16, S), F32),
                   jax.ShapeDtypeStruct((B, S, 512), F32),
                   jax.ShapeDtypeStruct((B, S, 512), F32)],
        compiler_params=_params("parallel", "parallel"),
        name="inproj",
    )(x, mod, nw, w_pad, q_tail)


def _compress_kernel(x_ref, pos_ref, w1_ref, w2_ref, o_ref, ot_ref):
    x = x_ref[0, 0]
    nc = x.shape[0]
    half = CMP_STRIDE * HEAD_DIM
    kv = pl.program_id(1) // N_KV
    a = _dot((x + pos_ref[0:1, :]).astype(BF16), w1_ref[kv, 0:half, :].astype(BF16))
    b = _dot((x + pos_ref[1:2, :]).astype(BF16), w1_ref[kv, half:2 * half, :].astype(BF16))
    hid = _gelu(a + pltpu.roll(b, nc - 1, 0))
    out = _dot(hid.astype(BF16), w2_ref[kv].astype(BF16))
    row = lax.broadcasted_iota(I32, out.shape, 0)
    out = jnp.where(row < nc - 1, out, 0.0)
    lane = lax.broadcasted_iota(I32, (nc, HEAD_DIM), 1)
    out = jnp.concatenate([out, jnp.where(lane < 2, 1.0, 0.0)], axis=1)
    o_ref[0, 0, 0:CMP_PAD, :] = jnp.zeros((CMP_PAD, 2 * HEAD_DIM), F32)
    o_ref[0, 0, CMP_PAD:CMP_PAD + nc, :] = out
    o_ref[0, 0, CMP_PAD + nc:, :] = jnp.zeros((CMP_WIN - CMP_PAD, 2 * HEAD_DIM), F32)
    ot_ref[0, 0] = out.T


def _compress(kvc, pos2, w1, w2):
    B, _, NC, W = kvc.shape
    return pl.pallas_call(
        _compress_kernel,
        grid=(B, 4),
        in_specs=[pl.BlockSpec((1, 1, NC, W), lambda b, i: (b, i, 0, 0)),
                  pl.BlockSpec((2, W), lambda b, i: (0, 0)),
                  pl.BlockSpec((2, 2 * W, HEAD_DIM), lambda b, i: (0, 0, 0)),
                  pl.BlockSpec((2, HEAD_DIM, HEAD_DIM), lambda b, i: (0, 0, 0))],
        out_specs=[pl.BlockSpec((1, 1, NC + CMP_WIN, 2 * HEAD_DIM), lambda b, i: (b, i, 0, 0)),
                   pl.BlockSpec((1, 1, 2 * HEAD_DIM, NC), lambda b, i: (b, i, 0, 0))],
        out_shape=[jax.ShapeDtypeStruct((B, 4, NC + CMP_WIN, 2 * HEAD_DIM), F32),
                   jax.ShapeDtypeStruct((B, 4, 2 * HEAD_DIM, NC), F32)],
        compiler_params=_params("parallel", "parallel"),
        name="compress",
    )(kvc, pos2, w1, w2)


def _attn_kernel(q_ref, ks_ref, vs_ref, kw_ref, vw_ref, kc_ref, vc_ref, vct_ref, g_ref, t_ref, w_ref, ovt_ref, ov_ref,
                 o_ref, m_scr, acc_scr, sel_scr, sa_scr, sb_scr):
    qb = pl.program_id(2)
    R = GQA * QB
    q = q_ref[0, 0]
    nc = vct_ref.shape[3]
    nb = ov_ref.shape[1]
    n_top = min(SLC_TOP, ks_ref.shape[2] // SLC_LEN)

    def heads_sum(x):
        out = x[:, 0:QB]
        for hq in range(1, GQA):
            out = out + x[:, hq * QB:(hq + 1) * QB]
        return out

    w0 = pl.multiple_of(qb * 8, 8)
    s_far = _dot(kc_ref[0, 0, CMP_PAD:CMP_PAD + nc, :].astype(BF16), q)
    n_io = lax.broadcasted_iota(I32, (nc, R), 0)
    s_far = jnp.where(n_io < qb * 8 - CMP_PAD, s_far, -jnp.inf)
    s_win = _dot(kc_ref[0, 0, pl.ds(w0, CMP_WIN), :].astype(BF16), q) + w_ref[0]
    j_w = lax.broadcasted_iota(I32, (CMP_WIN, R), 0)
    i_w = lax.broadcasted_iota(I32, (CMP_WIN, R), 1) & (QB - 1)
    dist_w = i_w - CMP_STRIDE * (j_w - CMP_PAD) - (CMP_LEN - 1)
    s_win = jnp.where((dist_w >= 0) & (j_w >= CMP_PAD - qb * 8), s_win, -jnp.inf)
    m = jnp.maximum(jnp.max(s_far, axis=0, keepdims=True), jnp.max(s_win, axis=0, keepdims=True))
    m = jnp.where(m == -jnp.inf, 0.0, m)
    e_far = jnp.exp2(s_far - m)
    e_win = jnp.exp2(s_win - m)
    l = jnp.sum(e_far, axis=0, keepdims=True) + jnp.sum(e_win, axis=0, keepdims=True)
    inv = 1.0 / jnp.maximum(l, 1e-30)
    p_far = e_far * inv
    p_win = e_win * inv
    vcw_t = vc_ref[0, 0, pl.ds(w0, CMP_WIN), :].T
    o_c = _dot(vct_ref[0, 0].astype(BF16), p_far.astype(BF16)) + _dot(vcw_t.astype(BF16), p_win.astype(BF16))
    ov_win_t = ov_ref[pl.ds(w0, CMP_WIN), :].T
    sc_t = (_dot(ovt_ref[...], heads_sum(p_far).astype(BF16))
            + _dot(ov_win_t.astype(BF16), heads_sum(p_win).astype(BF16)))

    blk = lax.broadcasted_iota(I32, (nb, QB), 0)
    t_io = lax.broadcasted_iota(I32, (nb, QB), 1)
    cur = qb * (QB // SLC_LEN) + t_io // SLC_LEN
    forced = (blk == 0) | (blk == cur) | (blk == cur - 1)
    val = jnp.where(forced, jnp.inf, jnp.where(blk <= cur, sc_t, -jnp.inf))
    sel_t = jnp.zeros((nb, QB), F32)
    for _ in range(n_top):
        mx = jnp.max(val, axis=0, keepdims=True)
        hit = (val == mx) & (val > -jnp.inf)
        first = jnp.min(jnp.where(hit, blk, nb), axis=0, keepdims=True)
        pick = blk == first
        sel_t = jnp.where(pick, 1.0, sel_t)
        val = jnp.where(pick, -jnp.inf, val)
    sel_scr[...] = jnp.where(sel_t > 0.5, 0.0, NEG)

    m_scr[...] = jnp.full(m_scr.shape, NEG, F32)
    acc_scr[...] = jnp.zeros(acc_scr.shape, F32)
    blocks_per_chunk = FAR_CHUNK * QB // SLC_LEN

    chunk = FAR_CHUNK * QB

    def qk(c):
        return _dot(ks_ref[0, 0, pl.ds(pl.multiple_of(c * chunk, chunk), chunk), :], q)

    def soft(c, s):
        rows = sel_scr[pl.ds(pl.multiple_of(c * blocks_per_chunk, blocks_per_chunk), blocks_per_chunk), :]
        drop = jnp.concatenate([jnp.broadcast_to(rows[r:r + 1, :], (SLC_LEN, QB)) for r in range(blocks_per_chunk)],
                               axis=0)
        s = s + jnp.concatenate([drop] * GQA, axis=1)
        m_old = m_scr[...]
        m_new = jnp.maximum(m_old, jnp.max(s, axis=0, keepdims=True))
        p = jnp.exp2(s - m_new)
        v_t = jnp.concatenate([vs_ref[0, 0, c * FAR_CHUNK + j] for j in range(FAR_CHUNK)], axis=1)
        acc_scr[...] = jnp.exp2(m_old - m_new) * acc_scr[...] + _dot(v_t, p.astype(BF16))
        m_scr[...] = m_new

    n_far = jnp.maximum(qb - (FAR_TILE_DIST - 1), 0) // FAR_CHUNK
    last_far = jnp.maximum(n_far - 1, 0)
    sa_scr[...] = qk(0)

    def pair_body(j, carry):
        sb_scr[...] = qk(2 * j + 1)
        soft(2 * j, sa_scr[...])
        sa_scr[...] = qk(jnp.minimum(2 * j + 2, last_far))
        soft(2 * j + 1, sb_scr[...])
        return carry

    lax.fori_loop(0, n_far // 2, pair_body, 0)

    @pl.when(n_far % 2 == 1)
    def _():
        soft(last_far, sa_scr[...])

    def near_body(c, carry):
        tiles = []
        for j in range(FAR_CHUNK):
            delta = qb - (c * FAR_CHUNK + j)
            tiles.append(t_ref[0, jnp.where(delta < 0, TILE_MASKED, delta)])
        soft(c, qk(c) + jnp.concatenate(tiles, axis=0))
        return carry

    lax.fori_loop(n_far, qb // FAR_CHUNK + 1, near_body, 0)
    acc = acc_scr[...]
    o_s = acc[0:HEAD_DIM, :] / acc[HEAD_DIM:HEAD_DIM + 1, :]

    n_wt = WINDOW // QB + 1
    k_t, v_t, b_t = [], [], []
    for j in range(n_wt):
        kb = qb - (n_wt - 1) + j
        kb0 = jnp.maximum(kb, 0)
        k_t.append(kw_ref[0, 0, pl.ds(pl.multiple_of(kb0 * QB, QB), QB), :])
        v_t.append(vw_ref[0, 0, kb0])
        b_t.append(t_ref[0, jnp.where(kb < 0, TILE_MASKED, TILE_WINDOW_EDGE if j == 0 else n_wt - 1 - j)])
    s = _dot(jnp.concatenate(k_t, axis=0), q) + jnp.concatenate(b_t, axis=0)
    p = jnp.exp2(s - jnp.max(s, axis=0, keepdims=True))
    acc = _dot(jnp.concatenate(v_t, axis=1), p.astype(BF16))
    o_w = acc[0:HEAD_DIM, :] / acc[HEAD_DIM:HEAD_DIM + 1, :]

    gates = g_ref[0, 0]
    outs = []
    for hq in range(GQA):
        cols = slice(hq * QB, (hq + 1) * QB)
        c = hq * N_BRANCH
        outs.append(gates[c:c + 1, :] * o_c[0:HEAD_DIM, cols] + gates[c + 1:c + 2, :] * o_s[:, cols]
                    + gates[c + 2:c + 3, :] * o_w[:, cols])
    o_ref[0] = jnp.concatenate(outs, axis=0).T


def _attn(q_t, kx, vx_t, kvc, kvc_t, gates_t, t_tab, w_tab, ov_t, ov):
    B, _, W, _ = q_t.shape
    S = kx.shape[2]
    ncp = kvc.shape[2]
    nb = ov.shape[1]
    keys = lambda j: pl.BlockSpec((1, 1, S, W), lambda b, g, t: (b, j * N_KV + g, 0, 0))
    vals = lambda j: pl.BlockSpec((1, 1, S // QB, W, QB), lambda b, g, t: (b, j * N_KV + g, 0, 0, 0))
    cmp = lambda j: pl.BlockSpec((1, 1, ncp, W), lambda b, g, t: (b, j * N_KV + g, 0, 0))
    return pl.pallas_call(
        _attn_kernel,
        grid=(B, N_KV, S // QB),
        in_specs=[pl.BlockSpec((1, 1, W, GQA * QB), lambda b, g, t: (b, g, 0, t)),
                  keys(0), vals(0), keys(1), vals(1), cmp(0), cmp(1),
                  pl.BlockSpec((1, 1, W, ncp - CMP_WIN), lambda b, g, t: (b, N_KV + g, 0, 0)),
                  pl.BlockSpec((1, 1, 16, QB), lambda b, g, t: (b, g, 0, t)),
                  pl.BlockSpec((1,) + t_tab.shape[1:], lambda b, g, t: (g, 0, 0, 0)),
                  pl.BlockSpec((1, CMP_WIN, GQA * QB), lambda b, g, t: (g, 0, 0)),
                  pl.BlockSpec(ov_t.shape, lambda b, g, t: (0, 0)),
                  pl.BlockSpec(ov.shape, lambda b, g, t: (0, 0))],
        out_specs=pl.BlockSpec((1, QB, GQA * HEAD_DIM), lambda b, g, t: (b, t, g)),
        out_shape=jax.ShapeDtypeStruct((B, S, N_HEADS * HEAD_DIM), F32),
        scratch_shapes=[pltpu.VMEM((1, GQA * QB), F32), pltpu.VMEM((W, GQA * QB), F32), pltpu.VMEM((nb, QB), F32),
                        pltpu.VMEM((FAR_CHUNK * QB, GQA * QB), F32), pltpu.VMEM((FAR_CHUNK * QB, GQA * QB), F32)],
        compiler_params=_params("parallel", "parallel", "arbitrary"),
        name="attn",
    )(q_t, kx, vx_t, kx, vx_t, kvc, kvc, kvc_t, gates_t, t_tab, w_tab, ov_t, ov)


def _rglru_kernel(xr_ref, gr_ref, cw_ref, cb_ref, wa_ref, ba_ref, wx_ref, bx_ref, lam_ref, gw_ref, o_ref,
                  xbuf, hprev, a_scr, u_scr, h_scr):
    ts = xr_ref.shape[1]
    C = xr_ref.shape[2]

    @pl.when(pl.program_id(1) == 0)
    def _():
        xbuf[0:8, :] = jnp.zeros((8, C), F32)
        hprev[...] = jnp.zeros(hprev.shape, F32)

    xbuf[8:8 + ts, :] = xr_ref[0]
    xc = cb_ref[...] + jnp.zeros((ts, C), F32)
    for j in range(CONV_WIDTH):
        xc = xc + cw_ref[j:j + 1, :] * xbuf[pl.ds(8 - (CONV_WIDTH - 1) + j, ts), :]
    xbuf[0:8, :] = xbuf[ts:ts + 8, :]

    xcb = xc.astype(BF16)
    r = _sigmoid(_dot(xcb, wa_ref[...]) + ba_ref[...])
    i = _sigmoid(_dot(xcb, wx_ref[...]) + bx_ref[...])
    z = -lam_ref[...]
    softplus = jnp.maximum(z, 0.0) + jnp.log(1.0 + jnp.exp(-jnp.abs(z)))
    log_a = -LRU_C * r * softplus
    a_scr[...] = jnp.exp(log_a)
    u_scr[...] = jnp.sqrt(1.0 - jnp.exp(2.0 * log_a)) * (i * xc)

    row = lax.broadcasted_iota(I32, (8, C), 0)

    def body(k, h):
        r0 = pl.multiple_of(k * 8, 8)
        a = a_scr[pl.ds(r0, 8), :]
        b = u_scr[pl.ds(r0, 8), :]
        for s in (1, 2, 4):
            keep = row >= s
            b = jnp.where(keep, a * pltpu.roll(b, s, 0) + b, b)
            a = jnp.where(keep, a * pltpu.roll(a, s, 0), a)
        hh = a * h + b
        h_scr[pl.ds(r0, 8), :] = hh
        return jnp.broadcast_to(hh[7:8, :], (8, C))

    hprev[...] = lax.fori_loop(0, ts // 8, body, hprev[...])
    out = h_scr[...] * _gelu(gr_ref[0])
    o_ref[0] = (_rms(out) * gw_ref[...]).astype(BF16)


def _rglru(xr, gr, cw, cb, wa_bd, ba, wx_bd, bx, lam, gw, ts=512):
    B, S, C = xr.shape
    ts = min(ts, S)
    vec = pl.BlockSpec((1, C), lambda b, t: (0, 0))
    mat = pl.BlockSpec((C, C), lambda b, t: (0, 0))
    seq = pl.BlockSpec((1, ts, C), lambda b, t: (b, t, 0))
    return pl.pallas_call(
        _rglru_kernel,
        grid=(B, S // ts),
        in_specs=[seq, seq, pl.BlockSpec((CONV_WIDTH, C), lambda b, t: (0, 0)), vec, mat, vec, mat, vec, vec, vec],
        out_specs=seq,
        out_shape=jax.ShapeDtypeStruct((B, S, C), BF16),
        scratch_shapes=[pltpu.VMEM((ts + 8, C), F32), pltpu.VMEM((8, C), F32), pltpu.VMEM((ts, C), F32),
                        pltpu.VMEM((ts, C), F32), pltpu.VMEM((ts, C), F32)],
        compiler_params=_params("parallel", "arbitrary"),
        name="rglru",
    )(xr, gr, cw, cb, wa_bd, ba, wx_bd, bx, lam, gw)


def _to_row_tiles(ref, x):
    rows = x.shape[0]
    for s in range(ROW_TILE):
        ref[pl.ds(s, rows, stride=ROW_TILE), :] = x[:, s * 128:(s + 1) * 128]


def _from_row_tiles(ref, rows, start=0):
    return jnp.concatenate([ref[pl.ds(start + s, rows, stride=ROW_TILE), :] for s in range(ROW_TILE)], axis=1)


def _outproj_kernel(oa_ref, yr_ref, x_ref, mod_ref, gaw_ref, wo_ref, n2_ref, wr_ref, x1_ref, h2_ref, h2t_ref, st_ref):
    da = oa_ref.shape[2]
    ya = (_rms(oa_ref[0]) * gaw_ref[...]).astype(BF16)
    mix = _dot(ya, wo_ref[0:da, :]) + _dot(yr_ref[0], wo_ref[da:, :])
    x1 = x_ref[0] + mod_ref[0, 2:3, :] * mix
    x1_ref[0] = x1
    h2 = (_rms(x1) * n2_ref[...]) * (1.0 + mod_ref[0, 4:5, :]) + mod_ref[0, 3:4, :]
    h2_ref[0] = h2
    _to_row_tiles(h2t_ref, h2)
    st_ref[...] = _sigmoid(_dot_nt(wr_ref[...], h2, precision=lax.Precision.HIGHEST))


def _outproj(oa, yr, x, mod, gaw, wo, n2, wr_t, tm=256):
    B, S, D = x.shape
    E = wr_t.shape[0]
    nt = S // tm
    row = lambda w: pl.BlockSpec((1, tm, w), lambda b, t: (b, t, 0))
    return pl.pallas_call(
        _outproj_kernel,
        grid=(B, nt),
        in_specs=[row(oa.shape[2]), row(yr.shape[2]), row(D),
                  pl.BlockSpec((1, 6, D), lambda b, t: (b, 0, 0)),
                  pl.BlockSpec((1, oa.shape[2]), lambda b, t: (0, 0)),
                  pl.BlockSpec(wo.shape, lambda b, t: (0, 0)),
                  pl.BlockSpec((1, D), lambda b, t: (0, 0)),
                  pl.BlockSpec((E, D), lambda b, t: (0, 0))],
        out_specs=[row(D), row(D), pl.BlockSpec((tm * ROW_TILE, D // ROW_TILE), lambda b, t: (b * nt + t, 0)),
                   pl.BlockSpec((E, tm), lambda b, t: (0, b * nt + t))],
        out_shape=[jax.ShapeDtypeStruct((B, S, D), F32), jax.ShapeDtypeStruct((B, S, D), F32),
                   jax.ShapeDtypeStruct((B * S * ROW_TILE, D // ROW_TILE), F32),
                   jax.ShapeDtypeStruct((E, B * S), F32)],
        compiler_params=_params("parallel", "parallel"),
        name="outproj",
    )(oa, yr, x, mod, gaw, wo, n2, wr_t)


def _route_kernel(s_ref, b_ref, eid_ref, wgt_ref, rank_ref, cnt_ref, carry):
    E, tn = s_ref.shape
    per = E // N_EXPERT_GROUPS

    @pl.when(pl.program_id(0) == 0)
    def _():
        carry[...] = jnp.zeros(carry.shape, F32)

    s = s_ref[...]
    s_sel = s + b_ref[...]
    eidx = lax.broadcasted_iota(I32, (E, tn), 0)

    grp = []
    for gi in range(N_EXPERT_GROUPS):
        xg = s_sel[gi * per:(gi + 1) * per, :]
        ig = lax.broadcasted_iota(I32, (per, tn), 0)
        m1 = jnp.max(xg, axis=0, keepdims=True)
        f1 = jnp.min(jnp.where(xg == m1, ig, per), axis=0, keepdims=True)
        m2 = jnp.max(jnp.where(ig == f1, -jnp.inf, xg), axis=0, keepdims=True)
        grp.append(m1 + m2)
    val = []
    for gi in range(N_EXPERT_GROUPS):
        rank = jnp.zeros((1, tn), I32)
        for gj in range(N_EXPERT_GROUPS):
            if gj == gi:
                continue
            ahead = (grp[gj] > grp[gi]) | ((grp[gj] == grp[gi]) & (gj < gi))
            rank = rank + ahead.astype(I32)
        val.append(jnp.where(rank < TOP_GROUPS, s_sel[gi * per:(gi + 1) * per, :], -jnp.inf))
    val = jnp.concatenate(val, axis=0)

    eids, tops = [], []
    hot = jnp.zeros((E, tn), F32)
    for _ in range(TOP_K):
        mx = jnp.max(val, axis=0, keepdims=True)
        first = jnp.min(jnp.where(val == mx, eidx, E), axis=0, keepdims=True)
        pick = eidx == first
        eids.append(first)
        tops.append(jnp.sum(jnp.where(pick, s, 0.0), axis=0, keepdims=True))
        hot = jnp.where(pick, 1.0, hot)
        val = jnp.where(pick, -jnp.inf, val)
    denom = tops[0]
    for t in tops[1:]:
        denom = denom + t

    ti = lax.broadcasted_iota(I32, (tn, tn), 0)
    tj = lax.broadcasted_iota(I32, (tn, tn), 1)
    upper = jnp.where(ti < tj, 1.0, 0.0).astype(BF16)
    before = _dot(hot.astype(BF16), upper) + carry[...]
    for k in range(TOP_K):
        eid_ref[k:k + 1, :] = eids[k]
        wgt_ref[k:k + 1, :] = ROUTED_SCALE * tops[k] / denom
        rank_ref[k:k + 1, :] = jnp.sum(jnp.where(eidx == eids[k], before, 0.0), axis=0, keepdims=True).astype(I32)
    carry[...] = carry[...] + jnp.sum(hot, axis=1, keepdims=True)
    cnt_ref[...] = carry[...]


def _route(s_t, b_col, tn=256):
    E, N = s_t.shape
    tn = min(tn, N)
    out = pl.BlockSpec((TOP_K, tn), lambda t: (0, t))
    return pl.pallas_call(
        _route_kernel,
        grid=(N // tn,),
        in_specs=[pl.BlockSpec((E, tn), lambda t: (0, t)), pl.BlockSpec((E, 1), lambda t: (0, 0))],
        out_specs=[out, out, out, pl.BlockSpec((E, 1), lambda t: (0, 0))],
        out_shape=[jax.ShapeDtypeStruct((TOP_K, N), I32), jax.ShapeDtypeStruct((TOP_K, N), F32),
                   jax.ShapeDtypeStruct((TOP_K, N), I32), jax.ShapeDtypeStruct((E, 1), F32)],
        scratch_shapes=[pltpu.VMEM((E, 1), F32)],
        compiler_params=_params("arbitrary"),
        name="route",
    )(s_t, b_col)


def _dest_kernel(eid_ref, rank_ref, off_ref, o_ref):
    E = off_ref.shape[0]
    tn = eid_ref.shape[1]
    eidx = lax.broadcasted_iota(I32, (E, tn), 0)
    for k in range(TOP_K):
        base = jnp.sum(jnp.where(eidx == eid_ref[k:k + 1, :], off_ref[...], 0.0), axis=0, keepdims=True)
        o_ref[k:k + 1, :] = base.astype(I32) + rank_ref[k:k + 1, :]


def _dest(eid_t, rank_t, off_col, tn=512):
    K, N = eid_t.shape
    tn = min(tn, N)
    blk = pl.BlockSpec((K, tn), lambda t: (0, t))
    return pl.pallas_call(
        _dest_kernel,
        grid=(N // tn,),
        in_specs=[blk, blk, pl.BlockSpec(off_col.shape, lambda t: (0, 0))],
        out_specs=blk,
        out_shape=jax.ShapeDtypeStruct((K, N), I32),
        compiler_params=_params("parallel"),
        name="dest",
    )(eid_t, rank_t, off_col)


def _row_tile(ref, r):
    return ref.at[pl.ds(pl.multiple_of(r * ROW_TILE, ROW_TILE), ROW_TILE)]


def _dispatch_kernel(dest_ref, h_ref, xs_hbm, sem):
    tn = h_ref.shape[0] // ROW_TILE

    def go(t, c):
        for k in range(TOP_K):
            pltpu.make_async_copy(_row_tile(h_ref, t), _row_tile(xs_hbm, dest_ref[k, t]), sem.at[0]).start()
        return c
    lax.fori_loop(0, tn, go, 0, unroll=4)
    for k in range(TOP_K):
        pltpu.make_async_copy(h_ref, xs_hbm.at[pl.ds(0, tn * ROW_TILE)], sem.at[0]).wait()


def _dispatch(dest_t, h2t, n_slots, tn=256):
    rows, W = h2t.shape
    tn = min(tn, rows // ROW_TILE)
    return pl.pallas_call(
        _dispatch_kernel,
        grid=(rows // ROW_TILE // tn,),
        in_specs=[pl.BlockSpec((TOP_K, tn), lambda t: (0, t), memory_space=pltpu.SMEM),
                  pl.BlockSpec((tn * ROW_TILE, W), lambda t: (t, 0))],
        out_specs=pl.BlockSpec(memory_space=pl.ANY),
        out_shape=jax.ShapeDtypeStruct((n_slots * ROW_TILE, W), F32),
        scratch_shapes=[pltpu.SemaphoreType.DMA((1,))],
        compiler_params=_params("arbitrary"),
        name="dispatch",
    )(dest_t, h2t)


def _experts_kernel(first_ref, size_ref, e_ref, new_ref, next_ref, wslot_ref, total_ref,
                    w1_hbm, w3_hbm, w2_hbm, xs_hbm, ys_hbm,
                    xbuf, ybuf, w1f, w3f, w2f, w1b, w3b, w2b, xsem, ysem, wsem):
    total = total_ref[0]

    def w_copies(e, slot):
        return [pltpu.make_async_copy(src.at[e], dst.at[slot], wsem.at[slot])
                for src, dst in ((w1_hbm, w1f), (w3_hbm, w3f), (w2_hbm, w2f))]

    def rows(c):
        return (pl.multiple_of(first_ref[c] * ROW_TILE, ROW_TILE), pl.multiple_of(size_ref[c] * ROW_TILE, ROW_TILE))

    def x_copy(c, slot):
        first, size = rows(c)
        return pltpu.make_async_copy(xs_hbm.at[pl.ds(first, size)], xbuf.at[slot, pl.ds(0, size)], xsem.at[slot])

    def y_copy(c, slot):
        first, size = rows(c)
        return pltpu.make_async_copy(ybuf.at[slot, pl.ds(0, size)], ys_hbm.at[pl.ds(first, size)], ysem.at[slot])

    xbuf[...] = jnp.zeros(xbuf.shape, F32)
    for cp in w_copies(e_ref[0], 0):
        cp.start()
    x_copy(0, 0).start()

    def body(c, carry):
        slot = c % 2

        @pl.when(new_ref[c] == 1)
        def _():
            ws = wslot_ref[c]
            for cp in w_copies(e_ref[c], ws):
                cp.wait()
            w1b[...] = w1f[ws].astype(BF16)
            w3b[...] = w3f[ws].astype(BF16)
            w2b[...] = w2f[ws].astype(BF16)

            @pl.when(next_ref[c] >= 0)
            def _():
                for cp in w_copies(next_ref[c], 1 - ws):
                    cp.start()

        x_copy(c, slot).wait()

        @pl.when(c + 1 < total)
        def _():
            x_copy(c + 1, 1 - slot).start()

        x = _from_row_tiles(xbuf.at[slot], MOE_BLOCK).astype(BF16)
        h1 = _dot(x, w1b[...])
        h3 = _dot(x, w3b[...])
        hid = (h1 * _sigmoid(h1)) * h3
        y = _dot(hid.astype(BF16), w2b[...])

        @pl.when(c >= 2)
        def _():
            y_copy(c - 2, slot).wait()

        _to_row_tiles(ybuf.at[slot], y)
        y_copy(c, slot).start()
        return carry

    lax.fori_loop(0, total, body, 0)

    @pl.when(total >= 2)
    def _():
        y_copy(total - 2, total % 2).wait()

    y_copy(total - 1, (total - 1) % 2).wait()


def _expert_chunks(counts, n_rows):
    E = counts.shape[0]
    n_max = n_rows // MOE_BLOCK + E
    starts = jnp.cumsum(counts) - counts
    n_chunks = (counts + MOE_BLOCK - 1) // MOE_BLOCK
    c_end = jnp.cumsum(n_chunks)
    c = jnp.arange(n_max, dtype=I32)
    e = jnp.minimum(jnp.sum(c_end[None, :] <= c[:, None], axis=1), E - 1).astype(I32)
    j = c - (c_end - n_chunks)[e]
    size = jnp.clip(counts[e] - j * MOE_BLOCK, 0, MOE_BLOCK)
    has_rows = counts > 0
    ids = jnp.where(has_rows, jnp.arange(E, dtype=I32), E)
    later = jnp.flip(lax.cummin(jnp.flip(ids)))
    nxt = jnp.concatenate([later[1:], jnp.full((1,), E, I32)])
    nxt = jnp.where(nxt >= E, -1, nxt)
    ordinal = jnp.cumsum(has_rows.astype(I32)) - 1
    to_i32 = lambda a: a.astype(I32)
    return tuple(map(to_i32, (starts[e] + j * MOE_BLOCK, size, e, j == 0, nxt[e], ordinal[e] % 2, c_end[-1:])))


def _experts(chunks, xs, w1, w3, w2):
    rows_total, W = xs.shape
    _, D, De = w1.shape
    any_space = pl.BlockSpec(memory_space=pl.ANY)
    grid_spec = pltpu.PrefetchScalarGridSpec(
        num_scalar_prefetch=len(chunks),
        grid=(1,),
        in_specs=[any_space] * 4,
        out_specs=any_space,
        scratch_shapes=[pltpu.VMEM((2, MOE_BLOCK * ROW_TILE, W), F32), pltpu.VMEM((2, MOE_BLOCK * ROW_TILE, W), F32),
                        pltpu.VMEM((2, D, De), F32), pltpu.VMEM((2, D, De), F32), pltpu.VMEM((2, De, D), F32),
                        pltpu.VMEM((D, De), BF16), pltpu.VMEM((D, De), BF16), pltpu.VMEM((De, D), BF16),
                        pltpu.SemaphoreType.DMA((2,)), pltpu.SemaphoreType.DMA((2,)), pltpu.SemaphoreType.DMA((2,))],
    )
    return pl.pallas_call(
        _experts_kernel,
        grid_spec=grid_spec,
        out_shape=jax.ShapeDtypeStruct((rows_total, W), F32),
        compiler_params=_params("arbitrary"),
        name="experts",
    )(*chunks, w1, w3, w2, xs)


def _combine_kernel(dest_ref, ys_hbm, w_ref, h2_ref, x1_ref, mod_ref, ws1_ref, ws3_ref, ws2_ref, fw_ref, o_ref,
                    ybuf, sem):
    tm = h2_ref.shape[0]

    def go(t, c):
        for k in range(TOP_K):
            pltpu.make_async_copy(_row_tile(ys_hbm, dest_ref[k, t]), _row_tile(ybuf.at[k], t), sem.at[0]).start()
        return c
    lax.fori_loop(0, tm, go, 0, unroll=4)

    h = h2_ref[...].astype(BF16)
    a = _dot(h, ws1_ref[...])
    hid = (a * _sigmoid(a)) * _dot(h, ws3_ref[...])
    shared = _dot(hid.astype(BF16), ws2_ref[...])

    for k in range(TOP_K):
        pltpu.make_async_copy(ys_hbm.at[pl.ds(0, tm * ROW_TILE)], ybuf.at[k], sem.at[0]).wait()
    routed = w_ref[:, 0:1] * _from_row_tiles(ybuf.at[0], tm)
    for k in range(1, TOP_K):
        routed = routed + w_ref[:, k:k + 1] * _from_row_tiles(ybuf.at[k], tm)
    x2 = x1_ref[...] + mod_ref[0, 5:6, :] * (routed + shared)
    o_ref[...] = _rms(x2) * fw_ref[...]


def _combine(dest_t, ys, wgt, h2, x1, mod, ws1, ws3, ws2, fw, seq, tm=256):
    N, D = h2.shape
    tm = min(tm, seq)
    nt = seq // tm
    row = pl.BlockSpec((tm, D), lambda t: (t, 0))
    full = lambda a: pl.BlockSpec(a.shape, lambda t: (0,) * a.ndim)
    return pl.pallas_call(
        _combine_kernel,
        grid=(N // tm,),
        in_specs=[pl.BlockSpec((TOP_K, tm), lambda t: (0, t), memory_space=pltpu.SMEM),
                  pl.BlockSpec(memory_space=pl.ANY),
                  pl.BlockSpec((tm, TOP_K), lambda t: (t, 0)),
                  row, row,
                  pl.BlockSpec((1, 6, D), lambda t: (t // nt, 0, 0)),
                  full(ws1), full(ws3), full(ws2), full(fw)],
        out_specs=row,
        out_shape=jax.ShapeDtypeStruct((N, D), F32),
        scratch_shapes=[pltpu.VMEM((TOP_K, tm * ROW_TILE, D // ROW_TILE), F32), pltpu.SemaphoreType.DMA((1,))],
        compiler_params=_params("arbitrary"),
        name="combine",
    )(dest_t, ys, wgt, h2, x1, mod, ws1, ws3, ws2, fw)


def _t5_bucket(dist):
    n = jnp.maximum(dist, 0)
    max_exact = REL_BUCKETS // 2
    nf = jnp.maximum(n, 1).astype(F32)
    large = max_exact + (jnp.log(nf / max_exact) / math.log(REL_MAX_DIST / max_exact)
                         * (REL_BUCKETS - max_exact)).astype(I32)
    large = jnp.minimum(large, REL_BUCKETS - 1)
    return jnp.where(n < max_exact, n, large)


def _bias_kernel(tbl_ref, far_ref, bt_ref, bw_ref, t_ref, w_ref):
    g = pl.program_id(0)
    d = pl.program_id(1)

    def lookup(bkt, head):
        out = jnp.full(bkt.shape, tbl_ref[0, head], F32)
        for b in range(1, REL_BUCKETS):
            out = jnp.where(bkt == b, tbl_ref[b, head], out)
        return out - far_ref[head]

    j = lax.broadcasted_iota(I32, (QB, QB), 0)
    i = lax.broadcasted_iota(I32, (QB, QB), 1)
    keep = ((d != 0) | (i >= j)) & (d != TILE_MASKED) & ((d != TILE_WINDOW_EDGE) | (j > i))
    for hq in range(GQA):
        t_ref[0, 0, :, hq * QB:(hq + 1) * QB] = jnp.where(keep, lookup(bt_ref[0], GQA * g + hq), NEG)

    @pl.when(d == 0)
    def _():
        for hq in range(GQA):
            w_ref[0, :, hq * QB:(hq + 1) * QB] = lookup(bw_ref[...], GQA * g + hq)


def _bias_tables(rel_table):
    tbl = rel_table.astype(F32) * LOG2E
    far_hi = tbl[REL_BUCKETS - 1].astype(BF16)
    far_lo = (tbl[REL_BUCKETS - 1] - far_hi.astype(F32)).astype(BF16)
    far = far_hi.astype(F32) + far_lo.astype(F32)
    q_tail = jnp.zeros((N_HEADS, HEAD_DIM), F32).at[:, 0].set(far_hi.astype(F32)).at[:, 1].set(far_lo.astype(F32))
    i = jnp.arange(QB)
    tile_dist = np.array(list(range(N_BIAS_TILES)) + [0, WINDOW // QB])
    d_t = (QB * jnp.asarray(tile_dist)[:, None, None] + i[None, None, :] - i[None, :, None])
    d_w = i[None, :] - CMP_STRIDE * (jnp.arange(CMP_WIN)[:, None] - CMP_PAD) - (CMP_LEN - 1)
    n_tiles = len(tile_dist)
    t_tab, w_tab = pl.pallas_call(
        _bias_kernel,
        grid=(N_KV, n_tiles),
        in_specs=[pl.BlockSpec(memory_space=pltpu.SMEM),
                  pl.BlockSpec(memory_space=pltpu.SMEM),
                  pl.BlockSpec((1, QB, QB), lambda g, d: (d, 0, 0)),
                  pl.BlockSpec((CMP_WIN, QB), lambda g, d: (0, 0))],
        out_specs=[pl.BlockSpec((1, 1, QB, GQA * QB), lambda g, d: (g, d, 0, 0)),
                   pl.BlockSpec((1, CMP_WIN, GQA * QB), lambda g, d: (g, 0, 0))],
        out_shape=[jax.ShapeDtypeStruct((N_KV, n_tiles, QB, GQA * QB), F32),
                   jax.ShapeDtypeStruct((N_KV, CMP_WIN, GQA * QB), F32)],
        compiler_params=_params("parallel", "arbitrary"),
        name="bias",
    )(tbl, far, _t5_bucket(d_t), _t5_bucket(d_w))
    far_min = min(FAR_TILE_DIST * QB - QB + 1, CMP_STRIDE * (CMP_PAD + 1) - (CMP_LEN - 1))
    max_exact = REL_BUCKETS // 2
    assert math.log(far_min / max_exact) / math.log(REL_MAX_DIST / max_exact) * max_exact > max_exact - 0.75
    return t_tab, w_tab, q_tail


def _overlap_tables(nc, nb):
    c_start = CMP_STRIDE * np.arange(nc)[:, None]
    s_start = SLC_LEN * np.arange(nb)[None, :]
    ov = ((c_start < s_start + SLC_LEN) & (c_start + CMP_LEN > s_start)).astype(np.float32)
    ov[nc - 1:] = 0.0
    out = np.zeros((nc + CMP_WIN, nb), np.float32)
    out[CMP_PAD:CMP_PAD + nc] = ov
    return jnp.asarray(ov.T, dtype=BF16), jnp.asarray(out)


def _block_diag(w):
    nblk, bs, _ = w.shape
    eye = jnp.eye(nblk, dtype=w.dtype)
    return (eye[:, None, :, None] * w[:, :, None, :]).reshape(nblk * bs, nblk * bs)


def _layer(x, c8, w_ada, b_ada, norm1_w, w_in, cmp_pos, cmp_k_w1, cmp_k_w2, cmp_v_w1, cmp_v_w2, rel_table,
           conv_w, conv_b, lru_wa, lru_ba, lru_wx, lru_bx, lru_lambda, gnorm_attn_w, gnorm_rnn_w, w_out,
           norm2_w, w_router, b_router, w1, w3, w2, ws1, ws3, ws2, final_norm_w):
    B, S, D = x.shape
    N = B * S
    E = w_router.shape[1]
    d_attn = N_HEADS * HEAD_DIM
    d_rnn = D - d_attn
    assert S % 512 == 0 and S // SLC_LEN <= 128

    mod = _ada(c8, w_ada, b_ada[None, :])[:B].reshape(B, 6, D)

    n_kv_cols = 6 * N_KV * HEAD_DIM
    n_gate = N_BRANCH * N_HEADS
    wq, wkv, wg, wxr, wgr = jnp.split(w_in, np.cumsum([d_attn, n_kv_cols, n_gate, d_rnn]).tolist(), axis=1)
    wg = jnp.pad(wg.reshape(D, N_KV, GQA * N_BRANCH), ((0, 0), (0, 0), (0, 128 - GQA * N_BRANCH))).reshape(D, 256)
    w_pad = jnp.concatenate([wq, wkv, wg, wxr, wgr], axis=1).astype(BF16)
    t_tab, w_tab, q_tail = _bias_tables(rel_table)
    q_t, kx, vx_t, kvc, gates_t, xr, gr = _inproj(x, mod, norm1_w[None, :], w_pad, q_tail)

    nc = S // CMP_STRIDE
    kvc16 = kvc.reshape(B, 4, nc, CMP_STRIDE * HEAD_DIM)
    pos2 = cmp_pos.reshape(2, CMP_STRIDE * HEAD_DIM)
    kvc_pad, kvc_t = _compress(kvc16, pos2, jnp.stack([cmp_k_w1, cmp_v_w1]), jnp.stack([cmp_k_w2, cmp_v_w2]))
    o_attn = _attn(q_t, kx, vx_t, kvc_pad, kvc_t, gates_t, t_tab, w_tab, *_overlap_tables(nc, 128))

    y_rnn = _rglru(xr, gr, conv_w.reshape(CONV_WIDTH, d_rnn), conv_b[None, :], _block_diag(lru_wa).astype(BF16),
                   lru_ba[None, :], _block_diag(lru_wx).astype(BF16), lru_bx[None, :], lru_lambda[None, :],
                   gnorm_rnn_w[None, :])

    x1, h2, h2t, s_t = _outproj(o_attn, y_rnn, x, mod, gnorm_attn_w[None, :], w_out.astype(BF16), norm2_w[None, :],
                                w_router.T)

    eid_t, wgt_t, rank_t, counts = _route(s_t, b_router[:, None])
    counts = counts[:, 0].astype(I32)
    off = jnp.cumsum(counts) - counts
    dest_t = _dest(eid_t, rank_t, off.astype(F32)[:, None])
    h2f = h2.reshape(N, D)
    xs = _dispatch(dest_t, h2t, N * TOP_K)
    ys = _experts(_expert_chunks(counts, N * TOP_K), xs, w1, w3, w2)
    out = _combine(dest_t, ys, wgt_t.T, h2f, x1.reshape(N, D), mod, ws1.astype(BF16), ws3.astype(BF16),
                   ws2.astype(BF16), final_norm_w[None, :], S)
    return out.reshape(B, S, D)


def kernel(x, c, w_ada, b_ada, norm1_w, w_in, cmp_pos, cmp_k_w1, cmp_k_w2, cmp_v_w1, cmp_v_w2, rel_table, conv_w, conv_b, lru_wa, lru_ba, lru_wx, lru_bx, lru_lambda, gnorm_attn_w, gnorm_rnn_w, w_out, norm2_w, w_router, b_router, w1, w3, w2, ws1, ws3, ws2, final_norm_w):
    assert w_ada.shape[0] == 1
    c8 = jnp.pad(c, ((0, 8 - c.shape[0]), (0, 0)))
    return _layer(x, c8, w_ada[0], b_ada[0], norm1_w[0], w_in[0], cmp_pos[0], cmp_k_w1[0], cmp_k_w2[0],
                  cmp_v_w1[0], cmp_v_w2[0], rel_table, conv_w[0], conv_b[0], lru_wa[0], lru_ba[0], lru_wx[0],
                  lru_bx[0], lru_lambda[0], gnorm_attn_w[0], gnorm_rnn_w[0], w_out[0], norm2_w[0], w_router[0],
                  b_router[0], w1[0], w3[0], w2[0], ws1[0], ws3[0], ws2[0], final_norm_w)
```

```python
import functools
import math

import jax
import jax.numpy as jnp
import numpy as np
from jax import lax
from jax.experimental import pallas as pl
from jax.experimental.pallas import tpu as pltpu

F32 = jnp.float32
BF16 = jnp.bfloat16
I32 = jnp.int32

HEAD_DIM = 64
N_HEADS = 8
N_KV = 2
GQA = N_HEADS // N_KV
N_BRANCH = 3
CONV_WIDTH = 4
LRU_C = 8.0
CMP_LEN = 32
CMP_STRIDE = 16
SLC_LEN = 64
SLC_TOP = 16
WINDOW = 512
QB = 128
REL_BUCKETS = 32
REL_MAX_DIST = 1024
N_EXPERT_GROUPS = 8
TOP_GROUPS = 4
TOP_K = 8
ROUTED_SCALE = 2.5
MOE_BLOCK = 256
MOE_TILE = 256
ROW_TILE = 8
EPS = 1e-6
NEG = -1e30
LOG2E = math.log2(math.e)
CMP_PAD = 120
CMP_WIN = 128
N_BIAS_TILES = 11
TILE_MASKED = N_BIAS_TILES
TILE_WINDOW_EDGE = N_BIAS_TILES + 1
FAR_TILE_DIST = 8
FAR_CHUNK = 4
VMEM_LIMIT = 52 * 1024 * 1024


def _dot(a, b, **kw):
    return jnp.dot(a, b, preferred_element_type=F32, **kw)


def _dot_nt(a, b, **kw):
    return lax.dot_general(a, b, (((1,), (1,)), ((), ())), preferred_element_type=F32, **kw)


def _gelu(x):
    return 0.5 * x * (1.0 + jnp.tanh(math.sqrt(2.0 / math.pi) * (x + 0.044715 * (x * x * x))))


def _sigmoid(x):
    return 1.0 / (1.0 + jnp.exp(-x))


def _rms(x):
    return x * lax.rsqrt(jnp.mean(x * x, axis=-1, keepdims=True) + EPS)


def _params(*sem):
    return pltpu.CompilerParams(dimension_semantics=sem, vmem_limit_bytes=VMEM_LIMIT)


def _ada_kernel(c_ref, w_ref, b_ref, o_ref):
    c = c_ref[...]
    a = c * _sigmoid(c)
    o_ref[...] = _dot(a, w_ref[...], precision=lax.Precision.HIGHEST) + b_ref[...]


def _ada(c8, w, b):
    d, n = w.shape
    tn = 1536
    return pl.pallas_call(
        _ada_kernel,
        grid=(n // tn,),
        in_specs=[pl.BlockSpec((8, d), lambda j: (0, 0)),
                  pl.BlockSpec((d, tn), lambda j: (0, j)),
                  pl.BlockSpec((1, tn), lambda j: (0, j))],
        out_specs=pl.BlockSpec((8, tn), lambda j: (0, j)),
        out_shape=jax.ShapeDtypeStruct((8, n), F32),
        compiler_params=_params("parallel"),
        name="ada",
    )(c8, w, b)


def _inproj_kernel(x_ref, mod_ref, nw_ref, w_ref, qt_ref, q_ref, kx_ref, vx_ref, kvc_ref, g_ref, xr_ref, gr_ref):
    h = _rms(x_ref[0]) * nw_ref[...]
    h = h * (1.0 + mod_ref[0, 1:2, :]) + mod_ref[0, 0:1, :]
    p = _dot(h.astype(BF16), w_ref[...])
    tm = p.shape[0]
    dq = N_HEADS * HEAD_DIM
    lane = lax.broadcasted_iota(I32, (tm, HEAD_DIM), 1)
    k_tail = jnp.where(lane < 2, 1.0, 0.0)
    v_tail = jnp.where(lane < 1, 1.0, 0.0)
    for hh in range(N_HEADS):
        qh = p[:, hh * HEAD_DIM:(hh + 1) * HEAD_DIM] * (HEAD_DIM ** -0.5 * LOG2E)
        q_tail = jnp.broadcast_to(qt_ref[hh:hh + 1, :], (tm, HEAD_DIM))
        q_t = jnp.concatenate([qh, q_tail], axis=1).T.astype(BF16)
        for blk in range(tm // QB):
            col = (blk * GQA + hh % GQA) * QB
            q_ref[0, hh // GQA, :, col:col + QB] = q_t[:, blk * QB:(blk + 1) * QB]
    for j in range(6):
        for g in range(N_KV):
            col = dq + (j * N_KV + g) * HEAD_DIM
            piece = p[:, col:col + HEAD_DIM]
            if j < 2:
                kvc_ref[0, j * N_KV + g] = piece
            elif j % 2 == 0:
                kx_ref[0, (j // 2 - 1) * N_KV + g] = jnp.concatenate([piece, k_tail], axis=1).astype(BF16)
            else:
                v_t = jnp.concatenate([piece, v_tail], axis=1).T.astype(BF16)
                for blk in range(tm // QB):
                    vx_ref[0, (j // 2 - 1) * N_KV + g, blk] = v_t[:, blk * QB:(blk + 1) * QB]
    c0 = dq + 6 * N_KV * HEAD_DIM
    for g in range(N_KV):
        g_ref[0, g] = _sigmoid(p[:, c0 + g * 128:c0 + (g + 1) * 128]).T[0:16, :]
    xr_ref[0] = p[:, c0 + 256:c0 + 768]
    gr_ref[0] = p[:, c0 + 768:c0 + 1280]


def _inproj(x, mod, nw, w_pad, q_tail, tm=256):
    B, S, D = x.shape
    ncol = w_pad.shape[1]
    heads = lambda n, w: pl.BlockSpec((1, n, tm, w), lambda b, t: (b, 0, t, 0))
    return pl.pallas_call(
        _inproj_kernel,
        grid=(B, S // tm),
        in_specs=[pl.BlockSpec((1, tm, D), lambda b, t: (b, t, 0)),
                  pl.BlockSpec((1, 6, D), lambda b, t: (b, 0, 0)),
                  pl.BlockSpec((1, D), lambda b, t: (0, 0)),
                  pl.BlockSpec((D, ncol), lambda b, t: (0, 0)),
                  pl.BlockSpec(q_tail.shape, lambda b, t: (0, 0))],
        out_specs=[pl.BlockSpec((1, N_KV, 2 * HEAD_DIM, tm * GQA), lambda b, t: (b, 0, 0, t)),
                   heads(4, 2 * HEAD_DIM),
                   pl.BlockSpec((1, 4, tm // QB, 2 * HEAD_DIM, QB), lambda b, t: (b, 0, t, 0, 0)),
                   heads(4, HEAD_DIM),
                   pl.BlockSpec((1, N_KV, 16, tm), lambda b, t: (b, 0, 0, t)),
                   pl.BlockSpec((1, tm, 512), lambda b, t: (b, t, 0)),
                   pl.BlockSpec((1, tm, 512), lambda b, t: (b, t, 0))],
        out_shape=[jax.ShapeDtypeStruct((B, N_KV, 2 * HEAD_DIM, S * GQA), BF16),
                   jax.ShapeDtypeStruct((B, 4, S, 2 * HEAD_DIM), BF16),
                   jax.ShapeDtypeStruct((B, 4, S // QB, 2 * HEAD_DIM, QB), BF16),
                   jax.ShapeDtypeStruct((B, 4, S, HEAD_DIM), F32),
                   jax.ShapeDtypeStruct((B, N_KV, 16, S), F32),
                   jax.ShapeDtypeStruct((B, S, 512), F32),
                   jax.ShapeDtypeStruct((B, S, 512), F32)],
        compiler_params=_params("parallel", "parallel"),
        name="inproj",
    )(x, mod, nw, w_pad, q_tail)


def _compress_kernel(x_ref, pos_ref, w1_ref, w2_ref, o_ref, ot_ref):
    x = x_ref[0, 0]
    nc = x.shape[0]
    half = CMP_STRIDE * HEAD_DIM
    kv = pl.program_id(1) // N_KV
    a = _dot((x + pos_ref[0:1, :]).astype(BF16), w1_ref[kv, 0:half, :].astype(BF16))
    b = _dot((x + pos_ref[1:2, :]).astype(BF16), w1_ref[kv, half:2 * half, :].astype(BF16))
    hid = _gelu(a + pltpu.roll(b, nc - 1, 0))
    out = _dot(hid.astype(BF16), w2_ref[kv].astype(BF16))
    row = lax.broadcasted_iota(I32, out.shape, 0)
    out = jnp.where(row < nc - 1, out, 0.0)
    lane = lax.broadcasted_iota(I32, (nc, HEAD_DIM), 1)
    out = jnp.concatenate([out, jnp.where(lane < 2, 1.0, 0.0)], axis=1)
    o_ref[0, 0, 0:CMP_PAD, :] = jnp.zeros((CMP_PAD, 2 * HEAD_DIM), F32)
    o_ref[0, 0, CMP_PAD:CMP_PAD + nc, :] = out
    o_ref[0, 0, CMP_PAD + nc:, :] = jnp.zeros((CMP_WIN - CMP_PAD, 2 * HEAD_DIM), F32)
    ot_ref[0, 0] = out.T


def _compress(kvc, pos2, w1, w2):
    B, _, NC, W = kvc.shape
    return pl.pallas_call(
        _compress_kernel,
        grid=(B, 4),
        in_specs=[pl.BlockSpec((1, 1, NC, W), lambda b, i: (b, i, 0, 0)),
                  pl.BlockSpec((2, W), lambda b, i: (0, 0)),
                  pl.BlockSpec((2, 2 * W, HEAD_DIM), lambda b, i: (0, 0, 0)),
                  pl.BlockSpec((2, HEAD_DIM, HEAD_DIM), lambda b, i: (0, 0, 0))],
        out_specs=[pl.BlockSpec((1, 1, NC + CMP_WIN, 2 * HEAD_DIM), lambda b, i: (b, i, 0, 0)),
                   pl.BlockSpec((1, 1, 2 * HEAD_DIM, NC), lambda b, i: (b, i, 0, 0))],
        out_shape=[jax.ShapeDtypeStruct((B, 4, NC + CMP_WIN, 2 * HEAD_DIM), F32),
                   jax.ShapeDtypeStruct((B, 4, 2 * HEAD_DIM, NC), F32)],
        compiler_params=_params("parallel", "parallel"),
        name="compress",
    )(kvc, pos2, w1, w2)


def _attn_kernel(q_ref, ks_ref, vs_ref, kw_ref, vw_ref, kc_ref, vc_ref, vct_ref, g_ref, t_ref, w_ref, ovt_ref, ov_ref,
                 o_ref, m_scr, acc_scr, sel_scr, sa_scr, sb_scr):
    qb = pl.program_id(2)
    R = GQA * QB
    q = q_ref[0, 0]
    nc = vct_ref.shape[3]
    nb = ov_ref.shape[1]
    n_top = min(SLC_TOP, ks_ref.shape[2] // SLC_LEN)

    def heads_sum(x):
        out = x[:, 0:QB]
        for hq in range(1, GQA):
            out = out + x[:, hq * QB:(hq + 1) * QB]
        return out

    w0 = pl.multiple_of(qb * 8, 8)
    s_far = _dot(kc_ref[0, 0, CMP_PAD:CMP_PAD + nc, :].astype(BF16), q)
    n_io = lax.broadcasted_iota(I32, (nc, R), 0)
    s_far = jnp.where(n_io < qb * 8 - CMP_PAD, s_far, -jnp.inf)
    s_win = _dot(kc_ref[0, 0, pl.ds(w0, CMP_WIN), :].astype(BF16), q) + w_ref[0]
    j_w = lax.broadcasted_iota(I32, (CMP_WIN, R), 0)
    i_w = lax.broadcasted_iota(I32, (CMP_WIN, R), 1) & (QB - 1)
    dist_w = i_w - CMP_STRIDE * (j_w - CMP_PAD) - (CMP_LEN - 1)
    s_win = jnp.where((dist_w >= 0) & (j_w >= CMP_PAD - qb * 8), s_win, -jnp.inf)
    m = jnp.maximum(jnp.max(s_far, axis=0, keepdims=True), jnp.max(s_win, axis=0, keepdims=True))
    m = jnp.where(m == -jnp.inf, 0.0, m)
    e_far = jnp.exp2(s_far - m)
    e_win = jnp.exp2(s_win - m)
    l = jnp.sum(e_far, axis=0, keepdims=True) + jnp.sum(e_win, axis=0, keepdims=True)
    inv = 1.0 / jnp.maximum(l, 1e-30)
    p_far = e_far * inv
    p_win = e_win * inv
    vcw_t = vc_ref[0, 0, pl.ds(w0, CMP_WIN), :].T
    o_c = _dot(vct_ref[0, 0].astype(BF16), p_far.astype(BF16)) + _dot(vcw_t.astype(BF16), p_win.astype(BF16))
    ov_win_t = ov_ref[pl.ds(w0, CMP_WIN), :].T
    sc_t = (_dot(ovt_ref[...], heads_sum(p_far).astype(BF16))
            + _dot(ov_win_t.astype(BF16), heads_sum(p_win).astype(BF16)))

    blk = lax.broadcasted_iota(I32, (nb, QB), 0)
    t_io = lax.broadcasted_iota(I32, (nb, QB), 1)
    cur = qb * (QB // SLC_LEN) + t_io // SLC_LEN
    forced = (blk == 0) | (blk == cur) | (blk == cur - 1)
    val = jnp.where(forced, jnp.inf, jnp.where(blk <= cur, sc_t, -jnp.inf))
    sel_t = jnp.zeros((nb, QB), F32)
    for _ in range(n_top):
        mx = jnp.max(val, axis=0, keepdims=True)
        hit = (val == mx) & (val > -jnp.inf)
        first = jnp.min(jnp.where(hit, blk, nb), axis=0, keepdims=True)
        pick = blk == first
        sel_t = jnp.where(pick, 1.0, sel_t)
        val = jnp.where(pick, -jnp.inf, val)
    sel_scr[...] = jnp.where(sel_t > 0.5, 0.0, NEG)

    m_scr[...] = jnp.full(m_scr.shape, NEG, F32)
    acc_scr[...] = jnp.zeros(acc_scr.shape, F32)
    blocks_per_chunk = FAR_CHUNK * QB // SLC_LEN

    chunk = FAR_CHUNK * QB

    def qk(c):
        return _dot(ks_ref[0, 0, pl.ds(pl.multiple_of(c * chunk, chunk), chunk), :], q)

    def soft(c, s):
        rows = sel_scr[pl.ds(pl.multiple_of(c * blocks_per_chunk, blocks_per_chunk), blocks_per_chunk), :]
        drop = jnp.concatenate([jnp.broadcast_to(rows[r:r + 1, :], (SLC_LEN, QB)) for r in range(blocks_per_chunk)],
                               axis=0)
        s = s + jnp.concatenate([drop] * GQA, axis=1)
        m_old = m_scr[...]
        m_new = jnp.maximum(m_old, jnp.max(s, axis=0, keepdims=True))
        p = jnp.exp2(s - m_new)
        v_t = jnp.concatenate([vs_ref[0, 0, c * FAR_CHUNK + j] for j in range(FAR_CHUNK)], axis=1)
        acc_scr[...] = jnp.exp2(m_old - m_new) * acc_scr[...] + _dot(v_t, p.astype(BF16))
        m_scr[...] = m_new

    n_far = jnp.maximum(qb - (FAR_TILE_DIST - 1), 0) // FAR_CHUNK
    last_far = jnp.maximum(n_far - 1, 0)
    sa_scr[...] = qk(0)

    def pair_body(j, carry):
        sb_scr[...] = qk(2 * j + 1)
        soft(2 * j, sa_scr[...])
        sa_scr[...] = qk(jnp.minimum(2 * j + 2, last_far))
        soft(2 * j + 1, sb_scr[...])
        return carry

    lax.fori_loop(0, n_far // 2, pair_body, 0)

    @pl.when(n_far % 2 == 1)
    def _():
        soft(last_far, sa_scr[...])

    def near_body(c, carry):
        tiles = []
        for j in range(FAR_CHUNK):
            delta = qb - (c * FAR_CHUNK + j)
            tiles.append(t_ref[0, jnp.where(delta < 0, TILE_MASKED, delta)])
        soft(c, qk(c) + jnp.concatenate(tiles, axis=0))
        return carry

    lax.fori_loop(n_far, qb // FAR_CHUNK + 1, near_body, 0)
    acc = acc_scr[...]
    o_s = acc[0:HEAD_DIM, :] / acc[HEAD_DIM:HEAD_DIM + 1, :]

    n_wt = WINDOW // QB + 1
    k_t, v_t, b_t = [], [], []
    for j in range(n_wt):
        kb = qb - (n_wt - 1) + j
        kb0 = jnp.maximum(kb, 0)
        k_t.append(kw_ref[0, 0, pl.ds(pl.multiple_of(kb0 * QB, QB), QB), :])
        v_t.append(vw_ref[0, 0, kb0])
        b_t.append(t_ref[0, jnp.where(kb < 0, TILE_MASKED, TILE_WINDOW_EDGE if j == 0 else n_wt - 1 - j)])
    s = _dot(jnp.concatenate(k_t, axis=0), q) + jnp.concatenate(b_t, axis=0)
    p = jnp.exp2(s - jnp.max(s, axis=0, keepdims=True))
    acc = _dot(jnp.concatenate(v_t, axis=1), p.astype(BF16))
    o_w = acc[0:HEAD_DIM, :] / acc[HEAD_DIM:HEAD_DIM + 1, :]

    gates = g_ref[0, 0]
    outs = []
    for hq in range(GQA):
        cols = slice(hq * QB, (hq + 1) * QB)
        c = hq * N_BRANCH
        outs.append(gates[c:c + 1, :] * o_c[0:HEAD_DIM, cols] + gates[c + 1:c + 2, :] * o_s[:, cols]
                    + gates[c + 2:c + 3, :] * o_w[:, cols])
    o_ref[0] = jnp.concatenate(outs, axis=0).T


def _attn(q_t, kx, vx_t, kvc, kvc_t, gates_t, t_tab, w_tab, ov_t, ov):
    B, _, W, _ = q_t.shape
    S = kx.shape[2]
    ncp = kvc.shape[2]
    nb = ov.shape[1]
    keys = lambda j: pl.BlockSpec((1, 1, S, W), lambda b, g, t: (b, j * N_KV + g, 0, 0))
    vals = lambda j: pl.BlockSpec((1, 1, S // QB, W, QB), lambda b, g, t: (b, j * N_KV + g, 0, 0, 0))
    cmp = lambda j: pl.BlockSpec((1, 1, ncp, W), lambda b, g, t: (b, j * N_KV + g, 0, 0))
    return pl.pallas_call(
        _attn_kernel,
        grid=(B, N_KV, S // QB),
        in_specs=[pl.BlockSpec((1, 1, W, GQA * QB), lambda b, g, t: (b, g, 0, t)),
                  keys(0), vals(0), keys(1), vals(1), cmp(0), cmp(1),
                  pl.BlockSpec((1, 1, W, ncp - CMP_WIN), lambda b, g, t: (b, N_KV + g, 0, 0)),
                  pl.BlockSpec((1, 1, 16, QB), lambda b, g, t: (b, g, 0, t)),
                  pl.BlockSpec((1,) + t_tab.shape[1:], lambda b, g, t: (g, 0, 0, 0)),
                  pl.BlockSpec((1, CMP_WIN, GQA * QB), lambda b, g, t: (g, 0, 0)),
                  pl.BlockSpec(ov_t.shape, lambda b, g, t: (0, 0)),
                  pl.BlockSpec(ov.shape, lambda b, g, t: (0, 0))],
        out_specs=pl.BlockSpec((1, QB, GQA * HEAD_DIM), lambda b, g, t: (b, t, g)),
        out_shape=jax.ShapeDtypeStruct((B, S, N_HEADS * HEAD_DIM), F32),
        scratch_shapes=[pltpu.VMEM((1, GQA * QB), F32), pltpu.VMEM((W, GQA * QB), F32), pltpu.VMEM((nb, QB), F32),
                        pltpu.VMEM((FAR_CHUNK * QB, GQA * QB), F32), pltpu.VMEM((FAR_CHUNK * QB, GQA * QB), F32)],
        compiler_params=_params("parallel", "parallel", "arbitrary"),
        name="attn",
    )(q_t, kx, vx_t, kx, vx_t, kvc, kvc, kvc_t, gates_t, t_tab, w_tab, ov_t, ov)


def _rglru_kernel(xr_ref, gr_ref, cw_ref, cb_ref, wa_ref, ba_ref, wx_ref, bx_ref, lam_ref, gw_ref, o_ref,
                  xbuf, hprev, a_scr, u_scr, h_scr):
    ts = xr_ref.shape[1]
    C = xr_ref.shape[2]

    @pl.when(pl.program_id(1) == 0)
    def _():
        xbuf[0:8, :] = jnp.zeros((8, C), F32)
        hprev[...] = jnp.zeros(hprev.shape, F32)

    xbuf[8:8 + ts, :] = xr_ref[0]
    xc = cb_ref[...] + jnp.zeros((ts, C), F32)
    for j in range(CONV_WIDTH):
        xc = xc + cw_ref[j:j + 1, :] * xbuf[pl.ds(8 - (CONV_WIDTH - 1) + j, ts), :]
    xbuf[0:8, :] = xbuf[ts:ts + 8, :]

    xcb = xc.astype(BF16)
    r = _sigmoid(_dot(xcb, wa_ref[...]) + ba_ref[...])
    i = _sigmoid(_dot(xcb, wx_ref[...]) + bx_ref[...])
    z = -lam_ref[...]
    softplus = jnp.maximum(z, 0.0) + jnp.log(1.0 + jnp.exp(-jnp.abs(z)))
    log_a = -LRU_C * r * softplus
    a_scr[...] = jnp.exp(log_a)
    u_scr[...] = jnp.sqrt(1.0 - jnp.exp(2.0 * log_a)) * (i * xc)

    row = lax.broadcasted_iota(I32, (8, C), 0)

    def body(k, h):
        r0 = pl.multiple_of(k * 8, 8)
        a = a_scr[pl.ds(r0, 8), :]
        b = u_scr[pl.ds(r0, 8), :]
        for s in (1, 2, 4):
            keep = row >= s
            b = jnp.where(keep, a * pltpu.roll(b, s, 0) + b, b)
            a = jnp.where(keep, a * pltpu.roll(a, s, 0), a)
        hh = a * h + b
        h_scr[pl.ds(r0, 8), :] = hh
        return jnp.broadcast_to(hh[7:8, :], (8, C))

    hprev[...] = lax.fori_loop(0, ts // 8, body, hprev[...])
    out = h_scr[...] * _gelu(gr_ref[0])
    o_ref[0] = (_rms(out) * gw_ref[...]).astype(BF16)


def _rglru(xr, gr, cw, cb, wa_bd, ba, wx_bd, bx, lam, gw, ts=512):
    B, S, C = xr.shape
    ts = min(ts, S)
    vec = pl.BlockSpec((1, C), lambda b, t: (0, 0))
    mat = pl.BlockSpec((C, C), lambda b, t: (0, 0))
    seq = pl.BlockSpec((1, ts, C), lambda b, t: (b, t, 0))
    return pl.pallas_call(
        _rglru_kernel,
        grid=(B, S // ts),
        in_specs=[seq, seq, pl.BlockSpec((CONV_WIDTH, C), lambda b, t: (0, 0)), vec, mat, vec, mat, vec, vec, vec],
        out_specs=seq,
        out_shape=jax.ShapeDtypeStruct((B, S, C), BF16),
        scratch_shapes=[pltpu.VMEM((ts + 8, C), F32), pltpu.VMEM((8, C), F32), pltpu.VMEM((ts, C), F32),
                        pltpu.VMEM((ts, C), F32), pltpu.VMEM((ts, C), F32)],
        compiler_params=_params("parallel", "arbitrary"),
        name="rglru",
    )(xr, gr, cw, cb, wa_bd, ba, wx_bd, bx, lam, gw)


def _to_row_tiles(ref, x):
    rows = x.shape[0]
    for s in range(ROW_TILE):
        ref[pl.ds(s, rows, stride=ROW_TILE), :] = x[:, s * 128:(s + 1) * 128]


def _from_row_tiles(ref, rows, start=0):
    return jnp.concatenate([ref[pl.ds(start + s, rows, stride=ROW_TILE), :] for s in range(ROW_TILE)], axis=1)


def _outproj_kernel(oa_ref, yr_ref, x_ref, mod_ref, gaw_ref, wo_ref, n2_ref, wr_ref, x1_ref, h2_ref, st_ref):
    da = oa_ref.shape[2]
    ya = (_rms(oa_ref[0]) * gaw_ref[...]).astype(BF16)
    mix = _dot(ya, wo_ref[0:da, :]) + _dot(yr_ref[0], wo_ref[da:, :])
    x1 = x_ref[0] + mod_ref[0, 2:3, :] * mix
    x1_ref[0] = x1
    h2 = (_rms(x1) * n2_ref[...]) * (1.0 + mod_ref[0, 4:5, :]) + mod_ref[0, 3:4, :]
    h2_ref[0] = h2
    st_ref[...] = _sigmoid(_dot_nt(wr_ref[...], h2, precision=lax.Precision.HIGHEST))


def _outproj(oa, yr, x, mod, gaw, wo, n2, wr_t, tm=256):
    B, S, D = x.shape
    E = wr_t.shape[0]
    nt = S // tm
    row = lambda w: pl.BlockSpec((1, tm, w), lambda b, t: (b, t, 0))
    return pl.pallas_call(
        _outproj_kernel,
        grid=(B, nt),
        in_specs=[row(oa.shape[2]), row(yr.shape[2]), row(D),
                  pl.BlockSpec((1, 6, D), lambda b, t: (b, 0, 0)),
                  pl.BlockSpec((1, oa.shape[2]), lambda b, t: (0, 0)),
                  pl.BlockSpec(wo.shape, lambda b, t: (0, 0)),
                  pl.BlockSpec((1, D), lambda b, t: (0, 0)),
                  pl.BlockSpec((E, D), lambda b, t: (0, 0))],
        out_specs=[row(D), row(D), pl.BlockSpec((E, tm), lambda b, t: (0, b * nt + t))],
        out_shape=[jax.ShapeDtypeStruct((B, S, D), F32), jax.ShapeDtypeStruct((B, S, D), F32),
                   jax.ShapeDtypeStruct((E, B * S), F32)],
        compiler_params=_params("parallel", "parallel"),
        name="outproj",
    )(oa, yr, x, mod, gaw, wo, n2, wr_t)


def _route_kernel(s_ref, b_ref, eid_ref, wgt_ref, rank_ref, cnt_ref, tcnt_ref, carry):
    E, tn = s_ref.shape
    per = E // N_EXPERT_GROUPS

    @pl.when(pl.program_id(0) == 0)
    def _():
        carry[...] = jnp.zeros(carry.shape, F32)

    s = s_ref[...]
    s_sel = s + b_ref[...]
    eidx = lax.broadcasted_iota(I32, (E, tn), 0)

    grp = []
    for gi in range(N_EXPERT_GROUPS):
        xg = s_sel[gi * per:(gi + 1) * per, :]
        ig = lax.broadcasted_iota(I32, (per, tn), 0)
        m1 = jnp.max(xg, axis=0, keepdims=True)
        f1 = jnp.min(jnp.where(xg == m1, ig, per), axis=0, keepdims=True)
        m2 = jnp.max(jnp.where(ig == f1, -jnp.inf, xg), axis=0, keepdims=True)
        grp.append(m1 + m2)
    val = []
    for gi in range(N_EXPERT_GROUPS):
        rank = jnp.zeros((1, tn), I32)
        for gj in range(N_EXPERT_GROUPS):
            if gj == gi:
                continue
            ahead = (grp[gj] > grp[gi]) | ((grp[gj] == grp[gi]) & (gj < gi))
            rank = rank + ahead.astype(I32)
        val.append(jnp.where(rank < TOP_GROUPS, s_sel[gi * per:(gi + 1) * per, :], -jnp.inf))
    val = jnp.concatenate(val, axis=0)

    eids, tops = [], []
    hot = jnp.zeros((E, tn), F32)
    for _ in range(TOP_K):
        mx = jnp.max(val, axis=0, keepdims=True)
        first = jnp.min(jnp.where(val == mx, eidx, E), axis=0, keepdims=True)
        pick = eidx == first
        eids.append(first)
        tops.append(jnp.sum(jnp.where(pick, s, 0.0), axis=0, keepdims=True))
        hot = jnp.where(pick, 1.0, hot)
        val = jnp.where(pick, -jnp.inf, val)
    denom = tops[0]
    for t in tops[1:]:
        denom = denom + t

    ti = lax.broadcasted_iota(I32, (tn, tn), 0)
    tj = lax.broadcasted_iota(I32, (tn, tn), 1)
    upper = jnp.where(ti < tj, 1.0, 0.0).astype(BF16)
    before = _dot(hot.astype(BF16), upper) + carry[...]
    for k in range(TOP_K):
        eid_ref[k:k + 1, :] = eids[k]
        wgt_ref[k:k + 1, :] = ROUTED_SCALE * tops[k] / denom
        rank_ref[k:k + 1, :] = jnp.sum(jnp.where(eidx == eids[k], before, 0.0), axis=0, keepdims=True).astype(I32)
    tile_count = jnp.sum(hot, axis=1, keepdims=True)
    tcnt_ref[0] = tile_count
    carry[...] = carry[...] + tile_count
    cnt_ref[...] = carry[...]


def _route(s_t, b_col, tn=256):
    E, N = s_t.shape
    tn = min(tn, N)
    out = pl.BlockSpec((TOP_K, tn), lambda t: (0, t))
    return pl.pallas_call(
        _route_kernel,
        grid=(N // tn,),
        in_specs=[pl.BlockSpec((E, tn), lambda t: (0, t)), pl.BlockSpec((E, 1), lambda t: (0, 0))],
        out_specs=[out, out, out, pl.BlockSpec((E, 1), lambda t: (0, 0)), pl.BlockSpec((1, E, 1), lambda t: (t, 0, 0))],
        out_shape=[jax.ShapeDtypeStruct((TOP_K, N), I32), jax.ShapeDtypeStruct((TOP_K, N), F32),
                   jax.ShapeDtypeStruct((TOP_K, N), I32), jax.ShapeDtypeStruct((E, 1), F32),
                   jax.ShapeDtypeStruct((N // tn, E, 1), F32)],
        scratch_shapes=[pltpu.VMEM((E, 1), F32)],
        compiler_params=_params("arbitrary"),
        name="route",
    )(s_t, b_col)


def _tilepos_kernel(eid_ref, rank_ref, base_ref, o_ref):
    E = base_ref.shape[1]
    tn = eid_ref.shape[1]
    eidx = lax.broadcasted_iota(I32, (E, tn), 0)
    for k in range(TOP_K):
        base = jnp.sum(jnp.where(eidx == eid_ref[k:k + 1, :], base_ref[0], 0.0), axis=0, keepdims=True)
        o_ref[k:k + 1, :] = base.astype(I32) + rank_ref[k:k + 1, :]


def _tilepos(eid_t, rank_t, base, tn):
    K, N = eid_t.shape
    blk = pl.BlockSpec((K, tn), lambda t: (0, t))
    return pl.pallas_call(
        _tilepos_kernel,
        grid=(N // tn,),
        in_specs=[blk, blk, pl.BlockSpec((1,) + base.shape[1:], lambda t: (t, 0, 0))],
        out_specs=blk,
        out_shape=jax.ShapeDtypeStruct((K, N), I32),
        compiler_params=_params("parallel"),
        name="tilepos",
    )(eid_t, rank_t, base)


def _run_copies(src_ref, cnt_ref, dst_ref, copy):
    def go(e, carry):
        n = cnt_ref[0, 0, e]

        @pl.when(n > 0)
        def _():
            size = pl.multiple_of(n * ROW_TILE, ROW_TILE)
            copy(pl.ds(pl.multiple_of(src_ref[0, 0, e] * ROW_TILE, ROW_TILE), size),
                 pl.ds(pl.multiple_of(dst_ref[0, 0, e] * ROW_TILE, ROW_TILE), size)).start()
        return carry
    lax.fori_loop(0, cnt_ref.shape[2], go, 0)


def _dispatch_kernel(src_ref, cnt_ref, dst_ref, pos_ref, h_ref, xs_hbm, xbuf, sem):
    tn = h_ref.shape[0]
    hb = h_ref[...].astype(BF16)
    slab = 512
    for sl in range(TOP_K * tn // slab):
        row = lax.broadcasted_iota(I32, (slab, tn), 0) + sl * slab
        onehot = jnp.zeros((slab, tn), F32)
        for k in range(TOP_K):
            onehot = jnp.where(row == pos_ref[k:k + 1, :], 1.0, onehot)
        xp = _dot(onehot.astype(BF16), hb)
        for s in range(ROW_TILE):
            xbuf[pl.ds(sl * slab * ROW_TILE + s, slab, stride=ROW_TILE), :] = xp[:, s * 128:(s + 1) * 128]
    _run_copies(src_ref, cnt_ref, dst_ref,
                lambda src, dst: pltpu.make_async_copy(xbuf.at[src], xs_hbm.at[dst], sem.at[0]))
    pltpu.make_async_copy(xbuf, xs_hbm.at[pl.ds(0, xbuf.shape[0])], sem.at[0]).wait()


def _run_specs(runs):
    return [pl.BlockSpec((1, 1, r.shape[2]), lambda t: (t, 0, 0), memory_space=pltpu.SMEM) for r in runs]


def _dispatch(runs, pos_t, h2, tn):
    N, D = h2.shape
    W = D // ROW_TILE
    return pl.pallas_call(
        _dispatch_kernel,
        grid=(N // tn,),
        in_specs=_run_specs(runs) + [pl.BlockSpec((TOP_K, tn), lambda t: (0, t)),
                                     pl.BlockSpec((tn, D), lambda t: (t, 0))],
        out_specs=pl.BlockSpec(memory_space=pl.ANY),
        out_shape=jax.ShapeDtypeStruct((N * TOP_K * ROW_TILE, W), F32),
        scratch_shapes=[pltpu.VMEM((TOP_K * tn * ROW_TILE, W), F32), pltpu.SemaphoreType.DMA((1,))],
        compiler_params=_params("arbitrary"),
        name="dispatch",
    )(*runs, pos_t, h2)


def _experts_kernel(first_ref, size_ref, e_ref, new_ref, next_ref, wslot_ref, total_ref,
                    w1_hbm, w3_hbm, w2_hbm, xs_hbm, ys_hbm,
                    xbuf, ybuf, w1f, w3f, w2f, w1b, w3b, w2b, xsem, ysem, wsem):
    total = total_ref[0]

    def w_copies(e, slot):
        return [pltpu.make_async_copy(src.at[e], dst.at[slot], wsem.at[slot])
                for src, dst in ((w1_hbm, w1f), (w3_hbm, w3f), (w2_hbm, w2f))]

    def rows(c):
        return (pl.multiple_of(first_ref[c] * ROW_TILE, ROW_TILE), pl.multiple_of(size_ref[c] * ROW_TILE, ROW_TILE))

    def x_copy(c, slot):
        first, size = rows(c)
        return pltpu.make_async_copy(xs_hbm.at[pl.ds(first, size)], xbuf.at[slot, pl.ds(0, size)], xsem.at[slot])

    def y_copy(c, slot):
        first, size = rows(c)
        return pltpu.make_async_copy(ybuf.at[slot, pl.ds(0, size)], ys_hbm.at[pl.ds(first, size)], ysem.at[slot])

    xbuf[...] = jnp.zeros(xbuf.shape, F32)
    for cp in w_copies(e_ref[0], 0):
        cp.start()
    x_copy(0, 0).start()

    def body(c, carry):
        slot = c % 2

        @pl.when(new_ref[c] == 1)
        def _():
            ws = wslot_ref[c]
            for cp in w_copies(e_ref[c], ws):
                cp.wait()
            w1b[...] = w1f[ws].astype(BF16)
            w3b[...] = w3f[ws].astype(BF16)
            w2b[...] = w2f[ws].astype(BF16)

            @pl.when(next_ref[c] >= 0)
            def _():
                for cp in w_copies(next_ref[c], 1 - ws):
                    cp.start()

        x_copy(c, slot).wait()

        @pl.when(c + 1 < total)
        def _():
            x_copy(c + 1, 1 - slot).start()

        x = _from_row_tiles(xbuf.at[slot], MOE_BLOCK).astype(BF16)
        h1 = _dot(x, w1b[...])
        h3 = _dot(x, w3b[...])
        hid = (h1 * _sigmoid(h1)) * h3
        y = _dot(hid.astype(BF16), w2b[...])

        @pl.when(c >= 2)
        def _():
            y_copy(c - 2, slot).wait()

        _to_row_tiles(ybuf.at[slot], y)
        y_copy(c, slot).start()
        return carry

    lax.fori_loop(0, total, body, 0)

    @pl.when(total >= 2)
    def _():
        y_copy(total - 2, total % 2).wait()

    y_copy(total - 1, (total - 1) % 2).wait()


def _expert_chunks(counts, n_rows):
    E = counts.shape[0]
    n_max = n_rows // MOE_BLOCK + E
    starts = jnp.cumsum(counts) - counts
    n_chunks = (counts + MOE_BLOCK - 1) // MOE_BLOCK
    c_end = jnp.cumsum(n_chunks)
    c = jnp.arange(n_max, dtype=I32)
    e = jnp.minimum(jnp.sum(c_end[None, :] <= c[:, None], axis=1), E - 1).astype(I32)
    j = c - (c_end - n_chunks)[e]
    size = jnp.clip(counts[e] - j * MOE_BLOCK, 0, MOE_BLOCK)
    has_rows = counts > 0
    ids = jnp.where(has_rows, jnp.arange(E, dtype=I32), E)
    later = jnp.flip(lax.cummin(jnp.flip(ids)))
    nxt = jnp.concatenate([later[1:], jnp.full((1,), E, I32)])
    nxt = jnp.where(nxt >= E, -1, nxt)
    ordinal = jnp.cumsum(has_rows.astype(I32)) - 1
    to_i32 = lambda a: a.astype(I32)
    return tuple(map(to_i32, (starts[e] + j * MOE_BLOCK, size, e, j == 0, nxt[e], ordinal[e] % 2, c_end[-1:])))


def _experts(chunks, xs, w1, w3, w2):
    rows_total, W = xs.shape
    _, D, De = w1.shape
    any_space = pl.BlockSpec(memory_space=pl.ANY)
    grid_spec = pltpu.PrefetchScalarGridSpec(
        num_scalar_prefetch=len(chunks),
        grid=(1,),
        in_specs=[any_space] * 4,
        out_specs=any_space,
        scratch_shapes=[pltpu.VMEM((2, MOE_BLOCK * ROW_TILE, W), F32), pltpu.VMEM((2, MOE_BLOCK * ROW_TILE, W), F32),
                        pltpu.VMEM((2, D, De), F32), pltpu.VMEM((2, D, De), F32), pltpu.VMEM((2, De, D), F32),
                        pltpu.VMEM((D, De), BF16), pltpu.VMEM((D, De), BF16), pltpu.VMEM((De, D), BF16),
                        pltpu.SemaphoreType.DMA((2,)), pltpu.SemaphoreType.DMA((2,)), pltpu.SemaphoreType.DMA((2,))],
    )
    return pl.pallas_call(
        _experts_kernel,
        grid_spec=grid_spec,
        out_shape=jax.ShapeDtypeStruct((rows_total, W), F32),
        compiler_params=_params("arbitrary"),
        name="experts",
    )(*chunks, w1, w3, w2, xs)


def _combine_kernel(src_ref, cnt_ref, dst_ref, ys_hbm, pos_ref, w_ref, h2_ref, x1_ref, mod_ref, ws1_ref, ws3_ref,
                    ws2_ref, fw_ref, o_ref, ybuf, sem):
    tm = h2_ref.shape[0]
    n_rows = TOP_K * tm

    _run_copies(src_ref, cnt_ref, dst_ref,
                lambda src, dst: pltpu.make_async_copy(ys_hbm.at[dst], ybuf.at[src], sem.at[0]))

    h = h2_ref[...].astype(BF16)
    a = _dot(h, ws1_ref[...])
    hid = (a * _sigmoid(a)) * _dot(h, ws3_ref[...])
    shared = _dot(hid.astype(BF16), ws2_ref[...])

    lane = lax.broadcasted_iota(I32, (tm, n_rows), 1)
    weights = jnp.zeros((tm, n_rows), F32)
    for k in range(TOP_K):
        weights = jnp.where(lane == pos_ref[:, k:k + 1], w_ref[:, k:k + 1], weights)
    w_hi = weights.astype(BF16)
    w_lo = (weights - w_hi.astype(F32)).astype(BF16)
    pltpu.make_async_copy(ys_hbm.at[pl.ds(0, ybuf.shape[0])], ybuf, sem.at[0]).wait()
    y = _from_row_tiles(ybuf, n_rows).astype(BF16)
    routed = _dot(w_hi, y) + _dot(w_lo, y)
    x2 = x1_ref[...] + mod_ref[0, 5:6, :] * (routed + shared)
    o_ref[...] = _rms(x2) * fw_ref[...]


def _combine(runs, ys, pos, wgt, h2, x1, mod, ws1, ws3, ws2, fw, seq, tm):
    N, D = h2.shape
    nt = seq // tm
    row = pl.BlockSpec((tm, D), lambda t: (t, 0))
    full = lambda a: pl.BlockSpec(a.shape, lambda t: (0,) * a.ndim)
    return pl.pallas_call(
        _combine_kernel,
        grid=(N // tm,),
        in_specs=_run_specs(runs) + [pl.BlockSpec(memory_space=pl.ANY),
                                     pl.BlockSpec((tm, TOP_K), lambda t: (t, 0)),
                                     pl.BlockSpec((tm, TOP_K), lambda t: (t, 0)),
                                     row, row,
                                     pl.BlockSpec((1, 6, D), lambda t: (t // nt, 0, 0)),
                                     full(ws1), full(ws3), full(ws2), full(fw)],
        out_specs=row,
        out_shape=jax.ShapeDtypeStruct((N, D), F32),
        scratch_shapes=[pltpu.VMEM((TOP_K * tm * ROW_TILE, D // ROW_TILE), F32), pltpu.SemaphoreType.DMA((1,))],
        compiler_params=_params("arbitrary"),
        name="combine",
    )(*runs, ys, pos, wgt, h2, x1, mod, ws1, ws3, ws2, fw)


def _t5_bucket(dist):
    n = jnp.maximum(dist, 0)
    max_exact = REL_BUCKETS // 2
    nf = jnp.maximum(n, 1).astype(F32)
    large = max_exact + (jnp.log(nf / max_exact) / math.log(REL_MAX_DIST / max_exact)
                         * (REL_BUCKETS - max_exact)).astype(I32)
    large = jnp.minimum(large, REL_BUCKETS - 1)
    return jnp.where(n < max_exact, n, large)


def _bias_kernel(tbl_ref, far_ref, bt_ref, bw_ref, t_ref, w_ref):
    g = pl.program_id(0)
    d = pl.program_id(1)

    def lookup(bkt, head):
        out = jnp.full(bkt.shape, tbl_ref[0, head], F32)
        for b in range(1, REL_BUCKETS):
            out = jnp.where(bkt == b, tbl_ref[b, head], out)
        return out - far_ref[head]

    j = lax.broadcasted_iota(I32, (QB, QB), 0)
    i = lax.broadcasted_iota(I32, (QB, QB), 1)
    keep = ((d != 0) | (i >= j)) & (d != TILE_MASKED) & ((d != TILE_WINDOW_EDGE) | (j > i))
    for hq in range(GQA):
        t_ref[0, 0, :, hq * QB:(hq + 1) * QB] = jnp.where(keep, lookup(bt_ref[0], GQA * g + hq), NEG)

    @pl.when(d == 0)
    def _():
        for hq in range(GQA):
            w_ref[0, :, hq * QB:(hq + 1) * QB] = lookup(bw_ref[...], GQA * g + hq)


def _bias_tables(rel_table):
    tbl = rel_table.astype(F32) * LOG2E
    far_hi = tbl[REL_BUCKETS - 1].astype(BF16)
    far_lo = (tbl[REL_BUCKETS - 1] - far_hi.astype(F32)).astype(BF16)
    far = far_hi.astype(F32) + far_lo.astype(F32)
    q_tail = jnp.zeros((N_HEADS, HEAD_DIM), F32).at[:, 0].set(far_hi.astype(F32)).at[:, 1].set(far_lo.astype(F32))
    i = jnp.arange(QB)
    tile_dist = np.array(list(range(N_BIAS_TILES)) + [0, WINDOW // QB])
    d_t = (QB * jnp.asarray(tile_dist)[:, None, None] + i[None, None, :] - i[None, :, None])
    d_w = i[None, :] - CMP_STRIDE * (jnp.arange(CMP_WIN)[:, None] - CMP_PAD) - (CMP_LEN - 1)
    n_tiles = len(tile_dist)
    t_tab, w_tab = pl.pallas_call(
        _bias_kernel,
        grid=(N_KV, n_tiles),
        in_specs=[pl.BlockSpec(memory_space=pltpu.SMEM),
                  pl.BlockSpec(memory_space=pltpu.SMEM),
                  pl.BlockSpec((1, QB, QB), lambda g, d: (d, 0, 0)),
                  pl.BlockSpec((CMP_WIN, QB), lambda g, d: (0, 0))],
        out_specs=[pl.BlockSpec((1, 1, QB, GQA * QB), lambda g, d: (g, d, 0, 0)),
                   pl.BlockSpec((1, CMP_WIN, GQA * QB), lambda g, d: (g, 0, 0))],
        out_shape=[jax.ShapeDtypeStruct((N_KV, n_tiles, QB, GQA * QB), F32),
                   jax.ShapeDtypeStruct((N_KV, CMP_WIN, GQA * QB), F32)],
        compiler_params=_params("parallel", "arbitrary"),
        name="bias",
    )(tbl, far, _t5_bucket(d_t), _t5_bucket(d_w))
    far_min = min(FAR_TILE_DIST * QB - QB + 1, CMP_STRIDE * (CMP_PAD + 1) - (CMP_LEN - 1))
    max_exact = REL_BUCKETS // 2
    assert math.log(far_min / max_exact) / math.log(REL_MAX_DIST / max_exact) * max_exact > max_exact - 0.75
    return t_tab, w_tab, q_tail


def _overlap_tables(nc, nb):
    c_start = CMP_STRIDE * np.arange(nc)[:, None]
    s_start = SLC_LEN * np.arange(nb)[None, :]
    ov = ((c_start < s_start + SLC_LEN) & (c_start + CMP_LEN > s_start)).astype(np.float32)
    ov[nc - 1:] = 0.0
    out = np.zeros((nc + CMP_WIN, nb), np.float32)
    out[CMP_PAD:CMP_PAD + nc] = ov
    return jnp.asarray(ov.T, dtype=BF16), jnp.asarray(out)


def _block_diag(w):
    nblk, bs, _ = w.shape
    eye = jnp.eye(nblk, dtype=w.dtype)
    return (eye[:, None, :, None] * w[:, :, None, :]).reshape(nblk * bs, nblk * bs)


def _layer(x, c8, w_ada, b_ada, norm1_w, w_in, cmp_pos, cmp_k_w1, cmp_k_w2, cmp_v_w1, cmp_v_w2, rel_table,
           conv_w, conv_b, lru_wa, lru_ba, lru_wx, lru_bx, lru_lambda, gnorm_attn_w, gnorm_rnn_w, w_out,
           norm2_w, w_router, b_router, w1, w3, w2, ws1, ws3, ws2, final_norm_w):
    B, S, D = x.shape
    N = B * S
    E = w_router.shape[1]
    d_attn = N_HEADS * HEAD_DIM
    d_rnn = D - d_attn
    assert S % 512 == 0 and S // SLC_LEN <= 128

    mod = _ada(c8, w_ada, b_ada[None, :])[:B].reshape(B, 6, D)

    n_kv_cols = 6 * N_KV * HEAD_DIM
    n_gate = N_BRANCH * N_HEADS
    wq, wkv, wg, wxr, wgr = jnp.split(w_in, np.cumsum([d_attn, n_kv_cols, n_gate, d_rnn]).tolist(), axis=1)
    wg = jnp.pad(wg.reshape(D, N_KV, GQA * N_BRANCH), ((0, 0), (0, 0), (0, 128 - GQA * N_BRANCH))).reshape(D, 256)
    w_pad = jnp.concatenate([wq, wkv, wg, wxr, wgr], axis=1).astype(BF16)
    t_tab, w_tab, q_tail = _bias_tables(rel_table)
    q_t, kx, vx_t, kvc, gates_t, xr, gr = _inproj(x, mod, norm1_w[None, :], w_pad, q_tail)

    nc = S // CMP_STRIDE
    kvc16 = kvc.reshape(B, 4, nc, CMP_STRIDE * HEAD_DIM)
    pos2 = cmp_pos.reshape(2, CMP_STRIDE * HEAD_DIM)
    kvc_pad, kvc_t = _compress(kvc16, pos2, jnp.stack([cmp_k_w1, cmp_v_w1]), jnp.stack([cmp_k_w2, cmp_v_w2]))
    o_attn = _attn(q_t, kx, vx_t, kvc_pad, kvc_t, gates_t, t_tab, w_tab, *_overlap_tables(nc, 128))

    y_rnn = _rglru(xr, gr, conv_w.reshape(CONV_WIDTH, d_rnn), conv_b[None, :], _block_diag(lru_wa).astype(BF16),
                   lru_ba[None, :], _block_diag(lru_wx).astype(BF16), lru_bx[None, :], lru_lambda[None, :],
                   gnorm_rnn_w[None, :])

    x1, h2, s_t = _outproj(o_attn, y_rnn, x, mod, gnorm_attn_w[None, :], w_out.astype(BF16), norm2_w[None, :],
                           w_router.T)

    tile = min(MOE_TILE, S)
    eid_t, wgt_t, rank_t, counts, tile_counts = _route(s_t, b_router[:, None], tile)
    counts = counts[:, 0].astype(I32)
    tcnt = tile_counts[:, :, 0].astype(I32)
    off = jnp.cumsum(counts) - counts
    run_src = jnp.cumsum(tcnt, axis=1) - tcnt
    earlier = jnp.cumsum(tcnt, axis=0) - tcnt
    run_dst = off[None, :] + earlier
    runs = tuple(r.astype(I32)[:, None, :] for r in (run_src, tcnt, run_dst))
    pos_t = _tilepos(eid_t, rank_t, (run_src - earlier).astype(F32)[:, :, None], tile)

    h2f = h2.reshape(N, D)
    xs = _dispatch(runs, pos_t, h2f, tile)
    ys = _experts(_expert_chunks(counts, N * TOP_K), xs, w1, w3, w2)
    out = _combine(runs, ys, pos_t.T, wgt_t.T, h2f, x1.reshape(N, D), mod, ws1.astype(BF16), ws3.astype(BF16),
                   ws2.astype(BF16), final_norm_w[None, :], S, tile)
    return out.reshape(B, S, D)


def kernel(x, c, w_ada, b_ada, norm1_w, w_in, cmp_pos, cmp_k_w1, cmp_k_w2, cmp_v_w1, cmp_v_w2, rel_table, conv_w, conv_b, lru_wa, lru_ba, lru_wx, lru_bx, lru_lambda, gnorm_attn_w, gnorm_rnn_w, w_out, norm2_w, w_router, b_router, w1, w3, w2, ws1, ws3, ws2, final_norm_w):
    assert w_ada.shape[0] == 1
    c8 = jnp.pad(c, ((0, 8 - c.shape[0]), (0, 0)))
    return _layer(x, c8, w_ada[0], b_ada[0], norm1_w[0], w_in[0], cmp_pos[0], cmp_k_w1[0], cmp_k_w2[0],
                  cmp_v_w1[0], cmp_v_w2[0], rel_table, conv_w[0], conv_b[0], lru_wa[0], lru_ba[0], lru_wx[0],
                  lru_bx[0], lru_lambda[0], gnorm_attn_w[0], gnorm_rnn_w[0], w_out[0], norm2_w[0], w_router[0],
                  b_router[0], w1[0], w3[0], w2[0], ws1[0], ws3[0], ws2[0], final_norm_w)
```

```python
import functools
import math

import jax
import jax.numpy as jnp
import numpy as np
from jax import lax
from jax.experimental import pallas as pl
from jax.experimental.pallas import tpu as pltpu

F32 = jnp.float32
BF16 = jnp.bfloat16
I32 = jnp.int32

HEAD_DIM = 64
N_HEADS = 8
N_KV = 2
GQA = N_HEADS // N_KV
N_BRANCH = 3
CONV_WIDTH = 4
LRU_C = 8.0
CMP_LEN = 32
CMP_STRIDE = 16
SLC_LEN = 64
SLC_TOP = 16
WINDOW = 512
QB = 128
REL_BUCKETS = 32
REL_MAX_DIST = 1024
N_EXPERT_GROUPS = 8
TOP_GROUPS = 4
TOP_K = 8
ROUTED_SCALE = 2.5
MOE_BLOCK = 256
MOE_TILE = 256
ROW_TILE = 8
EPS = 1e-6
NEG = -1e30
LOG2E = math.log2(math.e)
CMP_PAD = 120
CMP_WIN = 128
N_BIAS_TILES = 11
TILE_MASKED = N_BIAS_TILES
TILE_WINDOW_EDGE = N_BIAS_TILES + 1
FAR_TILE_DIST = 8
FAR_CHUNK = 4
VMEM_LIMIT = 52 * 1024 * 1024


def _dot(a, b, **kw):
    return jnp.dot(a, b, preferred_element_type=F32, **kw)


def _dot_nt(a, b, **kw):
    return lax.dot_general(a, b, (((1,), (1,)), ((), ())), preferred_element_type=F32, **kw)


def _gelu(x):
    return 0.5 * x * (1.0 + jnp.tanh(math.sqrt(2.0 / math.pi) * (x + 0.044715 * (x * x * x))))


def _sigmoid(x):
    return 1.0 / (1.0 + jnp.exp(-x))


def _rms(x):
    return x * lax.rsqrt(jnp.mean(x * x, axis=-1, keepdims=True) + EPS)


def _params(*sem):
    return pltpu.CompilerParams(dimension_semantics=sem, vmem_limit_bytes=VMEM_LIMIT)


def _ada_kernel(c_ref, w_ref, b_ref, o_ref):
    c = c_ref[...]
    a = c * _sigmoid(c)
    o_ref[...] = _dot(a, w_ref[...], precision=lax.Precision.HIGHEST) + b_ref[...]


def _ada(c8, w, b):
    d, n = w.shape
    tn = 1536
    return pl.pallas_call(
        _ada_kernel,
        grid=(n // tn,),
        in_specs=[pl.BlockSpec((8, d), lambda j: (0, 0)),
                  pl.BlockSpec((d, tn), lambda j: (0, j)),
                  pl.BlockSpec((1, tn), lambda j: (0, j))],
        out_specs=pl.BlockSpec((8, tn), lambda j: (0, j)),
        out_shape=jax.ShapeDtypeStruct((8, n), F32),
        compiler_params=_params("parallel"),
        name="ada",
    )(c8, w, b)


def _inproj_kernel(x_ref, mod_ref, nw_ref, w_ref, qt_ref, q_ref, kx_ref, vx_ref, kvc_ref, g_ref, xr_ref, gr_ref):
    h = _rms(x_ref[0]) * nw_ref[...]
    h = h * (1.0 + mod_ref[0, 1:2, :]) + mod_ref[0, 0:1, :]
    p = _dot(h.astype(BF16), w_ref[...])
    tm = p.shape[0]
    dq = N_HEADS * HEAD_DIM
    lane = lax.broadcasted_iota(I32, (tm, HEAD_DIM), 1)
    k_tail = jnp.where(lane < 2, 1.0, 0.0)
    v_tail = jnp.where(lane < 1, 1.0, 0.0)
    for hh in range(N_HEADS):
        qh = p[:, hh * HEAD_DIM:(hh + 1) * HEAD_DIM] * (HEAD_DIM ** -0.5 * LOG2E)
        q_tail = jnp.broadcast_to(qt_ref[hh:hh + 1, :], (tm, HEAD_DIM))
        q_t = jnp.concatenate([qh, q_tail], axis=1).T.astype(BF16)
        for blk in range(tm // QB):
            col = (blk * GQA + hh % GQA) * QB
            q_ref[0, hh // GQA, :, col:col + QB] = q_t[:, blk * QB:(blk + 1) * QB]
    for j in range(6):
        for g in range(N_KV):
            col = dq + (j * N_KV + g) * HEAD_DIM
            piece = p[:, col:col + HEAD_DIM]
            if j < 2:
                kvc_ref[0, j * N_KV + g] = piece
            elif j % 2 == 0:
                kx_ref[0, (j // 2 - 1) * N_KV + g] = jnp.concatenate([piece, k_tail], axis=1).astype(BF16)
            else:
                v_t = jnp.concatenate([piece, v_tail], axis=1).T.astype(BF16)
                for blk in range(tm // QB):
                    vx_ref[0, (j // 2 - 1) * N_KV + g, blk] = v_t[:, blk * QB:(blk + 1) * QB]
    c0 = dq + 6 * N_KV * HEAD_DIM
    for g in range(N_KV):
        g_ref[0, g] = _sigmoid(p[:, c0 + g * 128:c0 + (g + 1) * 128]).T[0:16, :]
    xr_ref[0] = p[:, c0 + 256:c0 + 768]
    gr_ref[0] = p[:, c0 + 768:c0 + 1280]


def _inproj(x, mod, nw, w_pad, q_tail, tm=256):
    B, S, D = x.shape
    ncol = w_pad.shape[1]
    heads = lambda n, w: pl.BlockSpec((1, n, tm, w), lambda b, t: (b, 0, t, 0))
    return pl.pallas_call(
        _inproj_kernel,
        grid=(B, S // tm),
        in_specs=[pl.BlockSpec((1, tm, D), lambda b, t: (b, t, 0)),
                  pl.BlockSpec((1, 6, D), lambda b, t: (b, 0, 0)),
                  pl.BlockSpec((1, D), lambda b, t: (0, 0)),
                  pl.BlockSpec((D, ncol), lambda b, t: (0, 0)),
                  pl.BlockSpec(q_tail.shape, lambda b, t: (0, 0))],
        out_specs=[pl.BlockSpec((1, N_KV, 2 * HEAD_DIM, tm * GQA), lambda b, t: (b, 0, 0, t)),
                   heads(4, 2 * HEAD_DIM),
                   pl.BlockSpec((1, 4, tm // QB, 2 * HEAD_DIM, QB), lambda b, t: (b, 0, t, 0, 0)),
                   heads(4, HEAD_DIM),
                   pl.BlockSpec((1, N_KV, 16, tm), lambda b, t: (b, 0, 0, t)),
                   pl.BlockSpec((1, tm, 512), lambda b, t: (b, t, 0)),
                   pl.BlockSpec((1, tm, 512), lambda b, t: (b, t, 0))],
        out_shape=[jax.ShapeDtypeStruct((B, N_KV, 2 * HEAD_DIM, S * GQA), BF16),
                   jax.ShapeDtypeStruct((B, 4, S, 2 * HEAD_DIM), BF16),
                   jax.ShapeDtypeStruct((B, 4, S // QB, 2 * HEAD_DIM, QB), BF16),
                   jax.ShapeDtypeStruct((B, 4, S, HEAD_DIM), F32),
                   jax.ShapeDtypeStruct((B, N_KV, 16, S), F32),
                   jax.ShapeDtypeStruct((B, S, 512), F32),
                   jax.ShapeDtypeStruct((B, S, 512), F32)],
        compiler_params=_params("parallel", "parallel"),
        name="inproj",
    )(x, mod, nw, w_pad, q_tail)


def _compress_kernel(x_ref, pos_ref, w1_ref, w2_ref, o_ref, ot_ref):
    x = x_ref[0, 0]
    nc = x.shape[0]
    half = CMP_STRIDE * HEAD_DIM
    kv = pl.program_id(1) // N_KV
    a = _dot((x + pos_ref[0:1, :]).astype(BF16), w1_ref[kv, 0:half, :].astype(BF16))
    b = _dot((x + pos_ref[1:2, :]).astype(BF16), w1_ref[kv, half:2 * half, :].astype(BF16))
    hid = _gelu(a + pltpu.roll(b, nc - 1, 0))
    out = _dot(hid.astype(BF16), w2_ref[kv].astype(BF16))
    row = lax.broadcasted_iota(I32, out.shape, 0)
    out = jnp.where(row < nc - 1, out, 0.0)
    lane = lax.broadcasted_iota(I32, (nc, HEAD_DIM), 1)
    out = jnp.concatenate([out, jnp.where(lane < 2, 1.0, 0.0)], axis=1)
    o_ref[0, 0, 0:CMP_PAD, :] = jnp.zeros((CMP_PAD, 2 * HEAD_DIM), F32)
    o_ref[0, 0, CMP_PAD:CMP_PAD + nc, :] = out
    o_ref[0, 0, CMP_PAD + nc:, :] = jnp.zeros((CMP_WIN - CMP_PAD, 2 * HEAD_DIM), F32)
    ot_ref[0, 0] = out.T


def _compress(kvc, pos2, w1, w2):
    B, _, NC, W = kvc.shape
    return pl.pallas_call(
        _compress_kernel,
        grid=(B, 4),
        in_specs=[pl.BlockSpec((1, 1, NC, W), lambda b, i: (b, i, 0, 0)),
                  pl.BlockSpec((2, W), lambda b, i: (0, 0)),
                  pl.BlockSpec((2, 2 * W, HEAD_DIM), lambda b, i: (0, 0, 0)),
                  pl.BlockSpec((2, HEAD_DIM, HEAD_DIM), lambda b, i: (0, 0, 0))],
        out_specs=[pl.BlockSpec((1, 1, NC + CMP_WIN, 2 * HEAD_DIM), lambda b, i: (b, i, 0, 0)),
                   pl.BlockSpec((1, 1, 2 * HEAD_DIM, NC), lambda b, i: (b, i, 0, 0))],
        out_shape=[jax.ShapeDtypeStruct((B, 4, NC + CMP_WIN, 2 * HEAD_DIM), F32),
                   jax.ShapeDtypeStruct((B, 4, 2 * HEAD_DIM, NC), F32)],
        compiler_params=_params("parallel", "parallel"),
        name="compress",
    )(kvc, pos2, w1, w2)


def _attn_kernel(q_ref, ks_ref, vs_ref, kw_ref, vw_ref, kc_ref, vc_ref, vct_ref, g_ref, t_ref, w_ref, ovt_ref, ov_ref,
                 o_ref, m_scr, acc_scr, sel_scr, sa_scr, sb_scr):
    qb = pl.program_id(2)
    R = GQA * QB
    q = q_ref[0, 0]
    nc = vct_ref.shape[3]
    nb = ov_ref.shape[1]
    n_top = min(SLC_TOP, ks_ref.shape[2] // SLC_LEN)

    def heads_sum(x):
        out = x[:, 0:QB]
        for hq in range(1, GQA):
            out = out + x[:, hq * QB:(hq + 1) * QB]
        return out

    w0 = pl.multiple_of(qb * 8, 8)
    s_far = _dot(kc_ref[0, 0, CMP_PAD:CMP_PAD + nc, :].astype(BF16), q)
    n_io = lax.broadcasted_iota(I32, (nc, R), 0)
    s_far = jnp.where(n_io < qb * 8 - CMP_PAD, s_far, -jnp.inf)
    s_win = _dot(kc_ref[0, 0, pl.ds(w0, CMP_WIN), :].astype(BF16), q) + w_ref[0]
    j_w = lax.broadcasted_iota(I32, (CMP_WIN, R), 0)
    i_w = lax.broadcasted_iota(I32, (CMP_WIN, R), 1) & (QB - 1)
    dist_w = i_w - CMP_STRIDE * (j_w - CMP_PAD) - (CMP_LEN - 1)
    s_win = jnp.where((dist_w >= 0) & (j_w >= CMP_PAD - qb * 8), s_win, -jnp.inf)
    m = jnp.maximum(jnp.max(s_far, axis=0, keepdims=True), jnp.max(s_win, axis=0, keepdims=True))
    m = jnp.where(m == -jnp.inf, 0.0, m)
    e_far = jnp.exp2(s_far - m)
    e_win = jnp.exp2(s_win - m)
    l = jnp.sum(e_far, axis=0, keepdims=True) + jnp.sum(e_win, axis=0, keepdims=True)
    inv = 1.0 / jnp.maximum(l, 1e-30)
    p_far = e_far * inv
    p_win = e_win * inv
    vcw_t = vc_ref[0, 0, pl.ds(w0, CMP_WIN), :].T
    o_c = _dot(vct_ref[0, 0].astype(BF16), p_far.astype(BF16)) + _dot(vcw_t.astype(BF16), p_win.astype(BF16))
    ov_win_t = ov_ref[pl.ds(w0, CMP_WIN), :].T
    sc_t = (_dot(ovt_ref[...], heads_sum(p_far).astype(BF16))
            + _dot(ov_win_t.astype(BF16), heads_sum(p_win).astype(BF16)))

    blk = lax.broadcasted_iota(I32, (nb, QB), 0)
    t_io = lax.broadcasted_iota(I32, (nb, QB), 1)
    cur = qb * (QB // SLC_LEN) + t_io // SLC_LEN
    forced = (blk == 0) | (blk == cur) | (blk == cur - 1)
    val = jnp.where(forced, jnp.inf, jnp.where(blk <= cur, sc_t, -jnp.inf))
    sel_t = jnp.zeros((nb, QB), F32)
    for _ in range(n_top):
        mx = jnp.max(val, axis=0, keepdims=True)
        hit = (val == mx) & (val > -jnp.inf)
        first = jnp.min(jnp.where(hit, blk, nb), axis=0, keepdims=True)
        pick = blk == first
        sel_t = jnp.where(pick, 1.0, sel_t)
        val = jnp.where(pick, -jnp.inf, val)
    sel_scr[...] = jnp.where(sel_t > 0.5, 0.0, NEG)

    m_scr[...] = jnp.full(m_scr.shape, NEG, F32)
    acc_scr[...] = jnp.zeros(acc_scr.shape, F32)
    blocks_per_chunk = FAR_CHUNK * QB // SLC_LEN

    chunk = FAR_CHUNK * QB

    def qk(c):
        return _dot(ks_ref[0, 0, pl.ds(pl.multiple_of(c * chunk, chunk), chunk), :], q)

    def soft(c, s):
        rows = sel_scr[pl.ds(pl.multiple_of(c * blocks_per_chunk, blocks_per_chunk), blocks_per_chunk), :]
        drop = jnp.concatenate([jnp.broadcast_to(rows[r:r + 1, :], (SLC_LEN, QB)) for r in range(blocks_per_chunk)],
                               axis=0)
        s = s + jnp.concatenate([drop] * GQA, axis=1)
        m_old = m_scr[...]
        m_new = jnp.maximum(m_old, jnp.max(s, axis=0, keepdims=True))
        p = jnp.exp2(s - m_new)
        v_t = jnp.concatenate([vs_ref[0, 0, c * FAR_CHUNK + j] for j in range(FAR_CHUNK)], axis=1)
        acc_scr[...] = jnp.exp2(m_old - m_new) * acc_scr[...] + _dot(v_t, p.astype(BF16))
        m_scr[...] = m_new

    def near_bias(c):
        tiles = []
        for j in range(FAR_CHUNK):
            delta = qb - (c * FAR_CHUNK + j)
            tiles.append(t_ref[0, jnp.where(delta < 0, TILE_MASKED, delta)])
        return jnp.concatenate(tiles, axis=0)

    def chunk_pairs(first, count, bias):
        last = first + jnp.maximum(count - 1, 0)
        sa_scr[...] = qk(first)

        def pair_body(j, carry):
            c = first + 2 * j
            sb_scr[...] = qk(c + 1)
            soft(c, sa_scr[...] + bias(c) if bias else sa_scr[...])
            sa_scr[...] = qk(jnp.minimum(c + 2, last))
            soft(c + 1, sb_scr[...] + bias(c + 1) if bias else sb_scr[...])
            return carry

        lax.fori_loop(0, count // 2, pair_body, 0)

        @pl.when(count % 2 == 1)
        def _():
            soft(last, sa_scr[...] + bias(last) if bias else sa_scr[...])

    n_far = jnp.maximum(qb - (FAR_TILE_DIST - 1), 0) // FAR_CHUNK
    chunk_pairs(0, n_far, None)
    chunk_pairs(n_far, qb // FAR_CHUNK + 1 - n_far, near_bias)
    acc = acc_scr[...]
    o_s = acc[0:HEAD_DIM, :] / acc[HEAD_DIM:HEAD_DIM + 1, :]

    n_wt = WINDOW // QB + 1
    k_t, v_t, b_t = [], [], []
    for j in range(n_wt):
        kb = qb - (n_wt - 1) + j
        kb0 = jnp.maximum(kb, 0)
        k_t.append(kw_ref[0, 0, pl.ds(pl.multiple_of(kb0 * QB, QB), QB), :])
        v_t.append(vw_ref[0, 0, kb0])
        b_t.append(t_ref[0, jnp.where(kb < 0, TILE_MASKED, TILE_WINDOW_EDGE if j == 0 else n_wt - 1 - j)])
    s = _dot(jnp.concatenate(k_t, axis=0), q) + jnp.concatenate(b_t, axis=0)
    p = jnp.exp2(s - jnp.max(s, axis=0, keepdims=True))
    acc = _dot(jnp.concatenate(v_t, axis=1), p.astype(BF16))
    o_w = acc[0:HEAD_DIM, :] / acc[HEAD_DIM:HEAD_DIM + 1, :]

    gates = g_ref[0, 0]
    outs = []
    for hq in range(GQA):
        cols = slice(hq * QB, (hq + 1) * QB)
        c = hq * N_BRANCH
        outs.append(gates[c:c + 1, :] * o_c[0:HEAD_DIM, cols] + gates[c + 1:c + 2, :] * o_s[:, cols]
                    + gates[c + 2:c + 3, :] * o_w[:, cols])
    o_ref[0] = jnp.concatenate(outs, axis=0).T


def _attn(q_t, kx, vx_t, kvc, kvc_t, gates_t, t_tab, w_tab, ov_t, ov):
    B, _, W, _ = q_t.shape
    S = kx.shape[2]
    ncp = kvc.shape[2]
    nb = ov.shape[1]
    keys = lambda j: pl.BlockSpec((1, 1, S, W), lambda b, g, t: (b, j * N_KV + g, 0, 0))
    vals = lambda j: pl.BlockSpec((1, 1, S // QB, W, QB), lambda b, g, t: (b, j * N_KV + g, 0, 0, 0))
    cmp = lambda j: pl.BlockSpec((1, 1, ncp, W), lambda b, g, t: (b, j * N_KV + g, 0, 0))
    return pl.pallas_call(
        _attn_kernel,
        grid=(B, N_KV, S // QB),
        in_specs=[pl.BlockSpec((1, 1, W, GQA * QB), lambda b, g, t: (b, g, 0, t)),
                  keys(0), vals(0), keys(1), vals(1), cmp(0), cmp(1),
                  pl.BlockSpec((1, 1, W, ncp - CMP_WIN), lambda b, g, t: (b, N_KV + g, 0, 0)),
                  pl.BlockSpec((1, 1, 16, QB), lambda b, g, t: (b, g, 0, t)),
                  pl.BlockSpec((1,) + t_tab.shape[1:], lambda b, g, t: (g, 0, 0, 0)),
                  pl.BlockSpec((1, CMP_WIN, GQA * QB), lambda b, g, t: (g, 0, 0)),
                  pl.BlockSpec(ov_t.shape, lambda b, g, t: (0, 0)),
                  pl.BlockSpec(ov.shape, lambda b, g, t: (0, 0))],
        out_specs=pl.BlockSpec((1, QB, GQA * HEAD_DIM), lambda b, g, t: (b, t, g)),
        out_shape=jax.ShapeDtypeStruct((B, S, N_HEADS * HEAD_DIM), F32),
        scratch_shapes=[pltpu.VMEM((1, GQA * QB), F32), pltpu.VMEM((W, GQA * QB), F32), pltpu.VMEM((nb, QB), F32),
                        pltpu.VMEM((FAR_CHUNK * QB, GQA * QB), F32), pltpu.VMEM((FAR_CHUNK * QB, GQA * QB), F32)],
        compiler_params=_params("parallel", "parallel", "arbitrary"),
        name="attn",
    )(q_t, kx, vx_t, kx, vx_t, kvc, kvc, kvc_t, gates_t, t_tab, w_tab, ov_t, ov)


def _rglru_kernel(xr_ref, gr_ref, cw_ref, cb_ref, wa_ref, ba_ref, wx_ref, bx_ref, lam_ref, gw_ref, o_ref,
                  xbuf, hprev, a_scr, u_scr, h_scr):
    ts = xr_ref.shape[1]
    C = xr_ref.shape[2]

    @pl.when(pl.program_id(1) == 0)
    def _():
        xbuf[0:8, :] = jnp.zeros((8, C), F32)
        hprev[...] = jnp.zeros(hprev.shape, F32)

    xbuf[8:8 + ts, :] = xr_ref[0]
    xc = cb_ref[...] + jnp.zeros((ts, C), F32)
    for j in range(CONV_WIDTH):
        xc = xc + cw_ref[j:j + 1, :] * xbuf[pl.ds(8 - (CONV_WIDTH - 1) + j, ts), :]
    xbuf[0:8, :] = xbuf[ts:ts + 8, :]

    xcb = xc.astype(BF16)
    r = _sigmoid(_dot(xcb, wa_ref[...]) + ba_ref[...])
    i = _sigmoid(_dot(xcb, wx_ref[...]) + bx_ref[...])
    z = -lam_ref[...]
    softplus = jnp.maximum(z, 0.0) + jnp.log(1.0 + jnp.exp(-jnp.abs(z)))
    log_a = -LRU_C * r * softplus
    a_scr[...] = jnp.exp(log_a)
    u_scr[...] = jnp.sqrt(1.0 - jnp.exp(2.0 * log_a)) * (i * xc)

    row = lax.broadcasted_iota(I32, (8, C), 0)

    def body(k, h):
        r0 = pl.multiple_of(k * 8, 8)
        a = a_scr[pl.ds(r0, 8), :]
        b = u_scr[pl.ds(r0, 8), :]
        for s in (1, 2, 4):
            keep = row >= s
            b = jnp.where(keep, a * pltpu.roll(b, s, 0) + b, b)
            a = jnp.where(keep, a * pltpu.roll(a, s, 0), a)
        hh = a * h + b
        h_scr[pl.ds(r0, 8), :] = hh
        return jnp.broadcast_to(hh[7:8, :], (8, C))

    hprev[...] = lax.fori_loop(0, ts // 8, body, hprev[...])
    out = h_scr[...] * _gelu(gr_ref[0])
    o_ref[0] = (_rms(out) * gw_ref[...]).astype(BF16)


def _rglru(xr, gr, cw, cb, wa_bd, ba, wx_bd, bx, lam, gw, ts=512):
    B, S, C = xr.shape
    ts = min(ts, S)
    vec = pl.BlockSpec((1, C), lambda b, t: (0, 0))
    mat = pl.BlockSpec((C, C), lambda b, t: (0, 0))
    seq = pl.BlockSpec((1, ts, C), lambda b, t: (b, t, 0))
    return pl.pallas_call(
        _rglru_kernel,
        grid=(B, S // ts),
        in_specs=[seq, seq, pl.BlockSpec((CONV_WIDTH, C), lambda b, t: (0, 0)), vec, mat, vec, mat, vec, vec, vec],
        out_specs=seq,
        out_shape=jax.ShapeDtypeStruct((B, S, C), BF16),
        scratch_shapes=[pltpu.VMEM((ts + 8, C), F32), pltpu.VMEM((8, C), F32), pltpu.VMEM((ts, C), F32),
                        pltpu.VMEM((ts, C), F32), pltpu.VMEM((ts, C), F32)],
        compiler_params=_params("parallel", "arbitrary"),
        name="rglru",
    )(xr, gr, cw, cb, wa_bd, ba, wx_bd, bx, lam, gw)


def _to_row_tiles(ref, x):
    rows = x.shape[0]
    for s in range(ROW_TILE):
        ref[pl.ds(s, rows, stride=ROW_TILE), :] = x[:, s * 128:(s + 1) * 128]


def _from_row_tiles(ref, rows, start=0):
    return jnp.concatenate([ref[pl.ds(start + s, rows, stride=ROW_TILE), :] for s in range(ROW_TILE)], axis=1)


def _outproj_kernel(oa_ref, yr_ref, x_ref, mod_ref, gaw_ref, wo_ref, n2_ref, wr_ref, x1_ref, h2_ref, st_ref):
    da = oa_ref.shape[2]
    ya = (_rms(oa_ref[0]) * gaw_ref[...]).astype(BF16)
    mix = _dot(ya, wo_ref[0:da, :]) + _dot(yr_ref[0], wo_ref[da:, :])
    x1 = x_ref[0] + mod_ref[0, 2:3, :] * mix
    x1_ref[0] = x1
    h2 = (_rms(x1) * n2_ref[...]) * (1.0 + mod_ref[0, 4:5, :]) + mod_ref[0, 3:4, :]
    h2_ref[0] = h2
    st_ref[...] = _sigmoid(_dot_nt(wr_ref[...], h2, precision=lax.Precision.HIGHEST))


def _outproj(oa, yr, x, mod, gaw, wo, n2, wr_t, tm=256):
    B, S, D = x.shape
    E = wr_t.shape[0]
    nt = S // tm
    row = lambda w: pl.BlockSpec((1, tm, w), lambda b, t: (b, t, 0))
    return pl.pallas_call(
        _outproj_kernel,
        grid=(B, nt),
        in_specs=[row(oa.shape[2]), row(yr.shape[2]), row(D),
                  pl.BlockSpec((1, 6, D), lambda b, t: (b, 0, 0)),
                  pl.BlockSpec((1, oa.shape[2]), lambda b, t: (0, 0)),
                  pl.BlockSpec(wo.shape, lambda b, t: (0, 0)),
                  pl.BlockSpec((1, D), lambda b, t: (0, 0)),
                  pl.BlockSpec((E, D), lambda b, t: (0, 0))],
        out_specs=[row(D), row(D), pl.BlockSpec((E, tm), lambda b, t: (0, b * nt + t))],
        out_shape=[jax.ShapeDtypeStruct((B, S, D), F32), jax.ShapeDtypeStruct((B, S, D), F32),
                   jax.ShapeDtypeStruct((E, B * S), F32)],
        compiler_params=_params("parallel", "parallel"),
        name="outproj",
    )(oa, yr, x, mod, gaw, wo, n2, wr_t)


def _route_kernel(s_ref, b_ref, eid_ref, wgt_ref, rank_ref, cnt_ref, tcnt_ref, carry):
    E, tn = s_ref.shape
    per = E // N_EXPERT_GROUPS

    @pl.when(pl.program_id(0) == 0)
    def _():
        carry[...] = jnp.zeros(carry.shape, F32)

    s = s_ref[...]
    s_sel = s + b_ref[...]
    eidx = lax.broadcasted_iota(I32, (E, tn), 0)

    grp = []
    for gi in range(N_EXPERT_GROUPS):
        xg = s_sel[gi * per:(gi + 1) * per, :]
        ig = lax.broadcasted_iota(I32, (per, tn), 0)
        m1 = jnp.max(xg, axis=0, keepdims=True)
        f1 = jnp.min(jnp.where(xg == m1, ig, per), axis=0, keepdims=True)
        m2 = jnp.max(jnp.where(ig == f1, -jnp.inf, xg), axis=0, keepdims=True)
        grp.append(m1 + m2)
    val = []
    for gi in range(N_EXPERT_GROUPS):
        rank = jnp.zeros((1, tn), I32)
        for gj in range(N_EXPERT_GROUPS):
            if gj == gi:
                continue
            ahead = (grp[gj] > grp[gi]) | ((grp[gj] == grp[gi]) & (gj < gi))
            rank = rank + ahead.astype(I32)
        val.append(jnp.where(rank < TOP_GROUPS, s_sel[gi * per:(gi + 1) * per, :], -jnp.inf))
    val = jnp.concatenate(val, axis=0)

    eids, tops = [], []
    hot = jnp.zeros((E, tn), F32)
    for _ in range(TOP_K):
        mx = jnp.max(val, axis=0, keepdims=True)
        first = jnp.min(jnp.where(val == mx, eidx, E), axis=0, keepdims=True)
        pick = eidx == first
        eids.append(first)
        tops.append(jnp.sum(jnp.where(pick, s, 0.0), axis=0, keepdims=True))
        hot = jnp.where(pick, 1.0, hot)
        val = jnp.where(pick, -jnp.inf, val)
    denom = tops[0]
    for t in tops[1:]:
        denom = denom + t

    ti = lax.broadcasted_iota(I32, (tn, tn), 0)
    tj = lax.broadcasted_iota(I32, (tn, tn), 1)
    upper = jnp.where(ti < tj, 1.0, 0.0).astype(BF16)
    before = _dot(hot.astype(BF16), upper) + carry[...]
    for k in range(TOP_K):
        eid_ref[k:k + 1, :] = eids[k]
        wgt_ref[k:k + 1, :] = ROUTED_SCALE * tops[k] / denom
        rank_ref[k:k + 1, :] = jnp.sum(jnp.where(eidx == eids[k], before, 0.0), axis=0, keepdims=True).astype(I32)
    tile_count = jnp.sum(hot, axis=1, keepdims=True)
    tcnt_ref[0] = tile_count
    carry[...] = carry[...] + tile_count
    cnt_ref[...] = carry[...]


def _route(s_t, b_col, tn=256):
    E, N = s_t.shape
    tn = min(tn, N)
    out = pl.BlockSpec((TOP_K, tn), lambda t: (0, t))
    return pl.pallas_call(
        _route_kernel,
        grid=(N // tn,),
        in_specs=[pl.BlockSpec((E, tn), lambda t: (0, t)), pl.BlockSpec((E, 1), lambda t: (0, 0))],
        out_specs=[out, out, out, pl.BlockSpec((E, 1), lambda t: (0, 0)), pl.BlockSpec((1, E, 1), lambda t: (t, 0, 0))],
        out_shape=[jax.ShapeDtypeStruct((TOP_K, N), I32), jax.ShapeDtypeStruct((TOP_K, N), F32),
                   jax.ShapeDtypeStruct((TOP_K, N), I32), jax.ShapeDtypeStruct((E, 1), F32),
                   jax.ShapeDtypeStruct((N // tn, E, 1), F32)],
        scratch_shapes=[pltpu.VMEM((E, 1), F32)],
        compiler_params=_params("arbitrary"),
        name="route",
    )(s_t, b_col)


def _tilepos_kernel(eid_ref, rank_ref, base_ref, o_ref):
    E = base_ref.shape[1]
    tn = eid_ref.shape[1]
    eidx = lax.broadcasted_iota(I32, (E, tn), 0)
    for k in range(TOP_K):
        base = jnp.sum(jnp.where(eidx == eid_ref[k:k + 1, :], base_ref[0], 0.0), axis=0, keepdims=True)
        o_ref[k:k + 1, :] = base.astype(I32) + rank_ref[k:k + 1, :]


def _tilepos(eid_t, rank_t, base, tn):
    K, N = eid_t.shape
    blk = pl.BlockSpec((K, tn), lambda t: (0, t))
    return pl.pallas_call(
        _tilepos_kernel,
        grid=(N // tn,),
        in_specs=[blk, blk, pl.BlockSpec((1,) + base.shape[1:], lambda t: (t, 0, 0))],
        out_specs=blk,
        out_shape=jax.ShapeDtypeStruct((K, N), I32),
        compiler_params=_params("parallel"),
        name="tilepos",
    )(eid_t, rank_t, base)


def _run_copies(src_ref, cnt_ref, dst_ref, copy):
    def go(e, carry):
        n = cnt_ref[0, 0, e]

        @pl.when(n > 0)
        def _():
            size = pl.multiple_of(n * ROW_TILE, ROW_TILE)
            copy(pl.ds(pl.multiple_of(src_ref[0, 0, e] * ROW_TILE, ROW_TILE), size),
                 pl.ds(pl.multiple_of(dst_ref[0, 0, e] * ROW_TILE, ROW_TILE), size)).start()
        return carry
    lax.fori_loop(0, cnt_ref.shape[2], go, 0)


def _dispatch_kernel(src_ref, cnt_ref, dst_ref, pos_ref, h_ref, xs_hbm, xbuf, sem):
    tn = h_ref.shape[0]
    hb = h_ref[...].astype(BF16)
    slab = 512
    for sl in range(TOP_K * tn // slab):
        row = lax.broadcasted_iota(I32, (slab, tn), 0) + sl * slab
        onehot = jnp.zeros((slab, tn), F32)
        for k in range(TOP_K):
            onehot = jnp.where(row == pos_ref[k:k + 1, :], 1.0, onehot)
        xp = _dot(onehot.astype(BF16), hb)
        for s in range(ROW_TILE):
            xbuf[pl.ds(sl * slab * ROW_TILE + s, slab, stride=ROW_TILE), :] = xp[:, s * 128:(s + 1) * 128]
    _run_copies(src_ref, cnt_ref, dst_ref,
                lambda src, dst: pltpu.make_async_copy(xbuf.at[src], xs_hbm.at[dst], sem.at[0]))
    pltpu.make_async_copy(xbuf, xs_hbm.at[pl.ds(0, xbuf.shape[0])], sem.at[0]).wait()


def _run_specs(runs):
    return [pl.BlockSpec((1, 1, r.shape[2]), lambda t: (t, 0, 0), memory_space=pltpu.SMEM) for r in runs]


def _dispatch(runs, pos_t, h2, tn):
    N, D = h2.shape
    W = D // ROW_TILE
    return pl.pallas_call(
        _dispatch_kernel,
        grid=(N // tn,),
        in_specs=_run_specs(runs) + [pl.BlockSpec((TOP_K, tn), lambda t: (0, t)),
                                     pl.BlockSpec((tn, D), lambda t: (t, 0))],
        out_specs=pl.BlockSpec(memory_space=pl.ANY),
        out_shape=jax.ShapeDtypeStruct((N * TOP_K * ROW_TILE, W), F32),
        scratch_shapes=[pltpu.VMEM((TOP_K * tn * ROW_TILE, W), F32), pltpu.SemaphoreType.DMA((1,))],
        compiler_params=_params("arbitrary"),
        name="dispatch",
    )(*runs, pos_t, h2)


def _experts_kernel(first_ref, size_ref, e_ref, new_ref, next_ref, wslot_ref, total_ref,
                    w1_hbm, w3_hbm, w2_hbm, xs_hbm, ys_hbm,
                    xbuf, ybuf, w1f, w3f, w2f, w1b, w3b, w2b, xsem, ysem, wsem):
    total = total_ref[0]

    def w_copies(e, slot):
        return [pltpu.make_async_copy(src.at[e], dst.at[slot], wsem.at[slot])
                for src, dst in ((w1_hbm, w1f), (w3_hbm, w3f), (w2_hbm, w2f))]

    def rows(c):
        return (pl.multiple_of(first_ref[c] * ROW_TILE, ROW_TILE), pl.multiple_of(size_ref[c] * ROW_TILE, ROW_TILE))

    def x_copy(c, slot):
        first, size = rows(c)
        return pltpu.make_async_copy(xs_hbm.at[pl.ds(first, size)], xbuf.at[slot, pl.ds(0, size)], xsem.at[slot])

    def y_copy(c, slot):
        first, size = rows(c)
        return pltpu.make_async_copy(ybuf.at[slot, pl.ds(0, size)], ys_hbm.at[pl.ds(first, size)], ysem.at[slot])

    xbuf[...] = jnp.zeros(xbuf.shape, F32)
    for cp in w_copies(e_ref[0], 0):
        cp.start()
    x_copy(0, 0).start()

    def body(c, carry):
        slot = c % 2

        @pl.when(new_ref[c] == 1)
        def _():
            ws = wslot_ref[c]
            for cp in w_copies(e_ref[c], ws):
                cp.wait()
            w1b[...] = w1f[ws].astype(BF16)
            w3b[...] = w3f[ws].astype(BF16)
            w2b[...] = w2f[ws].astype(BF16)

            @pl.when(next_ref[c] >= 0)
            def _():
                for cp in w_copies(next_ref[c], 1 - ws):
                    cp.start()

        x_copy(c, slot).wait()

        @pl.when(c + 1 < total)
        def _():
            x_copy(c + 1, 1 - slot).start()

        x = _from_row_tiles(xbuf.at[slot], MOE_BLOCK).astype(BF16)
        h1 = _dot(x, w1b[...])
        h3 = _dot(x, w3b[...])
        hid = (h1 * _sigmoid(h1)) * h3
        y = _dot(hid.astype(BF16), w2b[...])

        @pl.when(c >= 2)
        def _():
            y_copy(c - 2, slot).wait()

        _to_row_tiles(ybuf.at[slot], y)
        y_copy(c, slot).start()
        return carry

    lax.fori_loop(0, total, body, 0)

    @pl.when(total >= 2)
    def _():
        y_copy(total - 2, total % 2).wait()

    y_copy(total - 1, (total - 1) % 2).wait()


def _expert_chunks(counts, n_rows):
    E = counts.shape[0]
    n_max = n_rows // MOE_BLOCK + E
    starts = jnp.cumsum(counts) - counts
    n_chunks = (counts + MOE_BLOCK - 1) // MOE_BLOCK
    c_end = jnp.cumsum(n_chunks)
    c = jnp.arange(n_max, dtype=I32)
    e = jnp.minimum(jnp.sum(c_end[None, :] <= c[:, None], axis=1), E - 1).astype(I32)
    j = c - (c_end - n_chunks)[e]
    size = jnp.clip(counts[e] - j * MOE_BLOCK, 0, MOE_BLOCK)
    has_rows = counts > 0
    ids = jnp.where(has_rows, jnp.arange(E, dtype=I32), E)
    later = jnp.flip(lax.cummin(jnp.flip(ids)))
    nxt = jnp.concatenate([later[1:], jnp.full((1,), E, I32)])
    nxt = jnp.where(nxt >= E, -1, nxt)
    ordinal = jnp.cumsum(has_rows.astype(I32)) - 1
    to_i32 = lambda a: a.astype(I32)
    return tuple(map(to_i32, (starts[e] + j * MOE_BLOCK, size, e, j == 0, nxt[e], ordinal[e] % 2, c_end[-1:])))


def _experts(chunks, xs, w1, w3, w2):
    rows_total, W = xs.shape
    _, D, De = w1.shape
    any_space = pl.BlockSpec(memory_space=pl.ANY)
    grid_spec = pltpu.PrefetchScalarGridSpec(
        num_scalar_prefetch=len(chunks),
        grid=(1,),
        in_specs=[any_space] * 4,
        out_specs=any_space,
        scratch_shapes=[pltpu.VMEM((2, MOE_BLOCK * ROW_TILE, W), F32), pltpu.VMEM((2, MOE_BLOCK * ROW_TILE, W), F32),
                        pltpu.VMEM((2, D, De), F32), pltpu.VMEM((2, D, De), F32), pltpu.VMEM((2, De, D), F32),
                        pltpu.VMEM((D, De), BF16), pltpu.VMEM((D, De), BF16), pltpu.VMEM((De, D), BF16),
                        pltpu.SemaphoreType.DMA((2,)), pltpu.SemaphoreType.DMA((2,)), pltpu.SemaphoreType.DMA((2,))],
    )
    return pl.pallas_call(
        _experts_kernel,
        grid_spec=grid_spec,
        out_shape=jax.ShapeDtypeStruct((rows_total, W), F32),
        compiler_params=_params("arbitrary"),
        name="experts",
    )(*chunks, w1, w3, w2, xs)


def _combine_kernel(src_ref, cnt_ref, dst_ref, ys_hbm, pos_ref, post_ref, wt_ref, h2_ref, x1_ref, mod_ref, ws1_ref,
                    ws3_ref, ws2_ref, fw_ref, o_ref, ybuf, sem):
    tm = h2_ref.shape[0]
    n_rows = TOP_K * tm

    _run_copies(src_ref, cnt_ref, dst_ref,
                lambda src, dst: pltpu.make_async_copy(ys_hbm.at[dst], ybuf.at[src], sem.at[0]))

    h = h2_ref[...].astype(BF16)
    a = _dot(h, ws1_ref[...])
    hid = (a * _sigmoid(a)) * _dot(h, ws3_ref[...])
    shared = _dot(hid.astype(BF16), ws2_ref[...])

    row = lax.broadcasted_iota(I32, (n_rows, tm), 0)
    w_rows = jnp.zeros((n_rows, tm), F32)
    for k in range(TOP_K):
        w_rows = jnp.where(row == post_ref[k:k + 1, :], wt_ref[k:k + 1, :], w_rows)
    w_col = jnp.sum(w_rows, axis=1, keepdims=True)
    lane = lax.broadcasted_iota(I32, (tm, n_rows), 1)
    mine = jnp.zeros((tm, n_rows), F32)
    for k in range(TOP_K):
        mine = jnp.where(lane == pos_ref[:, k:k + 1], 1.0, mine)
    pltpu.make_async_copy(ys_hbm.at[pl.ds(0, ybuf.shape[0])], ybuf, sem.at[0]).wait()
    y = (_from_row_tiles(ybuf, n_rows) * w_col).astype(BF16)
    routed = _dot(mine.astype(BF16), y)
    x2 = x1_ref[...] + mod_ref[0, 5:6, :] * (routed + shared)
    o_ref[...] = _rms(x2) * fw_ref[...]


def _combine(runs, ys, pos, pos_t, wgt_t, h2, x1, mod, ws1, ws3, ws2, fw, seq, tm):
    N, D = h2.shape
    nt = seq // tm
    row = pl.BlockSpec((tm, D), lambda t: (t, 0))
    full = lambda a: pl.BlockSpec(a.shape, lambda t: (0,) * a.ndim)
    return pl.pallas_call(
        _combine_kernel,
        grid=(N // tm,),
        in_specs=_run_specs(runs) + [pl.BlockSpec(memory_space=pl.ANY),
                                     pl.BlockSpec((tm, TOP_K), lambda t: (t, 0)),
                                     pl.BlockSpec((TOP_K, tm), lambda t: (0, t)),
                                     pl.BlockSpec((TOP_K, tm), lambda t: (0, t)),
                                     row, row,
                                     pl.BlockSpec((1, 6, D), lambda t: (t // nt, 0, 0)),
                                     full(ws1), full(ws3), full(ws2), full(fw)],
        out_specs=row,
        out_shape=jax.ShapeDtypeStruct((N, D), F32),
        scratch_shapes=[pltpu.VMEM((TOP_K * tm * ROW_TILE, D // ROW_TILE), F32), pltpu.SemaphoreType.DMA((1,))],
        compiler_params=_params("arbitrary"),
        name="combine",
    )(*runs, ys, pos, pos_t, wgt_t, h2, x1, mod, ws1, ws3, ws2, fw)


def _t5_bucket(dist):
    n = jnp.maximum(dist, 0)
    max_exact = REL_BUCKETS // 2
    nf = jnp.maximum(n, 1).astype(F32)
    large = max_exact + (jnp.log(nf / max_exact) / math.log(REL_MAX_DIST / max_exact)
                         * (REL_BUCKETS - max_exact)).astype(I32)
    large = jnp.minimum(large, REL_BUCKETS - 1)
    return jnp.where(n < max_exact, n, large)


def _bias_kernel(tbl_ref, far_ref, bt_ref, bw_ref, t_ref, w_ref):
    g = pl.program_id(0)
    d = pl.program_id(1)

    def lookup(bkt, head):
        out = jnp.full(bkt.shape, tbl_ref[0, head], F32)
        for b in range(1, REL_BUCKETS):
            out = jnp.where(bkt == b, tbl_ref[b, head], out)
        return out - far_ref[head]

    j = lax.broadcasted_iota(I32, (QB, QB), 0)
    i = lax.broadcasted_iota(I32, (QB, QB), 1)
    keep = ((d != 0) | (i >= j)) & (d != TILE_MASKED) & ((d != TILE_WINDOW_EDGE) | (j > i))
    for hq in range(GQA):
        t_ref[0, 0, :, hq * QB:(hq + 1) * QB] = jnp.where(keep, lookup(bt_ref[0], GQA * g + hq), NEG)

    @pl.when(d == 0)
    def _():
        for hq in range(GQA):
            w_ref[0, :, hq * QB:(hq + 1) * QB] = lookup(bw_ref[...], GQA * g + hq)


def _bias_tables(rel_table):
    tbl = rel_table.astype(F32) * LOG2E
    far_hi = tbl[REL_BUCKETS - 1].astype(BF16)
    far_lo = (tbl[REL_BUCKETS - 1] - far_hi.astype(F32)).astype(BF16)
    far = far_hi.astype(F32) + far_lo.astype(F32)
    q_tail = jnp.zeros((N_HEADS, HEAD_DIM), F32).at[:, 0].set(far_hi.astype(F32)).at[:, 1].set(far_lo.astype(F32))
    i = jnp.arange(QB)
    tile_dist = np.array(list(range(N_BIAS_TILES)) + [0, WINDOW // QB])
    d_t = (QB * jnp.asarray(tile_dist)[:, None, None] + i[None, None, :] - i[None, :, None])
    d_w = i[None, :] - CMP_STRIDE * (jnp.arange(CMP_WIN)[:, None] - CMP_PAD) - (CMP_LEN - 1)
    n_tiles = len(tile_dist)
    t_tab, w_tab = pl.pallas_call(
        _bias_kernel,
        grid=(N_KV, n_tiles),
        in_specs=[pl.BlockSpec(memory_space=pltpu.SMEM),
                  pl.BlockSpec(memory_space=pltpu.SMEM),
                  pl.BlockSpec((1, QB, QB), lambda g, d: (d, 0, 0)),
                  pl.BlockSpec((CMP_WIN, QB), lambda g, d: (0, 0))],
        out_specs=[pl.BlockSpec((1, 1, QB, GQA * QB), lambda g, d: (g, d, 0, 0)),
                   pl.BlockSpec((1, CMP_WIN, GQA * QB), lambda g, d: (g, 0, 0))],
        out_shape=[jax.ShapeDtypeStruct((N_KV, n_tiles, QB, GQA * QB), F32),
                   jax.ShapeDtypeStruct((N_KV, CMP_WIN, GQA * QB), F32)],
        compiler_params=_params("parallel", "arbitrary"),
        name="bias",
    )(tbl, far, _t5_bucket(d_t), _t5_bucket(d_w))
    far_min = min(FAR_TILE_DIST * QB - QB + 1, CMP_STRIDE * (CMP_PAD + 1) - (CMP_LEN - 1))
    max_exact = REL_BUCKETS // 2
    assert math.log(far_min / max_exact) / math.log(REL_MAX_DIST / max_exact) * max_exact > max_exact - 0.75
    return t_tab, w_tab, q_tail


def _overlap_tables(nc, nb):
    c_start = CMP_STRIDE * np.arange(nc)[:, None]
    s_start = SLC_LEN * np.arange(nb)[None, :]
    ov = ((c_start < s_start + SLC_LEN) & (c_start + CMP_LEN > s_start)).astype(np.float32)
    ov[nc - 1:] = 0.0
    out = np.zeros((nc + CMP_WIN, nb), np.float32)
    out[CMP_PAD:CMP_PAD + nc] = ov
    return jnp.asarray(ov.T, dtype=BF16), jnp.asarray(out)


def _block_diag(w):
    nblk, bs, _ = w.shape
    eye = jnp.eye(nblk, dtype=w.dtype)
    return (eye[:, None, :, None] * w[:, :, None, :]).reshape(nblk * bs, nblk * bs)


def _layer(x, c8, w_ada, b_ada, norm1_w, w_in, cmp_pos, cmp_k_w1, cmp_k_w2, cmp_v_w1, cmp_v_w2, rel_table,
           conv_w, conv_b, lru_wa, lru_ba, lru_wx, lru_bx, lru_lambda, gnorm_attn_w, gnorm_rnn_w, w_out,
           norm2_w, w_router, b_router, w1, w3, w2, ws1, ws3, ws2, final_norm_w):
    B, S, D = x.shape
    N = B * S
    E = w_router.shape[1]
    d_attn = N_HEADS * HEAD_DIM
    d_rnn = D - d_attn
    assert S % 512 == 0 and S // SLC_LEN <= 128

    mod = _ada(c8, w_ada, b_ada[None, :])[:B].reshape(B, 6, D)

    n_kv_cols = 6 * N_KV * HEAD_DIM
    n_gate = N_BRANCH * N_HEADS
    wq, wkv, wg, wxr, wgr = jnp.split(w_in, np.cumsum([d_attn, n_kv_cols, n_gate, d_rnn]).tolist(), axis=1)
    wg = jnp.pad(wg.reshape(D, N_KV, GQA * N_BRANCH), ((0, 0), (0, 0), (0, 128 - GQA * N_BRANCH))).reshape(D, 256)
    w_pad = jnp.concatenate([wq, wkv, wg, wxr, wgr], axis=1).astype(BF16)
    t_tab, w_tab, q_tail = _bias_tables(rel_table)
    q_t, kx, vx_t, kvc, gates_t, xr, gr = _inproj(x, mod, norm1_w[None, :], w_pad, q_tail)

    nc = S // CMP_STRIDE
    kvc16 = kvc.reshape(B, 4, nc, CMP_STRIDE * HEAD_DIM)
    pos2 = cmp_pos.reshape(2, CMP_STRIDE * HEAD_DIM)
    kvc_pad, kvc_t = _compress(kvc16, pos2, jnp.stack([cmp_k_w1, cmp_v_w1]), jnp.stack([cmp_k_w2, cmp_v_w2]))
    o_attn = _attn(q_t, kx, vx_t, kvc_pad, kvc_t, gates_t, t_tab, w_tab, *_overlap_tables(nc, 128))

    y_rnn = _rglru(xr, gr, conv_w.reshape(CONV_WIDTH, d_rnn), conv_b[None, :], _block_diag(lru_wa).astype(BF16),
                   lru_ba[None, :], _block_diag(lru_wx).astype(BF16), lru_bx[None, :], lru_lambda[None, :],
                   gnorm_rnn_w[None, :])

    x1, h2, s_t = _outproj(o_attn, y_rnn, x, mod, gnorm_attn_w[None, :], w_out.astype(BF16), norm2_w[None, :],
                           w_router.T)

    tile = min(MOE_TILE, S)
    eid_t, wgt_t, rank_t, counts, tile_counts = _route(s_t, b_router[:, None], tile)
    counts = counts[:, 0].astype(I32)
    tcnt = tile_counts[:, :, 0].astype(I32)
    off = jnp.cumsum(counts) - counts
    run_src = jnp.cumsum(tcnt, axis=1) - tcnt
    earlier = jnp.cumsum(tcnt, axis=0) - tcnt
    run_dst = off[None, :] + earlier
    runs = tuple(r.astype(I32)[:, None, :] for r in (run_src, tcnt, run_dst))
    pos_t = _tilepos(eid_t, rank_t, (run_src - earlier).astype(F32)[:, :, None], tile)

    h2f = h2.reshape(N, D)
    xs = _dispatch(runs, pos_t, h2f, tile)
    ys = _experts(_expert_chunks(counts, N * TOP_K), xs, w1, w3, w2)
    out = _combine(runs, ys, pos_t.T, pos_t, wgt_t, h2f, x1.reshape(N, D), mod, ws1.astype(BF16), ws3.astype(BF16),
                   ws2.astype(BF16), final_norm_w[None, :], S, tile)
    return out.reshape(B, S, D)


def kernel(x, c, w_ada, b_ada, norm1_w, w_in, cmp_pos, cmp_k_w1, cmp_k_w2, cmp_v_w1, cmp_v_w2, rel_table, conv_w, conv_b, lru_wa, lru_ba, lru_wx, lru_bx, lru_lambda, gnorm_attn_w, gnorm_rnn_w, w_out, norm2_w, w_router, b_router, w1, w3, w2, ws1, ws3, ws2, final_norm_w):
    assert w_ada.shape[0] == 1
    c8 = jnp.pad(c, ((0, 8 - c.shape[0]), (0, 0)))
    return _layer(x, c8, w_ada[0], b_ada[0], norm1_w[0], w_in[0], cmp_pos[0], cmp_k_w1[0], cmp_k_w2[0],
                  cmp_v_w1[0], cmp_v_w2[0], rel_table, conv_w[0], conv_b[0], lru_wa[0], lru_ba[0], lru_wx[0],
                  lru_bx[0], lru_lambda[0], gnorm_attn_w[0], gnorm_rnn_w[0], w_out[0], norm2_w[0], w_router[0],
                  b_router[0], w1[0], w3[0], w2[0], ws1[0], ws3[0], ws2[0], final_norm_w)
```

```python
import functools
import math

import jax
import jax.numpy as jnp
import numpy as np
from jax import lax
from jax.experimental import pallas as pl
from jax.experimental.pallas import tpu as pltpu

F32 = jnp.float32
BF16 = jnp.bfloat16
I32 = jnp.int32

HEAD_DIM = 64
N_HEADS = 8
N_KV = 2
GQA = N_HEADS // N_KV
N_BRANCH = 3
CONV_WIDTH = 4
LRU_C = 8.0
CMP_LEN = 32
CMP_STRIDE = 16
SLC_LEN = 64
SLC_TOP = 16
WINDOW = 512
QB = 128
REL_BUCKETS = 32
REL_MAX_DIST = 1024
N_EXPERT_GROUPS = 8
TOP_GROUPS = 4
TOP_K = 8
ROUTED_SCALE = 2.5
MOE_BLOCK = 256
MOE_TILE = 256
ROW_TILE = 8
EPS = 1e-6
NEG = -1e30
LOG2E = math.log2(math.e)
CMP_PAD = 120
CMP_WIN = 128
N_BIAS_TILES = 11
TILE_MASKED = N_BIAS_TILES
TILE_WINDOW_EDGE = N_BIAS_TILES + 1
FAR_TILE_DIST = 8
FAR_CHUNK = 4
VMEM_LIMIT = 52 * 1024 * 1024


def _dot(a, b, **kw):
    return jnp.dot(a, b, preferred_element_type=F32, **kw)


def _dot_nt(a, b, **kw):
    return lax.dot_general(a, b, (((1,), (1,)), ((), ())), preferred_element_type=F32, **kw)


def _gelu(x):
    return 0.5 * x * (1.0 + jnp.tanh(math.sqrt(2.0 / math.pi) * (x + 0.044715 * (x * x * x))))


def _sigmoid(x):
    return 1.0 / (1.0 + jnp.exp(-x))


def _rms(x):
    return x * lax.rsqrt(jnp.mean(x * x, axis=-1, keepdims=True) + EPS)


def _params(*sem):
    return pltpu.CompilerParams(dimension_semantics=sem, vmem_limit_bytes=VMEM_LIMIT)


def _ada_kernel(c_ref, w_ref, b_ref, o_ref):
    c = c_ref[...]
    a = c * _sigmoid(c)
    o_ref[...] = _dot(a, w_ref[...], precision=lax.Precision.HIGHEST) + b_ref[...]


def _ada(c8, w, b):
    d, n = w.shape
    tn = 1536
    return pl.pallas_call(
        _ada_kernel,
        grid=(n // tn,),
        in_specs=[pl.BlockSpec((8, d), lambda j: (0, 0)),
                  pl.BlockSpec((d, tn), lambda j: (0, j)),
                  pl.BlockSpec((1, tn), lambda j: (0, j))],
        out_specs=pl.BlockSpec((8, tn), lambda j: (0, j)),
        out_shape=jax.ShapeDtypeStruct((8, n), F32),
        compiler_params=_params("parallel"),
        name="ada",
    )(c8, w, b)


def _inproj_kernel(x_ref, mod_ref, nw_ref, w_ref, qt_ref, q_ref, kx_ref, vx_ref, kvc_ref, g_ref, xr_ref, gr_ref):
    h = _rms(x_ref[0]) * nw_ref[...]
    h = h * (1.0 + mod_ref[0, 1:2, :]) + mod_ref[0, 0:1, :]
    p = _dot(h.astype(BF16), w_ref[...])
    tm = p.shape[0]
    dq = N_HEADS * HEAD_DIM
    lane = lax.broadcasted_iota(I32, (tm, HEAD_DIM), 1)
    k_tail = jnp.where(lane < 2, 1.0, 0.0)
    v_tail = jnp.where(lane < 1, 1.0, 0.0)
    for hh in range(N_HEADS):
        qh = p[:, hh * HEAD_DIM:(hh + 1) * HEAD_DIM] * (HEAD_DIM ** -0.5 * LOG2E)
        q_tail = jnp.broadcast_to(qt_ref[hh:hh + 1, :], (tm, HEAD_DIM))
        q_t = jnp.concatenate([qh, q_tail], axis=1).T.astype(BF16)
        for blk in range(tm // QB):
            col = (blk * GQA + hh % GQA) * QB
            q_ref[0, hh // GQA, :, col:col + QB] = q_t[:, blk * QB:(blk + 1) * QB]
    for j in range(6):
        for g in range(N_KV):
            col = dq + (j * N_KV + g) * HEAD_DIM
            piece = p[:, col:col + HEAD_DIM]
            if j < 2:
                kvc_ref[0, j * N_KV + g] = piece
            elif j % 2 == 0:
                kx_ref[0, (j // 2 - 1) * N_KV + g] = jnp.concatenate([piece, k_tail], axis=1).astype(BF16)
            else:
                v_t = jnp.concatenate([piece, v_tail], axis=1).T.astype(BF16)
                for blk in range(tm // QB):
                    vx_ref[0, (j // 2 - 1) * N_KV + g, blk] = v_t[:, blk * QB:(blk + 1) * QB]
    c0 = dq + 6 * N_KV * HEAD_DIM
    for g in range(N_KV):
        g_ref[0, g] = _sigmoid(p[:, c0 + g * 128:c0 + (g + 1) * 128]).T[0:16, :]
    xr_ref[0] = p[:, c0 + 256:c0 + 768]
    gr_ref[0] = p[:, c0 + 768:c0 + 1280]


def _inproj(x, mod, nw, w_pad, q_tail, tm=256):
    B, S, D = x.shape
    ncol = w_pad.shape[1]
    heads = lambda n, w: pl.BlockSpec((1, n, tm, w), lambda b, t: (b, 0, t, 0))
    return pl.pallas_call(
        _inproj_kernel,
        grid=(B, S // tm),
        in_specs=[pl.BlockSpec((1, tm, D), lambda b, t: (b, t, 0)),
                  pl.BlockSpec((1, 6, D), lambda b, t: (b, 0, 0)),
                  pl.BlockSpec((1, D), lambda b, t: (0, 0)),
                  pl.BlockSpec((D, ncol), lambda b, t: (0, 0)),
                  pl.BlockSpec(q_tail.shape, lambda b, t: (0, 0))],
        out_specs=[pl.BlockSpec((1, N_KV, 2 * HEAD_DIM, tm * GQA), lambda b, t: (b, 0, 0, t)),
                   heads(4, 2 * HEAD_DIM),
                   pl.BlockSpec((1, 4, tm // QB, 2 * HEAD_DIM, QB), lambda b, t: (b, 0, t, 0, 0)),
                   heads(4, HEAD_DIM),
                   pl.BlockSpec((1, N_KV, 16, tm), lambda b, t: (b, 0, 0, t)),
                   pl.BlockSpec((1, tm, 512), lambda b, t: (b, t, 0)),
                   pl.BlockSpec((1, tm, 512), lambda b, t: (b, t, 0))],
        out_shape=[jax.ShapeDtypeStruct((B, N_KV, 2 * HEAD_DIM, S * GQA), BF16),
                   jax.ShapeDtypeStruct((B, 4, S, 2 * HEAD_DIM), BF16),
                   jax.ShapeDtypeStruct((B, 4, S // QB, 2 * HEAD_DIM, QB), BF16),
                   jax.ShapeDtypeStruct((B, 4, S, HEAD_DIM), F32),
                   jax.ShapeDtypeStruct((B, N_KV, 16, S), F32),
                   jax.ShapeDtypeStruct((B, S, 512), F32),
                   jax.ShapeDtypeStruct((B, S, 512), F32)],
        compiler_params=_params("parallel", "parallel"),
        name="inproj",
    )(x, mod, nw, w_pad, q_tail)


def _compress_kernel(x_ref, pos_ref, w1_ref, w2_ref, o_ref, ot_ref):
    x = x_ref[0, 0]
    nc = x.shape[0]
    half = CMP_STRIDE * HEAD_DIM
    kv = pl.program_id(1) // N_KV
    a = _dot((x + pos_ref[0:1, :]).astype(BF16), w1_ref[kv, 0:half, :].astype(BF16))
    b = _dot((x + pos_ref[1:2, :]).astype(BF16), w1_ref[kv, half:2 * half, :].astype(BF16))
    hid = _gelu(a + pltpu.roll(b, nc - 1, 0))
    out = _dot(hid.astype(BF16), w2_ref[kv].astype(BF16))
    row = lax.broadcasted_iota(I32, out.shape, 0)
    out = jnp.where(row < nc - 1, out, 0.0)
    lane = lax.broadcasted_iota(I32, (nc, HEAD_DIM), 1)
    out = jnp.concatenate([out, jnp.where(lane < 2, 1.0, 0.0)], axis=1)
    o_ref[0, 0, 0:CMP_PAD, :] = jnp.zeros((CMP_PAD, 2 * HEAD_DIM), F32)
    o_ref[0, 0, CMP_PAD:CMP_PAD + nc, :] = out
    o_ref[0, 0, CMP_PAD + nc:, :] = jnp.zeros((CMP_WIN - CMP_PAD, 2 * HEAD_DIM), F32)
    ot_ref[0, 0] = out.T


def _compress(kvc, pos2, w1, w2):
    B, _, NC, W = kvc.shape
    return pl.pallas_call(
        _compress_kernel,
        grid=(B, 4),
        in_specs=[pl.BlockSpec((1, 1, NC, W), lambda b, i: (b, i, 0, 0)),
                  pl.BlockSpec((2, W), lambda b, i: (0, 0)),
                  pl.BlockSpec((2, 2 * W, HEAD_DIM), lambda b, i: (0, 0, 0)),
                  pl.BlockSpec((2, HEAD_DIM, HEAD_DIM), lambda b, i: (0, 0, 0))],
        out_specs=[pl.BlockSpec((1, 1, NC + CMP_WIN, 2 * HEAD_DIM), lambda b, i: (b, i, 0, 0)),
                   pl.BlockSpec((1, 1, 2 * HEAD_DIM, NC), lambda b, i: (b, i, 0, 0))],
        out_shape=[jax.ShapeDtypeStruct((B, 4, NC + CMP_WIN, 2 * HEAD_DIM), F32),
                   jax.ShapeDtypeStruct((B, 4, 2 * HEAD_DIM, NC), F32)],
        compiler_params=_params("parallel", "parallel"),
        name="compress",
    )(kvc, pos2, w1, w2)


def _attn_kernel(q_ref, ks_ref, vs_ref, kw_ref, vw_ref, kc_ref, vc_ref, vct_ref, g_ref, t_ref, w_ref, ovt_ref, ov_ref,
                 o_ref, m_scr, acc_scr, sel_scr, sa_scr, sb_scr):
    qb = pl.program_id(2)
    R = GQA * QB
    q = q_ref[0, 0]
    nc = vct_ref.shape[3]
    nb = ov_ref.shape[1]
    n_top = min(SLC_TOP, ks_ref.shape[2] // SLC_LEN)

    def heads_sum(x):
        out = x[:, 0:QB]
        for hq in range(1, GQA):
            out = out + x[:, hq * QB:(hq + 1) * QB]
        return out

    w0 = pl.multiple_of(qb * 8, 8)
    s_far = _dot(kc_ref[0, 0, CMP_PAD:CMP_PAD + nc, :].astype(BF16), q)
    n_io = lax.broadcasted_iota(I32, (nc, R), 0)
    s_far = jnp.where(n_io < qb * 8 - CMP_PAD, s_far, -jnp.inf)
    s_win = _dot(kc_ref[0, 0, pl.ds(w0, CMP_WIN), :].astype(BF16), q) + w_ref[0]
    j_w = lax.broadcasted_iota(I32, (CMP_WIN, R), 0)
    i_w = lax.broadcasted_iota(I32, (CMP_WIN, R), 1) & (QB - 1)
    dist_w = i_w - CMP_STRIDE * (j_w - CMP_PAD) - (CMP_LEN - 1)
    s_win = jnp.where((dist_w >= 0) & (j_w >= CMP_PAD - qb * 8), s_win, -jnp.inf)
    m = jnp.maximum(jnp.max(s_far, axis=0, keepdims=True), jnp.max(s_win, axis=0, keepdims=True))
    m = jnp.where(m == -jnp.inf, 0.0, m)
    e_far = jnp.exp2(s_far - m)
    e_win = jnp.exp2(s_win - m)
    l = jnp.sum(e_far, axis=0, keepdims=True) + jnp.sum(e_win, axis=0, keepdims=True)
    inv = 1.0 / jnp.maximum(l, 1e-30)
    p_far = e_far * inv
    p_win = e_win * inv
    vcw_t = vc_ref[0, 0, pl.ds(w0, CMP_WIN), :].T
    o_c = _dot(vct_ref[0, 0].astype(BF16), p_far.astype(BF16)) + _dot(vcw_t.astype(BF16), p_win.astype(BF16))
    ov_win_t = ov_ref[pl.ds(w0, CMP_WIN), :].T
    sc_t = (_dot(ovt_ref[...], heads_sum(p_far).astype(BF16))
            + _dot(ov_win_t.astype(BF16), heads_sum(p_win).astype(BF16)))

    blk = lax.broadcasted_iota(I32, (nb, QB), 0)
    t_io = lax.broadcasted_iota(I32, (nb, QB), 1)
    cur = qb * (QB // SLC_LEN) + t_io // SLC_LEN
    forced = (blk == 0) | (blk == cur) | (blk == cur - 1)
    val = jnp.where(forced, jnp.inf, jnp.where(blk <= cur, sc_t, -jnp.inf))
    sel_t = jnp.zeros((nb, QB), F32)
    for _ in range(n_top):
        mx = jnp.max(val, axis=0, keepdims=True)
        hit = (val == mx) & (val > -jnp.inf)
        first = jnp.min(jnp.where(hit, blk, nb), axis=0, keepdims=True)
        pick = blk == first
        sel_t = jnp.where(pick, 1.0, sel_t)
        val = jnp.where(pick, -jnp.inf, val)
    sel_scr[...] = jnp.where(sel_t > 0.5, 0.0, NEG)

    m_scr[...] = jnp.full(m_scr.shape, NEG, F32)
    acc_scr[...] = jnp.zeros(acc_scr.shape, F32)
    blocks_per_chunk = FAR_CHUNK * QB // SLC_LEN

    chunk = FAR_CHUNK * QB

    def qk(c):
        return _dot(ks_ref[0, 0, pl.ds(pl.multiple_of(c * chunk, chunk), chunk), :], q)

    def soft(c, s):
        rows = sel_scr[pl.ds(pl.multiple_of(c * blocks_per_chunk, blocks_per_chunk), blocks_per_chunk), :]
        drop = jnp.concatenate([jnp.broadcast_to(rows[r:r + 1, :], (SLC_LEN, QB)) for r in range(blocks_per_chunk)],
                               axis=0)
        s = s + jnp.concatenate([drop] * GQA, axis=1)
        m_old = m_scr[...]
        m_new = jnp.maximum(m_old, jnp.max(s, axis=0, keepdims=True))
        p = jnp.exp2(s - m_new)
        v_t = jnp.concatenate([vs_ref[0, 0, c * FAR_CHUNK + j] for j in range(FAR_CHUNK)], axis=1)
        acc_scr[...] = jnp.exp2(m_old - m_new) * acc_scr[...] + _dot(v_t, p.astype(BF16))
        m_scr[...] = m_new

    def near_bias(c):
        tiles = []
        for j in range(FAR_CHUNK):
            delta = qb - (c * FAR_CHUNK + j)
            tiles.append(t_ref[0, jnp.where(delta < 0, TILE_MASKED, delta)])
        return jnp.concatenate(tiles, axis=0)

    def chunk_pairs(first, count, bias):
        last = first + jnp.maximum(count - 1, 0)
        sa_scr[...] = qk(first)

        def pair_body(j, carry):
            c = first + 2 * j
            sb_scr[...] = qk(c + 1)
            soft(c, sa_scr[...] + bias(c) if bias else sa_scr[...])
            sa_scr[...] = qk(jnp.minimum(c + 2, last))
            soft(c + 1, sb_scr[...] + bias(c + 1) if bias else sb_scr[...])
            return carry

        lax.fori_loop(0, count // 2, pair_body, 0)

        @pl.when(count % 2 == 1)
        def _():
            soft(last, sa_scr[...] + bias(last) if bias else sa_scr[...])

    n_far = jnp.maximum(qb - (FAR_TILE_DIST - 1), 0) // FAR_CHUNK
    chunk_pairs(0, n_far, None)
    chunk_pairs(n_far, qb // FAR_CHUNK + 1 - n_far, near_bias)
    acc = acc_scr[...]
    o_s = acc[0:HEAD_DIM, :] / acc[HEAD_DIM:HEAD_DIM + 1, :]

    n_wt = WINDOW // QB + 1
    k_t, v_t, b_t = [], [], []
    for j in range(n_wt):
        kb = qb - (n_wt - 1) + j
        kb0 = jnp.maximum(kb, 0)
        k_t.append(kw_ref[0, 0, pl.ds(pl.multiple_of(kb0 * QB, QB), QB), :])
        v_t.append(vw_ref[0, 0, kb0])
        b_t.append(t_ref[0, jnp.where(kb < 0, TILE_MASKED, TILE_WINDOW_EDGE if j == 0 else n_wt - 1 - j)])
    s = _dot(jnp.concatenate(k_t, axis=0), q) + jnp.concatenate(b_t, axis=0)
    p = jnp.exp2(s - jnp.max(s, axis=0, keepdims=True))
    acc = _dot(jnp.concatenate(v_t, axis=1), p.astype(BF16))
    o_w = acc[0:HEAD_DIM, :] / acc[HEAD_DIM:HEAD_DIM + 1, :]

    gates = g_ref[0, 0]
    outs = []
    for hq in range(GQA):
        cols = slice(hq * QB, (hq + 1) * QB)
        c = hq * N_BRANCH
        outs.append(gates[c:c + 1, :] * o_c[0:HEAD_DIM, cols] + gates[c + 1:c + 2, :] * o_s[:, cols]
                    + gates[c + 2:c + 3, :] * o_w[:, cols])
    o_ref[0] = jnp.concatenate(outs, axis=0).T


def _attn(q_t, kx, vx_t, kvc, kvc_t, gates_t, t_tab, w_tab, ov_t, ov):
    B, _, W, _ = q_t.shape
    S = kx.shape[2]
    ncp = kvc.shape[2]
    nb = ov.shape[1]
    keys = lambda j: pl.BlockSpec((1, 1, S, W), lambda b, g, t: (b, j * N_KV + g, 0, 0))
    vals = lambda j: pl.BlockSpec((1, 1, S // QB, W, QB), lambda b, g, t: (b, j * N_KV + g, 0, 0, 0))
    cmp = lambda j: pl.BlockSpec((1, 1, ncp, W), lambda b, g, t: (b, j * N_KV + g, 0, 0))
    return pl.pallas_call(
        _attn_kernel,
        grid=(B, N_KV, S // QB),
        in_specs=[pl.BlockSpec((1, 1, W, GQA * QB), lambda b, g, t: (b, g, 0, t)),
                  keys(0), vals(0), keys(1), vals(1), cmp(0), cmp(1),
                  pl.BlockSpec((1, 1, W, ncp - CMP_WIN), lambda b, g, t: (b, N_KV + g, 0, 0)),
                  pl.BlockSpec((1, 1, 16, QB), lambda b, g, t: (b, g, 0, t)),
                  pl.BlockSpec((1,) + t_tab.shape[1:], lambda b, g, t: (g, 0, 0, 0)),
                  pl.BlockSpec((1, CMP_WIN, GQA * QB), lambda b, g, t: (g, 0, 0)),
                  pl.BlockSpec(ov_t.shape, lambda b, g, t: (0, 0)),
                  pl.BlockSpec(ov.shape, lambda b, g, t: (0, 0))],
        out_specs=pl.BlockSpec((1, QB, GQA * HEAD_DIM), lambda b, g, t: (b, t, g)),
        out_shape=jax.ShapeDtypeStruct((B, S, N_HEADS * HEAD_DIM), F32),
        scratch_shapes=[pltpu.VMEM((1, GQA * QB), F32), pltpu.VMEM((W, GQA * QB), F32), pltpu.VMEM((nb, QB), F32),
                        pltpu.VMEM((FAR_CHUNK * QB, GQA * QB), F32), pltpu.VMEM((FAR_CHUNK * QB, GQA * QB), F32)],
        compiler_params=_params("parallel", "parallel", "arbitrary"),
        name="attn",
    )(q_t, kx, vx_t, kx, vx_t, kvc, kvc, kvc_t, gates_t, t_tab, w_tab, ov_t, ov)


def _rglru_kernel(xr_ref, gr_ref, cw_ref, cb_ref, wa_ref, ba_ref, wx_ref, bx_ref, lam_ref, gw_ref, o_ref,
                  xbuf, hprev, a_scr, u_scr, h_scr):
    ts = xr_ref.shape[1]
    C = xr_ref.shape[2]

    @pl.when(pl.program_id(1) == 0)
    def _():
        xbuf[0:8, :] = jnp.zeros((8, C), F32)
        hprev[...] = jnp.zeros(hprev.shape, F32)

    xbuf[8:8 + ts, :] = xr_ref[0]
    xc = cb_ref[...] + jnp.zeros((ts, C), F32)
    for j in range(CONV_WIDTH):
        xc = xc + cw_ref[j:j + 1, :] * xbuf[pl.ds(8 - (CONV_WIDTH - 1) + j, ts), :]
    xbuf[0:8, :] = xbuf[ts:ts + 8, :]

    xcb = xc.astype(BF16)
    r = _sigmoid(_dot(xcb, wa_ref[...]) + ba_ref[...])
    i = _sigmoid(_dot(xcb, wx_ref[...]) + bx_ref[...])
    z = -lam_ref[...]
    softplus = jnp.maximum(z, 0.0) + jnp.log(1.0 + jnp.exp(-jnp.abs(z)))
    log_a = -LRU_C * r * softplus
    a_scr[...] = jnp.exp(log_a)
    u_scr[...] = jnp.sqrt(1.0 - jnp.exp(2.0 * log_a)) * (i * xc)

    row = lax.broadcasted_iota(I32, (8, C), 0)

    def body(k, h):
        r0 = pl.multiple_of(k * 8, 8)
        a = a_scr[pl.ds(r0, 8), :]
        b = u_scr[pl.ds(r0, 8), :]
        for s in (1, 2, 4):
            keep = row >= s
            b = jnp.where(keep, a * pltpu.roll(b, s, 0) + b, b)
            a = jnp.where(keep, a * pltpu.roll(a, s, 0), a)
        hh = a * h + b
        h_scr[pl.ds(r0, 8), :] = hh
        return jnp.broadcast_to(hh[7:8, :], (8, C))

    hprev[...] = lax.fori_loop(0, ts // 8, body, hprev[...])
    out = h_scr[...] * _gelu(gr_ref[0])
    o_ref[0] = (_rms(out) * gw_ref[...]).astype(BF16)


def _rglru(xr, gr, cw, cb, wa_bd, ba, wx_bd, bx, lam, gw, ts=512):
    B, S, C = xr.shape
    ts = min(ts, S)
    vec = pl.BlockSpec((1, C), lambda b, t: (0, 0))
    mat = pl.BlockSpec((C, C), lambda b, t: (0, 0))
    seq = pl.BlockSpec((1, ts, C), lambda b, t: (b, t, 0))
    return pl.pallas_call(
        _rglru_kernel,
        grid=(B, S // ts),
        in_specs=[seq, seq, pl.BlockSpec((CONV_WIDTH, C), lambda b, t: (0, 0)), vec, mat, vec, mat, vec, vec, vec],
        out_specs=seq,
        out_shape=jax.ShapeDtypeStruct((B, S, C), BF16),
        scratch_shapes=[pltpu.VMEM((ts + 8, C), F32), pltpu.VMEM((8, C), F32), pltpu.VMEM((ts, C), F32),
                        pltpu.VMEM((ts, C), F32), pltpu.VMEM((ts, C), F32)],
        compiler_params=_params("parallel", "arbitrary"),
        name="rglru",
    )(xr, gr, cw, cb, wa_bd, ba, wx_bd, bx, lam, gw)


def _to_row_tiles(ref, x):
    rows = x.shape[0]
    for s in range(ROW_TILE):
        ref[pl.ds(s, rows, stride=ROW_TILE), :] = x[:, s * 128:(s + 1) * 128]


def _from_row_tiles(ref, rows, start=0):
    return jnp.concatenate([ref[pl.ds(start + s, rows, stride=ROW_TILE), :] for s in range(ROW_TILE)], axis=1)


def _outproj_kernel(oa_ref, yr_ref, x_ref, mod_ref, gaw_ref, wo_ref, n2_ref, wr_ref, x1_ref, h2_ref, st_ref):
    da = oa_ref.shape[2]
    ya = (_rms(oa_ref[0]) * gaw_ref[...]).astype(BF16)
    mix = _dot(ya, wo_ref[0:da, :]) + _dot(yr_ref[0], wo_ref[da:, :])
    x1 = x_ref[0] + mod_ref[0, 2:3, :] * mix
    x1_ref[0] = x1
    h2 = (_rms(x1) * n2_ref[...]) * (1.0 + mod_ref[0, 4:5, :]) + mod_ref[0, 3:4, :]
    h2_ref[0] = h2.astype(BF16)
    st_ref[...] = _sigmoid(_dot_nt(wr_ref[...], h2, precision=lax.Precision.HIGHEST))


def _outproj(oa, yr, x, mod, gaw, wo, n2, wr_t, tm=256):
    B, S, D = x.shape
    E = wr_t.shape[0]
    nt = S // tm
    row = lambda w: pl.BlockSpec((1, tm, w), lambda b, t: (b, t, 0))
    return pl.pallas_call(
        _outproj_kernel,
        grid=(B, nt),
        in_specs=[row(oa.shape[2]), row(yr.shape[2]), row(D),
                  pl.BlockSpec((1, 6, D), lambda b, t: (b, 0, 0)),
                  pl.BlockSpec((1, oa.shape[2]), lambda b, t: (0, 0)),
                  pl.BlockSpec(wo.shape, lambda b, t: (0, 0)),
                  pl.BlockSpec((1, D), lambda b, t: (0, 0)),
                  pl.BlockSpec((E, D), lambda b, t: (0, 0))],
        out_specs=[row(D), row(D), pl.BlockSpec((E, tm), lambda b, t: (0, b * nt + t))],
        out_shape=[jax.ShapeDtypeStruct((B, S, D), F32), jax.ShapeDtypeStruct((B, S, D), BF16),
                   jax.ShapeDtypeStruct((E, B * S), F32)],
        compiler_params=_params("parallel", "parallel"),
        name="outproj",
    )(oa, yr, x, mod, gaw, wo, n2, wr_t)


def _route_kernel(s_ref, b_ref, eid_ref, wgt_ref, rank_ref, cnt_ref, tcnt_ref, carry):
    E, tn = s_ref.shape
    per = E // N_EXPERT_GROUPS

    @pl.when(pl.program_id(0) == 0)
    def _():
        carry[...] = jnp.zeros(carry.shape, F32)

    s = s_ref[...]
    s_sel = s + b_ref[...]
    eidx = lax.broadcasted_iota(I32, (E, tn), 0)

    grp = []
    for gi in range(N_EXPERT_GROUPS):
        xg = s_sel[gi * per:(gi + 1) * per, :]
        ig = lax.broadcasted_iota(I32, (per, tn), 0)
        m1 = jnp.max(xg, axis=0, keepdims=True)
        f1 = jnp.min(jnp.where(xg == m1, ig, per), axis=0, keepdims=True)
        m2 = jnp.max(jnp.where(ig == f1, -jnp.inf, xg), axis=0, keepdims=True)
        grp.append(m1 + m2)
    val = []
    for gi in range(N_EXPERT_GROUPS):
        rank = jnp.zeros((1, tn), I32)
        for gj in range(N_EXPERT_GROUPS):
            if gj == gi:
                continue
            ahead = (grp[gj] > grp[gi]) | ((grp[gj] == grp[gi]) & (gj < gi))
            rank = rank + ahead.astype(I32)
        val.append(jnp.where(rank < TOP_GROUPS, s_sel[gi * per:(gi + 1) * per, :], -jnp.inf))
    val = jnp.concatenate(val, axis=0)

    eids, tops = [], []
    hot = jnp.zeros((E, tn), F32)
    for _ in range(TOP_K):
        mx = jnp.max(val, axis=0, keepdims=True)
        first = jnp.min(jnp.where(val == mx, eidx, E), axis=0, keepdims=True)
        pick = eidx == first
        eids.append(first)
        tops.append(jnp.sum(jnp.where(pick, s, 0.0), axis=0, keepdims=True))
        hot = jnp.where(pick, 1.0, hot)
        val = jnp.where(pick, -jnp.inf, val)
    denom = tops[0]
    for t in tops[1:]:
        denom = denom + t

    ti = lax.broadcasted_iota(I32, (tn, tn), 0)
    tj = lax.broadcasted_iota(I32, (tn, tn), 1)
    upper = jnp.where(ti < tj, 1.0, 0.0).astype(BF16)
    before = _dot(hot.astype(BF16), upper) + carry[...]
    for k in range(TOP_K):
        eid_ref[k:k + 1, :] = eids[k]
        wgt_ref[k:k + 1, :] = ROUTED_SCALE * tops[k] / denom
        rank_ref[k:k + 1, :] = jnp.sum(jnp.where(eidx == eids[k], before, 0.0), axis=0, keepdims=True).astype(I32)
    tile_count = jnp.sum(hot, axis=1, keepdims=True)
    tcnt_ref[0] = tile_count
    carry[...] = carry[...] + tile_count
    cnt_ref[...] = carry[...]


def _route(s_t, b_col, tn=256):
    E, N = s_t.shape
    tn = min(tn, N)
    out = pl.BlockSpec((TOP_K, tn), lambda t: (0, t))
    return pl.pallas_call(
        _route_kernel,
        grid=(N // tn,),
        in_specs=[pl.BlockSpec((E, tn), lambda t: (0, t)), pl.BlockSpec((E, 1), lambda t: (0, 0))],
        out_specs=[out, out, out, pl.BlockSpec((E, 1), lambda t: (0, 0)), pl.BlockSpec((1, E, 1), lambda t: (t, 0, 0))],
        out_shape=[jax.ShapeDtypeStruct((TOP_K, N), I32), jax.ShapeDtypeStruct((TOP_K, N), F32),
                   jax.ShapeDtypeStruct((TOP_K, N), I32), jax.ShapeDtypeStruct((E, 1), F32),
                   jax.ShapeDtypeStruct((N // tn, E, 1), F32)],
        scratch_shapes=[pltpu.VMEM((E, 1), F32)],
        compiler_params=_params("arbitrary"),
        name="route",
    )(s_t, b_col)


def _tilepos_kernel(eid_ref, rank_ref, base_ref, o_ref):
    E = base_ref.shape[1]
    tn = eid_ref.shape[1]
    eidx = lax.broadcasted_iota(I32, (E, tn), 0)
    for k in range(TOP_K):
        base = jnp.sum(jnp.where(eidx == eid_ref[k:k + 1, :], base_ref[0], 0.0), axis=0, keepdims=True)
        o_ref[k:k + 1, :] = base.astype(I32) + rank_ref[k:k + 1, :]


def _tilepos(eid_t, rank_t, base, tn):
    K, N = eid_t.shape
    blk = pl.BlockSpec((K, tn), lambda t: (0, t))
    return pl.pallas_call(
        _tilepos_kernel,
        grid=(N // tn,),
        in_specs=[blk, blk, pl.BlockSpec((1,) + base.shape[1:], lambda t: (t, 0, 0))],
        out_specs=blk,
        out_shape=jax.ShapeDtypeStruct((K, N), I32),
        compiler_params=_params("parallel"),
        name="tilepos",
    )(eid_t, rank_t, base)


_HI = -65536


def _pack_pairs(even_rows, odd_rows):
    hi = lax.bitcast_convert_type(even_rows, I32) & _HI
    lo = lax.shift_right_logical(lax.bitcast_convert_type(odd_rows, I32), 16)
    return hi | lo


def _unpack_pairs(words):
    even_rows = lax.bitcast_convert_type(words & _HI, F32)
    odd_rows = lax.bitcast_convert_type(lax.shift_left(words, 16), F32)
    return even_rows, odd_rows


def _run_copies(src_ref, cnt_ref, dst_ref, n_runs, copy):
    def go(e, carry):
        n = cnt_ref[0, 0, e]

        @pl.when(n > 0)
        def _():
            size = pl.multiple_of(n * ROW_TILE, ROW_TILE)
            copy(pl.ds(pl.multiple_of(src_ref[0, 0, e] * ROW_TILE, ROW_TILE), size),
                 pl.ds(pl.multiple_of(dst_ref[0, 0, e] * ROW_TILE, ROW_TILE), size)).start()
        return carry
    lax.fori_loop(0, n_runs, go, 0)


def _pair_onehots(pos_rows, n_pairs, first_pair, width):
    even = 2 * (lax.broadcasted_iota(I32, (n_pairs, width), 0) + first_pair)
    hot_e = jnp.zeros((n_pairs, width), F32)
    hot_o = jnp.zeros((n_pairs, width), F32)
    for k in range(TOP_K):
        hot_e = jnp.where(even == pos_rows[k:k + 1, :], 1.0, hot_e)
        hot_o = jnp.where(even + 1 == pos_rows[k:k + 1, :], 1.0, hot_o)
    return hot_e, hot_o


def _dispatch_kernel(src_ref, cnt_ref, dst_ref, pos_ref, h_ref, xs_hbm, xbuf, sem):
    n_pairs = xbuf.shape[0] // ROW_TILE
    tn = h_ref.shape[0]
    hb = h_ref[...]
    slab = n_pairs // 3
    for sl in range(3):
        hot_e, hot_o = _pair_onehots(pos_ref[...], slab, sl * slab, tn)
        words = _pack_pairs(_dot(hot_e.astype(BF16), hb), _dot(hot_o.astype(BF16), hb))
        for s in range(ROW_TILE):
            xbuf[pl.ds(sl * slab * ROW_TILE + s, slab, stride=ROW_TILE), :] = words[:, s * 128:(s + 1) * 128]
    _run_copies(src_ref, cnt_ref, dst_ref, cnt_ref.shape[2],
                lambda src, dst: pltpu.make_async_copy(xbuf.at[src], xs_hbm.at[dst], sem.at[0]))
    pltpu.make_async_copy(xbuf, xs_hbm.at[pl.ds(0, xbuf.shape[0])], sem.at[0]).wait()


def _run_specs(runs):
    return [pl.BlockSpec((1, 1, r.shape[2]), lambda t: (t, 0, 0), memory_space=pltpu.SMEM) for r in runs]


def _tile_pairs(tn, n_experts):
    return (TOP_K * tn + n_experts) // 2


def _dispatch(runs, pos_t, h2, tn):
    N, D = h2.shape
    W = D // ROW_TILE
    n_pairs = _tile_pairs(tn, runs[0].shape[2] - 1)
    assert n_pairs % 24 == 0
    return pl.pallas_call(
        _dispatch_kernel,
        grid=(N // tn,),
        in_specs=_run_specs(runs) + [pl.BlockSpec((TOP_K, tn), lambda t: (0, t)),
                                     pl.BlockSpec((tn, D), lambda t: (t, 0))],
        out_specs=pl.BlockSpec(memory_space=pl.ANY),
        out_shape=jax.ShapeDtypeStruct((N // tn * n_pairs * ROW_TILE, W), I32),
        scratch_shapes=[pltpu.VMEM((n_pairs * ROW_TILE, W), I32), pltpu.SemaphoreType.DMA((1,))],
        compiler_params=_params("arbitrary"),
        name="dispatch",
    )(*runs, pos_t, h2)


def _experts_kernel(first_ref, size_ref, e_ref, new_ref, next_ref, wslot_ref, total_ref,
                    w1_hbm, w3_hbm, w2_hbm, xs_hbm, ys_hbm,
                    xbuf, ybuf, w1f, w3f, w2f, w1b, w3b, w2b, xsem, ysem, wsem):
    total = total_ref[0]

    def w_copies(e, slot):
        return [pltpu.make_async_copy(src.at[e], dst.at[slot], wsem.at[slot])
                for src, dst in ((w1_hbm, w1f), (w3_hbm, w3f), (w2_hbm, w2f))]

    def rows(c):
        return (pl.multiple_of(first_ref[c] * ROW_TILE, ROW_TILE), pl.multiple_of(size_ref[c] * ROW_TILE, ROW_TILE))

    def x_copy(c, slot):
        first, size = rows(c)
        return pltpu.make_async_copy(xs_hbm.at[pl.ds(first, size)], xbuf.at[slot, pl.ds(0, size)], xsem.at[slot])

    def y_copy(c, slot):
        first, size = rows(c)
        return pltpu.make_async_copy(ybuf.at[slot, pl.ds(0, size)], ys_hbm.at[pl.ds(first, size)], ysem.at[slot])

    xbuf[...] = jnp.zeros(xbuf.shape, I32)
    for cp in w_copies(e_ref[0], 0):
        cp.start()
    x_copy(0, 0).start()
    half = MOE_BLOCK // 2

    def body(c, carry):
        slot = c % 2

        @pl.when(new_ref[c] == 1)
        def _():
            ws = wslot_ref[c]
            for cp in w_copies(e_ref[c], ws):
                cp.wait()
            w1b[...] = w1f[ws].astype(BF16)
            w3b[...] = w3f[ws].astype(BF16)
            w2b[...] = w2f[ws].astype(BF16)

            @pl.when(next_ref[c] >= 0)
            def _():
                for cp in w_copies(next_ref[c], 1 - ws):
                    cp.start()

        x_copy(c, slot).wait()

        @pl.when(c + 1 < total)
        def _():
            x_copy(c + 1, 1 - slot).start()

        x = jnp.concatenate(_unpack_pairs(_from_row_tiles(xbuf.at[slot], half)), axis=0).astype(BF16)
        h1 = _dot(x, w1b[...])
        h3 = _dot(x, w3b[...])
        hid = (h1 * _sigmoid(h1)) * h3
        y = _dot(hid.astype(BF16), w2b[...]).astype(BF16).astype(F32)

        @pl.when(c >= 2)
        def _():
            y_copy(c - 2, slot).wait()

        _to_row_tiles(ybuf.at[slot], _pack_pairs(y[0:half], y[half:]))
        y_copy(c, slot).start()
        return carry

    lax.fori_loop(0, total, body, 0)

    @pl.when(total >= 2)
    def _():
        y_copy(total - 2, total % 2).wait()

    y_copy(total - 1, (total - 1) % 2).wait()

    ybuf[0] = jnp.zeros(ybuf.shape[1:], I32)
    tail_first = total_ref[1]
    tail_pairs = ys_hbm.shape[0] // ROW_TILE - tail_first

    def fill(j, carry):
        size = pl.multiple_of(jnp.minimum(half, tail_pairs - j * half) * ROW_TILE, ROW_TILE)
        cp = pltpu.make_async_copy(ybuf.at[0, pl.ds(0, size)],
                                   ys_hbm.at[pl.ds(pl.multiple_of((tail_first + j * half) * ROW_TILE, ROW_TILE), size)],
                                   ysem.at[0])
        cp.start()
        cp.wait()
        return carry

    lax.fori_loop(0, (tail_pairs + half - 1) // half, fill, 0)


def _expert_chunks(counts, n_pairs_total):
    E = counts.shape[0]
    block = MOE_BLOCK // 2
    n_max = n_pairs_total // block + E
    starts = jnp.cumsum(counts) - counts
    n_chunks = (counts + block - 1) // block
    c_end = jnp.cumsum(n_chunks)
    c = jnp.arange(n_max, dtype=I32)
    e = jnp.minimum(jnp.sum(c_end[None, :] <= c[:, None], axis=1), E - 1).astype(I32)
    j = c - (c_end - n_chunks)[e]
    size = jnp.clip(counts[e] - j * block, 0, block)
    has_rows = counts > 0
    ids = jnp.where(has_rows, jnp.arange(E, dtype=I32), E)
    later = jnp.flip(lax.cummin(jnp.flip(ids)))
    nxt = jnp.concatenate([later[1:], jnp.full((1,), E, I32)])
    nxt = jnp.where(nxt >= E, -1, nxt)
    ordinal = jnp.cumsum(has_rows.astype(I32)) - 1
    to_i32 = lambda a: a.astype(I32)
    meta = jnp.stack([c_end[-1], jnp.sum(counts)])
    return tuple(map(to_i32, (starts[e] + j * block, size, e, j == 0, nxt[e], ordinal[e] % 2, meta)))


def _experts(chunks, xs, w1, w3, w2):
    rows_total, W = xs.shape
    _, D, De = w1.shape
    any_space = pl.BlockSpec(memory_space=pl.ANY)
    block_rows = MOE_BLOCK // 2 * ROW_TILE
    grid_spec = pltpu.PrefetchScalarGridSpec(
        num_scalar_prefetch=len(chunks),
        grid=(1,),
        in_specs=[any_space] * 4,
        out_specs=any_space,
        scratch_shapes=[pltpu.VMEM((2, block_rows, W), I32), pltpu.VMEM((2, block_rows, W), I32),
                        pltpu.VMEM((2, D, De), F32), pltpu.VMEM((2, D, De), F32), pltpu.VMEM((2, De, D), F32),
                        pltpu.VMEM((D, De), BF16), pltpu.VMEM((D, De), BF16), pltpu.VMEM((De, D), BF16),
                        pltpu.SemaphoreType.DMA((2,)), pltpu.SemaphoreType.DMA((2,)), pltpu.SemaphoreType.DMA((2,))],
    )
    return pl.pallas_call(
        _experts_kernel,
        grid_spec=grid_spec,
        out_shape=jax.ShapeDtypeStruct((rows_total, W), I32),
        compiler_params=_params("arbitrary"),
        name="experts",
    )(*chunks, w1, w3, w2, xs)


def _combine_kernel(src_ref, cnt_ref, dst_ref, ys_hbm, pos_ref, post_ref, wt_ref, h2_ref, x1_ref, mod_ref, ws1_ref,
                    ws3_ref, ws2_ref, fw_ref, o_ref, ybuf, sem):
    tm = h2_ref.shape[0]
    n_pairs = ybuf.shape[0] // ROW_TILE
    n_exp = cnt_ref.shape[2] - 1

    @pl.when(pl.program_id(0) == 0)
    def _():
        ybuf[...] = jnp.zeros(ybuf.shape, I32)

    _run_copies(src_ref, cnt_ref, dst_ref, n_exp,
                lambda src, dst: pltpu.make_async_copy(ys_hbm.at[dst], ybuf.at[src], sem.at[0]))

    h = h2_ref[...]
    a = _dot(h, ws1_ref[...])
    hid = (a * _sigmoid(a)) * _dot(h, ws3_ref[...])
    shared = _dot(hid.astype(BF16), ws2_ref[...])

    pair = 2 * lax.broadcasted_iota(I32, (n_pairs, tm), 0)
    lane = 2 * lax.broadcasted_iota(I32, (tm, n_pairs), 1)
    w_cols, mine = [], []
    for odd in range(2):
        w_rows = jnp.zeros((n_pairs, tm), F32)
        hot = jnp.zeros((tm, n_pairs), F32)
        for k in range(TOP_K):
            w_rows = jnp.where(pair + odd == post_ref[k:k + 1, :], wt_ref[k:k + 1, :], w_rows)
            hot = jnp.where(lane + odd == pos_ref[:, k:k + 1], 1.0, hot)
        w_cols.append(jnp.sum(w_rows, axis=1, keepdims=True))
        mine.append(hot.astype(BF16))
    used = pl.multiple_of((src_ref[0, 0, n_exp]) * ROW_TILE, ROW_TILE)
    pltpu.make_async_copy(ys_hbm.at[pl.ds(0, used)], ybuf.at[pl.ds(0, used)], sem.at[0]).wait()
    y_even, y_odd = _unpack_pairs(_from_row_tiles(ybuf, n_pairs))
    routed = (_dot(mine[0], (y_even * w_cols[0]).astype(BF16)) + _dot(mine[1], (y_odd * w_cols[1]).astype(BF16)))
    x2 = x1_ref[...] + mod_ref[0, 5:6, :] * (routed + shared)
    o_ref[...] = _rms(x2) * fw_ref[...]


def _combine(runs, ys, pos, pos_t, wgt_t, h2, x1, mod, ws1, ws3, ws2, fw, seq, tm):
    N, D = h2.shape
    nt = seq // tm
    row = pl.BlockSpec((tm, D), lambda t: (t, 0))
    full = lambda a: pl.BlockSpec(a.shape, lambda t: (0,) * a.ndim)
    return pl.pallas_call(
        _combine_kernel,
        grid=(N // tm,),
        in_specs=_run_specs(runs) + [pl.BlockSpec(memory_space=pl.ANY),
                                     pl.BlockSpec((tm, TOP_K), lambda t: (t, 0)),
                                     pl.BlockSpec((TOP_K, tm), lambda t: (0, t)),
                                     pl.BlockSpec((TOP_K, tm), lambda t: (0, t)),
                                     row, row,
                                     pl.BlockSpec((1, 6, D), lambda t: (t // nt, 0, 0)),
                                     full(ws1), full(ws3), full(ws2), full(fw)],
        out_specs=row,
        out_shape=jax.ShapeDtypeStruct((N, D), F32),
        scratch_shapes=[pltpu.VMEM((_tile_pairs(tm, runs[0].shape[2] - 1) * ROW_TILE, D // ROW_TILE), I32),
                        pltpu.SemaphoreType.DMA((1,))],
        compiler_params=_params("arbitrary"),
        name="combine",
    )(*runs, ys, pos, pos_t, wgt_t, h2, x1, mod, ws1, ws3, ws2, fw)


def _t5_bucket(dist):
    n = jnp.maximum(dist, 0)
    max_exact = REL_BUCKETS // 2
    nf = jnp.maximum(n, 1).astype(F32)
    large = max_exact + (jnp.log(nf / max_exact) / math.log(REL_MAX_DIST / max_exact)
                         * (REL_BUCKETS - max_exact)).astype(I32)
    large = jnp.minimum(large, REL_BUCKETS - 1)
    return jnp.where(n < max_exact, n, large)


def _bias_kernel(tbl_ref, far_ref, bt_ref, bw_ref, t_ref, w_ref):
    g = pl.program_id(0)
    d = pl.program_id(1)

    def lookup(bkt, head):
        out = jnp.full(bkt.shape, tbl_ref[0, head], F32)
        for b in range(1, REL_BUCKETS):
            out = jnp.where(bkt == b, tbl_ref[b, head], out)
        return out - far_ref[head]

    j = lax.broadcasted_iota(I32, (QB, QB), 0)
    i = lax.broadcasted_iota(I32, (QB, QB), 1)
    keep = ((d != 0) | (i >= j)) & (d != TILE_MASKED) & ((d != TILE_WINDOW_EDGE) | (j > i))
    for hq in range(GQA):
        t_ref[0, 0, :, hq * QB:(hq + 1) * QB] = jnp.where(keep, lookup(bt_ref[0], GQA * g + hq), NEG)

    @pl.when(d == 0)
    def _():
        for hq in range(GQA):
            w_ref[0, :, hq * QB:(hq + 1) * QB] = lookup(bw_ref[...], GQA * g + hq)


def _bias_tables(rel_table):
    tbl = rel_table.astype(F32) * LOG2E
    far_hi = tbl[REL_BUCKETS - 1].astype(BF16)
    far_lo = (tbl[REL_BUCKETS - 1] - far_hi.astype(F32)).astype(BF16)
    far = far_hi.astype(F32) + far_lo.astype(F32)
    q_tail = jnp.zeros((N_HEADS, HEAD_DIM), F32).at[:, 0].set(far_hi.astype(F32)).at[:, 1].set(far_lo.astype(F32))
    i = jnp.arange(QB)
    tile_dist = np.array(list(range(N_BIAS_TILES)) + [0, WINDOW // QB])
    d_t = (QB * jnp.asarray(tile_dist)[:, None, None] + i[None, None, :] - i[None, :, None])
    d_w = i[None, :] - CMP_STRIDE * (jnp.arange(CMP_WIN)[:, None] - CMP_PAD) - (CMP_LEN - 1)
    n_tiles = len(tile_dist)
    t_tab, w_tab = pl.pallas_call(
        _bias_kernel,
        grid=(N_KV, n_tiles),
        in_specs=[pl.BlockSpec(memory_space=pltpu.SMEM),
                  pl.BlockSpec(memory_space=pltpu.SMEM),
                  pl.BlockSpec((1, QB, QB), lambda g, d: (d, 0, 0)),
                  pl.BlockSpec((CMP_WIN, QB), lambda g, d: (0, 0))],
        out_specs=[pl.BlockSpec((1, 1, QB, GQA * QB), lambda g, d: (g, d, 0, 0)),
                   pl.BlockSpec((1, CMP_WIN, GQA * QB), lambda g, d: (g, 0, 0))],
        out_shape=[jax.ShapeDtypeStruct((N_KV, n_tiles, QB, GQA * QB), F32),
                   jax.ShapeDtypeStruct((N_KV, CMP_WIN, GQA * QB), F32)],
        compiler_params=_params("parallel", "arbitrary"),
        name="bias",
    )(tbl, far, _t5_bucket(d_t), _t5_bucket(d_w))
    far_min = min(FAR_TILE_DIST * QB - QB + 1, CMP_STRIDE * (CMP_PAD + 1) - (CMP_LEN - 1))
    max_exact = REL_BUCKETS // 2
    assert math.log(far_min / max_exact) / math.log(REL_MAX_DIST / max_exact) * max_exact > max_exact - 0.75
    return t_tab, w_tab, q_tail


def _overlap_tables(nc, nb):
    c_start = CMP_STRIDE * np.arange(nc)[:, None]
    s_start = SLC_LEN * np.arange(nb)[None, :]
    ov = ((c_start < s_start + SLC_LEN) & (c_start + CMP_LEN > s_start)).astype(np.float32)
    ov[nc - 1:] = 0.0
    out = np.zeros((nc + CMP_WIN, nb), np.float32)
    out[CMP_PAD:CMP_PAD + nc] = ov
    return jnp.asarray(ov.T, dtype=BF16), jnp.asarray(out)


def _block_diag(w):
    nblk, bs, _ = w.shape
    eye = jnp.eye(nblk, dtype=w.dtype)
    return (eye[:, None, :, None] * w[:, :, None, :]).reshape(nblk * bs, nblk * bs)


def _layer(x, c8, w_ada, b_ada, norm1_w, w_in, cmp_pos, cmp_k_w1, cmp_k_w2, cmp_v_w1, cmp_v_w2, rel_table,
           conv_w, conv_b, lru_wa, lru_ba, lru_wx, lru_bx, lru_lambda, gnorm_attn_w, gnorm_rnn_w, w_out,
           norm2_w, w_router, b_router, w1, w3, w2, ws1, ws3, ws2, final_norm_w):
    B, S, D = x.shape
    N = B * S
    E = w_router.shape[1]
    d_attn = N_HEADS * HEAD_DIM
    d_rnn = D - d_attn
    assert S % 512 == 0 and S // SLC_LEN <= 128

    mod = _ada(c8, w_ada, b_ada[None, :])[:B].reshape(B, 6, D)

    n_kv_cols = 6 * N_KV * HEAD_DIM
    n_gate = N_BRANCH * N_HEADS
    wq, wkv, wg, wxr, wgr = jnp.split(w_in, np.cumsum([d_attn, n_kv_cols, n_gate, d_rnn]).tolist(), axis=1)
    wg = jnp.pad(wg.reshape(D, N_KV, GQA * N_BRANCH), ((0, 0), (0, 0), (0, 128 - GQA * N_BRANCH))).reshape(D, 256)
    w_pad = jnp.concatenate([wq, wkv, wg, wxr, wgr], axis=1).astype(BF16)
    t_tab, w_tab, q_tail = _bias_tables(rel_table)
    q_t, kx, vx_t, kvc, gates_t, xr, gr = _inproj(x, mod, norm1_w[None, :], w_pad, q_tail)

    nc = S // CMP_STRIDE
    kvc16 = kvc.reshape(B, 4, nc, CMP_STRIDE * HEAD_DIM)
    pos2 = cmp_pos.reshape(2, CMP_STRIDE * HEAD_DIM)
    kvc_pad, kvc_t = _compress(kvc16, pos2, jnp.stack([cmp_k_w1, cmp_v_w1]), jnp.stack([cmp_k_w2, cmp_v_w2]))
    o_attn = _attn(q_t, kx, vx_t, kvc_pad, kvc_t, gates_t, t_tab, w_tab, *_overlap_tables(nc, 128))

    y_rnn = _rglru(xr, gr, conv_w.reshape(CONV_WIDTH, d_rnn), conv_b[None, :], _block_diag(lru_wa).astype(BF16),
                   lru_ba[None, :], _block_diag(lru_wx).astype(BF16), lru_bx[None, :], lru_lambda[None, :],
                   gnorm_rnn_w[None, :])

    x1, h2, s_t = _outproj(o_attn, y_rnn, x, mod, gnorm_attn_w[None, :], w_out.astype(BF16), norm2_w[None, :],
                           w_router.T)

    tile = min(MOE_TILE, S)
    eid_t, wgt_t, rank_t, counts, tile_counts = _route(s_t, b_router[:, None], tile)
    del counts
    tcnt = tile_counts[:, :, 0].astype(I32)
    n_tiles = tcnt.shape[0]
    tile_pairs = _tile_pairs(tile, E)
    rpairs = (tcnt + 1) // 2
    rpairs = jnp.concatenate([rpairs, tile_pairs - jnp.sum(rpairs, axis=1, keepdims=True)], axis=1)
    per_expert = jnp.sum(rpairs, axis=0)
    off = jnp.cumsum(per_expert) - per_expert
    run_src = jnp.cumsum(rpairs, axis=1) - rpairs
    run_dst = off[None, :] + jnp.cumsum(rpairs, axis=0) - rpairs
    runs = tuple(r.astype(I32)[:, None, :] for r in (run_src, rpairs, run_dst))
    earlier = jnp.cumsum(tcnt, axis=0) - tcnt
    pos_t = _tilepos(eid_t, rank_t, (2 * run_src[:, :E] - earlier).astype(F32)[:, :, None], tile)

    h2f = h2.reshape(N, D)
    xs = _dispatch(runs, pos_t, h2f, tile)
    ys = _experts(_expert_chunks(per_expert[:E], n_tiles * tile_pairs), xs, w1, w3, w2)
    out = _combine(runs, ys, pos_t.T, pos_t, wgt_t, h2f, x1.reshape(N, D), mod, ws1.astype(BF16), ws3.astype(BF16),
                   ws2.astype(BF16), final_norm_w[None, :], S, tile)
    return out.reshape(B, S, D)


def kernel(x, c, w_ada, b_ada, norm1_w, w_in, cmp_pos, cmp_k_w1, cmp_k_w2, cmp_v_w1, cmp_v_w2, rel_table, conv_w, conv_b, lru_wa, lru_ba, lru_wx, lru_bx, lru_lambda, gnorm_attn_w, gnorm_rnn_w, w_out, norm2_w, w_router, b_router, w1, w3, w2, ws1, ws3, ws2, final_norm_w):
    assert w_ada.shape[0] == 1
    c8 = jnp.pad(c, ((0, 8 - c.shape[0]), (0, 0)))
    return _layer(x, c8, w_ada[0], b_ada[0], norm1_w[0], w_in[0], cmp_pos[0], cmp_k_w1[0], cmp_k_w2[0],
                  cmp_v_w1[0], cmp_v_w2[0], rel_table, conv_w[0], conv_b[0], lru_wa[0], lru_ba[0], lru_wx[0],
                  lru_bx[0], lru_lambda[0], gnorm_attn_w[0], gnorm_rnn_w[0], w_out[0], norm2_w[0], w_router[0],
                  b_router[0], w1[0], w3[0], w2[0], ws1[0], ws3[0], ws2[0], final_norm_w)
```

```python
import functools
import math

import jax
import jax.numpy as jnp
import numpy as np
from jax import lax
from jax.experimental import pallas as pl
from jax.experimental.pallas import tpu as pltpu

F32 = jnp.float32
BF16 = jnp.bfloat16
I32 = jnp.int32

HEAD_DIM = 64
N_HEADS = 8
N_KV = 2
GQA = N_HEADS // N_KV
N_BRANCH = 3
CONV_WIDTH = 4
LRU_C = 8.0
CMP_LEN = 32
CMP_STRIDE = 16
SLC_LEN = 64
SLC_TOP = 16
WINDOW = 512
QB = 128
REL_BUCKETS = 32
REL_MAX_DIST = 1024
N_EXPERT_GROUPS = 8
TOP_GROUPS = 4
TOP_K = 8
ROUTED_SCALE = 2.5
MOE_BLOCK = 576
MOE_TILE = 256
ROW_TILE = 8
EPS = 1e-6
NEG = -1e30
LOG2E = math.log2(math.e)
CMP_PAD = 120
CMP_WIN = 128
N_BIAS_TILES = 11
TILE_MASKED = N_BIAS_TILES
TILE_WINDOW_EDGE = N_BIAS_TILES + 1
FAR_TILE_DIST = 8
FAR_CHUNK = 4
VMEM_LIMIT = 52 * 1024 * 1024


def _dot(a, b, **kw):
    return jnp.dot(a, b, preferred_element_type=F32, **kw)


def _dot_nt(a, b, **kw):
    return lax.dot_general(a, b, (((1,), (1,)), ((), ())), preferred_element_type=F32, **kw)


def _gelu(x):
    return 0.5 * x * (1.0 + jnp.tanh(math.sqrt(2.0 / math.pi) * (x + 0.044715 * (x * x * x))))


def _sigmoid(x):
    return 1.0 / (1.0 + jnp.exp(-x))


def _rms(x):
    return x * lax.rsqrt(jnp.mean(x * x, axis=-1, keepdims=True) + EPS)


def _params(*sem):
    return pltpu.CompilerParams(dimension_semantics=sem, vmem_limit_bytes=VMEM_LIMIT)


def _ada_kernel(c_ref, w_ref, b_ref, o_ref):
    c = c_ref[...]
    a = c * _sigmoid(c)
    o_ref[...] = _dot(a, w_ref[...], precision=lax.Precision.HIGHEST) + b_ref[...]


def _ada(c8, w, b):
    d, n = w.shape
    tn = 1536
    return pl.pallas_call(
        _ada_kernel,
        grid=(n // tn,),
        in_specs=[pl.BlockSpec((8, d), lambda j: (0, 0)),
                  pl.BlockSpec((d, tn), lambda j: (0, j)),
                  pl.BlockSpec((1, tn), lambda j: (0, j))],
        out_specs=pl.BlockSpec((8, tn), lambda j: (0, j)),
        out_shape=jax.ShapeDtypeStruct((8, n), F32),
        compiler_params=_params("parallel"),
        name="ada",
    )(c8, w, b)


def _inproj_kernel(x_ref, mod_ref, nw_ref, w_ref, qt_ref, q_ref, kx_ref, vx_ref, kvc_ref, g_ref, xr_ref, gr_ref):
    h = _rms(x_ref[0]) * nw_ref[...]
    h = h * (1.0 + mod_ref[0, 1:2, :]) + mod_ref[0, 0:1, :]
    p = _dot(h.astype(BF16), w_ref[...])
    tm = p.shape[0]
    dq = N_HEADS * HEAD_DIM
    lane = lax.broadcasted_iota(I32, (tm, HEAD_DIM), 1)
    k_tail = jnp.where(lane < 2, 1.0, 0.0)
    v_tail = jnp.where(lane < 1, 1.0, 0.0)
    for hh in range(N_HEADS):
        qh = p[:, hh * HEAD_DIM:(hh + 1) * HEAD_DIM] * (HEAD_DIM ** -0.5 * LOG2E)
        q_tail = jnp.broadcast_to(qt_ref[hh:hh + 1, :], (tm, HEAD_DIM))
        q_t = jnp.concatenate([qh, q_tail], axis=1).T.astype(BF16)
        for blk in range(tm // QB):
            col = (blk * GQA + hh % GQA) * QB
            q_ref[0, hh // GQA, :, col:col + QB] = q_t[:, blk * QB:(blk + 1) * QB]
    for j in range(6):
        for g in range(N_KV):
            col = dq + (j * N_KV + g) * HEAD_DIM
            piece = p[:, col:col + HEAD_DIM]
            if j < 2:
                kvc_ref[0, j * N_KV + g] = piece
            elif j % 2 == 0:
                kx_ref[0, (j // 2 - 1) * N_KV + g] = jnp.concatenate([piece, k_tail], axis=1).astype(BF16)
            else:
                v_t = jnp.concatenate([piece, v_tail], axis=1).T.astype(BF16)
                for blk in range(tm // QB):
                    vx_ref[0, (j // 2 - 1) * N_KV + g, blk] = v_t[:, blk * QB:(blk + 1) * QB]
    c0 = dq + 6 * N_KV * HEAD_DIM
    for g in range(N_KV):
        g_ref[0, g] = _sigmoid(p[:, c0 + g * 128:c0 + (g + 1) * 128]).T[0:16, :]
    xr_ref[0] = p[:, c0 + 256:c0 + 768]
    gr_ref[0] = p[:, c0 + 768:c0 + 1280]


def _inproj(x, mod, nw, w_pad, q_tail, tm=256):
    B, S, D = x.shape
    ncol = w_pad.shape[1]
    heads = lambda n, w: pl.BlockSpec((1, n, tm, w), lambda b, t: (b, 0, t, 0))
    return pl.pallas_call(
        _inproj_kernel,
        grid=(B, S // tm),
        in_specs=[pl.BlockSpec((1, tm, D), lambda b, t: (b, t, 0)),
                  pl.BlockSpec((1, 6, D), lambda b, t: (b, 0, 0)),
                  pl.BlockSpec((1, D), lambda b, t: (0, 0)),
                  pl.BlockSpec((D, ncol), lambda b, t: (0, 0)),
                  pl.BlockSpec(q_tail.shape, lambda b, t: (0, 0))],
        out_specs=[pl.BlockSpec((1, N_KV, 2 * HEAD_DIM, tm * GQA), lambda b, t: (b, 0, 0, t)),
                   heads(4, 2 * HEAD_DIM),
                   pl.BlockSpec((1, 4, tm // QB, 2 * HEAD_DIM, QB), lambda b, t: (b, 0, t, 0, 0)),
                   heads(4, HEAD_DIM),
                   pl.BlockSpec((1, N_KV, 16, tm), lambda b, t: (b, 0, 0, t)),
                   pl.BlockSpec((1, tm, 512), lambda b, t: (b, t, 0)),
                   pl.BlockSpec((1, tm, 512), lambda b, t: (b, t, 0))],
        out_shape=[jax.ShapeDtypeStruct((B, N_KV, 2 * HEAD_DIM, S * GQA), BF16),
                   jax.ShapeDtypeStruct((B, 4, S, 2 * HEAD_DIM), BF16),
                   jax.ShapeDtypeStruct((B, 4, S // QB, 2 * HEAD_DIM, QB), BF16),
                   jax.ShapeDtypeStruct((B, 4, S, HEAD_DIM), F32),
                   jax.ShapeDtypeStruct((B, N_KV, 16, S), F32),
                   jax.ShapeDtypeStruct((B, S, 512), F32),
                   jax.ShapeDtypeStruct((B, S, 512), F32)],
        compiler_params=_params("parallel", "parallel"),
        name="inproj",
    )(x, mod, nw, w_pad, q_tail)


def _compress_kernel(x_ref, pos_ref, w1_ref, w2_ref, o_ref, ot_ref):
    x = x_ref[0, 0]
    nc = x.shape[0]
    half = CMP_STRIDE * HEAD_DIM
    kv = pl.program_id(1) // N_KV
    a = _dot((x + pos_ref[0:1, :]).astype(BF16), w1_ref[kv, 0:half, :].astype(BF16))
    b = _dot((x + pos_ref[1:2, :]).astype(BF16), w1_ref[kv, half:2 * half, :].astype(BF16))
    hid = _gelu(a + pltpu.roll(b, nc - 1, 0))
    out = _dot(hid.astype(BF16), w2_ref[kv].astype(BF16))
    row = lax.broadcasted_iota(I32, out.shape, 0)
    out = jnp.where(row < nc - 1, out, 0.0)
    lane = lax.broadcasted_iota(I32, (nc, HEAD_DIM), 1)
    out = jnp.concatenate([out, jnp.where(lane < 2, 1.0, 0.0)], axis=1)
    o_ref[0, 0, 0:CMP_PAD, :] = jnp.zeros((CMP_PAD, 2 * HEAD_DIM), F32)
    o_ref[0, 0, CMP_PAD:CMP_PAD + nc, :] = out
    o_ref[0, 0, CMP_PAD + nc:, :] = jnp.zeros((CMP_WIN - CMP_PAD, 2 * HEAD_DIM), F32)
    ot_ref[0, 0] = out.T


def _compress(kvc, pos2, w1, w2):
    B, _, NC, W = kvc.shape
    return pl.pallas_call(
        _compress_kernel,
        grid=(B, 4),
        in_specs=[pl.BlockSpec((1, 1, NC, W), lambda b, i: (b, i, 0, 0)),
                  pl.BlockSpec((2, W), lambda b, i: (0, 0)),
                  pl.BlockSpec((2, 2 * W, HEAD_DIM), lambda b, i: (0, 0, 0)),
                  pl.BlockSpec((2, HEAD_DIM, HEAD_DIM), lambda b, i: (0, 0, 0))],
        out_specs=[pl.BlockSpec((1, 1, NC + CMP_WIN, 2 * HEAD_DIM), lambda b, i: (b, i, 0, 0)),
                   pl.BlockSpec((1, 1, 2 * HEAD_DIM, NC), lambda b, i: (b, i, 0, 0))],
        out_shape=[jax.ShapeDtypeStruct((B, 4, NC + CMP_WIN, 2 * HEAD_DIM), F32),
                   jax.ShapeDtypeStruct((B, 4, 2 * HEAD_DIM, NC), F32)],
        compiler_params=_params("parallel", "parallel"),
        name="compress",
    )(kvc, pos2, w1, w2)


def _attn_kernel(q_ref, ks_ref, vs_ref, kw_ref, vw_ref, kc_ref, vc_ref, vct_ref, g_ref, t_ref, w_ref, ovt_ref, ov_ref,
                 o_ref, m_scr, acc_scr, sel_scr, sa_scr, sb_scr):
    qb = pl.program_id(2)
    R = GQA * QB
    q = q_ref[0, 0]
    nc = vct_ref.shape[3]
    nb = ov_ref.shape[1]
    n_top = min(SLC_TOP, ks_ref.shape[2] // SLC_LEN)

    def heads_sum(x):
        out = x[:, 0:QB]
        for hq in range(1, GQA):
            out = out + x[:, hq * QB:(hq + 1) * QB]
        return out

    w0 = pl.multiple_of(qb * 8, 8)
    s_far = _dot(kc_ref[0, 0, CMP_PAD:CMP_PAD + nc, :].astype(BF16), q)
    n_io = lax.broadcasted_iota(I32, (nc, R), 0)
    s_far = jnp.where(n_io < qb * 8 - CMP_PAD, s_far, -jnp.inf)
    s_win = _dot(kc_ref[0, 0, pl.ds(w0, CMP_WIN), :].astype(BF16), q) + w_ref[0]
    j_w = lax.broadcasted_iota(I32, (CMP_WIN, R), 0)
    i_w = lax.broadcasted_iota(I32, (CMP_WIN, R), 1) & (QB - 1)
    dist_w = i_w - CMP_STRIDE * (j_w - CMP_PAD) - (CMP_LEN - 1)
    s_win = jnp.where((dist_w >= 0) & (j_w >= CMP_PAD - qb * 8), s_win, -jnp.inf)
    m = jnp.maximum(jnp.max(s_far, axis=0, keepdims=True), jnp.max(s_win, axis=0, keepdims=True))
    m = jnp.where(m == -jnp.inf, 0.0, m)
    e_far = jnp.exp2(s_far - m)
    e_win = jnp.exp2(s_win - m)
    l = jnp.sum(e_far, axis=0, keepdims=True) + jnp.sum(e_win, axis=0, keepdims=True)
    inv = 1.0 / jnp.maximum(l, 1e-30)
    p_far = e_far * inv
    p_win = e_win * inv
    vcw_t = vc_ref[0, 0, pl.ds(w0, CMP_WIN), :].T
    o_c = _dot(vct_ref[0, 0].astype(BF16), p_far.astype(BF16)) + _dot(vcw_t.astype(BF16), p_win.astype(BF16))
    ov_win_t = ov_ref[pl.ds(w0, CMP_WIN), :].T
    sc_t = (_dot(ovt_ref[...], heads_sum(p_far).astype(BF16))
            + _dot(ov_win_t.astype(BF16), heads_sum(p_win).astype(BF16)))

    blk = lax.broadcasted_iota(I32, (nb, QB), 0)
    t_io = lax.broadcasted_iota(I32, (nb, QB), 1)
    cur = qb * (QB // SLC_LEN) + t_io // SLC_LEN
    forced = (blk == 0) | (blk == cur) | (blk == cur - 1)
    val = jnp.where(forced, jnp.inf, jnp.where(blk <= cur, sc_t, -jnp.inf))
    sel_t = jnp.zeros((nb, QB), F32)
    for _ in range(n_top):
        mx = jnp.max(val, axis=0, keepdims=True)
        hit = (val == mx) & (val > -jnp.inf)
        first = jnp.min(jnp.where(hit, blk, nb), axis=0, keepdims=True)
        pick = blk == first
        sel_t = jnp.where(pick, 1.0, sel_t)
        val = jnp.where(pick, -jnp.inf, val)
    sel_scr[...] = jnp.where(sel_t > 0.5, 0.0, NEG)

    m_scr[...] = jnp.full(m_scr.shape, NEG, F32)
    acc_scr[...] = jnp.zeros(acc_scr.shape, F32)
    blocks_per_chunk = FAR_CHUNK * QB // SLC_LEN

    chunk = FAR_CHUNK * QB

    def qk(c):
        return _dot(ks_ref[0, 0, pl.ds(pl.multiple_of(c * chunk, chunk), chunk), :], q)

    def soft(c, s):
        rows = sel_scr[pl.ds(pl.multiple_of(c * blocks_per_chunk, blocks_per_chunk), blocks_per_chunk), :]
        drop = jnp.concatenate([jnp.broadcast_to(rows[r:r + 1, :], (SLC_LEN, QB)) for r in range(blocks_per_chunk)],
                               axis=0)
        s = s + jnp.concatenate([drop] * GQA, axis=1)
        m_old = m_scr[...]
        m_new = jnp.maximum(m_old, jnp.max(s, axis=0, keepdims=True))
        p = jnp.exp2(s - m_new)
        v_t = jnp.concatenate([vs_ref[0, 0, c * FAR_CHUNK + j] for j in range(FAR_CHUNK)], axis=1)
        acc_scr[...] = jnp.exp2(m_old - m_new) * acc_scr[...] + _dot(v_t, p.astype(BF16))
        m_scr[...] = m_new

    def near_bias(c):
        tiles = []
        for j in range(FAR_CHUNK):
            delta = qb - (c * FAR_CHUNK + j)
            tiles.append(t_ref[0, jnp.where(delta < 0, TILE_MASKED, delta)])
        return jnp.concatenate(tiles, axis=0)

    def chunk_pairs(first, count, bias):
        last = first + jnp.maximum(count - 1, 0)
        sa_scr[...] = qk(first)

        def pair_body(j, carry):
            c = first + 2 * j
            sb_scr[...] = qk(c + 1)
            soft(c, sa_scr[...] + bias(c) if bias else sa_scr[...])
            sa_scr[...] = qk(jnp.minimum(c + 2, last))
            soft(c + 1, sb_scr[...] + bias(c + 1) if bias else sb_scr[...])
            return carry

        lax.fori_loop(0, count // 2, pair_body, 0)

        @pl.when(count % 2 == 1)
        def _():
            soft(last, sa_scr[...] + bias(last) if bias else sa_scr[...])

    n_far = jnp.maximum(qb - (FAR_TILE_DIST - 1), 0) // FAR_CHUNK
    chunk_pairs(0, n_far, None)
    chunk_pairs(n_far, qb // FAR_CHUNK + 1 - n_far, near_bias)
    acc = acc_scr[...]
    o_s = acc[0:HEAD_DIM, :] / acc[HEAD_DIM:HEAD_DIM + 1, :]

    n_wt = WINDOW // QB + 1
    k_t, v_t, b_t = [], [], []
    for j in range(n_wt):
        kb = qb - (n_wt - 1) + j
        kb0 = jnp.maximum(kb, 0)
        k_t.append(kw_ref[0, 0, pl.ds(pl.multiple_of(kb0 * QB, QB), QB), :])
        v_t.append(vw_ref[0, 0, kb0])
        b_t.append(t_ref[0, jnp.where(kb < 0, TILE_MASKED, TILE_WINDOW_EDGE if j == 0 else n_wt - 1 - j)])
    s = _dot(jnp.concatenate(k_t, axis=0), q) + jnp.concatenate(b_t, axis=0)
    p = jnp.exp2(s - jnp.max(s, axis=0, keepdims=True))
    acc = _dot(jnp.concatenate(v_t, axis=1), p.astype(BF16))
    o_w = acc[0:HEAD_DIM, :] / acc[HEAD_DIM:HEAD_DIM + 1, :]

    gates = g_ref[0, 0]
    outs = []
    for hq in range(GQA):
        cols = slice(hq * QB, (hq + 1) * QB)
        c = hq * N_BRANCH
        outs.append(gates[c:c + 1, :] * o_c[0:HEAD_DIM, cols] + gates[c + 1:c + 2, :] * o_s[:, cols]
                    + gates[c + 2:c + 3, :] * o_w[:, cols])
    o_ref[0] = jnp.concatenate(outs, axis=0).T


def _attn(q_t, kx, vx_t, kvc, kvc_t, gates_t, t_tab, w_tab, ov_t, ov):
    B, _, W, _ = q_t.shape
    S = kx.shape[2]
    ncp = kvc.shape[2]
    nb = ov.shape[1]
    keys = lambda j: pl.BlockSpec((1, 1, S, W), lambda b, g, t: (b, j * N_KV + g, 0, 0))
    vals = lambda j: pl.BlockSpec((1, 1, S // QB, W, QB), lambda b, g, t: (b, j * N_KV + g, 0, 0, 0))
    cmp = lambda j: pl.BlockSpec((1, 1, ncp, W), lambda b, g, t: (b, j * N_KV + g, 0, 0))
    return pl.pallas_call(
        _attn_kernel,
        grid=(B, N_KV, S // QB),
        in_specs=[pl.BlockSpec((1, 1, W, GQA * QB), lambda b, g, t: (b, g, 0, t)),
                  keys(0), vals(0), keys(1), vals(1), cmp(0), cmp(1),
                  pl.BlockSpec((1, 1, W, ncp - CMP_WIN), lambda b, g, t: (b, N_KV + g, 0, 0)),
                  pl.BlockSpec((1, 1, 16, QB), lambda b, g, t: (b, g, 0, t)),
                  pl.BlockSpec((1,) + t_tab.shape[1:], lambda b, g, t: (g, 0, 0, 0)),
                  pl.BlockSpec((1, CMP_WIN, GQA * QB), lambda b, g, t: (g, 0, 0)),
                  pl.BlockSpec(ov_t.shape, lambda b, g, t: (0, 0)),
                  pl.BlockSpec(ov.shape, lambda b, g, t: (0, 0))],
        out_specs=pl.BlockSpec((1, QB, GQA * HEAD_DIM), lambda b, g, t: (b, t, g)),
        out_shape=jax.ShapeDtypeStruct((B, S, N_HEADS * HEAD_DIM), F32),
        scratch_shapes=[pltpu.VMEM((1, GQA * QB), F32), pltpu.VMEM((W, GQA * QB), F32), pltpu.VMEM((nb, QB), F32),
                        pltpu.VMEM((FAR_CHUNK * QB, GQA * QB), F32), pltpu.VMEM((FAR_CHUNK * QB, GQA * QB), F32)],
        compiler_params=_params("parallel", "parallel", "arbitrary"),
        name="attn",
    )(q_t, kx, vx_t, kx, vx_t, kvc, kvc, kvc_t, gates_t, t_tab, w_tab, ov_t, ov)


def _rglru_kernel(xr_ref, gr_ref, cw_ref, cb_ref, wa_ref, ba_ref, wx_ref, bx_ref, lam_ref, gw_ref, o_ref,
                  xbuf, hprev, a_scr, u_scr, h_scr):
    ts = xr_ref.shape[1]
    C = xr_ref.shape[2]

    @pl.when(pl.program_id(1) == 0)
    def _():
        xbuf[0:8, :] = jnp.zeros((8, C), F32)
        hprev[...] = jnp.zeros(hprev.shape, F32)

    xbuf[8:8 + ts, :] = xr_ref[0]
    xc = cb_ref[...] + jnp.zeros((ts, C), F32)
    for j in range(CONV_WIDTH):
        xc = xc + cw_ref[j:j + 1, :] * xbuf[pl.ds(8 - (CONV_WIDTH - 1) + j, ts), :]
    xbuf[0:8, :] = xbuf[ts:ts + 8, :]

    xcb = xc.astype(BF16)
    r = _sigmoid(_dot(xcb, wa_ref[...]) + ba_ref[...])
    i = _sigmoid(_dot(xcb, wx_ref[...]) + bx_ref[...])
    z = -lam_ref[...]
    softplus = jnp.maximum(z, 0.0) + jnp.log(1.0 + jnp.exp(-jnp.abs(z)))
    log_a = -LRU_C * r * softplus
    a_scr[...] = jnp.exp(log_a)
    u_scr[...] = jnp.sqrt(1.0 - jnp.exp(2.0 * log_a)) * (i * xc)

    row = lax.broadcasted_iota(I32, (8, C), 0)

    def body(k, h):
        r0 = pl.multiple_of(k * 8, 8)
        a = a_scr[pl.ds(r0, 8), :]
        b = u_scr[pl.ds(r0, 8), :]
        for s in (1, 2, 4):
            keep = row >= s
            b = jnp.where(keep, a * pltpu.roll(b, s, 0) + b, b)
            a = jnp.where(keep, a * pltpu.roll(a, s, 0), a)
        hh = a * h + b
        h_scr[pl.ds(r0, 8), :] = hh
        return jnp.broadcast_to(hh[7:8, :], (8, C))

    hprev[...] = lax.fori_loop(0, ts // 8, body, hprev[...])
    out = h_scr[...] * _gelu(gr_ref[0])
    o_ref[0] = (_rms(out) * gw_ref[...]).astype(BF16)


def _rglru(xr, gr, cw, cb, wa_bd, ba, wx_bd, bx, lam, gw, ts=512):
    B, S, C = xr.shape
    ts = min(ts, S)
    vec = pl.BlockSpec((1, C), lambda b, t: (0, 0))
    mat = pl.BlockSpec((C, C), lambda b, t: (0, 0))
    seq = pl.BlockSpec((1, ts, C), lambda b, t: (b, t, 0))
    return pl.pallas_call(
        _rglru_kernel,
        grid=(B, S // ts),
        in_specs=[seq, seq, pl.BlockSpec((CONV_WIDTH, C), lambda b, t: (0, 0)), vec, mat, vec, mat, vec, vec, vec],
        out_specs=seq,
        out_shape=jax.ShapeDtypeStruct((B, S, C), BF16),
        scratch_shapes=[pltpu.VMEM((ts + 8, C), F32), pltpu.VMEM((8, C), F32), pltpu.VMEM((ts, C), F32),
                        pltpu.VMEM((ts, C), F32), pltpu.VMEM((ts, C), F32)],
        compiler_params=_params("parallel", "arbitrary"),
        name="rglru",
    )(xr, gr, cw, cb, wa_bd, ba, wx_bd, bx, lam, gw)


def _to_row_tiles(ref, x):
    rows = x.shape[0]
    for s in range(ROW_TILE):
        ref[pl.ds(s, rows, stride=ROW_TILE), :] = x[:, s * 128:(s + 1) * 128]


def _from_row_tiles(ref, rows, start=0):
    return jnp.concatenate([ref[pl.ds(start + s, rows, stride=ROW_TILE), :] for s in range(ROW_TILE)], axis=1)


def _outproj_kernel(oa_ref, yr_ref, x_ref, mod_ref, gaw_ref, wo_ref, n2_ref, wr_ref, x1_ref, h2_ref, st_ref):
    da = oa_ref.shape[2]
    ya = (_rms(oa_ref[0]) * gaw_ref[...]).astype(BF16)
    mix = _dot(ya, wo_ref[0:da, :]) + _dot(yr_ref[0], wo_ref[da:, :])
    x1 = x_ref[0] + mod_ref[0, 2:3, :] * mix
    x1_ref[0] = x1
    h2 = (_rms(x1) * n2_ref[...]) * (1.0 + mod_ref[0, 4:5, :]) + mod_ref[0, 3:4, :]
    h2_ref[0] = h2.astype(BF16)
    st_ref[...] = _sigmoid(_dot_nt(wr_ref[...], h2, precision=lax.Precision.HIGHEST))


def _outproj(oa, yr, x, mod, gaw, wo, n2, wr_t, tm=256):
    B, S, D = x.shape
    E = wr_t.shape[0]
    nt = S // tm
    row = lambda w: pl.BlockSpec((1, tm, w), lambda b, t: (b, t, 0))
    return pl.pallas_call(
        _outproj_kernel,
        grid=(B, nt),
        in_specs=[row(oa.shape[2]), row(yr.shape[2]), row(D),
                  pl.BlockSpec((1, 6, D), lambda b, t: (b, 0, 0)),
                  pl.BlockSpec((1, oa.shape[2]), lambda b, t: (0, 0)),
                  pl.BlockSpec(wo.shape, lambda b, t: (0, 0)),
                  pl.BlockSpec((1, D), lambda b, t: (0, 0)),
                  pl.BlockSpec((E, D), lambda b, t: (0, 0))],
        out_specs=[row(D), row(D), pl.BlockSpec((E, tm), lambda b, t: (0, b * nt + t))],
        out_shape=[jax.ShapeDtypeStruct((B, S, D), F32), jax.ShapeDtypeStruct((B, S, D), BF16),
                   jax.ShapeDtypeStruct((E, B * S), F32)],
        compiler_params=_params("parallel", "parallel"),
        name="outproj",
    )(oa, yr, x, mod, gaw, wo, n2, wr_t)


def _route_kernel(s_ref, b_ref, eid_ref, wgt_ref, rank_ref, cnt_ref, tcnt_ref, carry):
    E, tn = s_ref.shape
    per = E // N_EXPERT_GROUPS

    @pl.when(pl.program_id(0) == 0)
    def _():
        carry[...] = jnp.zeros(carry.shape, F32)

    s = s_ref[...]
    s_sel = s + b_ref[...]
    eidx = lax.broadcasted_iota(I32, (E, tn), 0)

    grp = []
    for gi in range(N_EXPERT_GROUPS):
        xg = s_sel[gi * per:(gi + 1) * per, :]
        ig = lax.broadcasted_iota(I32, (per, tn), 0)
        m1 = jnp.max(xg, axis=0, keepdims=True)
        f1 = jnp.min(jnp.where(xg == m1, ig, per), axis=0, keepdims=True)
        m2 = jnp.max(jnp.where(ig == f1, -jnp.inf, xg), axis=0, keepdims=True)
        grp.append(m1 + m2)
    val = []
    for gi in range(N_EXPERT_GROUPS):
        rank = jnp.zeros((1, tn), I32)
        for gj in range(N_EXPERT_GROUPS):
            if gj == gi:
                continue
            ahead = (grp[gj] > grp[gi]) | ((grp[gj] == grp[gi]) & (gj < gi))
            rank = rank + ahead.astype(I32)
        val.append(jnp.where(rank < TOP_GROUPS, s_sel[gi * per:(gi + 1) * per, :], -jnp.inf))
    val = jnp.concatenate(val, axis=0)

    eids, tops = [], []
    hot = jnp.zeros((E, tn), F32)
    for _ in range(TOP_K):
        mx = jnp.max(val, axis=0, keepdims=True)
        first = jnp.min(jnp.where(val == mx, eidx, E), axis=0, keepdims=True)
        pick = eidx == first
        eids.append(first)
        tops.append(jnp.sum(jnp.where(pick, s, 0.0), axis=0, keepdims=True))
        hot = jnp.where(pick, 1.0, hot)
        val = jnp.where(pick, -jnp.inf, val)
    denom = tops[0]
    for t in tops[1:]:
        denom = denom + t

    ti = lax.broadcasted_iota(I32, (tn, tn), 0)
    tj = lax.broadcasted_iota(I32, (tn, tn), 1)
    upper = jnp.where(ti < tj, 1.0, 0.0).astype(BF16)
    before = _dot(hot.astype(BF16), upper) + carry[...]
    for k in range(TOP_K):
        eid_ref[k:k + 1, :] = eids[k]
        wgt_ref[k:k + 1, :] = ROUTED_SCALE * tops[k] / denom
        rank_ref[k:k + 1, :] = jnp.sum(jnp.where(eidx == eids[k], before, 0.0), axis=0, keepdims=True).astype(I32)
    tile_count = jnp.sum(hot, axis=1, keepdims=True)
    tcnt_ref[0] = tile_count
    carry[...] = carry[...] + tile_count
    cnt_ref[...] = carry[...]


def _route(s_t, b_col, tn=256):
    E, N = s_t.shape
    tn = min(tn, N)
    out = pl.BlockSpec((TOP_K, tn), lambda t: (0, t))
    return pl.pallas_call(
        _route_kernel,
        grid=(N // tn,),
        in_specs=[pl.BlockSpec((E, tn), lambda t: (0, t)), pl.BlockSpec((E, 1), lambda t: (0, 0))],
        out_specs=[out, out, out, pl.BlockSpec((E, 1), lambda t: (0, 0)), pl.BlockSpec((1, E, 1), lambda t: (t, 0, 0))],
        out_shape=[jax.ShapeDtypeStruct((TOP_K, N), I32), jax.ShapeDtypeStruct((TOP_K, N), F32),
                   jax.ShapeDtypeStruct((TOP_K, N), I32), jax.ShapeDtypeStruct((E, 1), F32),
                   jax.ShapeDtypeStruct((N // tn, E, 1), F32)],
        scratch_shapes=[pltpu.VMEM((E, 1), F32)],
        compiler_params=_params("arbitrary"),
        name="route",
    )(s_t, b_col)


def _tilepos_kernel(eid_ref, rank_ref, base_ref, o_ref):
    E = base_ref.shape[1]
    tn = eid_ref.shape[1]
    eidx = lax.broadcasted_iota(I32, (E, tn), 0)
    for k in range(TOP_K):
        base = jnp.sum(jnp.where(eidx == eid_ref[k:k + 1, :], base_ref[0], 0.0), axis=0, keepdims=True)
        o_ref[k:k + 1, :] = base.astype(I32) + rank_ref[k:k + 1, :]


def _tilepos(eid_t, rank_t, base, tn):
    K, N = eid_t.shape
    blk = pl.BlockSpec((K, tn), lambda t: (0, t))
    return pl.pallas_call(
        _tilepos_kernel,
        grid=(N // tn,),
        in_specs=[blk, blk, pl.BlockSpec((1,) + base.shape[1:], lambda t: (t, 0, 0))],
        out_specs=blk,
        out_shape=jax.ShapeDtypeStruct((K, N), I32),
        compiler_params=_params("parallel"),
        name="tilepos",
    )(eid_t, rank_t, base)


_HI = -65536


def _pack_pairs(even_rows, odd_rows):
    hi = lax.bitcast_convert_type(even_rows, I32) & _HI
    lo = lax.shift_right_logical(lax.bitcast_convert_type(odd_rows, I32), 16)
    return hi | lo


def _unpack_pairs(words):
    even_rows = lax.bitcast_convert_type(words & _HI, F32)
    odd_rows = lax.bitcast_convert_type(lax.shift_left(words, 16), F32)
    return even_rows, odd_rows


def _run_copies(src_ref, cnt_ref, dst_ref, n_runs, copy):
    def go(e, carry):
        n = cnt_ref[0, 0, e]

        @pl.when(n > 0)
        def _():
            size = pl.multiple_of(n * ROW_TILE, ROW_TILE)
            copy(pl.ds(pl.multiple_of(src_ref[0, 0, e] * ROW_TILE, ROW_TILE), size),
                 pl.ds(pl.multiple_of(dst_ref[0, 0, e] * ROW_TILE, ROW_TILE), size)).start()
        return carry
    lax.fori_loop(0, n_runs, go, 0)


def _pair_onehots(pos_rows, n_pairs, first_pair, width):
    even = 2 * (lax.broadcasted_iota(I32, (n_pairs, width), 0) + first_pair)
    hot_e = jnp.zeros((n_pairs, width), F32)
    hot_o = jnp.zeros((n_pairs, width), F32)
    for k in range(TOP_K):
        hot_e = jnp.where(even == pos_rows[k:k + 1, :], 1.0, hot_e)
        hot_o = jnp.where(even + 1 == pos_rows[k:k + 1, :], 1.0, hot_o)
    return hot_e, hot_o


def _dispatch_kernel(src_ref, cnt_ref, dst_ref, pos_ref, h_ref, xs_hbm, xbuf, sem):
    n_pairs = xbuf.shape[0] // ROW_TILE
    tn = h_ref.shape[0]
    hb = h_ref[...]
    slab = n_pairs // 3
    for sl in range(3):
        hot_e, hot_o = _pair_onehots(pos_ref[...], slab, sl * slab, tn)
        words = _pack_pairs(_dot(hot_e.astype(BF16), hb), _dot(hot_o.astype(BF16), hb))
        for s in range(ROW_TILE):
            xbuf[pl.ds(sl * slab * ROW_TILE + s, slab, stride=ROW_TILE), :] = words[:, s * 128:(s + 1) * 128]
    _run_copies(src_ref, cnt_ref, dst_ref, cnt_ref.shape[2],
                lambda src, dst: pltpu.make_async_copy(xbuf.at[src], xs_hbm.at[dst], sem.at[0]))
    pltpu.make_async_copy(xbuf, xs_hbm.at[pl.ds(0, xbuf.shape[0])], sem.at[0]).wait()


def _run_specs(runs):
    return [pl.BlockSpec((1, 1, r.shape[2]), lambda t: (t, 0, 0), memory_space=pltpu.SMEM) for r in runs]


def _tile_pairs(tn, n_experts):
    return (TOP_K * tn + n_experts) // 2


def _dispatch(runs, pos_t, h2, tn):
    N, D = h2.shape
    W = D // ROW_TILE
    n_pairs = _tile_pairs(tn, runs[0].shape[2] - 1)
    assert n_pairs % 24 == 0
    return pl.pallas_call(
        _dispatch_kernel,
        grid=(N // tn,),
        in_specs=_run_specs(runs) + [pl.BlockSpec((TOP_K, tn), lambda t: (0, t)),
                                     pl.BlockSpec((tn, D), lambda t: (t, 0))],
        out_specs=pl.BlockSpec(memory_space=pl.ANY),
        out_shape=jax.ShapeDtypeStruct((N // tn * n_pairs * ROW_TILE, W), I32),
        scratch_shapes=[pltpu.VMEM((n_pairs * ROW_TILE, W), I32), pltpu.SemaphoreType.DMA((1,))],
        compiler_params=_params("arbitrary"),
        name="dispatch",
    )(*runs, pos_t, h2)


def _experts_kernel(first_ref, size_ref, e_ref, new_ref, next_ref, wslot_ref, total_ref,
                    w1_hbm, w3_hbm, w2_hbm, xs_hbm, ys_hbm,
                    xbuf, ybuf, w1f, w3f, w2f, w1b, w3b, w2b, xsem, ysem, wsem):
    total = total_ref[0]

    def w_copies(e, slot):
        return [pltpu.make_async_copy(src.at[e], dst.at[slot], wsem.at[slot])
                for src, dst in ((w1_hbm, w1f), (w3_hbm, w3f), (w2_hbm, w2f))]

    def rows(c):
        return (pl.multiple_of(first_ref[c] * ROW_TILE, ROW_TILE), pl.multiple_of(size_ref[c] * ROW_TILE, ROW_TILE))

    def x_copy(c, slot):
        first, size = rows(c)
        return pltpu.make_async_copy(xs_hbm.at[pl.ds(first, size)], xbuf.at[slot, pl.ds(0, size)], xsem.at[slot])

    def y_copy(c, slot):
        first, size = rows(c)
        return pltpu.make_async_copy(ybuf.at[slot, pl.ds(0, size)], ys_hbm.at[pl.ds(first, size)], ysem.at[slot])

    xbuf[...] = jnp.zeros(xbuf.shape, I32)
    for cp in w_copies(e_ref[0], 0):
        cp.start()
    x_copy(0, 0).start()
    half = MOE_BLOCK // 2

    def body(c, carry):
        slot = c % 2

        @pl.when(new_ref[c] == 1)
        def _():
            ws = wslot_ref[c]
            for cp in w_copies(e_ref[c], ws):
                cp.wait()
            w1b[...] = w1f[ws].astype(BF16)
            w3b[...] = w3f[ws].astype(BF16)
            w2b[...] = w2f[ws].astype(BF16)

            @pl.when(next_ref[c] >= 0)
            def _():
                for cp in w_copies(next_ref[c], 1 - ws):
                    cp.start()

        x_copy(c, slot).wait()

        @pl.when(c + 1 < total)
        def _():
            x_copy(c + 1, 1 - slot).start()

        x = jnp.concatenate(_unpack_pairs(_from_row_tiles(xbuf.at[slot], half)), axis=0).astype(BF16)
        h1 = _dot(x, w1b[...])
        h3 = _dot(x, w3b[...])
        hid = (h1 * _sigmoid(h1)) * h3
        y = _dot(hid.astype(BF16), w2b[...]).astype(BF16).astype(F32)

        @pl.when(c >= 2)
        def _():
            y_copy(c - 2, slot).wait()

        _to_row_tiles(ybuf.at[slot], _pack_pairs(y[0:half], y[half:]))
        y_copy(c, slot).start()
        return carry

    lax.fori_loop(0, total, body, 0)

    @pl.when(total >= 2)
    def _():
        y_copy(total - 2, total % 2).wait()

    y_copy(total - 1, (total - 1) % 2).wait()

    ybuf[0] = jnp.zeros(ybuf.shape[1:], I32)
    tail_first = total_ref[1]
    tail_pairs = ys_hbm.shape[0] // ROW_TILE - tail_first

    def fill(j, carry):
        size = pl.multiple_of(jnp.minimum(half, tail_pairs - j * half) * ROW_TILE, ROW_TILE)
        cp = pltpu.make_async_copy(ybuf.at[0, pl.ds(0, size)],
                                   ys_hbm.at[pl.ds(pl.multiple_of((tail_first + j * half) * ROW_TILE, ROW_TILE), size)],
                                   ysem.at[0])
        cp.start()
        cp.wait()
        return carry

    lax.fori_loop(0, (tail_pairs + half - 1) // half, fill, 0)


def _expert_chunks(counts, n_pairs_total):
    E = counts.shape[0]
    block = MOE_BLOCK // 2
    n_max = n_pairs_total // block + E
    starts = jnp.cumsum(counts) - counts
    n_chunks = (counts + block - 1) // block
    c_end = jnp.cumsum(n_chunks)
    c = jnp.arange(n_max, dtype=I32)
    e = jnp.minimum(jnp.sum(c_end[None, :] <= c[:, None], axis=1), E - 1).astype(I32)
    j = c - (c_end - n_chunks)[e]
    size = jnp.clip(counts[e] - j * block, 0, block)
    has_rows = counts > 0
    ids = jnp.where(has_rows, jnp.arange(E, dtype=I32), E)
    later = jnp.flip(lax.cummin(jnp.flip(ids)))
    nxt = jnp.concatenate([later[1:], jnp.full((1,), E, I32)])
    nxt = jnp.where(nxt >= E, -1, nxt)
    ordinal = jnp.cumsum(has_rows.astype(I32)) - 1
    to_i32 = lambda a: a.astype(I32)
    meta = jnp.stack([c_end[-1], jnp.sum(counts)])
    return tuple(map(to_i32, (starts[e] + j * block, size, e, j == 0, nxt[e], ordinal[e] % 2, meta)))


def _experts(chunks, xs, w1, w3, w2):
    rows_total, W = xs.shape
    _, D, De = w1.shape
    any_space = pl.BlockSpec(memory_space=pl.ANY)
    block_rows = MOE_BLOCK // 2 * ROW_TILE
    grid_spec = pltpu.PrefetchScalarGridSpec(
        num_scalar_prefetch=len(chunks),
        grid=(1,),
        in_specs=[any_space] * 4,
        out_specs=any_space,
        scratch_shapes=[pltpu.VMEM((2, block_rows, W), I32), pltpu.VMEM((2, block_rows, W), I32),
                        pltpu.VMEM((2, D, De), F32), pltpu.VMEM((2, D, De), F32), pltpu.VMEM((2, De, D), F32),
                        pltpu.VMEM((D, De), BF16), pltpu.VMEM((D, De), BF16), pltpu.VMEM((De, D), BF16),
                        pltpu.SemaphoreType.DMA((2,)), pltpu.SemaphoreType.DMA((2,)), pltpu.SemaphoreType.DMA((2,))],
    )
    return pl.pallas_call(
        _experts_kernel,
        grid_spec=grid_spec,
        out_shape=jax.ShapeDtypeStruct((rows_total, W), I32),
        compiler_params=_params("arbitrary"),
        name="experts",
    )(*chunks, w1, w3, w2, xs)


def _combine_kernel(src_ref, cnt_ref, dst_ref, ys_hbm, pos_ref, post_ref, wt_ref, h2_ref, x1_ref, mod_ref, ws1_ref,
                    ws3_ref, ws2_ref, fw_ref, o_ref, ybuf, sem):
    tm = h2_ref.shape[0]
    n_pairs = ybuf.shape[0] // ROW_TILE
    n_exp = cnt_ref.shape[2] - 1

    @pl.when(pl.program_id(0) == 0)
    def _():
        ybuf[...] = jnp.zeros(ybuf.shape, I32)

    _run_copies(src_ref, cnt_ref, dst_ref, n_exp,
                lambda src, dst: pltpu.make_async_copy(ys_hbm.at[dst], ybuf.at[src], sem.at[0]))

    h = h2_ref[...]
    a = _dot(h, ws1_ref[...])
    hid = (a * _sigmoid(a)) * _dot(h, ws3_ref[...])
    shared = _dot(hid.astype(BF16), ws2_ref[...])

    pair = 2 * lax.broadcasted_iota(I32, (n_pairs, tm), 0)
    lane = 2 * lax.broadcasted_iota(I32, (tm, n_pairs), 1)
    w_cols, mine = [], []
    for odd in range(2):
        w_rows = jnp.zeros((n_pairs, tm), F32)
        hot = jnp.zeros((tm, n_pairs), F32)
        for k in range(TOP_K):
            w_rows = jnp.where(pair + odd == post_ref[k:k + 1, :], wt_ref[k:k + 1, :], w_rows)
            hot = jnp.where(lane + odd == pos_ref[:, k:k + 1], 1.0, hot)
        w_cols.append(jnp.sum(w_rows, axis=1, keepdims=True))
        mine.append(hot.astype(BF16))
    used = pl.multiple_of((src_ref[0, 0, n_exp]) * ROW_TILE, ROW_TILE)
    pltpu.make_async_copy(ys_hbm.at[pl.ds(0, used)], ybuf.at[pl.ds(0, used)], sem.at[0]).wait()
    y_even, y_odd = _unpack_pairs(_from_row_tiles(ybuf, n_pairs))
    routed = (_dot(mine[0], (y_even * w_cols[0]).astype(BF16)) + _dot(mine[1], (y_odd * w_cols[1]).astype(BF16)))
    x2 = x1_ref[...] + mod_ref[0, 5:6, :] * (routed + shared)
    o_ref[...] = _rms(x2) * fw_ref[...]


def _combine(runs, ys, pos, pos_t, wgt_t, h2, x1, mod, ws1, ws3, ws2, fw, seq, tm):
    N, D = h2.shape
    nt = seq // tm
    row = pl.BlockSpec((tm, D), lambda t: (t, 0))
    full = lambda a: pl.BlockSpec(a.shape, lambda t: (0,) * a.ndim)
    return pl.pallas_call(
        _combine_kernel,
        grid=(N // tm,),
        in_specs=_run_specs(runs) + [pl.BlockSpec(memory_space=pl.ANY),
                                     pl.BlockSpec((tm, TOP_K), lambda t: (t, 0)),
                                     pl.BlockSpec((TOP_K, tm), lambda t: (0, t)),
                                     pl.BlockSpec((TOP_K, tm), lambda t: (0, t)),
                                     row, row,
                                     pl.BlockSpec((1, 6, D), lambda t: (t // nt, 0, 0)),
                                     full(ws1), full(ws3), full(ws2), full(fw)],
        out_specs=row,
        out_shape=jax.ShapeDtypeStruct((N, D), F32),
        scratch_shapes=[pltpu.VMEM((_tile_pairs(tm, runs[0].shape[2] - 1) * ROW_TILE, D // ROW_TILE), I32),
                        pltpu.SemaphoreType.DMA((1,))],
        compiler_params=_params("arbitrary"),
        name="combine",
    )(*runs, ys, pos, pos_t, wgt_t, h2, x1, mod, ws1, ws3, ws2, fw)


def _t5_bucket(dist):
    n = jnp.maximum(dist, 0)
    max_exact = REL_BUCKETS // 2
    nf = jnp.maximum(n, 1).astype(F32)
    large = max_exact + (jnp.log(nf / max_exact) / math.log(REL_MAX_DIST / max_exact)
                         * (REL_BUCKETS - max_exact)).astype(I32)
    large = jnp.minimum(large, REL_BUCKETS - 1)
    return jnp.where(n < max_exact, n, large)


def _bias_kernel(tbl_ref, far_ref, bt_ref, bw_ref, t_ref, w_ref):
    g = pl.program_id(0)
    d = pl.program_id(1)

    def lookup(bkt, head):
        out = jnp.full(bkt.shape, tbl_ref[0, head], F32)
        for b in range(1, REL_BUCKETS):
            out = jnp.where(bkt == b, tbl_ref[b, head], out)
        return out - far_ref[head]

    j = lax.broadcasted_iota(I32, (QB, QB), 0)
    i = lax.broadcasted_iota(I32, (QB, QB), 1)
    keep = ((d != 0) | (i >= j)) & (d != TILE_MASKED) & ((d != TILE_WINDOW_EDGE) | (j > i))
    for hq in range(GQA):
        t_ref[0, 0, :, hq * QB:(hq + 1) * QB] = jnp.where(keep, lookup(bt_ref[0], GQA * g + hq), NEG)

    @pl.when(d == 0)
    def _():
        for hq in range(GQA):
            w_ref[0, :, hq * QB:(hq + 1) * QB] = lookup(bw_ref[...], GQA * g + hq)


def _bias_tables(rel_table):
    tbl = rel_table.astype(F32) * LOG2E
    far_hi = tbl[REL_BUCKETS - 1].astype(BF16)
    far_lo = (tbl[REL_BUCKETS - 1] - far_hi.astype(F32)).astype(BF16)
    far = far_hi.astype(F32) + far_lo.astype(F32)
    q_tail = jnp.zeros((N_HEADS, HEAD_DIM), F32).at[:, 0].set(far_hi.astype(F32)).at[:, 1].set(far_lo.astype(F32))
    i = jnp.arange(QB)
    tile_dist = np.array(list(range(N_BIAS_TILES)) + [0, WINDOW // QB])
    d_t = (QB * jnp.asarray(tile_dist)[:, None, None] + i[None, None, :] - i[None, :, None])
    d_w = i[None, :] - CMP_STRIDE * (jnp.arange(CMP_WIN)[:, None] - CMP_PAD) - (CMP_LEN - 1)
    n_tiles = len(tile_dist)
    t_tab, w_tab = pl.pallas_call(
        _bias_kernel,
        grid=(N_KV, n_tiles),
        in_specs=[pl.BlockSpec(memory_space=pltpu.SMEM),
                  pl.BlockSpec(memory_space=pltpu.SMEM),
                  pl.BlockSpec((1, QB, QB), lambda g, d: (d, 0, 0)),
                  pl.BlockSpec((CMP_WIN, QB), lambda g, d: (0, 0))],
        out_specs=[pl.BlockSpec((1, 1, QB, GQA * QB), lambda g, d: (g, d, 0, 0)),
                   pl.BlockSpec((1, CMP_WIN, GQA * QB), lambda g, d: (g, 0, 0))],
        out_shape=[jax.ShapeDtypeStruct((N_KV, n_tiles, QB, GQA * QB), F32),
                   jax.ShapeDtypeStruct((N_KV, CMP_WIN, GQA * QB), F32)],
        compiler_params=_params("parallel", "arbitrary"),
        name="bias",
    )(tbl, far, _t5_bucket(d_t), _t5_bucket(d_w))
    far_min = min(FAR_TILE_DIST * QB - QB + 1, CMP_STRIDE * (CMP_PAD + 1) - (CMP_LEN - 1))
    max_exact = REL_BUCKETS // 2
    assert math.log(far_min / max_exact) / math.log(REL_MAX_DIST / max_exact) * max_exact > max_exact - 0.75
    return t_tab, w_tab, q_tail


def _overlap_tables(nc, nb):
    c_start = CMP_STRIDE * np.arange(nc)[:, None]
    s_start = SLC_LEN * np.arange(nb)[None, :]
    ov = ((c_start < s_start + SLC_LEN) & (c_start + CMP_LEN > s_start)).astype(np.float32)
    ov[nc - 1:] = 0.0
    out = np.zeros((nc + CMP_WIN, nb), np.float32)
    out[CMP_PAD:CMP_PAD + nc] = ov
    return jnp.asarray(ov.T, dtype=BF16), jnp.asarray(out)


def _block_diag(w):
    nblk, bs, _ = w.shape
    eye = jnp.eye(nblk, dtype=w.dtype)
    return (eye[:, None, :, None] * w[:, :, None, :]).reshape(nblk * bs, nblk * bs)


def _layer(x, c8, w_ada, b_ada, norm1_w, w_in, cmp_pos, cmp_k_w1, cmp_k_w2, cmp_v_w1, cmp_v_w2, rel_table,
           conv_w, conv_b, lru_wa, lru_ba, lru_wx, lru_bx, lru_lambda, gnorm_attn_w, gnorm_rnn_w, w_out,
           norm2_w, w_router, b_router, w1, w3, w2, ws1, ws3, ws2, final_norm_w):
    B, S, D = x.shape
    N = B * S
    E = w_router.shape[1]
    d_attn = N_HEADS * HEAD_DIM
    d_rnn = D - d_attn
    assert S % 512 == 0 and S // SLC_LEN <= 128

    mod = _ada(c8, w_ada, b_ada[None, :])[:B].reshape(B, 6, D)

    n_kv_cols = 6 * N_KV * HEAD_DIM
    n_gate = N_BRANCH * N_HEADS
    wq, wkv, wg, wxr, wgr = jnp.split(w_in, np.cumsum([d_attn, n_kv_cols, n_gate, d_rnn]).tolist(), axis=1)
    wg = jnp.pad(wg.reshape(D, N_KV, GQA * N_BRANCH), ((0, 0), (0, 0), (0, 128 - GQA * N_BRANCH))).reshape(D, 256)
    w_pad = jnp.concatenate([wq, wkv, wg, wxr, wgr], axis=1).astype(BF16)
    t_tab, w_tab, q_tail = _bias_tables(rel_table)
    q_t, kx, vx_t, kvc, gates_t, xr, gr = _inproj(x, mod, norm1_w[None, :], w_pad, q_tail)

    nc = S // CMP_STRIDE
    kvc16 = kvc.reshape(B, 4, nc, CMP_STRIDE * HEAD_DIM)
    pos2 = cmp_pos.reshape(2, CMP_STRIDE * HEAD_DIM)
    kvc_pad, kvc_t = _compress(kvc16, pos2, jnp.stack([cmp_k_w1, cmp_v_w1]), jnp.stack([cmp_k_w2, cmp_v_w2]))
    o_attn = _attn(q_t, kx, vx_t, kvc_pad, kvc_t, gates_t, t_tab, w_tab, *_overlap_tables(nc, 128))

    y_rnn = _rglru(xr, gr, conv_w.reshape(CONV_WIDTH, d_rnn), conv_b[None, :], _block_diag(lru_wa).astype(BF16),
                   lru_ba[None, :], _block_diag(lru_wx).astype(BF16), lru_bx[None, :], lru_lambda[None, :],
                   gnorm_rnn_w[None, :])

    x1, h2, s_t = _outproj(o_attn, y_rnn, x, mod, gnorm_attn_w[None, :], w_out.astype(BF16), norm2_w[None, :],
                           w_router.T)

    tile = min(MOE_TILE, S)
    eid_t, wgt_t, rank_t, counts, tile_counts = _route(s_t, b_router[:, None], tile)
    del counts
    tcnt = tile_counts[:, :, 0].astype(I32)
    n_tiles = tcnt.shape[0]
    tile_pairs = _tile_pairs(tile, E)
    rpairs = (tcnt + 1) // 2
    rpairs = jnp.concatenate([rpairs, tile_pairs - jnp.sum(rpairs, axis=1, keepdims=True)], axis=1)
    per_expert = jnp.sum(rpairs, axis=0)
    off = jnp.cumsum(per_expert) - per_expert
    run_src = jnp.cumsum(rpairs, axis=1) - rpairs
    run_dst = off[None, :] + jnp.cumsum(rpairs, axis=0) - rpairs
    runs = tuple(r.astype(I32)[:, None, :] for r in (run_src, rpairs, run_dst))
    earlier = jnp.cumsum(tcnt, axis=0) - tcnt
    pos_t = _tilepos(eid_t, rank_t, (2 * run_src[:, :E] - earlier).astype(F32)[:, :, None], tile)

    h2f = h2.reshape(N, D)
    xs = _dispatch(runs, pos_t, h2f, tile)
    ys = _experts(_expert_chunks(per_expert[:E], n_tiles * tile_pairs), xs, w1, w3, w2)
    out = _combine(runs, ys, pos_t.T, pos_t, wgt_t, h2f, x1.reshape(N, D), mod, ws1.astype(BF16), ws3.astype(BF16),
                   ws2.astype(BF16), final_norm_w[None, :], S, tile)
    return out.reshape(B, S, D)


def kernel(x, c, w_ada, b_ada, norm1_w, w_in, cmp_pos, cmp_k_w1, cmp_k_w2, cmp_v_w1, cmp_v_w2, rel_table, conv_w, conv_b, lru_wa, lru_ba, lru_wx, lru_bx, lru_lambda, gnorm_attn_w, gnorm_rnn_w, w_out, norm2_w, w_router, b_router, w1, w3, w2, ws1, ws3, ws2, final_norm_w):
    assert w_ada.shape[0] == 1
    c8 = jnp.pad(c, ((0, 8 - c.shape[0]), (0, 0)))
    return _layer(x, c8, w_ada[0], b_ada[0], norm1_w[0], w_in[0], cmp_pos[0], cmp_k_w1[0], cmp_k_w2[0],
                  cmp_v_w1[0], cmp_v_w2[0], rel_table, conv_w[0], conv_b[0], lru_wa[0], lru_ba[0], lru_wx[0],
                  lru_bx[0], lru_lambda[0], gnorm_attn_w[0], gnorm_rnn_w[0], w_out[0], norm2_w[0], w_router[0],
                  b_router[0], w1[0], w3[0], w2[0], ws1[0], ws3[0], ws2[0], final_norm_w)
```

```python
import functools
import math

import jax
import jax.numpy as jnp
import numpy as np
from jax import lax
from jax.experimental import pallas as pl
from jax.experimental.pallas import tpu as pltpu

F32 = jnp.float32
BF16 = jnp.bfloat16
I32 = jnp.int32

HEAD_DIM = 64
N_HEADS = 8
N_KV = 2
GQA = N_HEADS // N_KV
N_BRANCH = 3
CONV_WIDTH = 4
LRU_C = 8.0
CMP_LEN = 32
CMP_STRIDE = 16
SLC_LEN = 64
SLC_TOP = 16
WINDOW = 512
QB = 128
REL_BUCKETS = 32
REL_MAX_DIST = 1024
N_EXPERT_GROUPS = 8
TOP_GROUPS = 4
TOP_K = 8
ROUTED_SCALE = 2.5
MOE_BLOCK = 576
MOE_TILE = 256
ROW_TILE = 8
EPS = 1e-6
NEG = -1e30
LOG2E = math.log2(math.e)
CMP_PAD = 120
CMP_WIN = 128
N_BIAS_TILES = 11
TILE_MASKED = N_BIAS_TILES
TILE_WINDOW_EDGE = N_BIAS_TILES + 1
FAR_TILE_DIST = 8
FAR_CHUNK = 4
VMEM_LIMIT = 52 * 1024 * 1024


def _dot(a, b, **kw):
    return jnp.dot(a, b, preferred_element_type=F32, **kw)


def _dot_nt(a, b, **kw):
    return lax.dot_general(a, b, (((1,), (1,)), ((), ())), preferred_element_type=F32, **kw)


def _gelu(x):
    return 0.5 * x * (1.0 + jnp.tanh(math.sqrt(2.0 / math.pi) * (x + 0.044715 * (x * x * x))))


def _sigmoid(x):
    return 1.0 / (1.0 + jnp.exp(-x))


def _rms(x):
    return x * lax.rsqrt(jnp.mean(x * x, axis=-1, keepdims=True) + EPS)


def _params(*sem):
    return pltpu.CompilerParams(dimension_semantics=sem, vmem_limit_bytes=VMEM_LIMIT)


def _ada_kernel(c_ref, w_ref, b_ref, o_ref):
    c = c_ref[...]
    a = c * _sigmoid(c)
    o_ref[...] = _dot(a, w_ref[...], precision=lax.Precision.HIGHEST) + b_ref[...]


def _ada(c8, w, b):
    d, n = w.shape
    tn = 1536
    return pl.pallas_call(
        _ada_kernel,
        grid=(n // tn,),
        in_specs=[pl.BlockSpec((8, d), lambda j: (0, 0)),
                  pl.BlockSpec((d, tn), lambda j: (0, j)),
                  pl.BlockSpec((1, tn), lambda j: (0, j))],
        out_specs=pl.BlockSpec((8, tn), lambda j: (0, j)),
        out_shape=jax.ShapeDtypeStruct((8, n), F32),
        compiler_params=_params("parallel"),
        name="ada",
    )(c8, w, b)


def _inproj_kernel(x_ref, mod_ref, nw_ref, w_ref, qt_ref, q_ref, kx_ref, vx_ref, kvc_ref, g_ref, xr_ref, gr_ref):
    h = _rms(x_ref[0]) * nw_ref[...]
    h = h * (1.0 + mod_ref[0, 1:2, :]) + mod_ref[0, 0:1, :]
    p = _dot(h.astype(BF16), w_ref[...])
    tm = p.shape[0]
    dq = N_HEADS * HEAD_DIM
    lane = lax.broadcasted_iota(I32, (tm, HEAD_DIM), 1)
    k_tail = jnp.where(lane < 2, 1.0, 0.0)
    v_tail = jnp.where(lane < 1, 1.0, 0.0)
    for hh in range(N_HEADS):
        qh = p[:, hh * HEAD_DIM:(hh + 1) * HEAD_DIM] * (HEAD_DIM ** -0.5 * LOG2E)
        q_tail = jnp.broadcast_to(qt_ref[hh:hh + 1, :], (tm, HEAD_DIM))
        q_t = jnp.concatenate([qh, q_tail], axis=1).T.astype(BF16)
        for blk in range(tm // QB):
            col = (blk * GQA + hh % GQA) * QB
            q_ref[0, hh // GQA, :, col:col + QB] = q_t[:, blk * QB:(blk + 1) * QB]
    for j in range(6):
        for g in range(N_KV):
            col = dq + (j * N_KV + g) * HEAD_DIM
            piece = p[:, col:col + HEAD_DIM]
            if j < 2:
                kvc_ref[0, j * N_KV + g] = piece
            elif j % 2 == 0:
                kx_ref[0, (j // 2 - 1) * N_KV + g] = jnp.concatenate([piece, k_tail], axis=1).astype(BF16)
            else:
                v_t = jnp.concatenate([piece, v_tail], axis=1).T.astype(BF16)
                for blk in range(tm // QB):
                    vx_ref[0, (j // 2 - 1) * N_KV + g, blk] = v_t[:, blk * QB:(blk + 1) * QB]
    c0 = dq + 6 * N_KV * HEAD_DIM
    for g in range(N_KV):
        g_ref[0, g] = _sigmoid(p[:, c0 + g * 128:c0 + (g + 1) * 128]).T[0:16, :]
    xr_ref[0] = p[:, c0 + 256:c0 + 768]
    gr_ref[0] = p[:, c0 + 768:c0 + 1280]


def _inproj(x, mod, nw, w_pad, q_tail, tm=256):
    B, S, D = x.shape
    ncol = w_pad.shape[1]
    heads = lambda n, w: pl.BlockSpec((1, n, tm, w), lambda b, t: (b, 0, t, 0))
    return pl.pallas_call(
        _inproj_kernel,
        grid=(B, S // tm),
        in_specs=[pl.BlockSpec((1, tm, D), lambda b, t: (b, t, 0)),
                  pl.BlockSpec((1, 6, D), lambda b, t: (b, 0, 0)),
                  pl.BlockSpec((1, D), lambda b, t: (0, 0)),
                  pl.BlockSpec((D, ncol), lambda b, t: (0, 0)),
                  pl.BlockSpec(q_tail.shape, lambda b, t: (0, 0))],
        out_specs=[pl.BlockSpec((1, N_KV, 2 * HEAD_DIM, tm * GQA), lambda b, t: (b, 0, 0, t)),
                   heads(4, 2 * HEAD_DIM),
                   pl.BlockSpec((1, 4, tm // QB, 2 * HEAD_DIM, QB), lambda b, t: (b, 0, t, 0, 0)),
                   heads(4, HEAD_DIM),
                   pl.BlockSpec((1, N_KV, 16, tm), lambda b, t: (b, 0, 0, t)),
                   pl.BlockSpec((1, tm, 512), lambda b, t: (b, t, 0)),
                   pl.BlockSpec((1, tm, 512), lambda b, t: (b, t, 0))],
        out_shape=[jax.ShapeDtypeStruct((B, N_KV, 2 * HEAD_DIM, S * GQA), BF16),
                   jax.ShapeDtypeStruct((B, 4, S, 2 * HEAD_DIM), BF16),
                   jax.ShapeDtypeStruct((B, 4, S // QB, 2 * HEAD_DIM, QB), BF16),
                   jax.ShapeDtypeStruct((B, 4, S, HEAD_DIM), F32),
                   jax.ShapeDtypeStruct((B, N_KV, 16, S), F32),
                   jax.ShapeDtypeStruct((B, S, 512), F32),
                   jax.ShapeDtypeStruct((B, S, 512), F32)],
        compiler_params=_params("parallel", "parallel"),
        name="inproj",
    )(x, mod, nw, w_pad, q_tail)


def _compress_kernel(x_ref, pos_ref, w1_ref, w2_ref, o_ref, ot_ref):
    x = x_ref[0, 0]
    nc = x.shape[0]
    half = CMP_STRIDE * HEAD_DIM
    kv = pl.program_id(1) // N_KV
    a = _dot((x + pos_ref[0:1, :]).astype(BF16), w1_ref[kv, 0:half, :].astype(BF16))
    b = _dot((x + pos_ref[1:2, :]).astype(BF16), w1_ref[kv, half:2 * half, :].astype(BF16))
    hid = _gelu(a + pltpu.roll(b, nc - 1, 0))
    out = _dot(hid.astype(BF16), w2_ref[kv].astype(BF16))
    row = lax.broadcasted_iota(I32, out.shape, 0)
    out = jnp.where(row < nc - 1, out, 0.0)
    lane = lax.broadcasted_iota(I32, (nc, HEAD_DIM), 1)
    out = jnp.concatenate([out, jnp.where(lane < 2, 1.0, 0.0)], axis=1)
    o_ref[0, 0, 0:CMP_PAD, :] = jnp.zeros((CMP_PAD, 2 * HEAD_DIM), F32)
    o_ref[0, 0, CMP_PAD:CMP_PAD + nc, :] = out
    o_ref[0, 0, CMP_PAD + nc:, :] = jnp.zeros((CMP_WIN - CMP_PAD, 2 * HEAD_DIM), F32)
    ot_ref[0, 0] = out.T


def _compress(kvc, pos2, w1, w2):
    B, _, NC, W = kvc.shape
    return pl.pallas_call(
        _compress_kernel,
        grid=(B, 4),
        in_specs=[pl.BlockSpec((1, 1, NC, W), lambda b, i: (b, i, 0, 0)),
                  pl.BlockSpec((2, W), lambda b, i: (0, 0)),
                  pl.BlockSpec((2, 2 * W, HEAD_DIM), lambda b, i: (0, 0, 0)),
                  pl.BlockSpec((2, HEAD_DIM, HEAD_DIM), lambda b, i: (0, 0, 0))],
        out_specs=[pl.BlockSpec((1, 1, NC + CMP_WIN, 2 * HEAD_DIM), lambda b, i: (b, i, 0, 0)),
                   pl.BlockSpec((1, 1, 2 * HEAD_DIM, NC), lambda b, i: (b, i, 0, 0))],
        out_shape=[jax.ShapeDtypeStruct((B, 4, NC + CMP_WIN, 2 * HEAD_DIM), F32),
                   jax.ShapeDtypeStruct((B, 4, 2 * HEAD_DIM, NC), F32)],
        compiler_params=_params("parallel", "parallel"),
        name="compress",
    )(kvc, pos2, w1, w2)


def _attn_kernel(q_ref, ks_ref, vs_ref, kw_ref, vw_ref, kc_ref, vc_ref, vct_ref, g_ref, t_ref, w_ref, ovt_ref, ov_ref,
                 o_ref, m_scr, acc_scr, sel_scr, sa_scr, sb_scr):
    qb = pl.program_id(2)
    R = GQA * QB
    q = q_ref[0, 0]
    nc = vct_ref.shape[3]
    nb = ov_ref.shape[1]
    n_top = min(SLC_TOP, ks_ref.shape[2] // SLC_LEN)

    def heads_sum(x):
        out = x[:, 0:QB]
        for hq in range(1, GQA):
            out = out + x[:, hq * QB:(hq + 1) * QB]
        return out

    w0 = pl.multiple_of(qb * 8, 8)
    s_far = _dot(kc_ref[0, 0, CMP_PAD:CMP_PAD + nc, :].astype(BF16), q)
    n_io = lax.broadcasted_iota(I32, (nc, R), 0)
    s_far = jnp.where(n_io < qb * 8 - CMP_PAD, s_far, -jnp.inf)
    s_win = _dot(kc_ref[0, 0, pl.ds(w0, CMP_WIN), :].astype(BF16), q) + w_ref[0]
    j_w = lax.broadcasted_iota(I32, (CMP_WIN, R), 0)
    i_w = lax.broadcasted_iota(I32, (CMP_WIN, R), 1) & (QB - 1)
    dist_w = i_w - CMP_STRIDE * (j_w - CMP_PAD) - (CMP_LEN - 1)
    s_win = jnp.where((dist_w >= 0) & (j_w >= CMP_PAD - qb * 8), s_win, -jnp.inf)
    m = jnp.maximum(jnp.max(s_far, axis=0, keepdims=True), jnp.max(s_win, axis=0, keepdims=True))
    m = jnp.where(m == -jnp.inf, 0.0, m)
    e_far = jnp.exp2(s_far - m)
    e_win = jnp.exp2(s_win - m)
    l = jnp.sum(e_far, axis=0, keepdims=True) + jnp.sum(e_win, axis=0, keepdims=True)
    inv = 1.0 / jnp.maximum(l, 1e-30)
    p_far = e_far * inv
    p_win = e_win * inv
    vcw_t = vc_ref[0, 0, pl.ds(w0, CMP_WIN), :].T
    o_c = _dot(vct_ref[0, 0].astype(BF16), p_far.astype(BF16)) + _dot(vcw_t.astype(BF16), p_win.astype(BF16))
    ov_win_t = ov_ref[pl.ds(w0, CMP_WIN), :].T
    sc_t = (_dot(ovt_ref[...], heads_sum(p_far).astype(BF16))
            + _dot(ov_win_t.astype(BF16), heads_sum(p_win).astype(BF16)))

    n_wt = WINDOW // QB + 1
    k_t, v_t, b_t = [], [], []
    for j in range(n_wt):
        kb = qb - (n_wt - 1) + j
        kb0 = jnp.maximum(kb, 0)
        k_t.append(kw_ref[0, 0, pl.ds(pl.multiple_of(kb0 * QB, QB), QB), :])
        v_t.append(vw_ref[0, 0, kb0])
        b_t.append(t_ref[0, jnp.where(kb < 0, TILE_MASKED, TILE_WINDOW_EDGE if j == 0 else n_wt - 1 - j)])
    s = _dot(jnp.concatenate(k_t, axis=0), q) + jnp.concatenate(b_t, axis=0)
    p = jnp.exp2(s - jnp.max(s, axis=0, keepdims=True))
    acc = _dot(jnp.concatenate(v_t, axis=1), p.astype(BF16))
    o_w = acc[0:HEAD_DIM, :] / acc[HEAD_DIM:HEAD_DIM + 1, :]

    blk = lax.broadcasted_iota(I32, (nb, QB), 0)
    t_io = lax.broadcasted_iota(I32, (nb, QB), 1)
    cur = qb * (QB // SLC_LEN) + t_io // SLC_LEN
    forced = (blk == 0) | (blk == cur) | (blk == cur - 1)
    val = jnp.where(forced, jnp.inf, jnp.where(blk <= cur, sc_t, -jnp.inf))
    sel_t = jnp.zeros((nb, QB), F32)
    for _ in range(n_top):
        mx = jnp.max(val, axis=0, keepdims=True)
        hit = (val == mx) & (val > -jnp.inf)
        first = jnp.min(jnp.where(hit, blk, nb), axis=0, keepdims=True)
        pick = blk == first
        sel_t = jnp.where(pick, 1.0, sel_t)
        val = jnp.where(pick, -jnp.inf, val)
    sel_scr[...] = jnp.where(sel_t > 0.5, 0.0, NEG)

    m_scr[...] = jnp.full(m_scr.shape, NEG, F32)
    acc_scr[...] = jnp.zeros(acc_scr.shape, F32)
    blocks_per_chunk = FAR_CHUNK * QB // SLC_LEN

    chunk = FAR_CHUNK * QB

    def qk(c):
        return _dot(ks_ref[0, 0, pl.ds(pl.multiple_of(c * chunk, chunk), chunk), :], q)

    def soft(c, s):
        rows = sel_scr[pl.ds(pl.multiple_of(c * blocks_per_chunk, blocks_per_chunk), blocks_per_chunk), :]
        drop = jnp.concatenate([jnp.broadcast_to(rows[r:r + 1, :], (SLC_LEN, QB)) for r in range(blocks_per_chunk)],
                               axis=0)
        s = s + jnp.concatenate([drop] * GQA, axis=1)
        m_old = m_scr[...]
        m_new = jnp.maximum(m_old, jnp.max(s, axis=0, keepdims=True))
        p = jnp.exp2(s - m_new)
        v_t = jnp.concatenate([vs_ref[0, 0, c * FAR_CHUNK + j] for j in range(FAR_CHUNK)], axis=1)
        acc_scr[...] = jnp.exp2(m_old - m_new) * acc_scr[...] + _dot(v_t, p.astype(BF16))
        m_scr[...] = m_new

    def near_bias(c):
        tiles = []
        for j in range(FAR_CHUNK):
            delta = qb - (c * FAR_CHUNK + j)
            tiles.append(t_ref[0, jnp.where(delta < 0, TILE_MASKED, delta)])
        return jnp.concatenate(tiles, axis=0)

    def chunk_pairs(first, count, bias):
        last = first + jnp.maximum(count - 1, 0)
        sa_scr[...] = qk(first)

        def pair_body(j, carry):
            c = first + 2 * j
            sb_scr[...] = qk(c + 1)
            soft(c, sa_scr[...] + bias(c) if bias else sa_scr[...])
            sa_scr[...] = qk(jnp.minimum(c + 2, last))
            soft(c + 1, sb_scr[...] + bias(c + 1) if bias else sb_scr[...])
            return carry

        lax.fori_loop(0, count // 2, pair_body, 0)

        @pl.when(count % 2 == 1)
        def _():
            soft(last, sa_scr[...] + bias(last) if bias else sa_scr[...])

    n_far = jnp.maximum(qb - (FAR_TILE_DIST - 1), 0) // FAR_CHUNK
    chunk_pairs(0, n_far, None)
    chunk_pairs(n_far, qb // FAR_CHUNK + 1 - n_far, near_bias)
    acc = acc_scr[...]
    o_s = acc[0:HEAD_DIM, :] / acc[HEAD_DIM:HEAD_DIM + 1, :]

    gates = g_ref[0, 0]
    outs = []
    for hq in range(GQA):
        cols = slice(hq * QB, (hq + 1) * QB)
        c = hq * N_BRANCH
        outs.append(gates[c:c + 1, :] * o_c[0:HEAD_DIM, cols] + gates[c + 1:c + 2, :] * o_s[:, cols]
                    + gates[c + 2:c + 3, :] * o_w[:, cols])
    o_ref[0] = jnp.concatenate(outs, axis=0).T


def _attn(q_t, kx, vx_t, kvc, kvc_t, gates_t, t_tab, w_tab, ov_t, ov):
    B, _, W, _ = q_t.shape
    S = kx.shape[2]
    ncp = kvc.shape[2]
    nb = ov.shape[1]
    keys = lambda j: pl.BlockSpec((1, 1, S, W), lambda b, g, t: (b, j * N_KV + g, 0, 0))
    vals = lambda j: pl.BlockSpec((1, 1, S // QB, W, QB), lambda b, g, t: (b, j * N_KV + g, 0, 0, 0))
    cmp = lambda j: pl.BlockSpec((1, 1, ncp, W), lambda b, g, t: (b, j * N_KV + g, 0, 0))
    return pl.pallas_call(
        _attn_kernel,
        grid=(B, N_KV, S // QB),
        in_specs=[pl.BlockSpec((1, 1, W, GQA * QB), lambda b, g, t: (b, g, 0, t)),
                  keys(0), vals(0), keys(1), vals(1), cmp(0), cmp(1),
                  pl.BlockSpec((1, 1, W, ncp - CMP_WIN), lambda b, g, t: (b, N_KV + g, 0, 0)),
                  pl.BlockSpec((1, 1, 16, QB), lambda b, g, t: (b, g, 0, t)),
                  pl.BlockSpec((1,) + t_tab.shape[1:], lambda b, g, t: (g, 0, 0, 0)),
                  pl.BlockSpec((1, CMP_WIN, GQA * QB), lambda b, g, t: (g, 0, 0)),
                  pl.BlockSpec(ov_t.shape, lambda b, g, t: (0, 0)),
                  pl.BlockSpec(ov.shape, lambda b, g, t: (0, 0))],
        out_specs=pl.BlockSpec((1, QB, GQA * HEAD_DIM), lambda b, g, t: (b, t, g)),
        out_shape=jax.ShapeDtypeStruct((B, S, N_HEADS * HEAD_DIM), F32),
        scratch_shapes=[pltpu.VMEM((1, GQA * QB), F32), pltpu.VMEM((W, GQA * QB), F32), pltpu.VMEM((nb, QB), F32),
                        pltpu.VMEM((FAR_CHUNK * QB, GQA * QB), F32), pltpu.VMEM((FAR_CHUNK * QB, GQA * QB), F32)],
        compiler_params=_params("parallel", "parallel", "arbitrary"),
        name="attn",
    )(q_t, kx, vx_t, kx, vx_t, kvc, kvc, kvc_t, gates_t, t_tab, w_tab, ov_t, ov)


def _rglru_kernel(xr_ref, gr_ref, cw_ref, cb_ref, wa_ref, ba_ref, wx_ref, bx_ref, lam_ref, gw_ref, o_ref,
                  xbuf, hprev, a_scr, u_scr, h_scr):
    ts = xr_ref.shape[1]
    C = xr_ref.shape[2]

    @pl.when(pl.program_id(1) == 0)
    def _():
        xbuf[0:8, :] = jnp.zeros((8, C), F32)
        hprev[...] = jnp.zeros(hprev.shape, F32)

    xbuf[8:8 + ts, :] = xr_ref[0]
    xc = cb_ref[...] + jnp.zeros((ts, C), F32)
    for j in range(CONV_WIDTH):
        xc = xc + cw_ref[j:j + 1, :] * xbuf[pl.ds(8 - (CONV_WIDTH - 1) + j, ts), :]
    xbuf[0:8, :] = xbuf[ts:ts + 8, :]

    xcb = xc.astype(BF16)
    r = _sigmoid(_dot(xcb, wa_ref[...]) + ba_ref[...])
    i = _sigmoid(_dot(xcb, wx_ref[...]) + bx_ref[...])
    z = -lam_ref[...]
    softplus = jnp.maximum(z, 0.0) + jnp.log(1.0 + jnp.exp(-jnp.abs(z)))
    log_a = -LRU_C * r * softplus
    a_scr[...] = jnp.exp(log_a)
    u_scr[...] = jnp.sqrt(1.0 - jnp.exp(2.0 * log_a)) * (i * xc)

    row = lax.broadcasted_iota(I32, (8, C), 0)

    def body(k, h):
        r0 = pl.multiple_of(k * 8, 8)
        a = a_scr[pl.ds(r0, 8), :]
        b = u_scr[pl.ds(r0, 8), :]
        for s in (1, 2, 4):
            keep = row >= s
            b = jnp.where(keep, a * pltpu.roll(b, s, 0) + b, b)
            a = jnp.where(keep, a * pltpu.roll(a, s, 0), a)
        hh = a * h + b
        h_scr[pl.ds(r0, 8), :] = hh
        return jnp.broadcast_to(hh[7:8, :], (8, C))

    hprev[...] = lax.fori_loop(0, ts // 8, body, hprev[...])
    out = h_scr[...] * _gelu(gr_ref[0])
    o_ref[0] = (_rms(out) * gw_ref[...]).astype(BF16)


def _rglru(xr, gr, cw, cb, wa_bd, ba, wx_bd, bx, lam, gw, ts=512):
    B, S, C = xr.shape
    ts = min(ts, S)
    vec = pl.BlockSpec((1, C), lambda b, t: (0, 0))
    mat = pl.BlockSpec((C, C), lambda b, t: (0, 0))
    seq = pl.BlockSpec((1, ts, C), lambda b, t: (b, t, 0))
    return pl.pallas_call(
        _rglru_kernel,
        grid=(B, S // ts),
        in_specs=[seq, seq, pl.BlockSpec((CONV_WIDTH, C), lambda b, t: (0, 0)), vec, mat, vec, mat, vec, vec, vec],
        out_specs=seq,
        out_shape=jax.ShapeDtypeStruct((B, S, C), BF16),
        scratch_shapes=[pltpu.VMEM((ts + 8, C), F32), pltpu.VMEM((8, C), F32), pltpu.VMEM((ts, C), F32),
                        pltpu.VMEM((ts, C), F32), pltpu.VMEM((ts, C), F32)],
        compiler_params=_params("parallel", "arbitrary"),
        name="rglru",
    )(xr, gr, cw, cb, wa_bd, ba, wx_bd, bx, lam, gw)


def _to_row_tiles(ref, x):
    rows = x.shape[0]
    for s in range(ROW_TILE):
        ref[pl.ds(s, rows, stride=ROW_TILE), :] = x[:, s * 128:(s + 1) * 128]


def _from_row_tiles(ref, rows, start=0):
    return jnp.concatenate([ref[pl.ds(start + s, rows, stride=ROW_TILE), :] for s in range(ROW_TILE)], axis=1)


def _outproj_kernel(oa_ref, yr_ref, x_ref, mod_ref, gaw_ref, wo_ref, n2_ref, wr_ref, x1_ref, h2_ref, st_ref):
    da = oa_ref.shape[2]
    ya = (_rms(oa_ref[0]) * gaw_ref[...]).astype(BF16)
    mix = _dot(ya, wo_ref[0:da, :]) + _dot(yr_ref[0], wo_ref[da:, :])
    x1 = x_ref[0] + mod_ref[0, 2:3, :] * mix
    x1_ref[0] = x1
    h2 = (_rms(x1) * n2_ref[...]) * (1.0 + mod_ref[0, 4:5, :]) + mod_ref[0, 3:4, :]
    h2_ref[0] = h2.astype(BF16)
    st_ref[...] = _sigmoid(_dot_nt(wr_ref[...], h2, precision=lax.Precision.HIGHEST))


def _outproj(oa, yr, x, mod, gaw, wo, n2, wr_t, tm=256):
    B, S, D = x.shape
    E = wr_t.shape[0]
    nt = S // tm
    row = lambda w: pl.BlockSpec((1, tm, w), lambda b, t: (b, t, 0))
    return pl.pallas_call(
        _outproj_kernel,
        grid=(B, nt),
        in_specs=[row(oa.shape[2]), row(yr.shape[2]), row(D),
                  pl.BlockSpec((1, 6, D), lambda b, t: (b, 0, 0)),
                  pl.BlockSpec((1, oa.shape[2]), lambda b, t: (0, 0)),
                  pl.BlockSpec(wo.shape, lambda b, t: (0, 0)),
                  pl.BlockSpec((1, D), lambda b, t: (0, 0)),
                  pl.BlockSpec((E, D), lambda b, t: (0, 0))],
        out_specs=[row(D), row(D), pl.BlockSpec((E, tm), lambda b, t: (0, b * nt + t))],
        out_shape=[jax.ShapeDtypeStruct((B, S, D), F32), jax.ShapeDtypeStruct((B, S, D), BF16),
                   jax.ShapeDtypeStruct((E, B * S), F32)],
        compiler_params=_params("parallel", "parallel"),
        name="outproj",
    )(oa, yr, x, mod, gaw, wo, n2, wr_t)


def _route_kernel(s_ref, b_ref, eid_ref, wgt_ref, rank_ref, cnt_ref, tcnt_ref, carry):
    E, tn = s_ref.shape
    per = E // N_EXPERT_GROUPS

    @pl.when(pl.program_id(0) == 0)
    def _():
        carry[...] = jnp.zeros(carry.shape, F32)

    s = s_ref[...]
    s_sel = s + b_ref[...]
    eidx = lax.broadcasted_iota(I32, (E, tn), 0)

    grp = []
    for gi in range(N_EXPERT_GROUPS):
        xg = s_sel[gi * per:(gi + 1) * per, :]
        ig = lax.broadcasted_iota(I32, (per, tn), 0)
        m1 = jnp.max(xg, axis=0, keepdims=True)
        f1 = jnp.min(jnp.where(xg == m1, ig, per), axis=0, keepdims=True)
        m2 = jnp.max(jnp.where(ig == f1, -jnp.inf, xg), axis=0, keepdims=True)
        grp.append(m1 + m2)
    val = []
    for gi in range(N_EXPERT_GROUPS):
        rank = jnp.zeros((1, tn), I32)
        for gj in range(N_EXPERT_GROUPS):
            if gj == gi:
                continue
            ahead = (grp[gj] > grp[gi]) | ((grp[gj] == grp[gi]) & (gj < gi))
            rank = rank + ahead.astype(I32)
        val.append(jnp.where(rank < TOP_GROUPS, s_sel[gi * per:(gi + 1) * per, :], -jnp.inf))
    val = jnp.concatenate(val, axis=0)

    eids, tops = [], []
    hot = jnp.zeros((E, tn), F32)
    for _ in range(TOP_K):
        mx = jnp.max(val, axis=0, keepdims=True)
        first = jnp.min(jnp.where(val == mx, eidx, E), axis=0, keepdims=True)
        pick = eidx == first
        eids.append(first)
        tops.append(jnp.sum(jnp.where(pick, s, 0.0), axis=0, keepdims=True))
        hot = jnp.where(pick, 1.0, hot)
        val = jnp.where(pick, -jnp.inf, val)
    denom = tops[0]
    for t in tops[1:]:
        denom = denom + t

    ti = lax.broadcasted_iota(I32, (tn, tn), 0)
    tj = lax.broadcasted_iota(I32, (tn, tn), 1)
    upper = jnp.where(ti < tj, 1.0, 0.0).astype(BF16)
    before = _dot(hot.astype(BF16), upper) + carry[...]
    for k in range(TOP_K):
        eid_ref[k:k + 1, :] = eids[k]
        wgt_ref[k:k + 1, :] = ROUTED_SCALE * tops[k] / denom
        rank_ref[k:k + 1, :] = jnp.sum(jnp.where(eidx == eids[k], before, 0.0), axis=0, keepdims=True).astype(I32)
    tile_count = jnp.sum(hot, axis=1, keepdims=True)
    tcnt_ref[0] = tile_count
    carry[...] = carry[...] + tile_count
    cnt_ref[...] = carry[...]


def _route(s_t, b_col, tn=256):
    E, N = s_t.shape
    tn = min(tn, N)
    out = pl.BlockSpec((TOP_K, tn), lambda t: (0, t))
    return pl.pallas_call(
        _route_kernel,
        grid=(N // tn,),
        in_specs=[pl.BlockSpec((E, tn), lambda t: (0, t)), pl.BlockSpec((E, 1), lambda t: (0, 0))],
        out_specs=[out, out, out, pl.BlockSpec((E, 1), lambda t: (0, 0)), pl.BlockSpec((1, E, 1), lambda t: (t, 0, 0))],
        out_shape=[jax.ShapeDtypeStruct((TOP_K, N), I32), jax.ShapeDtypeStruct((TOP_K, N), F32),
                   jax.ShapeDtypeStruct((TOP_K, N), I32), jax.ShapeDtypeStruct((E, 1), F32),
                   jax.ShapeDtypeStruct((N // tn, E, 1), F32)],
        scratch_shapes=[pltpu.VMEM((E, 1), F32)],
        compiler_params=_params("arbitrary"),
        name="route",
    )(s_t, b_col)


def _tilepos_kernel(eid_ref, rank_ref, base_ref, o_ref):
    E = base_ref.shape[1]
    tn = eid_ref.shape[1]
    eidx = lax.broadcasted_iota(I32, (E, tn), 0)
    for k in range(TOP_K):
        base = jnp.sum(jnp.where(eidx == eid_ref[k:k + 1, :], base_ref[0], 0.0), axis=0, keepdims=True)
        o_ref[k:k + 1, :] = base.astype(I32) + rank_ref[k:k + 1, :]


def _tilepos(eid_t, rank_t, base, tn):
    K, N = eid_t.shape
    blk = pl.BlockSpec((K, tn), lambda t: (0, t))
    return pl.pallas_call(
        _tilepos_kernel,
        grid=(N // tn,),
        in_specs=[blk, blk, pl.BlockSpec((1,) + base.shape[1:], lambda t: (t, 0, 0))],
        out_specs=blk,
        out_shape=jax.ShapeDtypeStruct((K, N), I32),
        compiler_params=_params("parallel"),
        name="tilepos",
    )(eid_t, rank_t, base)


_HI = -65536


def _pack_pairs(even_rows, odd_rows):
    hi = lax.bitcast_convert_type(even_rows, I32) & _HI
    lo = lax.shift_right_logical(lax.bitcast_convert_type(odd_rows, I32), 16)
    return hi | lo


def _unpack_pairs(words):
    even_rows = lax.bitcast_convert_type(words & _HI, F32)
    odd_rows = lax.bitcast_convert_type(lax.shift_left(words, 16), F32)
    return even_rows, odd_rows


def _run_copies(src_ref, cnt_ref, dst_ref, n_runs, copy):
    def go(e, carry):
        n = cnt_ref[0, 0, e]

        @pl.when(n > 0)
        def _():
            size = pl.multiple_of(n * ROW_TILE, ROW_TILE)
            copy(pl.ds(pl.multiple_of(src_ref[0, 0, e] * ROW_TILE, ROW_TILE), size),
                 pl.ds(pl.multiple_of(dst_ref[0, 0, e] * ROW_TILE, ROW_TILE), size)).start()
        return carry
    lax.fori_loop(0, n_runs, go, 0)


def _pair_onehots(pos_rows, n_pairs, first_pair, width):
    even = 2 * (lax.broadcasted_iota(I32, (n_pairs, width), 0) + first_pair)
    hot_e = jnp.zeros((n_pairs, width), F32)
    hot_o = jnp.zeros((n_pairs, width), F32)
    for k in range(TOP_K):
        hot_e = jnp.where(even == pos_rows[k:k + 1, :], 1.0, hot_e)
        hot_o = jnp.where(even + 1 == pos_rows[k:k + 1, :], 1.0, hot_o)
    return hot_e, hot_o


def _dispatch_kernel(src_ref, cnt_ref, dst_ref, pos_ref, h_ref, xs_hbm, xbuf, sem):
    n_pairs = xbuf.shape[0] // ROW_TILE
    tn = h_ref.shape[0]
    hb = h_ref[...]
    slab = n_pairs // 3
    for sl in range(3):
        hot_e, hot_o = _pair_onehots(pos_ref[...], slab, sl * slab, tn)
        words = _pack_pairs(_dot(hot_e.astype(BF16), hb), _dot(hot_o.astype(BF16), hb))
        for s in range(ROW_TILE):
            xbuf[pl.ds(sl * slab * ROW_TILE + s, slab, stride=ROW_TILE), :] = words[:, s * 128:(s + 1) * 128]
    _run_copies(src_ref, cnt_ref, dst_ref, cnt_ref.shape[2],
                lambda src, dst: pltpu.make_async_copy(xbuf.at[src], xs_hbm.at[dst], sem.at[0]))
    pltpu.make_async_copy(xbuf, xs_hbm.at[pl.ds(0, xbuf.shape[0])], sem.at[0]).wait()


def _run_specs(runs):
    return [pl.BlockSpec((1, 1, r.shape[2]), lambda t: (t, 0, 0), memory_space=pltpu.SMEM) for r in runs]


def _tile_pairs(tn, n_experts):
    return (TOP_K * tn + n_experts) // 2


def _dispatch(runs, pos_t, h2, tn):
    N, D = h2.shape
    W = D // ROW_TILE
    n_pairs = _tile_pairs(tn, runs[0].shape[2] - 1)
    assert n_pairs % 24 == 0
    return pl.pallas_call(
        _dispatch_kernel,
        grid=(N // tn,),
        in_specs=_run_specs(runs) + [pl.BlockSpec((TOP_K, tn), lambda t: (0, t)),
                                     pl.BlockSpec((tn, D), lambda t: (t, 0))],
        out_specs=pl.BlockSpec(memory_space=pl.ANY),
        out_shape=jax.ShapeDtypeStruct((N // tn * n_pairs * ROW_TILE, W), I32),
        scratch_shapes=[pltpu.VMEM((n_pairs * ROW_TILE, W), I32), pltpu.SemaphoreType.DMA((1,))],
        compiler_params=_params("arbitrary"),
        name="dispatch",
    )(*runs, pos_t, h2)


def _experts_kernel(first_ref, size_ref, e_ref, new_ref, next_ref, wslot_ref, total_ref,
                    w1_hbm, w3_hbm, w2_hbm, xs_hbm, ys_hbm,
                    xbuf, ybuf, w1f, w3f, w2f, w1b, w3b, w2b, xsem, ysem, wsem):
    total = total_ref[0]

    def w_copies(e, slot):
        return [pltpu.make_async_copy(src.at[e], dst.at[slot], wsem.at[slot])
                for src, dst in ((w1_hbm, w1f), (w3_hbm, w3f), (w2_hbm, w2f))]

    def rows(c):
        return (pl.multiple_of(first_ref[c] * ROW_TILE, ROW_TILE), pl.multiple_of(size_ref[c] * ROW_TILE, ROW_TILE))

    def x_copy(c, slot):
        first, size = rows(c)
        return pltpu.make_async_copy(xs_hbm.at[pl.ds(first, size)], xbuf.at[slot, pl.ds(0, size)], xsem.at[slot])

    def y_copy(c, slot):
        first, size = rows(c)
        return pltpu.make_async_copy(ybuf.at[slot, pl.ds(0, size)], ys_hbm.at[pl.ds(first, size)], ysem.at[slot])

    xbuf[...] = jnp.zeros(xbuf.shape, I32)
    for cp in w_copies(e_ref[0], 0):
        cp.start()
    x_copy(0, 0).start()
    half = MOE_BLOCK // 2

    def body(c, carry):
        slot = c % 2

        @pl.when(new_ref[c] == 1)
        def _():
            ws = wslot_ref[c]
            for cp in w_copies(e_ref[c], ws):
                cp.wait()
            w1b[...] = w1f[ws].astype(BF16)
            w3b[...] = w3f[ws].astype(BF16)
            w2b[...] = w2f[ws].astype(BF16)

            @pl.when(next_ref[c] >= 0)
            def _():
                for cp in w_copies(next_ref[c], 1 - ws):
                    cp.start()

        x_copy(c, slot).wait()

        @pl.when(c + 1 < total)
        def _():
            x_copy(c + 1, 1 - slot).start()

        x = jnp.concatenate(_unpack_pairs(_from_row_tiles(xbuf.at[slot], half)), axis=0).astype(BF16)
        h1 = _dot(x, w1b[...])
        h3 = _dot(x, w3b[...])
        hid = (h1 * _sigmoid(h1)) * h3
        y = _dot(hid.astype(BF16), w2b[...]).astype(BF16).astype(F32)

        @pl.when(c >= 2)
        def _():
            y_copy(c - 2, slot).wait()

        _to_row_tiles(ybuf.at[slot], _pack_pairs(y[0:half], y[half:]))
        y_copy(c, slot).start()
        return carry

    lax.fori_loop(0, total, body, 0)

    @pl.when(total >= 2)
    def _():
        y_copy(total - 2, total % 2).wait()

    y_copy(total - 1, (total - 1) % 2).wait()

    ybuf[0] = jnp.zeros(ybuf.shape[1:], I32)
    tail_first = total_ref[1]
    tail_pairs = ys_hbm.shape[0] // ROW_TILE - tail_first

    def fill(j, carry):
        size = pl.multiple_of(jnp.minimum(half, tail_pairs - j * half) * ROW_TILE, ROW_TILE)
        cp = pltpu.make_async_copy(ybuf.at[0, pl.ds(0, size)],
                                   ys_hbm.at[pl.ds(pl.multiple_of((tail_first + j * half) * ROW_TILE, ROW_TILE), size)],
                                   ysem.at[0])
        cp.start()
        cp.wait()
        return carry

    lax.fori_loop(0, (tail_pairs + half - 1) // half, fill, 0)


def _expert_chunks(counts, n_pairs_total):
    E = counts.shape[0]
    block = MOE_BLOCK // 2
    n_max = n_pairs_total // block + E
    starts = jnp.cumsum(counts) - counts
    n_chunks = (counts + block - 1) // block
    c_end = jnp.cumsum(n_chunks)
    c = jnp.arange(n_max, dtype=I32)
    e = jnp.minimum(jnp.sum(c_end[None, :] <= c[:, None], axis=1), E - 1).astype(I32)
    j = c - (c_end - n_chunks)[e]
    size = jnp.clip(counts[e] - j * block, 0, block)
    has_rows = counts > 0
    ids = jnp.where(has_rows, jnp.arange(E, dtype=I32), E)
    later = jnp.flip(lax.cummin(jnp.flip(ids)))
    nxt = jnp.concatenate([later[1:], jnp.full((1,), E, I32)])
    nxt = jnp.where(nxt >= E, -1, nxt)
    ordinal = jnp.cumsum(has_rows.astype(I32)) - 1
    to_i32 = lambda a: a.astype(I32)
    meta = jnp.stack([c_end[-1], jnp.sum(counts)])
    return tuple(map(to_i32, (starts[e] + j * block, size, e, j == 0, nxt[e], ordinal[e] % 2, meta)))


def _experts(chunks, xs, w1, w3, w2):
    rows_total, W = xs.shape
    _, D, De = w1.shape
    any_space = pl.BlockSpec(memory_space=pl.ANY)
    block_rows = MOE_BLOCK // 2 * ROW_TILE
    grid_spec = pltpu.PrefetchScalarGridSpec(
        num_scalar_prefetch=len(chunks),
        grid=(1,),
        in_specs=[any_space] * 4,
        out_specs=any_space,
        scratch_shapes=[pltpu.VMEM((2, block_rows, W), I32), pltpu.VMEM((2, block_rows, W), I32),
                        pltpu.VMEM((2, D, De), F32), pltpu.VMEM((2, D, De), F32), pltpu.VMEM((2, De, D), F32),
                        pltpu.VMEM((D, De), BF16), pltpu.VMEM((D, De), BF16), pltpu.VMEM((De, D), BF16),
                        pltpu.SemaphoreType.DMA((2,)), pltpu.SemaphoreType.DMA((2,)), pltpu.SemaphoreType.DMA((2,))],
    )
    return pl.pallas_call(
        _experts_kernel,
        grid_spec=grid_spec,
        out_shape=jax.ShapeDtypeStruct((rows_total, W), I32),
        compiler_params=_params("arbitrary"),
        name="experts",
    )(*chunks, w1, w3, w2, xs)


def _combine_kernel(src_ref, cnt_ref, dst_ref, ys_hbm, post_ref, wt_ref, h2_ref, x1_ref, mod_ref, ws1_ref,
                    ws3_ref, ws2_ref, fw_ref, o_ref, ybuf, sem):
    tm = h2_ref.shape[0]
    n_pairs = ybuf.shape[0] // ROW_TILE
    n_exp = cnt_ref.shape[2] - 1

    @pl.when(pl.program_id(0) == 0)
    def _():
        ybuf[...] = jnp.zeros(ybuf.shape, I32)

    _run_copies(src_ref, cnt_ref, dst_ref, n_exp,
                lambda src, dst: pltpu.make_async_copy(ys_hbm.at[dst], ybuf.at[src], sem.at[0]))

    h = h2_ref[...]
    a = _dot(h, ws1_ref[...])
    hid = (a * _sigmoid(a)) * _dot(h, ws3_ref[...])
    shared = _dot(hid.astype(BF16), ws2_ref[...])

    pair = 2 * lax.broadcasted_iota(I32, (n_pairs, tm), 0)
    w_cols, mine = [], []
    for odd in range(2):
        w_rows = jnp.zeros((n_pairs, tm), F32)
        for k in range(TOP_K):
            w_rows = jnp.where(pair + odd == post_ref[k:k + 1, :], wt_ref[k:k + 1, :], w_rows)
        w_cols.append(jnp.sum(w_rows, axis=1, keepdims=True))
        mine.append(jnp.where(w_rows != 0.0, 1.0, 0.0).astype(BF16))
    used = pl.multiple_of((src_ref[0, 0, n_exp]) * ROW_TILE, ROW_TILE)
    pltpu.make_async_copy(ys_hbm.at[pl.ds(0, used)], ybuf.at[pl.ds(0, used)], sem.at[0]).wait()
    y_even, y_odd = _unpack_pairs(_from_row_tiles(ybuf, n_pairs))
    sum_rows = lambda hot, y: lax.dot_general(hot, y, (((0,), (0,)), ((), ())), preferred_element_type=F32)
    routed = (sum_rows(mine[0], (y_even * w_cols[0]).astype(BF16))
              + sum_rows(mine[1], (y_odd * w_cols[1]).astype(BF16)))
    x2 = x1_ref[...] + mod_ref[0, 5:6, :] * (routed + shared)
    o_ref[...] = _rms(x2) * fw_ref[...]


def _combine(runs, ys, pos_t, wgt_t, h2, x1, mod, ws1, ws3, ws2, fw, seq, tm):
    N, D = h2.shape
    nt = seq // tm
    row = pl.BlockSpec((tm, D), lambda t: (t, 0))
    full = lambda a: pl.BlockSpec(a.shape, lambda t: (0,) * a.ndim)
    return pl.pallas_call(
        _combine_kernel,
        grid=(N // tm,),
        in_specs=_run_specs(runs) + [pl.BlockSpec(memory_space=pl.ANY),
                                     pl.BlockSpec((TOP_K, tm), lambda t: (0, t)),
                                     pl.BlockSpec((TOP_K, tm), lambda t: (0, t)),
                                     row, row,
                                     pl.BlockSpec((1, 6, D), lambda t: (t // nt, 0, 0)),
                                     full(ws1), full(ws3), full(ws2), full(fw)],
        out_specs=row,
        out_shape=jax.ShapeDtypeStruct((N, D), F32),
        scratch_shapes=[pltpu.VMEM((_tile_pairs(tm, runs[0].shape[2] - 1) * ROW_TILE, D // ROW_TILE), I32),
                        pltpu.SemaphoreType.DMA((1,))],
        compiler_params=_params("arbitrary"),
        name="combine",
    )(*runs, ys, pos_t, wgt_t, h2, x1, mod, ws1, ws3, ws2, fw)


def _t5_bucket(dist):
    n = jnp.maximum(dist, 0)
    max_exact = REL_BUCKETS // 2
    nf = jnp.maximum(n, 1).astype(F32)
    large = max_exact + (jnp.log(nf / max_exact) / math.log(REL_MAX_DIST / max_exact)
                         * (REL_BUCKETS - max_exact)).astype(I32)
    large = jnp.minimum(large, REL_BUCKETS - 1)
    return jnp.where(n < max_exact, n, large)


def _bias_kernel(tbl_ref, far_ref, bt_ref, bw_ref, t_ref, w_ref):
    g = pl.program_id(0)
    d = pl.program_id(1)

    def lookup(bkt, head):
        out = jnp.full(bkt.shape, tbl_ref[0, head], F32)
        for b in range(1, REL_BUCKETS):
            out = jnp.where(bkt == b, tbl_ref[b, head], out)
        return out - far_ref[head]

    j = lax.broadcasted_iota(I32, (QB, QB), 0)
    i = lax.broadcasted_iota(I32, (QB, QB), 1)
    keep = ((d != 0) | (i >= j)) & (d != TILE_MASKED) & ((d != TILE_WINDOW_EDGE) | (j > i))
    for hq in range(GQA):
        t_ref[0, 0, :, hq * QB:(hq + 1) * QB] = jnp.where(keep, lookup(bt_ref[0], GQA * g + hq), NEG)

    @pl.when(d == 0)
    def _():
        for hq in range(GQA):
            w_ref[0, :, hq * QB:(hq + 1) * QB] = lookup(bw_ref[...], GQA * g + hq)


def _bias_tables(rel_table):
    tbl = rel_table.astype(F32) * LOG2E
    far_hi = tbl[REL_BUCKETS - 1].astype(BF16)
    far_lo = (tbl[REL_BUCKETS - 1] - far_hi.astype(F32)).astype(BF16)
    far = far_hi.astype(F32) + far_lo.astype(F32)
    q_tail = jnp.zeros((N_HEADS, HEAD_DIM), F32).at[:, 0].set(far_hi.astype(F32)).at[:, 1].set(far_lo.astype(F32))
    i = jnp.arange(QB)
    tile_dist = np.array(list(range(N_BIAS_TILES)) + [0, WINDOW // QB])
    d_t = (QB * jnp.asarray(tile_dist)[:, None, None] + i[None, None, :] - i[None, :, None])
    d_w = i[None, :] - CMP_STRIDE * (jnp.arange(CMP_WIN)[:, None] - CMP_PAD) - (CMP_LEN - 1)
    n_tiles = len(tile_dist)
    t_tab, w_tab = pl.pallas_call(
        _bias_kernel,
        grid=(N_KV, n_tiles),
        in_specs=[pl.BlockSpec(memory_space=pltpu.SMEM),
                  pl.BlockSpec(memory_space=pltpu.SMEM),
                  pl.BlockSpec((1, QB, QB), lambda g, d: (d, 0, 0)),
                  pl.BlockSpec((CMP_WIN, QB), lambda g, d: (0, 0))],
        out_specs=[pl.BlockSpec((1, 1, QB, GQA * QB), lambda g, d: (g, d, 0, 0)),
                   pl.BlockSpec((1, CMP_WIN, GQA * QB), lambda g, d: (g, 0, 0))],
        out_shape=[jax.ShapeDtypeStruct((N_KV, n_tiles, QB, GQA * QB), F32),
                   jax.ShapeDtypeStruct((N_KV, CMP_WIN, GQA * QB), F32)],
        compiler_params=_params("parallel", "arbitrary"),
        name="bias",
    )(tbl, far, _t5_bucket(d_t), _t5_bucket(d_w))
    far_min = min(FAR_TILE_DIST * QB - QB + 1, CMP_STRIDE * (CMP_PAD + 1) - (CMP_LEN - 1))
    max_exact = REL_BUCKETS // 2
    assert math.log(far_min / max_exact) / math.log(REL_MAX_DIST / max_exact) * max_exact > max_exact - 0.75
    return t_tab, w_tab, q_tail


def _overlap_tables(nc, nb):
    c_start = CMP_STRIDE * np.arange(nc)[:, None]
    s_start = SLC_LEN * np.arange(nb)[None, :]
    ov = ((c_start < s_start + SLC_LEN) & (c_start + CMP_LEN > s_start)).astype(np.float32)
    ov[nc - 1:] = 0.0
    out = np.zeros((nc + CMP_WIN, nb), np.float32)
    out[CMP_PAD:CMP_PAD + nc] = ov
    return jnp.asarray(ov.T, dtype=BF16), jnp.asarray(out)


def _block_diag(w):
    nblk, bs, _ = w.shape
    eye = jnp.eye(nblk, dtype=w.dtype)
    return (eye[:, None, :, None] * w[:, :, None, :]).reshape(nblk * bs, nblk * bs)


def _layer(x, c8, w_ada, b_ada, norm1_w, w_in, cmp_pos, cmp_k_w1, cmp_k_w2, cmp_v_w1, cmp_v_w2, rel_table,
           conv_w, conv_b, lru_wa, lru_ba, lru_wx, lru_bx, lru_lambda, gnorm_attn_w, gnorm_rnn_w, w_out,
           norm2_w, w_router, b_router, w1, w3, w2, ws1, ws3, ws2, final_norm_w):
    B, S, D = x.shape
    N = B * S
    E = w_router.shape[1]
    d_attn = N_HEADS * HEAD_DIM
    d_rnn = D - d_attn
    assert S % 512 == 0 and S // SLC_LEN <= 128

    mod = _ada(c8, w_ada, b_ada[None, :])[:B].reshape(B, 6, D)

    n_kv_cols = 6 * N_KV * HEAD_DIM
    n_gate = N_BRANCH * N_HEADS
    wq, wkv, wg, wxr, wgr = jnp.split(w_in, np.cumsum([d_attn, n_kv_cols, n_gate, d_rnn]).tolist(), axis=1)
    wg = jnp.pad(wg.reshape(D, N_KV, GQA * N_BRANCH), ((0, 0), (0, 0), (0, 128 - GQA * N_BRANCH))).reshape(D, 256)
    w_pad = jnp.concatenate([wq, wkv, wg, wxr, wgr], axis=1).astype(BF16)
    t_tab, w_tab, q_tail = _bias_tables(rel_table)
    q_t, kx, vx_t, kvc, gates_t, xr, gr = _inproj(x, mod, norm1_w[None, :], w_pad, q_tail)

    nc = S // CMP_STRIDE
    kvc16 = kvc.reshape(B, 4, nc, CMP_STRIDE * HEAD_DIM)
    pos2 = cmp_pos.reshape(2, CMP_STRIDE * HEAD_DIM)
    kvc_pad, kvc_t = _compress(kvc16, pos2, jnp.stack([cmp_k_w1, cmp_v_w1]), jnp.stack([cmp_k_w2, cmp_v_w2]))
    o_attn = _attn(q_t, kx, vx_t, kvc_pad, kvc_t, gates_t, t_tab, w_tab, *_overlap_tables(nc, 128))

    y_rnn = _rglru(xr, gr, conv_w.reshape(CONV_WIDTH, d_rnn), conv_b[None, :], _block_diag(lru_wa).astype(BF16),
                   lru_ba[None, :], _block_diag(lru_wx).astype(BF16), lru_bx[None, :], lru_lambda[None, :],
                   gnorm_rnn_w[None, :])

    x1, h2, s_t = _outproj(o_attn, y_rnn, x, mod, gnorm_attn_w[None, :], w_out.astype(BF16), norm2_w[None, :],
                           w_router.T)

    tile = min(MOE_TILE, S)
    eid_t, wgt_t, rank_t, counts, tile_counts = _route(s_t, b_router[:, None], tile)
    del counts
    tcnt = tile_counts[:, :, 0].astype(I32)
    n_tiles = tcnt.shape[0]
    tile_pairs = _tile_pairs(tile, E)
    rpairs = (tcnt + 1) // 2
    rpairs = jnp.concatenate([rpairs, tile_pairs - jnp.sum(rpairs, axis=1, keepdims=True)], axis=1)
    per_expert = jnp.sum(rpairs, axis=0)
    off = jnp.cumsum(per_expert) - per_expert
    run_src = jnp.cumsum(rpairs, axis=1) - rpairs
    run_dst = off[None, :] + jnp.cumsum(rpairs, axis=0) - rpairs
    runs = tuple(r.astype(I32)[:, None, :] for r in (run_src, rpairs, run_dst))
    earlier = jnp.cumsum(tcnt, axis=0) - tcnt
    pos_t = _tilepos(eid_t, rank_t, (2 * run_src[:, :E] - earlier).astype(F32)[:, :, None], tile)

    h2f = h2.reshape(N, D)
    xs = _dispatch(runs, pos_t, h2f, tile)
    ys = _experts(_expert_chunks(per_expert[:E], n_tiles * tile_pairs), xs, w1, w3, w2)
    out = _combine(runs, ys, pos_t, wgt_t, h2f, x1.reshape(N, D), mod, ws1.astype(BF16), ws3.astype(BF16),
                   ws2.astype(BF16), final_norm_w[None, :], S, tile)
    return out.reshape(B, S, D)


def kernel(x, c, w_ada, b_ada, norm1_w, w_in, cmp_pos, cmp_k_w1, cmp_k_w2, cmp_v_w1, cmp_v_w2, rel_table, conv_w, conv_b, lru_wa, lru_ba, lru_wx, lru_bx, lru_lambda, gnorm_attn_w, gnorm_rnn_w, w_out, norm2_w, w_router, b_router, w1, w3, w2, ws1, ws3, ws2, final_norm_w):
    assert w_ada.shape[0] == 1
    c8 = jnp.pad(c, ((0, 8 - c.shape[0]), (0, 0)))
    return _layer(x, c8, w_ada[0], b_ada[0], norm1_w[0], w_in[0], cmp_pos[0], cmp_k_w1[0], cmp_k_w2[0],
                  cmp_v_w1[0], cmp_v_w2[0], rel_table, conv_w[0], conv_b[0], lru_wa[0], lru_ba[0], lru_wx[0],
                  lru_bx[0], lru_lambda[0], gnorm_attn_w[0], gnorm_rnn_w[0], w_out[0], norm2_w[0], w_router[0],
                  b_router[0], w1[0], w3[0], w2[0], ws1[0], ws3[0], ws2[0], final_norm_w)
```

```python
import functools
import math

import jax
import jax.numpy as jnp
import numpy as np
from jax import lax
from jax.experimental import pallas as pl
from jax.experimental.pallas import tpu as pltpu

F32 = jnp.float32
BF16 = jnp.bfloat16
I32 = jnp.int32

HEAD_DIM = 64
N_HEADS = 8
N_KV = 2
GQA = N_HEADS // N_KV
N_BRANCH = 3
CONV_WIDTH = 4
LRU_C = 8.0
CMP_LEN = 32
CMP_STRIDE = 16
SLC_LEN = 64
SLC_TOP = 16
WINDOW = 512
QB = 128
REL_BUCKETS = 32
REL_MAX_DIST = 1024
N_EXPERT_GROUPS = 8
TOP_GROUPS = 4
TOP_K = 8
ROUTED_SCALE = 2.5
MOE_BLOCK = 576
MOE_TILE = 256
ROW_TILE = 8
EPS = 1e-6
NEG = -1e30
LOG2E = math.log2(math.e)
CMP_PAD = 120
CMP_WIN = 128
N_BIAS_TILES = 11
TILE_MASKED = N_BIAS_TILES
TILE_WINDOW_EDGE = N_BIAS_TILES + 1
FAR_TILE_DIST = 8
FAR_CHUNK = 4
VMEM_LIMIT = 52 * 1024 * 1024


def _dot(a, b, **kw):
    return jnp.dot(a, b, preferred_element_type=F32, **kw)


def _dot_nt(a, b, **kw):
    return lax.dot_general(a, b, (((1,), (1,)), ((), ())), preferred_element_type=F32, **kw)


def _gelu(x):
    return 0.5 * x * (1.0 + jnp.tanh(math.sqrt(2.0 / math.pi) * (x + 0.044715 * (x * x * x))))


def _sigmoid(x):
    return 1.0 / (1.0 + jnp.exp(-x))


def _rms(x):
    return x * lax.rsqrt(jnp.mean(x * x, axis=-1, keepdims=True) + EPS)


def _params(*sem):
    return pltpu.CompilerParams(dimension_semantics=sem, vmem_limit_bytes=VMEM_LIMIT)


def _ada_kernel(c_ref, w_ref, b_ref, o_ref):
    c = c_ref[...]
    a = c * _sigmoid(c)
    o_ref[...] = _dot(a, w_ref[...], precision=lax.Precision.HIGHEST) + b_ref[...]


def _ada(c8, w, b):
    d, n = w.shape
    tn = 1536
    return pl.pallas_call(
        _ada_kernel,
        grid=(n // tn,),
        in_specs=[pl.BlockSpec((8, d), lambda j: (0, 0)),
                  pl.BlockSpec((d, tn), lambda j: (0, j)),
                  pl.BlockSpec((1, tn), lambda j: (0, j))],
        out_specs=pl.BlockSpec((8, tn), lambda j: (0, j)),
        out_shape=jax.ShapeDtypeStruct((8, n), F32),
        compiler_params=_params("parallel"),
        name="ada",
    )(c8, w, b)


def _inproj_kernel(x_ref, mod_ref, nw_ref, w_ref, qt_ref, q_ref, kx_ref, vx_ref, kvc_ref, g_ref, xr_ref, gr_ref):
    h = _rms(x_ref[0]) * nw_ref[...]
    h = h * (1.0 + mod_ref[0, 1:2, :]) + mod_ref[0, 0:1, :]
    p = _dot(h.astype(BF16), w_ref[...])
    tm = p.shape[0]
    dq = N_HEADS * HEAD_DIM
    lane = lax.broadcasted_iota(I32, (tm, HEAD_DIM), 1)
    k_tail = jnp.where(lane < 2, 1.0, 0.0)
    v_tail = jnp.where(lane < 1, 1.0, 0.0)
    for hh in range(N_HEADS):
        qh = p[:, hh * HEAD_DIM:(hh + 1) * HEAD_DIM] * (HEAD_DIM ** -0.5 * LOG2E)
        q_tail = jnp.broadcast_to(qt_ref[hh:hh + 1, :], (tm, HEAD_DIM))
        q_t = jnp.concatenate([qh, q_tail], axis=1).T.astype(BF16)
        for blk in range(tm // QB):
            col = (blk * GQA + hh % GQA) * QB
            q_ref[0, hh // GQA, :, col:col + QB] = q_t[:, blk * QB:(blk + 1) * QB]
    for j in range(6):
        for g in range(N_KV):
            col = dq + (j * N_KV + g) * HEAD_DIM
            piece = p[:, col:col + HEAD_DIM]
            if j < 2:
                kvc_ref[0, j * N_KV + g] = piece
            elif j % 2 == 0:
                kx_ref[0, (j // 2 - 1) * N_KV + g] = jnp.concatenate([piece, k_tail], axis=1).astype(BF16)
            else:
                v_t = jnp.concatenate([piece, v_tail], axis=1).T.astype(BF16)
                for blk in range(tm // QB):
                    vx_ref[0, (j // 2 - 1) * N_KV + g, blk] = v_t[:, blk * QB:(blk + 1) * QB]
    c0 = dq + 6 * N_KV * HEAD_DIM
    for g in range(N_KV):
        g_ref[0, g] = _sigmoid(p[:, c0 + g * 128:c0 + (g + 1) * 128]).T[0:16, :]
    xr_ref[0] = p[:, c0 + 256:c0 + 768]
    gr_ref[0] = p[:, c0 + 768:c0 + 1280]


def _inproj(x, mod, nw, w_pad, q_tail, tm=256):
    B, S, D = x.shape
    ncol = w_pad.shape[1]
    heads = lambda n, w: pl.BlockSpec((1, n, tm, w), lambda b, t: (b, 0, t, 0))
    return pl.pallas_call(
        _inproj_kernel,
        grid=(B, S // tm),
        in_specs=[pl.BlockSpec((1, tm, D), lambda b, t: (b, t, 0)),
                  pl.BlockSpec((1, 6, D), lambda b, t: (b, 0, 0)),
                  pl.BlockSpec((1, D), lambda b, t: (0, 0)),
                  pl.BlockSpec((D, ncol), lambda b, t: (0, 0)),
                  pl.BlockSpec(q_tail.shape, lambda b, t: (0, 0))],
        out_specs=[pl.BlockSpec((1, N_KV, 2 * HEAD_DIM, tm * GQA), lambda b, t: (b, 0, 0, t)),
                   heads(4, 2 * HEAD_DIM),
                   pl.BlockSpec((1, 4, tm // QB, 2 * HEAD_DIM, QB), lambda b, t: (b, 0, t, 0, 0)),
                   heads(4, HEAD_DIM),
                   pl.BlockSpec((1, N_KV, 16, tm), lambda b, t: (b, 0, 0, t)),
                   pl.BlockSpec((1, tm, 512), lambda b, t: (b, t, 0)),
                   pl.BlockSpec((1, tm, 512), lambda b, t: (b, t, 0))],
        out_shape=[jax.ShapeDtypeStruct((B, N_KV, 2 * HEAD_DIM, S * GQA), BF16),
                   jax.ShapeDtypeStruct((B, 4, S, 2 * HEAD_DIM), BF16),
                   jax.ShapeDtypeStruct((B, 4, S // QB, 2 * HEAD_DIM, QB), BF16),
                   jax.ShapeDtypeStruct((B, 4, S, HEAD_DIM), F32),
                   jax.ShapeDtypeStruct((B, N_KV, 16, S), F32),
                   jax.ShapeDtypeStruct((B, S, 512), F32),
                   jax.ShapeDtypeStruct((B, S, 512), F32)],
        compiler_params=_params("parallel", "parallel"),
        name="inproj",
    )(x, mod, nw, w_pad, q_tail)


def _compress_kernel(x_ref, pos_ref, w1_ref, w2_ref, o_ref, ot_ref):
    x = x_ref[0, 0]
    nc = x.shape[0]
    half = CMP_STRIDE * HEAD_DIM
    kv = pl.program_id(1) // N_KV
    a = _dot((x + pos_ref[0:1, :]).astype(BF16), w1_ref[kv, 0:half, :].astype(BF16))
    b = _dot((x + pos_ref[1:2, :]).astype(BF16), w1_ref[kv, half:2 * half, :].astype(BF16))
    hid = _gelu(a + pltpu.roll(b, nc - 1, 0))
    out = _dot(hid.astype(BF16), w2_ref[kv].astype(BF16))
    row = lax.broadcasted_iota(I32, out.shape, 0)
    out = jnp.where(row < nc - 1, out, 0.0)
    lane = lax.broadcasted_iota(I32, (nc, HEAD_DIM), 1)
    out = jnp.concatenate([out, jnp.where(lane < 2, 1.0, 0.0)], axis=1)
    o_ref[0, 0, 0:CMP_PAD, :] = jnp.zeros((CMP_PAD, 2 * HEAD_DIM), F32)
    o_ref[0, 0, CMP_PAD:CMP_PAD + nc, :] = out
    o_ref[0, 0, CMP_PAD + nc:, :] = jnp.zeros((CMP_WIN - CMP_PAD, 2 * HEAD_DIM), F32)
    ot_ref[0, 0] = out.T


def _compress(kvc, pos2, w1, w2):
    B, _, NC, W = kvc.shape
    return pl.pallas_call(
        _compress_kernel,
        grid=(B, 4),
        in_specs=[pl.BlockSpec((1, 1, NC, W), lambda b, i: (b, i, 0, 0)),
                  pl.BlockSpec((2, W), lambda b, i: (0, 0)),
                  pl.BlockSpec((2, 2 * W, HEAD_DIM), lambda b, i: (0, 0, 0)),
                  pl.BlockSpec((2, HEAD_DIM, HEAD_DIM), lambda b, i: (0, 0, 0))],
        out_specs=[pl.BlockSpec((1, 1, NC + CMP_WIN, 2 * HEAD_DIM), lambda b, i: (b, i, 0, 0)),
                   pl.BlockSpec((1, 1, 2 * HEAD_DIM, NC), lambda b, i: (b, i, 0, 0))],
        out_shape=[jax.ShapeDtypeStruct((B, 4, NC + CMP_WIN, 2 * HEAD_DIM), F32),
                   jax.ShapeDtypeStruct((B, 4, 2 * HEAD_DIM, NC), F32)],
        compiler_params=_params("parallel", "parallel"),
        name="compress",
    )(kvc, pos2, w1, w2)


def _attn_kernel(q_ref, ks_ref, vs_ref, kw_ref, vw_ref, kc_ref, vc_ref, vct_ref, g_ref, t_ref, w_ref, ovt_ref, ov_ref,
                 o_ref, m_scr, acc_scr, sel_scr, sa_scr, sb_scr, sc_scr):
    qb = pl.program_id(2)
    R = GQA * QB
    q = q_ref[0, 0]
    nc = vct_ref.shape[3]
    nb = ov_ref.shape[1]
    n_top = min(SLC_TOP, ks_ref.shape[2] // SLC_LEN)

    def heads_sum(x):
        out = x[:, 0:QB]
        for hq in range(1, GQA):
            out = out + x[:, hq * QB:(hq + 1) * QB]
        return out

    w0 = pl.multiple_of(qb * 8, 8)
    s_far = _dot(kc_ref[0, 0, CMP_PAD:CMP_PAD + nc, :].astype(BF16), q)
    n_io = lax.broadcasted_iota(I32, (nc, R), 0)
    s_far = jnp.where(n_io < qb * 8 - CMP_PAD, s_far, -jnp.inf)
    s_win = _dot(kc_ref[0, 0, pl.ds(w0, CMP_WIN), :].astype(BF16), q) + w_ref[0]
    j_w = lax.broadcasted_iota(I32, (CMP_WIN, R), 0)
    i_w = lax.broadcasted_iota(I32, (CMP_WIN, R), 1) & (QB - 1)
    dist_w = i_w - CMP_STRIDE * (j_w - CMP_PAD) - (CMP_LEN - 1)
    s_win = jnp.where(dist_w >= 0, jnp.where(j_w >= CMP_PAD - qb * 8, s_win, -jnp.inf), -jnp.inf)
    m = jnp.maximum(jnp.max(s_far, axis=0, keepdims=True), jnp.max(s_win, axis=0, keepdims=True))
    m = jnp.where(m == -jnp.inf, 0.0, m)
    e_far = jnp.exp2(s_far - m)
    e_win = jnp.exp2(s_win - m)
    l = jnp.sum(e_far, axis=0, keepdims=True) + jnp.sum(e_win, axis=0, keepdims=True)
    inv = 1.0 / jnp.maximum(l, 1e-30)
    p_far = e_far * inv
    p_win = e_win * inv
    vcw_t = vc_ref[0, 0, pl.ds(w0, CMP_WIN), :].T
    o_c = _dot(vct_ref[0, 0].astype(BF16), p_far.astype(BF16)) + _dot(vcw_t.astype(BF16), p_win.astype(BF16))
    ov_win_t = ov_ref[pl.ds(w0, CMP_WIN), :].T
    sc_t = (_dot(ovt_ref[...], heads_sum(p_far).astype(BF16))
            + _dot(ov_win_t.astype(BF16), heads_sum(p_win).astype(BF16)))

    n_wt = WINDOW // QB + 1
    k_t, v_t, b_t = [], [], []
    for j in range(n_wt):
        kb = qb - (n_wt - 1) + j
        kb0 = jnp.maximum(kb, 0)
        k_t.append(kw_ref[0, 0, pl.ds(pl.multiple_of(kb0 * QB, QB), QB), :])
        v_t.append(vw_ref[0, 0, kb0])
        b_t.append(t_ref[0, jnp.where(kb < 0, TILE_MASKED, TILE_WINDOW_EDGE if j == 0 else n_wt - 1 - j)])
    s = _dot(jnp.concatenate(k_t, axis=0), q) + jnp.concatenate(b_t, axis=0)
    p = jnp.exp2(s - jnp.max(s, axis=0, keepdims=True))
    acc = _dot(jnp.concatenate(v_t, axis=1), p.astype(BF16))
    o_w = acc[0:HEAD_DIM, :] / acc[HEAD_DIM:HEAD_DIM + 1, :]

    chunk = FAR_CHUNK * QB

    def qk(c):
        return _dot(ks_ref[0, 0, pl.ds(pl.multiple_of(c * chunk, chunk), chunk), :], q)

    n_far = jnp.maximum(qb - (FAR_TILE_DIST - 1), 0) // FAR_CHUNK
    sa_scr[...] = qk(0)
    sc_scr[...] = qk(n_far)

    blk = lax.broadcasted_iota(I32, (nb, QB), 0)
    t_io = lax.broadcasted_iota(I32, (nb, QB), 1)
    cur = qb * (QB // SLC_LEN) + t_io // SLC_LEN
    val = jnp.where(blk <= cur, sc_t, -jnp.inf)
    for forced_blk in (0, cur, cur - 1):
        val = jnp.where(blk == forced_blk, jnp.inf, val)
    drop = jnp.full((nb, QB), NEG, F32)
    for _ in range(n_top):
        mx = jnp.max(val, axis=0, keepdims=True)
        first = jnp.min(jnp.where(val == mx, blk, nb), axis=0, keepdims=True)
        first = jnp.where(mx > -jnp.inf, first, nb)
        pick = blk == first
        drop = jnp.where(pick, 0.0, drop)
        val = jnp.where(pick, -jnp.inf, val)
    sel_scr[...] = drop

    m_scr[...] = jnp.full(m_scr.shape, NEG, F32)
    acc_scr[...] = jnp.zeros(acc_scr.shape, F32)
    blocks_per_chunk = FAR_CHUNK * QB // SLC_LEN

    def soft(c, s):
        rows = sel_scr[pl.ds(pl.multiple_of(c * blocks_per_chunk, blocks_per_chunk), blocks_per_chunk), :]
        drop = jnp.concatenate([jnp.broadcast_to(rows[r:r + 1, :], (SLC_LEN, QB)) for r in range(blocks_per_chunk)],
                               axis=0)
        s = s + jnp.concatenate([drop] * GQA, axis=1)
        m_old = m_scr[...]
        m_new = jnp.maximum(m_old, jnp.max(s, axis=0, keepdims=True))
        p = jnp.exp2(s - m_new)
        v_t = jnp.concatenate([vs_ref[0, 0, c * FAR_CHUNK + j] for j in range(FAR_CHUNK)], axis=1)
        acc_scr[...] = jnp.exp2(m_old - m_new) * acc_scr[...] + _dot(v_t, p.astype(BF16))
        m_scr[...] = m_new

    def near_bias(c):
        tiles = []
        for j in range(FAR_CHUNK):
            delta = qb - (c * FAR_CHUNK + j)
            tiles.append(t_ref[0, jnp.where(delta < 0, TILE_MASKED, delta)])
        return jnp.concatenate(tiles, axis=0)

    def chunk_pairs(first, count, bias, sa):
        last = first + jnp.maximum(count - 1, 0)

        def pair_body(j, carry):
            c = first + 2 * j
            sb_scr[...] = qk(c + 1)
            soft(c, sa[...] + bias(c) if bias else sa[...])
            sa[...] = qk(jnp.minimum(c + 2, last))
            soft(c + 1, sb_scr[...] + bias(c + 1) if bias else sb_scr[...])
            return carry

        lax.fori_loop(0, count // 2, pair_body, 0)

        @pl.when(count % 2 == 1)
        def _():
            soft(last, sa[...] + bias(last) if bias else sa[...])

    chunk_pairs(0, n_far, None, sa_scr)
    chunk_pairs(n_far, qb // FAR_CHUNK + 1 - n_far, near_bias, sc_scr)
    acc = acc_scr[...]
    o_s = acc[0:HEAD_DIM, :] / acc[HEAD_DIM:HEAD_DIM + 1, :]

    gates = g_ref[0, 0]
    outs = []
    for hq in range(GQA):
        cols = slice(hq * QB, (hq + 1) * QB)
        c = hq * N_BRANCH
        outs.append(gates[c:c + 1, :] * o_c[0:HEAD_DIM, cols] + gates[c + 1:c + 2, :] * o_s[:, cols]
                    + gates[c + 2:c + 3, :] * o_w[:, cols])
    o_ref[0] = jnp.concatenate(outs, axis=0).T


def _attn(q_t, kx, vx_t, kvc, kvc_t, gates_t, t_tab, w_tab, ov_t, ov):
    B, _, W, _ = q_t.shape
    S = kx.shape[2]
    ncp = kvc.shape[2]
    nb = ov.shape[1]
    keys = lambda j: pl.BlockSpec((1, 1, S, W), lambda b, g, t: (b, j * N_KV + g, 0, 0))
    vals = lambda j: pl.BlockSpec((1, 1, S // QB, W, QB), lambda b, g, t: (b, j * N_KV + g, 0, 0, 0))
    cmp = lambda j: pl.BlockSpec((1, 1, ncp, W), lambda b, g, t: (b, j * N_KV + g, 0, 0))
    return pl.pallas_call(
        _attn_kernel,
        grid=(B, N_KV, S // QB),
        in_specs=[pl.BlockSpec((1, 1, W, GQA * QB), lambda b, g, t: (b, g, 0, t)),
                  keys(0), vals(0), keys(1), vals(1), cmp(0), cmp(1),
                  pl.BlockSpec((1, 1, W, ncp - CMP_WIN), lambda b, g, t: (b, N_KV + g, 0, 0)),
                  pl.BlockSpec((1, 1, 16, QB), lambda b, g, t: (b, g, 0, t)),
                  pl.BlockSpec((1,) + t_tab.shape[1:], lambda b, g, t: (g, 0, 0, 0)),
                  pl.BlockSpec((1, CMP_WIN, GQA * QB), lambda b, g, t: (g, 0, 0)),
                  pl.BlockSpec(ov_t.shape, lambda b, g, t: (0, 0)),
                  pl.BlockSpec(ov.shape, lambda b, g, t: (0, 0))],
        out_specs=pl.BlockSpec((1, QB, GQA * HEAD_DIM), lambda b, g, t: (b, t, g)),
        out_shape=jax.ShapeDtypeStruct((B, S, N_HEADS * HEAD_DIM), F32),
        scratch_shapes=[pltpu.VMEM((1, GQA * QB), F32), pltpu.VMEM((W, GQA * QB), F32), pltpu.VMEM((nb, QB), F32),
                        pltpu.VMEM((FAR_CHUNK * QB, GQA * QB), F32), pltpu.VMEM((FAR_CHUNK * QB, GQA * QB), F32),
                        pltpu.VMEM((FAR_CHUNK * QB, GQA * QB), F32)],
        compiler_params=_params("parallel", "parallel", "arbitrary"),
        name="attn",
    )(q_t, kx, vx_t, kx, vx_t, kvc, kvc, kvc_t, gates_t, t_tab, w_tab, ov_t, ov)


def _rglru_kernel(xr_ref, gr_ref, cw_ref, cb_ref, wa_ref, ba_ref, wx_ref, bx_ref, lam_ref, gw_ref, o_ref,
                  xbuf, hprev, a_scr, u_scr, h_scr):
    ts = xr_ref.shape[1]
    C = xr_ref.shape[2]

    @pl.when(pl.program_id(1) == 0)
    def _():
        xbuf[0:8, :] = jnp.zeros((8, C), F32)
        hprev[...] = jnp.zeros(hprev.shape, F32)

    xbuf[8:8 + ts, :] = xr_ref[0]
    xc = cb_ref[...] + jnp.zeros((ts, C), F32)
    for j in range(CONV_WIDTH):
        xc = xc + cw_ref[j:j + 1, :] * xbuf[pl.ds(8 - (CONV_WIDTH - 1) + j, ts), :]
    xbuf[0:8, :] = xbuf[ts:ts + 8, :]

    xcb = xc.astype(BF16)
    r = _sigmoid(_dot(xcb, wa_ref[...]) + ba_ref[...])
    i = _sigmoid(_dot(xcb, wx_ref[...]) + bx_ref[...])
    z = -lam_ref[...]
    softplus = jnp.maximum(z, 0.0) + jnp.log(1.0 + jnp.exp(-jnp.abs(z)))
    log_a = -LRU_C * r * softplus
    a_scr[...] = jnp.exp(log_a)
    u_scr[...] = jnp.sqrt(1.0 - jnp.exp(2.0 * log_a)) * (i * xc)

    row = lax.broadcasted_iota(I32, (8, C), 0)

    def body(k, h):
        r0 = pl.multiple_of(k * 8, 8)
        a = a_scr[pl.ds(r0, 8), :]
        b = u_scr[pl.ds(r0, 8), :]
        for s in (1, 2, 4):
            keep = row >= s
            b = jnp.where(keep, a * pltpu.roll(b, s, 0) + b, b)
            a = jnp.where(keep, a * pltpu.roll(a, s, 0), a)
        hh = a * h + b
        h_scr[pl.ds(r0, 8), :] = hh
        return jnp.broadcast_to(hh[7:8, :], (8, C))

    hprev[...] = lax.fori_loop(0, ts // 8, body, hprev[...])
    out = h_scr[...] * _gelu(gr_ref[0])
    o_ref[0] = (_rms(out) * gw_ref[...]).astype(BF16)


def _rglru(xr, gr, cw, cb, wa_bd, ba, wx_bd, bx, lam, gw, ts=512):
    B, S, C = xr.shape
    ts = min(ts, S)
    vec = pl.BlockSpec((1, C), lambda b, t: (0, 0))
    mat = pl.BlockSpec((C, C), lambda b, t: (0, 0))
    seq = pl.BlockSpec((1, ts, C), lambda b, t: (b, t, 0))
    return pl.pallas_call(
        _rglru_kernel,
        grid=(B, S // ts),
        in_specs=[seq, seq, pl.BlockSpec((CONV_WIDTH, C), lambda b, t: (0, 0)), vec, mat, vec, mat, vec, vec, vec],
        out_specs=seq,
        out_shape=jax.ShapeDtypeStruct((B, S, C), BF16),
        scratch_shapes=[pltpu.VMEM((ts + 8, C), F32), pltpu.VMEM((8, C), F32), pltpu.VMEM((ts, C), F32),
                        pltpu.VMEM((ts, C), F32), pltpu.VMEM((ts, C), F32)],
        compiler_params=_params("parallel", "arbitrary"),
        name="rglru",
    )(xr, gr, cw, cb, wa_bd, ba, wx_bd, bx, lam, gw)


def _to_row_tiles(ref, x):
    rows = x.shape[0]
    for s in range(ROW_TILE):
        ref[pl.ds(s, rows, stride=ROW_TILE), :] = x[:, s * 128:(s + 1) * 128]


def _from_row_tiles(ref, rows, start=0):
    return jnp.concatenate([ref[pl.ds(start + s, rows, stride=ROW_TILE), :] for s in range(ROW_TILE)], axis=1)


def _outproj_kernel(oa_ref, yr_ref, x_ref, mod_ref, gaw_ref, wo_ref, n2_ref, wr_ref, x1_ref, h2_ref, st_ref):
    da = oa_ref.shape[2]
    ya = (_rms(oa_ref[0]) * gaw_ref[...]).astype(BF16)
    mix = _dot(ya, wo_ref[0:da, :]) + _dot(yr_ref[0], wo_ref[da:, :])
    x1 = x_ref[0] + mod_ref[0, 2:3, :] * mix
    x1_ref[0] = x1
    h2 = (_rms(x1) * n2_ref[...]) * (1.0 + mod_ref[0, 4:5, :]) + mod_ref[0, 3:4, :]
    h2_ref[0] = h2.astype(BF16)
    st_ref[...] = _sigmoid(_dot_nt(wr_ref[...], h2, precision=lax.Precision.HIGHEST))


def _outproj(oa, yr, x, mod, gaw, wo, n2, wr_t, tm=256):
    B, S, D = x.shape
    E = wr_t.shape[0]
    nt = S // tm
    row = lambda w: pl.BlockSpec((1, tm, w), lambda b, t: (b, t, 0))
    return pl.pallas_call(
        _outproj_kernel,
        grid=(B, nt),
        in_specs=[row(oa.shape[2]), row(yr.shape[2]), row(D),
                  pl.BlockSpec((1, 6, D), lambda b, t: (b, 0, 0)),
                  pl.BlockSpec((1, oa.shape[2]), lambda b, t: (0, 0)),
                  pl.BlockSpec(wo.shape, lambda b, t: (0, 0)),
                  pl.BlockSpec((1, D), lambda b, t: (0, 0)),
                  pl.BlockSpec((E, D), lambda b, t: (0, 0))],
        out_specs=[row(D), row(D), pl.BlockSpec((E, tm), lambda b, t: (0, b * nt + t))],
        out_shape=[jax.ShapeDtypeStruct((B, S, D), F32), jax.ShapeDtypeStruct((B, S, D), BF16),
                   jax.ShapeDtypeStruct((E, B * S), F32)],
        compiler_params=_params("parallel", "parallel"),
        name="outproj",
    )(oa, yr, x, mod, gaw, wo, n2, wr_t)


def _route_kernel(s_ref, b_ref, eid_ref, wgt_ref, rank_ref, cnt_ref, tcnt_ref, carry):
    E, tn = s_ref.shape
    per = E // N_EXPERT_GROUPS

    @pl.when(pl.program_id(0) == 0)
    def _():
        carry[...] = jnp.zeros(carry.shape, F32)

    s = s_ref[...]
    s_sel = s + b_ref[...]
    eidx = lax.broadcasted_iota(I32, (E, tn), 0)

    grp = []
    for gi in range(N_EXPERT_GROUPS):
        xg = s_sel[gi * per:(gi + 1) * per, :]
        ig = lax.broadcasted_iota(I32, (per, tn), 0)
        m1 = jnp.max(xg, axis=0, keepdims=True)
        f1 = jnp.min(jnp.where(xg == m1, ig, per), axis=0, keepdims=True)
        m2 = jnp.max(jnp.where(ig == f1, -jnp.inf, xg), axis=0, keepdims=True)
        grp.append(m1 + m2)
    val = []
    for gi in range(N_EXPERT_GROUPS):
        rank = jnp.zeros((1, tn), I32)
        for gj in range(N_EXPERT_GROUPS):
            if gj == gi:
                continue
            ahead = (grp[gj] > grp[gi]) | ((grp[gj] == grp[gi]) & (gj < gi))
            rank = rank + ahead.astype(I32)
        val.append(jnp.where(rank < TOP_GROUPS, s_sel[gi * per:(gi + 1) * per, :], -jnp.inf))
    val = jnp.concatenate(val, axis=0)

    eids, tops = [], []
    hot = jnp.zeros((E, tn), F32)
    for _ in range(TOP_K):
        mx = jnp.max(val, axis=0, keepdims=True)
        first = jnp.min(jnp.where(val == mx, eidx, E), axis=0, keepdims=True)
        pick = eidx == first
        eids.append(first)
        tops.append(jnp.sum(jnp.where(pick, s, 0.0), axis=0, keepdims=True))
        hot = jnp.where(pick, 1.0, hot)
        val = jnp.where(pick, -jnp.inf, val)
    denom = tops[0]
    for t in tops[1:]:
        denom = denom + t

    ti = lax.broadcasted_iota(I32, (tn, tn), 0)
    tj = lax.broadcasted_iota(I32, (tn, tn), 1)
    upper = jnp.where(ti < tj, 1.0, 0.0).astype(BF16)
    before = _dot(hot.astype(BF16), upper) + carry[...]
    for k in range(TOP_K):
        eid_ref[k:k + 1, :] = eids[k]
        wgt_ref[k:k + 1, :] = ROUTED_SCALE * tops[k] / denom
        rank_ref[k:k + 1, :] = jnp.sum(jnp.where(eidx == eids[k], before, 0.0), axis=0, keepdims=True).astype(I32)
    tile_count = jnp.sum(hot, axis=1, keepdims=True)
    tcnt_ref[0] = tile_count
    carry[...] = carry[...] + tile_count
    cnt_ref[...] = carry[...]


def _route(s_t, b_col, tn=256):
    E, N = s_t.shape
    tn = min(tn, N)
    out = pl.BlockSpec((TOP_K, tn), lambda t: (0, t))
    return pl.pallas_call(
        _route_kernel,
        grid=(N // tn,),
        in_specs=[pl.BlockSpec((E, tn), lambda t: (0, t)), pl.BlockSpec((E, 1), lambda t: (0, 0))],
        out_specs=[out, out, out, pl.BlockSpec((E, 1), lambda t: (0, 0)), pl.BlockSpec((1, E, 1), lambda t: (t, 0, 0))],
        out_shape=[jax.ShapeDtypeStruct((TOP_K, N), I32), jax.ShapeDtypeStruct((TOP_K, N), F32),
                   jax.ShapeDtypeStruct((TOP_K, N), I32), jax.ShapeDtypeStruct((E, 1), F32),
                   jax.ShapeDtypeStruct((N // tn, E, 1), F32)],
        scratch_shapes=[pltpu.VMEM((E, 1), F32)],
        compiler_params=_params("arbitrary"),
        name="route",
    )(s_t, b_col)


def _tilepos_kernel(eid_ref, rank_ref, base_ref, o_ref):
    E = base_ref.shape[1]
    tn = eid_ref.shape[1]
    eidx = lax.broadcasted_iota(I32, (E, tn), 0)
    for k in range(TOP_K):
        base = jnp.sum(jnp.where(eidx == eid_ref[k:k + 1, :], base_ref[0], 0.0), axis=0, keepdims=True)
        o_ref[k:k + 1, :] = base.astype(I32) + rank_ref[k:k + 1, :]


def _tilepos(eid_t, rank_t, base, tn):
    K, N = eid_t.shape
    blk = pl.BlockSpec((K, tn), lambda t: (0, t))
    return pl.pallas_call(
        _tilepos_kernel,
        grid=(N // tn,),
        in_specs=[blk, blk, pl.BlockSpec((1,) + base.shape[1:], lambda t: (t, 0, 0))],
        out_specs=blk,
        out_shape=jax.ShapeDtypeStruct((K, N), I32),
        compiler_params=_params("parallel"),
        name="tilepos",
    )(eid_t, rank_t, base)


_HI = -65536


def _pack_pairs(even_rows, odd_rows):
    hi = lax.bitcast_convert_type(even_rows, I32) & _HI
    lo = lax.shift_right_logical(lax.bitcast_convert_type(odd_rows, I32), 16)
    return hi | lo


def _unpack_pairs(words):
    even_rows = lax.bitcast_convert_type(words & _HI, F32)
    odd_rows = lax.bitcast_convert_type(lax.shift_left(words, 16), F32)
    return even_rows, odd_rows


def _run_copies(src_ref, cnt_ref, dst_ref, n_runs, copy):
    def go(e, carry):
        n = cnt_ref[0, 0, e]

        @pl.when(n > 0)
        def _():
            size = pl.multiple_of(n * ROW_TILE, ROW_TILE)
            copy(pl.ds(pl.multiple_of(src_ref[0, 0, e] * ROW_TILE, ROW_TILE), size),
                 pl.ds(pl.multiple_of(dst_ref[0, 0, e] * ROW_TILE, ROW_TILE), size)).start()
        return carry
    lax.fori_loop(0, n_runs, go, 0, unroll=4)


def _pair_onehots(pos_rows, n_pairs, first_pair, width):
    even = 2 * (lax.broadcasted_iota(I32, (n_pairs, width), 0) + first_pair)
    hot_e = jnp.zeros((n_pairs, width), F32)
    hot_o = jnp.zeros((n_pairs, width), F32)
    for k in range(TOP_K):
        hot_e = jnp.where(even == pos_rows[k:k + 1, :], 1.0, hot_e)
        hot_o = jnp.where(even + 1 == pos_rows[k:k + 1, :], 1.0, hot_o)
    return hot_e, hot_o


def _dispatch_kernel(src_ref, cnt_ref, dst_ref, pos_ref, h_ref, xs_hbm, xbuf, sem):
    n_pairs = xbuf.shape[0] // ROW_TILE
    tn = h_ref.shape[0]
    hb = h_ref[...]
    slab = n_pairs // 3
    for sl in range(3):
        hot_e, hot_o = _pair_onehots(pos_ref[...], slab, sl * slab, tn)
        words = _pack_pairs(_dot(hot_e.astype(BF16), hb), _dot(hot_o.astype(BF16), hb))
        for s in range(ROW_TILE):
            xbuf[pl.ds(sl * slab * ROW_TILE + s, slab, stride=ROW_TILE), :] = words[:, s * 128:(s + 1) * 128]
    _run_copies(src_ref, cnt_ref, dst_ref, cnt_ref.shape[2],
                lambda src, dst: pltpu.make_async_copy(xbuf.at[src], xs_hbm.at[dst], sem.at[0]))
    pltpu.make_async_copy(xbuf, xs_hbm.at[pl.ds(0, xbuf.shape[0])], sem.at[0]).wait()


def _run_specs(runs):
    return [pl.BlockSpec((1, 1, r.shape[2]), lambda t: (t, 0, 0), memory_space=pltpu.SMEM) for r in runs]


def _tile_pairs(tn, n_experts):
    return (TOP_K * tn + n_experts) // 2


def _dispatch(runs, pos_t, h2, tn):
    N, D = h2.shape
    W = D // ROW_TILE
    n_pairs = _tile_pairs(tn, runs[0].shape[2] - 1)
    assert n_pairs % 24 == 0
    return pl.pallas_call(
        _dispatch_kernel,
        grid=(N // tn,),
        in_specs=_run_specs(runs) + [pl.BlockSpec((TOP_K, tn), lambda t: (0, t)),
                                     pl.BlockSpec((tn, D), lambda t: (t, 0))],
        out_specs=pl.BlockSpec(memory_space=pl.ANY),
        out_shape=jax.ShapeDtypeStruct((N // tn * n_pairs * ROW_TILE, W), I32),
        scratch_shapes=[pltpu.VMEM((n_pairs * ROW_TILE, W), I32), pltpu.SemaphoreType.DMA((1,))],
        compiler_params=_params("arbitrary"),
        name="dispatch",
    )(*runs, pos_t, h2)


def _experts_kernel(first_ref, size_ref, e_ref, new_ref, next_ref, wslot_ref, total_ref,
                    w1_hbm, w3_hbm, w2_hbm, xs_hbm, ys_hbm,
                    xbuf, ybuf, w1f, w3f, w2f, w1b, w3b, w2b, xsem, ysem, wsem):
    total = total_ref[0]

    def w_copies(e, slot):
        return [pltpu.make_async_copy(src.at[e], dst.at[slot], wsem.at[slot])
                for src, dst in ((w1_hbm, w1f), (w3_hbm, w3f), (w2_hbm, w2f))]

    def rows(c):
        return (pl.multiple_of(first_ref[c] * ROW_TILE, ROW_TILE), pl.multiple_of(size_ref[c] * ROW_TILE, ROW_TILE))

    def x_copy(c, slot):
        first, size = rows(c)
        return pltpu.make_async_copy(xs_hbm.at[pl.ds(first, size)], xbuf.at[slot, pl.ds(0, size)], xsem.at[slot])

    def y_copy(c, slot):
        first, size = rows(c)
        return pltpu.make_async_copy(ybuf.at[slot, pl.ds(0, size)], ys_hbm.at[pl.ds(first, size)], ysem.at[slot])

    xbuf[...] = jnp.zeros(xbuf.shape, I32)
    for cp in w_copies(e_ref[0], 0):
        cp.start()
    x_copy(0, 0).start()
    half = MOE_BLOCK // 2

    def body(c, carry):
        slot = c % 2

        @pl.when(new_ref[c] == 1)
        def _():
            ws = wslot_ref[c]
            for cp in w_copies(e_ref[c], ws):
                cp.wait()
            w1b[...] = w1f[ws].astype(BF16)
            w3b[...] = w3f[ws].astype(BF16)
            w2b[...] = w2f[ws].astype(BF16)

            @pl.when(next_ref[c] >= 0)
            def _():
                for cp in w_copies(next_ref[c], 1 - ws):
                    cp.start()

        x_copy(c, slot).wait()

        @pl.when(c + 1 < total)
        def _():
            x_copy(c + 1, 1 - slot).start()

        x = jnp.concatenate(_unpack_pairs(_from_row_tiles(xbuf.at[slot], half)), axis=0).astype(BF16)
        h1 = _dot(x, w1b[...])
        h3 = _dot(x, w3b[...])
        hid = (h1 * _sigmoid(h1)) * h3
        y = _dot(hid.astype(BF16), w2b[...]).astype(BF16).astype(F32)

        @pl.when(c >= 2)
        def _():
            y_copy(c - 2, slot).wait()

        _to_row_tiles(ybuf.at[slot], _pack_pairs(y[0:half], y[half:]))
        y_copy(c, slot).start()
        return carry

    lax.fori_loop(0, total, body, 0)

    @pl.when(total >= 2)
    def _():
        y_copy(total - 2, total % 2).wait()

    y_copy(total - 1, (total - 1) % 2).wait()

    ybuf[0] = jnp.zeros(ybuf.shape[1:], I32)
    tail_first = total_ref[1]
    tail_pairs = ys_hbm.shape[0] // ROW_TILE - tail_first

    def fill(j, carry):
        size = pl.multiple_of(jnp.minimum(half, tail_pairs - j * half) * ROW_TILE, ROW_TILE)
        cp = pltpu.make_async_copy(ybuf.at[0, pl.ds(0, size)],
                                   ys_hbm.at[pl.ds(pl.multiple_of((tail_first + j * half) * ROW_TILE, ROW_TILE), size)],
                                   ysem.at[0])
        cp.start()
        cp.wait()
        return carry

    lax.fori_loop(0, (tail_pairs + half - 1) // half, fill, 0)


def _expert_chunks(counts, n_pairs_total):
    E = counts.shape[0]
    block = MOE_BLOCK // 2
    n_max = n_pairs_total // block + E
    starts = jnp.cumsum(counts) - counts
    n_chunks = (counts + block - 1) // block
    c_end = jnp.cumsum(n_chunks)
    c = jnp.arange(n_max, dtype=I32)
    e = jnp.minimum(jnp.sum(c_end[None, :] <= c[:, None], axis=1), E - 1).astype(I32)
    j = c - (c_end - n_chunks)[e]
    size = jnp.clip(counts[e] - j * block, 0, block)
    has_rows = counts > 0
    ids = jnp.where(has_rows, jnp.arange(E, dtype=I32), E)
    later = jnp.flip(lax.cummin(jnp.flip(ids)))
    nxt = jnp.concatenate([later[1:], jnp.full((1,), E, I32)])
    nxt = jnp.where(nxt >= E, -1, nxt)
    ordinal = jnp.cumsum(has_rows.astype(I32)) - 1
    to_i32 = lambda a: a.astype(I32)
    meta = jnp.stack([c_end[-1], jnp.sum(counts)])
    return tuple(map(to_i32, (starts[e] + j * block, size, e, j == 0, nxt[e], ordinal[e] % 2, meta)))


def _experts(chunks, xs, w1, w3, w2):
    rows_total, W = xs.shape
    _, D, De = w1.shape
    any_space = pl.BlockSpec(memory_space=pl.ANY)
    block_rows = MOE_BLOCK // 2 * ROW_TILE
    grid_spec = pltpu.PrefetchScalarGridSpec(
        num_scalar_prefetch=len(chunks),
        grid=(1,),
        in_specs=[any_space] * 4,
        out_specs=any_space,
        scratch_shapes=[pltpu.VMEM((2, block_rows, W), I32), pltpu.VMEM((2, block_rows, W), I32),
                        pltpu.VMEM((2, D, De), F32), pltpu.VMEM((2, D, De), F32), pltpu.VMEM((2, De, D), F32),
                        pltpu.VMEM((D, De), BF16), pltpu.VMEM((D, De), BF16), pltpu.VMEM((De, D), BF16),
                        pltpu.SemaphoreType.DMA((2,)), pltpu.SemaphoreType.DMA((2,)), pltpu.SemaphoreType.DMA((2,))],
    )
    return pl.pallas_call(
        _experts_kernel,
        grid_spec=grid_spec,
        out_shape=jax.ShapeDtypeStruct((rows_total, W), I32),
        compiler_params=_params("arbitrary"),
        name="experts",
    )(*chunks, w1, w3, w2, xs)


def _combine_kernel(src_ref, cnt_ref, dst_ref, ys_hbm, post_ref, wt_ref, h2_ref, x1_ref, mod_ref, ws1_ref,
                    ws3_ref, ws2_ref, fw_ref, o_ref, ybuf, sem):
    tm = h2_ref.shape[0]
    n_pairs = ybuf.shape[0] // ROW_TILE
    n_exp = cnt_ref.shape[2] - 1

    @pl.when(pl.program_id(0) == 0)
    def _():
        ybuf[...] = jnp.zeros(ybuf.shape, I32)

    _run_copies(src_ref, cnt_ref, dst_ref, n_exp,
                lambda src, dst: pltpu.make_async_copy(ys_hbm.at[dst], ybuf.at[src], sem.at[0]))

    h = h2_ref[...]
    a = _dot(h, ws1_ref[...])
    hid = (a * _sigmoid(a)) * _dot(h, ws3_ref[...])
    shared = _dot(hid.astype(BF16), ws2_ref[...])

    pair = 2 * lax.broadcasted_iota(I32, (n_pairs, tm), 0)
    w_cols, mine = [], []
    for odd in range(2):
        w_rows = jnp.zeros((n_pairs, tm), F32)
        for k in range(TOP_K):
            w_rows = jnp.where(pair + odd == post_ref[k:k + 1, :], wt_ref[k:k + 1, :], w_rows)
        w_cols.append(jnp.sum(w_rows, axis=1, keepdims=True))
        mine.append(jnp.where(w_rows != 0.0, 1.0, 0.0).astype(BF16))
    used = pl.multiple_of((src_ref[0, 0, n_exp]) * ROW_TILE, ROW_TILE)
    pltpu.make_async_copy(ys_hbm.at[pl.ds(0, used)], ybuf.at[pl.ds(0, used)], sem.at[0]).wait()
    y_even, y_odd = _unpack_pairs(_from_row_tiles(ybuf, n_pairs))
    sum_rows = lambda hot, y: lax.dot_general(hot, y, (((0,), (0,)), ((), ())), preferred_element_type=F32)
    routed = (sum_rows(mine[0], (y_even * w_cols[0]).astype(BF16))
              + sum_rows(mine[1], (y_odd * w_cols[1]).astype(BF16)))
    x2 = x1_ref[...] + mod_ref[0, 5:6, :] * (routed + shared)
    o_ref[...] = _rms(x2) * fw_ref[...]


def _combine(runs, ys, pos_t, wgt_t, h2, x1, mod, ws1, ws3, ws2, fw, seq, tm):
    N, D = h2.shape
    nt = seq // tm
    row = pl.BlockSpec((tm, D), lambda t: (t, 0))
    full = lambda a: pl.BlockSpec(a.shape, lambda t: (0,) * a.ndim)
    return pl.pallas_call(
        _combine_kernel,
        grid=(N // tm,),
        in_specs=_run_specs(runs) + [pl.BlockSpec(memory_space=pl.ANY),
                                     pl.BlockSpec((TOP_K, tm), lambda t: (0, t)),
                                     pl.BlockSpec((TOP_K, tm), lambda t: (0, t)),
                                     row, row,
                                     pl.BlockSpec((1, 6, D), lambda t: (t // nt, 0, 0)),
                                     full(ws1), full(ws3), full(ws2), full(fw)],
        out_specs=row,
        out_shape=jax.ShapeDtypeStruct((N, D), F32),
        scratch_shapes=[pltpu.VMEM((_tile_pairs(tm, runs[0].shape[2] - 1) * ROW_TILE, D // ROW_TILE), I32),
                        pltpu.SemaphoreType.DMA((1,))],
        compiler_params=_params("arbitrary"),
        name="combine",
    )(*runs, ys, pos_t, wgt_t, h2, x1, mod, ws1, ws3, ws2, fw)


def _t5_bucket(dist):
    n = jnp.maximum(dist, 0)
    max_exact = REL_BUCKETS // 2
    nf = jnp.maximum(n, 1).astype(F32)
    large = max_exact + (jnp.log(nf / max_exact) / math.log(REL_MAX_DIST / max_exact)
                         * (REL_BUCKETS - max_exact)).astype(I32)
    large = jnp.minimum(large, REL_BUCKETS - 1)
    return jnp.where(n < max_exact, n, large)


def _bias_kernel(tbl_ref, far_ref, bt_ref, bw_ref, t_ref, w_ref):
    g = pl.program_id(0)
    d = pl.program_id(1)

    def lookup(bkt, head):
        out = jnp.full(bkt.shape, tbl_ref[0, head], F32)
        for b in range(1, REL_BUCKETS):
            out = jnp.where(bkt == b, tbl_ref[b, head], out)
        return out - far_ref[head]

    j = lax.broadcasted_iota(I32, (QB, QB), 0)
    i = lax.broadcasted_iota(I32, (QB, QB), 1)
    keep = ((d != 0) | (i >= j)) & (d != TILE_MASKED) & ((d != TILE_WINDOW_EDGE) | (j > i))
    for hq in range(GQA):
        t_ref[0, 0, :, hq * QB:(hq + 1) * QB] = jnp.where(keep, lookup(bt_ref[0], GQA * g + hq), NEG)

    @pl.when(d == 0)
    def _():
        for hq in range(GQA):
            w_ref[0, :, hq * QB:(hq + 1) * QB] = lookup(bw_ref[...], GQA * g + hq)


def _bias_tables(rel_table):
    tbl = rel_table.astype(F32) * LOG2E
    far_hi = tbl[REL_BUCKETS - 1].astype(BF16)
    far_lo = (tbl[REL_BUCKETS - 1] - far_hi.astype(F32)).astype(BF16)
    far = far_hi.astype(F32) + far_lo.astype(F32)
    q_tail = jnp.zeros((N_HEADS, HEAD_DIM), F32).at[:, 0].set(far_hi.astype(F32)).at[:, 1].set(far_lo.astype(F32))
    i = jnp.arange(QB)
    tile_dist = np.array(list(range(N_BIAS_TILES)) + [0, WINDOW // QB])
    d_t = (QB * jnp.asarray(tile_dist)[:, None, None] + i[None, None, :] - i[None, :, None])
    d_w = i[None, :] - CMP_STRIDE * (jnp.arange(CMP_WIN)[:, None] - CMP_PAD) - (CMP_LEN - 1)
    n_tiles = len(tile_dist)
    t_tab, w_tab = pl.pallas_call(
        _bias_kernel,
        grid=(N_KV, n_tiles),
        in_specs=[pl.BlockSpec(memory_space=pltpu.SMEM),
                  pl.BlockSpec(memory_space=pltpu.SMEM),
                  pl.BlockSpec((1, QB, QB), lambda g, d: (d, 0, 0)),
                  pl.BlockSpec((CMP_WIN, QB), lambda g, d: (0, 0))],
        out_specs=[pl.BlockSpec((1, 1, QB, GQA * QB), lambda g, d: (g, d, 0, 0)),
                   pl.BlockSpec((1, CMP_WIN, GQA * QB), lambda g, d: (g, 0, 0))],
        out_shape=[jax.ShapeDtypeStruct((N_KV, n_tiles, QB, GQA * QB), F32),
                   jax.ShapeDtypeStruct((N_KV, CMP_WIN, GQA * QB), F32)],
        compiler_params=_params("parallel", "arbitrary"),
        name="bias",
    )(tbl, far, _t5_bucket(d_t), _t5_bucket(d_w))
    far_min = min(FAR_TILE_DIST * QB - QB + 1, CMP_STRIDE * (CMP_PAD + 1) - (CMP_LEN - 1))
    max_exact = REL_BUCKETS // 2
    assert math.log(far_min / max_exact) / math.log(REL_MAX_DIST / max_exact) * max_exact > max_exact - 0.75
    return t_tab, w_tab, q_tail


def _overlap_tables(nc, nb):
    c_start = CMP_STRIDE * np.arange(nc)[:, None]
    s_start = SLC_LEN * np.arange(nb)[None, :]
    ov = ((c_start < s_start + SLC_LEN) & (c_start + CMP_LEN > s_start)).astype(np.float32)
    ov[nc - 1:] = 0.0
    out = np.zeros((nc + CMP_WIN, nb), np.float32)
    out[CMP_PAD:CMP_PAD + nc] = ov
    return jnp.asarray(ov.T, dtype=BF16), jnp.asarray(out)


def _block_diag(w):
    nblk, bs, _ = w.shape
    eye = jnp.eye(nblk, dtype=w.dtype)
    return (eye[:, None, :, None] * w[:, :, None, :]).reshape(nblk * bs, nblk * bs)


def _layer(x, c8, w_ada, b_ada, norm1_w, w_in, cmp_pos, cmp_k_w1, cmp_k_w2, cmp_v_w1, cmp_v_w2, rel_table,
           conv_w, conv_b, lru_wa, lru_ba, lru_wx, lru_bx, lru_lambda, gnorm_attn_w, gnorm_rnn_w, w_out,
           norm2_w, w_router, b_router, w1, w3, w2, ws1, ws3, ws2, final_norm_w):
    B, S, D = x.shape
    N = B * S
    E = w_router.shape[1]
    d_attn = N_HEADS * HEAD_DIM
    d_rnn = D - d_attn
    assert S % 512 == 0 and S // SLC_LEN <= 128

    mod = _ada(c8, w_ada, b_ada[None, :])[:B].reshape(B, 6, D)

    n_kv_cols = 6 * N_KV * HEAD_DIM
    n_gate = N_BRANCH * N_HEADS
    wq, wkv, wg, wxr, wgr = jnp.split(w_in, np.cumsum([d_attn, n_kv_cols, n_gate, d_rnn]).tolist(), axis=1)
    wg = jnp.pad(wg.reshape(D, N_KV, GQA * N_BRANCH), ((0, 0), (0, 0), (0, 128 - GQA * N_BRANCH))).reshape(D, 256)
    w_pad = jnp.concatenate([wq, wkv, wg, wxr, wgr], axis=1).astype(BF16)
    t_tab, w_tab, q_tail = _bias_tables(rel_table)
    q_t, kx, vx_t, kvc, gates_t, xr, gr = _inproj(x, mod, norm1_w[None, :], w_pad, q_tail)

    nc = S // CMP_STRIDE
    kvc16 = kvc.reshape(B, 4, nc, CMP_STRIDE * HEAD_DIM)
    pos2 = cmp_pos.reshape(2, CMP_STRIDE * HEAD_DIM)
    kvc_pad, kvc_t = _compress(kvc16, pos2, jnp.stack([cmp_k_w1, cmp_v_w1]), jnp.stack([cmp_k_w2, cmp_v_w2]))
    o_attn = _attn(q_t, kx, vx_t, kvc_pad, kvc_t, gates_t, t_tab, w_tab, *_overlap_tables(nc, 128))

    y_rnn = _rglru(xr, gr, conv_w.reshape(CONV_WIDTH, d_rnn), conv_b[None, :], _block_diag(lru_wa).astype(BF16),
                   lru_ba[None, :], _block_diag(lru_wx).astype(BF16), lru_bx[None, :], lru_lambda[None, :],
                   gnorm_rnn_w[None, :])

    x1, h2, s_t = _outproj(o_attn, y_rnn, x, mod, gnorm_attn_w[None, :], w_out.astype(BF16), norm2_w[None, :],
                           w_router.T)

    tile = min(MOE_TILE, S)
    eid_t, wgt_t, rank_t, counts, tile_counts = _route(s_t, b_router[:, None], tile)
    del counts
    tcnt = tile_counts[:, :, 0].astype(I32)
    n_tiles = tcnt.shape[0]
    tile_pairs = _tile_pairs(tile, E)
    rpairs = (tcnt + 1) // 2
    rpairs = jnp.concatenate([rpairs, tile_pairs - jnp.sum(rpairs, axis=1, keepdims=True)], axis=1)
    per_expert = jnp.sum(rpairs, axis=0)
    off = jnp.cumsum(per_expert) - per_expert
    run_src = jnp.cumsum(rpairs, axis=1) - rpairs
    run_dst = off[None, :] + jnp.cumsum(rpairs, axis=0) - rpairs
    runs = tuple(r.astype(I32)[:, None, :] for r in (run_src, rpairs, run_dst))
    earlier = jnp.cumsum(tcnt, axis=0) - tcnt
    pos_t = _tilepos(eid_t, rank_t, (2 * run_src[:, :E] - earlier).astype(F32)[:, :, None], tile)

    h2f = h2.reshape(N, D)
    xs = _dispatch(runs, pos_t, h2f, tile)
    ys = _experts(_expert_chunks(per_expert[:E], n_tiles * tile_pairs), xs, w1, w3, w2)
    out = _combine(runs, ys, pos_t, wgt_t, h2f, x1.reshape(N, D), mod, ws1.astype(BF16), ws3.astype(BF16),
                   ws2.astype(BF16), final_norm_w[None, :], S, tile)
    return out.reshape(B, S, D)


def kernel(x, c, w_ada, b_ada, norm1_w, w_in, cmp_pos, cmp_k_w1, cmp_k_w2, cmp_v_w1, cmp_v_w2, rel_table, conv_w, conv_b, lru_wa, lru_ba, lru_wx, lru_bx, lru_lambda, gnorm_attn_w, gnorm_rnn_w, w_out, norm2_w, w_router, b_router, w1, w3, w2, ws1, ws3, ws2, final_norm_w):
    assert w_ada.shape[0] == 1
    c8 = jnp.pad(c, ((0, 8 - c.shape[0]), (0, 0)))
    return _layer(x, c8, w_ada[0], b_ada[0], norm1_w[0], w_in[0], cmp_pos[0], cmp_k_w1[0], cmp_k_w2[0],
                  cmp_v_w1[0], cmp_v_w2[0], rel_table, conv_w[0], conv_b[0], lru_wa[0], lru_ba[0], lru_wx[0],
                  lru_bx[0], lru_lambda[0], gnorm_attn_w[0], gnorm_rnn_w[0], w_out[0], norm2_w[0], w_router[0],
                  b_router[0], w1[0], w3[0], w2[0], ws1[0], ws3[0], ws2[0], final_norm_w)
```

```python
import math

import jax
import jax.numpy as jnp
import numpy as np
from jax import lax
from jax.experimental import pallas as pl
from jax.experimental.pallas import tpu as pltpu

F32 = jnp.float32
BF16 = jnp.bfloat16
I32 = jnp.int32

HEAD_DIM = 64
N_HEADS = 8
N_KV = 2
GQA = N_HEADS // N_KV
N_BRANCH = 3
CONV_WIDTH = 4
LRU_C = 8.0
CMP_LEN = 32
CMP_STRIDE = 16
SLC_LEN = 64
SLC_TOP = 16
WINDOW = 512
LANES = 128
QB = 128
MAX_SLC_BLOCKS = 128
GATE_ROWS = 16
REL_BUCKETS = 32
REL_MAX_DIST = 1024
N_EXPERT_GROUPS = 8
TOP_GROUPS = 4
TOP_K = 8
ROUTED_SCALE = 2.5
MOE_BLOCK = 576
MOE_TILE = 256
ROW_TILE = 8
EPS = 1e-6
NEG = -1e30
LOG2E = math.log2(math.e)
CMP_PAD = 120
CMP_WIN = 128
N_BIAS_TILES = 11
TILE_MASKED = N_BIAS_TILES
TILE_WINDOW_EDGE = N_BIAS_TILES + 1
FAR_TILE_DIST = 8
FAR_CHUNK = 4
VMEM_LIMIT = 52 * 1024 * 1024


def _dot(a, b, **kw):
    return jnp.dot(a, b, preferred_element_type=F32, **kw)


def _dot_nt(a, b, **kw):
    return lax.dot_general(a, b, (((1,), (1,)), ((), ())), preferred_element_type=F32, **kw)


def _gelu(x):
    return 0.5 * x * (1.0 + jnp.tanh(math.sqrt(2.0 / math.pi) * (x + 0.044715 * (x * x * x))))


def _sigmoid(x):
    return 1.0 / (1.0 + jnp.exp(-x))


def _rms(x):
    return x * lax.rsqrt(jnp.mean(x * x, axis=-1, keepdims=True) + EPS)


def _params(*sem):
    return pltpu.CompilerParams(dimension_semantics=sem, vmem_limit_bytes=VMEM_LIMIT)


def _ada_kernel(c_ref, w_ref, b_ref, o_ref):
    c = c_ref[...]
    a = c * _sigmoid(c)
    o_ref[...] = _dot(a, w_ref[...], precision=lax.Precision.HIGHEST) + b_ref[...]


def _ada(c8, w, b):
    d, n = w.shape
    tn = 1536
    return pl.pallas_call(
        _ada_kernel,
        grid=(n // tn,),
        in_specs=[pl.BlockSpec((8, d), lambda j: (0, 0)),
                  pl.BlockSpec((d, tn), lambda j: (0, j)),
                  pl.BlockSpec((1, tn), lambda j: (0, j))],
        out_specs=pl.BlockSpec((8, tn), lambda j: (0, j)),
        out_shape=jax.ShapeDtypeStruct((8, n), F32),
        compiler_params=_params("parallel"),
        name="ada",
    )(c8, w, b)


def _inproj_kernel(x_ref, mod_ref, nw_ref, w_ref, qt_ref, q_ref, kx_ref, vx_ref, kvc_ref, g_ref, xr_ref, gr_ref):
    h = _rms(x_ref[0]) * nw_ref[...]
    h = h * (1.0 + mod_ref[0, 1:2, :]) + mod_ref[0, 0:1, :]
    p = _dot(h.astype(BF16), w_ref[...])
    tm = p.shape[0]
    dq = N_HEADS * HEAD_DIM
    lane = lax.broadcasted_iota(I32, (tm, HEAD_DIM), 1)
    k_tail = jnp.where(lane < 2, 1.0, 0.0)
    v_tail = jnp.where(lane < 1, 1.0, 0.0)
    for hh in range(N_HEADS):
        qh = p[:, hh * HEAD_DIM:(hh + 1) * HEAD_DIM] * (HEAD_DIM ** -0.5 * LOG2E)
        q_tail = jnp.broadcast_to(qt_ref[hh:hh + 1, :], (tm, HEAD_DIM))
        q_t = jnp.concatenate([qh, q_tail], axis=1).T.astype(BF16)
        for blk in range(tm // QB):
            col = (blk * GQA + hh % GQA) * QB
            q_ref[0, hh // GQA, :, col:col + QB] = q_t[:, blk * QB:(blk + 1) * QB]
    for j in range(6):
        for g in range(N_KV):
            col = dq + (j * N_KV + g) * HEAD_DIM
            piece = p[:, col:col + HEAD_DIM]
            if j < 2:
                kvc_ref[0, j * N_KV + g] = piece
            elif j % 2 == 0:
                kx_ref[0, (j // 2 - 1) * N_KV + g] = jnp.concatenate([piece, k_tail], axis=1).astype(BF16)
            else:
                v_t = jnp.concatenate([piece, v_tail], axis=1).T.astype(BF16)
                for blk in range(tm // QB):
                    vx_ref[0, (j // 2 - 1) * N_KV + g, blk] = v_t[:, blk * QB:(blk + 1) * QB]
    c0 = dq + 6 * N_KV * HEAD_DIM
    for g in range(N_KV):
        g_ref[0, g] = _sigmoid(p[:, c0 + g * LANES:c0 + (g + 1) * LANES]).T[0:GATE_ROWS, :]
    c1 = c0 + N_KV * LANES
    d_rnn = xr_ref.shape[2]
    xr_ref[0] = p[:, c1:c1 + d_rnn]
    gr_ref[0] = p[:, c1 + d_rnn:c1 + 2 * d_rnn]


def _inproj(x, mod, nw, w_pad, q_tail, tm=256):
    B, S, D = x.shape
    ncol = w_pad.shape[1]
    heads = lambda n, w: pl.BlockSpec((1, n, tm, w), lambda b, t: (b, 0, t, 0))
    return pl.pallas_call(
        _inproj_kernel,
        grid=(B, S // tm),
        in_specs=[pl.BlockSpec((1, tm, D), lambda b, t: (b, t, 0)),
                  pl.BlockSpec((1, 6, D), lambda b, t: (b, 0, 0)),
                  pl.BlockSpec((1, D), lambda b, t: (0, 0)),
                  pl.BlockSpec((D, ncol), lambda b, t: (0, 0)),
                  pl.BlockSpec(q_tail.shape, lambda b, t: (0, 0))],
        out_specs=[pl.BlockSpec((1, N_KV, 2 * HEAD_DIM, tm * GQA), lambda b, t: (b, 0, 0, t)),
                   heads(4, 2 * HEAD_DIM),
                   pl.BlockSpec((1, 4, tm // QB, 2 * HEAD_DIM, QB), lambda b, t: (b, 0, t, 0, 0)),
                   heads(4, HEAD_DIM),
                   pl.BlockSpec((1, N_KV, GATE_ROWS, tm), lambda b, t: (b, 0, 0, t)),
                   pl.BlockSpec((1, tm, 512), lambda b, t: (b, t, 0)),
                   pl.BlockSpec((1, tm, 512), lambda b, t: (b, t, 0))],
        out_shape=[jax.ShapeDtypeStruct((B, N_KV, 2 * HEAD_DIM, S * GQA), BF16),
                   jax.ShapeDtypeStruct((B, 4, S, 2 * HEAD_DIM), BF16),
                   jax.ShapeDtypeStruct((B, 4, S // QB, 2 * HEAD_DIM, QB), BF16),
                   jax.ShapeDtypeStruct((B, 4, S, HEAD_DIM), F32),
                   jax.ShapeDtypeStruct((B, N_KV, GATE_ROWS, S), F32),
                   jax.ShapeDtypeStruct((B, S, 512), F32),
                   jax.ShapeDtypeStruct((B, S, 512), F32)],
        compiler_params=_params("parallel", "parallel"),
        name="inproj",
    )(x, mod, nw, w_pad, q_tail)


def _compress_kernel(x_ref, pos_ref, w1_ref, w2_ref, o_ref, ot_ref):
    x = x_ref[0, 0]
    nc = x.shape[0]
    half = CMP_STRIDE * HEAD_DIM
    kv = pl.program_id(1) // N_KV
    a = _dot((x + pos_ref[0:1, :]).astype(BF16), w1_ref[kv, 0:half, :].astype(BF16))
    b = _dot((x + pos_ref[1:2, :]).astype(BF16), w1_ref[kv, half:2 * half, :].astype(BF16))
    hid = _gelu(a + pltpu.roll(b, nc - 1, 0))
    out = _dot(hid.astype(BF16), w2_ref[kv].astype(BF16))
    row = lax.broadcasted_iota(I32, out.shape, 0)
    out = jnp.where(row < nc - 1, out, 0.0)
    lane = lax.broadcasted_iota(I32, (nc, HEAD_DIM), 1)
    out = jnp.concatenate([out, jnp.where(lane < 2, 1.0, 0.0)], axis=1)
    o_ref[0, 0, 0:CMP_PAD, :] = jnp.zeros((CMP_PAD, 2 * HEAD_DIM), F32)
    o_ref[0, 0, CMP_PAD:CMP_PAD + nc, :] = out
    o_ref[0, 0, CMP_PAD + nc:, :] = jnp.zeros((CMP_WIN - CMP_PAD, 2 * HEAD_DIM), F32)
    ot_ref[0, 0] = out.T


def _compress(kvc, pos2, w1, w2):
    B, _, NC, W = kvc.shape
    return pl.pallas_call(
        _compress_kernel,
        grid=(B, 4),
        in_specs=[pl.BlockSpec((1, 1, NC, W), lambda b, i: (b, i, 0, 0)),
                  pl.BlockSpec((2, W), lambda b, i: (0, 0)),
                  pl.BlockSpec((2, 2 * W, HEAD_DIM), lambda b, i: (0, 0, 0)),
                  pl.BlockSpec((2, HEAD_DIM, HEAD_DIM), lambda b, i: (0, 0, 0))],
        out_specs=[pl.BlockSpec((1, 1, NC + CMP_WIN, 2 * HEAD_DIM), lambda b, i: (b, i, 0, 0)),
                   pl.BlockSpec((1, 1, 2 * HEAD_DIM, NC), lambda b, i: (b, i, 0, 0))],
        out_shape=[jax.ShapeDtypeStruct((B, 4, NC + CMP_WIN, 2 * HEAD_DIM), F32),
                   jax.ShapeDtypeStruct((B, 4, 2 * HEAD_DIM, NC), F32)],
        compiler_params=_params("parallel", "parallel"),
        name="compress",
    )(kvc, pos2, w1, w2)


def _attn_kernel(q_ref, ks_ref, vs_ref, kw_ref, vw_ref, kc_ref, vc_ref, vct_ref, g_ref, t_ref, w_ref, ovt_ref, ov_ref,
                 o_ref, m_scr, acc_scr, sel_scr, sa_scr, sb_scr, sc_scr):
    qb = pl.program_id(2)
    R = GQA * QB
    q = q_ref[0, 0]
    nc = vct_ref.shape[3]
    nb = ov_ref.shape[1]
    n_top = min(SLC_TOP, ks_ref.shape[2] // SLC_LEN)

    def heads_sum(x):
        out = x[:, 0:QB]
        for hq in range(1, GQA):
            out = out + x[:, hq * QB:(hq + 1) * QB]
        return out

    w0 = pl.multiple_of(qb * 8, 8)
    s_far = _dot(kc_ref[0, 0, CMP_PAD:CMP_PAD + nc, :].astype(BF16), q)
    n_io = lax.broadcasted_iota(I32, (nc, R), 0)
    s_far = jnp.where(n_io < qb * 8 - CMP_PAD, s_far, -jnp.inf)
    s_win = _dot(kc_ref[0, 0, pl.ds(w0, CMP_WIN), :].astype(BF16), q) + w_ref[0]
    j_w = lax.broadcasted_iota(I32, (CMP_WIN, R), 0)
    i_w = lax.broadcasted_iota(I32, (CMP_WIN, R), 1) & (QB - 1)
    dist_w = i_w - CMP_STRIDE * (j_w - CMP_PAD) - (CMP_LEN - 1)
    s_win = jnp.where(dist_w >= 0, jnp.where(j_w >= CMP_PAD - qb * 8, s_win, -jnp.inf), -jnp.inf)
    m = jnp.maximum(jnp.max(s_far, axis=0, keepdims=True), jnp.max(s_win, axis=0, keepdims=True))
    m = jnp.where(m == -jnp.inf, 0.0, m)
    e_far = jnp.exp2(s_far - m)
    e_win = jnp.exp2(s_win - m)
    l = jnp.sum(e_far, axis=0, keepdims=True) + jnp.sum(e_win, axis=0, keepdims=True)
    inv = 1.0 / jnp.maximum(l, 1e-30)
    p_far = e_far * inv
    p_win = e_win * inv
    vcw_t = vc_ref[0, 0, pl.ds(w0, CMP_WIN), :].T
    o_c = _dot(vct_ref[0, 0].astype(BF16), p_far.astype(BF16)) + _dot(vcw_t.astype(BF16), p_win.astype(BF16))
    ov_win_t = ov_ref[pl.ds(w0, CMP_WIN), :].T
    sc_t = (_dot(ovt_ref[...], heads_sum(p_far).astype(BF16))
            + _dot(ov_win_t.astype(BF16), heads_sum(p_win).astype(BF16)))

    n_wt = WINDOW // QB + 1
    k_t, v_t, b_t = [], [], []
    for j in range(n_wt):
        kb = qb - (n_wt - 1) + j
        kb0 = jnp.maximum(kb, 0)
        k_t.append(kw_ref[0, 0, pl.ds(pl.multiple_of(kb0 * QB, QB), QB), :])
        v_t.append(vw_ref[0, 0, kb0])
        b_t.append(t_ref[0, jnp.where(kb < 0, TILE_MASKED, TILE_WINDOW_EDGE if j == 0 else n_wt - 1 - j)])
    s = _dot(jnp.concatenate(k_t, axis=0), q) + jnp.concatenate(b_t, axis=0)
    p = jnp.exp2(s - jnp.max(s, axis=0, keepdims=True))
    acc = _dot(jnp.concatenate(v_t, axis=1), p.astype(BF16))
    o_w = acc[0:HEAD_DIM, :] / acc[HEAD_DIM:HEAD_DIM + 1, :]

    chunk = FAR_CHUNK * QB

    def qk(c):
        return _dot(ks_ref[0, 0, pl.ds(pl.multiple_of(c * chunk, chunk), chunk), :], q)

    n_far = jnp.maximum(qb - (FAR_TILE_DIST - 1), 0) // FAR_CHUNK
    sa_scr[...] = qk(0)
    sc_scr[...] = qk(n_far)

    blk = lax.broadcasted_iota(I32, (nb, QB), 0)
    t_io = lax.broadcasted_iota(I32, (nb, QB), 1)
    cur = qb * (QB // SLC_LEN) + t_io // SLC_LEN
    val = jnp.where(blk <= cur, sc_t, -jnp.inf)
    for forced_blk in (0, cur, cur - 1):
        val = jnp.where(blk == forced_blk, jnp.inf, val)
    drop = jnp.full((nb, QB), NEG, F32)
    for _ in range(n_top):
        mx = jnp.max(val, axis=0, keepdims=True)
        first = jnp.min(jnp.where(val == mx, blk, nb), axis=0, keepdims=True)
        first = jnp.where(mx > -jnp.inf, first, nb)
        pick = blk == first
        drop = jnp.where(pick, 0.0, drop)
        val = jnp.where(pick, -jnp.inf, val)
    sel_scr[...] = drop

    m_scr[...] = jnp.full(m_scr.shape, NEG, F32)
    acc_scr[...] = jnp.zeros(acc_scr.shape, F32)
    blocks_per_chunk = FAR_CHUNK * QB // SLC_LEN

    def soft(c, s):
        rows = sel_scr[pl.ds(pl.multiple_of(c * blocks_per_chunk, blocks_per_chunk), blocks_per_chunk), :]
        drop = jnp.concatenate([jnp.broadcast_to(rows[r:r + 1, :], (SLC_LEN, QB)) for r in range(blocks_per_chunk)],
                               axis=0)
        s = s + jnp.concatenate([drop] * GQA, axis=1)
        m_old = m_scr[...]
        m_new = jnp.maximum(m_old, jnp.max(s, axis=0, keepdims=True))
        p = jnp.exp2(s - m_new)
        v_t = jnp.concatenate([vs_ref[0, 0, c * FAR_CHUNK + j] for j in range(FAR_CHUNK)], axis=1)
        acc_scr[...] = jnp.exp2(m_old - m_new) * acc_scr[...] + _dot(v_t, p.astype(BF16))
        m_scr[...] = m_new

    def near_bias(c):
        tiles = []
        for j in range(FAR_CHUNK):
            delta = qb - (c * FAR_CHUNK + j)
            tiles.append(t_ref[0, jnp.where(delta < 0, TILE_MASKED, delta)])
        return jnp.concatenate(tiles, axis=0)

    def chunk_pairs(first, count, bias, sa):
        last = first + jnp.maximum(count - 1, 0)

        def pair_body(j, carry):
            c = first + 2 * j
            sb_scr[...] = qk(c + 1)
            soft(c, sa[...] + bias(c) if bias else sa[...])
            sa[...] = qk(jnp.minimum(c + 2, last))
            soft(c + 1, sb_scr[...] + bias(c + 1) if bias else sb_scr[...])
            return carry

        lax.fori_loop(0, count // 2, pair_body, 0)

        @pl.when(count % 2 == 1)
        def _():
            soft(last, sa[...] + bias(last) if bias else sa[...])

    chunk_pairs(0, n_far, None, sa_scr)
    chunk_pairs(n_far, qb // FAR_CHUNK + 1 - n_far, near_bias, sc_scr)
    acc = acc_scr[...]
    o_s = acc[0:HEAD_DIM, :] / acc[HEAD_DIM:HEAD_DIM + 1, :]

    gates = g_ref[0, 0]
    outs = []
    for hq in range(GQA):
        cols = slice(hq * QB, (hq + 1) * QB)
        c = hq * N_BRANCH
        outs.append(gates[c:c + 1, :] * o_c[0:HEAD_DIM, cols] + gates[c + 1:c + 2, :] * o_s[:, cols]
                    + gates[c + 2:c + 3, :] * o_w[:, cols])
    o_ref[0] = jnp.concatenate(outs, axis=0).T


def _attn(q_t, kx, vx_t, kvc, kvc_t, gates_t, t_tab, w_tab, ov_t, ov):
    B, _, W, _ = q_t.shape
    S = kx.shape[2]
    ncp = kvc.shape[2]
    nb = ov.shape[1]
    keys = lambda j: pl.BlockSpec((1, 1, S, W), lambda b, g, t: (b, j * N_KV + g, 0, 0))
    vals = lambda j: pl.BlockSpec((1, 1, S // QB, W, QB), lambda b, g, t: (b, j * N_KV + g, 0, 0, 0))
    cmp = lambda j: pl.BlockSpec((1, 1, ncp, W), lambda b, g, t: (b, j * N_KV + g, 0, 0))
    return pl.pallas_call(
        _attn_kernel,
        grid=(B, N_KV, S // QB),
        in_specs=[pl.BlockSpec((1, 1, W, GQA * QB), lambda b, g, t: (b, g, 0, t)),
                  keys(0), vals(0), keys(1), vals(1), cmp(0), cmp(1),
                  pl.BlockSpec((1, 1, W, ncp - CMP_WIN), lambda b, g, t: (b, N_KV + g, 0, 0)),
                  pl.BlockSpec((1, 1, GATE_ROWS, QB), lambda b, g, t: (b, g, 0, t)),
                  pl.BlockSpec((1,) + t_tab.shape[1:], lambda b, g, t: (g, 0, 0, 0)),
                  pl.BlockSpec((1, CMP_WIN, GQA * QB), lambda b, g, t: (g, 0, 0)),
                  pl.BlockSpec(ov_t.shape, lambda b, g, t: (0, 0)),
                  pl.BlockSpec(ov.shape, lambda b, g, t: (0, 0))],
        out_specs=pl.BlockSpec((1, QB, GQA * HEAD_DIM), lambda b, g, t: (b, t, g)),
        out_shape=jax.ShapeDtypeStruct((B, S, N_HEADS * HEAD_DIM), F32),
        scratch_shapes=[pltpu.VMEM((1, GQA * QB), F32), pltpu.VMEM((W, GQA * QB), F32), pltpu.VMEM((nb, QB), F32),
                        pltpu.VMEM((FAR_CHUNK * QB, GQA * QB), F32), pltpu.VMEM((FAR_CHUNK * QB, GQA * QB), F32),
                        pltpu.VMEM((FAR_CHUNK * QB, GQA * QB), F32)],
        compiler_params=_params("parallel", "parallel", "arbitrary"),
        name="attn",
    )(q_t, kx, vx_t, kx, vx_t, kvc, kvc, kvc_t, gates_t, t_tab, w_tab, ov_t, ov)


def _rglru_kernel(xr_ref, gr_ref, cw_ref, cb_ref, wa_ref, ba_ref, wx_ref, bx_ref, lam_ref, gw_ref, o_ref,
                  xbuf, hprev, a_scr, u_scr, h_scr):
    ts = xr_ref.shape[1]
    C = xr_ref.shape[2]

    @pl.when(pl.program_id(1) == 0)
    def _():
        xbuf[0:8, :] = jnp.zeros((8, C), F32)
        hprev[...] = jnp.zeros(hprev.shape, F32)

    xbuf[8:8 + ts, :] = xr_ref[0]
    xc = cb_ref[...] + jnp.zeros((ts, C), F32)
    for j in range(CONV_WIDTH):
        xc = xc + cw_ref[j:j + 1, :] * xbuf[pl.ds(8 - (CONV_WIDTH - 1) + j, ts), :]
    xbuf[0:8, :] = xbuf[ts:ts + 8, :]

    xcb = xc.astype(BF16)
    r = _sigmoid(_dot(xcb, wa_ref[...]) + ba_ref[...])
    i = _sigmoid(_dot(xcb, wx_ref[...]) + bx_ref[...])
    z = -lam_ref[...]
    softplus = jnp.maximum(z, 0.0) + jnp.log(1.0 + jnp.exp(-jnp.abs(z)))
    log_a = -LRU_C * r * softplus
    a_scr[...] = jnp.exp(log_a)
    u_scr[...] = jnp.sqrt(1.0 - jnp.exp(2.0 * log_a)) * (i * xc)

    row = lax.broadcasted_iota(I32, (8, C), 0)

    def body(k, h):
        r0 = pl.multiple_of(k * 8, 8)
        a = a_scr[pl.ds(r0, 8), :]
        b = u_scr[pl.ds(r0, 8), :]
        for s in (1, 2, 4):
            keep = row >= s
            b = jnp.where(keep, a * pltpu.roll(b, s, 0) + b, b)
            a = jnp.where(keep, a * pltpu.roll(a, s, 0), a)
        hh = a * h + b
        h_scr[pl.ds(r0, 8), :] = hh
        return jnp.broadcast_to(hh[7:8, :], (8, C))

    hprev[...] = lax.fori_loop(0, ts // 8, body, hprev[...])
    out = h_scr[...] * _gelu(gr_ref[0])
    o_ref[0] = (_rms(out) * gw_ref[...]).astype(BF16)


def _rglru(xr, gr, cw, cb, wa_bd, ba, wx_bd, bx, lam, gw, ts=512):
    B, S, C = xr.shape
    ts = min(ts, S)
    vec = pl.BlockSpec((1, C), lambda b, t: (0, 0))
    mat = pl.BlockSpec((C, C), lambda b, t: (0, 0))
    seq = pl.BlockSpec((1, ts, C), lambda b, t: (b, t, 0))
    return pl.pallas_call(
        _rglru_kernel,
        grid=(B, S // ts),
        in_specs=[seq, seq, pl.BlockSpec((CONV_WIDTH, C), lambda b, t: (0, 0)), vec, mat, vec, mat, vec, vec, vec],
        out_specs=seq,
        out_shape=jax.ShapeDtypeStruct((B, S, C), BF16),
        scratch_shapes=[pltpu.VMEM((ts + 8, C), F32), pltpu.VMEM((8, C), F32), pltpu.VMEM((ts, C), F32),
                        pltpu.VMEM((ts, C), F32), pltpu.VMEM((ts, C), F32)],
        compiler_params=_params("parallel", "arbitrary"),
        name="rglru",
    )(xr, gr, cw, cb, wa_bd, ba, wx_bd, bx, lam, gw)


def _to_row_tiles(ref, x):
    rows = x.shape[0]
    for s in range(ROW_TILE):
        ref[pl.ds(s, rows, stride=ROW_TILE), :] = x[:, s * LANES:(s + 1) * LANES]


def _from_row_tiles(ref, rows, start=0):
    return jnp.concatenate([ref[pl.ds(start + s, rows, stride=ROW_TILE), :] for s in range(ROW_TILE)], axis=1)


def _outproj_kernel(oa_ref, yr_ref, x_ref, mod_ref, gaw_ref, wo_ref, n2_ref, wr_ref, x1_ref, h2_ref, st_ref):
    da = oa_ref.shape[2]
    ya = (_rms(oa_ref[0]) * gaw_ref[...]).astype(BF16)
    mix = _dot(ya, wo_ref[0:da, :]) + _dot(yr_ref[0], wo_ref[da:, :])
    x1 = x_ref[0] + mod_ref[0, 2:3, :] * mix
    x1_ref[0] = x1
    h2 = (_rms(x1) * n2_ref[...]) * (1.0 + mod_ref[0, 4:5, :]) + mod_ref[0, 3:4, :]
    h2_ref[0] = h2.astype(BF16)
    st_ref[...] = _sigmoid(_dot_nt(wr_ref[...], h2, precision=lax.Precision.HIGHEST))


def _outproj(oa, yr, x, mod, gaw, wo, n2, wr_t, tm=256):
    B, S, D = x.shape
    E = wr_t.shape[0]
    nt = S // tm
    row = lambda w: pl.BlockSpec((1, tm, w), lambda b, t: (b, t, 0))
    return pl.pallas_call(
        _outproj_kernel,
        grid=(B, nt),
        in_specs=[row(oa.shape[2]), row(yr.shape[2]), row(D),
                  pl.BlockSpec((1, 6, D), lambda b, t: (b, 0, 0)),
                  pl.BlockSpec((1, oa.shape[2]), lambda b, t: (0, 0)),
                  pl.BlockSpec(wo.shape, lambda b, t: (0, 0)),
                  pl.BlockSpec((1, D), lambda b, t: (0, 0)),
                  pl.BlockSpec((E, D), lambda b, t: (0, 0))],
        out_specs=[row(D), row(D), pl.BlockSpec((E, tm), lambda b, t: (0, b * nt + t))],
        out_shape=[jax.ShapeDtypeStruct((B, S, D), F32), jax.ShapeDtypeStruct((B, S, D), BF16),
                   jax.ShapeDtypeStruct((E, B * S), F32)],
        compiler_params=_params("parallel", "parallel"),
        name="outproj",
    )(oa, yr, x, mod, gaw, wo, n2, wr_t)


def _route_kernel(s_ref, b_ref, wgt_ref, pos_ref, tcnt_ref):
    E, tn = s_ref.shape
    per = E // N_EXPERT_GROUPS
    s = s_ref[...]
    s_sel = s + b_ref[...]
    eidx = lax.broadcasted_iota(I32, (E, tn), 0)

    grp = []
    for gi in range(N_EXPERT_GROUPS):
        xg = s_sel[gi * per:(gi + 1) * per, :]
        ig = lax.broadcasted_iota(I32, (per, tn), 0)
        m1 = jnp.max(xg, axis=0, keepdims=True)
        f1 = jnp.min(jnp.where(xg == m1, ig, per), axis=0, keepdims=True)
        m2 = jnp.max(jnp.where(ig == f1, -jnp.inf, xg), axis=0, keepdims=True)
        grp.append(m1 + m2)
    val = []
    for gi in range(N_EXPERT_GROUPS):
        rank = jnp.zeros((1, tn), I32)
        for gj in range(N_EXPERT_GROUPS):
            if gj == gi:
                continue
            ahead = (grp[gj] > grp[gi]) | ((grp[gj] == grp[gi]) & (gj < gi))
            rank = rank + ahead.astype(I32)
        val.append(jnp.where(rank < TOP_GROUPS, s_sel[gi * per:(gi + 1) * per, :], -jnp.inf))
    val = jnp.concatenate(val, axis=0)

    eids, tops = [], []
    hot = jnp.zeros((E, tn), F32)
    for _ in range(TOP_K):
        mx = jnp.max(val, axis=0, keepdims=True)
        first = jnp.min(jnp.where(val == mx, eidx, E), axis=0, keepdims=True)
        pick = eidx == first
        eids.append(first)
        tops.append(jnp.sum(jnp.where(pick, s, 0.0), axis=0, keepdims=True))
        hot = jnp.where(pick, 1.0, hot)
        val = jnp.where(pick, -jnp.inf, val)
    denom = tops[0]
    for t in tops[1:]:
        denom = denom + t

    ti = lax.broadcasted_iota(I32, (tn, tn), 0)
    tj = lax.broadcasted_iota(I32, (tn, tn), 1)
    upper = jnp.where(ti < tj, 1.0, 0.0).astype(BF16)
    before = _dot(hot.astype(BF16), upper)
    tile_count = jnp.sum(hot, axis=1, keepdims=True)
    padded = tile_count + (tile_count - 2.0 * jnp.floor(tile_count * 0.5))
    ei = lax.broadcasted_iota(I32, (E, E), 0)
    ej = lax.broadcasted_iota(I32, (E, E), 1)
    earlier_experts = jnp.where(ej < ei, 1.0, 0.0).astype(BF16)
    run_start = _dot(earlier_experts, jnp.broadcast_to(padded, (E, LANES)).astype(BF16))[:, 0:1]
    for k in range(TOP_K):
        wgt_ref[k:k + 1, :] = ROUTED_SCALE * tops[k] / denom
        pos_ref[k:k + 1, :] = jnp.sum(jnp.where(eidx == eids[k], before + run_start, 0.0), axis=0,
                                      keepdims=True).astype(I32)
    tcnt_ref[0] = tile_count


def _route(s_t, b_col, tn):
    E, N = s_t.shape
    out = pl.BlockSpec((TOP_K, tn), lambda t: (0, t))
    return pl.pallas_call(
        _route_kernel,
        grid=(N // tn,),
        in_specs=[pl.BlockSpec((E, tn), lambda t: (0, t)), pl.BlockSpec((E, 1), lambda t: (0, 0))],
        out_specs=[out, out, pl.BlockSpec((1, E, 1), lambda t: (t, 0, 0))],
        out_shape=[jax.ShapeDtypeStruct((TOP_K, N), F32), jax.ShapeDtypeStruct((TOP_K, N), I32),
                   jax.ShapeDtypeStruct((N // tn, E, 1), F32)],
        compiler_params=_params("parallel"),
        name="route",
    )(s_t, b_col)


_HI = -65536


def _pack_pairs(even_rows, odd_rows):
    hi = lax.bitcast_convert_type(even_rows, I32) & _HI
    lo = lax.shift_right_logical(lax.bitcast_convert_type(odd_rows, I32), 16)
    return hi | lo


def _unpack_pairs(words):
    even_rows = lax.bitcast_convert_type(words & _HI, F32)
    odd_rows = lax.bitcast_convert_type(lax.shift_left(words, 16), F32)
    return even_rows, odd_rows


def _run_copies(src_ref, cnt_ref, dst_ref, n_runs, copy):
    def go(e, carry):
        n = cnt_ref[0, 0, e]

        @pl.when(n > 0)
        def _():
            size = pl.multiple_of(n * ROW_TILE, ROW_TILE)
            copy(pl.ds(pl.multiple_of(src_ref[0, 0, e] * ROW_TILE, ROW_TILE), size),
                 pl.ds(pl.multiple_of(dst_ref[0, 0, e] * ROW_TILE, ROW_TILE), size)).start()
        return carry
    lax.fori_loop(0, n_runs, go, 0, unroll=4)


def _pair_onehots(pos_rows, n_pairs, first_pair, width):
    even = 2 * (lax.broadcasted_iota(I32, (n_pairs, width), 0) + first_pair)
    hot_e = jnp.zeros((n_pairs, width), F32)
    hot_o = jnp.zeros((n_pairs, width), F32)
    for k in range(TOP_K):
        hot_e = jnp.where(even == pos_rows[k:k + 1, :], 1.0, hot_e)
        hot_o = jnp.where(even + 1 == pos_rows[k:k + 1, :], 1.0, hot_o)
    return hot_e, hot_o


def _dispatch_kernel(src_ref, cnt_ref, dst_ref, pos_ref, h_ref, xs_hbm, xbuf, sem):
    n_pairs = xbuf.shape[0] // ROW_TILE
    tn = h_ref.shape[0]
    hb = h_ref[...]
    slab = n_pairs // 3
    for sl in range(3):
        hot_e, hot_o = _pair_onehots(pos_ref[...], slab, sl * slab, tn)
        words = _pack_pairs(_dot(hot_e.astype(BF16), hb), _dot(hot_o.astype(BF16), hb))
        for s in range(ROW_TILE):
            xbuf[pl.ds(sl * slab * ROW_TILE + s, slab, stride=ROW_TILE), :] = words[:, s * LANES:(s + 1) * LANES]
    _run_copies(src_ref, cnt_ref, dst_ref, cnt_ref.shape[2],
                lambda src, dst: pltpu.make_async_copy(xbuf.at[src], xs_hbm.at[dst], sem.at[0]))
    pltpu.make_async_copy(xbuf, xs_hbm.at[pl.ds(0, xbuf.shape[0])], sem.at[0]).wait()


def _run_specs(runs):
    return [pl.BlockSpec((1, 1, r.shape[2]), lambda t: (t, 0, 0), memory_space=pltpu.SMEM) for r in runs]


def _tile_pairs(tn, n_experts):
    return (TOP_K * tn + n_experts) // 2


def _dispatch(runs, pos_t, h2, tn):
    N, D = h2.shape
    W = D // ROW_TILE
    n_pairs = _tile_pairs(tn, runs[0].shape[2] - 1)
    assert n_pairs % 24 == 0
    return pl.pallas_call(
        _dispatch_kernel,
        grid=(N // tn,),
        in_specs=_run_specs(runs) + [pl.BlockSpec((TOP_K, tn), lambda t: (0, t)),
                                     pl.BlockSpec((tn, D), lambda t: (t, 0))],
        out_specs=pl.BlockSpec(memory_space=pl.ANY),
        out_shape=jax.ShapeDtypeStruct((N // tn * n_pairs * ROW_TILE, W), I32),
        scratch_shapes=[pltpu.VMEM((n_pairs * ROW_TILE, W), I32), pltpu.SemaphoreType.DMA((1,))],
        compiler_params=_params("arbitrary"),
        name="dispatch",
    )(*runs, pos_t, h2)


def _experts_kernel(first_ref, size_ref, e_ref, new_ref, next_ref, wslot_ref, total_ref,
                    w1_hbm, w3_hbm, w2_hbm, xs_hbm, ys_hbm,
                    xbuf, ybuf, w1f, w3f, w2f, w1b, w3b, w2b, xsem, ysem, wsem):
    total = total_ref[0]

    def w_copies(e, slot):
        return [pltpu.make_async_copy(src.at[e], dst.at[slot], wsem.at[slot])
                for src, dst in ((w1_hbm, w1f), (w3_hbm, w3f), (w2_hbm, w2f))]

    def rows(c):
        return (pl.multiple_of(first_ref[c] * ROW_TILE, ROW_TILE), pl.multiple_of(size_ref[c] * ROW_TILE, ROW_TILE))

    def x_copy(c, slot):
        first, size = rows(c)
        return pltpu.make_async_copy(xs_hbm.at[pl.ds(first, size)], xbuf.at[slot, pl.ds(0, size)], xsem.at[slot])

    def y_copy(c, slot):
        first, size = rows(c)
        return pltpu.make_async_copy(ybuf.at[slot, pl.ds(0, size)], ys_hbm.at[pl.ds(first, size)], ysem.at[slot])

    xbuf[...] = jnp.zeros(xbuf.shape, I32)
    for cp in w_copies(e_ref[0], 0):
        cp.start()
    x_copy(0, 0).start()
    half = MOE_BLOCK // 2

    def body(c, carry):
        slot = c % 2

        @pl.when(new_ref[c] == 1)
        def _():
            ws = wslot_ref[c]
            for cp in w_copies(e_ref[c], ws):
                cp.wait()
            w1b[...] = w1f[ws].astype(BF16)
            w3b[...] = w3f[ws].astype(BF16)
            w2b[...] = w2f[ws].astype(BF16)

            @pl.when(next_ref[c] >= 0)
            def _():
                for cp in w_copies(next_ref[c], 1 - ws):
                    cp.start()

        x_copy(c, slot).wait()

        @pl.when(c + 1 < total)
        def _():
            x_copy(c + 1, 1 - slot).start()

        x = jnp.concatenate(_unpack_pairs(_from_row_tiles(xbuf.at[slot], half)), axis=0).astype(BF16)
        h1 = _dot(x, w1b[...])
        h3 = _dot(x, w3b[...])
        hid = (h1 * _sigmoid(h1)) * h3
        y = _dot(hid.astype(BF16), w2b[...]).astype(BF16).astype(F32)

        @pl.when(c >= 2)
        def _():
            y_copy(c - 2, slot).wait()

        _to_row_tiles(ybuf.at[slot], _pack_pairs(y[0:half], y[half:]))
        y_copy(c, slot).start()
        return carry

    lax.fori_loop(0, total, body, 0)

    @pl.when(total >= 2)
    def _():
        y_copy(total - 2, total % 2).wait()

    y_copy(total - 1, (total - 1) % 2).wait()

    ybuf[0] = jnp.zeros(ybuf.shape[1:], I32)
    tail_first = total_ref[1]
    tail_pairs = ys_hbm.shape[0] // ROW_TILE - tail_first

    def fill(j, carry):
        size = pl.multiple_of(jnp.minimum(half, tail_pairs - j * half) * ROW_TILE, ROW_TILE)
        cp = pltpu.make_async_copy(ybuf.at[0, pl.ds(0, size)],
                                   ys_hbm.at[pl.ds(pl.multiple_of((tail_first + j * half) * ROW_TILE, ROW_TILE), size)],
                                   ysem.at[0])
        cp.start()
        cp.wait()
        return carry

    lax.fori_loop(0, (tail_pairs + half - 1) // half, fill, 0)


def _expert_chunks(counts, n_pairs_total):
    E = counts.shape[0]
    block = MOE_BLOCK // 2
    n_max = n_pairs_total // block + E
    starts = jnp.cumsum(counts) - counts
    n_chunks = (counts + block - 1) // block
    c_end = jnp.cumsum(n_chunks)
    c = jnp.arange(n_max, dtype=I32)
    e = jnp.minimum(jnp.sum(c_end[None, :] <= c[:, None], axis=1), E - 1).astype(I32)
    j = c - (c_end - n_chunks)[e]
    size = jnp.clip(counts[e] - j * block, 0, block)
    has_rows = counts > 0
    ids = jnp.where(has_rows, jnp.arange(E, dtype=I32), E)
    later = jnp.flip(lax.cummin(jnp.flip(ids)))
    nxt = jnp.concatenate([later[1:], jnp.full((1,), E, I32)])
    nxt = jnp.where(nxt >= E, -1, nxt)
    ordinal = jnp.cumsum(has_rows.astype(I32)) - 1
    to_i32 = lambda a: a.astype(I32)
    meta = jnp.stack([c_end[-1], jnp.sum(counts)])
    return tuple(map(to_i32, (starts[e] + j * block, size, e, j == 0, nxt[e], ordinal[e] % 2, meta)))


def _experts(chunks, xs, w1, w3, w2):
    rows_total, W = xs.shape
    _, D, De = w1.shape
    any_space = pl.BlockSpec(memory_space=pl.ANY)
    block_rows = MOE_BLOCK // 2 * ROW_TILE
    grid_spec = pltpu.PrefetchScalarGridSpec(
        num_scalar_prefetch=len(chunks),
        grid=(1,),
        in_specs=[any_space] * 4,
        out_specs=any_space,
        scratch_shapes=[pltpu.VMEM((2, block_rows, W), I32), pltpu.VMEM((2, block_rows, W), I32),
                        pltpu.VMEM((2, D, De), F32), pltpu.VMEM((2, D, De), F32), pltpu.VMEM((2, De, D), F32),
                        pltpu.VMEM((D, De), BF16), pltpu.VMEM((D, De), BF16), pltpu.VMEM((De, D), BF16),
                        pltpu.SemaphoreType.DMA((2,)), pltpu.SemaphoreType.DMA((2,)), pltpu.SemaphoreType.DMA((2,))],
    )
    return pl.pallas_call(
        _experts_kernel,
        grid_spec=grid_spec,
        out_shape=jax.ShapeDtypeStruct((rows_total, W), I32),
        compiler_params=_params("arbitrary"),
        name="experts",
    )(*chunks, w1, w3, w2, xs)


def _combine_kernel(src_ref, cnt_ref, dst_ref, ys_hbm, post_ref, wt_ref, h2_ref, x1_ref, mod_ref, ws1_ref,
                    ws3_ref, ws2_ref, fw_ref, o_ref, ybuf, sem):
    tm = h2_ref.shape[0]
    n_pairs = ybuf.shape[0] // ROW_TILE
    n_exp = cnt_ref.shape[2] - 1

    @pl.when(pl.program_id(0) == 0)
    def _():
        ybuf[...] = jnp.zeros(ybuf.shape, I32)

    _run_copies(src_ref, cnt_ref, dst_ref, n_exp,
                lambda src, dst: pltpu.make_async_copy(ys_hbm.at[dst], ybuf.at[src], sem.at[0]))

    h = h2_ref[...]
    a = _dot(h, ws1_ref[...])
    hid = (a * _sigmoid(a)) * _dot(h, ws3_ref[...])
    shared = _dot(hid.astype(BF16), ws2_ref[...])

    pair = 2 * lax.broadcasted_iota(I32, (n_pairs, tm), 0)
    w_cols, mine = [], []
    for odd in range(2):
        w_rows = jnp.zeros((n_pairs, tm), F32)
        for k in range(TOP_K):
            w_rows = jnp.where(pair + odd == post_ref[k:k + 1, :], wt_ref[k:k + 1, :], w_rows)
        w_cols.append(jnp.sum(w_rows, axis=1, keepdims=True))
        mine.append(jnp.where(w_rows != 0.0, 1.0, 0.0).astype(BF16))
    used = pl.multiple_of((src_ref[0, 0, n_exp]) * ROW_TILE, ROW_TILE)
    pltpu.make_async_copy(ys_hbm.at[pl.ds(0, used)], ybuf.at[pl.ds(0, used)], sem.at[0]).wait()
    y_even, y_odd = _unpack_pairs(_from_row_tiles(ybuf, n_pairs))
    sum_rows = lambda hot, y: lax.dot_general(hot, y, (((0,), (0,)), ((), ())), preferred_element_type=F32)
    routed = (sum_rows(mine[0], (y_even * w_cols[0]).astype(BF16))
              + sum_rows(mine[1], (y_odd * w_cols[1]).astype(BF16)))
    x2 = x1_ref[...] + mod_ref[0, 5:6, :] * (routed + shared)
    o_ref[...] = _rms(x2) * fw_ref[...]


def _combine(runs, ys, pos_t, wgt_t, h2, x1, mod, ws1, ws3, ws2, fw, seq, tm):
    N, D = h2.shape
    nt = seq // tm
    row = pl.BlockSpec((tm, D), lambda t: (t, 0))
    full = lambda a: pl.BlockSpec(a.shape, lambda t: (0,) * a.ndim)
    return pl.pallas_call(
        _combine_kernel,
        grid=(N // tm,),
        in_specs=_run_specs(runs) + [pl.BlockSpec(memory_space=pl.ANY),
                                     pl.BlockSpec((TOP_K, tm), lambda t: (0, t)),
                                     pl.BlockSpec((TOP_K, tm), lambda t: (0, t)),
                                     row, row,
                                     pl.BlockSpec((1, 6, D), lambda t: (t // nt, 0, 0)),
                                     full(ws1), full(ws3), full(ws2), full(fw)],
        out_specs=row,
        out_shape=jax.ShapeDtypeStruct((N, D), F32),
        scratch_shapes=[pltpu.VMEM((_tile_pairs(tm, runs[0].shape[2] - 1) * ROW_TILE, D // ROW_TILE), I32),
                        pltpu.SemaphoreType.DMA((1,))],
        compiler_params=_params("arbitrary"),
        name="combine",
    )(*runs, ys, pos_t, wgt_t, h2, x1, mod, ws1, ws3, ws2, fw)


def _t5_bucket(dist):
    n = jnp.maximum(dist, 0)
    max_exact = REL_BUCKETS // 2
    nf = jnp.maximum(n, 1).astype(F32)
    large = max_exact + (jnp.log(nf / max_exact) / math.log(REL_MAX_DIST / max_exact)
                         * (REL_BUCKETS - max_exact)).astype(I32)
    large = jnp.minimum(large, REL_BUCKETS - 1)
    return jnp.where(n < max_exact, n, large)


def _bias_kernel(tbl_ref, far_ref, bt_ref, bw_ref, t_ref, w_ref):
    g = pl.program_id(0)
    d = pl.program_id(1)

    def lookup(bkt, head):
        out = jnp.full(bkt.shape, tbl_ref[0, head], F32)
        for b in range(1, REL_BUCKETS):
            out = jnp.where(bkt == b, tbl_ref[b, head], out)
        return out - far_ref[head]

    j = lax.broadcasted_iota(I32, (QB, QB), 0)
    i = lax.broadcasted_iota(I32, (QB, QB), 1)
    keep = ((d != 0) | (i >= j)) & (d != TILE_MASKED) & ((d != TILE_WINDOW_EDGE) | (j > i))
    for hq in range(GQA):
        t_ref[0, 0, :, hq * QB:(hq + 1) * QB] = jnp.where(keep, lookup(bt_ref[0], GQA * g + hq), NEG)

    @pl.when(d == 0)
    def _():
        for hq in range(GQA):
            w_ref[0, :, hq * QB:(hq + 1) * QB] = lookup(bw_ref[...], GQA * g + hq)


def _bias_tables(rel_table):
    tbl = rel_table.astype(F32) * LOG2E
    far_hi = tbl[REL_BUCKETS - 1].astype(BF16)
    far_lo = (tbl[REL_BUCKETS - 1] - far_hi.astype(F32)).astype(BF16)
    far = far_hi.astype(F32) + far_lo.astype(F32)
    q_tail = jnp.zeros((N_HEADS, HEAD_DIM), F32).at[:, 0].set(far_hi.astype(F32)).at[:, 1].set(far_lo.astype(F32))
    i = jnp.arange(QB)
    tile_dist = np.array(list(range(N_BIAS_TILES)) + [0, WINDOW // QB])
    d_t = (QB * jnp.asarray(tile_dist)[:, None, None] + i[None, None, :] - i[None, :, None])
    d_w = i[None, :] - CMP_STRIDE * (jnp.arange(CMP_WIN)[:, None] - CMP_PAD) - (CMP_LEN - 1)
    n_tiles = len(tile_dist)
    t_tab, w_tab = pl.pallas_call(
        _bias_kernel,
        grid=(N_KV, n_tiles),
        in_specs=[pl.BlockSpec(memory_space=pltpu.SMEM),
                  pl.BlockSpec(memory_space=pltpu.SMEM),
                  pl.BlockSpec((1, QB, QB), lambda g, d: (d, 0, 0)),
                  pl.BlockSpec((CMP_WIN, QB), lambda g, d: (0, 0))],
        out_specs=[pl.BlockSpec((1, 1, QB, GQA * QB), lambda g, d: (g, d, 0, 0)),
                   pl.BlockSpec((1, CMP_WIN, GQA * QB), lambda g, d: (g, 0, 0))],
        out_shape=[jax.ShapeDtypeStruct((N_KV, n_tiles, QB, GQA * QB), F32),
                   jax.ShapeDtypeStruct((N_KV, CMP_WIN, GQA * QB), F32)],
        compiler_params=_params("parallel", "arbitrary"),
        name="bias",
    )(tbl, far, _t5_bucket(d_t), _t5_bucket(d_w))
    far_min = min(FAR_TILE_DIST * QB - QB + 1, CMP_STRIDE * (CMP_PAD + 1) - (CMP_LEN - 1))
    max_exact = REL_BUCKETS // 2
    assert math.log(far_min / max_exact) / math.log(REL_MAX_DIST / max_exact) * max_exact > max_exact - 0.75
    return t_tab, w_tab, q_tail


def _overlap_tables(nc, nb):
    c_start = CMP_STRIDE * np.arange(nc)[:, None]
    s_start = SLC_LEN * np.arange(nb)[None, :]
    ov = ((c_start < s_start + SLC_LEN) & (c_start + CMP_LEN > s_start)).astype(np.float32)
    ov[nc - 1:] = 0.0
    out = np.zeros((nc + CMP_WIN, nb), np.float32)
    out[CMP_PAD:CMP_PAD + nc] = ov
    return jnp.asarray(ov.T, dtype=BF16), jnp.asarray(out)


def _block_diag(w):
    nblk, bs, _ = w.shape
    eye = jnp.eye(nblk, dtype=w.dtype)
    return (eye[:, None, :, None] * w[:, :, None, :]).reshape(nblk * bs, nblk * bs)


def _layer(x, c8, w_ada, b_ada, norm1_w, w_in, cmp_pos, cmp_k_w1, cmp_k_w2, cmp_v_w1, cmp_v_w2, rel_table,
           conv_w, conv_b, lru_wa, lru_ba, lru_wx, lru_bx, lru_lambda, gnorm_attn_w, gnorm_rnn_w, w_out,
           norm2_w, w_router, b_router, w1, w3, w2, ws1, ws3, ws2, final_norm_w):
    B, S, D = x.shape
    N = B * S
    E = w_router.shape[1]
    d_attn = N_HEADS * HEAD_DIM
    d_rnn = D - d_attn
    assert S % (FAR_CHUNK * QB) == 0 and S // SLC_LEN <= MAX_SLC_BLOCKS

    mod = _ada(c8, w_ada, b_ada[None, :])[:B].reshape(B, 6, D)

    n_kv_cols = 6 * N_KV * HEAD_DIM
    n_gate = N_BRANCH * N_HEADS
    wq, wkv, wg, wxr, wgr = jnp.split(w_in, np.cumsum([d_attn, n_kv_cols, n_gate, d_rnn]).tolist(), axis=1)
    wg = jnp.pad(wg.reshape(D, N_KV, GQA * N_BRANCH), ((0, 0), (0, 0), (0, LANES - GQA * N_BRANCH)))
    wg = wg.reshape(D, N_KV * LANES)
    w_pad = jnp.concatenate([wq, wkv, wg, wxr, wgr], axis=1).astype(BF16)
    t_tab, w_tab, q_tail = _bias_tables(rel_table)
    q_t, kx, vx_t, kvc, gates_t, xr, gr = _inproj(x, mod, norm1_w[None, :], w_pad, q_tail)

    nc = S // CMP_STRIDE
    kvc16 = kvc.reshape(B, 4, nc, CMP_STRIDE * HEAD_DIM)
    pos2 = cmp_pos.reshape(2, CMP_STRIDE * HEAD_DIM)
    kvc_pad, kvc_t = _compress(kvc16, pos2, jnp.stack([cmp_k_w1, cmp_v_w1]), jnp.stack([cmp_k_w2, cmp_v_w2]))
    o_attn = _attn(q_t, kx, vx_t, kvc_pad, kvc_t, gates_t, t_tab, w_tab, *_overlap_tables(nc, MAX_SLC_BLOCKS))

    y_rnn = _rglru(xr, gr, conv_w.reshape(CONV_WIDTH, d_rnn), conv_b[None, :], _block_diag(lru_wa).astype(BF16),
                   lru_ba[None, :], _block_diag(lru_wx).astype(BF16), lru_bx[None, :], lru_lambda[None, :],
                   gnorm_rnn_w[None, :])

    x1, h2, s_t = _outproj(o_attn, y_rnn, x, mod, gnorm_attn_w[None, :], w_out.astype(BF16), norm2_w[None, :],
                           w_router.T)

    tile = min(MOE_TILE, S)
    wgt_t, pos_t, tile_counts = _route(s_t, b_router[:, None], tile)
    tcnt = tile_counts[:, :, 0].astype(I32)
    n_tiles = tcnt.shape[0]
    tile_pairs = _tile_pairs(tile, E)
    rpairs = (tcnt + 1) // 2
    rpairs = jnp.concatenate([rpairs, tile_pairs - jnp.sum(rpairs, axis=1, keepdims=True)], axis=1)
    per_expert = jnp.sum(rpairs, axis=0)
    off = jnp.cumsum(per_expert) - per_expert
    run_src = jnp.cumsum(rpairs, axis=1) - rpairs
    run_dst = off[None, :] + jnp.cumsum(rpairs, axis=0) - rpairs
    runs = tuple(r.astype(I32)[:, None, :] for r in (run_src, rpairs, run_dst))

    h2f = h2.reshape(N, D)
    xs = _dispatch(runs, pos_t, h2f, tile)
    ys = _experts(_expert_chunks(per_expert[:E], n_tiles * tile_pairs), xs, w1, w3, w2)
    out = _combine(runs, ys, pos_t, wgt_t, h2f, x1.reshape(N, D), mod, ws1.astype(BF16), ws3.astype(BF16),
                   ws2.astype(BF16), final_norm_w[None, :], S, tile)
    return out.reshape(B, S, D)


def kernel(x, c, w_ada, b_ada, norm1_w, w_in, cmp_pos, cmp_k_w1, cmp_k_w2, cmp_v_w1, cmp_v_w2, rel_table, conv_w, conv_b, lru_wa, lru_ba, lru_wx, lru_bx, lru_lambda, gnorm_attn_w, gnorm_rnn_w, w_out, norm2_w, w_router, b_router, w1, w3, w2, ws1, ws3, ws2, final_norm_w):
    assert w_ada.shape[0] == 1
    c8 = jnp.pad(c, ((0, 8 - c.shape[0]), (0, 0)))
    return _layer(x, c8, w_ada[0], b_ada[0], norm1_w[0], w_in[0], cmp_pos[0], cmp_k_w1[0], cmp_k_w2[0],
                  cmp_v_w1[0], cmp_v_w2[0], rel_table, conv_w[0], conv_b[0], lru_wa[0], lru_ba[0], lru_wx[0],
                  lru_bx[0], lru_lambda[0], gnorm_attn_w[0], gnorm_rnn_w[0], w_out[0], norm2_w[0], w_router[0],
                  b_router[0], w1[0], w3[0], w2[0], ws1[0], ws3[0], ws2[0], final_norm_w)
```

```python
import math

import jax
import jax.numpy as jnp
import numpy as np
from jax import lax
from jax.experimental import pallas as pl
from jax.experimental.pallas import tpu as pltpu

F32 = jnp.float32
BF16 = jnp.bfloat16
I32 = jnp.int32

HEAD_DIM = 64
N_HEADS = 8
N_KV = 2
GQA = N_HEADS // N_KV
N_BRANCH = 3
CONV_WIDTH = 4
LRU_C = 8.0
CMP_LEN = 32
CMP_STRIDE = 16
SLC_LEN = 64
SLC_TOP = 16
WINDOW = 512
LANES = 128
QB = 128
MAX_SLC_BLOCKS = 128
GATE_ROWS = 16
V_ROWS = 80
REL_BUCKETS = 32
REL_MAX_DIST = 1024
N_EXPERT_GROUPS = 8
TOP_GROUPS = 4
TOP_K = 8
ROUTED_SCALE = 2.5
MOE_BLOCK = 576
MOE_TILE = 256
ROW_TILE = 8
EPS = 1e-6
NEG = -1e30
LOG2E = math.log2(math.e)
CMP_PAD = 120
CMP_WIN = 128
N_BIAS_TILES = 11
TILE_MASKED = N_BIAS_TILES
TILE_WINDOW_EDGE = N_BIAS_TILES + 1
FAR_TILE_DIST = 8
FAR_CHUNK = 4
VMEM_LIMIT = 52 * 1024 * 1024


def _dot(a, b, **kw):
    return jnp.dot(a, b, preferred_element_type=F32, **kw)


def _dot_nt(a, b, **kw):
    return lax.dot_general(a, b, (((1,), (1,)), ((), ())), preferred_element_type=F32, **kw)


def _gelu(x):
    return 0.5 * x * (1.0 + jnp.tanh(math.sqrt(2.0 / math.pi) * (x + 0.044715 * (x * x * x))))


def _sigmoid(x):
    return 1.0 / (1.0 + jnp.exp(-x))


def _rms(x):
    return x * lax.rsqrt(jnp.mean(x * x, axis=-1, keepdims=True) + EPS)


def _params(*sem):
    return pltpu.CompilerParams(dimension_semantics=sem, vmem_limit_bytes=VMEM_LIMIT)


def _ada_kernel(c_ref, w_ref, b_ref, o_ref):
    c = c_ref[...]
    a = c * _sigmoid(c)
    o_ref[...] = _dot(a, w_ref[...], precision=lax.Precision.HIGHEST) + b_ref[...]


def _ada(c8, w, b):
    d, n = w.shape
    tn = 1536
    return pl.pallas_call(
        _ada_kernel,
        grid=(n // tn,),
        in_specs=[pl.BlockSpec((8, d), lambda j: (0, 0)),
                  pl.BlockSpec((d, tn), lambda j: (0, j)),
                  pl.BlockSpec((1, tn), lambda j: (0, j))],
        out_specs=pl.BlockSpec((8, tn), lambda j: (0, j)),
        out_shape=jax.ShapeDtypeStruct((8, n), F32),
        compiler_params=_params("parallel"),
        name="ada",
    )(c8, w, b)


def _inproj_kernel(x_ref, mod_ref, nw_ref, w_ref, qt_ref, q_ref, kx_ref, vx_ref, kvc_ref, g_ref, xr_ref, gr_ref):
    h = _rms(x_ref[0]) * nw_ref[...]
    h = h * (1.0 + mod_ref[0, 1:2, :]) + mod_ref[0, 0:1, :]
    p = _dot(h.astype(BF16), w_ref[...])
    tm = p.shape[0]
    dq = N_HEADS * HEAD_DIM
    lane = lax.broadcasted_iota(I32, (tm, HEAD_DIM), 1)
    k_tail = jnp.where(lane < 2, 1.0, 0.0)
    v_tail = jnp.where(lane < 1, 1.0, 0.0)
    for hh in range(N_HEADS):
        qh = p[:, hh * HEAD_DIM:(hh + 1) * HEAD_DIM] * (HEAD_DIM ** -0.5 * LOG2E)
        q_tail = jnp.broadcast_to(qt_ref[hh:hh + 1, :], (tm, HEAD_DIM))
        q_t = jnp.concatenate([qh, q_tail], axis=1).T.astype(BF16)
        for blk in range(tm // QB):
            col = (blk * GQA + hh % GQA) * QB
            q_ref[0, hh // GQA, :, col:col + QB] = q_t[:, blk * QB:(blk + 1) * QB]
    for j in range(6):
        for g in range(N_KV):
            col = dq + (j * N_KV + g) * HEAD_DIM
            piece = p[:, col:col + HEAD_DIM]
            if j < 2:
                kvc_ref[0, j * N_KV + g] = piece
            elif j % 2 == 0:
                kx_ref[0, (j // 2 - 1) * N_KV + g] = jnp.concatenate([piece, k_tail], axis=1).astype(BF16)
            else:
                v_t = jnp.concatenate([piece, v_tail], axis=1).T.astype(BF16)
                for blk in range(tm // QB):
                    vx_ref[0, (j // 2 - 1) * N_KV + g, blk] = v_t[0:V_ROWS, blk * QB:(blk + 1) * QB]
    c0 = dq + 6 * N_KV * HEAD_DIM
    for g in range(N_KV):
        g_ref[0, g] = _sigmoid(p[:, c0 + g * LANES:c0 + (g + 1) * LANES]).T[0:GATE_ROWS, :]
    c1 = c0 + N_KV * LANES
    d_rnn = xr_ref.shape[2]
    xr_ref[0] = p[:, c1:c1 + d_rnn]
    gr_ref[0] = p[:, c1 + d_rnn:c1 + 2 * d_rnn]


def _inproj(x, mod, nw, w_pad, q_tail, tm=256):
    B, S, D = x.shape
    ncol = w_pad.shape[1]
    heads = lambda n, w: pl.BlockSpec((1, n, tm, w), lambda b, t: (b, 0, t, 0))
    return pl.pallas_call(
        _inproj_kernel,
        grid=(B, S // tm),
        in_specs=[pl.BlockSpec((1, tm, D), lambda b, t: (b, t, 0)),
                  pl.BlockSpec((1, 6, D), lambda b, t: (b, 0, 0)),
                  pl.BlockSpec((1, D), lambda b, t: (0, 0)),
                  pl.BlockSpec((D, ncol), lambda b, t: (0, 0)),
                  pl.BlockSpec(q_tail.shape, lambda b, t: (0, 0))],
        out_specs=[pl.BlockSpec((1, N_KV, 2 * HEAD_DIM, tm * GQA), lambda b, t: (b, 0, 0, t)),
                   heads(4, 2 * HEAD_DIM),
                   pl.BlockSpec((1, 4, tm // QB, V_ROWS, QB), lambda b, t: (b, 0, t, 0, 0)),
                   heads(4, HEAD_DIM),
                   pl.BlockSpec((1, N_KV, GATE_ROWS, tm), lambda b, t: (b, 0, 0, t)),
                   pl.BlockSpec((1, tm, 512), lambda b, t: (b, t, 0)),
                   pl.BlockSpec((1, tm, 512), lambda b, t: (b, t, 0))],
        out_shape=[jax.ShapeDtypeStruct((B, N_KV, 2 * HEAD_DIM, S * GQA), BF16),
                   jax.ShapeDtypeStruct((B, 4, S, 2 * HEAD_DIM), BF16),
                   jax.ShapeDtypeStruct((B, 4, S // QB, V_ROWS, QB), BF16),
                   jax.ShapeDtypeStruct((B, 4, S, HEAD_DIM), F32),
                   jax.ShapeDtypeStruct((B, N_KV, GATE_ROWS, S), F32),
                   jax.ShapeDtypeStruct((B, S, 512), F32),
                   jax.ShapeDtypeStruct((B, S, 512), F32)],
        compiler_params=_params("parallel", "parallel"),
        name="inproj",
    )(x, mod, nw, w_pad, q_tail)


def _compress_kernel(x_ref, pos_ref, w1_ref, w2_ref, o_ref, ot_ref):
    x = x_ref[0, 0]
    nc = x.shape[0]
    half = CMP_STRIDE * HEAD_DIM
    kv = pl.program_id(1) // N_KV
    a = _dot((x + pos_ref[0:1, :]).astype(BF16), w1_ref[kv, 0:half, :].astype(BF16))
    b = _dot((x + pos_ref[1:2, :]).astype(BF16), w1_ref[kv, half:2 * half, :].astype(BF16))
    hid = _gelu(a + pltpu.roll(b, nc - 1, 0))
    out = _dot(hid.astype(BF16), w2_ref[kv].astype(BF16))
    row = lax.broadcasted_iota(I32, out.shape, 0)
    out = jnp.where(row < nc - 1, out, 0.0)
    lane = lax.broadcasted_iota(I32, (nc, HEAD_DIM), 1)
    out = jnp.concatenate([out, jnp.where(lane < 2, 1.0, 0.0)], axis=1)
    o_ref[0, 0, 0:CMP_PAD, :] = jnp.zeros((CMP_PAD, 2 * HEAD_DIM), F32)
    o_ref[0, 0, CMP_PAD:CMP_PAD + nc, :] = out
    o_ref[0, 0, CMP_PAD + nc:, :] = jnp.zeros((CMP_WIN - CMP_PAD, 2 * HEAD_DIM), F32)
    ot_ref[0, 0] = out.T


def _compress(kvc, pos2, w1, w2):
    B, _, NC, W = kvc.shape
    return pl.pallas_call(
        _compress_kernel,
        grid=(B, 4),
        in_specs=[pl.BlockSpec((1, 1, NC, W), lambda b, i: (b, i, 0, 0)),
                  pl.BlockSpec((2, W), lambda b, i: (0, 0)),
                  pl.BlockSpec((2, 2 * W, HEAD_DIM), lambda b, i: (0, 0, 0)),
                  pl.BlockSpec((2, HEAD_DIM, HEAD_DIM), lambda b, i: (0, 0, 0))],
        out_specs=[pl.BlockSpec((1, 1, NC + CMP_WIN, 2 * HEAD_DIM), lambda b, i: (b, i, 0, 0)),
                   pl.BlockSpec((1, 1, 2 * HEAD_DIM, NC), lambda b, i: (b, i, 0, 0))],
        out_shape=[jax.ShapeDtypeStruct((B, 4, NC + CMP_WIN, 2 * HEAD_DIM), F32),
                   jax.ShapeDtypeStruct((B, 4, 2 * HEAD_DIM, NC), F32)],
        compiler_params=_params("parallel", "parallel"),
        name="compress",
    )(kvc, pos2, w1, w2)


def _attn_kernel(q_ref, ks_ref, vs_ref, kw_ref, vw_ref, kc_ref, vc_ref, vct_ref, g_ref, t_ref, w_ref, ovt_ref, ov_ref,
                 o_ref, m_scr, acc_scr, sel_scr, sa_scr, sb_scr, sc_scr):
    qb = pl.program_id(2)
    R = GQA * QB
    q = q_ref[0, 0]
    nc = vct_ref.shape[3]
    nb = ov_ref.shape[1]
    n_top = min(SLC_TOP, ks_ref.shape[2] // SLC_LEN)

    def heads_sum(x):
        out = x[:, 0:QB]
        for hq in range(1, GQA):
            out = out + x[:, hq * QB:(hq + 1) * QB]
        return out

    w0 = pl.multiple_of(qb * 8, 8)
    s_far = _dot(kc_ref[0, 0, CMP_PAD:CMP_PAD + nc, :].astype(BF16), q)
    n_io = lax.broadcasted_iota(I32, (nc, R), 0)
    s_far = jnp.where(n_io < qb * 8 - CMP_PAD, s_far, -jnp.inf)
    s_win = _dot(kc_ref[0, 0, pl.ds(w0, CMP_WIN), :].astype(BF16), q) + w_ref[0]
    j_w = lax.broadcasted_iota(I32, (CMP_WIN, R), 0)
    i_w = lax.broadcasted_iota(I32, (CMP_WIN, R), 1) & (QB - 1)
    dist_w = i_w - CMP_STRIDE * (j_w - CMP_PAD) - (CMP_LEN - 1)
    s_win = jnp.where(dist_w >= 0, jnp.where(j_w >= CMP_PAD - qb * 8, s_win, -jnp.inf), -jnp.inf)
    m = jnp.maximum(jnp.max(s_far, axis=0, keepdims=True), jnp.max(s_win, axis=0, keepdims=True))
    m = jnp.where(m == -jnp.inf, 0.0, m)
    e_far = jnp.exp2(s_far - m)
    e_win = jnp.exp2(s_win - m)
    l = jnp.sum(e_far, axis=0, keepdims=True) + jnp.sum(e_win, axis=0, keepdims=True)
    inv = 1.0 / jnp.maximum(l, 1e-30)
    p_far = e_far * inv
    p_win = e_win * inv
    vcw_t = vc_ref[0, 0, pl.ds(w0, CMP_WIN), :].T[0:V_ROWS, :]
    o_c = (_dot(vct_ref[0, 0, 0:V_ROWS, :].astype(BF16), p_far.astype(BF16))
           + _dot(vcw_t.astype(BF16), p_win.astype(BF16)))
    ov_win_t = ov_ref[pl.ds(w0, CMP_WIN), :].T
    sc_t = (_dot(ovt_ref[...], heads_sum(p_far).astype(BF16))
            + _dot(ov_win_t.astype(BF16), heads_sum(p_win).astype(BF16)))

    n_wt = WINDOW // QB + 1
    k_t, v_t, b_t = [], [], []
    for j in range(n_wt):
        kb = qb - (n_wt - 1) + j
        kb0 = jnp.maximum(kb, 0)
        k_t.append(kw_ref[0, 0, pl.ds(pl.multiple_of(kb0 * QB, QB), QB), :])
        v_t.append(vw_ref[0, 0, kb0])
        b_t.append(t_ref[0, jnp.where(kb < 0, TILE_MASKED, TILE_WINDOW_EDGE if j == 0 else n_wt - 1 - j)])
    s = _dot(jnp.concatenate(k_t, axis=0), q) + jnp.concatenate(b_t, axis=0)
    p = jnp.exp2(s - jnp.max(s, axis=0, keepdims=True))
    acc = _dot(jnp.concatenate(v_t, axis=1), p.astype(BF16))
    o_w = acc[0:HEAD_DIM, :] / acc[HEAD_DIM:HEAD_DIM + 1, :]

    chunk = FAR_CHUNK * QB

    def qk(c):
        return _dot(ks_ref[0, 0, pl.ds(pl.multiple_of(c * chunk, chunk), chunk), :], q)

    n_far = jnp.maximum(qb - (FAR_TILE_DIST - 1), 0) // FAR_CHUNK
    sa_scr[...] = qk(0)
    sc_scr[...] = qk(n_far)

    blk = lax.broadcasted_iota(I32, (nb, QB), 0)
    t_io = lax.broadcasted_iota(I32, (nb, QB), 1)
    cur = qb * (QB // SLC_LEN) + t_io // SLC_LEN
    val = jnp.where(blk <= cur, sc_t, -jnp.inf)
    for forced_blk in (0, cur, cur - 1):
        val = jnp.where(blk == forced_blk, jnp.inf, val)
    drop = jnp.full((nb, QB), NEG, F32)
    for _ in range(n_top):
        mx = jnp.max(val, axis=0, keepdims=True)
        first = jnp.min(jnp.where(val == mx, blk, nb), axis=0, keepdims=True)
        first = jnp.where(mx > -jnp.inf, first, nb)
        pick = blk == first
        drop = jnp.where(pick, 0.0, drop)
        val = jnp.where(pick, -jnp.inf, val)
    sel_scr[...] = drop

    m_scr[...] = jnp.full(m_scr.shape, NEG, F32)
    acc_scr[...] = jnp.zeros(acc_scr.shape, F32)
    blocks_per_chunk = FAR_CHUNK * QB // SLC_LEN

    def soft(c, s):
        rows = sel_scr[pl.ds(pl.multiple_of(c * blocks_per_chunk, blocks_per_chunk), blocks_per_chunk), :]
        drop = jnp.concatenate([jnp.broadcast_to(rows[r:r + 1, :], (SLC_LEN, QB)) for r in range(blocks_per_chunk)],
                               axis=0)
        s = s + jnp.concatenate([drop] * GQA, axis=1)
        m_old = m_scr[...]
        m_new = jnp.maximum(m_old, jnp.max(s, axis=0, keepdims=True))
        p = jnp.exp2(s - m_new)
        v_t = jnp.concatenate([vs_ref[0, 0, c * FAR_CHUNK + j] for j in range(FAR_CHUNK)], axis=1)
        acc_scr[...] = jnp.exp2(m_old - m_new) * acc_scr[...] + _dot(v_t, p.astype(BF16))
        m_scr[...] = m_new

    def near_bias(c):
        tiles = []
        for j in range(FAR_CHUNK):
            delta = qb - (c * FAR_CHUNK + j)
            tiles.append(t_ref[0, jnp.where(delta < 0, TILE_MASKED, delta)])
        return jnp.concatenate(tiles, axis=0)

    def chunk_pairs(first, count, bias, sa):
        last = first + jnp.maximum(count - 1, 0)

        def pair_body(j, carry):
            c = first + 2 * j
            sb_scr[...] = qk(c + 1)
            soft(c, sa[...] + bias(c) if bias else sa[...])
            sa[...] = qk(jnp.minimum(c + 2, last))
            soft(c + 1, sb_scr[...] + bias(c + 1) if bias else sb_scr[...])
            return carry

        lax.fori_loop(0, count // 2, pair_body, 0)

        @pl.when(count % 2 == 1)
        def _():
            soft(last, sa[...] + bias(last) if bias else sa[...])

    chunk_pairs(0, n_far, None, sa_scr)
    chunk_pairs(n_far, qb // FAR_CHUNK + 1 - n_far, near_bias, sc_scr)
    acc = acc_scr[...]
    o_s = acc[0:HEAD_DIM, :] / acc[HEAD_DIM:HEAD_DIM + 1, :]

    gates = g_ref[0, 0]
    outs = []
    for hq in range(GQA):
        cols = slice(hq * QB, (hq + 1) * QB)
        c = hq * N_BRANCH
        outs.append(gates[c:c + 1, :] * o_c[0:HEAD_DIM, cols] + gates[c + 1:c + 2, :] * o_s[:, cols]
                    + gates[c + 2:c + 3, :] * o_w[:, cols])
    o_ref[0] = jnp.concatenate(outs, axis=0).T


def _attn(q_t, kx, vx_t, kvc, kvc_t, gates_t, t_tab, w_tab, ov_t, ov):
    B, _, W, _ = q_t.shape
    S = kx.shape[2]
    ncp = kvc.shape[2]
    nb = ov.shape[1]
    keys = lambda j: pl.BlockSpec((1, 1, S, W), lambda b, g, t: (b, j * N_KV + g, 0, 0))
    vals = lambda j: pl.BlockSpec((1, 1, S // QB, V_ROWS, QB), lambda b, g, t: (b, j * N_KV + g, 0, 0, 0))
    cmp = lambda j: pl.BlockSpec((1, 1, ncp, W), lambda b, g, t: (b, j * N_KV + g, 0, 0))
    return pl.pallas_call(
        _attn_kernel,
        grid=(B, N_KV, S // QB),
        in_specs=[pl.BlockSpec((1, 1, W, GQA * QB), lambda b, g, t: (b, g, 0, t)),
                  keys(0), vals(0), keys(1), vals(1), cmp(0), cmp(1),
                  pl.BlockSpec((1, 1, W, ncp - CMP_WIN), lambda b, g, t: (b, N_KV + g, 0, 0)),
                  pl.BlockSpec((1, 1, GATE_ROWS, QB), lambda b, g, t: (b, g, 0, t)),
                  pl.BlockSpec((1,) + t_tab.shape[1:], lambda b, g, t: (g, 0, 0, 0)),
                  pl.BlockSpec((1, CMP_WIN, GQA * QB), lambda b, g, t: (g, 0, 0)),
                  pl.BlockSpec(ov_t.shape, lambda b, g, t: (0, 0)),
                  pl.BlockSpec(ov.shape, lambda b, g, t: (0, 0))],
        out_specs=pl.BlockSpec((1, QB, GQA * HEAD_DIM), lambda b, g, t: (b, t, g)),
        out_shape=jax.ShapeDtypeStruct((B, S, N_HEADS * HEAD_DIM), F32),
        scratch_shapes=[pltpu.VMEM((1, GQA * QB), F32), pltpu.VMEM((V_ROWS, GQA * QB), F32), pltpu.VMEM((nb, QB), F32),
                        pltpu.VMEM((FAR_CHUNK * QB, GQA * QB), F32), pltpu.VMEM((FAR_CHUNK * QB, GQA * QB), F32),
                        pltpu.VMEM((FAR_CHUNK * QB, GQA * QB), F32)],
        compiler_params=_params("parallel", "parallel", "arbitrary"),
        name="attn",
    )(q_t, kx, vx_t, kx, vx_t, kvc, kvc, kvc_t, gates_t, t_tab, w_tab, ov_t, ov)


def _rglru_kernel(xr_ref, gr_ref, cw_ref, cb_ref, wa_ref, ba_ref, wx_ref, bx_ref, lam_ref, gw_ref, o_ref,
                  xbuf, hprev, a_scr, u_scr, h_scr):
    ts = xr_ref.shape[1]
    C = xr_ref.shape[2]

    @pl.when(pl.program_id(1) == 0)
    def _():
        xbuf[0:8, :] = jnp.zeros((8, C), F32)
        hprev[...] = jnp.zeros(hprev.shape, F32)

    xbuf[8:8 + ts, :] = xr_ref[0]
    xc = cb_ref[...] + jnp.zeros((ts, C), F32)
    for j in range(CONV_WIDTH):
        xc = xc + cw_ref[j:j + 1, :] * xbuf[pl.ds(8 - (CONV_WIDTH - 1) + j, ts), :]
    xbuf[0:8, :] = xbuf[ts:ts + 8, :]

    xcb = xc.astype(BF16)
    r = _sigmoid(_dot(xcb, wa_ref[...]) + ba_ref[...])
    i = _sigmoid(_dot(xcb, wx_ref[...]) + bx_ref[...])
    z = -lam_ref[...]
    softplus = jnp.maximum(z, 0.0) + jnp.log(1.0 + jnp.exp(-jnp.abs(z)))
    log_a = -LRU_C * r * softplus
    a = jnp.exp(log_a)
    a_scr[...] = a
    u_scr[...] = jnp.sqrt(1.0 - a * a) * (i * xc)

    row = lax.broadcasted_iota(I32, (8, C), 0)

    def body(k, h):
        r0 = pl.multiple_of(k * 8, 8)
        a = a_scr[pl.ds(r0, 8), :]
        b = u_scr[pl.ds(r0, 8), :]
        for s in (1, 2, 4):
            keep = row >= s
            b = jnp.where(keep, a * pltpu.roll(b, s, 0) + b, b)
            a = jnp.where(keep, a * pltpu.roll(a, s, 0), a)
        hh = a * h + b
        h_scr[pl.ds(r0, 8), :] = hh
        return jnp.broadcast_to(hh[7:8, :], (8, C))

    hprev[...] = lax.fori_loop(0, ts // 8, body, hprev[...])
    out = h_scr[...] * _gelu(gr_ref[0])
    o_ref[0] = (_rms(out) * gw_ref[...]).astype(BF16)


def _rglru(xr, gr, cw, cb, wa_bd, ba, wx_bd, bx, lam, gw, ts=512):
    B, S, C = xr.shape
    ts = min(ts, S)
    vec = pl.BlockSpec((1, C), lambda b, t: (0, 0))
    mat = pl.BlockSpec((C, C), lambda b, t: (0, 0))
    seq = pl.BlockSpec((1, ts, C), lambda b, t: (b, t, 0))
    return pl.pallas_call(
        _rglru_kernel,
        grid=(B, S // ts),
        in_specs=[seq, seq, pl.BlockSpec((CONV_WIDTH, C), lambda b, t: (0, 0)), vec, mat, vec, mat, vec, vec, vec],
        out_specs=seq,
        out_shape=jax.ShapeDtypeStruct((B, S, C), BF16),
        scratch_shapes=[pltpu.VMEM((ts + 8, C), F32), pltpu.VMEM((8, C), F32), pltpu.VMEM((ts, C), F32),
                        pltpu.VMEM((ts, C), F32), pltpu.VMEM((ts, C), F32)],
        compiler_params=_params("parallel", "arbitrary"),
        name="rglru",
    )(xr, gr, cw, cb, wa_bd, ba, wx_bd, bx, lam, gw)


def _to_row_tiles(ref, x):
    rows = x.shape[0]
    for s in range(ROW_TILE):
        ref[pl.ds(s, rows, stride=ROW_TILE), :] = x[:, s * LANES:(s + 1) * LANES]


def _from_row_tiles(ref, rows, start=0):
    return jnp.concatenate([ref[pl.ds(start + s, rows, stride=ROW_TILE), :] for s in range(ROW_TILE)], axis=1)


def _outproj_kernel(oa_ref, yr_ref, x_ref, mod_ref, gaw_ref, wo_ref, n2_ref, wr_ref, x1_ref, h2_ref, st_ref):
    da = oa_ref.shape[2]
    ya = (_rms(oa_ref[0]) * gaw_ref[...]).astype(BF16)
    mix = _dot(ya, wo_ref[0:da, :]) + _dot(yr_ref[0], wo_ref[da:, :])
    x1 = x_ref[0] + mod_ref[0, 2:3, :] * mix
    x1_ref[0] = x1
    h2 = (_rms(x1) * n2_ref[...]) * (1.0 + mod_ref[0, 4:5, :]) + mod_ref[0, 3:4, :]
    h2_ref[0] = h2.astype(BF16)
    st_ref[...] = _sigmoid(_dot_nt(wr_ref[...], h2, precision=lax.Precision.HIGHEST))


def _outproj(oa, yr, x, mod, gaw, wo, n2, wr_t, tm=256):
    B, S, D = x.shape
    E = wr_t.shape[0]
    nt = S // tm
    row = lambda w: pl.BlockSpec((1, tm, w), lambda b, t: (b, t, 0))
    return pl.pallas_call(
        _outproj_kernel,
        grid=(B, nt),
        in_specs=[row(oa.shape[2]), row(yr.shape[2]), row(D),
                  pl.BlockSpec((1, 6, D), lambda b, t: (b, 0, 0)),
                  pl.BlockSpec((1, oa.shape[2]), lambda b, t: (0, 0)),
                  pl.BlockSpec(wo.shape, lambda b, t: (0, 0)),
                  pl.BlockSpec((1, D), lambda b, t: (0, 0)),
                  pl.BlockSpec((E, D), lambda b, t: (0, 0))],
        out_specs=[row(D), row(D), pl.BlockSpec((E, tm), lambda b, t: (0, b * nt + t))],
        out_shape=[jax.ShapeDtypeStruct((B, S, D), F32), jax.ShapeDtypeStruct((B, S, D), BF16),
                   jax.ShapeDtypeStruct((E, B * S), F32)],
        compiler_params=_params("parallel", "parallel"),
        name="outproj",
    )(oa, yr, x, mod, gaw, wo, n2, wr_t)


def _route_kernel(s_ref, b_ref, wgt_ref, pos_ref, tcnt_ref):
    E, tn = s_ref.shape
    per = E // N_EXPERT_GROUPS
    s = s_ref[...]
    s_sel = s + b_ref[...]
    eidx = lax.broadcasted_iota(I32, (E, tn), 0)

    grp = []
    for gi in range(N_EXPERT_GROUPS):
        xg = s_sel[gi * per:(gi + 1) * per, :]
        ig = lax.broadcasted_iota(I32, (per, tn), 0)
        m1 = jnp.max(xg, axis=0, keepdims=True)
        f1 = jnp.min(jnp.where(xg == m1, ig, per), axis=0, keepdims=True)
        m2 = jnp.max(jnp.where(ig == f1, -jnp.inf, xg), axis=0, keepdims=True)
        grp.append(m1 + m2)
    val = []
    for gi in range(N_EXPERT_GROUPS):
        rank = jnp.zeros((1, tn), I32)
        for gj in range(N_EXPERT_GROUPS):
            if gj == gi:
                continue
            ahead = (grp[gj] > grp[gi]) | ((grp[gj] == grp[gi]) & (gj < gi))
            rank = rank + ahead.astype(I32)
        val.append(jnp.where(rank < TOP_GROUPS, s_sel[gi * per:(gi + 1) * per, :], -jnp.inf))
    val = jnp.concatenate(val, axis=0)

    eids, tops = [], []
    hot = jnp.zeros((E, tn), F32)
    for _ in range(TOP_K):
        mx = jnp.max(val, axis=0, keepdims=True)
        first = jnp.min(jnp.where(val == mx, eidx, E), axis=0, keepdims=True)
        pick = eidx == first
        eids.append(first)
        tops.append(jnp.sum(jnp.where(pick, s, 0.0), axis=0, keepdims=True))
        hot = jnp.where(pick, 1.0, hot)
        val = jnp.where(pick, -jnp.inf, val)
    denom = tops[0]
    for t in tops[1:]:
        denom = denom + t

    ti = lax.broadcasted_iota(I32, (tn, tn), 0)
    tj = lax.broadcasted_iota(I32, (tn, tn), 1)
    upper = jnp.where(ti < tj, 1.0, 0.0).astype(BF16)
    before = _dot(hot.astype(BF16), upper)
    tile_count = jnp.sum(hot, axis=1, keepdims=True)
    padded = tile_count + (tile_count - 2.0 * jnp.floor(tile_count * 0.5))
    ei = lax.broadcasted_iota(I32, (E, E), 0)
    ej = lax.broadcasted_iota(I32, (E, E), 1)
    earlier_experts = jnp.where(ej < ei, 1.0, 0.0).astype(BF16)
    run_start = _dot(earlier_experts, jnp.broadcast_to(padded, (E, LANES)).astype(BF16))[:, 0:1]
    for k in range(TOP_K):
        wgt_ref[k:k + 1, :] = ROUTED_SCALE * tops[k] / denom
        pos_ref[k:k + 1, :] = jnp.sum(jnp.where(eidx == eids[k], before + run_start, 0.0), axis=0,
                                      keepdims=True).astype(I32)
    tcnt_ref[0] = tile_count


def _route(s_t, b_col, tn):
    E, N = s_t.shape
    out = pl.BlockSpec((TOP_K, tn), lambda t: (0, t))
    return pl.pallas_call(
        _route_kernel,
        grid=(N // tn,),
        in_specs=[pl.BlockSpec((E, tn), lambda t: (0, t)), pl.BlockSpec((E, 1), lambda t: (0, 0))],
        out_specs=[out, out, pl.BlockSpec((1, E, 1), lambda t: (t, 0, 0))],
        out_shape=[jax.ShapeDtypeStruct((TOP_K, N), F32), jax.ShapeDtypeStruct((TOP_K, N), I32),
                   jax.ShapeDtypeStruct((N // tn, E, 1), F32)],
        compiler_params=_params("parallel"),
        name="route",
    )(s_t, b_col)


_HI = -65536


def _pack_pairs(even_rows, odd_rows):
    hi = lax.bitcast_convert_type(even_rows, I32) & _HI
    lo = lax.shift_right_logical(lax.bitcast_convert_type(odd_rows, I32), 16)
    return hi | lo


def _unpack_pairs(words):
    even_rows = lax.bitcast_convert_type(words & _HI, F32)
    odd_rows = lax.bitcast_convert_type(lax.shift_left(words, 16), F32)
    return even_rows, odd_rows


def _run_copies(src_ref, cnt_ref, dst_ref, n_runs, copy):
    def go(e, carry):
        n = cnt_ref[0, 0, e]

        @pl.when(n > 0)
        def _():
            size = pl.multiple_of(n * ROW_TILE, ROW_TILE)
            copy(pl.ds(pl.multiple_of(src_ref[0, 0, e] * ROW_TILE, ROW_TILE), size),
                 pl.ds(pl.multiple_of(dst_ref[0, 0, e] * ROW_TILE, ROW_TILE), size)).start()
        return carry
    lax.fori_loop(0, n_runs, go, 0, unroll=8)


def _pair_onehots(pos_rows, n_pairs, first_pair, width):
    even = 2 * (lax.broadcasted_iota(I32, (n_pairs, width), 0) + first_pair)
    hot_e = jnp.zeros((n_pairs, width), F32)
    hot_o = jnp.zeros((n_pairs, width), F32)
    for k in range(TOP_K):
        hot_e = jnp.where(even == pos_rows[k:k + 1, :], 1.0, hot_e)
        hot_o = jnp.where(even + 1 == pos_rows[k:k + 1, :], 1.0, hot_o)
    return hot_e, hot_o


def _dispatch_kernel(src_ref, cnt_ref, dst_ref, pos_ref, h_ref, xs_hbm, xbuf, sem):
    n_pairs = xbuf.shape[0] // ROW_TILE
    tn = h_ref.shape[0]
    hb = h_ref[...]
    slab = n_pairs // 3
    for sl in range(3):
        hot_e, hot_o = _pair_onehots(pos_ref[...], slab, sl * slab, tn)
        words = _pack_pairs(_dot(hot_e.astype(BF16), hb), _dot(hot_o.astype(BF16), hb))
        for s in range(ROW_TILE):
            xbuf[pl.ds(sl * slab * ROW_TILE + s, slab, stride=ROW_TILE), :] = words[:, s * LANES:(s + 1) * LANES]
    _run_copies(src_ref, cnt_ref, dst_ref, cnt_ref.shape[2],
                lambda src, dst: pltpu.make_async_copy(xbuf.at[src], xs_hbm.at[dst], sem.at[0]))
    pltpu.make_async_copy(xbuf, xs_hbm.at[pl.ds(0, xbuf.shape[0])], sem.at[0]).wait()


def _run_specs(runs):
    return [pl.BlockSpec((1, 1, r.shape[2]), lambda t: (t, 0, 0), memory_space=pltpu.SMEM) for r in runs]


def _tile_pairs(tn, n_experts):
    return (TOP_K * tn + n_experts) // 2


def _dispatch(runs, pos_t, h2, tn):
    N, D = h2.shape
    W = D // ROW_TILE
    n_pairs = _tile_pairs(tn, runs[0].shape[2] - 1)
    assert n_pairs % 24 == 0
    return pl.pallas_call(
        _dispatch_kernel,
        grid=(N // tn,),
        in_specs=_run_specs(runs) + [pl.BlockSpec((TOP_K, tn), lambda t: (0, t)),
                                     pl.BlockSpec((tn, D), lambda t: (t, 0))],
        out_specs=pl.BlockSpec(memory_space=pl.ANY),
        out_shape=jax.ShapeDtypeStruct((N // tn * n_pairs * ROW_TILE, W), I32),
        scratch_shapes=[pltpu.VMEM((n_pairs * ROW_TILE, W), I32), pltpu.SemaphoreType.DMA((1,))],
        compiler_params=_params("arbitrary"),
        name="dispatch",
    )(*runs, pos_t, h2)


def _experts_kernel(first_ref, size_ref, e_ref, new_ref, next_ref, wslot_ref, total_ref,
                    w1_hbm, w3_hbm, w2_hbm, xs_hbm, ys_hbm,
                    xbuf, ybuf, w1f, w3f, w2f, w1b, w3b, w2b, xsem, ysem, wsem):
    total = total_ref[0]

    def w_copies(e, slot):
        return [pltpu.make_async_copy(src.at[e], dst.at[slot], wsem.at[slot])
                for src, dst in ((w1_hbm, w1f), (w3_hbm, w3f), (w2_hbm, w2f))]

    def rows(c):
        return (pl.multiple_of(first_ref[c] * ROW_TILE, ROW_TILE), pl.multiple_of(size_ref[c] * ROW_TILE, ROW_TILE))

    def x_copy(c, slot):
        first, size = rows(c)
        return pltpu.make_async_copy(xs_hbm.at[pl.ds(first, size)], xbuf.at[slot, pl.ds(0, size)], xsem.at[slot])

    def y_copy(c, slot):
        first, size = rows(c)
        return pltpu.make_async_copy(ybuf.at[slot, pl.ds(0, size)], ys_hbm.at[pl.ds(first, size)], ysem.at[slot])

    xbuf[...] = jnp.zeros(xbuf.shape, I32)
    for cp in w_copies(e_ref[0], 0):
        cp.start()
    x_copy(0, 0).start()
    half = MOE_BLOCK // 2

    def body(c, carry):
        slot = c % 2

        @pl.when(new_ref[c] == 1)
        def _():
            ws = wslot_ref[c]
            for cp in w_copies(e_ref[c], ws):
                cp.wait()
            w1b[...] = w1f[ws].astype(BF16)
            w3b[...] = w3f[ws].astype(BF16)
            w2b[...] = w2f[ws].astype(BF16)

            @pl.when(next_ref[c] >= 0)
            def _():
                for cp in w_copies(next_ref[c], 1 - ws):
                    cp.start()

        x_copy(c, slot).wait()

        @pl.when(c + 1 < total)
        def _():
            x_copy(c + 1, 1 - slot).start()

        x = jnp.concatenate(_unpack_pairs(_from_row_tiles(xbuf.at[slot], half)), axis=0).astype(BF16)
        h1 = _dot(x, w1b[...])
        h3 = _dot(x, w3b[...])
        hid = (h1 * _sigmoid(h1)) * h3
        y = _dot(hid.astype(BF16), w2b[...]).astype(BF16).astype(F32)

        @pl.when(c >= 2)
        def _():
            y_copy(c - 2, slot).wait()

        _to_row_tiles(ybuf.at[slot], _pack_pairs(y[0:half], y[half:]))
        y_copy(c, slot).start()
        return carry

    lax.fori_loop(0, total, body, 0)

    @pl.when(total >= 2)
    def _():
        y_copy(total - 2, total % 2).wait()

    y_copy(total - 1, (total - 1) % 2).wait()

    ybuf[0] = jnp.zeros(ybuf.shape[1:], I32)
    tail_first = total_ref[1]
    tail_pairs = ys_hbm.shape[0] // ROW_TILE - tail_first

    def fill(j, carry):
        size = pl.multiple_of(jnp.minimum(half, tail_pairs - j * half) * ROW_TILE, ROW_TILE)
        cp = pltpu.make_async_copy(ybuf.at[0, pl.ds(0, size)],
                                   ys_hbm.at[pl.ds(pl.multiple_of((tail_first + j * half) * ROW_TILE, ROW_TILE), size)],
                                   ysem.at[0])
        cp.start()
        cp.wait()
        return carry

    lax.fori_loop(0, (tail_pairs + half - 1) // half, fill, 0)


def _expert_chunks(counts, n_pairs_total):
    E = counts.shape[0]
    block = MOE_BLOCK // 2
    n_max = n_pairs_total // block + E
    starts = jnp.cumsum(counts) - counts
    n_chunks = (counts + block - 1) // block
    c_end = jnp.cumsum(n_chunks)
    c = jnp.arange(n_max, dtype=I32)
    e = jnp.minimum(jnp.sum(c_end[None, :] <= c[:, None], axis=1), E - 1).astype(I32)
    j = c - (c_end - n_chunks)[e]
    size = jnp.clip(counts[e] - j * block, 0, block)
    has_rows = counts > 0
    ids = jnp.where(has_rows, jnp.arange(E, dtype=I32), E)
    later = jnp.flip(lax.cummin(jnp.flip(ids)))
    nxt = jnp.concatenate([later[1:], jnp.full((1,), E, I32)])
    nxt = jnp.where(nxt >= E, -1, nxt)
    ordinal = jnp.cumsum(has_rows.astype(I32)) - 1
    to_i32 = lambda a: a.astype(I32)
    meta = jnp.stack([c_end[-1], jnp.sum(counts)])
    return tuple(map(to_i32, (starts[e] + j * block, size, e, j == 0, nxt[e], ordinal[e] % 2, meta)))


def _experts(chunks, xs, w1, w3, w2):
    rows_total, W = xs.shape
    _, D, De = w1.shape
    any_space = pl.BlockSpec(memory_space=pl.ANY)
    block_rows = MOE_BLOCK // 2 * ROW_TILE
    grid_spec = pltpu.PrefetchScalarGridSpec(
        num_scalar_prefetch=len(chunks),
        grid=(1,),
        in_specs=[any_space] * 4,
        out_specs=any_space,
        scratch_shapes=[pltpu.VMEM((2, block_rows, W), I32), pltpu.VMEM((2, block_rows, W), I32),
                        pltpu.VMEM((2, D, De), F32), pltpu.VMEM((2, D, De), F32), pltpu.VMEM((2, De, D), F32),
                        pltpu.VMEM((D, De), BF16), pltpu.VMEM((D, De), BF16), pltpu.VMEM((De, D), BF16),
                        pltpu.SemaphoreType.DMA((2,)), pltpu.SemaphoreType.DMA((2,)), pltpu.SemaphoreType.DMA((2,))],
    )
    return pl.pallas_call(
        _experts_kernel,
        grid_spec=grid_spec,
        out_shape=jax.ShapeDtypeStruct((rows_total, W), I32),
        compiler_params=_params("arbitrary"),
        name="experts",
    )(*chunks, w1, w3, w2, xs)


def _combine_kernel(src_ref, cnt_ref, dst_ref, ys_hbm, post_ref, wt_ref, h2_ref, x1_ref, mod_ref, ws1_ref,
                    ws3_ref, ws2_ref, fw_ref, o_ref, ybuf, sem):
    tm = h2_ref.shape[0]
    n_pairs = ybuf.shape[0] // ROW_TILE
    n_exp = cnt_ref.shape[2] - 1

    @pl.when(pl.program_id(0) == 0)
    def _():
        ybuf[...] = jnp.zeros(ybuf.shape, I32)

    _run_copies(src_ref, cnt_ref, dst_ref, n_exp,
                lambda src, dst: pltpu.make_async_copy(ys_hbm.at[dst], ybuf.at[src], sem.at[0]))

    h = h2_ref[...]
    a = _dot(h, ws1_ref[...])
    hid = (a * _sigmoid(a)) * _dot(h, ws3_ref[...])
    shared = _dot(hid.astype(BF16), ws2_ref[...])

    pair = 2 * lax.broadcasted_iota(I32, (n_pairs, tm), 0)
    w_cols, mine = [], []
    for odd in range(2):
        w_rows = jnp.zeros((n_pairs, tm), F32)
        for k in range(TOP_K):
            w_rows = jnp.where(pair + odd == post_ref[k:k + 1, :], wt_ref[k:k + 1, :], w_rows)
        w_cols.append(jnp.sum(w_rows, axis=1, keepdims=True))
        mine.append(jnp.where(w_rows != 0.0, 1.0, 0.0).astype(BF16))
    used = pl.multiple_of((src_ref[0, 0, n_exp]) * ROW_TILE, ROW_TILE)
    pltpu.make_async_copy(ys_hbm.at[pl.ds(0, used)], ybuf.at[pl.ds(0, used)], sem.at[0]).wait()
    y_even, y_odd = _unpack_pairs(_from_row_tiles(ybuf, n_pairs))
    sum_rows = lambda hot, y: lax.dot_general(hot, y, (((0,), (0,)), ((), ())), preferred_element_type=F32)
    routed = (sum_rows(mine[0], (y_even * w_cols[0]).astype(BF16))
              + sum_rows(mine[1], (y_odd * w_cols[1]).astype(BF16)))
    x2 = x1_ref[...] + mod_ref[0, 5:6, :] * (routed + shared)
    o_ref[...] = _rms(x2) * fw_ref[...]


def _combine(runs, ys, pos_t, wgt_t, h2, x1, mod, ws1, ws3, ws2, fw, seq, tm):
    N, D = h2.shape
    nt = seq // tm
    row = pl.BlockSpec((tm, D), lambda t: (t, 0))
    full = lambda a: pl.BlockSpec(a.shape, lambda t: (0,) * a.ndim)
    return pl.pallas_call(
        _combine_kernel,
        grid=(N // tm,),
        in_specs=_run_specs(runs) + [pl.BlockSpec(memory_space=pl.ANY),
                                     pl.BlockSpec((TOP_K, tm), lambda t: (0, t)),
                                     pl.BlockSpec((TOP_K, tm), lambda t: (0, t)),
                                     row, row,
                                     pl.BlockSpec((1, 6, D), lambda t: (t // nt, 0, 0)),
                                     full(ws1), full(ws3), full(ws2), full(fw)],
        out_specs=row,
        out_shape=jax.ShapeDtypeStruct((N, D), F32),
        scratch_shapes=[pltpu.VMEM((_tile_pairs(tm, runs[0].shape[2] - 1) * ROW_TILE, D // ROW_TILE), I32),
                        pltpu.SemaphoreType.DMA((1,))],
        compiler_params=_params("arbitrary"),
        name="combine",
    )(*runs, ys, pos_t, wgt_t, h2, x1, mod, ws1, ws3, ws2, fw)


def _t5_bucket(dist):
    n = jnp.maximum(dist, 0)
    max_exact = REL_BUCKETS // 2
    nf = jnp.maximum(n, 1).astype(F32)
    large = max_exact + (jnp.log(nf / max_exact) / math.log(REL_MAX_DIST / max_exact)
                         * (REL_BUCKETS - max_exact)).astype(I32)
    large = jnp.minimum(large, REL_BUCKETS - 1)
    return jnp.where(n < max_exact, n, large)


def _bias_kernel(tbl_ref, far_ref, bt_ref, bw_ref, t_ref, w_ref):
    g = pl.program_id(0)
    d = pl.program_id(1)

    def lookup(bkt, head):
        out = jnp.full(bkt.shape, tbl_ref[0, head], F32)
        for b in range(1, REL_BUCKETS):
            out = jnp.where(bkt == b, tbl_ref[b, head], out)
        return out - far_ref[head]

    j = lax.broadcasted_iota(I32, (QB, QB), 0)
    i = lax.broadcasted_iota(I32, (QB, QB), 1)
    keep = ((d != 0) | (i >= j)) & (d != TILE_MASKED) & ((d != TILE_WINDOW_EDGE) | (j > i))
    for hq in range(GQA):
        t_ref[0, 0, :, hq * QB:(hq + 1) * QB] = jnp.where(keep, lookup(bt_ref[0], GQA * g + hq), NEG)

    @pl.when(d == 0)
    def _():
        for hq in range(GQA):
            w_ref[0, :, hq * QB:(hq + 1) * QB] = lookup(bw_ref[...], GQA * g + hq)


def _bias_tables(rel_table):
    tbl = rel_table.astype(F32) * LOG2E
    far_hi = tbl[REL_BUCKETS - 1].astype(BF16)
    far_lo = (tbl[REL_BUCKETS - 1] - far_hi.astype(F32)).astype(BF16)
    far = far_hi.astype(F32) + far_lo.astype(F32)
    q_tail = jnp.zeros((N_HEADS, HEAD_DIM), F32).at[:, 0].set(far_hi.astype(F32)).at[:, 1].set(far_lo.astype(F32))
    i = jnp.arange(QB)
    tile_dist = np.array(list(range(N_BIAS_TILES)) + [0, WINDOW // QB])
    d_t = (QB * jnp.asarray(tile_dist)[:, None, None] + i[None, None, :] - i[None, :, None])
    d_w = i[None, :] - CMP_STRIDE * (jnp.arange(CMP_WIN)[:, None] - CMP_PAD) - (CMP_LEN - 1)
    n_tiles = len(tile_dist)
    t_tab, w_tab = pl.pallas_call(
        _bias_kernel,
        grid=(N_KV, n_tiles),
        in_specs=[pl.BlockSpec(memory_space=pltpu.SMEM),
                  pl.BlockSpec(memory_space=pltpu.SMEM),
                  pl.BlockSpec((1, QB, QB), lambda g, d: (d, 0, 0)),
                  pl.BlockSpec((CMP_WIN, QB), lambda g, d: (0, 0))],
        out_specs=[pl.BlockSpec((1, 1, QB, GQA * QB), lambda g, d: (g, d, 0, 0)),
                   pl.BlockSpec((1, CMP_WIN, GQA * QB), lambda g, d: (g, 0, 0))],
        out_shape=[jax.ShapeDtypeStruct((N_KV, n_tiles, QB, GQA * QB), F32),
                   jax.ShapeDtypeStruct((N_KV, CMP_WIN, GQA * QB), F32)],
        compiler_params=_params("parallel", "arbitrary"),
        name="bias",
    )(tbl, far, _t5_bucket(d_t), _t5_bucket(d_w))
    far_min = min(FAR_TILE_DIST * QB - QB + 1, CMP_STRIDE * (CMP_PAD + 1) - (CMP_LEN - 1))
    max_exact = REL_BUCKETS // 2
    assert math.log(far_min / max_exact) / math.log(REL_MAX_DIST / max_exact) * max_exact > max_exact - 0.75
    return t_tab, w_tab, q_tail


def _overlap_tables(nc, nb):
    c_start = CMP_STRIDE * np.arange(nc)[:, None]
    s_start = SLC_LEN * np.arange(nb)[None, :]
    ov = ((c_start < s_start + SLC_LEN) & (c_start + CMP_LEN > s_start)).astype(np.float32)
    ov[nc - 1:] = 0.0
    out = np.zeros((nc + CMP_WIN, nb), np.float32)
    out[CMP_PAD:CMP_PAD + nc] = ov
    return jnp.asarray(ov.T, dtype=BF16), jnp.asarray(out)


def _block_diag(w):
    nblk, bs, _ = w.shape
    eye = jnp.eye(nblk, dtype=w.dtype)
    return (eye[:, None, :, None] * w[:, :, None, :]).reshape(nblk * bs, nblk * bs)


def _layer(x, c8, w_ada, b_ada, norm1_w, w_in, cmp_pos, cmp_k_w1, cmp_k_w2, cmp_v_w1, cmp_v_w2, rel_table,
           conv_w, conv_b, lru_wa, lru_ba, lru_wx, lru_bx, lru_lambda, gnorm_attn_w, gnorm_rnn_w, w_out,
           norm2_w, w_router, b_router, w1, w3, w2, ws1, ws3, ws2, final_norm_w):
    B, S, D = x.shape
    N = B * S
    E = w_router.shape[1]
    d_attn = N_HEADS * HEAD_DIM
    d_rnn = D - d_attn
    assert S % (FAR_CHUNK * QB) == 0 and S // SLC_LEN <= MAX_SLC_BLOCKS

    mod = _ada(c8, w_ada, b_ada[None, :])[:B].reshape(B, 6, D)

    n_kv_cols = 6 * N_KV * HEAD_DIM
    n_gate = N_BRANCH * N_HEADS
    wq, wkv, wg, wxr, wgr = jnp.split(w_in, np.cumsum([d_attn, n_kv_cols, n_gate, d_rnn]).tolist(), axis=1)
    wg = jnp.pad(wg.reshape(D, N_KV, GQA * N_BRANCH), ((0, 0), (0, 0), (0, LANES - GQA * N_BRANCH)))
    wg = wg.reshape(D, N_KV * LANES)
    w_pad = jnp.concatenate([wq, wkv, wg, wxr, wgr], axis=1).astype(BF16)
    t_tab, w_tab, q_tail = _bias_tables(rel_table)
    q_t, kx, vx_t, kvc, gates_t, xr, gr = _inproj(x, mod, norm1_w[None, :], w_pad, q_tail)

    nc = S // CMP_STRIDE
    kvc16 = kvc.reshape(B, 4, nc, CMP_STRIDE * HEAD_DIM)
    pos2 = cmp_pos.reshape(2, CMP_STRIDE * HEAD_DIM)
    kvc_pad, kvc_t = _compress(kvc16, pos2, jnp.stack([cmp_k_w1, cmp_v_w1]), jnp.stack([cmp_k_w2, cmp_v_w2]))
    o_attn = _attn(q_t, kx, vx_t, kvc_pad, kvc_t, gates_t, t_tab, w_tab, *_overlap_tables(nc, MAX_SLC_BLOCKS))

    y_rnn = _rglru(xr, gr, conv_w.reshape(CONV_WIDTH, d_rnn), conv_b[None, :], _block_diag(lru_wa).astype(BF16),
                   lru_ba[None, :], _block_diag(lru_wx).astype(BF16), lru_bx[None, :], lru_lambda[None, :],
                   gnorm_rnn_w[None, :])

    x1, h2, s_t = _outproj(o_attn, y_rnn, x, mod, gnorm_attn_w[None, :], w_out.astype(BF16), norm2_w[None, :],
                           w_router.T)

    tile = min(MOE_TILE, S)
    wgt_t, pos_t, tile_counts = _route(s_t, b_router[:, None], tile)
    tcnt = tile_counts[:, :, 0].astype(I32)
    n_tiles = tcnt.shape[0]
    tile_pairs = _tile_pairs(tile, E)
    rpairs = (tcnt + 1) // 2
    rpairs = jnp.concatenate([rpairs, tile_pairs - jnp.sum(rpairs, axis=1, keepdims=True)], axis=1)
    per_expert = jnp.sum(rpairs, axis=0)
    off = jnp.cumsum(per_expert) - per_expert
    run_src = jnp.cumsum(rpairs, axis=1) - rpairs
    run_dst = off[None, :] + jnp.cumsum(rpairs, axis=0) - rpairs
    runs = tuple(r.astype(I32)[:, None, :] for r in (run_src, rpairs, run_dst))

    h2f = h2.reshape(N, D)
    xs = _dispatch(runs, pos_t, h2f, tile)
    ys = _experts(_expert_chunks(per_expert[:E], n_tiles * tile_pairs), xs, w1, w3, w2)
    out = _combine(runs, ys, pos_t, wgt_t, h2f, x1.reshape(N, D), mod, ws1.astype(BF16), ws3.astype(BF16),
                   ws2.astype(BF16), final_norm_w[None, :], S, tile)
    return out.reshape(B, S, D)


def kernel(x, c, w_ada, b_ada, norm1_w, w_in, cmp_pos, cmp_k_w1, cmp_k_w2, cmp_v_w1, cmp_v_w2, rel_table, conv_w, conv_b, lru_wa, lru_ba, lru_wx, lru_bx, lru_lambda, gnorm_attn_w, gnorm_rnn_w, w_out, norm2_w, w_router, b_router, w1, w3, w2, ws1, ws3, ws2, final_norm_w):
    assert w_ada.shape[0] == 1
    c8 = jnp.pad(c, ((0, 8 - c.shape[0]), (0, 0)))
    return _layer(x, c8, w_ada[0], b_ada[0], norm1_w[0], w_in[0], cmp_pos[0], cmp_k_w1[0], cmp_k_w2[0],
                  cmp_v_w1[0], cmp_v_w2[0], rel_table, conv_w[0], conv_b[0], lru_wa[0], lru_ba[0], lru_wx[0],
                  lru_bx[0], lru_lambda[0], gnorm_attn_w[0], gnorm_rnn_w[0], w_out[0], norm2_w[0], w_router[0],
                  b_router[0], w1[0], w3[0], w2[0], ws1[0], ws3[0], ws2[0], final_norm_w)
```

```python
import math

import jax
import jax.numpy as jnp
import numpy as np
from jax import lax
from jax.experimental import pallas as pl
from jax.experimental.pallas import tpu as pltpu

F32 = jnp.float32
BF16 = jnp.bfloat16
I32 = jnp.int32

HEAD_DIM = 64
N_HEADS = 8
N_KV = 2
GQA = N_HEADS // N_KV
N_BRANCH = 3
CONV_WIDTH = 4
LRU_C = 8.0
CMP_LEN = 32
CMP_STRIDE = 16
SLC_LEN = 64
SLC_TOP = 16
WINDOW = 512
LANES = 128
QB = 128
MAX_SLC_BLOCKS = 128
GATE_ROWS = 16
V_ROWS = 80
REL_BUCKETS = 32
REL_MAX_DIST = 1024
N_EXPERT_GROUPS = 8
TOP_GROUPS = 4
TOP_K = 8
ROUTED_SCALE = 2.5
MOE_BLOCK = 576
MOE_TILE = 256
ROW_TILE = 8
EPS = 1e-6
NEG = -1e30
LOG2E = math.log2(math.e)
CMP_PAD = 120
CMP_WIN = 128
N_BIAS_TILES = 11
TILE_MASKED = N_BIAS_TILES
TILE_WINDOW_EDGE = N_BIAS_TILES + 1
FAR_TILE_DIST = 8
FAR_CHUNK = 4
VMEM_LIMIT = 52 * 1024 * 1024


def _dot(a, b, **kw):
    return jnp.dot(a, b, preferred_element_type=F32, **kw)


def _dot_nt(a, b, **kw):
    return lax.dot_general(a, b, (((1,), (1,)), ((), ())), preferred_element_type=F32, **kw)


def _gelu(x):
    return 0.5 * x * (1.0 + jnp.tanh(math.sqrt(2.0 / math.pi) * (x + 0.044715 * (x * x * x))))


def _sigmoid(x):
    return 1.0 / (1.0 + jnp.exp(-x))


def _rms(x):
    return x * lax.rsqrt(jnp.mean(x * x, axis=-1, keepdims=True) + EPS)


def _params(*sem):
    return pltpu.CompilerParams(dimension_semantics=sem, vmem_limit_bytes=VMEM_LIMIT)


def _ada_kernel(c_ref, w_ref, b_ref, o_ref):
    c = c_ref[...]
    a = c * _sigmoid(c)
    o_ref[...] = _dot(a, w_ref[...], precision=lax.Precision.HIGHEST) + b_ref[...]


def _ada(c8, w, b):
    d, n = w.shape
    tn = 1536
    return pl.pallas_call(
        _ada_kernel,
        grid=(n // tn,),
        in_specs=[pl.BlockSpec((8, d), lambda j: (0, 0)),
                  pl.BlockSpec((d, tn), lambda j: (0, j)),
                  pl.BlockSpec((1, tn), lambda j: (0, j))],
        out_specs=pl.BlockSpec((8, tn), lambda j: (0, j)),
        out_shape=jax.ShapeDtypeStruct((8, n), F32),
        compiler_params=_params("parallel"),
        name="ada",
    )(c8, w, b)


def _inproj_kernel(x_ref, mod_ref, nw_ref, w_ref, qt_ref, q_ref, kx_ref, vx_ref, kvc_ref, g_ref, xr_ref, gr_ref):
    h = _rms(x_ref[0]) * nw_ref[...]
    h = h * (1.0 + mod_ref[0, 1:2, :]) + mod_ref[0, 0:1, :]
    p = _dot(h.astype(BF16), w_ref[...])
    tm = p.shape[0]
    dq = N_HEADS * HEAD_DIM
    lane = lax.broadcasted_iota(I32, (tm, HEAD_DIM), 1)
    k_tail = jnp.where(lane < 2, 1.0, 0.0)
    v_tail = jnp.where(lane < 1, 1.0, 0.0)
    for hh in range(N_HEADS):
        qh = p[:, hh * HEAD_DIM:(hh + 1) * HEAD_DIM] * (HEAD_DIM ** -0.5 * LOG2E)
        q_tail = jnp.broadcast_to(qt_ref[hh:hh + 1, :], (tm, HEAD_DIM))
        q_t = jnp.concatenate([qh, q_tail], axis=1).T.astype(BF16)
        for blk in range(tm // QB):
            col = (blk * GQA + hh % GQA) * QB
            q_ref[0, hh // GQA, :, col:col + QB] = q_t[:, blk * QB:(blk + 1) * QB]
    for j in range(6):
        for g in range(N_KV):
            col = dq + (j * N_KV + g) * HEAD_DIM
            piece = p[:, col:col + HEAD_DIM]
            if j < 2:
                kvc_ref[0, j * N_KV + g] = piece
            elif j % 2 == 0:
                kx_ref[0, (j // 2 - 1) * N_KV + g] = jnp.concatenate([piece, k_tail], axis=1).astype(BF16)
            else:
                v_t = jnp.concatenate([piece, v_tail], axis=1).T.astype(BF16)
                for blk in range(tm // QB):
                    vx_ref[0, (j // 2 - 1) * N_KV + g, blk] = v_t[0:V_ROWS, blk * QB:(blk + 1) * QB]
    c0 = dq + 6 * N_KV * HEAD_DIM
    for g in range(N_KV):
        g_ref[0, g] = _sigmoid(p[:, c0 + g * LANES:c0 + (g + 1) * LANES]).T[0:GATE_ROWS, :]
    c1 = c0 + N_KV * LANES
    d_rnn = xr_ref.shape[2]
    xr_ref[0] = p[:, c1:c1 + d_rnn]
    gr_ref[0] = p[:, c1 + d_rnn:c1 + 2 * d_rnn]


def _inproj(x, mod, nw, w_pad, q_tail, tm=1024):
    B, S, D = x.shape
    ncol = w_pad.shape[1]
    heads = lambda n, w: pl.BlockSpec((1, n, tm, w), lambda b, t: (b, 0, t, 0))
    return pl.pallas_call(
        _inproj_kernel,
        grid=(B, S // tm),
        in_specs=[pl.BlockSpec((1, tm, D), lambda b, t: (b, t, 0)),
                  pl.BlockSpec((1, 6, D), lambda b, t: (b, 0, 0)),
                  pl.BlockSpec((1, D), lambda b, t: (0, 0)),
                  pl.BlockSpec((D, ncol), lambda b, t: (0, 0)),
                  pl.BlockSpec(q_tail.shape, lambda b, t: (0, 0))],
        out_specs=[pl.BlockSpec((1, N_KV, 2 * HEAD_DIM, tm * GQA), lambda b, t: (b, 0, 0, t)),
                   heads(4, 2 * HEAD_DIM),
                   pl.BlockSpec((1, 4, tm // QB, V_ROWS, QB), lambda b, t: (b, 0, t, 0, 0)),
                   heads(4, HEAD_DIM),
                   pl.BlockSpec((1, N_KV, GATE_ROWS, tm), lambda b, t: (b, 0, 0, t)),
                   pl.BlockSpec((1, tm, 512), lambda b, t: (b, t, 0)),
                   pl.BlockSpec((1, tm, 512), lambda b, t: (b, t, 0))],
        out_shape=[jax.ShapeDtypeStruct((B, N_KV, 2 * HEAD_DIM, S * GQA), BF16),
                   jax.ShapeDtypeStruct((B, 4, S, 2 * HEAD_DIM), BF16),
                   jax.ShapeDtypeStruct((B, 4, S // QB, V_ROWS, QB), BF16),
                   jax.ShapeDtypeStruct((B, 4, S, HEAD_DIM), F32),
                   jax.ShapeDtypeStruct((B, N_KV, GATE_ROWS, S), F32),
                   jax.ShapeDtypeStruct((B, S, 512), F32),
                   jax.ShapeDtypeStruct((B, S, 512), F32)],
        compiler_params=_params("parallel", "parallel"),
        name="inproj",
    )(x, mod, nw, w_pad, q_tail)


def _compress_kernel(x_ref, pos_ref, w1_ref, w2_ref, o_ref, ot_ref):
    x = x_ref[0, 0]
    nc = x.shape[0]
    half = CMP_STRIDE * HEAD_DIM
    kv = pl.program_id(1) // N_KV
    a = _dot((x + pos_ref[0:1, :]).astype(BF16), w1_ref[kv, 0:half, :].astype(BF16))
    b = _dot((x + pos_ref[1:2, :]).astype(BF16), w1_ref[kv, half:2 * half, :].astype(BF16))
    hid = _gelu(a + pltpu.roll(b, nc - 1, 0))
    out = _dot(hid.astype(BF16), w2_ref[kv].astype(BF16))
    row = lax.broadcasted_iota(I32, out.shape, 0)
    out = jnp.where(row < nc - 1, out, 0.0)
    lane = lax.broadcasted_iota(I32, (nc, HEAD_DIM), 1)
    out = jnp.concatenate([out, jnp.where(lane < 2, 1.0, 0.0)], axis=1)
    o_ref[0, 0, 0:CMP_PAD, :] = jnp.zeros((CMP_PAD, 2 * HEAD_DIM), F32)
    o_ref[0, 0, CMP_PAD:CMP_PAD + nc, :] = out
    o_ref[0, 0, CMP_PAD + nc:, :] = jnp.zeros((CMP_WIN - CMP_PAD, 2 * HEAD_DIM), F32)
    ot_ref[0, 0] = out.T


def _compress(kvc, pos2, w1, w2):
    B, _, NC, W = kvc.shape
    return pl.pallas_call(
        _compress_kernel,
        grid=(B, 4),
        in_specs=[pl.BlockSpec((1, 1, NC, W), lambda b, i: (b, i, 0, 0)),
                  pl.BlockSpec((2, W), lambda b, i: (0, 0)),
                  pl.BlockSpec((2, 2 * W, HEAD_DIM), lambda b, i: (0, 0, 0)),
                  pl.BlockSpec((2, HEAD_DIM, HEAD_DIM), lambda b, i: (0, 0, 0))],
        out_specs=[pl.BlockSpec((1, 1, NC + CMP_WIN, 2 * HEAD_DIM), lambda b, i: (b, i, 0, 0)),
                   pl.BlockSpec((1, 1, 2 * HEAD_DIM, NC), lambda b, i: (b, i, 0, 0))],
        out_shape=[jax.ShapeDtypeStruct((B, 4, NC + CMP_WIN, 2 * HEAD_DIM), F32),
                   jax.ShapeDtypeStruct((B, 4, 2 * HEAD_DIM, NC), F32)],
        compiler_params=_params("parallel", "parallel"),
        name="compress",
    )(kvc, pos2, w1, w2)


def _attn_kernel(q_ref, ks_ref, vs_ref, kw_ref, vw_ref, kc_ref, vc_ref, vct_ref, g_ref, t_ref, w_ref, ovt_ref, ov_ref,
                 o_ref, m_scr, acc_scr, sel_scr, sa_scr, sb_scr, sc_scr):
    qb = pl.program_id(2)
    R = GQA * QB
    q = q_ref[0, 0]
    nc = vct_ref.shape[3]
    nb = ov_ref.shape[1]
    n_top = min(SLC_TOP, ks_ref.shape[2] // SLC_LEN)

    def heads_sum(x):
        out = x[:, 0:QB]
        for hq in range(1, GQA):
            out = out + x[:, hq * QB:(hq + 1) * QB]
        return out

    w0 = pl.multiple_of(qb * 8, 8)
    s_far = _dot(kc_ref[0, 0, CMP_PAD:CMP_PAD + nc, :].astype(BF16), q)
    n_io = lax.broadcasted_iota(I32, (nc, R), 0)
    s_far = jnp.where(n_io < qb * 8 - CMP_PAD, s_far, -jnp.inf)
    s_win = _dot(kc_ref[0, 0, pl.ds(w0, CMP_WIN), :].astype(BF16), q) + w_ref[0]
    j_w = lax.broadcasted_iota(I32, (CMP_WIN, R), 0)
    i_w = lax.broadcasted_iota(I32, (CMP_WIN, R), 1) & (QB - 1)
    dist_w = i_w - CMP_STRIDE * (j_w - CMP_PAD) - (CMP_LEN - 1)
    s_win = jnp.where(dist_w >= 0, jnp.where(j_w >= CMP_PAD - qb * 8, s_win, -jnp.inf), -jnp.inf)
    m = jnp.maximum(jnp.max(s_far, axis=0, keepdims=True), jnp.max(s_win, axis=0, keepdims=True))
    m = jnp.where(m == -jnp.inf, 0.0, m)
    e_far = jnp.exp2(s_far - m)
    e_win = jnp.exp2(s_win - m)
    l = jnp.sum(e_far, axis=0, keepdims=True) + jnp.sum(e_win, axis=0, keepdims=True)
    inv = 1.0 / jnp.maximum(l, 1e-30)
    p_far = e_far * inv
    p_win = e_win * inv
    vcw_t = vc_ref[0, 0, pl.ds(w0, CMP_WIN), :].T[0:V_ROWS, :]
    o_c = (_dot(vct_ref[0, 0, 0:V_ROWS, :].astype(BF16), p_far.astype(BF16))
           + _dot(vcw_t.astype(BF16), p_win.astype(BF16)))
    ov_win_t = ov_ref[pl.ds(w0, CMP_WIN), :].T
    sc_t = (_dot(ovt_ref[...], heads_sum(p_far).astype(BF16))
            + _dot(ov_win_t.astype(BF16), heads_sum(p_win).astype(BF16)))

    n_wt = WINDOW // QB + 1
    k_t, v_t, b_t = [], [], []
    for j in range(n_wt):
        kb = qb - (n_wt - 1) + j
        kb0 = jnp.maximum(kb, 0)
        k_t.append(kw_ref[0, 0, pl.ds(pl.multiple_of(kb0 * QB, QB), QB), :])
        v_t.append(vw_ref[0, 0, kb0])
        b_t.append(t_ref[0, jnp.where(kb < 0, TILE_MASKED, TILE_WINDOW_EDGE if j == 0 else n_wt - 1 - j)])
    s = _dot(jnp.concatenate(k_t, axis=0), q) + jnp.concatenate(b_t, axis=0)
    p = jnp.exp2(s - jnp.max(s, axis=0, keepdims=True))
    acc = _dot(jnp.concatenate(v_t, axis=1), p.astype(BF16))
    o_w = acc[0:HEAD_DIM, :] / acc[HEAD_DIM:HEAD_DIM + 1, :]

    chunk = FAR_CHUNK * QB

    def qk(c):
        return _dot(ks_ref[0, 0, pl.ds(pl.multiple_of(c * chunk, chunk), chunk), :], q)

    n_far = jnp.maximum(qb - (FAR_TILE_DIST - 1), 0) // FAR_CHUNK
    sa_scr[...] = qk(0)
    sc_scr[...] = qk(n_far)

    blk = lax.broadcasted_iota(I32, (nb, QB), 0)
    t_io = lax.broadcasted_iota(I32, (nb, QB), 1)
    cur = qb * (QB // SLC_LEN) + t_io // SLC_LEN
    val = jnp.where(blk <= cur, sc_t, -jnp.inf)
    for forced_blk in (0, cur, cur - 1):
        val = jnp.where(blk == forced_blk, jnp.inf, val)
    drop = jnp.full((nb, QB), NEG, F32)
    for _ in range(n_top):
        mx = jnp.max(val, axis=0, keepdims=True)
        first = jnp.min(jnp.where(val == mx, blk, nb), axis=0, keepdims=True)
        first = jnp.where(mx > -jnp.inf, first, nb)
        pick = blk == first
        drop = jnp.where(pick, 0.0, drop)
        val = jnp.where(pick, -jnp.inf, val)
    sel_scr[...] = drop

    m_scr[...] = jnp.full(m_scr.shape, NEG, F32)
    acc_scr[...] = jnp.zeros(acc_scr.shape, F32)
    blocks_per_chunk = FAR_CHUNK * QB // SLC_LEN

    def soft(c, s):
        rows = sel_scr[pl.ds(pl.multiple_of(c * blocks_per_chunk, blocks_per_chunk), blocks_per_chunk), :]
        drop = jnp.concatenate([jnp.broadcast_to(rows[r:r + 1, :], (SLC_LEN, QB)) for r in range(blocks_per_chunk)],
                               axis=0)
        s = s + jnp.concatenate([drop] * GQA, axis=1)
        m_old = m_scr[...]
        m_new = jnp.maximum(m_old, jnp.max(s, axis=0, keepdims=True))
        p = jnp.exp2(s - m_new)
        v_t = jnp.concatenate([vs_ref[0, 0, c * FAR_CHUNK + j] for j in range(FAR_CHUNK)], axis=1)
        acc_scr[...] = jnp.exp2(m_old - m_new) * acc_scr[...] + _dot(v_t, p.astype(BF16))
        m_scr[...] = m_new

    def near_bias(c):
        tiles = []
        for j in range(FAR_CHUNK):
            delta = qb - (c * FAR_CHUNK + j)
            tiles.append(t_ref[0, jnp.where(delta < 0, TILE_MASKED, delta)])
        return jnp.concatenate(tiles, axis=0)

    def chunk_pairs(first, count, bias, sa):
        last = first + jnp.maximum(count - 1, 0)

        def pair_body(j, carry):
            c = first + 2 * j
            sb_scr[...] = qk(c + 1)
            soft(c, sa[...] + bias(c) if bias else sa[...])
            sa[...] = qk(jnp.minimum(c + 2, last))
            soft(c + 1, sb_scr[...] + bias(c + 1) if bias else sb_scr[...])
            return carry

        lax.fori_loop(0, count // 2, pair_body, 0)

        @pl.when(count % 2 == 1)
        def _():
            soft(last, sa[...] + bias(last) if bias else sa[...])

    chunk_pairs(0, n_far, None, sa_scr)
    chunk_pairs(n_far, qb // FAR_CHUNK + 1 - n_far, near_bias, sc_scr)
    acc = acc_scr[...]
    o_s = acc[0:HEAD_DIM, :] / acc[HEAD_DIM:HEAD_DIM + 1, :]

    gates = g_ref[0, 0]
    outs = []
    for hq in range(GQA):
        cols = slice(hq * QB, (hq + 1) * QB)
        c = hq * N_BRANCH
        outs.append(gates[c:c + 1, :] * o_c[0:HEAD_DIM, cols] + gates[c + 1:c + 2, :] * o_s[:, cols]
                    + gates[c + 2:c + 3, :] * o_w[:, cols])
    o_ref[0] = jnp.concatenate(outs, axis=0).T


def _attn(q_t, kx, vx_t, kvc, kvc_t, gates_t, t_tab, w_tab, ov_t, ov):
    B, _, W, _ = q_t.shape
    S = kx.shape[2]
    ncp = kvc.shape[2]
    nb = ov.shape[1]
    keys = lambda j: pl.BlockSpec((1, 1, S, W), lambda b, g, t: (b, j * N_KV + g, 0, 0))
    vals = lambda j: pl.BlockSpec((1, 1, S // QB, V_ROWS, QB), lambda b, g, t: (b, j * N_KV + g, 0, 0, 0))
    cmp = lambda j: pl.BlockSpec((1, 1, ncp, W), lambda b, g, t: (b, j * N_KV + g, 0, 0))
    return pl.pallas_call(
        _attn_kernel,
        grid=(B, N_KV, S // QB),
        in_specs=[pl.BlockSpec((1, 1, W, GQA * QB), lambda b, g, t: (b, g, 0, t)),
                  keys(0), vals(0), keys(1), vals(1), cmp(0), cmp(1),
                  pl.BlockSpec((1, 1, W, ncp - CMP_WIN), lambda b, g, t: (b, N_KV + g, 0, 0)),
                  pl.BlockSpec((1, 1, GATE_ROWS, QB), lambda b, g, t: (b, g, 0, t)),
                  pl.BlockSpec((1,) + t_tab.shape[1:], lambda b, g, t: (g, 0, 0, 0)),
                  pl.BlockSpec((1, CMP_WIN, GQA * QB), lambda b, g, t: (g, 0, 0)),
                  pl.BlockSpec(ov_t.shape, lambda b, g, t: (0, 0)),
                  pl.BlockSpec(ov.shape, lambda b, g, t: (0, 0))],
        out_specs=pl.BlockSpec((1, QB, GQA * HEAD_DIM), lambda b, g, t: (b, t, g)),
        out_shape=jax.ShapeDtypeStruct((B, S, N_HEADS * HEAD_DIM), F32),
        scratch_shapes=[pltpu.VMEM((1, GQA * QB), F32), pltpu.VMEM((V_ROWS, GQA * QB), F32), pltpu.VMEM((nb, QB), F32),
                        pltpu.VMEM((FAR_CHUNK * QB, GQA * QB), F32), pltpu.VMEM((FAR_CHUNK * QB, GQA * QB), F32),
                        pltpu.VMEM((FAR_CHUNK * QB, GQA * QB), F32)],
        compiler_params=_params("parallel", "parallel", "arbitrary"),
        name="attn",
    )(q_t, kx, vx_t, kx, vx_t, kvc, kvc, kvc_t, gates_t, t_tab, w_tab, ov_t, ov)


def _rglru_kernel(xr_ref, gr_ref, cw_ref, cb_ref, wa_ref, ba_ref, wx_ref, bx_ref, lam_ref, gw_ref, o_ref,
                  xbuf, hprev, a_scr, u_scr, h_scr):
    ts = xr_ref.shape[1]
    C = xr_ref.shape[2]

    @pl.when(pl.program_id(1) == 0)
    def _():
        xbuf[0:8, :] = jnp.zeros((8, C), F32)
        hprev[...] = jnp.zeros(hprev.shape, F32)

    xbuf[8:8 + ts, :] = xr_ref[0]
    xc = cb_ref[...] + jnp.zeros((ts, C), F32)
    for j in range(CONV_WIDTH):
        xc = xc + cw_ref[j:j + 1, :] * xbuf[pl.ds(8 - (CONV_WIDTH - 1) + j, ts), :]
    xbuf[0:8, :] = xbuf[ts:ts + 8, :]

    xcb = xc.astype(BF16)
    r = _sigmoid(_dot(xcb, wa_ref[...]) + ba_ref[...])
    i = _sigmoid(_dot(xcb, wx_ref[...]) + bx_ref[...])
    z = -lam_ref[...]
    softplus = jnp.maximum(z, 0.0) + jnp.log(1.0 + jnp.exp(-jnp.abs(z)))
    log_a = -LRU_C * r * softplus
    a = jnp.exp(log_a)
    a_scr[...] = a
    u_scr[...] = jnp.sqrt(1.0 - a * a) * (i * xc)

    row = lax.broadcasted_iota(I32, (8, C), 0)

    def body(k, h):
        r0 = pl.multiple_of(k * 8, 8)
        a = a_scr[pl.ds(r0, 8), :]
        b = u_scr[pl.ds(r0, 8), :]
        for s in (1, 2, 4):
            keep = row >= s
            b = jnp.where(keep, a * pltpu.roll(b, s, 0) + b, b)
            a = jnp.where(keep, a * pltpu.roll(a, s, 0), a)
        hh = a * h + b
        h_scr[pl.ds(r0, 8), :] = hh
        return jnp.broadcast_to(hh[7:8, :], (8, C))

    hprev[...] = lax.fori_loop(0, ts // 8, body, hprev[...])
    out = h_scr[...] * _gelu(gr_ref[0])
    o_ref[0] = (_rms(out) * gw_ref[...]).astype(BF16)


def _rglru(xr, gr, cw, cb, wa_bd, ba, wx_bd, bx, lam, gw, ts=512):
    B, S, C = xr.shape
    ts = min(ts, S)
    vec = pl.BlockSpec((1, C), lambda b, t: (0, 0))
    mat = pl.BlockSpec((C, C), lambda b, t: (0, 0))
    seq = pl.BlockSpec((1, ts, C), lambda b, t: (b, t, 0))
    return pl.pallas_call(
        _rglru_kernel,
        grid=(B, S // ts),
        in_specs=[seq, seq, pl.BlockSpec((CONV_WIDTH, C), lambda b, t: (0, 0)), vec, mat, vec, mat, vec, vec, vec],
        out_specs=seq,
        out_shape=jax.ShapeDtypeStruct((B, S, C), BF16),
        scratch_shapes=[pltpu.VMEM((ts + 8, C), F32), pltpu.VMEM((8, C), F32), pltpu.VMEM((ts, C), F32),
                        pltpu.VMEM((ts, C), F32), pltpu.VMEM((ts, C), F32)],
        compiler_params=_params("parallel", "arbitrary"),
        name="rglru",
    )(xr, gr, cw, cb, wa_bd, ba, wx_bd, bx, lam, gw)


def _to_row_tiles(ref, x):
    rows = x.shape[0]
    for s in range(ROW_TILE):
        ref[pl.ds(s, rows, stride=ROW_TILE), :] = x[:, s * LANES:(s + 1) * LANES]


def _from_row_tiles(ref, rows, start=0):
    return jnp.concatenate([ref[pl.ds(start + s, rows, stride=ROW_TILE), :] for s in range(ROW_TILE)], axis=1)


def _outproj_kernel(oa_ref, yr_ref, x_ref, mod_ref, gaw_ref, wo_ref, n2_ref, wr_ref, x1_ref, h2_ref, st_ref):
    da = oa_ref.shape[2]
    ya = (_rms(oa_ref[0]) * gaw_ref[...]).astype(BF16)
    mix = _dot(ya, wo_ref[0:da, :]) + _dot(yr_ref[0], wo_ref[da:, :])
    x1 = x_ref[0] + mod_ref[0, 2:3, :] * mix
    x1_ref[0] = x1
    h2 = (_rms(x1) * n2_ref[...]) * (1.0 + mod_ref[0, 4:5, :]) + mod_ref[0, 3:4, :]
    h_hi = h2.astype(BF16)
    h2_ref[0] = h_hi
    h_lo = (h2 - h_hi.astype(F32)).astype(BF16)
    logits = _dot_nt(wr_ref[0], h_hi) + (_dot_nt(wr_ref[1], h_hi) + _dot_nt(wr_ref[0], h_lo))
    st_ref[...] = _sigmoid(logits)


def _outproj(oa, yr, x, mod, gaw, wo, n2, wr_t, tm=1024):
    B, S, D = x.shape
    E = wr_t.shape[1]
    nt = S // tm
    row = lambda w: pl.BlockSpec((1, tm, w), lambda b, t: (b, t, 0))
    return pl.pallas_call(
        _outproj_kernel,
        grid=(B, nt),
        in_specs=[row(oa.shape[2]), row(yr.shape[2]), row(D),
                  pl.BlockSpec((1, 6, D), lambda b, t: (b, 0, 0)),
                  pl.BlockSpec((1, oa.shape[2]), lambda b, t: (0, 0)),
                  pl.BlockSpec(wo.shape, lambda b, t: (0, 0)),
                  pl.BlockSpec((1, D), lambda b, t: (0, 0)),
                  pl.BlockSpec((2, E, D), lambda b, t: (0, 0, 0))],
        out_specs=[row(D), row(D), pl.BlockSpec((E, tm), lambda b, t: (0, b * nt + t))],
        out_shape=[jax.ShapeDtypeStruct((B, S, D), F32), jax.ShapeDtypeStruct((B, S, D), BF16),
                   jax.ShapeDtypeStruct((E, B * S), F32)],
        compiler_params=_params("parallel", "parallel"),
        name="outproj",
    )(oa, yr, x, mod, gaw, wo, n2, wr_t)


def _route_kernel(s_ref, b_ref, wgt_ref, pos_ref, tcnt_ref):
    E, tn = s_ref.shape
    per = E // N_EXPERT_GROUPS
    s = s_ref[...]
    s_sel = s + b_ref[...]
    eidx = lax.broadcasted_iota(I32, (E, tn), 0)

    grp = []
    for gi in range(N_EXPERT_GROUPS):
        xg = s_sel[gi * per:(gi + 1) * per, :]
        ig = lax.broadcasted_iota(I32, (per, tn), 0)
        m1 = jnp.max(xg, axis=0, keepdims=True)
        f1 = jnp.min(jnp.where(xg == m1, ig, per), axis=0, keepdims=True)
        m2 = jnp.max(jnp.where(ig == f1, -jnp.inf, xg), axis=0, keepdims=True)
        grp.append(m1 + m2)
    val = []
    for gi in range(N_EXPERT_GROUPS):
        rank = jnp.zeros((1, tn), I32)
        for gj in range(N_EXPERT_GROUPS):
            if gj == gi:
                continue
            ahead = (grp[gj] > grp[gi]) | ((grp[gj] == grp[gi]) & (gj < gi))
            rank = rank + ahead.astype(I32)
        val.append(jnp.where(rank < TOP_GROUPS, s_sel[gi * per:(gi + 1) * per, :], -jnp.inf))
    val = jnp.concatenate(val, axis=0)

    eids, tops = [], []
    hot = jnp.zeros((E, tn), F32)
    for _ in range(TOP_K):
        mx = jnp.max(val, axis=0, keepdims=True)
        first = jnp.min(jnp.where(val == mx, eidx, E), axis=0, keepdims=True)
        pick = eidx == first
        eids.append(first)
        tops.append(jnp.sum(jnp.where(pick, s, 0.0), axis=0, keepdims=True))
        hot = jnp.where(pick, 1.0, hot)
        val = jnp.where(pick, -jnp.inf, val)
    denom = tops[0]
    for t in tops[1:]:
        denom = denom + t

    ti = lax.broadcasted_iota(I32, (tn, tn), 0)
    tj = lax.broadcasted_iota(I32, (tn, tn), 1)
    upper = jnp.where(ti < tj, 1.0, 0.0).astype(BF16)
    before = _dot(hot.astype(BF16), upper)
    tile_count = jnp.sum(hot, axis=1, keepdims=True)
    padded = tile_count + (tile_count - 2.0 * jnp.floor(tile_count * 0.5))
    ei = lax.broadcasted_iota(I32, (E, E), 0)
    ej = lax.broadcasted_iota(I32, (E, E), 1)
    earlier_experts = jnp.where(ej < ei, 1.0, 0.0).astype(BF16)
    run_start = _dot(earlier_experts, jnp.broadcast_to(padded, (E, LANES)).astype(BF16))[:, 0:1]
    for k in range(TOP_K):
        wgt_ref[k:k + 1, :] = ROUTED_SCALE * tops[k] / denom
        pos_ref[k:k + 1, :] = jnp.sum(jnp.where(eidx == eids[k], before + run_start, 0.0), axis=0,
                                      keepdims=True).astype(I32)
    tcnt_ref[0] = tile_count


def _route(s_t, b_col, tn):
    E, N = s_t.shape
    out = pl.BlockSpec((TOP_K, tn), lambda t: (0, t))
    return pl.pallas_call(
        _route_kernel,
        grid=(N // tn,),
        in_specs=[pl.BlockSpec((E, tn), lambda t: (0, t)), pl.BlockSpec((E, 1), lambda t: (0, 0))],
        out_specs=[out, out, pl.BlockSpec((1, E, 1), lambda t: (t, 0, 0))],
        out_shape=[jax.ShapeDtypeStruct((TOP_K, N), F32), jax.ShapeDtypeStruct((TOP_K, N), I32),
                   jax.ShapeDtypeStruct((N // tn, E, 1), F32)],
        compiler_params=_params("parallel"),
        name="route",
    )(s_t, b_col)


_HI = -65536


def _pack_pairs(even_rows, odd_rows):
    hi = lax.bitcast_convert_type(even_rows, I32) & _HI
    lo = lax.shift_right_logical(lax.bitcast_convert_type(odd_rows, I32), 16)
    return hi | lo


def _unpack_pairs(words):
    even_rows = lax.bitcast_convert_type(words & _HI, F32)
    odd_rows = lax.bitcast_convert_type(lax.shift_left(words, 16), F32)
    return even_rows, odd_rows


def _run_copies(src_ref, cnt_ref, dst_ref, n_runs, copy):
    def go(e, carry):
        n = cnt_ref[0, 0, e]

        @pl.when(n > 0)
        def _():
            size = pl.multiple_of(n * ROW_TILE, ROW_TILE)
            copy(pl.ds(pl.multiple_of(src_ref[0, 0, e] * ROW_TILE, ROW_TILE), size),
                 pl.ds(pl.multiple_of(dst_ref[0, 0, e] * ROW_TILE, ROW_TILE), size)).start()
        return carry
    lax.fori_loop(0, n_runs, go, 0, unroll=8)


def _pair_onehots(pos_rows, n_pairs, first_pair, width):
    even = 2 * (lax.broadcasted_iota(I32, (n_pairs, width), 0) + first_pair)
    hot_e = jnp.zeros((n_pairs, width), F32)
    hot_o = jnp.zeros((n_pairs, width), F32)
    for k in range(TOP_K):
        hot_e = jnp.where(even == pos_rows[k:k + 1, :], 1.0, hot_e)
        hot_o = jnp.where(even + 1 == pos_rows[k:k + 1, :], 1.0, hot_o)
    return hot_e, hot_o


def _dispatch_kernel(src_ref, cnt_ref, dst_ref, pos_ref, h_ref, xs_hbm, xbuf, sem):
    n_pairs = xbuf.shape[0] // ROW_TILE
    tn = h_ref.shape[0]
    hb = h_ref[...]
    slab = n_pairs // 3
    for sl in range(3):
        hot_e, hot_o = _pair_onehots(pos_ref[...], slab, sl * slab, tn)
        words = _pack_pairs(_dot(hot_e.astype(BF16), hb), _dot(hot_o.astype(BF16), hb))
        for s in range(ROW_TILE):
            xbuf[pl.ds(sl * slab * ROW_TILE + s, slab, stride=ROW_TILE), :] = words[:, s * LANES:(s + 1) * LANES]
    _run_copies(src_ref, cnt_ref, dst_ref, cnt_ref.shape[2],
                lambda src, dst: pltpu.make_async_copy(xbuf.at[src], xs_hbm.at[dst], sem.at[0]))
    pltpu.make_async_copy(xbuf, xs_hbm.at[pl.ds(0, xbuf.shape[0])], sem.at[0]).wait()


def _run_specs(runs):
    return [pl.BlockSpec((1, 1, r.shape[2]), lambda t: (t, 0, 0), memory_space=pltpu.SMEM) for r in runs]


def _tile_pairs(tn, n_experts):
    return (TOP_K * tn + n_experts) // 2


def _dispatch(runs, pos_t, h2, tn):
    N, D = h2.shape
    W = D // ROW_TILE
    n_pairs = _tile_pairs(tn, runs[0].shape[2] - 1)
    assert n_pairs % 24 == 0
    return pl.pallas_call(
        _dispatch_kernel,
        grid=(N // tn,),
        in_specs=_run_specs(runs) + [pl.BlockSpec((TOP_K, tn), lambda t: (0, t)),
                                     pl.BlockSpec((tn, D), lambda t: (t, 0))],
        out_specs=pl.BlockSpec(memory_space=pl.ANY),
        out_shape=jax.ShapeDtypeStruct((N // tn * n_pairs * ROW_TILE, W), I32),
        scratch_shapes=[pltpu.VMEM((n_pairs * ROW_TILE, W), I32), pltpu.SemaphoreType.DMA((1,))],
        compiler_params=_params("arbitrary"),
        name="dispatch",
    )(*runs, pos_t, h2)


def _experts_kernel(first_ref, size_ref, e_ref, new_ref, next_ref, wslot_ref, total_ref,
                    w1_hbm, w3_hbm, w2_hbm, xs_hbm, ys_hbm,
                    xbuf, ybuf, w1f, w3f, w2f, w1b, w3b, w2b, xsem, ysem, wsem):
    total = total_ref[0]

    def w_copies(e, slot):
        return [pltpu.make_async_copy(src.at[e], dst.at[slot], wsem.at[slot])
                for src, dst in ((w1_hbm, w1f), (w3_hbm, w3f), (w2_hbm, w2f))]

    def rows(c):
        return (pl.multiple_of(first_ref[c] * ROW_TILE, ROW_TILE), pl.multiple_of(size_ref[c] * ROW_TILE, ROW_TILE))

    def x_copy(c, slot):
        first, size = rows(c)
        return pltpu.make_async_copy(xs_hbm.at[pl.ds(first, size)], xbuf.at[slot, pl.ds(0, size)], xsem.at[slot])

    def y_copy(c, slot):
        first, size = rows(c)
        return pltpu.make_async_copy(ybuf.at[slot, pl.ds(0, size)], ys_hbm.at[pl.ds(first, size)], ysem.at[slot])

    xbuf[...] = jnp.zeros(xbuf.shape, I32)
    for cp in w_copies(e_ref[0], 0):
        cp.start()
    x_copy(0, 0).start()
    half = MOE_BLOCK // 2

    def body(c, carry):
        slot = c % 2

        @pl.when(new_ref[c] == 1)
        def _():
            ws = wslot_ref[c]
            for cp in w_copies(e_ref[c], ws):
                cp.wait()
            w1b[...] = w1f[ws].astype(BF16)
            w3b[...] = w3f[ws].astype(BF16)
            w2b[...] = w2f[ws].astype(BF16)

            @pl.when(next_ref[c] >= 0)
            def _():
                for cp in w_copies(next_ref[c], 1 - ws):
                    cp.start()

        x_copy(c, slot).wait()

        @pl.when(c + 1 < total)
        def _():
            x_copy(c + 1, 1 - slot).start()

        x = jnp.concatenate(_unpack_pairs(_from_row_tiles(xbuf.at[slot], half)), axis=0).astype(BF16)
        h1 = _dot(x, w1b[...])
        h3 = _dot(x, w3b[...])
        hid = (h1 * _sigmoid(h1)) * h3
        y = _dot(hid.astype(BF16), w2b[...]).astype(BF16).astype(F32)

        @pl.when(c >= 2)
        def _():
            y_copy(c - 2, slot).wait()

        _to_row_tiles(ybuf.at[slot], _pack_pairs(y[0:half], y[half:]))
        y_copy(c, slot).start()
        return carry

    lax.fori_loop(0, total, body, 0)

    @pl.when(total >= 2)
    def _():
        y_copy(total - 2, total % 2).wait()

    y_copy(total - 1, (total - 1) % 2).wait()

    ybuf[0] = jnp.zeros(ybuf.shape[1:], I32)
    tail_first = total_ref[1]
    tail_pairs = ys_hbm.shape[0] // ROW_TILE - tail_first

    def fill(j, carry):
        size = pl.multiple_of(jnp.minimum(half, tail_pairs - j * half) * ROW_TILE, ROW_TILE)
        cp = pltpu.make_async_copy(ybuf.at[0, pl.ds(0, size)],
                                   ys_hbm.at[pl.ds(pl.multiple_of((tail_first + j * half) * ROW_TILE, ROW_TILE), size)],
                                   ysem.at[0])
        cp.start()
        cp.wait()
        return carry

    lax.fori_loop(0, (tail_pairs + half - 1) // half, fill, 0)


def _expert_chunks(counts, n_pairs_total):
    E = counts.shape[0]
    block = MOE_BLOCK // 2
    n_max = n_pairs_total // block + E
    starts = jnp.cumsum(counts) - counts
    n_chunks = (counts + block - 1) // block
    c_end = jnp.cumsum(n_chunks)
    c = jnp.arange(n_max, dtype=I32)
    e = jnp.minimum(jnp.sum(c_end[None, :] <= c[:, None], axis=1), E - 1).astype(I32)
    j = c - (c_end - n_chunks)[e]
    size = jnp.clip(counts[e] - j * block, 0, block)
    has_rows = counts > 0
    ids = jnp.where(has_rows, jnp.arange(E, dtype=I32), E)
    later = jnp.flip(lax.cummin(jnp.flip(ids)))
    nxt = jnp.concatenate([later[1:], jnp.full((1,), E, I32)])
    nxt = jnp.where(nxt >= E, -1, nxt)
    ordinal = jnp.cumsum(has_rows.astype(I32)) - 1
    to_i32 = lambda a: a.astype(I32)
    meta = jnp.stack([c_end[-1], jnp.sum(counts)])
    return tuple(map(to_i32, (starts[e] + j * block, size, e, j == 0, nxt[e], ordinal[e] % 2, meta)))


def _experts(chunks, xs, w1, w3, w2):
    rows_total, W = xs.shape
    _, D, De = w1.shape
    any_space = pl.BlockSpec(memory_space=pl.ANY)
    block_rows = MOE_BLOCK // 2 * ROW_TILE
    grid_spec = pltpu.PrefetchScalarGridSpec(
        num_scalar_prefetch=len(chunks),
        grid=(1,),
        in_specs=[any_space] * 4,
        out_specs=any_space,
        scratch_shapes=[pltpu.VMEM((2, block_rows, W), I32), pltpu.VMEM((2, block_rows, W), I32),
                        pltpu.VMEM((2, D, De), F32), pltpu.VMEM((2, D, De), F32), pltpu.VMEM((2, De, D), F32),
                        pltpu.VMEM((D, De), BF16), pltpu.VMEM((D, De), BF16), pltpu.VMEM((De, D), BF16),
                        pltpu.SemaphoreType.DMA((2,)), pltpu.SemaphoreType.DMA((2,)), pltpu.SemaphoreType.DMA((2,))],
    )
    return pl.pallas_call(
        _experts_kernel,
        grid_spec=grid_spec,
        out_shape=jax.ShapeDtypeStruct((rows_total, W), I32),
        compiler_params=_params("arbitrary"),
        name="experts",
    )(*chunks, w1, w3, w2, xs)


def _combine_kernel(src_ref, cnt_ref, dst_ref, ys_hbm, post_ref, wt_ref, h2_ref, x1_ref, mod_ref, ws1_ref,
                    ws3_ref, ws2_ref, fw_ref, o_ref, ybuf, sem):
    tm = h2_ref.shape[0]
    n_pairs = ybuf.shape[0] // ROW_TILE
    n_exp = cnt_ref.shape[2] - 1

    @pl.when(pl.program_id(0) == 0)
    def _():
        ybuf[...] = jnp.zeros(ybuf.shape, I32)

    _run_copies(src_ref, cnt_ref, dst_ref, n_exp,
                lambda src, dst: pltpu.make_async_copy(ys_hbm.at[dst], ybuf.at[src], sem.at[0]))

    h = h2_ref[...]
    a = _dot(h, ws1_ref[...])
    hid = (a * _sigmoid(a)) * _dot(h, ws3_ref[...])
    shared = _dot(hid.astype(BF16), ws2_ref[...])

    pair = 2 * lax.broadcasted_iota(I32, (n_pairs, tm), 0)
    w_cols, mine = [], []
    for odd in range(2):
        w_rows = jnp.zeros((n_pairs, tm), F32)
        for k in range(TOP_K):
            w_rows = jnp.where(pair + odd == post_ref[k:k + 1, :], wt_ref[k:k + 1, :], w_rows)
        w_cols.append(jnp.sum(w_rows, axis=1, keepdims=True))
        mine.append(jnp.where(w_rows != 0.0, 1.0, 0.0).astype(BF16))
    used = pl.multiple_of((src_ref[0, 0, n_exp]) * ROW_TILE, ROW_TILE)
    pltpu.make_async_copy(ys_hbm.at[pl.ds(0, used)], ybuf.at[pl.ds(0, used)], sem.at[0]).wait()
    y_even, y_odd = _unpack_pairs(_from_row_tiles(ybuf, n_pairs))
    sum_rows = lambda hot, y: lax.dot_general(hot, y, (((0,), (0,)), ((), ())), preferred_element_type=F32)
    routed = (sum_rows(mine[0], (y_even * w_cols[0]).astype(BF16))
              + sum_rows(mine[1], (y_odd * w_cols[1]).astype(BF16)))
    x2 = x1_ref[...] + mod_ref[0, 5:6, :] * (routed + shared)
    o_ref[...] = _rms(x2) * fw_ref[...]


def _combine(runs, ys, pos_t, wgt_t, h2, x1, mod, ws1, ws3, ws2, fw, seq, tm):
    N, D = h2.shape
    nt = seq // tm
    row = pl.BlockSpec((tm, D), lambda t: (t, 0))
    full = lambda a: pl.BlockSpec(a.shape, lambda t: (0,) * a.ndim)
    return pl.pallas_call(
        _combine_kernel,
        grid=(N // tm,),
        in_specs=_run_specs(runs) + [pl.BlockSpec(memory_space=pl.ANY),
                                     pl.BlockSpec((TOP_K, tm), lambda t: (0, t)),
                                     pl.BlockSpec((TOP_K, tm), lambda t: (0, t)),
                                     row, row,
                                     pl.BlockSpec((1, 6, D), lambda t: (t // nt, 0, 0)),
                                     full(ws1), full(ws3), full(ws2), full(fw)],
        out_specs=row,
        out_shape=jax.ShapeDtypeStruct((N, D), F32),
        scratch_shapes=[pltpu.VMEM((_tile_pairs(tm, runs[0].shape[2] - 1) * ROW_TILE, D // ROW_TILE), I32),
                        pltpu.SemaphoreType.DMA((1,))],
        compiler_params=_params("arbitrary"),
        name="combine",
    )(*runs, ys, pos_t, wgt_t, h2, x1, mod, ws1, ws3, ws2, fw)


def _t5_bucket(dist):
    n = jnp.maximum(dist, 0)
    max_exact = REL_BUCKETS // 2
    nf = jnp.maximum(n, 1).astype(F32)
    large = max_exact + (jnp.log(nf / max_exact) / math.log(REL_MAX_DIST / max_exact)
                         * (REL_BUCKETS - max_exact)).astype(I32)
    large = jnp.minimum(large, REL_BUCKETS - 1)
    return jnp.where(n < max_exact, n, large)


def _bias_kernel(tbl_ref, far_ref, bt_ref, bw_ref, t_ref, w_ref):
    g = pl.program_id(0)
    d = pl.program_id(1)

    def lookup(bkt, head):
        out = jnp.full(bkt.shape, tbl_ref[0, head], F32)
        for b in range(1, REL_BUCKETS):
            out = jnp.where(bkt == b, tbl_ref[b, head], out)
        return out - far_ref[head]

    j = lax.broadcasted_iota(I32, (QB, QB), 0)
    i = lax.broadcasted_iota(I32, (QB, QB), 1)
    keep = ((d != 0) | (i >= j)) & (d != TILE_MASKED) & ((d != TILE_WINDOW_EDGE) | (j > i))
    for hq in range(GQA):
        t_ref[0, 0, :, hq * QB:(hq + 1) * QB] = jnp.where(keep, lookup(bt_ref[0], GQA * g + hq), NEG)

    @pl.when(d == 0)
    def _():
        for hq in range(GQA):
            w_ref[0, :, hq * QB:(hq + 1) * QB] = lookup(bw_ref[...], GQA * g + hq)


def _bias_tables(rel_table):
    tbl = rel_table.astype(F32) * LOG2E
    far_hi = tbl[REL_BUCKETS - 1].astype(BF16)
    far_lo = (tbl[REL_BUCKETS - 1] - far_hi.astype(F32)).astype(BF16)
    far = far_hi.astype(F32) + far_lo.astype(F32)
    q_tail = jnp.zeros((N_HEADS, HEAD_DIM), F32).at[:, 0].set(far_hi.astype(F32)).at[:, 1].set(far_lo.astype(F32))
    i = jnp.arange(QB)
    tile_dist = np.array(list(range(N_BIAS_TILES)) + [0, WINDOW // QB])
    d_t = (QB * jnp.asarray(tile_dist)[:, None, None] + i[None, None, :] - i[None, :, None])
    d_w = i[None, :] - CMP_STRIDE * (jnp.arange(CMP_WIN)[:, None] - CMP_PAD) - (CMP_LEN - 1)
    n_tiles = len(tile_dist)
    t_tab, w_tab = pl.pallas_call(
        _bias_kernel,
        grid=(N_KV, n_tiles),
        in_specs=[pl.BlockSpec(memory_space=pltpu.SMEM),
                  pl.BlockSpec(memory_space=pltpu.SMEM),
                  pl.BlockSpec((1, QB, QB), lambda g, d: (d, 0, 0)),
                  pl.BlockSpec((CMP_WIN, QB), lambda g, d: (0, 0))],
        out_specs=[pl.BlockSpec((1, 1, QB, GQA * QB), lambda g, d: (g, d, 0, 0)),
                   pl.BlockSpec((1, CMP_WIN, GQA * QB), lambda g, d: (g, 0, 0))],
        out_shape=[jax.ShapeDtypeStruct((N_KV, n_tiles, QB, GQA * QB), F32),
                   jax.ShapeDtypeStruct((N_KV, CMP_WIN, GQA * QB), F32)],
        compiler_params=_params("parallel", "arbitrary"),
        name="bias",
    )(tbl, far, _t5_bucket(d_t), _t5_bucket(d_w))
    far_min = min(FAR_TILE_DIST * QB - QB + 1, CMP_STRIDE * (CMP_PAD + 1) - (CMP_LEN - 1))
    max_exact = REL_BUCKETS // 2
    assert math.log(far_min / max_exact) / math.log(REL_MAX_DIST / max_exact) * max_exact > max_exact - 0.75
    return t_tab, w_tab, q_tail


def _overlap_tables(nc, nb):
    c_start = CMP_STRIDE * np.arange(nc)[:, None]
    s_start = SLC_LEN * np.arange(nb)[None, :]
    ov = ((c_start < s_start + SLC_LEN) & (c_start + CMP_LEN > s_start)).astype(np.float32)
    ov[nc - 1:] = 0.0
    out = np.zeros((nc + CMP_WIN, nb), np.float32)
    out[CMP_PAD:CMP_PAD + nc] = ov
    return jnp.asarray(ov.T, dtype=BF16), jnp.asarray(out)


def _block_diag(w):
    nblk, bs, _ = w.shape
    eye = jnp.eye(nblk, dtype=w.dtype)
    return (eye[:, None, :, None] * w[:, :, None, :]).reshape(nblk * bs, nblk * bs)


def _layer(x, c8, w_ada, b_ada, norm1_w, w_in, cmp_pos, cmp_k_w1, cmp_k_w2, cmp_v_w1, cmp_v_w2, rel_table,
           conv_w, conv_b, lru_wa, lru_ba, lru_wx, lru_bx, lru_lambda, gnorm_attn_w, gnorm_rnn_w, w_out,
           norm2_w, w_router, b_router, w1, w3, w2, ws1, ws3, ws2, final_norm_w):
    B, S, D = x.shape
    N = B * S
    E = w_router.shape[1]
    d_attn = N_HEADS * HEAD_DIM
    d_rnn = D - d_attn
    assert S % (FAR_CHUNK * QB) == 0 and S // SLC_LEN <= MAX_SLC_BLOCKS

    mod = _ada(c8, w_ada, b_ada[None, :])[:B].reshape(B, 6, D)

    n_kv_cols = 6 * N_KV * HEAD_DIM
    n_gate = N_BRANCH * N_HEADS
    wq, wkv, wg, wxr, wgr = jnp.split(w_in, np.cumsum([d_attn, n_kv_cols, n_gate, d_rnn]).tolist(), axis=1)
    wg = jnp.pad(wg.reshape(D, N_KV, GQA * N_BRANCH), ((0, 0), (0, 0), (0, LANES - GQA * N_BRANCH)))
    wg = wg.reshape(D, N_KV * LANES)
    w_pad = jnp.concatenate([wq, wkv, wg, wxr, wgr], axis=1).astype(BF16)
    t_tab, w_tab, q_tail = _bias_tables(rel_table)
    q_t, kx, vx_t, kvc, gates_t, xr, gr = _inproj(x, mod, norm1_w[None, :], w_pad, q_tail)

    nc = S // CMP_STRIDE
    kvc16 = kvc.reshape(B, 4, nc, CMP_STRIDE * HEAD_DIM)
    pos2 = cmp_pos.reshape(2, CMP_STRIDE * HEAD_DIM)
    kvc_pad, kvc_t = _compress(kvc16, pos2, jnp.stack([cmp_k_w1, cmp_v_w1]), jnp.stack([cmp_k_w2, cmp_v_w2]))
    o_attn = _attn(q_t, kx, vx_t, kvc_pad, kvc_t, gates_t, t_tab, w_tab, *_overlap_tables(nc, MAX_SLC_BLOCKS))

    y_rnn = _rglru(xr, gr, conv_w.reshape(CONV_WIDTH, d_rnn), conv_b[None, :], _block_diag(lru_wa).astype(BF16),
                   lru_ba[None, :], _block_diag(lru_wx).astype(BF16), lru_bx[None, :], lru_lambda[None, :],
                   gnorm_rnn_w[None, :])

    wr_hi = w_router.T.astype(BF16)
    wr_lo = (w_router.T - wr_hi.astype(F32)).astype(BF16)
    x1, h2, s_t = _outproj(o_attn, y_rnn, x, mod, gnorm_attn_w[None, :], w_out.astype(BF16), norm2_w[None, :],
                           jnp.stack([wr_hi, wr_lo]))

    tile = min(MOE_TILE, S)
    wgt_t, pos_t, tile_counts = _route(s_t, b_router[:, None], tile)
    tcnt = tile_counts[:, :, 0].astype(I32)
    n_tiles = tcnt.shape[0]
    tile_pairs = _tile_pairs(tile, E)
    rpairs = (tcnt + 1) // 2
    rpairs = jnp.concatenate([rpairs, tile_pairs - jnp.sum(rpairs, axis=1, keepdims=True)], axis=1)
    per_expert = jnp.sum(rpairs, axis=0)
    off = jnp.cumsum(per_expert) - per_expert
    run_src = jnp.cumsum(rpairs, axis=1) - rpairs
    run_dst = off[None, :] + jnp.cumsum(rpairs, axis=0) - rpairs
    runs = tuple(r.astype(I32)[:, None, :] for r in (run_src, rpairs, run_dst))

    h2f = h2.reshape(N, D)
    xs = _dispatch(runs, pos_t, h2f, tile)
    ys = _experts(_expert_chunks(per_expert[:E], n_tiles * tile_pairs), xs, w1, w3, w2)
    out = _combine(runs, ys, pos_t, wgt_t, h2f, x1.reshape(N, D), mod, ws1.astype(BF16), ws3.astype(BF16),
                   ws2.astype(BF16), final_norm_w[None, :], S, tile)
    return out.reshape(B, S, D)


def kernel(x, c, w_ada, b_ada, norm1_w, w_in, cmp_pos, cmp_k_w1, cmp_k_w2, cmp_v_w1, cmp_v_w2, rel_table, conv_w, conv_b, lru_wa, lru_ba, lru_wx, lru_bx, lru_lambda, gnorm_attn_w, gnorm_rnn_w, w_out, norm2_w, w_router, b_router, w1, w3, w2, ws1, ws3, ws2, final_norm_w):
    assert w_ada.shape[0] == 1
    c8 = jnp.pad(c, ((0, 8 - c.shape[0]), (0, 0)))
    return _layer(x, c8, w_ada[0], b_ada[0], norm1_w[0], w_in[0], cmp_pos[0], cmp_k_w1[0], cmp_k_w2[0],
                  cmp_v_w1[0], cmp_v_w2[0], rel_table, conv_w[0], conv_b[0], lru_wa[0], lru_ba[0], lru_wx[0],
                  lru_bx[0], lru_lambda[0], gnorm_attn_w[0], gnorm_rnn_w[0], w_out[0], norm2_w[0], w_router[0],
                  b_router[0], w1[0], w3[0], w2[0], ws1[0], ws3[0], ws2[0], final_norm_w)
```

```python
import math

import jax
import jax.numpy as jnp
import numpy as np
from jax import lax
from jax.experimental import pallas as pl
from jax.experimental.pallas import tpu as pltpu

F32 = jnp.float32
BF16 = jnp.bfloat16
I32 = jnp.int32

HEAD_DIM = 64
N_HEADS = 8
N_KV = 2
GQA = N_HEADS // N_KV
N_BRANCH = 3
CONV_WIDTH = 4
LRU_C = 8.0
CMP_LEN = 32
CMP_STRIDE = 16
SLC_LEN = 64
SLC_TOP = 16
WINDOW = 512
LANES = 128
QB = 128
MAX_SLC_BLOCKS = 128
GATE_ROWS = 16
V_ROWS = 80
REL_BUCKETS = 32
REL_MAX_DIST = 1024
N_EXPERT_GROUPS = 8
TOP_GROUPS = 4
TOP_K = 8
ROUTED_SCALE = 2.5
MOE_BLOCK = 576
MOE_TILE = 256
ROW_TILE = 8
EPS = 1e-6
NEG = -1e30
LOG2E = math.log2(math.e)
CMP_PAD = 120
CMP_WIN = 128
N_BIAS_TILES = 11
TILE_MASKED = N_BIAS_TILES
TILE_WINDOW_EDGE = N_BIAS_TILES + 1
FAR_TILE_DIST = 8
FAR_CHUNK = 4
VMEM_LIMIT = 52 * 1024 * 1024


def _dot(a, b, **kw):
    return jnp.dot(a, b, preferred_element_type=F32, **kw)


def _dot_nt(a, b, **kw):
    return lax.dot_general(a, b, (((1,), (1,)), ((), ())), preferred_element_type=F32, **kw)


def _gelu(x):
    return 0.5 * x * (1.0 + jnp.tanh(math.sqrt(2.0 / math.pi) * (x + 0.044715 * (x * x * x))))


def _sigmoid(x):
    return 1.0 / (1.0 + jnp.exp(-x))


def _rms(x):
    return x * lax.rsqrt(jnp.mean(x * x, axis=-1, keepdims=True) + EPS)


def _params(*sem):
    return pltpu.CompilerParams(dimension_semantics=sem, vmem_limit_bytes=VMEM_LIMIT)


def _ada_kernel(c_ref, w_ref, b_ref, o_ref):
    c = c_ref[...]
    a = c * _sigmoid(c)
    o_ref[...] = _dot(a, w_ref[...], precision=lax.Precision.HIGHEST) + b_ref[...]


def _ada(c8, w, b):
    d, n = w.shape
    tn = 1536
    return pl.pallas_call(
        _ada_kernel,
        grid=(n // tn,),
        in_specs=[pl.BlockSpec((8, d), lambda j: (0, 0)),
                  pl.BlockSpec((d, tn), lambda j: (0, j)),
                  pl.BlockSpec((1, tn), lambda j: (0, j))],
        out_specs=pl.BlockSpec((8, tn), lambda j: (0, j)),
        out_shape=jax.ShapeDtypeStruct((8, n), F32),
        compiler_params=_params("parallel"),
        name="ada",
    )(c8, w, b)


def _inproj_kernel(x_ref, mod_ref, nw_ref, w_ref, qt_ref, q_ref, kx_ref, vx_ref, kvc_ref, g_ref, xr_ref, gr_ref):
    h = _rms(x_ref[0]) * nw_ref[...]
    h = h * (1.0 + mod_ref[0, 1:2, :]) + mod_ref[0, 0:1, :]
    p = _dot(h.astype(BF16), w_ref[...])
    tm = p.shape[0]
    dq = N_HEADS * HEAD_DIM
    lane = lax.broadcasted_iota(I32, (tm, HEAD_DIM), 1)
    k_tail = jnp.where(lane < 2, 1.0, 0.0)
    v_tail = jnp.where(lane < 1, 1.0, 0.0)
    for hh in range(N_HEADS):
        qh = p[:, hh * HEAD_DIM:(hh + 1) * HEAD_DIM] * (HEAD_DIM ** -0.5 * LOG2E)
        q_tail = jnp.broadcast_to(qt_ref[hh:hh + 1, :], (tm, HEAD_DIM))
        q_t = jnp.concatenate([qh, q_tail], axis=1).T.astype(BF16)
        for blk in range(tm // QB):
            col = (blk * GQA + hh % GQA) * QB
            q_ref[0, hh // GQA, :, col:col + QB] = q_t[:, blk * QB:(blk + 1) * QB]
    for j in range(6):
        for g in range(N_KV):
            col = dq + (j * N_KV + g) * HEAD_DIM
            piece = p[:, col:col + HEAD_DIM]
            if j < 2:
                kvc_ref[0, j * N_KV + g] = piece
            elif j % 2 == 0:
                kx_ref[0, (j // 2 - 1) * N_KV + g] = jnp.concatenate([piece, k_tail], axis=1).astype(BF16)
            else:
                v_t = jnp.concatenate([piece, v_tail], axis=1).T.astype(BF16)
                for blk in range(tm // QB):
                    vx_ref[0, (j // 2 - 1) * N_KV + g, blk] = v_t[0:V_ROWS, blk * QB:(blk + 1) * QB]
    c0 = dq + 6 * N_KV * HEAD_DIM
    for g in range(N_KV):
        g_ref[0, g] = _sigmoid(p[:, c0 + g * LANES:c0 + (g + 1) * LANES]).T[0:GATE_ROWS, :]
    c1 = c0 + N_KV * LANES
    d_rnn = xr_ref.shape[2]
    xr_ref[0] = p[:, c1:c1 + d_rnn]
    gr_ref[0] = p[:, c1 + d_rnn:c1 + 2 * d_rnn]


def _inproj(x, mod, nw, w_pad, q_tail, tm=1024):
    B, S, D = x.shape
    ncol = w_pad.shape[1]
    heads = lambda n, w: pl.BlockSpec((1, n, tm, w), lambda b, t: (b, 0, t, 0))
    return pl.pallas_call(
        _inproj_kernel,
        grid=(B, S // tm),
        in_specs=[pl.BlockSpec((1, tm, D), lambda b, t: (b, t, 0)),
                  pl.BlockSpec((1, 6, D), lambda b, t: (b, 0, 0)),
                  pl.BlockSpec((1, D), lambda b, t: (0, 0)),
                  pl.BlockSpec((D, ncol), lambda b, t: (0, 0)),
                  pl.BlockSpec(q_tail.shape, lambda b, t: (0, 0))],
        out_specs=[pl.BlockSpec((1, N_KV, 2 * HEAD_DIM, tm * GQA), lambda b, t: (b, 0, 0, t)),
                   heads(4, 2 * HEAD_DIM),
                   pl.BlockSpec((1, 4, tm // QB, V_ROWS, QB), lambda b, t: (b, 0, t, 0, 0)),
                   heads(4, HEAD_DIM),
                   pl.BlockSpec((1, N_KV, GATE_ROWS, tm), lambda b, t: (b, 0, 0, t)),
                   pl.BlockSpec((1, tm, 512), lambda b, t: (b, t, 0)),
                   pl.BlockSpec((1, tm, 512), lambda b, t: (b, t, 0))],
        out_shape=[jax.ShapeDtypeStruct((B, N_KV, 2 * HEAD_DIM, S * GQA), BF16),
                   jax.ShapeDtypeStruct((B, 4, S, 2 * HEAD_DIM), BF16),
                   jax.ShapeDtypeStruct((B, 4, S // QB, V_ROWS, QB), BF16),
                   jax.ShapeDtypeStruct((B, 4, S, HEAD_DIM), F32),
                   jax.ShapeDtypeStruct((B, N_KV, GATE_ROWS, S), F32),
                   jax.ShapeDtypeStruct((B, S, 512), F32),
                   jax.ShapeDtypeStruct((B, S, 512), F32)],
        compiler_params=_params("parallel", "parallel"),
        name="inproj",
    )(x, mod, nw, w_pad, q_tail)


def _compress_kernel(x_ref, pos_ref, w1_ref, w2_ref, o_ref, ot_ref):
    nc = x_ref.shape[2] // CMP_STRIDE
    half = CMP_STRIDE * HEAD_DIM
    kv = pl.program_id(1) // N_KV
    a = jnp.zeros((nc, HEAD_DIM), F32)
    b = jnp.zeros((nc, HEAD_DIM), F32)
    for l in range(CMP_STRIDE):
        x_l = x_ref[0, 0, pl.ds(l, nc, stride=CMP_STRIDE), :]
        cols = slice(l * HEAD_DIM, (l + 1) * HEAD_DIM)
        a = a + _dot((x_l + pos_ref[0:1, cols]).astype(BF16), w1_ref[kv, cols, :].astype(BF16))
        b = b + _dot((x_l + pos_ref[1:2, cols]).astype(BF16),
                     w1_ref[kv, half + l * HEAD_DIM:half + (l + 1) * HEAD_DIM, :].astype(BF16))
    hid = _gelu(a + pltpu.roll(b, nc - 1, 0))
    out = _dot(hid.astype(BF16), w2_ref[kv].astype(BF16))
    row = lax.broadcasted_iota(I32, out.shape, 0)
    out = jnp.where(row < nc - 1, out, 0.0)
    lane = lax.broadcasted_iota(I32, (nc, HEAD_DIM), 1)
    out = jnp.concatenate([out, jnp.where(lane < 2, 1.0, 0.0)], axis=1)
    o_ref[0, 0, 0:CMP_PAD, :] = jnp.zeros((CMP_PAD, 2 * HEAD_DIM), F32)
    o_ref[0, 0, CMP_PAD:CMP_PAD + nc, :] = out
    o_ref[0, 0, CMP_PAD + nc:, :] = jnp.zeros((CMP_WIN - CMP_PAD, 2 * HEAD_DIM), F32)
    ot_ref[0, 0] = out.T


def _compress(kvc, pos2, w1, w2):
    B, _, S, _ = kvc.shape
    NC = S // CMP_STRIDE
    W = CMP_STRIDE * HEAD_DIM
    return pl.pallas_call(
        _compress_kernel,
        grid=(B, 4),
        in_specs=[pl.BlockSpec((1, 1, S, HEAD_DIM), lambda b, i: (b, i, 0, 0)),
                  pl.BlockSpec((2, W), lambda b, i: (0, 0)),
                  pl.BlockSpec((2, 2 * W, HEAD_DIM), lambda b, i: (0, 0, 0)),
                  pl.BlockSpec((2, HEAD_DIM, HEAD_DIM), lambda b, i: (0, 0, 0))],
        out_specs=[pl.BlockSpec((1, 1, NC + CMP_WIN, 2 * HEAD_DIM), lambda b, i: (b, i, 0, 0)),
                   pl.BlockSpec((1, 1, 2 * HEAD_DIM, NC), lambda b, i: (b, i, 0, 0))],
        out_shape=[jax.ShapeDtypeStruct((B, 4, NC + CMP_WIN, 2 * HEAD_DIM), F32),
                   jax.ShapeDtypeStruct((B, 4, 2 * HEAD_DIM, NC), F32)],
        compiler_params=_params("parallel", "parallel"),
        name="compress",
    )(kvc, pos2, w1, w2)


def _attn_kernel(q_ref, ks_ref, vs_ref, kw_ref, vw_ref, kc_ref, vc_ref, vct_ref, g_ref, t_ref, w_ref, ovt_ref, ov_ref,
                 o_ref, m_scr, acc_scr, sel_scr, sa_scr, sb_scr, sc_scr):
    qb = pl.program_id(2)
    R = GQA * QB
    q = q_ref[0, 0]
    nc = vct_ref.shape[3]
    nb = ov_ref.shape[1]
    n_top = min(SLC_TOP, ks_ref.shape[2] // SLC_LEN)

    def heads_sum(x):
        out = x[:, 0:QB]
        for hq in range(1, GQA):
            out = out + x[:, hq * QB:(hq + 1) * QB]
        return out

    w0 = pl.multiple_of(qb * 8, 8)
    s_far = _dot(kc_ref[0, 0, CMP_PAD:CMP_PAD + nc, :].astype(BF16), q)
    n_io = lax.broadcasted_iota(I32, (nc, R), 0)
    s_far = jnp.where(n_io < qb * 8 - CMP_PAD, s_far, -jnp.inf)
    s_win = _dot(kc_ref[0, 0, pl.ds(w0, CMP_WIN), :].astype(BF16), q) + w_ref[0]
    j_w = lax.broadcasted_iota(I32, (CMP_WIN, R), 0)
    i_w = lax.broadcasted_iota(I32, (CMP_WIN, R), 1) & (QB - 1)
    dist_w = i_w - CMP_STRIDE * (j_w - CMP_PAD) - (CMP_LEN - 1)
    s_win = jnp.where(dist_w >= 0, jnp.where(j_w >= CMP_PAD - qb * 8, s_win, -jnp.inf), -jnp.inf)
    m = jnp.maximum(jnp.max(s_far, axis=0, keepdims=True), jnp.max(s_win, axis=0, keepdims=True))
    m = jnp.where(m == -jnp.inf, 0.0, m)
    e_far = jnp.exp2(s_far - m)
    e_win = jnp.exp2(s_win - m)
    l = jnp.sum(e_far, axis=0, keepdims=True) + jnp.sum(e_win, axis=0, keepdims=True)
    inv = 1.0 / jnp.maximum(l, 1e-30)
    p_far = e_far * inv
    p_win = e_win * inv
    vcw_t = vc_ref[0, 0, pl.ds(w0, CMP_WIN), :].T[0:V_ROWS, :]
    o_c = (_dot(vct_ref[0, 0, 0:V_ROWS, :].astype(BF16), p_far.astype(BF16))
           + _dot(vcw_t.astype(BF16), p_win.astype(BF16)))
    ov_win_t = ov_ref[pl.ds(w0, CMP_WIN), :].T
    sc_t = (_dot(ovt_ref[...], heads_sum(p_far).astype(BF16))
            + _dot(ov_win_t.astype(BF16), heads_sum(p_win).astype(BF16)))

    n_wt = WINDOW // QB + 1
    k_t, v_t, b_t = [], [], []
    for j in range(n_wt):
        kb = qb - (n_wt - 1) + j
        kb0 = jnp.maximum(kb, 0)
        k_t.append(kw_ref[0, 0, pl.ds(pl.multiple_of(kb0 * QB, QB), QB), :])
        v_t.append(vw_ref[0, 0, kb0])
        b_t.append(t_ref[0, jnp.where(kb < 0, TILE_MASKED, TILE_WINDOW_EDGE if j == 0 else n_wt - 1 - j)])
    s = _dot(jnp.concatenate(k_t, axis=0), q) + jnp.concatenate(b_t, axis=0)
    p = jnp.exp2(s - jnp.max(s, axis=0, keepdims=True))
    acc = _dot(jnp.concatenate(v_t, axis=1), p.astype(BF16))
    o_w = acc[0:HEAD_DIM, :] / acc[HEAD_DIM:HEAD_DIM + 1, :]

    chunk = FAR_CHUNK * QB

    def qk(c):
        return _dot(ks_ref[0, 0, pl.ds(pl.multiple_of(c * chunk, chunk), chunk), :], q)

    n_far = jnp.maximum(qb - (FAR_TILE_DIST - 1), 0) // FAR_CHUNK
    sa_scr[...] = qk(0)
    sc_scr[...] = qk(n_far)

    blk = lax.broadcasted_iota(I32, (nb, QB), 0)
    t_io = lax.broadcasted_iota(I32, (nb, QB), 1)
    cur = qb * (QB // SLC_LEN) + t_io // SLC_LEN
    val = jnp.where(blk <= cur, sc_t, -jnp.inf)
    for forced_blk in (0, cur, cur - 1):
        val = jnp.where(blk == forced_blk, jnp.inf, val)
    drop = jnp.full((nb, QB), NEG, F32)
    for _ in range(n_top):
        mx = jnp.max(val, axis=0, keepdims=True)
        first = jnp.min(jnp.where(val == mx, blk, nb), axis=0, keepdims=True)
        first = jnp.where(mx > -jnp.inf, first, nb)
        pick = blk == first
        drop = jnp.where(pick, 0.0, drop)
        val = jnp.where(pick, -jnp.inf, val)
    sel_scr[...] = drop

    m_scr[...] = jnp.full(m_scr.shape, NEG, F32)
    acc_scr[...] = jnp.zeros(acc_scr.shape, F32)
    blocks_per_chunk = FAR_CHUNK * QB // SLC_LEN

    def soft(c, s):
        rows = sel_scr[pl.ds(pl.multiple_of(c * blocks_per_chunk, blocks_per_chunk), blocks_per_chunk), :]
        drop = jnp.concatenate([jnp.broadcast_to(rows[r:r + 1, :], (SLC_LEN, QB)) for r in range(blocks_per_chunk)],
                               axis=0)
        s = s + jnp.concatenate([drop] * GQA, axis=1)
        m_old = m_scr[...]
        m_new = jnp.maximum(m_old, jnp.max(s, axis=0, keepdims=True))
        p = jnp.exp2(s - m_new)
        v_t = jnp.concatenate([vs_ref[0, 0, c * FAR_CHUNK + j] for j in range(FAR_CHUNK)], axis=1)
        acc_scr[...] = jnp.exp2(m_old - m_new) * acc_scr[...] + _dot(v_t, p.astype(BF16))
        m_scr[...] = m_new

    def near_bias(c):
        tiles = []
        for j in range(FAR_CHUNK):
            delta = qb - (c * FAR_CHUNK + j)
            tiles.append(t_ref[0, jnp.where(delta < 0, TILE_MASKED, delta)])
        return jnp.concatenate(tiles, axis=0)

    def chunk_pairs(first, count, bias, sa):
        last = first + jnp.maximum(count - 1, 0)

        def pair_body(j, carry):
            c = first + 2 * j
            sb_scr[...] = qk(c + 1)
            soft(c, sa[...] + bias(c) if bias else sa[...])
            sa[...] = qk(jnp.minimum(c + 2, last))
            soft(c + 1, sb_scr[...] + bias(c + 1) if bias else sb_scr[...])
            return carry

        lax.fori_loop(0, count // 2, pair_body, 0)

        @pl.when(count % 2 == 1)
        def _():
            soft(last, sa[...] + bias(last) if bias else sa[...])

    chunk_pairs(0, n_far, None, sa_scr)
    chunk_pairs(n_far, qb // FAR_CHUNK + 1 - n_far, near_bias, sc_scr)
    acc = acc_scr[...]
    o_s = acc[0:HEAD_DIM, :] / acc[HEAD_DIM:HEAD_DIM + 1, :]

    gates = g_ref[0, 0]
    outs = []
    for hq in range(GQA):
        cols = slice(hq * QB, (hq + 1) * QB)
        c = hq * N_BRANCH
        outs.append(gates[c:c + 1, :] * o_c[0:HEAD_DIM, cols] + gates[c + 1:c + 2, :] * o_s[:, cols]
                    + gates[c + 2:c + 3, :] * o_w[:, cols])
    o_ref[0] = jnp.concatenate(outs, axis=0).T


def _attn(q_t, kx, vx_t, kvc, kvc_t, gates_t, t_tab, w_tab, ov_t, ov):
    B, _, W, _ = q_t.shape
    S = kx.shape[2]
    ncp = kvc.shape[2]
    nb = ov.shape[1]
    keys = lambda j: pl.BlockSpec((1, 1, S, W), lambda b, g, t: (b, j * N_KV + g, 0, 0))
    vals = lambda j: pl.BlockSpec((1, 1, S // QB, V_ROWS, QB), lambda b, g, t: (b, j * N_KV + g, 0, 0, 0))
    cmp = lambda j: pl.BlockSpec((1, 1, ncp, W), lambda b, g, t: (b, j * N_KV + g, 0, 0))
    return pl.pallas_call(
        _attn_kernel,
        grid=(B, N_KV, S // QB),
        in_specs=[pl.BlockSpec((1, 1, W, GQA * QB), lambda b, g, t: (b, g, 0, t)),
                  keys(0), vals(0), keys(1), vals(1), cmp(0), cmp(1),
                  pl.BlockSpec((1, 1, W, ncp - CMP_WIN), lambda b, g, t: (b, N_KV + g, 0, 0)),
                  pl.BlockSpec((1, 1, GATE_ROWS, QB), lambda b, g, t: (b, g, 0, t)),
                  pl.BlockSpec((1,) + t_tab.shape[1:], lambda b, g, t: (g, 0, 0, 0)),
                  pl.BlockSpec((1, CMP_WIN, GQA * QB), lambda b, g, t: (g, 0, 0)),
                  pl.BlockSpec(ov_t.shape, lambda b, g, t: (0, 0)),
                  pl.BlockSpec(ov.shape, lambda b, g, t: (0, 0))],
        out_specs=pl.BlockSpec((1, QB, GQA * HEAD_DIM), lambda b, g, t: (b, t, g)),
        out_shape=jax.ShapeDtypeStruct((B, S, N_HEADS * HEAD_DIM), F32),
        scratch_shapes=[pltpu.VMEM((1, GQA * QB), F32), pltpu.VMEM((V_ROWS, GQA * QB), F32), pltpu.VMEM((nb, QB), F32),
                        pltpu.VMEM((FAR_CHUNK * QB, GQA * QB), F32), pltpu.VMEM((FAR_CHUNK * QB, GQA * QB), F32),
                        pltpu.VMEM((FAR_CHUNK * QB, GQA * QB), F32)],
        compiler_params=_params("parallel", "parallel", "arbitrary"),
        name="attn",
    )(q_t, kx, vx_t, kx, vx_t, kvc, kvc, kvc_t, gates_t, t_tab, w_tab, ov_t, ov)


def _rglru_kernel(xr_ref, gr_ref, cw_ref, cb_ref, wa_ref, ba_ref, wx_ref, bx_ref, lam_ref, gw_ref, o_ref,
                  xbuf, hprev, a_scr, u_scr, h_scr):
    ts = xr_ref.shape[1]
    C = xr_ref.shape[2]

    @pl.when(pl.program_id(1) == 0)
    def _():
        xbuf[0:8, :] = jnp.zeros((8, C), F32)
        hprev[...] = jnp.zeros(hprev.shape, F32)

    xbuf[8:8 + ts, :] = xr_ref[0]
    xc = cb_ref[...] + jnp.zeros((ts, C), F32)
    for j in range(CONV_WIDTH):
        xc = xc + cw_ref[j:j + 1, :] * xbuf[pl.ds(8 - (CONV_WIDTH - 1) + j, ts), :]
    xbuf[0:8, :] = xbuf[ts:ts + 8, :]

    xcb = xc.astype(BF16)
    r = _sigmoid(_dot(xcb, wa_ref[...]) + ba_ref[...])
    i = _sigmoid(_dot(xcb, wx_ref[...]) + bx_ref[...])
    z = -lam_ref[...]
    softplus = jnp.maximum(z, 0.0) + jnp.log(1.0 + jnp.exp(-jnp.abs(z)))
    log_a = -LRU_C * r * softplus
    a = jnp.exp(log_a)
    a_scr[...] = a
    u_scr[...] = jnp.sqrt(1.0 - a * a) * (i * xc)

    row = lax.broadcasted_iota(I32, (8, C), 0)

    def body(k, h):
        r0 = pl.multiple_of(k * 8, 8)
        a = a_scr[pl.ds(r0, 8), :]
        b = u_scr[pl.ds(r0, 8), :]
        for s in (1, 2, 4):
            keep = row >= s
            b = jnp.where(keep, a * pltpu.roll(b, s, 0) + b, b)
            a = jnp.where(keep, a * pltpu.roll(a, s, 0), a)
        hh = a * h + b
        h_scr[pl.ds(r0, 8), :] = hh
        return jnp.broadcast_to(hh[7:8, :], (8, C))

    hprev[...] = lax.fori_loop(0, ts // 8, body, hprev[...])
    out = h_scr[...] * _gelu(gr_ref[0])
    o_ref[0] = (_rms(out) * gw_ref[...]).astype(BF16)


def _rglru(xr, gr, cw, cb, wa_bd, ba, wx_bd, bx, lam, gw, ts=512):
    B, S, C = xr.shape
    ts = min(ts, S)
    vec = pl.BlockSpec((1, C), lambda b, t: (0, 0))
    mat = pl.BlockSpec((C, C), lambda b, t: (0, 0))
    seq = pl.BlockSpec((1, ts, C), lambda b, t: (b, t, 0))
    return pl.pallas_call(
        _rglru_kernel,
        grid=(B, S // ts),
        in_specs=[seq, seq, pl.BlockSpec((CONV_WIDTH, C), lambda b, t: (0, 0)), vec, mat, vec, mat, vec, vec, vec],
        out_specs=seq,
        out_shape=jax.ShapeDtypeStruct((B, S, C), BF16),
        scratch_shapes=[pltpu.VMEM((ts + 8, C), F32), pltpu.VMEM((8, C), F32), pltpu.VMEM((ts, C), F32),
                        pltpu.VMEM((ts, C), F32), pltpu.VMEM((ts, C), F32)],
        compiler_params=_params("parallel", "arbitrary"),
        name="rglru",
    )(xr, gr, cw, cb, wa_bd, ba, wx_bd, bx, lam, gw)


def _to_row_tiles(ref, x):
    rows = x.shape[0]
    for s in range(ROW_TILE):
        ref[pl.ds(s, rows, stride=ROW_TILE), :] = x[:, s * LANES:(s + 1) * LANES]


def _from_row_tiles(ref, rows, start=0):
    return jnp.concatenate([ref[pl.ds(start + s, rows, stride=ROW_TILE), :] for s in range(ROW_TILE)], axis=1)


def _outproj_kernel(oa_ref, yr_ref, x_ref, mod_ref, gaw_ref, wo_ref, n2_ref, wr_ref, x1_ref, h2_ref, st_ref):
    da = oa_ref.shape[2]
    ya = (_rms(oa_ref[0]) * gaw_ref[...]).astype(BF16)
    mix = _dot(ya, wo_ref[0:da, :]) + _dot(yr_ref[0], wo_ref[da:, :])
    x1 = x_ref[0] + mod_ref[0, 2:3, :] * mix
    x1_ref[0] = x1
    h2 = (_rms(x1) * n2_ref[...]) * (1.0 + mod_ref[0, 4:5, :]) + mod_ref[0, 3:4, :]
    h_hi = h2.astype(BF16)
    h2_ref[0] = h_hi
    h_lo = (h2 - h_hi.astype(F32)).astype(BF16)
    logits = _dot_nt(wr_ref[0], h_hi) + (_dot_nt(wr_ref[1], h_hi) + _dot_nt(wr_ref[0], h_lo))
    st_ref[...] = _sigmoid(logits)


def _outproj(oa, yr, x, mod, gaw, wo, n2, wr_t, tm=1024):
    B, S, D = x.shape
    E = wr_t.shape[1]
    nt = S // tm
    row = lambda w: pl.BlockSpec((1, tm, w), lambda b, t: (b, t, 0))
    return pl.pallas_call(
        _outproj_kernel,
        grid=(B, nt),
        in_specs=[row(oa.shape[2]), row(yr.shape[2]), row(D),
                  pl.BlockSpec((1, 6, D), lambda b, t: (b, 0, 0)),
                  pl.BlockSpec((1, oa.shape[2]), lambda b, t: (0, 0)),
                  pl.BlockSpec(wo.shape, lambda b, t: (0, 0)),
                  pl.BlockSpec((1, D), lambda b, t: (0, 0)),
                  pl.BlockSpec((2, E, D), lambda b, t: (0, 0, 0))],
        out_specs=[row(D), row(D), pl.BlockSpec((E, tm), lambda b, t: (0, b * nt + t))],
        out_shape=[jax.ShapeDtypeStruct((B, S, D), F32), jax.ShapeDtypeStruct((B, S, D), BF16),
                   jax.ShapeDtypeStruct((E, B * S), F32)],
        compiler_params=_params("parallel", "parallel"),
        name="outproj",
    )(oa, yr, x, mod, gaw, wo, n2, wr_t)


def _route_kernel(s_ref, b_ref, wgt_ref, pos_ref, tcnt_ref):
    E, tn = s_ref.shape
    per = E // N_EXPERT_GROUPS
    s = s_ref[...]
    s_sel = s + b_ref[...]
    eidx = lax.broadcasted_iota(I32, (E, tn), 0)

    grp = []
    for gi in range(N_EXPERT_GROUPS):
        xg = s_sel[gi * per:(gi + 1) * per, :]
        ig = lax.broadcasted_iota(I32, (per, tn), 0)
        m1 = jnp.max(xg, axis=0, keepdims=True)
        f1 = jnp.min(jnp.where(xg == m1, ig, per), axis=0, keepdims=True)
        m2 = jnp.max(jnp.where(ig == f1, -jnp.inf, xg), axis=0, keepdims=True)
        grp.append(m1 + m2)
    val = []
    for gi in range(N_EXPERT_GROUPS):
        rank = jnp.zeros((1, tn), I32)
        for gj in range(N_EXPERT_GROUPS):
            if gj == gi:
                continue
            ahead = (grp[gj] > grp[gi]) | ((grp[gj] == grp[gi]) & (gj < gi))
            rank = rank + ahead.astype(I32)
        val.append(jnp.where(rank < TOP_GROUPS, s_sel[gi * per:(gi + 1) * per, :], -jnp.inf))
    val = jnp.concatenate(val, axis=0)

    eids, tops = [], []
    hot = jnp.zeros((E, tn), F32)
    for _ in range(TOP_K):
        mx = jnp.max(val, axis=0, keepdims=True)
        first = jnp.min(jnp.where(val == mx, eidx, E), axis=0, keepdims=True)
        pick = eidx == first
        eids.append(first)
        tops.append(jnp.sum(jnp.where(pick, s, 0.0), axis=0, keepdims=True))
        hot = jnp.where(pick, 1.0, hot)
        val = jnp.where(pick, -jnp.inf, val)
    denom = tops[0]
    for t in tops[1:]:
        denom = denom + t

    ti = lax.broadcasted_iota(I32, (tn, tn), 0)
    tj = lax.broadcasted_iota(I32, (tn, tn), 1)
    upper = jnp.where(ti < tj, 1.0, 0.0).astype(BF16)
    before = _dot(hot.astype(BF16), upper)
    tile_count = jnp.sum(hot, axis=1, keepdims=True)
    padded = tile_count + (tile_count - 2.0 * jnp.floor(tile_count * 0.5))
    ei = lax.broadcasted_iota(I32, (E, E), 0)
    ej = lax.broadcasted_iota(I32, (E, E), 1)
    earlier_experts = jnp.where(ej < ei, 1.0, 0.0).astype(BF16)
    run_start = _dot(earlier_experts, jnp.broadcast_to(padded, (E, LANES)).astype(BF16))[:, 0:1]
    for k in range(TOP_K):
        wgt_ref[k:k + 1, :] = ROUTED_SCALE * tops[k] / denom
        pos_ref[k:k + 1, :] = jnp.sum(jnp.where(eidx == eids[k], before + run_start, 0.0), axis=0,
                                      keepdims=True).astype(I32)
    tcnt_ref[0] = tile_count


def _route(s_t, b_col, tn):
    E, N = s_t.shape
    out = pl.BlockSpec((TOP_K, tn), lambda t: (0, t))
    return pl.pallas_call(
        _route_kernel,
        grid=(N // tn,),
        in_specs=[pl.BlockSpec((E, tn), lambda t: (0, t)), pl.BlockSpec((E, 1), lambda t: (0, 0))],
        out_specs=[out, out, pl.BlockSpec((1, E, 1), lambda t: (t, 0, 0))],
        out_shape=[jax.ShapeDtypeStruct((TOP_K, N), F32), jax.ShapeDtypeStruct((TOP_K, N), I32),
                   jax.ShapeDtypeStruct((N // tn, E, 1), F32)],
        compiler_params=_params("parallel"),
        name="route",
    )(s_t, b_col)


_HI = -65536


def _pack_pairs(even_rows, odd_rows):
    hi = lax.bitcast_convert_type(even_rows, I32) & _HI
    lo = lax.shift_right_logical(lax.bitcast_convert_type(odd_rows, I32), 16)
    return hi | lo


def _unpack_pairs(words):
    even_rows = lax.bitcast_convert_type(words & _HI, F32)
    odd_rows = lax.bitcast_convert_type(lax.shift_left(words, 16), F32)
    return even_rows, odd_rows


def _run_copies(src_ref, cnt_ref, dst_ref, n_runs, copy):
    def go(e, carry):
        n = cnt_ref[0, 0, e]

        @pl.when(n > 0)
        def _():
            size = pl.multiple_of(n * ROW_TILE, ROW_TILE)
            copy(pl.ds(pl.multiple_of(src_ref[0, 0, e] * ROW_TILE, ROW_TILE), size),
                 pl.ds(pl.multiple_of(dst_ref[0, 0, e] * ROW_TILE, ROW_TILE), size)).start()
        return carry
    lax.fori_loop(0, n_runs, go, 0, unroll=8)


def _pair_onehots(pos_rows, n_pairs, first_pair, width):
    even = 2 * (lax.broadcasted_iota(I32, (n_pairs, width), 0) + first_pair)
    hot_e = jnp.zeros((n_pairs, width), F32)
    hot_o = jnp.zeros((n_pairs, width), F32)
    for k in range(TOP_K):
        hot_e = jnp.where(even == pos_rows[k:k + 1, :], 1.0, hot_e)
        hot_o = jnp.where(even + 1 == pos_rows[k:k + 1, :], 1.0, hot_o)
    return hot_e, hot_o


def _dispatch_kernel(src_ref, cnt_ref, dst_ref, pos_ref, h_ref, xs_hbm, xbuf, sem):
    n_pairs = xbuf.shape[0] // ROW_TILE
    tn = h_ref.shape[0]
    hb = h_ref[...]
    slab = n_pairs // 3
    for sl in range(3):
        hot_e, hot_o = _pair_onehots(pos_ref[...], slab, sl * slab, tn)
        words = _pack_pairs(_dot(hot_e.astype(BF16), hb), _dot(hot_o.astype(BF16), hb))
        for s in range(ROW_TILE):
            xbuf[pl.ds(sl * slab * ROW_TILE + s, slab, stride=ROW_TILE), :] = words[:, s * LANES:(s + 1) * LANES]
    _run_copies(src_ref, cnt_ref, dst_ref, cnt_ref.shape[2],
                lambda src, dst: pltpu.make_async_copy(xbuf.at[src], xs_hbm.at[dst], sem.at[0]))
    pltpu.make_async_copy(xbuf, xs_hbm.at[pl.ds(0, xbuf.shape[0])], sem.at[0]).wait()


def _run_specs(runs):
    return [pl.BlockSpec((1, 1, r.shape[2]), lambda t: (t, 0, 0), memory_space=pltpu.SMEM) for r in runs]


def _tile_pairs(tn, n_experts):
    return (TOP_K * tn + n_experts) // 2


def _dispatch(runs, pos_t, h2, tn):
    N, D = h2.shape
    W = D // ROW_TILE
    n_pairs = _tile_pairs(tn, runs[0].shape[2] - 1)
    assert n_pairs % 24 == 0
    return pl.pallas_call(
        _dispatch_kernel,
        grid=(N // tn,),
        in_specs=_run_specs(runs) + [pl.BlockSpec((TOP_K, tn), lambda t: (0, t)),
                                     pl.BlockSpec((tn, D), lambda t: (t, 0))],
        out_specs=pl.BlockSpec(memory_space=pl.ANY),
        out_shape=jax.ShapeDtypeStruct((N // tn * n_pairs * ROW_TILE, W), I32),
        scratch_shapes=[pltpu.VMEM((n_pairs * ROW_TILE, W), I32), pltpu.SemaphoreType.DMA((1,))],
        compiler_params=_params("arbitrary"),
        name="dispatch",
    )(*runs, pos_t, h2)


def _experts_kernel(first_ref, size_ref, e_ref, new_ref, next_ref, wslot_ref, total_ref,
                    w1_hbm, w3_hbm, w2_hbm, xs_hbm, ys_hbm,
                    xbuf, ybuf, w1f, w3f, w2f, w1b, w3b, w2b, xsem, ysem, wsem):
    total = total_ref[0]

    def w_copies(e, slot):
        return [pltpu.make_async_copy(src.at[e], dst.at[slot], wsem.at[slot])
                for src, dst in ((w1_hbm, w1f), (w3_hbm, w3f), (w2_hbm, w2f))]

    def rows(c):
        return (pl.multiple_of(first_ref[c] * ROW_TILE, ROW_TILE), pl.multiple_of(size_ref[c] * ROW_TILE, ROW_TILE))

    def x_copy(c, slot):
        first, size = rows(c)
        return pltpu.make_async_copy(xs_hbm.at[pl.ds(first, size)], xbuf.at[slot, pl.ds(0, size)], xsem.at[slot])

    def y_copy(c, slot):
        first, size = rows(c)
        return pltpu.make_async_copy(ybuf.at[slot, pl.ds(0, size)], ys_hbm.at[pl.ds(first, size)], ysem.at[slot])

    xbuf[...] = jnp.zeros(xbuf.shape, I32)
    for cp in w_copies(e_ref[0], 0):
        cp.start()
    x_copy(0, 0).start()
    half = MOE_BLOCK // 2

    def body(c, carry):
        slot = c % 2

        @pl.when(new_ref[c] == 1)
        def _():
            ws = wslot_ref[c]
            for cp in w_copies(e_ref[c], ws):
                cp.wait()
            w1b[...] = w1f[ws].astype(BF16)
            w3b[...] = w3f[ws].astype(BF16)
            w2b[...] = w2f[ws].astype(BF16)

            @pl.when(next_ref[c] >= 0)
            def _():
                for cp in w_copies(next_ref[c], 1 - ws):
                    cp.start()

        x_copy(c, slot).wait()

        @pl.when(c + 1 < total)
        def _():
            x_copy(c + 1, 1 - slot).start()

        x = jnp.concatenate(_unpack_pairs(_from_row_tiles(xbuf.at[slot], half)), axis=0).astype(BF16)
        h1 = _dot(x, w1b[...])
        h3 = _dot(x, w3b[...])
        hid = (h1 * _sigmoid(h1)) * h3
        y = _dot(hid.astype(BF16), w2b[...]).astype(BF16).astype(F32)

        @pl.when(c >= 2)
        def _():
            y_copy(c - 2, slot).wait()

        _to_row_tiles(ybuf.at[slot], _pack_pairs(y[0:half], y[half:]))
        y_copy(c, slot).start()
        return carry

    lax.fori_loop(0, total, body, 0)

    @pl.when(total >= 2)
    def _():
        y_copy(total - 2, total % 2).wait()

    y_copy(total - 1, (total - 1) % 2).wait()

    ybuf[0] = jnp.zeros(ybuf.shape[1:], I32)
    tail_first = total_ref[1]
    tail_pairs = ys_hbm.shape[0] // ROW_TILE - tail_first

    def fill(j, carry):
        size = pl.multiple_of(jnp.minimum(half, tail_pairs - j * half) * ROW_TILE, ROW_TILE)
        cp = pltpu.make_async_copy(ybuf.at[0, pl.ds(0, size)],
                                   ys_hbm.at[pl.ds(pl.multiple_of((tail_first + j * half) * ROW_TILE, ROW_TILE), size)],
                                   ysem.at[0])
        cp.start()
        cp.wait()
        return carry

    lax.fori_loop(0, (tail_pairs + half - 1) // half, fill, 0)


def _expert_chunks(counts, n_pairs_total):
    E = counts.shape[0]
    block = MOE_BLOCK // 2
    n_max = n_pairs_total // block + E
    starts = jnp.cumsum(counts) - counts
    n_chunks = (counts + block - 1) // block
    c_end = jnp.cumsum(n_chunks)
    c = jnp.arange(n_max, dtype=I32)
    e = jnp.minimum(jnp.sum(c_end[None, :] <= c[:, None], axis=1), E - 1).astype(I32)
    j = c - (c_end - n_chunks)[e]
    size = jnp.clip(counts[e] - j * block, 0, block)
    has_rows = counts > 0
    ids = jnp.where(has_rows, jnp.arange(E, dtype=I32), E)
    later = jnp.flip(lax.cummin(jnp.flip(ids)))
    nxt = jnp.concatenate([later[1:], jnp.full((1,), E, I32)])
    nxt = jnp.where(nxt >= E, -1, nxt)
    ordinal = jnp.cumsum(has_rows.astype(I32)) - 1
    to_i32 = lambda a: a.astype(I32)
    meta = jnp.stack([c_end[-1], jnp.sum(counts)])
    return tuple(map(to_i32, (starts[e] + j * block, size, e, j == 0, nxt[e], ordinal[e] % 2, meta)))


def _experts(chunks, xs, w1, w3, w2):
    rows_total, W = xs.shape
    _, D, De = w1.shape
    any_space = pl.BlockSpec(memory_space=pl.ANY)
    block_rows = MOE_BLOCK // 2 * ROW_TILE
    grid_spec = pltpu.PrefetchScalarGridSpec(
        num_scalar_prefetch=len(chunks),
        grid=(1,),
        in_specs=[any_space] * 4,
        out_specs=any_space,
        scratch_shapes=[pltpu.VMEM((2, block_rows, W), I32), pltpu.VMEM((2, block_rows, W), I32),
                        pltpu.VMEM((2, D, De), F32), pltpu.VMEM((2, D, De), F32), pltpu.VMEM((2, De, D), F32),
                        pltpu.VMEM((D, De), BF16), pltpu.VMEM((D, De), BF16), pltpu.VMEM((De, D), BF16),
                        pltpu.SemaphoreType.DMA((2,)), pltpu.SemaphoreType.DMA((2,)), pltpu.SemaphoreType.DMA((2,))],
    )
    return pl.pallas_call(
        _experts_kernel,
        grid_spec=grid_spec,
        out_shape=jax.ShapeDtypeStruct((rows_total, W), I32),
        compiler_params=_params("arbitrary"),
        name="experts",
    )(*chunks, w1, w3, w2, xs)


def _combine_kernel(src_ref, cnt_ref, dst_ref, ys_hbm, post_ref, wt_ref, h2_ref, x1_ref, mod_ref, ws1_ref,
                    ws3_ref, ws2_ref, fw_ref, o_ref, ybuf, sem):
    tm = h2_ref.shape[0]
    n_pairs = ybuf.shape[0] // ROW_TILE
    n_exp = cnt_ref.shape[2] - 1

    @pl.when(pl.program_id(0) == 0)
    def _():
        ybuf[...] = jnp.zeros(ybuf.shape, I32)

    _run_copies(src_ref, cnt_ref, dst_ref, n_exp,
                lambda src, dst: pltpu.make_async_copy(ys_hbm.at[dst], ybuf.at[src], sem.at[0]))

    h = h2_ref[...]
    a = _dot(h, ws1_ref[...])
    hid = (a * _sigmoid(a)) * _dot(h, ws3_ref[...])
    shared = _dot(hid.astype(BF16), ws2_ref[...])

    pair = 2 * lax.broadcasted_iota(I32, (n_pairs, tm), 0)
    w_cols, mine = [], []
    for odd in range(2):
        w_rows = jnp.zeros((n_pairs, tm), F32)
        for k in range(TOP_K):
            w_rows = jnp.where(pair + odd == post_ref[k:k + 1, :], wt_ref[k:k + 1, :], w_rows)
        w_cols.append(jnp.sum(w_rows, axis=1, keepdims=True))
        mine.append(jnp.where(w_rows != 0.0, 1.0, 0.0).astype(BF16))
    used = pl.multiple_of((src_ref[0, 0, n_exp]) * ROW_TILE, ROW_TILE)
    pltpu.make_async_copy(ys_hbm.at[pl.ds(0, used)], ybuf.at[pl.ds(0, used)], sem.at[0]).wait()
    y_even, y_odd = _unpack_pairs(_from_row_tiles(ybuf, n_pairs))
    sum_rows = lambda hot, y: lax.dot_general(hot, y, (((0,), (0,)), ((), ())), preferred_element_type=F32)
    routed = (sum_rows(mine[0], (y_even * w_cols[0]).astype(BF16))
              + sum_rows(mine[1], (y_odd * w_cols[1]).astype(BF16)))
    x2 = x1_ref[...] + mod_ref[0, 5:6, :] * (routed + shared)
    o_ref[...] = _rms(x2) * fw_ref[...]


def _combine(runs, ys, pos_t, wgt_t, h2, x1, mod, ws1, ws3, ws2, fw, seq, tm):
    N, D = h2.shape
    nt = seq // tm
    row = pl.BlockSpec((tm, D), lambda t: (t, 0))
    full = lambda a: pl.BlockSpec(a.shape, lambda t: (0,) * a.ndim)
    return pl.pallas_call(
        _combine_kernel,
        grid=(N // tm,),
        in_specs=_run_specs(runs) + [pl.BlockSpec(memory_space=pl.ANY),
                                     pl.BlockSpec((TOP_K, tm), lambda t: (0, t)),
                                     pl.BlockSpec((TOP_K, tm), lambda t: (0, t)),
                                     row, row,
                                     pl.BlockSpec((1, 6, D), lambda t: (t // nt, 0, 0)),
                                     full(ws1), full(ws3), full(ws2), full(fw)],
        out_specs=row,
        out_shape=jax.ShapeDtypeStruct((N, D), F32),
        scratch_shapes=[pltpu.VMEM((_tile_pairs(tm, runs[0].shape[2] - 1) * ROW_TILE, D // ROW_TILE), I32),
                        pltpu.SemaphoreType.DMA((1,))],
        compiler_params=_params("arbitrary"),
        name="combine",
    )(*runs, ys, pos_t, wgt_t, h2, x1, mod, ws1, ws3, ws2, fw)


def _t5_bucket(dist):
    n = jnp.maximum(dist, 0)
    max_exact = REL_BUCKETS // 2
    nf = jnp.maximum(n, 1).astype(F32)
    large = max_exact + (jnp.log(nf / max_exact) / math.log(REL_MAX_DIST / max_exact)
                         * (REL_BUCKETS - max_exact)).astype(I32)
    large = jnp.minimum(large, REL_BUCKETS - 1)
    return jnp.where(n < max_exact, n, large)


def _bias_kernel(tbl_ref, far_ref, bt_ref, bw_ref, t_ref, w_ref):
    g = pl.program_id(0)
    d = pl.program_id(1)

    def lookup(bkt, head):
        out = jnp.full(bkt.shape, tbl_ref[0, head], F32)
        for b in range(1, REL_BUCKETS):
            out = jnp.where(bkt == b, tbl_ref[b, head], out)
        return out - far_ref[head]

    j = lax.broadcasted_iota(I32, (QB, QB), 0)
    i = lax.broadcasted_iota(I32, (QB, QB), 1)
    keep = ((d != 0) | (i >= j)) & (d != TILE_MASKED) & ((d != TILE_WINDOW_EDGE) | (j > i))
    for hq in range(GQA):
        t_ref[0, 0, :, hq * QB:(hq + 1) * QB] = jnp.where(keep, lookup(bt_ref[0], GQA * g + hq), NEG)

    @pl.when(d == 0)
    def _():
        for hq in range(GQA):
            w_ref[0, :, hq * QB:(hq + 1) * QB] = lookup(bw_ref[...], GQA * g + hq)


def _bias_tables(rel_table):
    tbl = rel_table.astype(F32) * LOG2E
    far_hi = tbl[REL_BUCKETS - 1].astype(BF16)
    far_lo = (tbl[REL_BUCKETS - 1] - far_hi.astype(F32)).astype(BF16)
    far = far_hi.astype(F32) + far_lo.astype(F32)
    q_tail = jnp.zeros((N_HEADS, HEAD_DIM), F32).at[:, 0].set(far_hi.astype(F32)).at[:, 1].set(far_lo.astype(F32))
    i = jnp.arange(QB)
    tile_dist = np.array(list(range(N_BIAS_TILES)) + [0, WINDOW // QB])
    d_t = (QB * jnp.asarray(tile_dist)[:, None, None] + i[None, None, :] - i[None, :, None])
    d_w = i[None, :] - CMP_STRIDE * (jnp.arange(CMP_WIN)[:, None] - CMP_PAD) - (CMP_LEN - 1)
    n_tiles = len(tile_dist)
    t_tab, w_tab = pl.pallas_call(
        _bias_kernel,
        grid=(N_KV, n_tiles),
        in_specs=[pl.BlockSpec(memory_space=pltpu.SMEM),
                  pl.BlockSpec(memory_space=pltpu.SMEM),
                  pl.BlockSpec((1, QB, QB), lambda g, d: (d, 0, 0)),
                  pl.BlockSpec((CMP_WIN, QB), lambda g, d: (0, 0))],
        out_specs=[pl.BlockSpec((1, 1, QB, GQA * QB), lambda g, d: (g, d, 0, 0)),
                   pl.BlockSpec((1, CMP_WIN, GQA * QB), lambda g, d: (g, 0, 0))],
        out_shape=[jax.ShapeDtypeStruct((N_KV, n_tiles, QB, GQA * QB), F32),
                   jax.ShapeDtypeStruct((N_KV, CMP_WIN, GQA * QB), F32)],
        compiler_params=_params("parallel", "arbitrary"),
        name="bias",
    )(tbl, far, _t5_bucket(d_t), _t5_bucket(d_w))
    far_min = min(FAR_TILE_DIST * QB - QB + 1, CMP_STRIDE * (CMP_PAD + 1) - (CMP_LEN - 1))
    max_exact = REL_BUCKETS // 2
    assert math.log(far_min / max_exact) / math.log(REL_MAX_DIST / max_exact) * max_exact > max_exact - 0.75
    return t_tab, w_tab, q_tail


def _overlap_tables(nc, nb):
    c_start = CMP_STRIDE * np.arange(nc)[:, None]
    s_start = SLC_LEN * np.arange(nb)[None, :]
    ov = ((c_start < s_start + SLC_LEN) & (c_start + CMP_LEN > s_start)).astype(np.float32)
    ov[nc - 1:] = 0.0
    out = np.zeros((nc + CMP_WIN, nb), np.float32)
    out[CMP_PAD:CMP_PAD + nc] = ov
    return jnp.asarray(ov.T, dtype=BF16), jnp.asarray(out)


def _block_diag(w):
    nblk, bs, _ = w.shape
    eye = jnp.eye(nblk, dtype=w.dtype)
    return (eye[:, None, :, None] * w[:, :, None, :]).reshape(nblk * bs, nblk * bs)


def _layer(x, c8, w_ada, b_ada, norm1_w, w_in, cmp_pos, cmp_k_w1, cmp_k_w2, cmp_v_w1, cmp_v_w2, rel_table,
           conv_w, conv_b, lru_wa, lru_ba, lru_wx, lru_bx, lru_lambda, gnorm_attn_w, gnorm_rnn_w, w_out,
           norm2_w, w_router, b_router, w1, w3, w2, ws1, ws3, ws2, final_norm_w):
    B, S, D = x.shape
    N = B * S
    E = w_router.shape[1]
    d_attn = N_HEADS * HEAD_DIM
    d_rnn = D - d_attn
    assert S % (FAR_CHUNK * QB) == 0 and S // SLC_LEN <= MAX_SLC_BLOCKS

    mod = _ada(c8, w_ada, b_ada[None, :])[:B].reshape(B, 6, D)

    n_kv_cols = 6 * N_KV * HEAD_DIM
    n_gate = N_BRANCH * N_HEADS
    wq, wkv, wg, wxr, wgr = jnp.split(w_in, np.cumsum([d_attn, n_kv_cols, n_gate, d_rnn]).tolist(), axis=1)
    wg = jnp.pad(wg.reshape(D, N_KV, GQA * N_BRANCH), ((0, 0), (0, 0), (0, LANES - GQA * N_BRANCH)))
    wg = wg.reshape(D, N_KV * LANES)
    w_pad = jnp.concatenate([wq, wkv, wg, wxr, wgr], axis=1).astype(BF16)
    t_tab, w_tab, q_tail = _bias_tables(rel_table)
    q_t, kx, vx_t, kvc, gates_t, xr, gr = _inproj(x, mod, norm1_w[None, :], w_pad, q_tail)

    nc = S // CMP_STRIDE
    pos2 = cmp_pos.reshape(2, CMP_STRIDE * HEAD_DIM)
    kvc_pad, kvc_t = _compress(kvc, pos2, jnp.stack([cmp_k_w1, cmp_v_w1]), jnp.stack([cmp_k_w2, cmp_v_w2]))
    o_attn = _attn(q_t, kx, vx_t, kvc_pad, kvc_t, gates_t, t_tab, w_tab, *_overlap_tables(nc, MAX_SLC_BLOCKS))

    y_rnn = _rglru(xr, gr, conv_w.reshape(CONV_WIDTH, d_rnn), conv_b[None, :], _block_diag(lru_wa).astype(BF16),
                   lru_ba[None, :], _block_diag(lru_wx).astype(BF16), lru_bx[None, :], lru_lambda[None, :],
                   gnorm_rnn_w[None, :])

    wr_hi = w_router.T.astype(BF16)
    wr_lo = (w_router.T - wr_hi.astype(F32)).astype(BF16)
    x1, h2, s_t = _outproj(o_attn, y_rnn, x, mod, gnorm_attn_w[None, :], w_out.astype(BF16), norm2_w[None, :],
                           jnp.stack([wr_hi, wr_lo]))

    tile = min(MOE_TILE, S)
    wgt_t, pos_t, tile_counts = _route(s_t, b_router[:, None], tile)
    tcnt = tile_counts[:, :, 0].astype(I32)
    n_tiles = tcnt.shape[0]
    tile_pairs = _tile_pairs(tile, E)
    rpairs = (tcnt + 1) // 2
    rpairs = jnp.concatenate([rpairs, tile_pairs - jnp.sum(rpairs, axis=1, keepdims=True)], axis=1)
    per_expert = jnp.sum(rpairs, axis=0)
    off = jnp.cumsum(per_expert) - per_expert
    run_src = jnp.cumsum(rpairs, axis=1) - rpairs
    run_dst = off[None, :] + jnp.cumsum(rpairs, axis=0) - rpairs
    runs = tuple(r.astype(I32)[:, None, :] for r in (run_src, rpairs, run_dst))

    h2f = h2.reshape(N, D)
    xs = _dispatch(runs, pos_t, h2f, tile)
    ys = _experts(_expert_chunks(per_expert[:E], n_tiles * tile_pairs), xs, w1, w3, w2)
    out = _combine(runs, ys, pos_t, wgt_t, h2f, x1.reshape(N, D), mod, ws1.astype(BF16), ws3.astype(BF16),
                   ws2.astype(BF16), final_norm_w[None, :], S, tile)
    return out.reshape(B, S, D)


def kernel(x, c, w_ada, b_ada, norm1_w, w_in, cmp_pos, cmp_k_w1, cmp_k_w2, cmp_v_w1, cmp_v_w2, rel_table, conv_w, conv_b, lru_wa, lru_ba, lru_wx, lru_bx, lru_lambda, gnorm_attn_w, gnorm_rnn_w, w_out, norm2_w, w_router, b_router, w1, w3, w2, ws1, ws3, ws2, final_norm_w):
    assert w_ada.shape[0] == 1
    c8 = jnp.pad(c, ((0, 8 - c.shape[0]), (0, 0)))
    return _layer(x, c8, w_ada[0], b_ada[0], norm1_w[0], w_in[0], cmp_pos[0], cmp_k_w1[0], cmp_k_w2[0],
                  cmp_v_w1[0], cmp_v_w2[0], rel_table, conv_w[0], conv_b[0], lru_wa[0], lru_ba[0], lru_wx[0],
                  lru_bx[0], lru_lambda[0], gnorm_attn_w[0], gnorm_rnn_w[0], w_out[0], norm2_w[0], w_router[0],
                  b_router[0], w1[0], w3[0], w2[0], ws1[0], ws3[0], ws2[0], final_norm_w)
```

```python
import math

import jax
import jax.numpy as jnp
import numpy as np
from jax import lax
from jax.experimental import pallas as pl
from jax.experimental.pallas import tpu as pltpu

F32 = jnp.float32
BF16 = jnp.bfloat16
I32 = jnp.int32

HEAD_DIM = 64
N_HEADS = 8
N_KV = 2
GQA = N_HEADS // N_KV
N_BRANCH = 3
CONV_WIDTH = 4
LRU_C = 8.0
CMP_LEN = 32
CMP_STRIDE = 16
SLC_LEN = 64
SLC_TOP = 16
WINDOW = 512
LANES = 128
QB = 128
MAX_SLC_BLOCKS = 128
GATE_ROWS = 16
V_ROWS = 80
REL_BUCKETS = 32
REL_MAX_DIST = 1024
N_EXPERT_GROUPS = 8
TOP_GROUPS = 4
TOP_K = 8
ROUTED_SCALE = 2.5
MOE_BLOCK = 576
MOE_TILE = 256
ROW_TILE = 8
EPS = 1e-6
NEG = -1e30
LOG2E = math.log2(math.e)
CMP_PAD = 120
CMP_WIN = 128
N_BIAS_TILES = 11
TILE_MASKED = N_BIAS_TILES
TILE_WINDOW_EDGE = N_BIAS_TILES + 1
FAR_TILE_DIST = 8
FAR_CHUNK = 4
VMEM_LIMIT = 52 * 1024 * 1024


def _dot(a, b, **kw):
    return jnp.dot(a, b, preferred_element_type=F32, **kw)


def _dot_nt(a, b, **kw):
    return lax.dot_general(a, b, (((1,), (1,)), ((), ())), preferred_element_type=F32, **kw)


def _gelu(x):
    return 0.5 * x * (1.0 + jnp.tanh(math.sqrt(2.0 / math.pi) * (x + 0.044715 * (x * x * x))))


def _sigmoid(x):
    return 1.0 / (1.0 + jnp.exp(-x))


def _rms(x):
    return x * lax.rsqrt(jnp.mean(x * x, axis=-1, keepdims=True) + EPS)


def _params(*sem):
    return pltpu.CompilerParams(dimension_semantics=sem, vmem_limit_bytes=VMEM_LIMIT)


def _ada_kernel(c_ref, w_ref, b_ref, o_ref):
    c = c_ref[...]
    a = c * _sigmoid(c)
    o_ref[...] = _dot(a, w_ref[...], precision=lax.Precision.HIGHEST) + b_ref[...]


def _ada(c8, w, b):
    d, n = w.shape
    tn = 1536
    return pl.pallas_call(
        _ada_kernel,
        grid=(n // tn,),
        in_specs=[pl.BlockSpec((8, d), lambda j: (0, 0)),
                  pl.BlockSpec((d, tn), lambda j: (0, j)),
                  pl.BlockSpec((1, tn), lambda j: (0, j))],
        out_specs=pl.BlockSpec((8, tn), lambda j: (0, j)),
        out_shape=jax.ShapeDtypeStruct((8, n), F32),
        compiler_params=_params("parallel"),
        name="ada",
    )(c8, w, b)


def _inproj_kernel(x_ref, mod_ref, nw_ref, w_ref, qt_ref, q_ref, kx_ref, vx_ref, kvc_ref, g_ref, xr_ref, gr_ref):
    h = _rms(x_ref[0]) * nw_ref[...]
    h = h * (1.0 + mod_ref[0, 1:2, :]) + mod_ref[0, 0:1, :]
    p = _dot(h.astype(BF16), w_ref[...])
    tm = p.shape[0]
    dq = N_HEADS * HEAD_DIM
    lane = lax.broadcasted_iota(I32, (tm, HEAD_DIM), 1)
    k_tail = jnp.where(lane < 2, 1.0, 0.0)
    v_tail = jnp.where(lane < 1, 1.0, 0.0)
    for hh in range(N_HEADS):
        qh = p[:, hh * HEAD_DIM:(hh + 1) * HEAD_DIM] * (HEAD_DIM ** -0.5 * LOG2E)
        q_tail = jnp.broadcast_to(qt_ref[hh:hh + 1, :], (tm, HEAD_DIM))
        q_t = jnp.concatenate([qh, q_tail], axis=1).T.astype(BF16)
        for blk in range(tm // QB):
            col = (blk * GQA + hh % GQA) * QB
            q_ref[0, hh // GQA, :, col:col + QB] = q_t[:, blk * QB:(blk + 1) * QB]
    for j in range(6):
        for g in range(N_KV):
            col = dq + (j * N_KV + g) * HEAD_DIM
            piece = p[:, col:col + HEAD_DIM]
            if j < 2:
                kvc_ref[0, j * N_KV + g] = piece
            elif j % 2 == 0:
                kx_ref[0, (j // 2 - 1) * N_KV + g] = jnp.concatenate([piece, k_tail], axis=1).astype(BF16)
            else:
                v_t = jnp.concatenate([piece, v_tail], axis=1).T.astype(BF16)
                for blk in range(tm // QB):
                    vx_ref[0, (j // 2 - 1) * N_KV + g, blk] = v_t[0:V_ROWS, blk * QB:(blk + 1) * QB]
    c0 = dq + 6 * N_KV * HEAD_DIM
    for g in range(N_KV):
        g_ref[0, g] = _sigmoid(p[:, c0 + g * LANES:c0 + (g + 1) * LANES]).T[0:GATE_ROWS, :]
    c1 = c0 + N_KV * LANES
    d_rnn = xr_ref.shape[2]
    xr_ref[0] = p[:, c1:c1 + d_rnn]
    gr_ref[0] = p[:, c1 + d_rnn:c1 + 2 * d_rnn]


def _inproj(x, mod, nw, w_pad, q_tail, tm=1024):
    B, S, D = x.shape
    ncol = w_pad.shape[1]
    heads = lambda n, w: pl.BlockSpec((1, n, tm, w), lambda b, t: (b, 0, t, 0))
    return pl.pallas_call(
        _inproj_kernel,
        grid=(B, S // tm),
        in_specs=[pl.BlockSpec((1, tm, D), lambda b, t: (b, t, 0)),
                  pl.BlockSpec((1, 6, D), lambda b, t: (b, 0, 0)),
                  pl.BlockSpec((1, D), lambda b, t: (0, 0)),
                  pl.BlockSpec((D, ncol), lambda b, t: (0, 0)),
                  pl.BlockSpec(q_tail.shape, lambda b, t: (0, 0))],
        out_specs=[pl.BlockSpec((1, N_KV, 2 * HEAD_DIM, tm * GQA), lambda b, t: (b, 0, 0, t)),
                   heads(4, 2 * HEAD_DIM),
                   pl.BlockSpec((1, 4, tm // QB, V_ROWS, QB), lambda b, t: (b, 0, t, 0, 0)),
                   heads(4, HEAD_DIM),
                   pl.BlockSpec((1, N_KV, GATE_ROWS, tm), lambda b, t: (b, 0, 0, t)),
                   pl.BlockSpec((1, tm, 512), lambda b, t: (b, t, 0)),
                   pl.BlockSpec((1, tm, 512), lambda b, t: (b, t, 0))],
        out_shape=[jax.ShapeDtypeStruct((B, N_KV, 2 * HEAD_DIM, S * GQA), BF16),
                   jax.ShapeDtypeStruct((B, 4, S, 2 * HEAD_DIM), BF16),
                   jax.ShapeDtypeStruct((B, 4, S // QB, V_ROWS, QB), BF16),
                   jax.ShapeDtypeStruct((B, 4, S, HEAD_DIM), F32),
                   jax.ShapeDtypeStruct((B, N_KV, GATE_ROWS, S), F32),
                   jax.ShapeDtypeStruct((B, S, 512), F32),
                   jax.ShapeDtypeStruct((B, S, 512), F32)],
        compiler_params=_params("parallel", "parallel"),
        name="inproj",
    )(x, mod, nw, w_pad, q_tail)


def _compress_kernel(x_ref, pos_ref, w1_ref, w2_ref, o_ref, ot_ref):
    nc = x_ref.shape[2] // CMP_STRIDE
    half = CMP_STRIDE * HEAD_DIM
    kv = pl.program_id(1) // N_KV
    a = jnp.zeros((nc, HEAD_DIM), F32)
    b = jnp.zeros((nc, HEAD_DIM), F32)
    for l in range(CMP_STRIDE):
        x_l = x_ref[0, 0, pl.ds(l, nc, stride=CMP_STRIDE), :]
        cols = slice(l * HEAD_DIM, (l + 1) * HEAD_DIM)
        a = a + _dot((x_l + pos_ref[0:1, cols]).astype(BF16), w1_ref[kv, cols, :].astype(BF16))
        b = b + _dot((x_l + pos_ref[1:2, cols]).astype(BF16),
                     w1_ref[kv, half + l * HEAD_DIM:half + (l + 1) * HEAD_DIM, :].astype(BF16))
    hid = _gelu(a + pltpu.roll(b, nc - 1, 0))
    out = _dot(hid.astype(BF16), w2_ref[kv].astype(BF16))
    row = lax.broadcasted_iota(I32, out.shape, 0)
    out = jnp.where(row < nc - 1, out, 0.0)
    lane = lax.broadcasted_iota(I32, (nc, HEAD_DIM), 1)
    out = jnp.concatenate([out, jnp.where(lane < 2, 1.0, 0.0)], axis=1)
    o_ref[0, 0, 0:CMP_PAD, :] = jnp.zeros((CMP_PAD, 2 * HEAD_DIM), F32)
    o_ref[0, 0, CMP_PAD:CMP_PAD + nc, :] = out
    o_ref[0, 0, CMP_PAD + nc:, :] = jnp.zeros((CMP_WIN - CMP_PAD, 2 * HEAD_DIM), F32)
    ot_ref[0, 0] = out.T


def _compress(kvc, pos2, w1, w2):
    B, _, S, _ = kvc.shape
    NC = S // CMP_STRIDE
    W = CMP_STRIDE * HEAD_DIM
    return pl.pallas_call(
        _compress_kernel,
        grid=(B, 4),
        in_specs=[pl.BlockSpec((1, 1, S, HEAD_DIM), lambda b, i: (b, i, 0, 0)),
                  pl.BlockSpec((2, W), lambda b, i: (0, 0)),
                  pl.BlockSpec((2, 2 * W, HEAD_DIM), lambda b, i: (0, 0, 0)),
                  pl.BlockSpec((2, HEAD_DIM, HEAD_DIM), lambda b, i: (0, 0, 0))],
        out_specs=[pl.BlockSpec((1, 1, NC + CMP_WIN, 2 * HEAD_DIM), lambda b, i: (b, i, 0, 0)),
                   pl.BlockSpec((1, 1, 2 * HEAD_DIM, NC), lambda b, i: (b, i, 0, 0))],
        out_shape=[jax.ShapeDtypeStruct((B, 4, NC + CMP_WIN, 2 * HEAD_DIM), F32),
                   jax.ShapeDtypeStruct((B, 4, 2 * HEAD_DIM, NC), F32)],
        compiler_params=_params("parallel", "parallel"),
        name="compress",
    )(kvc, pos2, w1, w2)


def _attn_kernel(q_ref, ks_ref, vs_ref, kw_ref, vw_ref, kc_ref, vc_ref, vct_ref, g_ref, t_ref, w_ref, ovt_ref, ov_ref,
                 o_ref, m_scr, acc_scr, sel_scr, sa_scr, sb_scr, sc_scr):
    qb = pl.program_id(2)
    R = GQA * QB
    q = q_ref[0, 0]
    nc = vct_ref.shape[3]
    nb = ov_ref.shape[1]
    n_top = min(SLC_TOP, ks_ref.shape[2] // SLC_LEN)

    def heads_sum(x):
        out = x[:, 0:QB]
        for hq in range(1, GQA):
            out = out + x[:, hq * QB:(hq + 1) * QB]
        return out

    w0 = pl.multiple_of(qb * 8, 8)
    s_far = _dot(kc_ref[0, 0, CMP_PAD:CMP_PAD + nc, :].astype(BF16), q)
    n_io = lax.broadcasted_iota(I32, (nc, R), 0)
    s_far = jnp.where(n_io < qb * 8 - CMP_PAD, s_far, -jnp.inf)
    s_win = _dot(kc_ref[0, 0, pl.ds(w0, CMP_WIN), :].astype(BF16), q) + w_ref[0]
    j_w = lax.broadcasted_iota(I32, (CMP_WIN, R), 0)
    i_w = lax.broadcasted_iota(I32, (CMP_WIN, R), 1) & (QB - 1)
    dist_w = i_w - CMP_STRIDE * (j_w - CMP_PAD) - (CMP_LEN - 1)
    s_win = jnp.where(dist_w >= 0, jnp.where(j_w >= CMP_PAD - qb * 8, s_win, -jnp.inf), -jnp.inf)
    m = jnp.maximum(jnp.max(s_far, axis=0, keepdims=True), jnp.max(s_win, axis=0, keepdims=True))
    m = jnp.where(m == -jnp.inf, 0.0, m)
    e_far = jnp.exp2(s_far - m)
    e_win = jnp.exp2(s_win - m)
    l = jnp.sum(e_far, axis=0, keepdims=True) + jnp.sum(e_win, axis=0, keepdims=True)
    inv = 1.0 / jnp.maximum(l, 1e-30)
    p_far = e_far * inv
    p_win = e_win * inv
    vcw_t = vc_ref[0, 0, pl.ds(w0, CMP_WIN), :].T[0:V_ROWS, :]
    o_c = (_dot(vct_ref[0, 0, 0:V_ROWS, :].astype(BF16), p_far.astype(BF16))
           + _dot(vcw_t.astype(BF16), p_win.astype(BF16)))
    ov_win_t = ov_ref[pl.ds(w0, CMP_WIN), :].T
    sc_t = (_dot(ovt_ref[...], heads_sum(p_far).astype(BF16))
            + _dot(ov_win_t.astype(BF16), heads_sum(p_win).astype(BF16)))

    n_wt = WINDOW // QB + 1
    k_t, v_t, b_t = [], [], []
    for j in range(n_wt):
        kb = qb - (n_wt - 1) + j
        kb0 = jnp.maximum(kb, 0)
        k_t.append(kw_ref[0, 0, pl.ds(pl.multiple_of(kb0 * QB, QB), QB), :])
        v_t.append(vw_ref[0, 0, kb0])
        b_t.append(t_ref[0, jnp.where(kb < 0, TILE_MASKED, TILE_WINDOW_EDGE if j == 0 else n_wt - 1 - j)])
    s = _dot(jnp.concatenate(k_t, axis=0), q) + jnp.concatenate(b_t, axis=0)
    p = jnp.exp2(s - jnp.max(s, axis=0, keepdims=True))
    acc = _dot(jnp.concatenate(v_t, axis=1), p.astype(BF16))
    o_w = acc[0:HEAD_DIM, :] / acc[HEAD_DIM:HEAD_DIM + 1, :]

    chunk = FAR_CHUNK * QB

    def qk(c):
        return _dot(ks_ref[0, 0, pl.ds(pl.multiple_of(c * chunk, chunk), chunk), :], q)

    n_far = jnp.maximum(qb - (FAR_TILE_DIST - 1), 0) // FAR_CHUNK
    sa_scr[...] = qk(0)
    sc_scr[...] = qk(n_far)

    blk = lax.broadcasted_iota(I32, (nb, QB), 0)
    t_io = lax.broadcasted_iota(I32, (nb, QB), 1)
    cur = qb * (QB // SLC_LEN) + t_io // SLC_LEN
    val = jnp.where(blk <= cur, sc_t, -jnp.inf)
    for forced_blk in (0, cur, cur - 1):
        val = jnp.where(blk == forced_blk, jnp.inf, val)
    drop = jnp.full((nb, QB), NEG, F32)
    for _ in range(n_top):
        mx = jnp.max(val, axis=0, keepdims=True)
        first = jnp.min(jnp.where(val == mx, blk, nb), axis=0, keepdims=True)
        first = jnp.where(mx > -jnp.inf, first, nb)
        pick = blk == first
        drop = jnp.where(pick, 0.0, drop)
        val = jnp.where(pick, -jnp.inf, val)
    sel_scr[...] = drop

    m_scr[...] = jnp.full(m_scr.shape, NEG, F32)
    acc_scr[...] = jnp.zeros(acc_scr.shape, F32)
    blocks_per_chunk = FAR_CHUNK * QB // SLC_LEN

    def soft(c, s):
        rows = sel_scr[pl.ds(pl.multiple_of(c * blocks_per_chunk, blocks_per_chunk), blocks_per_chunk), :]
        drop = jnp.concatenate([jnp.broadcast_to(rows[r:r + 1, :], (SLC_LEN, QB)) for r in range(blocks_per_chunk)],
                               axis=0)
        s = s + jnp.concatenate([drop] * GQA, axis=1)
        m_old = m_scr[...]
        m_new = jnp.maximum(m_old, jnp.max(s, axis=0, keepdims=True))
        p = jnp.exp2(s - m_new)
        v_t = jnp.concatenate([vs_ref[0, 0, c * FAR_CHUNK + j] for j in range(FAR_CHUNK)], axis=1)
        acc_scr[...] = jnp.exp2(m_old - m_new) * acc_scr[...] + _dot(v_t, p.astype(BF16))
        m_scr[...] = m_new

    def near_bias(c):
        tiles = []
        for j in range(FAR_CHUNK):
            delta = qb - (c * FAR_CHUNK + j)
            tiles.append(t_ref[0, jnp.where(delta < 0, TILE_MASKED, delta)])
        return jnp.concatenate(tiles, axis=0)

    def chunk_pairs(first, count, bias, sa):
        last = first + jnp.maximum(count - 1, 0)

        def pair_body(j, carry):
            c = first + 2 * j
            sb_scr[...] = qk(c + 1)
            soft(c, sa[...] + bias(c) if bias else sa[...])
            sa[...] = qk(jnp.minimum(c + 2, last))
            soft(c + 1, sb_scr[...] + bias(c + 1) if bias else sb_scr[...])
            return carry

        lax.fori_loop(0, count // 2, pair_body, 0)

        @pl.when(count % 2 == 1)
        def _():
            soft(last, sa[...] + bias(last) if bias else sa[...])

    chunk_pairs(0, n_far, None, sa_scr)
    chunk_pairs(n_far, qb // FAR_CHUNK + 1 - n_far, near_bias, sc_scr)
    acc = acc_scr[...]
    o_s = acc[0:HEAD_DIM, :] / acc[HEAD_DIM:HEAD_DIM + 1, :]

    gates = g_ref[0, 0]
    outs = []
    for hq in range(GQA):
        cols = slice(hq * QB, (hq + 1) * QB)
        c = hq * N_BRANCH
        outs.append(gates[c:c + 1, :] * o_c[0:HEAD_DIM, cols] + gates[c + 1:c + 2, :] * o_s[:, cols]
                    + gates[c + 2:c + 3, :] * o_w[:, cols])
    o_ref[0] = jnp.concatenate(outs, axis=0).T


def _attn(q_t, kx, vx_t, kvc, kvc_t, gates_t, t_tab, w_tab, ov_t, ov):
    B, _, W, _ = q_t.shape
    S = kx.shape[2]
    ncp = kvc.shape[2]
    nb = ov.shape[1]
    keys = lambda j: pl.BlockSpec((1, 1, S, W), lambda b, g, t: (b, j * N_KV + g, 0, 0))
    vals = lambda j: pl.BlockSpec((1, 1, S // QB, V_ROWS, QB), lambda b, g, t: (b, j * N_KV + g, 0, 0, 0))
    cmp = lambda j: pl.BlockSpec((1, 1, ncp, W), lambda b, g, t: (b, j * N_KV + g, 0, 0))
    return pl.pallas_call(
        _attn_kernel,
        grid=(B, N_KV, S // QB),
        in_specs=[pl.BlockSpec((1, 1, W, GQA * QB), lambda b, g, t: (b, g, 0, t)),
                  keys(0), vals(0), keys(1), vals(1), cmp(0), cmp(1),
                  pl.BlockSpec((1, 1, W, ncp - CMP_WIN), lambda b, g, t: (b, N_KV + g, 0, 0)),
                  pl.BlockSpec((1, 1, GATE_ROWS, QB), lambda b, g, t: (b, g, 0, t)),
                  pl.BlockSpec((1,) + t_tab.shape[1:], lambda b, g, t: (g, 0, 0, 0)),
                  pl.BlockSpec((1, CMP_WIN, GQA * QB), lambda b, g, t: (g, 0, 0)),
                  pl.BlockSpec(ov_t.shape, lambda b, g, t: (0, 0)),
                  pl.BlockSpec(ov.shape, lambda b, g, t: (0, 0))],
        out_specs=pl.BlockSpec((1, QB, GQA * HEAD_DIM), lambda b, g, t: (b, t, g)),
        out_shape=jax.ShapeDtypeStruct((B, S, N_HEADS * HEAD_DIM), F32),
        scratch_shapes=[pltpu.VMEM((1, GQA * QB), F32), pltpu.VMEM((V_ROWS, GQA * QB), F32), pltpu.VMEM((nb, QB), F32),
                        pltpu.VMEM((FAR_CHUNK * QB, GQA * QB), F32), pltpu.VMEM((FAR_CHUNK * QB, GQA * QB), F32),
                        pltpu.VMEM((FAR_CHUNK * QB, GQA * QB), F32)],
        compiler_params=_params("parallel", "parallel", "arbitrary"),
        name="attn",
    )(q_t, kx, vx_t, kx, vx_t, kvc, kvc, kvc_t, gates_t, t_tab, w_tab, ov_t, ov)


def _rglru_kernel(xr_ref, gr_ref, cw_ref, cb_ref, wa_ref, ba_ref, wx_ref, bx_ref, lam_ref, gw_ref, o_ref,
                  xbuf, hprev, a_scr, u_scr, h_scr):
    ts = xr_ref.shape[1]
    C = xr_ref.shape[2]

    @pl.when(pl.program_id(1) == 0)
    def _():
        xbuf[0:8, :] = jnp.zeros((8, C), F32)
        hprev[...] = jnp.zeros(hprev.shape, F32)

    xbuf[8:8 + ts, :] = xr_ref[0]
    xc = cb_ref[...] + jnp.zeros((ts, C), F32)
    for j in range(CONV_WIDTH):
        xc = xc + cw_ref[j:j + 1, :] * xbuf[pl.ds(8 - (CONV_WIDTH - 1) + j, ts), :]
    xbuf[0:8, :] = xbuf[ts:ts + 8, :]

    xcb = xc.astype(BF16)
    r = _sigmoid(_dot(xcb, wa_ref[...]) + ba_ref[...])
    i = _sigmoid(_dot(xcb, wx_ref[...]) + bx_ref[...])
    z = -lam_ref[...]
    softplus = jnp.maximum(z, 0.0) + jnp.log(1.0 + jnp.exp(-jnp.abs(z)))
    log_a = -LRU_C * r * softplus
    a = jnp.exp(log_a)
    a_scr[...] = a
    u_scr[...] = jnp.sqrt(1.0 - a * a) * (i * xc)

    row = lax.broadcasted_iota(I32, (8, C), 0)

    def body(k, h):
        r0 = pl.multiple_of(k * 8, 8)
        a = a_scr[pl.ds(r0, 8), :]
        b = u_scr[pl.ds(r0, 8), :]
        for s in (1, 2, 4):
            keep = row >= s
            b = jnp.where(keep, a * pltpu.roll(b, s, 0) + b, b)
            a = jnp.where(keep, a * pltpu.roll(a, s, 0), a)
        hh = a * h + b
        h_scr[pl.ds(r0, 8), :] = hh
        return jnp.broadcast_to(hh[7:8, :], (8, C))

    hprev[...] = lax.fori_loop(0, ts // 8, body, hprev[...])
    out = h_scr[...] * _gelu(gr_ref[0])
    o_ref[0] = (_rms(out) * gw_ref[...]).astype(BF16)


def _rglru(xr, gr, cw, cb, wa_bd, ba, wx_bd, bx, lam, gw, ts=512):
    B, S, C = xr.shape
    ts = min(ts, S)
    vec = pl.BlockSpec((1, C), lambda b, t: (0, 0))
    mat = pl.BlockSpec((C, C), lambda b, t: (0, 0))
    seq = pl.BlockSpec((1, ts, C), lambda b, t: (b, t, 0))
    return pl.pallas_call(
        _rglru_kernel,
        grid=(B, S // ts),
        in_specs=[seq, seq, pl.BlockSpec((CONV_WIDTH, C), lambda b, t: (0, 0)), vec, mat, vec, mat, vec, vec, vec],
        out_specs=seq,
        out_shape=jax.ShapeDtypeStruct((B, S, C), BF16),
        scratch_shapes=[pltpu.VMEM((ts + 8, C), F32), pltpu.VMEM((8, C), F32), pltpu.VMEM((ts, C), F32),
                        pltpu.VMEM((ts, C), F32), pltpu.VMEM((ts, C), F32)],
        compiler_params=_params("parallel", "arbitrary"),
        name="rglru",
    )(xr, gr, cw, cb, wa_bd, ba, wx_bd, bx, lam, gw)


def _to_row_tiles(ref, x):
    rows = x.shape[0]
    for s in range(ROW_TILE):
        ref[pl.ds(s, rows, stride=ROW_TILE), :] = x[:, s * LANES:(s + 1) * LANES]


def _from_row_tiles(ref, rows):
    return jnp.concatenate([ref[pl.ds(s, rows, stride=ROW_TILE), :] for s in range(ROW_TILE)], axis=1)


def _outproj_kernel(oa_ref, yr_ref, x_ref, mod_ref, gaw_ref, wo_ref, n2_ref, wr_ref, x1_ref, h2_ref, st_ref):
    da = oa_ref.shape[2]
    ya = (_rms(oa_ref[0]) * gaw_ref[...]).astype(BF16)
    mix = _dot(ya, wo_ref[0:da, :]) + _dot(yr_ref[0], wo_ref[da:, :])
    x1 = x_ref[0] + mod_ref[0, 2:3, :] * mix
    x1_ref[0] = x1
    h2 = (_rms(x1) * n2_ref[...]) * (1.0 + mod_ref[0, 4:5, :]) + mod_ref[0, 3:4, :]
    h_hi = h2.astype(BF16)
    h2_ref[0] = h_hi
    h_lo = (h2 - h_hi.astype(F32)).astype(BF16)
    logits = _dot_nt(wr_ref[0], h_hi) + (_dot_nt(wr_ref[1], h_hi) + _dot_nt(wr_ref[0], h_lo))
    st_ref[...] = _sigmoid(logits)


def _outproj(oa, yr, x, mod, gaw, wo, n2, wr_t, tm=1024):
    B, S, D = x.shape
    E = wr_t.shape[1]
    nt = S // tm
    row = lambda w: pl.BlockSpec((1, tm, w), lambda b, t: (b, t, 0))
    return pl.pallas_call(
        _outproj_kernel,
        grid=(B, nt),
        in_specs=[row(oa.shape[2]), row(yr.shape[2]), row(D),
                  pl.BlockSpec((1, 6, D), lambda b, t: (b, 0, 0)),
                  pl.BlockSpec((1, oa.shape[2]), lambda b, t: (0, 0)),
                  pl.BlockSpec(wo.shape, lambda b, t: (0, 0)),
                  pl.BlockSpec((1, D), lambda b, t: (0, 0)),
                  pl.BlockSpec((2, E, D), lambda b, t: (0, 0, 0))],
        out_specs=[row(D), row(D), pl.BlockSpec((E, tm), lambda b, t: (0, b * nt + t))],
        out_shape=[jax.ShapeDtypeStruct((B, S, D), F32), jax.ShapeDtypeStruct((B, S, D), BF16),
                   jax.ShapeDtypeStruct((E, B * S), F32)],
        compiler_params=_params("parallel", "parallel"),
        name="outproj",
    )(oa, yr, x, mod, gaw, wo, n2, wr_t)


def _route_kernel(s_ref, b_ref, wgt_ref, pos_ref, tcnt_ref):
    E, tn = s_ref.shape
    per = E // N_EXPERT_GROUPS
    s = s_ref[...]
    s_sel = s + b_ref[...]
    eidx = lax.broadcasted_iota(I32, (E, tn), 0)

    grp = []
    for gi in range(N_EXPERT_GROUPS):
        xg = s_sel[gi * per:(gi + 1) * per, :]
        ig = lax.broadcasted_iota(I32, (per, tn), 0)
        m1 = jnp.max(xg, axis=0, keepdims=True)
        f1 = jnp.min(jnp.where(xg == m1, ig, per), axis=0, keepdims=True)
        m2 = jnp.max(jnp.where(ig == f1, -jnp.inf, xg), axis=0, keepdims=True)
        grp.append(m1 + m2)
    val = []
    for gi in range(N_EXPERT_GROUPS):
        rank = jnp.zeros((1, tn), I32)
        for gj in range(N_EXPERT_GROUPS):
            if gj == gi:
                continue
            ahead = (grp[gj] > grp[gi]) | ((grp[gj] == grp[gi]) & (gj < gi))
            rank = rank + ahead.astype(I32)
        val.append(jnp.where(rank < TOP_GROUPS, s_sel[gi * per:(gi + 1) * per, :], -jnp.inf))
    val = jnp.concatenate(val, axis=0)

    eids, tops = [], []
    hot = jnp.zeros((E, tn), F32)
    for _ in range(TOP_K):
        mx = jnp.max(val, axis=0, keepdims=True)
        first = jnp.min(jnp.where(val == mx, eidx, E), axis=0, keepdims=True)
        pick = eidx == first
        eids.append(first)
        tops.append(jnp.sum(jnp.where(pick, s, 0.0), axis=0, keepdims=True))
        hot = jnp.where(pick, 1.0, hot)
        val = jnp.where(pick, -jnp.inf, val)
    denom = tops[0]
    for t in tops[1:]:
        denom = denom + t

    ti = lax.broadcasted_iota(I32, (tn, tn), 0)
    tj = lax.broadcasted_iota(I32, (tn, tn), 1)
    upper = jnp.where(ti < tj, 1.0, 0.0).astype(BF16)
    before = _dot(hot.astype(BF16), upper)
    tile_count = jnp.sum(hot, axis=1, keepdims=True)
    padded = tile_count + (tile_count - 2.0 * jnp.floor(tile_count * 0.5))
    ei = lax.broadcasted_iota(I32, (E, E), 0)
    ej = lax.broadcasted_iota(I32, (E, E), 1)
    earlier_experts = jnp.where(ej < ei, 1.0, 0.0).astype(BF16)
    run_start = _dot(earlier_experts, jnp.broadcast_to(padded, (E, LANES)).astype(BF16))[:, 0:1]
    for k in range(TOP_K):
        wgt_ref[k:k + 1, :] = ROUTED_SCALE * tops[k] / denom
        pos_ref[k:k + 1, :] = jnp.sum(jnp.where(eidx == eids[k], before + run_start, 0.0), axis=0,
                                      keepdims=True).astype(I32)
    tcnt_ref[0] = tile_count


def _route(s_t, b_col, tn):
    E, N = s_t.shape
    out = pl.BlockSpec((TOP_K, tn), lambda t: (0, t))
    return pl.pallas_call(
        _route_kernel,
        grid=(N // tn,),
        in_specs=[pl.BlockSpec((E, tn), lambda t: (0, t)), pl.BlockSpec((E, 1), lambda t: (0, 0))],
        out_specs=[out, out, pl.BlockSpec((1, E, 1), lambda t: (t, 0, 0))],
        out_shape=[jax.ShapeDtypeStruct((TOP_K, N), F32), jax.ShapeDtypeStruct((TOP_K, N), I32),
                   jax.ShapeDtypeStruct((N // tn, E, 1), F32)],
        compiler_params=_params("parallel"),
        name="route",
    )(s_t, b_col)


_HI = -65536


def _pack_pairs(even_rows, odd_rows):
    hi = lax.bitcast_convert_type(even_rows, I32) & _HI
    lo = lax.shift_right_logical(lax.bitcast_convert_type(odd_rows, I32), 16)
    return hi | lo


def _unpack_pairs(words):
    even_rows = lax.bitcast_convert_type(words & _HI, F32)
    odd_rows = lax.bitcast_convert_type(lax.shift_left(words, 16), F32)
    return even_rows, odd_rows


def _run_copies(src_ref, cnt_ref, dst_ref, n_runs, copy):
    def go(e, carry):
        n = cnt_ref[0, 0, e]

        @pl.when(n > 0)
        def _():
            size = pl.multiple_of(n * ROW_TILE, ROW_TILE)
            copy(pl.ds(pl.multiple_of(src_ref[0, 0, e] * ROW_TILE, ROW_TILE), size),
                 pl.ds(pl.multiple_of(dst_ref[0, 0, e] * ROW_TILE, ROW_TILE), size)).start()
        return carry
    lax.fori_loop(0, n_runs, go, 0, unroll=8)


def _pair_onehots(pos_rows, n_pairs, first_pair, width):
    even = 2 * (lax.broadcasted_iota(I32, (n_pairs, width), 0) + first_pair)
    hot_e = jnp.zeros((n_pairs, width), F32)
    hot_o = jnp.zeros((n_pairs, width), F32)
    for k in range(TOP_K):
        hot_e = jnp.where(even == pos_rows[k:k + 1, :], 1.0, hot_e)
        hot_o = jnp.where(even + 1 == pos_rows[k:k + 1, :], 1.0, hot_o)
    return hot_e, hot_o


def _dispatch_kernel(src_ref, cnt_ref, dst_ref, pos_ref, h_ref, xs_hbm, xbuf, sem):
    t = pl.program_id(0)
    slot = t % 2
    buf = xbuf.at[slot]
    n_pairs = xbuf.shape[1] // ROW_TILE
    tn = h_ref.shape[0]

    def wait_copies(s):
        pltpu.make_async_copy(xbuf.at[s], xs_hbm.at[pl.ds(0, xbuf.shape[1])], sem.at[s]).wait()

    @pl.when(t >= 2)
    def _():
        wait_copies(slot)

    hb = h_ref[...]
    slab = n_pairs // 3
    for sl in range(3):
        hot_e, hot_o = _pair_onehots(pos_ref[...], slab, sl * slab, tn)
        words = _pack_pairs(_dot(hot_e.astype(BF16), hb), _dot(hot_o.astype(BF16), hb))
        for s in range(ROW_TILE):
            buf[pl.ds(sl * slab * ROW_TILE + s, slab, stride=ROW_TILE), :] = words[:, s * LANES:(s + 1) * LANES]
    _run_copies(src_ref, cnt_ref, dst_ref, cnt_ref.shape[2],
                lambda src, dst: pltpu.make_async_copy(buf.at[src], xs_hbm.at[dst], sem.at[slot]))

    @pl.when(t == pl.num_programs(0) - 1)
    def _():
        @pl.when(t >= 1)
        def _():
            wait_copies(1 - slot)
        wait_copies(slot)


def _run_specs(runs):
    return [pl.BlockSpec((1, 1, r.shape[2]), lambda t: (t, 0, 0), memory_space=pltpu.SMEM) for r in runs]


def _tile_pairs(tn, n_experts):
    return (TOP_K * tn + n_experts) // 2


def _dispatch(runs, pos_t, h2, tn):
    N, D = h2.shape
    W = D // ROW_TILE
    n_pairs = _tile_pairs(tn, runs[0].shape[2] - 1)
    assert n_pairs % 24 == 0
    return pl.pallas_call(
        _dispatch_kernel,
        grid=(N // tn,),
        in_specs=_run_specs(runs) + [pl.BlockSpec((TOP_K, tn), lambda t: (0, t)),
                                     pl.BlockSpec((tn, D), lambda t: (t, 0))],
        out_specs=pl.BlockSpec(memory_space=pl.ANY),
        out_shape=jax.ShapeDtypeStruct((N // tn * n_pairs * ROW_TILE, W), I32),
        scratch_shapes=[pltpu.VMEM((2, n_pairs * ROW_TILE, W), I32), pltpu.SemaphoreType.DMA((2,))],
        compiler_params=_params("arbitrary"),
        name="dispatch",
    )(*runs, pos_t, h2)


def _experts_kernel(first_ref, size_ref, e_ref, new_ref, next_ref, wslot_ref, total_ref,
                    w1_hbm, w3_hbm, w2_hbm, xs_hbm, ys_hbm,
                    xbuf, ybuf, w1f, w3f, w2f, w1b, w3b, w2b, xsem, ysem, wsem):
    total = total_ref[0]

    def w_copies(e, slot):
        return [pltpu.make_async_copy(src.at[e], dst.at[slot], wsem.at[slot])
                for src, dst in ((w1_hbm, w1f), (w3_hbm, w3f), (w2_hbm, w2f))]

    def rows(c):
        return (pl.multiple_of(first_ref[c] * ROW_TILE, ROW_TILE), pl.multiple_of(size_ref[c] * ROW_TILE, ROW_TILE))

    def x_copy(c, slot):
        first, size = rows(c)
        return pltpu.make_async_copy(xs_hbm.at[pl.ds(first, size)], xbuf.at[slot, pl.ds(0, size)], xsem.at[slot])

    def y_copy(c, slot):
        first, size = rows(c)
        return pltpu.make_async_copy(ybuf.at[slot, pl.ds(0, size)], ys_hbm.at[pl.ds(first, size)], ysem.at[slot])

    xbuf[...] = jnp.zeros(xbuf.shape, I32)
    for cp in w_copies(e_ref[0], 0):
        cp.start()
    x_copy(0, 0).start()
    half = MOE_BLOCK // 2

    def body(c, carry):
        slot = c % 2

        @pl.when(new_ref[c] == 1)
        def _():
            ws = wslot_ref[c]
            for cp in w_copies(e_ref[c], ws):
                cp.wait()
            w1b[...] = w1f[ws].astype(BF16)
            w3b[...] = w3f[ws].astype(BF16)
            w2b[...] = w2f[ws].astype(BF16)

            @pl.when(next_ref[c] >= 0)
            def _():
                for cp in w_copies(next_ref[c], 1 - ws):
                    cp.start()

        x_copy(c, slot).wait()

        @pl.when(c + 1 < total)
        def _():
            x_copy(c + 1, 1 - slot).start()

        x = jnp.concatenate(_unpack_pairs(_from_row_tiles(xbuf.at[slot], half)), axis=0).astype(BF16)
        h1 = _dot(x, w1b[...])
        h3 = _dot(x, w3b[...])
        hid = (h1 * _sigmoid(h1)) * h3
        y = _dot(hid.astype(BF16), w2b[...]).astype(BF16).astype(F32)

        @pl.when(c >= 2)
        def _():
            y_copy(c - 2, slot).wait()

        _to_row_tiles(ybuf.at[slot], _pack_pairs(y[0:half], y[half:]))
        y_copy(c, slot).start()
        return carry

    lax.fori_loop(0, total, body, 0)

    @pl.when(total >= 2)
    def _():
        y_copy(total - 2, total % 2).wait()

    y_copy(total - 1, (total - 1) % 2).wait()

    ybuf[0] = jnp.zeros(ybuf.shape[1:], I32)
    tail_first = total_ref[1]
    tail_pairs = ys_hbm.shape[0] // ROW_TILE - tail_first

    def fill(j, carry):
        size = pl.multiple_of(jnp.minimum(half, tail_pairs - j * half) * ROW_TILE, ROW_TILE)
        cp = pltpu.make_async_copy(ybuf.at[0, pl.ds(0, size)],
                                   ys_hbm.at[pl.ds(pl.multiple_of((tail_first + j * half) * ROW_TILE, ROW_TILE), size)],
                                   ysem.at[0])
        cp.start()
        cp.wait()
        return carry

    lax.fori_loop(0, (tail_pairs + half - 1) // half, fill, 0)


def _expert_chunks(counts, n_pairs_total):
    E = counts.shape[0]
    block = MOE_BLOCK // 2
    n_max = n_pairs_total // block + E
    starts = jnp.cumsum(counts) - counts
    n_chunks = (counts + block - 1) // block
    c_end = jnp.cumsum(n_chunks)
    c = jnp.arange(n_max, dtype=I32)
    e = jnp.minimum(jnp.sum(c_end[None, :] <= c[:, None], axis=1), E - 1).astype(I32)
    j = c - (c_end - n_chunks)[e]
    size = jnp.clip(counts[e] - j * block, 0, block)
    has_rows = counts > 0
    ids = jnp.where(has_rows, jnp.arange(E, dtype=I32), E)
    later = jnp.flip(lax.cummin(jnp.flip(ids)))
    nxt = jnp.concatenate([later[1:], jnp.full((1,), E, I32)])
    nxt = jnp.where(nxt >= E, -1, nxt)
    ordinal = jnp.cumsum(has_rows.astype(I32)) - 1
    to_i32 = lambda a: a.astype(I32)
    meta = jnp.stack([c_end[-1], jnp.sum(counts)])
    return tuple(map(to_i32, (starts[e] + j * block, size, e, j == 0, nxt[e], ordinal[e] % 2, meta)))


def _experts(chunks, xs, w1, w3, w2):
    rows_total, W = xs.shape
    _, D, De = w1.shape
    any_space = pl.BlockSpec(memory_space=pl.ANY)
    block_rows = MOE_BLOCK // 2 * ROW_TILE
    grid_spec = pltpu.PrefetchScalarGridSpec(
        num_scalar_prefetch=len(chunks),
        grid=(1,),
        in_specs=[any_space] * 4,
        out_specs=any_space,
        scratch_shapes=[pltpu.VMEM((2, block_rows, W), I32), pltpu.VMEM((2, block_rows, W), I32),
                        pltpu.VMEM((2, D, De), F32), pltpu.VMEM((2, D, De), F32), pltpu.VMEM((2, De, D), F32),
                        pltpu.VMEM((D, De), BF16), pltpu.VMEM((D, De), BF16), pltpu.VMEM((De, D), BF16),
                        pltpu.SemaphoreType.DMA((2,)), pltpu.SemaphoreType.DMA((2,)), pltpu.SemaphoreType.DMA((2,))],
    )
    return pl.pallas_call(
        _experts_kernel,
        grid_spec=grid_spec,
        out_shape=jax.ShapeDtypeStruct((rows_total, W), I32),
        compiler_params=_params("arbitrary"),
        name="experts",
    )(*chunks, w1, w3, w2, xs)


def _combine_kernel(src_ref, cnt_ref, dst_ref, ys_hbm, post_ref, wt_ref, h2_ref, x1_ref, mod_ref, ws1_ref,
                    ws3_ref, ws2_ref, fw_ref, o_ref, ybuf, sem):
    tm = h2_ref.shape[0]
    n_pairs = ybuf.shape[0] // ROW_TILE
    n_exp = cnt_ref.shape[2] - 1

    @pl.when(pl.program_id(0) == 0)
    def _():
        ybuf[...] = jnp.zeros(ybuf.shape, I32)

    _run_copies(src_ref, cnt_ref, dst_ref, n_exp,
                lambda src, dst: pltpu.make_async_copy(ys_hbm.at[dst], ybuf.at[src], sem.at[0]))

    h = h2_ref[...]
    a = _dot(h, ws1_ref[...])
    hid = (a * _sigmoid(a)) * _dot(h, ws3_ref[...])
    shared = _dot(hid.astype(BF16), ws2_ref[...])

    pair = 2 * lax.broadcasted_iota(I32, (n_pairs, tm), 0)
    w_cols, mine = [], []
    for odd in range(2):
        w_rows = jnp.zeros((n_pairs, tm), F32)
        for k in range(TOP_K):
            w_rows = jnp.where(pair + odd == post_ref[k:k + 1, :], wt_ref[k:k + 1, :], w_rows)
        w_cols.append(jnp.sum(w_rows, axis=1, keepdims=True))
        mine.append(jnp.where(w_rows != 0.0, 1.0, 0.0).astype(BF16))
    used = pl.multiple_of((src_ref[0, 0, n_exp]) * ROW_TILE, ROW_TILE)
    pltpu.make_async_copy(ys_hbm.at[pl.ds(0, used)], ybuf.at[pl.ds(0, used)], sem.at[0]).wait()
    y_even, y_odd = _unpack_pairs(_from_row_tiles(ybuf, n_pairs))
    sum_rows = lambda hot, y: lax.dot_general(hot, y, (((0,), (0,)), ((), ())), preferred_element_type=F32)
    routed = (sum_rows(mine[0], (y_even * w_cols[0]).astype(BF16))
              + sum_rows(mine[1], (y_odd * w_cols[1]).astype(BF16)))
    x2 = x1_ref[...] + mod_ref[0, 5:6, :] * (routed + shared)
    o_ref[...] = _rms(x2) * fw_ref[...]


def _combine(runs, ys, pos_t, wgt_t, h2, x1, mod, ws1, ws3, ws2, fw, seq, tm):
    N, D = h2.shape
    nt = seq // tm
    row = pl.BlockSpec((tm, D), lambda t: (t, 0))
    full = lambda a: pl.BlockSpec(a.shape, lambda t: (0,) * a.ndim)
    return pl.pallas_call(
        _combine_kernel,
        grid=(N // tm,),
        in_specs=_run_specs(runs) + [pl.BlockSpec(memory_space=pl.ANY),
                                     pl.BlockSpec((TOP_K, tm), lambda t: (0, t)),
                                     pl.BlockSpec((TOP_K, tm), lambda t: (0, t)),
                                     row, row,
                                     pl.BlockSpec((1, 6, D), lambda t: (t // nt, 0, 0)),
                                     full(ws1), full(ws3), full(ws2), full(fw)],
        out_specs=row,
        out_shape=jax.ShapeDtypeStruct((N, D), F32),
        scratch_shapes=[pltpu.VMEM((_tile_pairs(tm, runs[0].shape[2] - 1) * ROW_TILE, D // ROW_TILE), I32),
                        pltpu.SemaphoreType.DMA((1,))],
        compiler_params=_params("arbitrary"),
        name="combine",
    )(*runs, ys, pos_t, wgt_t, h2, x1, mod, ws1, ws3, ws2, fw)


def _t5_bucket(dist):
    n = jnp.maximum(dist, 0)
    max_exact = REL_BUCKETS // 2
    nf = jnp.maximum(n, 1).astype(F32)
    large = max_exact + (jnp.log(nf / max_exact) / math.log(REL_MAX_DIST / max_exact)
                         * (REL_BUCKETS - max_exact)).astype(I32)
    large = jnp.minimum(large, REL_BUCKETS - 1)
    return jnp.where(n < max_exact, n, large)


def _bias_kernel(tbl_ref, far_ref, bt_ref, bw_ref, t_ref, w_ref):
    g = pl.program_id(0)
    d = pl.program_id(1)

    def lookup(bkt, head):
        out = jnp.full(bkt.shape, tbl_ref[0, head], F32)
        for b in range(1, REL_BUCKETS):
            out = jnp.where(bkt == b, tbl_ref[b, head], out)
        return out - far_ref[head]

    j = lax.broadcasted_iota(I32, (QB, QB), 0)
    i = lax.broadcasted_iota(I32, (QB, QB), 1)
    keep = ((d != 0) | (i >= j)) & (d != TILE_MASKED) & ((d != TILE_WINDOW_EDGE) | (j > i))
    for hq in range(GQA):
        t_ref[0, 0, :, hq * QB:(hq + 1) * QB] = jnp.where(keep, lookup(bt_ref[0], GQA * g + hq), NEG)

    @pl.when(d == 0)
    def _():
        for hq in range(GQA):
            w_ref[0, :, hq * QB:(hq + 1) * QB] = lookup(bw_ref[...], GQA * g + hq)


def _bias_tables(rel_table):
    tbl = rel_table.astype(F32) * LOG2E
    far_hi = tbl[REL_BUCKETS - 1].astype(BF16)
    far_lo = (tbl[REL_BUCKETS - 1] - far_hi.astype(F32)).astype(BF16)
    far = far_hi.astype(F32) + far_lo.astype(F32)
    q_tail = jnp.zeros((N_HEADS, HEAD_DIM), F32).at[:, 0].set(far_hi.astype(F32)).at[:, 1].set(far_lo.astype(F32))
    i = jnp.arange(QB)
    tile_dist = np.array(list(range(N_BIAS_TILES)) + [0, WINDOW // QB])
    d_t = (QB * jnp.asarray(tile_dist)[:, None, None] + i[None, None, :] - i[None, :, None])
    d_w = i[None, :] - CMP_STRIDE * (jnp.arange(CMP_WIN)[:, None] - CMP_PAD) - (CMP_LEN - 1)
    n_tiles = len(tile_dist)
    t_tab, w_tab = pl.pallas_call(
        _bias_kernel,
        grid=(N_KV, n_tiles),
        in_specs=[pl.BlockSpec(memory_space=pltpu.SMEM),
                  pl.BlockSpec(memory_space=pltpu.SMEM),
                  pl.BlockSpec((1, QB, QB), lambda g, d: (d, 0, 0)),
                  pl.BlockSpec((CMP_WIN, QB), lambda g, d: (0, 0))],
        out_specs=[pl.BlockSpec((1, 1, QB, GQA * QB), lambda g, d: (g, d, 0, 0)),
                   pl.BlockSpec((1, CMP_WIN, GQA * QB), lambda g, d: (g, 0, 0))],
        out_shape=[jax.ShapeDtypeStruct((N_KV, n_tiles, QB, GQA * QB), F32),
                   jax.ShapeDtypeStruct((N_KV, CMP_WIN, GQA * QB), F32)],
        compiler_params=_params("parallel", "arbitrary"),
        name="bias",
    )(tbl, far, _t5_bucket(d_t), _t5_bucket(d_w))
    far_min = min(FAR_TILE_DIST * QB - QB + 1, CMP_STRIDE * (CMP_PAD + 1) - (CMP_LEN - 1))
    max_exact = REL_BUCKETS // 2
    assert math.log(far_min / max_exact) / math.log(REL_MAX_DIST / max_exact) * max_exact > max_exact - 0.75
    return t_tab, w_tab, q_tail


def _overlap_tables(nc, nb):
    c_start = CMP_STRIDE * np.arange(nc)[:, None]
    s_start = SLC_LEN * np.arange(nb)[None, :]
    ov = ((c_start < s_start + SLC_LEN) & (c_start + CMP_LEN > s_start)).astype(np.float32)
    ov[nc - 1:] = 0.0
    out = np.zeros((nc + CMP_WIN, nb), np.float32)
    out[CMP_PAD:CMP_PAD + nc] = ov
    return jnp.asarray(ov.T, dtype=BF16), jnp.asarray(out)


def _block_diag(w):
    nblk, bs, _ = w.shape
    eye = jnp.eye(nblk, dtype=w.dtype)
    return (eye[:, None, :, None] * w[:, :, None, :]).reshape(nblk * bs, nblk * bs)


def _layer(x, c8, w_ada, b_ada, norm1_w, w_in, cmp_pos, cmp_k_w1, cmp_k_w2, cmp_v_w1, cmp_v_w2, rel_table,
           conv_w, conv_b, lru_wa, lru_ba, lru_wx, lru_bx, lru_lambda, gnorm_attn_w, gnorm_rnn_w, w_out,
           norm2_w, w_router, b_router, w1, w3, w2, ws1, ws3, ws2, final_norm_w):
    B, S, D = x.shape
    N = B * S
    E = w_router.shape[1]
    d_attn = N_HEADS * HEAD_DIM
    d_rnn = D - d_attn
    assert S % (FAR_CHUNK * QB) == 0 and S // SLC_LEN <= MAX_SLC_BLOCKS

    mod = _ada(c8, w_ada, b_ada[None, :])[:B].reshape(B, 6, D)

    n_kv_cols = 6 * N_KV * HEAD_DIM
    n_gate = N_BRANCH * N_HEADS
    wq, wkv, wg, wxr, wgr = jnp.split(w_in, np.cumsum([d_attn, n_kv_cols, n_gate, d_rnn]).tolist(), axis=1)
    wg = jnp.pad(wg.reshape(D, N_KV, GQA * N_BRANCH), ((0, 0), (0, 0), (0, LANES - GQA * N_BRANCH)))
    wg = wg.reshape(D, N_KV * LANES)
    w_pad = jnp.concatenate([wq, wkv, wg, wxr, wgr], axis=1).astype(BF16)
    t_tab, w_tab, q_tail = _bias_tables(rel_table)
    q_t, kx, vx_t, kvc, gates_t, xr, gr = _inproj(x, mod, norm1_w[None, :], w_pad, q_tail)

    nc = S // CMP_STRIDE
    pos2 = cmp_pos.reshape(2, CMP_STRIDE * HEAD_DIM)
    kvc_pad, kvc_t = _compress(kvc, pos2, jnp.stack([cmp_k_w1, cmp_v_w1]), jnp.stack([cmp_k_w2, cmp_v_w2]))
    o_attn = _attn(q_t, kx, vx_t, kvc_pad, kvc_t, gates_t, t_tab, w_tab, *_overlap_tables(nc, MAX_SLC_BLOCKS))

    y_rnn = _rglru(xr, gr, conv_w.reshape(CONV_WIDTH, d_rnn), conv_b[None, :], _block_diag(lru_wa).astype(BF16),
                   lru_ba[None, :], _block_diag(lru_wx).astype(BF16), lru_bx[None, :], lru_lambda[None, :],
                   gnorm_rnn_w[None, :])

    wr_hi = w_router.T.astype(BF16)
    wr_lo = (w_router.T - wr_hi.astype(F32)).astype(BF16)
    x1, h2, s_t = _outproj(o_attn, y_rnn, x, mod, gnorm_attn_w[None, :], w_out.astype(BF16), norm2_w[None, :],
                           jnp.stack([wr_hi, wr_lo]))

    tile = min(MOE_TILE, S)
    wgt_t, pos_t, tile_counts = _route(s_t, b_router[:, None], tile)
    tcnt = tile_counts[:, :, 0].astype(I32)
    n_tiles = tcnt.shape[0]
    tile_pairs = _tile_pairs(tile, E)
    rpairs = (tcnt + 1) // 2
    rpairs = jnp.concatenate([rpairs, tile_pairs - jnp.sum(rpairs, axis=1, keepdims=True)], axis=1)
    per_expert = jnp.sum(rpairs, axis=0)
    off = jnp.cumsum(per_expert) - per_expert
    run_src = jnp.cumsum(rpairs, axis=1) - rpairs
    run_dst = off[None, :] + jnp.cumsum(rpairs, axis=0) - rpairs
    runs = tuple(r.astype(I32)[:, None, :] for r in (run_src, rpairs, run_dst))

    h2f = h2.reshape(N, D)
    xs = _dispatch(runs, pos_t, h2f, tile)
    ys = _experts(_expert_chunks(per_expert[:E], n_tiles * tile_pairs), xs, w1, w3, w2)
    out = _combine(runs, ys, pos_t, wgt_t, h2f, x1.reshape(N, D), mod, ws1.astype(BF16), ws3.astype(BF16),
                   ws2.astype(BF16), final_norm_w[None, :], S, tile)
    return out.reshape(B, S, D)


def kernel(x, c, w_ada, b_ada, norm1_w, w_in, cmp_pos, cmp_k_w1, cmp_k_w2, cmp_v_w1, cmp_v_w2, rel_table, conv_w, conv_b, lru_wa, lru_ba, lru_wx, lru_bx, lru_lambda, gnorm_attn_w, gnorm_rnn_w, w_out, norm2_w, w_router, b_router, w1, w3, w2, ws1, ws3, ws2, final_norm_w):
    assert w_ada.shape[0] == 1
    c8 = jnp.pad(c, ((0, 8 - c.shape[0]), (0, 0)))
    return _layer(x, c8, w_ada[0], b_ada[0], norm1_w[0], w_in[0], cmp_pos[0], cmp_k_w1[0], cmp_k_w2[0],
                  cmp_v_w1[0], cmp_v_w2[0], rel_table, conv_w[0], conv_b[0], lru_wa[0], lru_ba[0], lru_wx[0],
                  lru_bx[0], lru_lambda[0], gnorm_attn_w[0], gnorm_rnn_w[0], w_out[0], norm2_w[0], w_router[0],
                  b_router[0], w1[0], w3[0], w2[0], ws1[0], ws3[0], ws2[0], final_norm_w)
```

```python
import math

import jax
import jax.numpy as jnp
import numpy as np
from jax import lax
from jax.experimental import pallas as pl
from jax.experimental.pallas import tpu as pltpu

F32 = jnp.float32
BF16 = jnp.bfloat16
I32 = jnp.int32

HEAD_DIM = 64
N_HEADS = 8
N_KV = 2
GQA = N_HEADS // N_KV
N_BRANCH = 3
CONV_WIDTH = 4
LRU_C = 8.0
CMP_LEN = 32
CMP_STRIDE = 16
SLC_LEN = 64
SLC_TOP = 16
WINDOW = 512
LANES = 128
QB = 128
Q_PER_STEP = 2
MAX_SLC_BLOCKS = 128
GATE_ROWS = 16
V_ROWS = 80
REL_BUCKETS = 32
REL_MAX_DIST = 1024
N_EXPERT_GROUPS = 8
TOP_GROUPS = 4
TOP_K = 8
ROUTED_SCALE = 2.5
MOE_BLOCK = 576
MOE_TILE = 256
ROW_TILE = 8
EPS = 1e-6
NEG = -1e30
LOG2E = math.log2(math.e)
CMP_PAD = 120
CMP_WIN = 128
N_BIAS_TILES = 11
TILE_MASKED = N_BIAS_TILES
TILE_WINDOW_EDGE = N_BIAS_TILES + 1
FAR_TILE_DIST = 8
FAR_CHUNK = 4
VMEM_LIMIT = 52 * 1024 * 1024


def _dot(a, b, **kw):
    return jnp.dot(a, b, preferred_element_type=F32, **kw)


def _dot_nt(a, b, **kw):
    return lax.dot_general(a, b, (((1,), (1,)), ((), ())), preferred_element_type=F32, **kw)


def _gelu(x):
    return 0.5 * x * (1.0 + jnp.tanh(math.sqrt(2.0 / math.pi) * (x + 0.044715 * (x * x * x))))


def _sigmoid(x):
    return 1.0 / (1.0 + jnp.exp(-x))


def _rms(x):
    return x * lax.rsqrt(jnp.mean(x * x, axis=-1, keepdims=True) + EPS)


def _params(*sem):
    return pltpu.CompilerParams(dimension_semantics=sem, vmem_limit_bytes=VMEM_LIMIT)


def _ada_kernel(c_ref, w_ref, b_ref, o_ref):
    c = c_ref[...]
    a = c * _sigmoid(c)
    o_ref[...] = _dot(a, w_ref[...], precision=lax.Precision.HIGHEST) + b_ref[...]


def _ada(c8, w, b):
    d, n = w.shape
    tn = 1536
    return pl.pallas_call(
        _ada_kernel,
        grid=(n // tn,),
        in_specs=[pl.BlockSpec((8, d), lambda j: (0, 0)),
                  pl.BlockSpec((d, tn), lambda j: (0, j)),
                  pl.BlockSpec((1, tn), lambda j: (0, j))],
        out_specs=pl.BlockSpec((8, tn), lambda j: (0, j)),
        out_shape=jax.ShapeDtypeStruct((8, n), F32),
        compiler_params=_params("parallel"),
        name="ada",
    )(c8, w, b)


def _inproj_kernel(x_ref, mod_ref, nw_ref, w_ref, qt_ref, q_ref, kx_ref, vx_ref, kvc_ref, g_ref, xr_ref, gr_ref):
    h = _rms(x_ref[0]) * nw_ref[...]
    h = h * (1.0 + mod_ref[0, 1:2, :]) + mod_ref[0, 0:1, :]
    p = _dot(h.astype(BF16), w_ref[...])
    tm = p.shape[0]
    dq = N_HEADS * HEAD_DIM
    lane = lax.broadcasted_iota(I32, (tm, HEAD_DIM), 1)
    k_tail = jnp.where(lane < 2, 1.0, 0.0)
    v_tail = jnp.where(lane < 1, 1.0, 0.0)
    for hh in range(N_HEADS):
        qh = p[:, hh * HEAD_DIM:(hh + 1) * HEAD_DIM] * (HEAD_DIM ** -0.5 * LOG2E)
        q_tail = jnp.broadcast_to(qt_ref[hh:hh + 1, :], (tm, HEAD_DIM))
        q_t = jnp.concatenate([qh, q_tail], axis=1).T.astype(BF16)
        for blk in range(tm // QB):
            col = (blk * GQA + hh % GQA) * QB
            q_ref[0, hh // GQA, :, col:col + QB] = q_t[:, blk * QB:(blk + 1) * QB]
    for j in range(6):
        for g in range(N_KV):
            col = dq + (j * N_KV + g) * HEAD_DIM
            piece = p[:, col:col + HEAD_DIM]
            if j < 2:
                kvc_ref[0, j * N_KV + g] = piece
            elif j % 2 == 0:
                kx_ref[0, (j // 2 - 1) * N_KV + g] = jnp.concatenate([piece, k_tail], axis=1).astype(BF16)
            else:
                v_t = jnp.concatenate([piece, v_tail], axis=1).T.astype(BF16)
                for blk in range(tm // QB):
                    vx_ref[0, (j // 2 - 1) * N_KV + g, blk] = v_t[0:V_ROWS, blk * QB:(blk + 1) * QB]
    c0 = dq + 6 * N_KV * HEAD_DIM
    for g in range(N_KV):
        g_ref[0, g] = _sigmoid(p[:, c0 + g * LANES:c0 + (g + 1) * LANES]).T[0:GATE_ROWS, :]
    c1 = c0 + N_KV * LANES
    d_rnn = xr_ref.shape[2]
    xr_ref[0] = p[:, c1:c1 + d_rnn]
    gr_ref[0] = p[:, c1 + d_rnn:c1 + 2 * d_rnn]


def _inproj(x, mod, nw, w_pad, q_tail, tm=1024):
    B, S, D = x.shape
    ncol = w_pad.shape[1]
    heads = lambda n, w: pl.BlockSpec((1, n, tm, w), lambda b, t: (b, 0, t, 0))
    return pl.pallas_call(
        _inproj_kernel,
        grid=(B, S // tm),
        in_specs=[pl.BlockSpec((1, tm, D), lambda b, t: (b, t, 0)),
                  pl.BlockSpec((1, 6, D), lambda b, t: (b, 0, 0)),
                  pl.BlockSpec((1, D), lambda b, t: (0, 0)),
                  pl.BlockSpec((D, ncol), lambda b, t: (0, 0)),
                  pl.BlockSpec(q_tail.shape, lambda b, t: (0, 0))],
        out_specs=[pl.BlockSpec((1, N_KV, 2 * HEAD_DIM, tm * GQA), lambda b, t: (b, 0, 0, t)),
                   heads(4, 2 * HEAD_DIM),
                   pl.BlockSpec((1, 4, tm // QB, V_ROWS, QB), lambda b, t: (b, 0, t, 0, 0)),
                   heads(4, HEAD_DIM),
                   pl.BlockSpec((1, N_KV, GATE_ROWS, tm), lambda b, t: (b, 0, 0, t)),
                   pl.BlockSpec((1, tm, 512), lambda b, t: (b, t, 0)),
                   pl.BlockSpec((1, tm, 512), lambda b, t: (b, t, 0))],
        out_shape=[jax.ShapeDtypeStruct((B, N_KV, 2 * HEAD_DIM, S * GQA), BF16),
                   jax.ShapeDtypeStruct((B, 4, S, 2 * HEAD_DIM), BF16),
                   jax.ShapeDtypeStruct((B, 4, S // QB, V_ROWS, QB), BF16),
                   jax.ShapeDtypeStruct((B, 4, S, HEAD_DIM), F32),
                   jax.ShapeDtypeStruct((B, N_KV, GATE_ROWS, S), F32),
                   jax.ShapeDtypeStruct((B, S, 512), F32),
                   jax.ShapeDtypeStruct((B, S, 512), F32)],
        compiler_params=_params("parallel", "parallel"),
        name="inproj",
    )(x, mod, nw, w_pad, q_tail)


def _compress_kernel(x_ref, pos_ref, w1_ref, w2_ref, o_ref, ot_ref):
    nc = x_ref.shape[2] // CMP_STRIDE
    half = CMP_STRIDE * HEAD_DIM
    kv = pl.program_id(1) // N_KV
    a = jnp.zeros((nc, HEAD_DIM), F32)
    b = jnp.zeros((nc, HEAD_DIM), F32)
    for l in range(CMP_STRIDE):
        x_l = x_ref[0, 0, pl.ds(l, nc, stride=CMP_STRIDE), :]
        cols = slice(l * HEAD_DIM, (l + 1) * HEAD_DIM)
        a = a + _dot((x_l + pos_ref[0:1, cols]).astype(BF16), w1_ref[kv, cols, :].astype(BF16))
        b = b + _dot((x_l + pos_ref[1:2, cols]).astype(BF16),
                     w1_ref[kv, half + l * HEAD_DIM:half + (l + 1) * HEAD_DIM, :].astype(BF16))
    hid = _gelu(a + pltpu.roll(b, nc - 1, 0))
    out = _dot(hid.astype(BF16), w2_ref[kv].astype(BF16))
    row = lax.broadcasted_iota(I32, out.shape, 0)
    out = jnp.where(row < nc - 1, out, 0.0)
    lane = lax.broadcasted_iota(I32, (nc, HEAD_DIM), 1)
    out = jnp.concatenate([out, jnp.where(lane < 2, 1.0, 0.0)], axis=1)
    o_ref[0, 0, 0:CMP_PAD, :] = jnp.zeros((CMP_PAD, 2 * HEAD_DIM), F32)
    o_ref[0, 0, CMP_PAD:CMP_PAD + nc, :] = out
    o_ref[0, 0, CMP_PAD + nc:, :] = jnp.zeros((CMP_WIN - CMP_PAD, 2 * HEAD_DIM), F32)
    ot_ref[0, 0] = out.T


def _compress(kvc, pos2, w1, w2):
    B, _, S, _ = kvc.shape
    NC = S // CMP_STRIDE
    W = CMP_STRIDE * HEAD_DIM
    return pl.pallas_call(
        _compress_kernel,
        grid=(B, 4),
        in_specs=[pl.BlockSpec((1, 1, S, HEAD_DIM), lambda b, i: (b, i, 0, 0)),
                  pl.BlockSpec((2, W), lambda b, i: (0, 0)),
                  pl.BlockSpec((2, 2 * W, HEAD_DIM), lambda b, i: (0, 0, 0)),
                  pl.BlockSpec((2, HEAD_DIM, HEAD_DIM), lambda b, i: (0, 0, 0))],
        out_specs=[pl.BlockSpec((1, 1, NC + CMP_WIN, 2 * HEAD_DIM), lambda b, i: (b, i, 0, 0)),
                   pl.BlockSpec((1, 1, 2 * HEAD_DIM, NC), lambda b, i: (b, i, 0, 0))],
        out_shape=[jax.ShapeDtypeStruct((B, 4, NC + CMP_WIN, 2 * HEAD_DIM), F32),
                   jax.ShapeDtypeStruct((B, 4, 2 * HEAD_DIM, NC), F32)],
        compiler_params=_params("parallel", "parallel"),
        name="compress",
    )(kvc, pos2, w1, w2)


def _attn_kernel(*refs):
    for qi in range(Q_PER_STEP):
        _attn_block(qi, *refs)


def _attn_block(qi, q_ref, ks_ref, vs_ref, kw_ref, vw_ref, kc_ref, vc_ref, vct_ref, g_ref, t_ref, w_ref, ovt_ref,
                ov_ref, o_ref, m_scr, acc_scr, sel_scr, sa_scr, sb_scr, sc_scr):
    qb = pl.program_id(2) * Q_PER_STEP + qi
    R = GQA * QB
    q = q_ref[0, 0, :, qi * R:(qi + 1) * R]
    nc = vct_ref.shape[3]
    nb = ov_ref.shape[1]
    n_top = min(SLC_TOP, ks_ref.shape[2] // SLC_LEN)

    def heads_sum(x):
        out = x[:, 0:QB]
        for hq in range(1, GQA):
            out = out + x[:, hq * QB:(hq + 1) * QB]
        return out

    w0 = pl.multiple_of(qb * 8, 8)
    s_far = _dot(kc_ref[0, 0, CMP_PAD:CMP_PAD + nc, :].astype(BF16), q)
    n_io = lax.broadcasted_iota(I32, (nc, R), 0)
    s_far = jnp.where(n_io < qb * 8 - CMP_PAD, s_far, -jnp.inf)
    s_win = _dot(kc_ref[0, 0, pl.ds(w0, CMP_WIN), :].astype(BF16), q) + w_ref[0]
    j_w = lax.broadcasted_iota(I32, (CMP_WIN, R), 0)
    i_w = lax.broadcasted_iota(I32, (CMP_WIN, R), 1) & (QB - 1)
    dist_w = i_w - CMP_STRIDE * (j_w - CMP_PAD) - (CMP_LEN - 1)
    s_win = jnp.where(dist_w >= 0, jnp.where(j_w >= CMP_PAD - qb * 8, s_win, -jnp.inf), -jnp.inf)
    m = jnp.maximum(jnp.max(s_far, axis=0, keepdims=True), jnp.max(s_win, axis=0, keepdims=True))
    m = jnp.where(m == -jnp.inf, 0.0, m)
    e_far = jnp.exp2(s_far - m)
    e_win = jnp.exp2(s_win - m)
    l = jnp.sum(e_far, axis=0, keepdims=True) + jnp.sum(e_win, axis=0, keepdims=True)
    inv = 1.0 / jnp.maximum(l, 1e-30)
    p_far = e_far * inv
    p_win = e_win * inv
    vcw_t = vc_ref[0, 0, pl.ds(w0, CMP_WIN), :].T[0:V_ROWS, :]
    o_c = (_dot(vct_ref[0, 0, 0:V_ROWS, :].astype(BF16), p_far.astype(BF16))
           + _dot(vcw_t.astype(BF16), p_win.astype(BF16)))
    ov_win_t = ov_ref[pl.ds(w0, CMP_WIN), :].T
    sc_t = (_dot(ovt_ref[...], heads_sum(p_far).astype(BF16))
            + _dot(ov_win_t.astype(BF16), heads_sum(p_win).astype(BF16)))

    n_wt = WINDOW // QB + 1
    k_t, v_t, b_t = [], [], []
    for j in range(n_wt):
        kb = qb - (n_wt - 1) + j
        kb0 = jnp.maximum(kb, 0)
        k_t.append(kw_ref[0, 0, pl.ds(pl.multiple_of(kb0 * QB, QB), QB), :])
        v_t.append(vw_ref[0, 0, kb0])
        b_t.append(t_ref[0, jnp.where(kb < 0, TILE_MASKED, TILE_WINDOW_EDGE if j == 0 else n_wt - 1 - j)])
    s = _dot(jnp.concatenate(k_t, axis=0), q) + jnp.concatenate(b_t, axis=0)
    p = jnp.exp2(s - jnp.max(s, axis=0, keepdims=True))
    acc = _dot(jnp.concatenate(v_t, axis=1), p.astype(BF16))
    o_w = acc[0:HEAD_DIM, :] / acc[HEAD_DIM:HEAD_DIM + 1, :]

    chunk = FAR_CHUNK * QB

    def qk(c):
        return _dot(ks_ref[0, 0, pl.ds(pl.multiple_of(c * chunk, chunk), chunk), :], q)

    n_far = jnp.maximum(qb - (FAR_TILE_DIST - 1), 0) // FAR_CHUNK
    sa_scr[...] = qk(0)
    sc_scr[...] = qk(n_far)

    blk = lax.broadcasted_iota(I32, (nb, QB), 0)
    t_io = lax.broadcasted_iota(I32, (nb, QB), 1)
    cur = qb * (QB // SLC_LEN) + t_io // SLC_LEN
    val = jnp.where(blk <= cur, sc_t, -jnp.inf)
    for forced_blk in (0, cur, cur - 1):
        val = jnp.where(blk == forced_blk, jnp.inf, val)
    drop = jnp.full((nb, QB), NEG, F32)
    for _ in range(n_top):
        mx = jnp.max(val, axis=0, keepdims=True)
        first = jnp.min(jnp.where(val == mx, blk, nb), axis=0, keepdims=True)
        first = jnp.where(mx > -jnp.inf, first, nb)
        pick = blk == first
        drop = jnp.where(pick, 0.0, drop)
        val = jnp.where(pick, -jnp.inf, val)
    sel_scr[...] = drop

    m_scr[...] = jnp.full(m_scr.shape, NEG, F32)
    acc_scr[...] = jnp.zeros(acc_scr.shape, F32)
    blocks_per_chunk = FAR_CHUNK * QB // SLC_LEN

    def soft(c, s):
        rows = sel_scr[pl.ds(pl.multiple_of(c * blocks_per_chunk, blocks_per_chunk), blocks_per_chunk), :]
        drop = jnp.concatenate([jnp.broadcast_to(rows[r:r + 1, :], (SLC_LEN, QB)) for r in range(blocks_per_chunk)],
                               axis=0)
        s = s + jnp.concatenate([drop] * GQA, axis=1)
        m_old = m_scr[...]
        m_new = jnp.maximum(m_old, jnp.max(s, axis=0, keepdims=True))
        p = jnp.exp2(s - m_new)
        v_t = jnp.concatenate([vs_ref[0, 0, c * FAR_CHUNK + j] for j in range(FAR_CHUNK)], axis=1)
        acc_scr[...] = jnp.exp2(m_old - m_new) * acc_scr[...] + _dot(v_t, p.astype(BF16))
        m_scr[...] = m_new

    def near_bias(c):
        tiles = []
        for j in range(FAR_CHUNK):
            delta = qb - (c * FAR_CHUNK + j)
            tiles.append(t_ref[0, jnp.where(delta < 0, TILE_MASKED, delta)])
        return jnp.concatenate(tiles, axis=0)

    def chunk_pairs(first, count, bias, sa):
        last = first + jnp.maximum(count - 1, 0)

        def pair_body(j, carry):
            c = first + 2 * j
            sb_scr[...] = qk(c + 1)
            soft(c, sa[...] + bias(c) if bias else sa[...])
            sa[...] = qk(jnp.minimum(c + 2, last))
            soft(c + 1, sb_scr[...] + bias(c + 1) if bias else sb_scr[...])
            return carry

        lax.fori_loop(0, count // 2, pair_body, 0)

        @pl.when(count % 2 == 1)
        def _():
            soft(last, sa[...] + bias(last) if bias else sa[...])

    chunk_pairs(0, n_far, None, sa_scr)
    chunk_pairs(n_far, qb // FAR_CHUNK + 1 - n_far, near_bias, sc_scr)
    acc = acc_scr[...]
    o_s = acc[0:HEAD_DIM, :] / acc[HEAD_DIM:HEAD_DIM + 1, :]

    gates = g_ref[0, 0, :, qi * QB:(qi + 1) * QB]
    outs = []
    for hq in range(GQA):
        cols = slice(hq * QB, (hq + 1) * QB)
        c = hq * N_BRANCH
        outs.append(gates[c:c + 1, :] * o_c[0:HEAD_DIM, cols] + gates[c + 1:c + 2, :] * o_s[:, cols]
                    + gates[c + 2:c + 3, :] * o_w[:, cols])
    o_ref[0, qi * QB:(qi + 1) * QB, :] = jnp.concatenate(outs, axis=0).T


def _attn(q_t, kx, vx_t, kvc, kvc_t, gates_t, t_tab, w_tab, ov_t, ov):
    B, _, W, _ = q_t.shape
    S = kx.shape[2]
    ncp = kvc.shape[2]
    nb = ov.shape[1]
    keys = lambda j: pl.BlockSpec((1, 1, S, W), lambda b, g, t: (b, j * N_KV + g, 0, 0))
    vals = lambda j: pl.BlockSpec((1, 1, S // QB, V_ROWS, QB), lambda b, g, t: (b, j * N_KV + g, 0, 0, 0))
    cmp = lambda j: pl.BlockSpec((1, 1, ncp, W), lambda b, g, t: (b, j * N_KV + g, 0, 0))
    return pl.pallas_call(
        _attn_kernel,
        grid=(B, N_KV, S // (Q_PER_STEP * QB)),
        in_specs=[pl.BlockSpec((1, 1, W, Q_PER_STEP * GQA * QB), lambda b, g, t: (b, g, 0, t)),
                  keys(0), vals(0), keys(1), vals(1), cmp(0), cmp(1),
                  pl.BlockSpec((1, 1, W, ncp - CMP_WIN), lambda b, g, t: (b, N_KV + g, 0, 0)),
                  pl.BlockSpec((1, 1, GATE_ROWS, Q_PER_STEP * QB), lambda b, g, t: (b, g, 0, t)),
                  pl.BlockSpec((1,) + t_tab.shape[1:], lambda b, g, t: (g, 0, 0, 0)),
                  pl.BlockSpec((1, CMP_WIN, GQA * QB), lambda b, g, t: (g, 0, 0)),
                  pl.BlockSpec(ov_t.shape, lambda b, g, t: (0, 0)),
                  pl.BlockSpec(ov.shape, lambda b, g, t: (0, 0))],
        out_specs=pl.BlockSpec((1, Q_PER_STEP * QB, GQA * HEAD_DIM), lambda b, g, t: (b, t, g)),
        out_shape=jax.ShapeDtypeStruct((B, S, N_HEADS * HEAD_DIM), F32),
        scratch_shapes=[pltpu.VMEM((1, GQA * QB), F32), pltpu.VMEM((V_ROWS, GQA * QB), F32), pltpu.VMEM((nb, QB), F32),
                        pltpu.VMEM((FAR_CHUNK * QB, GQA * QB), F32), pltpu.VMEM((FAR_CHUNK * QB, GQA * QB), F32),
                        pltpu.VMEM((FAR_CHUNK * QB, GQA * QB), F32)],
        compiler_params=_params("parallel", "parallel", "arbitrary"),
        name="attn",
    )(q_t, kx, vx_t, kx, vx_t, kvc, kvc, kvc_t, gates_t, t_tab, w_tab, ov_t, ov)


def _rglru_kernel(xr_ref, gr_ref, cw_ref, cb_ref, wa_ref, ba_ref, wx_ref, bx_ref, lam_ref, gw_ref, o_ref,
                  xbuf, hprev, a_scr, u_scr, h_scr):
    ts = xr_ref.shape[1]
    C = xr_ref.shape[2]

    @pl.when(pl.program_id(1) == 0)
    def _():
        xbuf[0:8, :] = jnp.zeros((8, C), F32)
        hprev[...] = jnp.zeros(hprev.shape, F32)

    xbuf[8:8 + ts, :] = xr_ref[0]
    xc = cb_ref[...] + jnp.zeros((ts, C), F32)
    for j in range(CONV_WIDTH):
        xc = xc + cw_ref[j:j + 1, :] * xbuf[pl.ds(8 - (CONV_WIDTH - 1) + j, ts), :]
    xbuf[0:8, :] = xbuf[ts:ts + 8, :]

    xcb = xc.astype(BF16)
    r = _sigmoid(_dot(xcb, wa_ref[...]) + ba_ref[...])
    i = _sigmoid(_dot(xcb, wx_ref[...]) + bx_ref[...])
    z = -lam_ref[...]
    softplus = jnp.maximum(z, 0.0) + jnp.log(1.0 + jnp.exp(-jnp.abs(z)))
    log_a = -LRU_C * r * softplus
    a = jnp.exp(log_a)
    a_scr[...] = a
    u_scr[...] = jnp.sqrt(1.0 - a * a) * (i * xc)

    row = lax.broadcasted_iota(I32, (8, C), 0)

    def body(k, h):
        r0 = pl.multiple_of(k * 8, 8)
        a = a_scr[pl.ds(r0, 8), :]
        b = u_scr[pl.ds(r0, 8), :]
        for s in (1, 2, 4):
            keep = row >= s
            b = jnp.where(keep, a * pltpu.roll(b, s, 0) + b, b)
            a = jnp.where(keep, a * pltpu.roll(a, s, 0), a)
        hh = a * h + b
        h_scr[pl.ds(r0, 8), :] = hh
        return jnp.broadcast_to(hh[7:8, :], (8, C))

    hprev[...] = lax.fori_loop(0, ts // 8, body, hprev[...])
    out = h_scr[...] * _gelu(gr_ref[0])
    o_ref[0] = (_rms(out) * gw_ref[...]).astype(BF16)


def _rglru(xr, gr, cw, cb, wa_bd, ba, wx_bd, bx, lam, gw, ts=512):
    B, S, C = xr.shape
    ts = min(ts, S)
    vec = pl.BlockSpec((1, C), lambda b, t: (0, 0))
    mat = pl.BlockSpec((C, C), lambda b, t: (0, 0))
    seq = pl.BlockSpec((1, ts, C), lambda b, t: (b, t, 0))
    return pl.pallas_call(
        _rglru_kernel,
        grid=(B, S // ts),
        in_specs=[seq, seq, pl.BlockSpec((CONV_WIDTH, C), lambda b, t: (0, 0)), vec, mat, vec, mat, vec, vec, vec],
        out_specs=seq,
        out_shape=jax.ShapeDtypeStruct((B, S, C), BF16),
        scratch_shapes=[pltpu.VMEM((ts + 8, C), F32), pltpu.VMEM((8, C), F32), pltpu.VMEM((ts, C), F32),
                        pltpu.VMEM((ts, C), F32), pltpu.VMEM((ts, C), F32)],
        compiler_params=_params("parallel", "arbitrary"),
        name="rglru",
    )(xr, gr, cw, cb, wa_bd, ba, wx_bd, bx, lam, gw)


def _to_row_tiles(ref, x):
    rows = x.shape[0]
    for s in range(ROW_TILE):
        ref[pl.ds(s, rows, stride=ROW_TILE), :] = x[:, s * LANES:(s + 1) * LANES]


def _from_row_tiles(ref, rows):
    return jnp.concatenate([ref[pl.ds(s, rows, stride=ROW_TILE), :] for s in range(ROW_TILE)], axis=1)


def _outproj_kernel(oa_ref, yr_ref, x_ref, mod_ref, gaw_ref, wo_ref, n2_ref, wr_ref, x1_ref, h2_ref, st_ref):
    da = oa_ref.shape[2]
    ya = (_rms(oa_ref[0]) * gaw_ref[...]).astype(BF16)
    mix = _dot(ya, wo_ref[0:da, :]) + _dot(yr_ref[0], wo_ref[da:, :])
    x1 = x_ref[0] + mod_ref[0, 2:3, :] * mix
    x1_ref[0] = x1
    h2 = (_rms(x1) * n2_ref[...]) * (1.0 + mod_ref[0, 4:5, :]) + mod_ref[0, 3:4, :]
    h_hi = h2.astype(BF16)
    h2_ref[0] = h_hi
    h_lo = (h2 - h_hi.astype(F32)).astype(BF16)
    logits = _dot_nt(wr_ref[0], h_hi) + (_dot_nt(wr_ref[1], h_hi) + _dot_nt(wr_ref[0], h_lo))
    st_ref[...] = _sigmoid(logits)


def _outproj(oa, yr, x, mod, gaw, wo, n2, wr_t, tm=1024):
    B, S, D = x.shape
    E = wr_t.shape[1]
    nt = S // tm
    row = lambda w: pl.BlockSpec((1, tm, w), lambda b, t: (b, t, 0))
    return pl.pallas_call(
        _outproj_kernel,
        grid=(B, nt),
        in_specs=[row(oa.shape[2]), row(yr.shape[2]), row(D),
                  pl.BlockSpec((1, 6, D), lambda b, t: (b, 0, 0)),
                  pl.BlockSpec((1, oa.shape[2]), lambda b, t: (0, 0)),
                  pl.BlockSpec(wo.shape, lambda b, t: (0, 0)),
                  pl.BlockSpec((1, D), lambda b, t: (0, 0)),
                  pl.BlockSpec((2, E, D), lambda b, t: (0, 0, 0))],
        out_specs=[row(D), row(D), pl.BlockSpec((E, tm), lambda b, t: (0, b * nt + t))],
        out_shape=[jax.ShapeDtypeStruct((B, S, D), F32), jax.ShapeDtypeStruct((B, S, D), BF16),
                   jax.ShapeDtypeStruct((E, B * S), F32)],
        compiler_params=_params("parallel", "parallel"),
        name="outproj",
    )(oa, yr, x, mod, gaw, wo, n2, wr_t)


def _route_kernel(s_ref, b_ref, wgt_ref, pos_ref, tcnt_ref):
    E, tn = s_ref.shape
    per = E // N_EXPERT_GROUPS
    s = s_ref[...]
    s_sel = s + b_ref[...]
    eidx = lax.broadcasted_iota(I32, (E, tn), 0)

    grp = []
    for gi in range(N_EXPERT_GROUPS):
        xg = s_sel[gi * per:(gi + 1) * per, :]
        ig = lax.broadcasted_iota(I32, (per, tn), 0)
        m1 = jnp.max(xg, axis=0, keepdims=True)
        f1 = jnp.min(jnp.where(xg == m1, ig, per), axis=0, keepdims=True)
        m2 = jnp.max(jnp.where(ig == f1, -jnp.inf, xg), axis=0, keepdims=True)
        grp.append(m1 + m2)
    val = []
    for gi in range(N_EXPERT_GROUPS):
        rank = jnp.zeros((1, tn), I32)
        for gj in range(N_EXPERT_GROUPS):
            if gj == gi:
                continue
            ahead = (grp[gj] > grp[gi]) | ((grp[gj] == grp[gi]) & (gj < gi))
            rank = rank + ahead.astype(I32)
        val.append(jnp.where(rank < TOP_GROUPS, s_sel[gi * per:(gi + 1) * per, :], -jnp.inf))
    val = jnp.concatenate(val, axis=0)

    eids, tops = [], []
    hot = jnp.zeros((E, tn), F32)
    for _ in range(TOP_K):
        mx = jnp.max(val, axis=0, keepdims=True)
        first = jnp.min(jnp.where(val == mx, eidx, E), axis=0, keepdims=True)
        pick = eidx == first
        eids.append(first)
        tops.append(jnp.sum(jnp.where(pick, s, 0.0), axis=0, keepdims=True))
        hot = jnp.where(pick, 1.0, hot)
        val = jnp.where(pick, -jnp.inf, val)
    denom = tops[0]
    for t in tops[1:]:
        denom = denom + t

    ti = lax.broadcasted_iota(I32, (tn, tn), 0)
    tj = lax.broadcasted_iota(I32, (tn, tn), 1)
    upper = jnp.where(ti < tj, 1.0, 0.0).astype(BF16)
    before = _dot(hot.astype(BF16), upper)
    tile_count = jnp.sum(hot, axis=1, keepdims=True)
    padded = tile_count + (tile_count - 2.0 * jnp.floor(tile_count * 0.5))
    ei = lax.broadcasted_iota(I32, (E, E), 0)
    ej = lax.broadcasted_iota(I32, (E, E), 1)
    earlier_experts = jnp.where(ej < ei, 1.0, 0.0).astype(BF16)
    run_start = _dot(earlier_experts, jnp.broadcast_to(padded, (E, LANES)).astype(BF16))[:, 0:1]
    for k in range(TOP_K):
        wgt_ref[k:k + 1, :] = ROUTED_SCALE * tops[k] / denom
        pos_ref[k:k + 1, :] = jnp.sum(jnp.where(eidx == eids[k], before + run_start, 0.0), axis=0,
                                      keepdims=True).astype(I32)
    tcnt_ref[0] = tile_count


def _route(s_t, b_col, tn):
    E, N = s_t.shape
    out = pl.BlockSpec((TOP_K, tn), lambda t: (0, t))
    return pl.pallas_call(
        _route_kernel,
        grid=(N // tn,),
        in_specs=[pl.BlockSpec((E, tn), lambda t: (0, t)), pl.BlockSpec((E, 1), lambda t: (0, 0))],
        out_specs=[out, out, pl.BlockSpec((1, E, 1), lambda t: (t, 0, 0))],
        out_shape=[jax.ShapeDtypeStruct((TOP_K, N), F32), jax.ShapeDtypeStruct((TOP_K, N), I32),
                   jax.ShapeDtypeStruct((N // tn, E, 1), F32)],
        compiler_params=_params("parallel"),
        name="route",
    )(s_t, b_col)


_HI = -65536


def _pack_pairs(even_rows, odd_rows):
    hi = lax.bitcast_convert_type(even_rows, I32) & _HI
    lo = lax.shift_right_logical(lax.bitcast_convert_type(odd_rows, I32), 16)
    return hi | lo


def _unpack_pairs(words):
    even_rows = lax.bitcast_convert_type(words & _HI, F32)
    odd_rows = lax.bitcast_convert_type(lax.shift_left(words, 16), F32)
    return even_rows, odd_rows


def _run_copies(src_ref, cnt_ref, dst_ref, n_runs, copy):
    def go(e, carry):
        n = cnt_ref[0, 0, e]

        @pl.when(n > 0)
        def _():
            size = pl.multiple_of(n * ROW_TILE, ROW_TILE)
            copy(pl.ds(pl.multiple_of(src_ref[0, 0, e] * ROW_TILE, ROW_TILE), size),
                 pl.ds(pl.multiple_of(dst_ref[0, 0, e] * ROW_TILE, ROW_TILE), size)).start()
        return carry
    lax.fori_loop(0, n_runs, go, 0, unroll=8)


def _pair_onehots(pos_rows, n_pairs, first_pair, width):
    even = 2 * (lax.broadcasted_iota(I32, (n_pairs, width), 0) + first_pair)
    hot_e = jnp.zeros((n_pairs, width), F32)
    hot_o = jnp.zeros((n_pairs, width), F32)
    for k in range(TOP_K):
        hot_e = jnp.where(even == pos_rows[k:k + 1, :], 1.0, hot_e)
        hot_o = jnp.where(even + 1 == pos_rows[k:k + 1, :], 1.0, hot_o)
    return hot_e, hot_o


def _dispatch_kernel(src_ref, cnt_ref, dst_ref, pos_ref, h_ref, xs_hbm, xbuf, sem):
    t = pl.program_id(0)
    slot = t % 2
    buf = xbuf.at[slot]
    n_pairs = xbuf.shape[1] // ROW_TILE
    tn = h_ref.shape[0]

    def wait_copies(s):
        pltpu.make_async_copy(xbuf.at[s], xs_hbm.at[pl.ds(0, xbuf.shape[1])], sem.at[s]).wait()

    @pl.when(t >= 2)
    def _():
        wait_copies(slot)

    hb = h_ref[...]
    slab = n_pairs // 3
    for sl in range(3):
        hot_e, hot_o = _pair_onehots(pos_ref[...], slab, sl * slab, tn)
        words = _pack_pairs(_dot(hot_e.astype(BF16), hb), _dot(hot_o.astype(BF16), hb))
        for s in range(ROW_TILE):
            buf[pl.ds(sl * slab * ROW_TILE + s, slab, stride=ROW_TILE), :] = words[:, s * LANES:(s + 1) * LANES]
    _run_copies(src_ref, cnt_ref, dst_ref, cnt_ref.shape[2],
                lambda src, dst: pltpu.make_async_copy(buf.at[src], xs_hbm.at[dst], sem.at[slot]))

    @pl.when(t == pl.num_programs(0) - 1)
    def _():
        @pl.when(t >= 1)
        def _():
            wait_copies(1 - slot)
        wait_copies(slot)


def _run_specs(runs):
    return [pl.BlockSpec((1, 1, r.shape[2]), lambda t: (t, 0, 0), memory_space=pltpu.SMEM) for r in runs]


def _tile_pairs(tn, n_experts):
    return (TOP_K * tn + n_experts) // 2


def _dispatch(runs, pos_t, h2, tn):
    N, D = h2.shape
    W = D // ROW_TILE
    n_pairs = _tile_pairs(tn, runs[0].shape[2] - 1)
    assert n_pairs % 24 == 0
    return pl.pallas_call(
        _dispatch_kernel,
        grid=(N // tn,),
        in_specs=_run_specs(runs) + [pl.BlockSpec((TOP_K, tn), lambda t: (0, t)),
                                     pl.BlockSpec((tn, D), lambda t: (t, 0))],
        out_specs=pl.BlockSpec(memory_space=pl.ANY),
        out_shape=jax.ShapeDtypeStruct((N // tn * n_pairs * ROW_TILE, W), I32),
        scratch_shapes=[pltpu.VMEM((2, n_pairs * ROW_TILE, W), I32), pltpu.SemaphoreType.DMA((2,))],
        compiler_params=_params("arbitrary"),
        name="dispatch",
    )(*runs, pos_t, h2)


def _experts_kernel(first_ref, size_ref, e_ref, new_ref, next_ref, wslot_ref, total_ref,
                    w1_hbm, w3_hbm, w2_hbm, xs_hbm, ys_hbm,
                    xbuf, ybuf, w1f, w3f, w2f, w1b, w3b, w2b, xsem, ysem, wsem):
    total = total_ref[0]

    def w_copies(e, slot):
        return [pltpu.make_async_copy(src.at[e], dst.at[slot], wsem.at[slot])
                for src, dst in ((w1_hbm, w1f), (w3_hbm, w3f), (w2_hbm, w2f))]

    def rows(c):
        return (pl.multiple_of(first_ref[c] * ROW_TILE, ROW_TILE), pl.multiple_of(size_ref[c] * ROW_TILE, ROW_TILE))

    def x_copy(c, slot):
        first, size = rows(c)
        return pltpu.make_async_copy(xs_hbm.at[pl.ds(first, size)], xbuf.at[slot, pl.ds(0, size)], xsem.at[slot])

    def y_copy(c, slot):
        first, size = rows(c)
        return pltpu.make_async_copy(ybuf.at[slot, pl.ds(0, size)], ys_hbm.at[pl.ds(first, size)], ysem.at[slot])

    xbuf[...] = jnp.zeros(xbuf.shape, I32)
    for cp in w_copies(e_ref[0], 0):
        cp.start()
    x_copy(0, 0).start()
    half = MOE_BLOCK // 2

    def body(c, carry):
        slot = c % 2

        @pl.when(new_ref[c] == 1)
        def _():
            ws = wslot_ref[c]
            for cp in w_copies(e_ref[c], ws):
                cp.wait()
            w1b[...] = w1f[ws].astype(BF16)
            w3b[...] = w3f[ws].astype(BF16)
            w2b[...] = w2f[ws].astype(BF16)

            @pl.when(next_ref[c] >= 0)
            def _():
                for cp in w_copies(next_ref[c], 1 - ws):
                    cp.start()

        x_copy(c, slot).wait()

        @pl.when(c + 1 < total)
        def _():
            x_copy(c + 1, 1 - slot).start()

        x = jnp.concatenate(_unpack_pairs(_from_row_tiles(xbuf.at[slot], half)), axis=0).astype(BF16)
        h1 = _dot(x, w1b[...])
        h3 = _dot(x, w3b[...])
        hid = (h1 * _sigmoid(h1)) * h3
        y = _dot(hid.astype(BF16), w2b[...]).astype(BF16).astype(F32)

        @pl.when(c >= 2)
        def _():
            y_copy(c - 2, slot).wait()

        _to_row_tiles(ybuf.at[slot], _pack_pairs(y[0:half], y[half:]))
        y_copy(c, slot).start()
        return carry

    lax.fori_loop(0, total, body, 0)

    @pl.when(total >= 2)
    def _():
        y_copy(total - 2, total % 2).wait()

    y_copy(total - 1, (total - 1) % 2).wait()

    ybuf[0] = jnp.zeros(ybuf.shape[1:], I32)
    tail_first = total_ref[1]
    tail_pairs = ys_hbm.shape[0] // ROW_TILE - tail_first

    def fill(j, carry):
        size = pl.multiple_of(jnp.minimum(half, tail_pairs - j * half) * ROW_TILE, ROW_TILE)
        cp = pltpu.make_async_copy(ybuf.at[0, pl.ds(0, size)],
                                   ys_hbm.at[pl.ds(pl.multiple_of((tail_first + j * half) * ROW_TILE, ROW_TILE), size)],
                                   ysem.at[0])
        cp.start()
        cp.wait()
        return carry

    lax.fori_loop(0, (tail_pairs + half - 1) // half, fill, 0)


def _expert_chunks(counts, n_pairs_total):
    E = counts.shape[0]
    block = MOE_BLOCK // 2
    n_max = n_pairs_total // block + E
    starts = jnp.cumsum(counts) - counts
    n_chunks = (counts + block - 1) // block
    c_end = jnp.cumsum(n_chunks)
    c = jnp.arange(n_max, dtype=I32)
    e = jnp.minimum(jnp.sum(c_end[None, :] <= c[:, None], axis=1), E - 1).astype(I32)
    j = c - (c_end - n_chunks)[e]
    size = jnp.clip(counts[e] - j * block, 0, block)
    has_rows = counts > 0
    ids = jnp.where(has_rows, jnp.arange(E, dtype=I32), E)
    later = jnp.flip(lax.cummin(jnp.flip(ids)))
    nxt = jnp.concatenate([later[1:], jnp.full((1,), E, I32)])
    nxt = jnp.where(nxt >= E, -1, nxt)
    ordinal = jnp.cumsum(has_rows.astype(I32)) - 1
    to_i32 = lambda a: a.astype(I32)
    meta = jnp.stack([c_end[-1], jnp.sum(counts)])
    return tuple(map(to_i32, (starts[e] + j * block, size, e, j == 0, nxt[e], ordinal[e] % 2, meta)))


def _experts(chunks, xs, w1, w3, w2):
    rows_total, W = xs.shape
    _, D, De = w1.shape
    any_space = pl.BlockSpec(memory_space=pl.ANY)
    block_rows = MOE_BLOCK // 2 * ROW_TILE
    grid_spec = pltpu.PrefetchScalarGridSpec(
        num_scalar_prefetch=len(chunks),
        grid=(1,),
        in_specs=[any_space] * 4,
        out_specs=any_space,
        scratch_shapes=[pltpu.VMEM((2, block_rows, W), I32), pltpu.VMEM((2, block_rows, W), I32),
                        pltpu.VMEM((2, D, De), F32), pltpu.VMEM((2, D, De), F32), pltpu.VMEM((2, De, D), F32),
                        pltpu.VMEM((D, De), BF16), pltpu.VMEM((D, De), BF16), pltpu.VMEM((De, D), BF16),
                        pltpu.SemaphoreType.DMA((2,)), pltpu.SemaphoreType.DMA((2,)), pltpu.SemaphoreType.DMA((2,))],
    )
    return pl.pallas_call(
        _experts_kernel,
        grid_spec=grid_spec,
        out_shape=jax.ShapeDtypeStruct((rows_total, W), I32),
        compiler_params=_params("arbitrary"),
        name="experts",
    )(*chunks, w1, w3, w2, xs)


def _combine_kernel(src_ref, cnt_ref, dst_ref, ys_hbm, post_ref, wt_ref, h2_ref, x1_ref, mod_ref, ws1_ref,
                    ws3_ref, ws2_ref, fw_ref, o_ref, ybuf, sem):
    tm = h2_ref.shape[0]
    n_pairs = ybuf.shape[0] // ROW_TILE
    n_exp = cnt_ref.shape[2] - 1

    @pl.when(pl.program_id(0) == 0)
    def _():
        ybuf[...] = jnp.zeros(ybuf.shape, I32)

    _run_copies(src_ref, cnt_ref, dst_ref, n_exp,
                lambda src, dst: pltpu.make_async_copy(ys_hbm.at[dst], ybuf.at[src], sem.at[0]))

    h = h2_ref[...]
    a = _dot(h, ws1_ref[...])
    hid = (a * _sigmoid(a)) * _dot(h, ws3_ref[...])
    shared = _dot(hid.astype(BF16), ws2_ref[...])

    pair = 2 * lax.broadcasted_iota(I32, (n_pairs, tm), 0)
    w_cols, mine = [], []
    for odd in range(2):
        w_rows = jnp.zeros((n_pairs, tm), F32)
        for k in range(TOP_K):
            w_rows = jnp.where(pair + odd == post_ref[k:k + 1, :], wt_ref[k:k + 1, :], w_rows)
        w_cols.append(jnp.sum(w_rows, axis=1, keepdims=True))
        mine.append(jnp.where(w_rows != 0.0, 1.0, 0.0).astype(BF16))
    used = pl.multiple_of((src_ref[0, 0, n_exp]) * ROW_TILE, ROW_TILE)
    pltpu.make_async_copy(ys_hbm.at[pl.ds(0, used)], ybuf.at[pl.ds(0, used)], sem.at[0]).wait()
    y_even, y_odd = _unpack_pairs(_from_row_tiles(ybuf, n_pairs))
    sum_rows = lambda hot, y: lax.dot_general(hot, y, (((0,), (0,)), ((), ())), preferred_element_type=F32)
    routed = (sum_rows(mine[0], (y_even * w_cols[0]).astype(BF16))
              + sum_rows(mine[1], (y_odd * w_cols[1]).astype(BF16)))
    x2 = x1_ref[...] + mod_ref[0, 5:6, :] * (routed + shared)
    o_ref[...] = _rms(x2) * fw_ref[...]


def _combine(runs, ys, pos_t, wgt_t, h2, x1, mod, ws1, ws3, ws2, fw, seq, tm):
    N, D = h2.shape
    nt = seq // tm
    row = pl.BlockSpec((tm, D), lambda t: (t, 0))
    full = lambda a: pl.BlockSpec(a.shape, lambda t: (0,) * a.ndim)
    return pl.pallas_call(
        _combine_kernel,
        grid=(N // tm,),
        in_specs=_run_specs(runs) + [pl.BlockSpec(memory_space=pl.ANY),
                                     pl.BlockSpec((TOP_K, tm), lambda t: (0, t)),
                                     pl.BlockSpec((TOP_K, tm), lambda t: (0, t)),
                                     row, row,
                                     pl.BlockSpec((1, 6, D), lambda t: (t // nt, 0, 0)),
                                     full(ws1), full(ws3), full(ws2), full(fw)],
        out_specs=row,
        out_shape=jax.ShapeDtypeStruct((N, D), F32),
        scratch_shapes=[pltpu.VMEM((_tile_pairs(tm, runs[0].shape[2] - 1) * ROW_TILE, D // ROW_TILE), I32),
                        pltpu.SemaphoreType.DMA((1,))],
        compiler_params=_params("arbitrary"),
        name="combine",
    )(*runs, ys, pos_t, wgt_t, h2, x1, mod, ws1, ws3, ws2, fw)


def _t5_bucket(dist):
    n = jnp.maximum(dist, 0)
    max_exact = REL_BUCKETS // 2
    nf = jnp.maximum(n, 1).astype(F32)
    large = max_exact + (jnp.log(nf / max_exact) / math.log(REL_MAX_DIST / max_exact)
                         * (REL_BUCKETS - max_exact)).astype(I32)
    large = jnp.minimum(large, REL_BUCKETS - 1)
    return jnp.where(n < max_exact, n, large)


def _bias_kernel(tbl_ref, far_ref, bt_ref, bw_ref, t_ref, w_ref):
    g = pl.program_id(0)
    d = pl.program_id(1)

    def lookup(bkt, head):
        out = jnp.full(bkt.shape, tbl_ref[0, head], F32)
        for b in range(1, REL_BUCKETS):
            out = jnp.where(bkt == b, tbl_ref[b, head], out)
        return out - far_ref[head]

    j = lax.broadcasted_iota(I32, (QB, QB), 0)
    i = lax.broadcasted_iota(I32, (QB, QB), 1)
    keep = ((d != 0) | (i >= j)) & (d != TILE_MASKED) & ((d != TILE_WINDOW_EDGE) | (j > i))
    for hq in range(GQA):
        t_ref[0, 0, :, hq * QB:(hq + 1) * QB] = jnp.where(keep, lookup(bt_ref[0], GQA * g + hq), NEG)

    @pl.when(d == 0)
    def _():
        for hq in range(GQA):
            w_ref[0, :, hq * QB:(hq + 1) * QB] = lookup(bw_ref[...], GQA * g + hq)


def _bias_tables(rel_table):
    tbl = rel_table.astype(F32) * LOG2E
    far_hi = tbl[REL_BUCKETS - 1].astype(BF16)
    far_lo = (tbl[REL_BUCKETS - 1] - far_hi.astype(F32)).astype(BF16)
    far = far_hi.astype(F32) + far_lo.astype(F32)
    q_tail = jnp.zeros((N_HEADS, HEAD_DIM), F32).at[:, 0].set(far_hi.astype(F32)).at[:, 1].set(far_lo.astype(F32))
    i = jnp.arange(QB)
    tile_dist = np.array(list(range(N_BIAS_TILES)) + [0, WINDOW // QB])
    d_t = (QB * jnp.asarray(tile_dist)[:, None, None] + i[None, None, :] - i[None, :, None])
    d_w = i[None, :] - CMP_STRIDE * (jnp.arange(CMP_WIN)[:, None] - CMP_PAD) - (CMP_LEN - 1)
    n_tiles = len(tile_dist)
    t_tab, w_tab = pl.pallas_call(
        _bias_kernel,
        grid=(N_KV, n_tiles),
        in_specs=[pl.BlockSpec(memory_space=pltpu.SMEM),
                  pl.BlockSpec(memory_space=pltpu.SMEM),
                  pl.BlockSpec((1, QB, QB), lambda g, d: (d, 0, 0)),
                  pl.BlockSpec((CMP_WIN, QB), lambda g, d: (0, 0))],
        out_specs=[pl.BlockSpec((1, 1, QB, GQA * QB), lambda g, d: (g, d, 0, 0)),
                   pl.BlockSpec((1, CMP_WIN, GQA * QB), lambda g, d: (g, 0, 0))],
        out_shape=[jax.ShapeDtypeStruct((N_KV, n_tiles, QB, GQA * QB), F32),
                   jax.ShapeDtypeStruct((N_KV, CMP_WIN, GQA * QB), F32)],
        compiler_params=_params("parallel", "arbitrary"),
        name="bias",
    )(tbl, far, _t5_bucket(d_t), _t5_bucket(d_w))
    far_min = min(FAR_TILE_DIST * QB - QB + 1, CMP_STRIDE * (CMP_PAD + 1) - (CMP_LEN - 1))
    max_exact = REL_BUCKETS // 2
    assert math.log(far_min / max_exact) / math.log(REL_MAX_DIST / max_exact) * max_exact > max_exact - 0.75
    return t_tab, w_tab, q_tail


def _overlap_tables(nc, nb):
    c_start = CMP_STRIDE * np.arange(nc)[:, None]
    s_start = SLC_LEN * np.arange(nb)[None, :]
    ov = ((c_start < s_start + SLC_LEN) & (c_start + CMP_LEN > s_start)).astype(np.float32)
    ov[nc - 1:] = 0.0
    out = np.zeros((nc + CMP_WIN, nb), np.float32)
    out[CMP_PAD:CMP_PAD + nc] = ov
    return jnp.asarray(ov.T, dtype=BF16), jnp.asarray(out)


def _block_diag(w):
    nblk, bs, _ = w.shape
    eye = jnp.eye(nblk, dtype=w.dtype)
    return (eye[:, None, :, None] * w[:, :, None, :]).reshape(nblk * bs, nblk * bs)


def _layer(x, c8, w_ada, b_ada, norm1_w, w_in, cmp_pos, cmp_k_w1, cmp_k_w2, cmp_v_w1, cmp_v_w2, rel_table,
           conv_w, conv_b, lru_wa, lru_ba, lru_wx, lru_bx, lru_lambda, gnorm_attn_w, gnorm_rnn_w, w_out,
           norm2_w, w_router, b_router, w1, w3, w2, ws1, ws3, ws2, final_norm_w):
    B, S, D = x.shape
    N = B * S
    E = w_router.shape[1]
    d_attn = N_HEADS * HEAD_DIM
    d_rnn = D - d_attn
    assert S % (FAR_CHUNK * QB) == 0 and S // SLC_LEN <= MAX_SLC_BLOCKS

    mod = _ada(c8, w_ada, b_ada[None, :])[:B].reshape(B, 6, D)

    n_kv_cols = 6 * N_KV * HEAD_DIM
    n_gate = N_BRANCH * N_HEADS
    wq, wkv, wg, wxr, wgr = jnp.split(w_in, np.cumsum([d_attn, n_kv_cols, n_gate, d_rnn]).tolist(), axis=1)
    wg = jnp.pad(wg.reshape(D, N_KV, GQA * N_BRANCH), ((0, 0), (0, 0), (0, LANES - GQA * N_BRANCH)))
    wg = wg.reshape(D, N_KV * LANES)
    w_pad = jnp.concatenate([wq, wkv, wg, wxr, wgr], axis=1).astype(BF16)
    t_tab, w_tab, q_tail = _bias_tables(rel_table)
    q_t, kx, vx_t, kvc, gates_t, xr, gr = _inproj(x, mod, norm1_w[None, :], w_pad, q_tail)

    nc = S // CMP_STRIDE
    pos2 = cmp_pos.reshape(2, CMP_STRIDE * HEAD_DIM)
    kvc_pad, kvc_t = _compress(kvc, pos2, jnp.stack([cmp_k_w1, cmp_v_w1]), jnp.stack([cmp_k_w2, cmp_v_w2]))
    o_attn = _attn(q_t, kx, vx_t, kvc_pad, kvc_t, gates_t, t_tab, w_tab, *_overlap_tables(nc, MAX_SLC_BLOCKS))

    y_rnn = _rglru(xr, gr, conv_w.reshape(CONV_WIDTH, d_rnn), conv_b[None, :], _block_diag(lru_wa).astype(BF16),
                   lru_ba[None, :], _block_diag(lru_wx).astype(BF16), lru_bx[None, :], lru_lambda[None, :],
                   gnorm_rnn_w[None, :])

    wr_hi = w_router.T.astype(BF16)
    wr_lo = (w_router.T - wr_hi.astype(F32)).astype(BF16)
    x1, h2, s_t = _outproj(o_attn, y_rnn, x, mod, gnorm_attn_w[None, :], w_out.astype(BF16), norm2_w[None, :],
                           jnp.stack([wr_hi, wr_lo]))

    tile = min(MOE_TILE, S)
    wgt_t, pos_t, tile_counts = _route(s_t, b_router[:, None], tile)
    tcnt = tile_counts[:, :, 0].astype(I32)
    n_tiles = tcnt.shape[0]
    tile_pairs = _tile_pairs(tile, E)
    rpairs = (tcnt + 1) // 2
    rpairs = jnp.concatenate([rpairs, tile_pairs - jnp.sum(rpairs, axis=1, keepdims=True)], axis=1)
    per_expert = jnp.sum(rpairs, axis=0)
    off = jnp.cumsum(per_expert) - per_expert
    run_src = jnp.cumsum(rpairs, axis=1) - rpairs
    run_dst = off[None, :] + jnp.cumsum(rpairs, axis=0) - rpairs
    runs = tuple(r.astype(I32)[:, None, :] for r in (run_src, rpairs, run_dst))

    h2f = h2.reshape(N, D)
    xs = _dispatch(runs, pos_t, h2f, tile)
    ys = _experts(_expert_chunks(per_expert[:E], n_tiles * tile_pairs), xs, w1, w3, w2)
    out = _combine(runs, ys, pos_t, wgt_t, h2f, x1.reshape(N, D), mod, ws1.astype(BF16), ws3.astype(BF16),
                   ws2.astype(BF16), final_norm_w[None, :], S, tile)
    return out.reshape(B, S, D)


def kernel(x, c, w_ada, b_ada, norm1_w, w_in, cmp_pos, cmp_k_w1, cmp_k_w2, cmp_v_w1, cmp_v_w2, rel_table, conv_w, conv_b, lru_wa, lru_ba, lru_wx, lru_bx, lru_lambda, gnorm_attn_w, gnorm_rnn_w, w_out, norm2_w, w_router, b_router, w1, w3, w2, ws1, ws3, ws2, final_norm_w):
    assert w_ada.shape[0] == 1
    c8 = jnp.pad(c, ((0, 8 - c.shape[0]), (0, 0)))
    return _layer(x, c8, w_ada[0], b_ada[0], norm1_w[0], w_in[0], cmp_pos[0], cmp_k_w1[0], cmp_k_w2[0],
                  cmp_v_w1[0], cmp_v_w2[0], rel_table, conv_w[0], conv_b[0], lru_wa[0], lru_ba[0], lru_wx[0],
                  lru_bx[0], lru_lambda[0], gnorm_attn_w[0], gnorm_rnn_w[0], w_out[0], norm2_w[0], w_router[0],
                  b_router[0], w1[0], w3[0], w2[0], ws1[0], ws3[0], ws2[0], final_norm_w)
```

```python
import math

import jax
import jax.numpy as jnp
import numpy as np
from jax import lax
from jax.experimental import pallas as pl
from jax.experimental.pallas import tpu as pltpu

F32 = jnp.float32
BF16 = jnp.bfloat16
I32 = jnp.int32

HEAD_DIM = 64
N_HEADS = 8
N_KV = 2
GQA = N_HEADS // N_KV
N_BRANCH = 3
CONV_WIDTH = 4
LRU_C = 8.0
CMP_LEN = 32
CMP_STRIDE = 16
SLC_LEN = 64
SLC_TOP = 16
WINDOW = 512
LANES = 128
QB = 128
MAX_SLC_BLOCKS = 128
GATE_ROWS = 16
V_ROWS = 80
REL_BUCKETS = 32
REL_MAX_DIST = 1024
N_EXPERT_GROUPS = 8
TOP_GROUPS = 4
TOP_K = 8
ROUTED_SCALE = 2.5
MOE_BLOCK = 576
MOE_TILE = 256
ROW_TILE = 8
EPS = 1e-6
NEG = -1e30
LOG2E = math.log2(math.e)
CMP_PAD = 120
CMP_WIN = 128
N_BIAS_TILES = 11
TILE_MASKED = N_BIAS_TILES
TILE_WINDOW_EDGE = N_BIAS_TILES + 1
FAR_TILE_DIST = 8
FAR_CHUNK = 4
VMEM_LIMIT = 52 * 1024 * 1024


def _dot(a, b, **kw):
    return jnp.dot(a, b, preferred_element_type=F32, **kw)


def _dot_nt(a, b, **kw):
    return lax.dot_general(a, b, (((1,), (1,)), ((), ())), preferred_element_type=F32, **kw)


def _gelu(x):
    return 0.5 * x * (1.0 + jnp.tanh(math.sqrt(2.0 / math.pi) * (x + 0.044715 * (x * x * x))))


def _sigmoid(x):
    return 1.0 / (1.0 + jnp.exp(-x))


def _rms(x):
    return x * lax.rsqrt(jnp.mean(x * x, axis=-1, keepdims=True) + EPS)


def _params(*sem):
    return pltpu.CompilerParams(dimension_semantics=sem, vmem_limit_bytes=VMEM_LIMIT)


def _ada_kernel(c_ref, w_ref, b_ref, o_ref):
    c = c_ref[...]
    a = c * _sigmoid(c)
    o_ref[...] = _dot(a, w_ref[...], precision=lax.Precision.HIGHEST) + b_ref[...]


def _ada(c8, w, b):
    d, n = w.shape
    tn = 1536
    return pl.pallas_call(
        _ada_kernel,
        grid=(n // tn,),
        in_specs=[pl.BlockSpec((8, d), lambda j: (0, 0)),
                  pl.BlockSpec((d, tn), lambda j: (0, j)),
                  pl.BlockSpec((1, tn), lambda j: (0, j))],
        out_specs=pl.BlockSpec((8, tn), lambda j: (0, j)),
        out_shape=jax.ShapeDtypeStruct((8, n), F32),
        compiler_params=_params("parallel"),
        name="ada",
    )(c8, w, b)


def _inproj_kernel(x_ref, mod_ref, nw_ref, w_ref, qt_ref, q_ref, kx_ref, vx_ref, kvc_ref, g_ref, xr_ref, gr_ref):
    h = _rms(x_ref[0]) * nw_ref[...]
    h = h * (1.0 + mod_ref[0, 1:2, :]) + mod_ref[0, 0:1, :]
    p = _dot(h.astype(BF16), w_ref[...])
    tm = p.shape[0]
    dq = N_HEADS * HEAD_DIM
    lane = lax.broadcasted_iota(I32, (tm, HEAD_DIM), 1)
    k_tail = jnp.where(lane < 2, 1.0, 0.0)
    v_tail = jnp.where(lane < 1, 1.0, 0.0)
    for hh in range(N_HEADS):
        qh = p[:, hh * HEAD_DIM:(hh + 1) * HEAD_DIM] * (HEAD_DIM ** -0.5 * LOG2E)
        q_tail = jnp.broadcast_to(qt_ref[hh:hh + 1, :], (tm, HEAD_DIM))
        q_t = jnp.concatenate([qh, q_tail], axis=1).T.astype(BF16)
        for blk in range(tm // QB):
            col = (blk * GQA + hh % GQA) * QB
            q_ref[0, hh // GQA, :, col:col + QB] = q_t[:, blk * QB:(blk + 1) * QB]
    for j in range(6):
        for g in range(N_KV):
            col = dq + (j * N_KV + g) * HEAD_DIM
            piece = p[:, col:col + HEAD_DIM]
            if j < 2:
                kvc_ref[0, j * N_KV + g] = piece
            elif j % 2 == 0:
                kx_ref[0, (j // 2 - 1) * N_KV + g] = jnp.concatenate([piece, k_tail], axis=1).astype(BF16)
            else:
                v_t = jnp.concatenate([piece, v_tail], axis=1).T.astype(BF16)
                for blk in range(tm // QB):
                    vx_ref[0, (j // 2 - 1) * N_KV + g, blk] = v_t[0:V_ROWS, blk * QB:(blk + 1) * QB]
    c0 = dq + 6 * N_KV * HEAD_DIM
    for g in range(N_KV):
        g_ref[0, g] = _sigmoid(p[:, c0 + g * LANES:c0 + (g + 1) * LANES]).T[0:GATE_ROWS, :]
    c1 = c0 + N_KV * LANES
    d_rnn = xr_ref.shape[2]
    xr_ref[0] = p[:, c1:c1 + d_rnn]
    gr_ref[0] = p[:, c1 + d_rnn:c1 + 2 * d_rnn]


def _inproj(x, mod, nw, w_pad, q_tail, tm=1024):
    B, S, D = x.shape
    ncol = w_pad.shape[1]
    heads = lambda n, w: pl.BlockSpec((1, n, tm, w), lambda b, t: (b, 0, t, 0))
    return pl.pallas_call(
        _inproj_kernel,
        grid=(B, S // tm),
        in_specs=[pl.BlockSpec((1, tm, D), lambda b, t: (b, t, 0)),
                  pl.BlockSpec((1, 6, D), lambda b, t: (b, 0, 0)),
                  pl.BlockSpec((1, D), lambda b, t: (0, 0)),
                  pl.BlockSpec((D, ncol), lambda b, t: (0, 0)),
                  pl.BlockSpec(q_tail.shape, lambda b, t: (0, 0))],
        out_specs=[pl.BlockSpec((1, N_KV, 2 * HEAD_DIM, tm * GQA), lambda b, t: (b, 0, 0, t)),
                   heads(4, 2 * HEAD_DIM),
                   pl.BlockSpec((1, 4, tm // QB, V_ROWS, QB), lambda b, t: (b, 0, t, 0, 0)),
                   heads(4, HEAD_DIM),
                   pl.BlockSpec((1, N_KV, GATE_ROWS, tm), lambda b, t: (b, 0, 0, t)),
                   pl.BlockSpec((1, tm, 512), lambda b, t: (b, t, 0)),
                   pl.BlockSpec((1, tm, 512), lambda b, t: (b, t, 0))],
        out_shape=[jax.ShapeDtypeStruct((B, N_KV, 2 * HEAD_DIM, S * GQA), BF16),
                   jax.ShapeDtypeStruct((B, 4, S, 2 * HEAD_DIM), BF16),
                   jax.ShapeDtypeStruct((B, 4, S // QB, V_ROWS, QB), BF16),
                   jax.ShapeDtypeStruct((B, 4, S, HEAD_DIM), F32),
                   jax.ShapeDtypeStruct((B, N_KV, GATE_ROWS, S), F32),
                   jax.ShapeDtypeStruct((B, S, 512), F32),
                   jax.ShapeDtypeStruct((B, S, 512), F32)],
        compiler_params=_params("parallel", "parallel"),
        name="inproj",
    )(x, mod, nw, w_pad, q_tail)


def _compress_kernel(x_ref, pos_ref, w1_ref, w2_ref, o_ref, ot_ref):
    nc = x_ref.shape[2] // CMP_STRIDE
    half = CMP_STRIDE * HEAD_DIM
    kv = pl.program_id(1) // N_KV
    a = jnp.zeros((nc, HEAD_DIM), F32)
    b = jnp.zeros((nc, HEAD_DIM), F32)
    for l in range(CMP_STRIDE):
        x_l = x_ref[0, 0, pl.ds(l, nc, stride=CMP_STRIDE), :]
        cols = slice(l * HEAD_DIM, (l + 1) * HEAD_DIM)
        a = a + _dot((x_l + pos_ref[0:1, cols]).astype(BF16), w1_ref[kv, cols, :].astype(BF16))
        b = b + _dot((x_l + pos_ref[1:2, cols]).astype(BF16),
                     w1_ref[kv, half + l * HEAD_DIM:half + (l + 1) * HEAD_DIM, :].astype(BF16))
    hid = _gelu(a + pltpu.roll(b, nc - 1, 0))
    out = _dot(hid.astype(BF16), w2_ref[kv].astype(BF16))
    row = lax.broadcasted_iota(I32, out.shape, 0)
    out = jnp.where(row < nc - 1, out, 0.0)
    lane = lax.broadcasted_iota(I32, (nc, HEAD_DIM), 1)
    out = jnp.concatenate([out, jnp.where(lane < 2, 1.0, 0.0)], axis=1)
    o_ref[0, 0, 0:CMP_PAD, :] = jnp.zeros((CMP_PAD, 2 * HEAD_DIM), F32)
    o_ref[0, 0, CMP_PAD:CMP_PAD + nc, :] = out
    o_ref[0, 0, CMP_PAD + nc:, :] = jnp.zeros((CMP_WIN - CMP_PAD, 2 * HEAD_DIM), F32)
    ot_ref[0, 0] = out.T


def _compress(kvc, pos2, w1, w2):
    B, _, S, _ = kvc.shape
    NC = S // CMP_STRIDE
    W = CMP_STRIDE * HEAD_DIM
    return pl.pallas_call(
        _compress_kernel,
        grid=(B, 4),
        in_specs=[pl.BlockSpec((1, 1, S, HEAD_DIM), lambda b, i: (b, i, 0, 0)),
                  pl.BlockSpec((2, W), lambda b, i: (0, 0)),
                  pl.BlockSpec((2, 2 * W, HEAD_DIM), lambda b, i: (0, 0, 0)),
                  pl.BlockSpec((2, HEAD_DIM, HEAD_DIM), lambda b, i: (0, 0, 0))],
        out_specs=[pl.BlockSpec((1, 1, NC + CMP_WIN, 2 * HEAD_DIM), lambda b, i: (b, i, 0, 0)),
                   pl.BlockSpec((1, 1, 2 * HEAD_DIM, NC), lambda b, i: (b, i, 0, 0))],
        out_shape=[jax.ShapeDtypeStruct((B, 4, NC + CMP_WIN, 2 * HEAD_DIM), F32),
                   jax.ShapeDtypeStruct((B, 4, 2 * HEAD_DIM, NC), F32)],
        compiler_params=_params("parallel", "parallel"),
        name="compress",
    )(kvc, pos2, w1, w2)


def _attn_kernel(q_ref, ks_ref, vs_ref, kw_ref, vw_ref, kc_ref, vc_ref, vct_ref, g_ref, t_ref, w_ref, ovt_ref, ov_ref,
                 o_ref, m_scr, acc_scr, sel_scr, sa_scr, sb_scr, sc_scr):
    qb = pl.program_id(2)
    R = GQA * QB
    q = q_ref[0, 0]
    nc = vct_ref.shape[3]
    nb = ov_ref.shape[1]
    n_top = min(SLC_TOP, ks_ref.shape[2] // SLC_LEN)

    def heads_sum(x):
        out = x[:, 0:QB]
        for hq in range(1, GQA):
            out = out + x[:, hq * QB:(hq + 1) * QB]
        return out

    w0 = pl.multiple_of(qb * 8, 8)
    s_far = _dot(kc_ref[0, 0, CMP_PAD:CMP_PAD + nc, :].astype(BF16), q)
    n_io = lax.broadcasted_iota(I32, (nc, R), 0)
    s_far = jnp.where(n_io < qb * 8 - CMP_PAD, s_far, -jnp.inf)
    s_win = _dot(kc_ref[0, 0, pl.ds(w0, CMP_WIN), :].astype(BF16), q) + w_ref[0]
    j_w = lax.broadcasted_iota(I32, (CMP_WIN, R), 0)
    i_w = lax.broadcasted_iota(I32, (CMP_WIN, R), 1) & (QB - 1)
    dist_w = i_w - CMP_STRIDE * (j_w - CMP_PAD) - (CMP_LEN - 1)
    s_win = jnp.where(dist_w >= 0, jnp.where(j_w >= CMP_PAD - qb * 8, s_win, -jnp.inf), -jnp.inf)
    m = jnp.maximum(jnp.max(s_far, axis=0, keepdims=True), jnp.max(s_win, axis=0, keepdims=True))
    m = jnp.where(m == -jnp.inf, 0.0, m)
    e_far = jnp.exp2(s_far - m)
    e_win = jnp.exp2(s_win - m)
    l = jnp.sum(e_far, axis=0, keepdims=True) + jnp.sum(e_win, axis=0, keepdims=True)
    inv = 1.0 / jnp.maximum(l, 1e-30)
    p_far = e_far * inv
    p_win = e_win * inv
    vcw_t = vc_ref[0, 0, pl.ds(w0, CMP_WIN), :].T[0:V_ROWS, :]
    o_c = (_dot(vct_ref[0, 0, 0:V_ROWS, :].astype(BF16), p_far.astype(BF16))
           + _dot(vcw_t.astype(BF16), p_win.astype(BF16)))
    ov_win_t = ov_ref[pl.ds(w0, CMP_WIN), :].T
    sc_t = (_dot(ovt_ref[...], heads_sum(p_far).astype(BF16))
            + _dot(ov_win_t.astype(BF16), heads_sum(p_win).astype(BF16)))

    n_wt = WINDOW // QB + 1
    k_t, v_t, b_t = [], [], []
    for j in range(n_wt):
        kb = qb - (n_wt - 1) + j
        kb0 = jnp.maximum(kb, 0)
        k_t.append(kw_ref[0, 0, pl.ds(pl.multiple_of(kb0 * QB, QB), QB), :])
        v_t.append(vw_ref[0, 0, kb0])
        b_t.append(t_ref[0, jnp.where(kb < 0, TILE_MASKED, TILE_WINDOW_EDGE if j == 0 else n_wt - 1 - j)])
    s = _dot(jnp.concatenate(k_t, axis=0), q) + jnp.concatenate(b_t, axis=0)
    p = jnp.exp2(s - jnp.max(s, axis=0, keepdims=True))
    acc = _dot(jnp.concatenate(v_t, axis=1), p.astype(BF16))
    o_w = acc[0:HEAD_DIM, :] / acc[HEAD_DIM:HEAD_DIM + 1, :]

    chunk = FAR_CHUNK * QB

    def qk(c):
        return _dot(ks_ref[0, 0, pl.ds(pl.multiple_of(c * chunk, chunk), chunk), :], q)

    n_far = jnp.maximum(qb - (FAR_TILE_DIST - 1), 0) // FAR_CHUNK
    sa_scr[...] = qk(0)
    sc_scr[...] = qk(n_far)

    blk = lax.broadcasted_iota(I32, (nb, QB), 0)
    t_io = lax.broadcasted_iota(I32, (nb, QB), 1)
    cur = qb * (QB // SLC_LEN) + t_io // SLC_LEN
    val = jnp.where(blk <= cur, sc_t, -jnp.inf)
    for forced_blk in (0, cur, cur - 1):
        val = jnp.where(blk == forced_blk, jnp.inf, val)
    drop = jnp.full((nb, QB), NEG, F32)
    for _ in range(n_top):
        mx = jnp.max(val, axis=0, keepdims=True)
        first = jnp.min(jnp.where(val == mx, blk, nb), axis=0, keepdims=True)
        first = jnp.where(mx > -jnp.inf, first, nb)
        pick = blk == first
        drop = jnp.where(pick, 0.0, drop)
        val = jnp.where(pick, -jnp.inf, val)
    sel_scr[...] = drop

    m_scr[...] = jnp.full(m_scr.shape, NEG, F32)
    acc_scr[...] = jnp.zeros(acc_scr.shape, F32)
    blocks_per_chunk = FAR_CHUNK * QB // SLC_LEN

    def soft(c, s):
        rows = sel_scr[pl.ds(pl.multiple_of(c * blocks_per_chunk, blocks_per_chunk), blocks_per_chunk), :]
        drop = jnp.concatenate([jnp.broadcast_to(rows[r:r + 1, :], (SLC_LEN, QB)) for r in range(blocks_per_chunk)],
                               axis=0)
        s = s + jnp.concatenate([drop] * GQA, axis=1)
        m_old = m_scr[...]
        m_new = jnp.maximum(m_old, jnp.max(s, axis=0, keepdims=True))
        p = jnp.exp2(s - m_new)
        v_t = jnp.concatenate([vs_ref[0, 0, c * FAR_CHUNK + j] for j in range(FAR_CHUNK)], axis=1)
        acc_scr[...] = jnp.exp2(m_old - m_new) * acc_scr[...] + _dot(v_t, p.astype(BF16))
        m_scr[...] = m_new

    def near_bias(c):
        tiles = []
        for j in range(FAR_CHUNK):
            delta = qb - (c * FAR_CHUNK + j)
            tiles.append(t_ref[0, jnp.where(delta < 0, TILE_MASKED, delta)])
        return jnp.concatenate(tiles, axis=0)

    def chunk_pairs(first, count, bias, sa):
        last = first + jnp.maximum(count - 1, 0)

        def pair_body(j, carry):
            c = first + 2 * j
            sb_scr[...] = qk(c + 1)
            soft(c, sa[...] + bias(c) if bias else sa[...])
            sa[...] = qk(jnp.minimum(c + 2, last))
            soft(c + 1, sb_scr[...] + bias(c + 1) if bias else sb_scr[...])
            return carry

        lax.fori_loop(0, count // 2, pair_body, 0)

        @pl.when(count % 2 == 1)
        def _():
            soft(last, sa[...] + bias(last) if bias else sa[...])

    chunk_pairs(0, n_far, None, sa_scr)
    chunk_pairs(n_far, qb // FAR_CHUNK + 1 - n_far, near_bias, sc_scr)
    acc = acc_scr[...]
    o_s = acc[0:HEAD_DIM, :] / acc[HEAD_DIM:HEAD_DIM + 1, :]

    gates = g_ref[0, 0]
    outs = []
    for hq in range(GQA):
        cols = slice(hq * QB, (hq + 1) * QB)
        c = hq * N_BRANCH
        outs.append(gates[c:c + 1, :] * o_c[0:HEAD_DIM, cols] + gates[c + 1:c + 2, :] * o_s[:, cols]
                    + gates[c + 2:c + 3, :] * o_w[:, cols])
    o_ref[0] = jnp.concatenate(outs, axis=0).T


def _attn(q_t, kx, vx_t, kvc, kvc_t, gates_t, t_tab, w_tab, ov_t, ov):
    B, _, W, _ = q_t.shape
    S = kx.shape[2]
    ncp = kvc.shape[2]
    nb = ov.shape[1]
    keys = lambda j: pl.BlockSpec((1, 1, S, W), lambda b, g, t: (b, j * N_KV + g, 0, 0))
    vals = lambda j: pl.BlockSpec((1, 1, S // QB, V_ROWS, QB), lambda b, g, t: (b, j * N_KV + g, 0, 0, 0))
    cmp = lambda j: pl.BlockSpec((1, 1, ncp, W), lambda b, g, t: (b, j * N_KV + g, 0, 0))
    return pl.pallas_call(
        _attn_kernel,
        grid=(B, N_KV, S // QB),
        in_specs=[pl.BlockSpec((1, 1, W, GQA * QB), lambda b, g, t: (b, g, 0, t)),
                  keys(0), vals(0), keys(1), vals(1), cmp(0), cmp(1),
                  pl.BlockSpec((1, 1, W, ncp - CMP_WIN), lambda b, g, t: (b, N_KV + g, 0, 0)),
                  pl.BlockSpec((1, 1, GATE_ROWS, QB), lambda b, g, t: (b, g, 0, t)),
                  pl.BlockSpec((1,) + t_tab.shape[1:], lambda b, g, t: (g, 0, 0, 0)),
                  pl.BlockSpec((1, CMP_WIN, GQA * QB), lambda b, g, t: (g, 0, 0)),
                  pl.BlockSpec(ov_t.shape, lambda b, g, t: (0, 0)),
                  pl.BlockSpec(ov.shape, lambda b, g, t: (0, 0))],
        out_specs=pl.BlockSpec((1, QB, GQA * HEAD_DIM), lambda b, g, t: (b, t, g)),
        out_shape=jax.ShapeDtypeStruct((B, S, N_HEADS * HEAD_DIM), F32),
        scratch_shapes=[pltpu.VMEM((1, GQA * QB), F32), pltpu.VMEM((V_ROWS, GQA * QB), F32), pltpu.VMEM((nb, QB), F32),
                        pltpu.VMEM((FAR_CHUNK * QB, GQA * QB), F32), pltpu.VMEM((FAR_CHUNK * QB, GQA * QB), F32),
                        pltpu.VMEM((FAR_CHUNK * QB, GQA * QB), F32)],
        compiler_params=_params("parallel", "parallel", "arbitrary"),
        name="attn",
    )(q_t, kx, vx_t, kx, vx_t, kvc, kvc, kvc_t, gates_t, t_tab, w_tab, ov_t, ov)


def _rglru_kernel(xr_ref, gr_ref, cw_ref, cb_ref, wa_ref, ba_ref, wx_ref, bx_ref, lam_ref, gw_ref, o_ref,
                  xbuf, hprev, a_scr, u_scr, h_scr):
    ts = xr_ref.shape[1]
    C = xr_ref.shape[2]

    @pl.when(pl.program_id(1) == 0)
    def _():
        xbuf[0:8, :] = jnp.zeros((8, C), F32)
        hprev[...] = jnp.zeros(hprev.shape, F32)

    xbuf[8:8 + ts, :] = xr_ref[0]
    xc = cb_ref[...] + jnp.zeros((ts, C), F32)
    for j in range(CONV_WIDTH):
        xc = xc + cw_ref[j:j + 1, :] * xbuf[pl.ds(8 - (CONV_WIDTH - 1) + j, ts), :]
    xbuf[0:8, :] = xbuf[ts:ts + 8, :]

    xcb = xc.astype(BF16)
    r = _sigmoid(_dot(xcb, wa_ref[...]) + ba_ref[...])
    i = _sigmoid(_dot(xcb, wx_ref[...]) + bx_ref[...])
    z = -lam_ref[...]
    softplus = jnp.maximum(z, 0.0) + jnp.log(1.0 + jnp.exp(-jnp.abs(z)))
    log_a = -LRU_C * r * softplus
    a = jnp.exp(log_a)
    a_scr[...] = a
    u_scr[...] = jnp.sqrt(1.0 - a * a) * (i * xc)

    row = lax.broadcasted_iota(I32, (8, C), 0)

    def body(k, h):
        r0 = pl.multiple_of(k * 8, 8)
        a = a_scr[pl.ds(r0, 8), :]
        b = u_scr[pl.ds(r0, 8), :]
        for s in (1, 2, 4):
            keep = row >= s
            b = jnp.where(keep, a * pltpu.roll(b, s, 0) + b, b)
            a = jnp.where(keep, a * pltpu.roll(a, s, 0), a)
        hh = a * h + b
        h_scr[pl.ds(r0, 8), :] = hh
        return jnp.broadcast_to(hh[7:8, :], (8, C))

    hprev[...] = lax.fori_loop(0, ts // 8, body, hprev[...])
    out = h_scr[...] * _gelu(gr_ref[0])
    o_ref[0] = (_rms(out) * gw_ref[...]).astype(BF16)


def _rglru(xr, gr, cw, cb, wa_bd, ba, wx_bd, bx, lam, gw, ts=512):
    B, S, C = xr.shape
    ts = min(ts, S)
    vec = pl.BlockSpec((1, C), lambda b, t: (0, 0))
    mat = pl.BlockSpec((C, C), lambda b, t: (0, 0))
    seq = pl.BlockSpec((1, ts, C), lambda b, t: (b, t, 0))
    return pl.pallas_call(
        _rglru_kernel,
        grid=(B, S // ts),
        in_specs=[seq, seq, pl.BlockSpec((CONV_WIDTH, C), lambda b, t: (0, 0)), vec, mat, vec, mat, vec, vec, vec],
        out_specs=seq,
        out_shape=jax.ShapeDtypeStruct((B, S, C), BF16),
        scratch_shapes=[pltpu.VMEM((ts + 8, C), F32), pltpu.VMEM((8, C), F32), pltpu.VMEM((ts, C), F32),
                        pltpu.VMEM((ts, C), F32), pltpu.VMEM((ts, C), F32)],
        compiler_params=_params("parallel", "arbitrary"),
        name="rglru",
    )(xr, gr, cw, cb, wa_bd, ba, wx_bd, bx, lam, gw)


def _to_row_tiles(ref, x):
    rows = x.shape[0]
    for s in range(ROW_TILE):
        ref[pl.ds(s, rows, stride=ROW_TILE), :] = x[:, s * LANES:(s + 1) * LANES]


def _from_row_tiles(ref, rows):
    return jnp.concatenate([ref[pl.ds(s, rows, stride=ROW_TILE), :] for s in range(ROW_TILE)], axis=1)


def _outproj_kernel(oa_ref, yr_ref, x_ref, mod_ref, gaw_ref, wo_ref, n2_ref, wr_ref, rb_ref, x1_ref, h2_ref,
                    wgt_ref, pos_ref, tcnt_ref):
    da = oa_ref.shape[2]
    ya = (_rms(oa_ref[0]) * gaw_ref[...]).astype(BF16)
    mix = _dot(ya, wo_ref[0:da, :]) + _dot(yr_ref[0], wo_ref[da:, :])
    x1 = x_ref[0] + mod_ref[0, 2:3, :] * mix
    x1_ref[0] = x1
    h2 = (_rms(x1) * n2_ref[...]) * (1.0 + mod_ref[0, 4:5, :]) + mod_ref[0, 3:4, :]
    h_hi = h2.astype(BF16)
    h2_ref[0] = h_hi
    h_lo = (h2 - h_hi.astype(F32)).astype(BF16)
    logits = _dot_nt(wr_ref[0], h_hi) + (_dot_nt(wr_ref[1], h_hi) + _dot_nt(wr_ref[0], h_lo))
    probs = _sigmoid(logits)
    tile = probs.shape[1] // tcnt_ref.shape[0]
    for j in range(tcnt_ref.shape[0]):
        _route_tile(probs[:, j * tile:(j + 1) * tile], rb_ref[...], wgt_ref, pos_ref, tcnt_ref, j)


def _outproj(oa, yr, x, mod, gaw, wo, n2, wr_t, rb_col, tile, tm=1024):
    B, S, D = x.shape
    E = wr_t.shape[1]
    tm = min(tm, S)
    nt = S // tm
    assert tm % tile == 0
    routed = pl.BlockSpec((TOP_K, tm), lambda b, t: (0, b * nt + t))
    row = lambda w: pl.BlockSpec((1, tm, w), lambda b, t: (b, t, 0))
    return pl.pallas_call(
        _outproj_kernel,
        grid=(B, nt),
        in_specs=[row(oa.shape[2]), row(yr.shape[2]), row(D),
                  pl.BlockSpec((1, 6, D), lambda b, t: (b, 0, 0)),
                  pl.BlockSpec((1, oa.shape[2]), lambda b, t: (0, 0)),
                  pl.BlockSpec(wo.shape, lambda b, t: (0, 0)),
                  pl.BlockSpec((1, D), lambda b, t: (0, 0)),
                  pl.BlockSpec((2, E, D), lambda b, t: (0, 0, 0)),
                  pl.BlockSpec((E, 1), lambda b, t: (0, 0))],
        out_specs=[row(D), row(D), routed, routed,
                   pl.BlockSpec((tm // tile, E, 1), lambda b, t: (b * nt + t, 0, 0))],
        out_shape=[jax.ShapeDtypeStruct((B, S, D), F32), jax.ShapeDtypeStruct((B, S, D), BF16),
                   jax.ShapeDtypeStruct((TOP_K, B * S), F32), jax.ShapeDtypeStruct((TOP_K, B * S), I32),
                   jax.ShapeDtypeStruct((B * S // tile, E, 1), F32)],
        compiler_params=_params("parallel", "parallel"),
        name="outproj",
    )(oa, yr, x, mod, gaw, wo, n2, wr_t, rb_col)


def _route_tile(s, bias_col, wgt_ref, pos_ref, tcnt_ref, j):
    E, tn = s.shape
    per = E // N_EXPERT_GROUPS
    lanes = slice(j * tn, (j + 1) * tn)
    s_sel = s + bias_col
    eidx = lax.broadcasted_iota(I32, (E, tn), 0)

    grp = []
    for gi in range(N_EXPERT_GROUPS):
        xg = s_sel[gi * per:(gi + 1) * per, :]
        ig = lax.broadcasted_iota(I32, (per, tn), 0)
        m1 = jnp.max(xg, axis=0, keepdims=True)
        f1 = jnp.min(jnp.where(xg == m1, ig, per), axis=0, keepdims=True)
        m2 = jnp.max(jnp.where(ig == f1, -jnp.inf, xg), axis=0, keepdims=True)
        grp.append(m1 + m2)
    val = []
    for gi in range(N_EXPERT_GROUPS):
        rank = jnp.zeros((1, tn), I32)
        for gj in range(N_EXPERT_GROUPS):
            if gj == gi:
                continue
            ahead = (grp[gj] > grp[gi]) | ((grp[gj] == grp[gi]) & (gj < gi))
            rank = rank + ahead.astype(I32)
        val.append(jnp.where(rank < TOP_GROUPS, s_sel[gi * per:(gi + 1) * per, :], -jnp.inf))
    val = jnp.concatenate(val, axis=0)

    eids, tops = [], []
    hot = jnp.zeros((E, tn), F32)
    for _ in range(TOP_K):
        mx = jnp.max(val, axis=0, keepdims=True)
        first = jnp.min(jnp.where(val == mx, eidx, E), axis=0, keepdims=True)
        pick = eidx == first
        eids.append(first)
        tops.append(jnp.sum(jnp.where(pick, s, 0.0), axis=0, keepdims=True))
        hot = jnp.where(pick, 1.0, hot)
        val = jnp.where(pick, -jnp.inf, val)
    denom = tops[0]
    for t in tops[1:]:
        denom = denom + t

    ti = lax.broadcasted_iota(I32, (tn, tn), 0)
    tj = lax.broadcasted_iota(I32, (tn, tn), 1)
    upper = jnp.where(ti < tj, 1.0, 0.0).astype(BF16)
    before = _dot(hot.astype(BF16), upper)
    tile_count = jnp.sum(hot, axis=1, keepdims=True)
    padded = tile_count + (tile_count - 2.0 * jnp.floor(tile_count * 0.5))
    ei = lax.broadcasted_iota(I32, (E, E), 0)
    ej = lax.broadcasted_iota(I32, (E, E), 1)
    earlier_experts = jnp.where(ej < ei, 1.0, 0.0).astype(BF16)
    run_start = _dot(earlier_experts, jnp.broadcast_to(padded, (E, LANES)).astype(BF16))[:, 0:1]
    for k in range(TOP_K):
        wgt_ref[k:k + 1, lanes] = ROUTED_SCALE * tops[k] / denom
        pos_ref[k:k + 1, lanes] = jnp.sum(jnp.where(eidx == eids[k], before + run_start, 0.0), axis=0,
                                          keepdims=True).astype(I32)
    tcnt_ref[j] = tile_count


_HI = -65536


def _pack_pairs(even_rows, odd_rows):
    hi = lax.bitcast_convert_type(even_rows, I32) & _HI
    lo = lax.shift_right_logical(lax.bitcast_convert_type(odd_rows, I32), 16)
    return hi | lo


def _unpack_pairs(words):
    even_rows = lax.bitcast_convert_type(words & _HI, F32)
    odd_rows = lax.bitcast_convert_type(lax.shift_left(words, 16), F32)
    return even_rows, odd_rows


def _run_copies(src_ref, cnt_ref, dst_ref, n_runs, copy):
    def go(e, carry):
        n = cnt_ref[0, 0, e]

        @pl.when(n > 0)
        def _():
            size = pl.multiple_of(n * ROW_TILE, ROW_TILE)
            copy(pl.ds(pl.multiple_of(src_ref[0, 0, e] * ROW_TILE, ROW_TILE), size),
                 pl.ds(pl.multiple_of(dst_ref[0, 0, e] * ROW_TILE, ROW_TILE), size)).start()
        return carry
    lax.fori_loop(0, n_runs, go, 0, unroll=8)


def _pair_onehots(pos_rows, n_pairs, first_pair, width):
    even = 2 * (lax.broadcasted_iota(I32, (n_pairs, width), 0) + first_pair)
    hot_e = jnp.zeros((n_pairs, width), F32)
    hot_o = jnp.zeros((n_pairs, width), F32)
    for k in range(TOP_K):
        hot_e = jnp.where(even == pos_rows[k:k + 1, :], 1.0, hot_e)
        hot_o = jnp.where(even + 1 == pos_rows[k:k + 1, :], 1.0, hot_o)
    return hot_e, hot_o


def _dispatch_kernel(src_ref, cnt_ref, dst_ref, pos_ref, h_ref, xs_hbm, xbuf, sem):
    t = pl.program_id(0)
    slot = t % 2
    buf = xbuf.at[slot]
    n_pairs = xbuf.shape[1] // ROW_TILE
    tn = h_ref.shape[0]

    def wait_copies(s):
        pltpu.make_async_copy(xbuf.at[s], xs_hbm.at[pl.ds(0, xbuf.shape[1])], sem.at[s]).wait()

    @pl.when(t >= 2)
    def _():
        wait_copies(slot)

    hb = h_ref[...]
    slab = n_pairs // 3
    for sl in range(3):
        hot_e, hot_o = _pair_onehots(pos_ref[...], slab, sl * slab, tn)
        words = _pack_pairs(_dot(hot_e.astype(BF16), hb), _dot(hot_o.astype(BF16), hb))
        for s in range(ROW_TILE):
            buf[pl.ds(sl * slab * ROW_TILE + s, slab, stride=ROW_TILE), :] = words[:, s * LANES:(s + 1) * LANES]
    _run_copies(src_ref, cnt_ref, dst_ref, cnt_ref.shape[2],
                lambda src, dst: pltpu.make_async_copy(buf.at[src], xs_hbm.at[dst], sem.at[slot]))

    @pl.when(t == pl.num_programs(0) - 1)
    def _():
        @pl.when(t >= 1)
        def _():
            wait_copies(1 - slot)
        wait_copies(slot)


def _run_specs(runs):
    return [pl.BlockSpec((1, 1, r.shape[2]), lambda t: (t, 0, 0), memory_space=pltpu.SMEM) for r in runs]


def _tile_pairs(tn, n_experts):
    return (TOP_K * tn + n_experts) // 2


def _dispatch(runs, pos_t, h2, tn):
    N, D = h2.shape
    W = D // ROW_TILE
    n_pairs = _tile_pairs(tn, runs[0].shape[2] - 1)
    assert n_pairs % 24 == 0
    return pl.pallas_call(
        _dispatch_kernel,
        grid=(N // tn,),
        in_specs=_run_specs(runs) + [pl.BlockSpec((TOP_K, tn), lambda t: (0, t)),
                                     pl.BlockSpec((tn, D), lambda t: (t, 0))],
        out_specs=pl.BlockSpec(memory_space=pl.ANY),
        out_shape=jax.ShapeDtypeStruct((N // tn * n_pairs * ROW_TILE, W), I32),
        scratch_shapes=[pltpu.VMEM((2, n_pairs * ROW_TILE, W), I32), pltpu.SemaphoreType.DMA((2,))],
        compiler_params=_params("arbitrary"),
        name="dispatch",
    )(*runs, pos_t, h2)


def _experts_kernel(first_ref, size_ref, e_ref, new_ref, next_ref, wslot_ref, total_ref,
                    w1_hbm, w3_hbm, w2_hbm, xs_hbm, ys_hbm,
                    xbuf, ybuf, w1f, w3f, w2f, w1b, w3b, w2b, xsem, ysem, wsem):
    total = total_ref[0]

    def w_copies(e, slot):
        return [pltpu.make_async_copy(src.at[e], dst.at[slot], wsem.at[slot])
                for src, dst in ((w1_hbm, w1f), (w3_hbm, w3f), (w2_hbm, w2f))]

    def rows(c):
        return (pl.multiple_of(first_ref[c] * ROW_TILE, ROW_TILE), pl.multiple_of(size_ref[c] * ROW_TILE, ROW_TILE))

    def x_copy(c, slot):
        first, size = rows(c)
        return pltpu.make_async_copy(xs_hbm.at[pl.ds(first, size)], xbuf.at[slot, pl.ds(0, size)], xsem.at[slot])

    def y_copy(c, slot):
        first, size = rows(c)
        return pltpu.make_async_copy(ybuf.at[slot, pl.ds(0, size)], ys_hbm.at[pl.ds(first, size)], ysem.at[slot])

    xbuf[...] = jnp.zeros(xbuf.shape, I32)
    for cp in w_copies(e_ref[0], 0):
        cp.start()
    x_copy(0, 0).start()
    half = MOE_BLOCK // 2

    def body(c, carry):
        slot = c % 2

        @pl.when(new_ref[c] == 1)
        def _():
            ws = wslot_ref[c]
            for cp in w_copies(e_ref[c], ws):
                cp.wait()
            w1b[...] = w1f[ws].astype(BF16)
            w3b[...] = w3f[ws].astype(BF16)
            w2b[...] = w2f[ws].astype(BF16)

            @pl.when(next_ref[c] >= 0)
            def _():
                for cp in w_copies(next_ref[c], 1 - ws):
                    cp.start()

        x_copy(c, slot).wait()

        @pl.when(c + 1 < total)
        def _():
            x_copy(c + 1, 1 - slot).start()

        x = jnp.concatenate(_unpack_pairs(_from_row_tiles(xbuf.at[slot], half)), axis=0).astype(BF16)
        h1 = _dot(x, w1b[...])
        h3 = _dot(x, w3b[...])
        hid = (h1 * _sigmoid(h1)) * h3
        y = _dot(hid.astype(BF16), w2b[...]).astype(BF16).astype(F32)

        @pl.when(c >= 2)
        def _():
            y_copy(c - 2, slot).wait()

        _to_row_tiles(ybuf.at[slot], _pack_pairs(y[0:half], y[half:]))
        y_copy(c, slot).start()
        return carry

    lax.fori_loop(0, total, body, 0)

    @pl.when(total >= 2)
    def _():
        y_copy(total - 2, total % 2).wait()

    y_copy(total - 1, (total - 1) % 2).wait()

    ybuf[0] = jnp.zeros(ybuf.shape[1:], I32)
    tail_first = total_ref[1]
    tail_pairs = ys_hbm.shape[0] // ROW_TILE - tail_first

    def fill(j, carry):
        size = pl.multiple_of(jnp.minimum(half, tail_pairs - j * half) * ROW_TILE, ROW_TILE)
        cp = pltpu.make_async_copy(ybuf.at[0, pl.ds(0, size)],
                                   ys_hbm.at[pl.ds(pl.multiple_of((tail_first + j * half) * ROW_TILE, ROW_TILE), size)],
                                   ysem.at[0])
        cp.start()
        cp.wait()
        return carry

    lax.fori_loop(0, (tail_pairs + half - 1) // half, fill, 0)


def _expert_chunks(counts, n_pairs_total):
    E = counts.shape[0]
    block = MOE_BLOCK // 2
    n_max = n_pairs_total // block + E
    starts = jnp.cumsum(counts) - counts
    n_chunks = (counts + block - 1) // block
    c_end = jnp.cumsum(n_chunks)
    c = jnp.arange(n_max, dtype=I32)
    e = jnp.minimum(jnp.sum(c_end[None, :] <= c[:, None], axis=1), E - 1).astype(I32)
    j = c - (c_end - n_chunks)[e]
    size = jnp.clip(counts[e] - j * block, 0, block)
    has_rows = counts > 0
    ids = jnp.where(has_rows, jnp.arange(E, dtype=I32), E)
    later = jnp.flip(lax.cummin(jnp.flip(ids)))
    nxt = jnp.concatenate([later[1:], jnp.full((1,), E, I32)])
    nxt = jnp.where(nxt >= E, -1, nxt)
    ordinal = jnp.cumsum(has_rows.astype(I32)) - 1
    to_i32 = lambda a: a.astype(I32)
    meta = jnp.stack([c_end[-1], jnp.sum(counts)])
    return tuple(map(to_i32, (starts[e] + j * block, size, e, j == 0, nxt[e], ordinal[e] % 2, meta)))


def _experts(chunks, xs, w1, w3, w2):
    rows_total, W = xs.shape
    _, D, De = w1.shape
    any_space = pl.BlockSpec(memory_space=pl.ANY)
    block_rows = MOE_BLOCK // 2 * ROW_TILE
    grid_spec = pltpu.PrefetchScalarGridSpec(
        num_scalar_prefetch=len(chunks),
        grid=(1,),
        in_specs=[any_space] * 4,
        out_specs=any_space,
        scratch_shapes=[pltpu.VMEM((2, block_rows, W), I32), pltpu.VMEM((2, block_rows, W), I32),
                        pltpu.VMEM((2, D, De), F32), pltpu.VMEM((2, D, De), F32), pltpu.VMEM((2, De, D), F32),
                        pltpu.VMEM((D, De), BF16), pltpu.VMEM((D, De), BF16), pltpu.VMEM((De, D), BF16),
                        pltpu.SemaphoreType.DMA((2,)), pltpu.SemaphoreType.DMA((2,)), pltpu.SemaphoreType.DMA((2,))],
    )
    return pl.pallas_call(
        _experts_kernel,
        grid_spec=grid_spec,
        out_shape=jax.ShapeDtypeStruct((rows_total, W), I32),
        compiler_params=_params("arbitrary"),
        name="experts",
    )(*chunks, w1, w3, w2, xs)


def _combine_kernel(src_ref, cnt_ref, dst_ref, ys_hbm, post_ref, wt_ref, h2_ref, x1_ref, mod_ref, ws1_ref,
                    ws3_ref, ws2_ref, fw_ref, o_ref, ybuf, sem):
    tm = h2_ref.shape[0]
    n_pairs = ybuf.shape[0] // ROW_TILE
    n_exp = cnt_ref.shape[2] - 1

    @pl.when(pl.program_id(0) == 0)
    def _():
        ybuf[...] = jnp.zeros(ybuf.shape, I32)

    _run_copies(src_ref, cnt_ref, dst_ref, n_exp,
                lambda src, dst: pltpu.make_async_copy(ys_hbm.at[dst], ybuf.at[src], sem.at[0]))

    h = h2_ref[...]
    a = _dot(h, ws1_ref[...])
    hid = (a * _sigmoid(a)) * _dot(h, ws3_ref[...])
    shared = _dot(hid.astype(BF16), ws2_ref[...])

    pair = 2 * lax.broadcasted_iota(I32, (n_pairs, tm), 0)
    w_cols, mine = [], []
    for odd in range(2):
        w_rows = jnp.zeros((n_pairs, tm), F32)
        for k in range(TOP_K):
            w_rows = jnp.where(pair + odd == post_ref[k:k + 1, :], wt_ref[k:k + 1, :], w_rows)
        w_cols.append(jnp.sum(w_rows, axis=1, keepdims=True))
        mine.append(jnp.where(w_rows != 0.0, 1.0, 0.0).astype(BF16))
    used = pl.multiple_of((src_ref[0, 0, n_exp]) * ROW_TILE, ROW_TILE)
    pltpu.make_async_copy(ys_hbm.at[pl.ds(0, used)], ybuf.at[pl.ds(0, used)], sem.at[0]).wait()
    y_even, y_odd = _unpack_pairs(_from_row_tiles(ybuf, n_pairs))
    sum_rows = lambda hot, y: lax.dot_general(hot, y, (((0,), (0,)), ((), ())), preferred_element_type=F32)
    routed = (sum_rows(mine[0], (y_even * w_cols[0]).astype(BF16))
              + sum_rows(mine[1], (y_odd * w_cols[1]).astype(BF16)))
    x2 = x1_ref[...] + mod_ref[0, 5:6, :] * (routed + shared)
    o_ref[...] = _rms(x2) * fw_ref[...]


def _combine(runs, ys, pos_t, wgt_t, h2, x1, mod, ws1, ws3, ws2, fw, seq, tm):
    N, D = h2.shape
    nt = seq // tm
    row = pl.BlockSpec((tm, D), lambda t: (t, 0))
    full = lambda a: pl.BlockSpec(a.shape, lambda t: (0,) * a.ndim)
    return pl.pallas_call(
        _combine_kernel,
        grid=(N // tm,),
        in_specs=_run_specs(runs) + [pl.BlockSpec(memory_space=pl.ANY),
                                     pl.BlockSpec((TOP_K, tm), lambda t: (0, t)),
                                     pl.BlockSpec((TOP_K, tm), lambda t: (0, t)),
                                     row, row,
                                     pl.BlockSpec((1, 6, D), lambda t: (t // nt, 0, 0)),
                                     full(ws1), full(ws3), full(ws2), full(fw)],
        out_specs=row,
        out_shape=jax.ShapeDtypeStruct((N, D), F32),
        scratch_shapes=[pltpu.VMEM((_tile_pairs(tm, runs[0].shape[2] - 1) * ROW_TILE, D // ROW_TILE), I32),
                        pltpu.SemaphoreType.DMA((1,))],
        compiler_params=_params("arbitrary"),
        name="combine",
    )(*runs, ys, pos_t, wgt_t, h2, x1, mod, ws1, ws3, ws2, fw)


def _t5_bucket(dist):
    n = jnp.maximum(dist, 0)
    max_exact = REL_BUCKETS // 2
    nf = jnp.maximum(n, 1).astype(F32)
    large = max_exact + (jnp.log(nf / max_exact) / math.log(REL_MAX_DIST / max_exact)
                         * (REL_BUCKETS - max_exact)).astype(I32)
    large = jnp.minimum(large, REL_BUCKETS - 1)
    return jnp.where(n < max_exact, n, large)


def _bias_kernel(tbl_ref, far_ref, bt_ref, bw_ref, t_ref, w_ref):
    g = pl.program_id(0)
    d = pl.program_id(1)

    def lookup(bkt, head):
        out = jnp.full(bkt.shape, tbl_ref[0, head], F32)
        for b in range(1, REL_BUCKETS):
            out = jnp.where(bkt == b, tbl_ref[b, head], out)
        return out - far_ref[head]

    j = lax.broadcasted_iota(I32, (QB, QB), 0)
    i = lax.broadcasted_iota(I32, (QB, QB), 1)
    keep = ((d != 0) | (i >= j)) & (d != TILE_MASKED) & ((d != TILE_WINDOW_EDGE) | (j > i))
    for hq in range(GQA):
        t_ref[0, 0, :, hq * QB:(hq + 1) * QB] = jnp.where(keep, lookup(bt_ref[0], GQA * g + hq), NEG)

    @pl.when(d == 0)
    def _():
        for hq in range(GQA):
            w_ref[0, :, hq * QB:(hq + 1) * QB] = lookup(bw_ref[...], GQA * g + hq)


def _bias_tables(rel_table):
    tbl = rel_table.astype(F32) * LOG2E
    far_hi = tbl[REL_BUCKETS - 1].astype(BF16)
    far_lo = (tbl[REL_BUCKETS - 1] - far_hi.astype(F32)).astype(BF16)
    far = far_hi.astype(F32) + far_lo.astype(F32)
    q_tail = jnp.zeros((N_HEADS, HEAD_DIM), F32).at[:, 0].set(far_hi.astype(F32)).at[:, 1].set(far_lo.astype(F32))
    i = jnp.arange(QB)
    tile_dist = np.array(list(range(N_BIAS_TILES)) + [0, WINDOW // QB])
    d_t = (QB * jnp.asarray(tile_dist)[:, None, None] + i[None, None, :] - i[None, :, None])
    d_w = i[None, :] - CMP_STRIDE * (jnp.arange(CMP_WIN)[:, None] - CMP_PAD) - (CMP_LEN - 1)
    n_tiles = len(tile_dist)
    t_tab, w_tab = pl.pallas_call(
        _bias_kernel,
        grid=(N_KV, n_tiles),
        in_specs=[pl.BlockSpec(memory_space=pltpu.SMEM),
                  pl.BlockSpec(memory_space=pltpu.SMEM),
                  pl.BlockSpec((1, QB, QB), lambda g, d: (d, 0, 0)),
                  pl.BlockSpec((CMP_WIN, QB), lambda g, d: (0, 0))],
        out_specs=[pl.BlockSpec((1, 1, QB, GQA * QB), lambda g, d: (g, d, 0, 0)),
                   pl.BlockSpec((1, CMP_WIN, GQA * QB), lambda g, d: (g, 0, 0))],
        out_shape=[jax.ShapeDtypeStruct((N_KV, n_tiles, QB, GQA * QB), F32),
                   jax.ShapeDtypeStruct((N_KV, CMP_WIN, GQA * QB), F32)],
        compiler_params=_params("parallel", "arbitrary"),
        name="bias",
    )(tbl, far, _t5_bucket(d_t), _t5_bucket(d_w))
    far_min = min(FAR_TILE_DIST * QB - QB + 1, CMP_STRIDE * (CMP_PAD + 1) - (CMP_LEN - 1))
    max_exact = REL_BUCKETS // 2
    assert math.log(far_min / max_exact) / math.log(REL_MAX_DIST / max_exact) * max_exact > max_exact - 0.75
    return t_tab, w_tab, q_tail


def _overlap_tables(nc, nb):
    c_start = CMP_STRIDE * np.arange(nc)[:, None]
    s_start = SLC_LEN * np.arange(nb)[None, :]
    ov = ((c_start < s_start + SLC_LEN) & (c_start + CMP_LEN > s_start)).astype(np.float32)
    ov[nc - 1:] = 0.0
    out = np.zeros((nc + CMP_WIN, nb), np.float32)
    out[CMP_PAD:CMP_PAD + nc] = ov
    return jnp.asarray(ov.T, dtype=BF16), jnp.asarray(out)


def _block_diag(w):
    nblk, bs, _ = w.shape
    eye = jnp.eye(nblk, dtype=w.dtype)
    return (eye[:, None, :, None] * w[:, :, None, :]).reshape(nblk * bs, nblk * bs)


def _layer(x, c8, w_ada, b_ada, norm1_w, w_in, cmp_pos, cmp_k_w1, cmp_k_w2, cmp_v_w1, cmp_v_w2, rel_table,
           conv_w, conv_b, lru_wa, lru_ba, lru_wx, lru_bx, lru_lambda, gnorm_attn_w, gnorm_rnn_w, w_out,
           norm2_w, w_router, b_router, w1, w3, w2, ws1, ws3, ws2, final_norm_w):
    B, S, D = x.shape
    N = B * S
    E = w_router.shape[1]
    d_attn = N_HEADS * HEAD_DIM
    d_rnn = D - d_attn
    assert S % (FAR_CHUNK * QB) == 0 and S // SLC_LEN <= MAX_SLC_BLOCKS

    mod = _ada(c8, w_ada, b_ada[None, :])[:B].reshape(B, 6, D)

    n_kv_cols = 6 * N_KV * HEAD_DIM
    n_gate = N_BRANCH * N_HEADS
    wq, wkv, wg, wxr, wgr = jnp.split(w_in, np.cumsum([d_attn, n_kv_cols, n_gate, d_rnn]).tolist(), axis=1)
    wg = jnp.pad(wg.reshape(D, N_KV, GQA * N_BRANCH), ((0, 0), (0, 0), (0, LANES - GQA * N_BRANCH)))
    wg = wg.reshape(D, N_KV * LANES)
    w_pad = jnp.concatenate([wq, wkv, wg, wxr, wgr], axis=1).astype(BF16)
    t_tab, w_tab, q_tail = _bias_tables(rel_table)
    q_t, kx, vx_t, kvc, gates_t, xr, gr = _inproj(x, mod, norm1_w[None, :], w_pad, q_tail)

    nc = S // CMP_STRIDE
    pos2 = cmp_pos.reshape(2, CMP_STRIDE * HEAD_DIM)
    kvc_pad, kvc_t = _compress(kvc, pos2, jnp.stack([cmp_k_w1, cmp_v_w1]), jnp.stack([cmp_k_w2, cmp_v_w2]))
    o_attn = _attn(q_t, kx, vx_t, kvc_pad, kvc_t, gates_t, t_tab, w_tab, *_overlap_tables(nc, MAX_SLC_BLOCKS))

    y_rnn = _rglru(xr, gr, conv_w.reshape(CONV_WIDTH, d_rnn), conv_b[None, :], _block_diag(lru_wa).astype(BF16),
                   lru_ba[None, :], _block_diag(lru_wx).astype(BF16), lru_bx[None, :], lru_lambda[None, :],
                   gnorm_rnn_w[None, :])

    wr_hi = w_router.T.astype(BF16)
    wr_lo = (w_router.T - wr_hi.astype(F32)).astype(BF16)
    tile = min(MOE_TILE, S)
    x1, h2, wgt_t, pos_t, tile_counts = _outproj(o_attn, y_rnn, x, mod, gnorm_attn_w[None, :], w_out.astype(BF16),
                                                 norm2_w[None, :], jnp.stack([wr_hi, wr_lo]), b_router[:, None], tile)

    tcnt = tile_counts[:, :, 0].astype(I32)
    n_tiles = tcnt.shape[0]
    tile_pairs = _tile_pairs(tile, E)
    rpairs = (tcnt + 1) // 2
    rpairs = jnp.concatenate([rpairs, tile_pairs - jnp.sum(rpairs, axis=1, keepdims=True)], axis=1)
    per_expert = jnp.sum(rpairs, axis=0)
    off = jnp.cumsum(per_expert) - per_expert
    run_src = jnp.cumsum(rpairs, axis=1) - rpairs
    run_dst = off[None, :] + jnp.cumsum(rpairs, axis=0) - rpairs
    runs = tuple(r.astype(I32)[:, None, :] for r in (run_src, rpairs, run_dst))

    h2f = h2.reshape(N, D)
    xs = _dispatch(runs, pos_t, h2f, tile)
    ys = _experts(_expert_chunks(per_expert[:E], n_tiles * tile_pairs), xs, w1, w3, w2)
    out = _combine(runs, ys, pos_t, wgt_t, h2f, x1.reshape(N, D), mod, ws1.astype(BF16), ws3.astype(BF16),
                   ws2.astype(BF16), final_norm_w[None, :], S, tile)
    return out.reshape(B, S, D)


def kernel(x, c, w_ada, b_ada, norm1_w, w_in, cmp_pos, cmp_k_w1, cmp_k_w2, cmp_v_w1, cmp_v_w2, rel_table, conv_w, conv_b, lru_wa, lru_ba, lru_wx, lru_bx, lru_lambda, gnorm_attn_w, gnorm_rnn_w, w_out, norm2_w, w_router, b_router, w1, w3, w2, ws1, ws3, ws2, final_norm_w):
    assert w_ada.shape[0] == 1
    c8 = jnp.pad(c, ((0, 8 - c.shape[0]), (0, 0)))
    return _layer(x, c8, w_ada[0], b_ada[0], norm1_w[0], w_in[0], cmp_pos[0], cmp_k_w1[0], cmp_k_w2[0],
                  cmp_v_w1[0], cmp_v_w2[0], rel_table, conv_w[0], conv_b[0], lru_wa[0], lru_ba[0], lru_wx[0],
                  lru_bx[0], lru_lambda[0], gnorm_attn_w[0], gnorm_rnn_w[0], w_out[0], norm2_w[0], w_router[0],
                  b_router[0], w1[0], w3[0], w2[0], ws1[0], ws3[0], ws2[0], final_norm_w)
```
